```python
import jax, jax.numpy as jnp
from jax import lax
import numpy as np

D_MODEL = 1024
BATCH = 8
SEQ = 2048
DEPTH = 4

N_MIXERS = 3
MEM_LEN = 256
EPS = 1e-6
ROPE_THETA = 10000.0
MAX_POS_OFFSET = 4096

MLA_HEADS = 8
MLA_NOPE = 128
MLA_ROPE = 64
MLA_V = 128
MLA_Q_RANK = 384
MLA_KV_RANK = 256
Q_BLOCK = 128

GDN_HEADS = 8
GDN_DK = 128
GDN_DV = 128
GDN_CONV = 4
GDN_CHUNK = 64
GDN_QKV = GDN_HEADS * (2 * GDN_DK + GDN_DV)
GDN_PROJ = GDN_QKV + GDN_HEADS * GDN_DV + 2 * GDN_HEADS

SC_WIDTH = D_MODEL
SC_CONV = 3

X_HEADS = 4
X_HEAD_DIM = D_MODEL // X_HEADS

D_FF = 4 * D_MODEL

N_A = (DEPTH + 2) // N_MIXERS
N_B = (DEPTH + 1) // N_MIXERS
N_C = DEPTH // N_MIXERS

kernel_name = "hybrid_mla_gdn_shortconv_memxattn"


def rms_norm(x, g):
    xf = x.astype(jnp.float32)
    y = xf * lax.rsqrt(jnp.mean(xf * xf, axis=-1, keepdims=True) + EPS)
    return (y * g.astype(jnp.float32)).astype(x.dtype)


def rope_tables(positions):
    inv_freq = ROPE_THETA ** (-jnp.arange(0, MLA_ROPE, 2, dtype=jnp.float32) / MLA_ROPE)
    ang = positions.astype(jnp.float32)[..., None] * inv_freq
    return jnp.cos(ang), jnp.sin(ang)


def apply_rope(x, cos, sin):
    c = cos[:, :, None, :]
    s = sin[:, :, None, :]
    x1, x2 = jnp.split(x.astype(jnp.float32), 2, axis=-1)
    return jnp.concatenate([x1 * c - x2 * s, x2 * c + x1 * s], axis=-1).astype(x.dtype)


def causal_depthwise_conv(x, w):
    k, c = w.shape
    return lax.conv_general_dilated(
        x, w[:, None, :].astype(x.dtype), window_strides=(1,), padding=[(k - 1, 0)],
        dimension_numbers=("NWC", "WIO", "NWC"), feature_group_count=c)


def mla_mixer(h, cos, sin, w_in, q_norm, kv_norm, w_uq, w_ukv, w_o):
    b, s, _ = h.shape
    z = h @ w_in
    c_q, c_kv, k_rope = jnp.split(z, [MLA_Q_RANK, MLA_Q_RANK + MLA_KV_RANK], axis=-1)
    q = (rms_norm(c_q, q_norm) @ w_uq).reshape(b, s, MLA_HEADS, MLA_NOPE + MLA_ROPE)
    q_nope = q[..., :MLA_NOPE]
    q_rope = apply_rope(q[..., MLA_NOPE:], cos, sin)
    kv = (rms_norm(c_kv, kv_norm) @ w_ukv).reshape(b, s, MLA_HEADS, MLA_NOPE + MLA_V)
    k_nope, v = kv[..., :MLA_NOPE], kv[..., MLA_NOPE:]
    k_rope = apply_rope(k_rope[:, :, None, :], cos, sin)[:, :, 0, :]
    scale = (MLA_NOPE + MLA_ROPE) ** -0.5
    outs = []
    for start in range(0, s, Q_BLOCK):
        end = start + Q_BLOCK
        sc = (jnp.einsum("bqhd,bkhd->bhqk", q_nope[:, start:end], k_nope[:, :end])
              + jnp.einsum("bqhr,bkr->bhqk", q_rope[:, start:end], k_rope[:, :end]))
        sc = sc.astype(jnp.float32) * scale
        mask = (start + jnp.arange(Q_BLOCK))[:, None] >= jnp.arange(end)[None, :]
        sc = jnp.where(mask, sc, -jnp.inf)
        p = jax.nn.softmax(sc, axis=-1).astype(v.dtype)
        outs.append(jnp.einsum("bhqk,bkhd->bqhd", p, v[:, :end]))
    o = jnp.concatenate(outs, axis=1).reshape(b, s, MLA_HEADS * MLA_V)
    return o @ w_o


def chunk_gated_delta_rule(q, k, v, g, beta):
    b, s, h, dk = q.shape
    dv = v.shape[-1]
    c = GDN_CHUNK
    n = s // c

    def to_chunks(t):
        t = t.astype(jnp.float32).reshape((b, n, c, h) + t.shape[3:])
        return jnp.moveaxis(t, (1, 3), (0, 2))

    qc, kc, vc = to_chunks(q), to_chunks(k), to_chunks(v)
    gc = lax.cumsum(to_chunks(g), axis=3)
    bc = to_chunks(beta)
    tri = jnp.tril(jnp.ones((c, c), dtype=bool))
    strict = jnp.tril(jnp.ones((c, c), dtype=bool), -1)
    decay = jnp.exp(jnp.where(tri, gc[..., :, None] - gc[..., None, :], -jnp.inf))
    k_beta = kc * bc[..., None]
    m = jnp.where(strict, jnp.einsum("nbhid,nbhjd->nbhij", k_beta, kc) * decay, 0.0)
    eye = jnp.eye(c, dtype=jnp.float32)
    t_inv = lax.linalg.triangular_solve(eye + m, jnp.broadcast_to(eye, m.shape),
                                        left_side=True, lower=True, unit_diagonal=True)
    u = t_inv @ (vc * bc[..., None])
    w = t_inv @ (k_beta * jnp.exp(gc)[..., None])
    attn_intra = jnp.einsum("nbhid,nbhjd->nbhij", qc, kc) * decay

    def step(state, xs):
        q_i, k_i, u_i, w_i, g_i, a_i = xs
        v_new = u_i - w_i @ state
        o_i = (q_i * jnp.exp(g_i)[..., None]) @ state + a_i @ v_new
        g_last = g_i[..., -1:]
        state = (state * jnp.exp(g_last)[..., None]
                 + jnp.einsum("bhcd,bhce->bhde", k_i * jnp.exp(g_last - g_i)[..., None], v_new))
        return state, o_i

    s0 = jnp.zeros((b, h, dk, dv), jnp.float32)
    _, o = lax.scan(step, s0, (qc, kc, u, w, gc, attn_intra))
    return jnp.moveaxis(o, (0, 2), (1, 3)).reshape(b, s, h, dv)


def gdn_mixer(h, w_in, conv_w, a_log, dt_bias, o_norm, w_o):
    b, s, _ = h.shape
    z = h @ w_in
    qkv, gate, beta_logit, a_logit = jnp.split(
        z, [GDN_QKV, GDN_QKV + GDN_HEADS * GDN_DV, GDN_QKV + GDN_HEADS * GDN_DV + GDN_HEADS], axis=-1)
    qkv = jax.nn.silu(causal_depthwise_conv(qkv, conv_w))
    q, k, v = jnp.split(qkv, [GDN_HEADS * GDN_DK, 2 * GDN_HEADS * GDN_DK], axis=-1)
    q = q.reshape(b, s, GDN_HEADS, GDN_DK).astype(jnp.float32)
    k = k.reshape(b, s, GDN_HEADS, GDN_DK).astype(jnp.float32)
    v = v.reshape(b, s, GDN_HEADS, GDN_DV)
    q = q * lax.rsqrt(jnp.sum(q * q, -1, keepdims=True) + EPS) * (GDN_DK ** -0.5)
    k = k * lax.rsqrt(jnp.sum(k * k, -1, keepdims=True) + EPS)
    beta = jax.nn.sigmoid(beta_logit.astype(jnp.float32))
    g = -jnp.exp(a_log.astype(jnp.float32)) * jax.nn.softplus(
        a_logit.astype(jnp.float32) + dt_bias.astype(jnp.float32))
    o = chunk_gated_delta_rule(q, k, v, g, beta)
    o = rms_norm(o, o_norm) * jax.nn.silu(gate.reshape(b, s, GDN_HEADS, GDN_DV).astype(jnp.float32))
    return o.reshape(b, s, GDN_HEADS * GDN_DV).astype(h.dtype) @ w_o


def short_conv_mixer(h, w_in, conv_w, w_o):
    z = h @ w_in
    b_gate, c_gate, u = jnp.split(z, 3, axis=-1)
    y = b_gate * causal_depthwise_conv(c_gate * u, conv_w)
    return y @ w_o


def memory_cross_attention(h, mem_n, w_q, w_kv, w_o):
    b, s, _ = h.shape
    m = mem_n.shape[1]
    q = (h @ w_q).reshape(b, s, X_HEADS, X_HEAD_DIM)
    k, v = jnp.split(mem_n @ w_kv, 2, axis=-1)
    k = k.reshape(b, m, X_HEADS, X_HEAD_DIM)
    v = v.reshape(b, m, X_HEADS, X_HEAD_DIM)
    sc = jnp.einsum("bqhd,bkhd->bhqk", q, k).astype(jnp.float32) * (X_HEAD_DIM ** -0.5)
    p = jax.nn.softmax(sc, axis=-1).astype(v.dtype)
    o = jnp.einsum("bhqk,bkhd->bqhd", p, v).reshape(b, s, X_HEADS * X_HEAD_DIM)
    return o @ w_o


def relu2_mlp(h, w1, w2):
    return jnp.square(jax.nn.relu(h @ w1)) @ w2


def _fwd_setup_inputs(seed: int = 0) -> dict:
    key = jax.random.key(seed)
    ks = iter(jax.random.split(key, 40))

    def w(shape, fan_in):
        return jax.random.normal(next(ks), shape, jnp.float32) * (fan_in ** -0.5)

    def gain(shape):
        return 1.0 + 0.02 * jax.random.normal(next(ks), shape, jnp.float32)

    x = jax.random.normal(next(ks), (BATCH, SEQ, D_MODEL), jnp.float32)
    mem = jax.random.normal(next(ks), (BATCH, MEM_LEN, D_MODEL), jnp.float32)
    offsets = jax.random.randint(next(ks), (BATCH, 1), 0, MAX_POS_OFFSET, dtype=jnp.int32)
    positions = offsets + jnp.arange(SEQ, dtype=jnp.int32)[None, :]

    mla_w_in = w((N_A, D_MODEL, MLA_Q_RANK + MLA_KV_RANK + MLA_ROPE), D_MODEL)
    mla_q_norm = gain((N_A, MLA_Q_RANK))
    mla_kv_norm = gain((N_A, MLA_KV_RANK))
    mla_w_uq = w((N_A, MLA_Q_RANK, MLA_HEADS * (MLA_NOPE + MLA_ROPE)), MLA_Q_RANK)
    mla_w_ukv = w((N_A, MLA_KV_RANK, MLA_HEADS * (MLA_NOPE + MLA_V)), MLA_KV_RANK)
    mla_w_o = w((N_A, MLA_HEADS * MLA_V, D_MODEL), MLA_HEADS * MLA_V)

    gdn_w_in = w((N_B, D_MODEL, GDN_PROJ), D_MODEL)
    gdn_conv_w = w((N_B, GDN_CONV, GDN_QKV), GDN_CONV)
    gdn_a_log = jnp.log(jax.random.uniform(next(ks), (N_B, GDN_HEADS), jnp.float32, 1.0, 16.0))
    dt = jnp.exp(jax.random.uniform(next(ks), (N_B, GDN_HEADS), jnp.float32,
                                    float(np.log(1e-3)), float(np.log(1e-1))))
    gdn_dt_bias = dt + jnp.log(-jnp.expm1(-dt))
    gdn_o_norm = gain((N_B, GDN_DV))
    gdn_w_o = w((N_B, GDN_HEADS * GDN_DV, D_MODEL), GDN_HEADS * GDN_DV)

    sc_w_in = w((N_C, D_MODEL, 3 * SC_WIDTH), D_MODEL)
    sc_conv_w = w((N_C, SC_CONV, SC_WIDTH), SC_CONV)
    sc_w_o = w((N_C, SC_WIDTH, D_MODEL), SC_WIDTH)

    norm_mix = gain((DEPTH, D_MODEL))
    norm_mem = gain((DEPTH, D_MODEL))
    norm_mlp = gain((DEPTH, D_MODEL))
    xa_w_q = w((DEPTH, D_MODEL, X_HEADS * X_HEAD_DIM), D_MODEL)
    xa_w_kv = w((DEPTH, D_MODEL, 2 * X_HEADS * X_HEAD_DIM), D_MODEL)
    xa_w_o = w((DEPTH, X_HEADS * X_HEAD_DIM, D_MODEL), X_HEADS * X_HEAD_DIM)
    mlp_w1 = w((DEPTH, D_MODEL, D_FF), D_MODEL)
    mlp_w2 = w((DEPTH, D_FF, D_MODEL), D_FF)
    mem_norm = gain((D_MODEL,))
    final_norm = gain((D_MODEL,))

    return {
        "x": x, "mem": mem, "positions": positions,
        "mla_w_in": mla_w_in, "mla_q_norm": mla_q_norm, "mla_kv_norm": mla_kv_norm,
        "mla_w_uq": mla_w_uq, "mla_w_ukv": mla_w_ukv, "mla_w_o": mla_w_o,
        "gdn_w_in": gdn_w_in, "gdn_conv_w": gdn_conv_w, "gdn_a_log": gdn_a_log,
        "gdn_dt_bias": gdn_dt_bias, "gdn_o_norm": gdn_o_norm, "gdn_w_o": gdn_w_o,
        "sc_w_in": sc_w_in, "sc_conv_w": sc_conv_w, "sc_w_o": sc_w_o,
        "norm_mix": norm_mix, "norm_mem": norm_mem, "norm_mlp": norm_mlp,
        "xa_w_q": xa_w_q, "xa_w_kv": xa_w_kv, "xa_w_o": xa_w_o,
        "mlp_w1": mlp_w1, "mlp_w2": mlp_w2,
        "mem_norm": mem_norm, "final_norm": final_norm,
    }


def _fwd_reference(x, mem, positions,
              mla_w_in, mla_q_norm, mla_kv_norm, mla_w_uq, mla_w_ukv, mla_w_o,
              gdn_w_in, gdn_conv_w, gdn_a_log, gdn_dt_bias, gdn_o_norm, gdn_w_o,
              sc_w_in, sc_conv_w, sc_w_o,
              norm_mix, norm_mem, norm_mlp,
              xa_w_q, xa_w_kv, xa_w_o,
              mlp_w1, mlp_w2,
              mem_norm, final_norm):
    cos, sin = rope_tables(positions)
    mem_n = rms_norm(mem, mem_norm)
    for i in range(DEPTH):
        j = i // N_MIXERS
        kind = i % N_MIXERS
        h = rms_norm(x, norm_mix[i])
        if kind == 0:
            y = mla_mixer(h, cos, sin, mla_w_in[j], mla_q_norm[j], mla_kv_norm[j],
                          mla_w_uq[j], mla_w_ukv[j], mla_w_o[j])
        elif kind == 1:
            y = gdn_mixer(h, gdn_w_in[j], gdn_conv_w[j], gdn_a_log[j], gdn_dt_bias[j],
                          gdn_o_norm[j], gdn_w_o[j])
        else:
            y = short_conv_mixer(h, sc_w_in[j], sc_conv_w[j], sc_w_o[j])
        x = x + y
        x = x + memory_cross_attention(rms_norm(x, norm_mem[i]), mem_n,
                                       xa_w_q[i], xa_w_kv[i], xa_w_o[i])
        x = x + relu2_mlp(rms_norm(x, norm_mlp[i]), mlp_w1[i], mlp_w2[i])
    return rms_norm(x, final_norm)


import jax as _jax
import jax.numpy as _jnp

TWIN_FORMAT = 'train_step'
FWD_PARAMS = ['x', 'mem', 'positions', 'mla_w_in', 'mla_q_norm', 'mla_kv_norm', 'mla_w_uq', 'mla_w_ukv', 'mla_w_o', 'gdn_w_in', 'gdn_conv_w', 'gdn_a_log', 'gdn_dt_bias', 'gdn_o_norm', 'gdn_w_o', 'sc_w_in', 'sc_conv_w', 'sc_w_o', 'norm_mix', 'norm_mem', 'norm_mlp', 'xa_w_q', 'xa_w_kv', 'xa_w_o', 'mlp_w1', 'mlp_w2', 'mem_norm', 'final_norm']
TWIN_WEIGHTS = ['mla_w_in', 'mla_q_norm', 'mla_kv_norm', 'mla_w_uq', 'mla_w_ukv', 'mla_w_o', 'gdn_w_in', 'gdn_conv_w', 'gdn_a_log', 'gdn_dt_bias', 'gdn_o_norm', 'gdn_w_o', 'sc_w_in', 'sc_conv_w', 'sc_w_o', 'norm_mix', 'norm_mem', 'norm_mlp', 'xa_w_q', 'xa_w_kv', 'xa_w_o', 'mlp_w1', 'mlp_w2', 'mem_norm', 'final_norm']
TWIN_DIFF_INPUT = 'x'
TWIN_INPUTS = ['x', 'mem', 'positions', 'mla_w_in', 'mla_q_norm', 'mla_kv_norm', 'mla_w_uq', 'mla_w_ukv', 'mla_w_o', 'gdn_w_in', 'gdn_conv_w', 'gdn_a_log', 'gdn_dt_bias', 'gdn_o_norm', 'gdn_w_o', 'sc_w_in', 'sc_conv_w', 'sc_w_o', 'norm_mix', 'norm_mem', 'norm_mlp', 'xa_w_q', 'xa_w_kv', 'xa_w_o', 'mlp_w1', 'mlp_w2', 'mem_norm', 'final_norm', 'loss_target', 'm_mla_w_in', 'm_mla_q_norm', 'm_mla_kv_norm', 'm_mla_w_uq', 'm_mla_w_ukv', 'm_mla_w_o', 'm_gdn_w_in', 'm_gdn_conv_w', 'm_gdn_a_log', 'm_gdn_dt_bias', 'm_gdn_o_norm', 'm_gdn_w_o', 'm_sc_w_in', 'm_sc_conv_w', 'm_sc_w_o', 'm_norm_mix', 'm_norm_mem', 'm_norm_mlp', 'm_xa_w_q', 'm_xa_w_kv', 'm_xa_w_o', 'm_mlp_w1', 'm_mlp_w2', 'm_mem_norm', 'm_final_norm', 'v_mla_w_in', 'v_mla_q_norm', 'v_mla_kv_norm', 'v_mla_w_uq', 'v_mla_w_ukv', 'v_mla_w_o', 'v_gdn_w_in', 'v_gdn_conv_w', 'v_gdn_a_log', 'v_gdn_dt_bias', 'v_gdn_o_norm', 'v_gdn_w_o', 'v_sc_w_in', 'v_sc_conv_w', 'v_sc_w_o', 'v_norm_mix', 'v_norm_mem', 'v_norm_mlp', 'v_xa_w_q', 'v_xa_w_kv', 'v_xa_w_o', 'v_mlp_w1', 'v_mlp_w2', 'v_mem_norm', 'v_final_norm']
TWIN_OUTPUTS = ['loss', 'grad_x', 'grad_mla_w_in', 'grad_mla_q_norm', 'grad_mla_kv_norm', 'grad_mla_w_uq', 'grad_mla_w_ukv', 'grad_mla_w_o', 'grad_gdn_w_in', 'grad_gdn_conv_w', 'grad_gdn_a_log', 'grad_gdn_dt_bias', 'grad_gdn_o_norm', 'grad_gdn_w_o', 'grad_sc_w_in', 'grad_sc_conv_w', 'grad_sc_w_o', 'grad_norm_mix', 'grad_norm_mem', 'grad_norm_mlp', 'grad_xa_w_q', 'grad_xa_w_kv', 'grad_xa_w_o', 'grad_mlp_w1', 'grad_mlp_w2', 'grad_mem_norm', 'grad_final_norm', 'delta_mla_w_in', 'delta_mla_q_norm', 'delta_mla_kv_norm', 'delta_mla_w_uq', 'delta_mla_w_ukv', 'delta_mla_w_o', 'delta_gdn_w_in', 'delta_gdn_conv_w', 'delta_gdn_a_log', 'delta_gdn_dt_bias', 'delta_gdn_o_norm', 'delta_gdn_w_o', 'delta_sc_w_in', 'delta_sc_conv_w', 'delta_sc_w_o', 'delta_norm_mix', 'delta_norm_mem', 'delta_norm_mlp', 'delta_xa_w_q', 'delta_xa_w_kv', 'delta_xa_w_o', 'delta_mlp_w1', 'delta_mlp_w2', 'delta_mem_norm', 'delta_final_norm', 'new_m_mla_w_in', 'new_m_mla_q_norm', 'new_m_mla_kv_norm', 'new_m_mla_w_uq', 'new_m_mla_w_ukv', 'new_m_mla_w_o', 'new_m_gdn_w_in', 'new_m_gdn_conv_w', 'new_m_gdn_a_log', 'new_m_gdn_dt_bias', 'new_m_gdn_o_norm', 'new_m_gdn_w_o', 'new_m_sc_w_in', 'new_m_sc_conv_w', 'new_m_sc_w_o', 'new_m_norm_mix', 'new_m_norm_mem', 'new_m_norm_mlp', 'new_m_xa_w_q', 'new_m_xa_w_kv', 'new_m_xa_w_o', 'new_m_mlp_w1', 'new_m_mlp_w2', 'new_m_mem_norm', 'new_m_final_norm', 'new_v_mla_w_in', 'new_v_mla_q_norm', 'new_v_mla_kv_norm', 'new_v_mla_w_uq', 'new_v_mla_w_ukv', 'new_v_mla_w_o', 'new_v_gdn_w_in', 'new_v_gdn_conv_w', 'new_v_gdn_a_log', 'new_v_gdn_dt_bias', 'new_v_gdn_o_norm', 'new_v_gdn_w_o', 'new_v_sc_w_in', 'new_v_sc_conv_w', 'new_v_sc_w_o', 'new_v_norm_mix', 'new_v_norm_mem', 'new_v_norm_mlp', 'new_v_xa_w_q', 'new_v_xa_w_kv', 'new_v_xa_w_o', 'new_v_mlp_w1', 'new_v_mlp_w2', 'new_v_mem_norm', 'new_v_final_norm']
TWIN_LEAF_KINDS = {'loss': 'loss', 'grad_x': 'grad_x', 'grad_mla_w_in': 'grad_w', 'grad_mla_q_norm': 'grad_w', 'grad_mla_kv_norm': 'grad_w', 'grad_mla_w_uq': 'grad_w', 'grad_mla_w_ukv': 'grad_w', 'grad_mla_w_o': 'grad_w', 'grad_gdn_w_in': 'grad_w', 'grad_gdn_conv_w': 'grad_w', 'grad_gdn_a_log': 'grad_w', 'grad_gdn_dt_bias': 'grad_w', 'grad_gdn_o_norm': 'grad_w', 'grad_gdn_w_o': 'grad_w', 'grad_sc_w_in': 'grad_w', 'grad_sc_conv_w': 'grad_w', 'grad_sc_w_o': 'grad_w', 'grad_norm_mix': 'grad_w', 'grad_norm_mem': 'grad_w', 'grad_norm_mlp': 'grad_w', 'grad_xa_w_q': 'grad_w', 'grad_xa_w_kv': 'grad_w', 'grad_xa_w_o': 'grad_w', 'grad_mlp_w1': 'grad_w', 'grad_mlp_w2': 'grad_w', 'grad_mem_norm': 'grad_w', 'grad_final_norm': 'grad_w', 'delta_mla_w_in': 'delta_w', 'delta_mla_q_norm': 'delta_w', 'delta_mla_kv_norm': 'delta_w', 'delta_mla_w_uq': 'delta_w', 'delta_mla_w_ukv': 'delta_w', 'delta_mla_w_o': 'delta_w', 'delta_gdn_w_in': 'delta_w', 'delta_gdn_conv_w': 'delta_w', 'delta_gdn_a_log': 'delta_w', 'delta_gdn_dt_bias': 'delta_w', 'delta_gdn_o_norm': 'delta_w', 'delta_gdn_w_o': 'delta_w', 'delta_sc_w_in': 'delta_w', 'delta_sc_conv_w': 'delta_w', 'delta_sc_w_o': 'delta_w', 'delta_norm_mix': 'delta_w', 'delta_norm_mem': 'delta_w', 'delta_norm_mlp': 'delta_w', 'delta_xa_w_q': 'delta_w', 'delta_xa_w_kv': 'delta_w', 'delta_xa_w_o': 'delta_w', 'delta_mlp_w1': 'delta_w', 'delta_mlp_w2': 'delta_w', 'delta_mem_norm': 'delta_w', 'delta_final_norm': 'delta_w', 'new_m_mla_w_in': 'new_m', 'new_m_mla_q_norm': 'new_m', 'new_m_mla_kv_norm': 'new_m', 'new_m_mla_w_uq': 'new_m', 'new_m_mla_w_ukv': 'new_m', 'new_m_mla_w_o': 'new_m', 'new_m_gdn_w_in': 'new_m', 'new_m_gdn_conv_w': 'new_m', 'new_m_gdn_a_log': 'new_m', 'new_m_gdn_dt_bias': 'new_m', 'new_m_gdn_o_norm': 'new_m', 'new_m_gdn_w_o': 'new_m', 'new_m_sc_w_in': 'new_m', 'new_m_sc_conv_w': 'new_m', 'new_m_sc_w_o': 'new_m', 'new_m_norm_mix': 'new_m', 'new_m_norm_mem': 'new_m', 'new_m_norm_mlp': 'new_m', 'new_m_xa_w_q': 'new_m', 'new_m_xa_w_kv': 'new_m', 'new_m_xa_w_o': 'new_m', 'new_m_mlp_w1': 'new_m', 'new_m_mlp_w2': 'new_m', 'new_m_mem_norm': 'new_m', 'new_m_final_norm': 'new_m', 'new_v_mla_w_in': 'new_v', 'new_v_mla_q_norm': 'new_v', 'new_v_mla_kv_norm': 'new_v', 'new_v_mla_w_uq': 'new_v', 'new_v_mla_w_ukv': 'new_v', 'new_v_mla_w_o': 'new_v', 'new_v_gdn_w_in': 'new_v', 'new_v_gdn_conv_w': 'new_v', 'new_v_gdn_a_log': 'new_v', 'new_v_gdn_dt_bias': 'new_v', 'new_v_gdn_o_norm': 'new_v', 'new_v_gdn_w_o': 'new_v', 'new_v_sc_w_in': 'new_v', 'new_v_sc_conv_w': 'new_v', 'new_v_sc_w_o': 'new_v', 'new_v_norm_mix': 'new_v', 'new_v_norm_mem': 'new_v', 'new_v_norm_mlp': 'new_v', 'new_v_xa_w_q': 'new_v', 'new_v_xa_w_kv': 'new_v', 'new_v_xa_w_o': 'new_v', 'new_v_mlp_w1': 'new_v', 'new_v_mlp_w2': 'new_v', 'new_v_mem_norm': 'new_v', 'new_v_final_norm': 'new_v'}


def _forward(args):
    return _fwd_reference(*[args[k] for k in FWD_PARAMS])


def _output_shape():
    out = _jax.eval_shape(lambda: _forward(_fwd_setup_inputs(0)))
    return out.shape, out.dtype

N_MICROBATCH = 1
ADAM_LR = 0.001
ADAM_B1 = 0.9
ADAM_B2 = 0.999
ADAM_EPS = 1e-08
ADAM_WD = 0.01
ADAM_STEP = 10
PER_EXAMPLE_BATCH_AXIS = {'x': 0, 'mem': 0, 'positions': 0, 'loss_target': 0}
SHARED_INPUTS = []
_WEIGHT_DTYPES = {'mla_w_in': _jnp.float32, 'mla_q_norm': _jnp.float32, 'mla_kv_norm': _jnp.float32, 'mla_w_uq': _jnp.float32, 'mla_w_ukv': _jnp.float32, 'mla_w_o': _jnp.float32, 'gdn_w_in': _jnp.float32, 'gdn_conv_w': _jnp.float32, 'gdn_a_log': _jnp.float32, 'gdn_dt_bias': _jnp.float32, 'gdn_o_norm': _jnp.float32, 'gdn_w_o': _jnp.float32, 'sc_w_in': _jnp.float32, 'sc_conv_w': _jnp.float32, 'sc_w_o': _jnp.float32, 'norm_mix': _jnp.float32, 'norm_mem': _jnp.float32, 'norm_mlp': _jnp.float32, 'xa_w_q': _jnp.float32, 'xa_w_kv': _jnp.float32, 'xa_w_o': _jnp.float32, 'mlp_w1': _jnp.float32, 'mlp_w2': _jnp.float32, 'mem_norm': _jnp.float32, 'final_norm': _jnp.float32}
MOMENT_SCALE = {'mla_w_in': 5.702422e-02, 'mla_q_norm': 4.144706e-02, 'mla_kv_norm': 8.102973e-02, 'mla_w_uq': 2.026663e-02, 'mla_w_ukv': 2.771107e-02, 'mla_w_o': 3.338840e-02, 'gdn_w_in': 4.824624e-02, 'gdn_conv_w': 4.478358e-02, 'gdn_a_log': 2.538836e-01, 'gdn_dt_bias': 2.505917e-01, 'gdn_o_norm': 1.628294e-01, 'gdn_w_o': 5.772921e-02, 'sc_w_in': 5.621663e-02, 'sc_conv_w': 5.875355e-02, 'sc_w_o': 5.643659e-02, 'norm_mix': 7.512202e-02, 'norm_mem': 1.202192e-02, 'norm_mlp': 1.092732e-01, 'xa_w_q': 1.200517e-02, 'xa_w_kv': 1.219391e-02, 'xa_w_o': 1.238247e-02, 'mlp_w1': 5.365935e-02, 'mlp_w2': 9.940869e-02, 'mem_norm': 3.608285e-02, 'final_norm': 1.648009e+01}


def _to_microbatches(a, axis):
    t = _jnp.moveaxis(a, axis, 0)
    t = t.reshape((N_MICROBATCH, t.shape[0] // N_MICROBATCH) + t.shape[1:])
    return _jnp.moveaxis(t, 1, axis + 1)


def setup_inputs(seed: int = 0) -> dict:
    inp = _fwd_setup_inputs(seed)
    key = _jax.random.fold_in(_jax.random.key(seed), 7919)
    shape, _ = _output_shape()
    out = dict(inp)
    out["loss_target"] = _jax.random.normal(_jax.random.fold_in(key, 0), shape, _jnp.float32)
    for i, name in enumerate(TWIN_WEIGHTS):
        w = inp[name].astype(_jnp.float32)
        if MOMENT_SCALE is None:
            s = _jnp.sqrt(_jnp.mean(_jnp.square(w)) + 1e-30)
        else:
            s = MOMENT_SCALE[name]
        km, kv = _jax.random.split(_jax.random.fold_in(key, i + 1))
        out[name] = w
        out["m_" + name] = s * _jax.random.normal(km, w.shape, _jnp.float32)
        out["v_" + name] = (s * s) * _jax.random.uniform(kv, w.shape, _jnp.float32, 0.5, 1.5)
    if N_MICROBATCH > 1:
        for name, axis in PER_EXAMPLE_BATCH_AXIS.items():
            out[name] = _to_microbatches(out[name], axis)
    return {'x': out['x'], 'mem': out['mem'], 'positions': out['positions'], 'mla_w_in': out['mla_w_in'], 'mla_q_norm': out['mla_q_norm'], 'mla_kv_norm': out['mla_kv_norm'], 'mla_w_uq': out['mla_w_uq'], 'mla_w_ukv': out['mla_w_ukv'], 'mla_w_o': out['mla_w_o'], 'gdn_w_in': out['gdn_w_in'], 'gdn_conv_w': out['gdn_conv_w'], 'gdn_a_log': out['gdn_a_log'], 'gdn_dt_bias': out['gdn_dt_bias'], 'gdn_o_norm': out['gdn_o_norm'], 'gdn_w_o': out['gdn_w_o'], 'sc_w_in': out['sc_w_in'], 'sc_conv_w': out['sc_conv_w'], 'sc_w_o': out['sc_w_o'], 'norm_mix': out['norm_mix'], 'norm_mem': out['norm_mem'], 'norm_mlp': out['norm_mlp'], 'xa_w_q': out['xa_w_q'], 'xa_w_kv': out['xa_w_kv'], 'xa_w_o': out['xa_w_o'], 'mlp_w1': out['mlp_w1'], 'mlp_w2': out['mlp_w2'], 'mem_norm': out['mem_norm'], 'final_norm': out['final_norm'], 'loss_target': out['loss_target'], 'm_mla_w_in': out['m_mla_w_in'], 'm_mla_q_norm': out['m_mla_q_norm'], 'm_mla_kv_norm': out['m_mla_kv_norm'], 'm_mla_w_uq': out['m_mla_w_uq'], 'm_mla_w_ukv': out['m_mla_w_ukv'], 'm_mla_w_o': out['m_mla_w_o'], 'm_gdn_w_in': out['m_gdn_w_in'], 'm_gdn_conv_w': out['m_gdn_conv_w'], 'm_gdn_a_log': out['m_gdn_a_log'], 'm_gdn_dt_bias': out['m_gdn_dt_bias'], 'm_gdn_o_norm': out['m_gdn_o_norm'], 'm_gdn_w_o': out['m_gdn_w_o'], 'm_sc_w_in': out['m_sc_w_in'], 'm_sc_conv_w': out['m_sc_conv_w'], 'm_sc_w_o': out['m_sc_w_o'], 'm_norm_mix': out['m_norm_mix'], 'm_norm_mem': out['m_norm_mem'], 'm_norm_mlp': out['m_norm_mlp'], 'm_xa_w_q': out['m_xa_w_q'], 'm_xa_w_kv': out['m_xa_w_kv'], 'm_xa_w_o': out['m_xa_w_o'], 'm_mlp_w1': out['m_mlp_w1'], 'm_mlp_w2': out['m_mlp_w2'], 'm_mem_norm': out['m_mem_norm'], 'm_final_norm': out['m_final_norm'], 'v_mla_w_in': out['v_mla_w_in'], 'v_mla_q_norm': out['v_mla_q_norm'], 'v_mla_kv_norm': out['v_mla_kv_norm'], 'v_mla_w_uq': out['v_mla_w_uq'], 'v_mla_w_ukv': out['v_mla_w_ukv'], 'v_mla_w_o': out['v_mla_w_o'], 'v_gdn_w_in': out['v_gdn_w_in'], 'v_gdn_conv_w': out['v_gdn_conv_w'], 'v_gdn_a_log': out['v_gdn_a_log'], 'v_gdn_dt_bias': out['v_gdn_dt_bias'], 'v_gdn_o_norm': out['v_gdn_o_norm'], 'v_gdn_w_o': out['v_gdn_w_o'], 'v_sc_w_in': out['v_sc_w_in'], 'v_sc_conv_w': out['v_sc_conv_w'], 'v_sc_w_o': out['v_sc_w_o'], 'v_norm_mix': out['v_norm_mix'], 'v_norm_mem': out['v_norm_mem'], 'v_norm_mlp': out['v_norm_mlp'], 'v_xa_w_q': out['v_xa_w_q'], 'v_xa_w_kv': out['v_xa_w_kv'], 'v_xa_w_o': out['v_xa_w_o'], 'v_mlp_w1': out['v_mlp_w1'], 'v_mlp_w2': out['v_mlp_w2'], 'v_mem_norm': out['v_mem_norm'], 'v_final_norm': out['v_final_norm']}


def _loss(weights, diff, rest, loss_target):
    with _jax.named_scope("forward"):
        args = {**rest, TWIN_DIFF_INPUT: diff, **{k: w.astype(_WEIGHT_DTYPES[k]) for k, w in weights.items()}}
        y = _forward(args)
    with _jax.named_scope("loss_head"):
        err = _jnp.square(y.astype(_jnp.float32) - loss_target)
        return 0.5 * _jnp.sum(_jnp.mean(err, axis=-1)) if err.ndim else 0.5 * err


def _adamw(w, g, m, v):
    m = ADAM_B1 * m + (1.0 - ADAM_B1) * g
    v = ADAM_B2 * v + (1.0 - ADAM_B2) * _jnp.square(g)
    m_hat = m / (1.0 - ADAM_B1 ** ADAM_STEP)
    v_hat = v / (1.0 - ADAM_B2 ** ADAM_STEP)
    delta = -ADAM_LR * (m_hat / (_jnp.sqrt(v_hat) + ADAM_EPS) + ADAM_WD * w)
    return delta, m, v


def reference(x, mem, positions, mla_w_in, mla_q_norm, mla_kv_norm, mla_w_uq, mla_w_ukv, mla_w_o, gdn_w_in, gdn_conv_w, gdn_a_log, gdn_dt_bias, gdn_o_norm, gdn_w_o, sc_w_in, sc_conv_w, sc_w_o, norm_mix, norm_mem, norm_mlp, xa_w_q, xa_w_kv, xa_w_o, mlp_w1, mlp_w2, mem_norm, final_norm, loss_target, m_mla_w_in, m_mla_q_norm, m_mla_kv_norm, m_mla_w_uq, m_mla_w_ukv, m_mla_w_o, m_gdn_w_in, m_gdn_conv_w, m_gdn_a_log, m_gdn_dt_bias, m_gdn_o_norm, m_gdn_w_o, m_sc_w_in, m_sc_conv_w, m_sc_w_o, m_norm_mix, m_norm_mem, m_norm_mlp, m_xa_w_q, m_xa_w_kv, m_xa_w_o, m_mlp_w1, m_mlp_w2, m_mem_norm, m_final_norm, v_mla_w_in, v_mla_q_norm, v_mla_kv_norm, v_mla_w_uq, v_mla_w_ukv, v_mla_w_o, v_gdn_w_in, v_gdn_conv_w, v_gdn_a_log, v_gdn_dt_bias, v_gdn_o_norm, v_gdn_w_o, v_sc_w_in, v_sc_conv_w, v_sc_w_o, v_norm_mix, v_norm_mem, v_norm_mlp, v_xa_w_q, v_xa_w_kv, v_xa_w_o, v_mlp_w1, v_mlp_w2, v_mem_norm, v_final_norm):
    given = dict(x=x, mem=mem, positions=positions, mla_w_in=mla_w_in, mla_q_norm=mla_q_norm, mla_kv_norm=mla_kv_norm, mla_w_uq=mla_w_uq, mla_w_ukv=mla_w_ukv, mla_w_o=mla_w_o, gdn_w_in=gdn_w_in, gdn_conv_w=gdn_conv_w, gdn_a_log=gdn_a_log, gdn_dt_bias=gdn_dt_bias, gdn_o_norm=gdn_o_norm, gdn_w_o=gdn_w_o, sc_w_in=sc_w_in, sc_conv_w=sc_conv_w, sc_w_o=sc_w_o, norm_mix=norm_mix, norm_mem=norm_mem, norm_mlp=norm_mlp, xa_w_q=xa_w_q, xa_w_kv=xa_w_kv, xa_w_o=xa_w_o, mlp_w1=mlp_w1, mlp_w2=mlp_w2, mem_norm=mem_norm, final_norm=final_norm, loss_target=loss_target, m_mla_w_in=m_mla_w_in, m_mla_q_norm=m_mla_q_norm, m_mla_kv_norm=m_mla_kv_norm, m_mla_w_uq=m_mla_w_uq, m_mla_w_ukv=m_mla_w_ukv, m_mla_w_o=m_mla_w_o, m_gdn_w_in=m_gdn_w_in, m_gdn_conv_w=m_gdn_conv_w, m_gdn_a_log=m_gdn_a_log, m_gdn_dt_bias=m_gdn_dt_bias, m_gdn_o_norm=m_gdn_o_norm, m_gdn_w_o=m_gdn_w_o, m_sc_w_in=m_sc_w_in, m_sc_conv_w=m_sc_conv_w, m_sc_w_o=m_sc_w_o, m_norm_mix=m_norm_mix, m_norm_mem=m_norm_mem, m_norm_mlp=m_norm_mlp, m_xa_w_q=m_xa_w_q, m_xa_w_kv=m_xa_w_kv, m_xa_w_o=m_xa_w_o, m_mlp_w1=m_mlp_w1, m_mlp_w2=m_mlp_w2, m_mem_norm=m_mem_norm, m_final_norm=m_final_norm, v_mla_w_in=v_mla_w_in, v_mla_q_norm=v_mla_q_norm, v_mla_kv_norm=v_mla_kv_norm, v_mla_w_uq=v_mla_w_uq, v_mla_w_ukv=v_mla_w_ukv, v_mla_w_o=v_mla_w_o, v_gdn_w_in=v_gdn_w_in, v_gdn_conv_w=v_gdn_conv_w, v_gdn_a_log=v_gdn_a_log, v_gdn_dt_bias=v_gdn_dt_bias, v_gdn_o_norm=v_gdn_o_norm, v_gdn_w_o=v_gdn_w_o, v_sc_w_in=v_sc_w_in, v_sc_conv_w=v_sc_conv_w, v_sc_w_o=v_sc_w_o, v_norm_mix=v_norm_mix, v_norm_mem=v_norm_mem, v_norm_mlp=v_norm_mlp, v_xa_w_q=v_xa_w_q, v_xa_w_kv=v_xa_w_kv, v_xa_w_o=v_xa_w_o, v_mlp_w1=v_mlp_w1, v_mlp_w2=v_mlp_w2, v_mem_norm=v_mem_norm, v_final_norm=v_final_norm)
    weights = {n: given[n] for n in TWIN_WEIGHTS}
    shared = {n: given[n] for n in SHARED_INPUTS}
    per_example = {n: given[n] for n in ['x', 'mem', 'positions']}
    grad_fn = _jax.value_and_grad(_loss, argnums=(0, 1))

    def one_microbatch(ex, loss_target):
        ex = dict(ex)
        diff = ex.pop(TWIN_DIFF_INPUT)
        return grad_fn(weights, diff, {**shared, **ex}, loss_target)

    if N_MICROBATCH == 1:
        loss, (grad_w, grad_x) = one_microbatch(per_example, given["loss_target"])
    else:
        def body(carry, xs):
            loss_sum, grad_sum = carry
            l_k, (gw_k, gx_k) = one_microbatch(xs[0], xs[1])
            with _jax.named_scope("update"):
                return (loss_sum + l_k, _jax.tree.map(_jnp.add, grad_sum, gw_k)), gx_k

        init = (_jnp.zeros((), _jnp.float32), _jax.tree.map(_jnp.zeros_like, weights))
        (loss, grad_w), grad_x = _jax.lax.scan(body, init, (per_example, given["loss_target"]))
    with _jax.named_scope("update"):
        delta_w, new_m, new_v = {}, {}, {}
        for n in TWIN_WEIGHTS:
            delta_w[n], new_m[n], new_v[n] = _adamw(weights[n], grad_w[n], given["m_" + n], given["v_" + n])
    return (loss, grad_x, *[grad_w[n] for n in TWIN_WEIGHTS], *[delta_w[n] for n in TWIN_WEIGHTS],
            *[new_m[n] for n in TWIN_WEIGHTS], *[new_v[n] for n in TWIN_WEIGHTS])
```

```python
import functools
import math

import jax
import jax.numpy as jnp
from jax import lax
from jax.experimental import pallas as pl
from jax.experimental.pallas import tpu as pltpu

F32 = jnp.float32
BF16 = jnp.bfloat16

N_DEV = 8
LANES = 128
EPS = 1e-6
ROPE_THETA = 10000.0
MLA_HEADS, MLA_NOPE, MLA_ROPE, MLA_V = 8, 128, 64, 128
MLA_Q_RANK, MLA_KV_RANK = 384, 256
GDN_HEADS, GDN_DK, GDN_CONV, GDN_CHUNK = 8, 128, 4, 64
X_HEADS, X_HEAD_DIM = 4, 256
DEPTH, N_MIXERS = 4, 3
ADAM_LR, ADAM_B1, ADAM_B2, ADAM_EPS, ADAM_WD, ADAM_STEP = 0.001, 0.9, 0.999, 1e-08, 0.01, 10
NEG_BIG = -1e30
PACK_W = 1024


_NN = ((1,), (0,))
_NT = ((1,), (1,))
_TN = ((0,), (0,))


def _dg(a, b, dims, hi):
    if hi:
        return lax.dot_general(a, b, (dims, ((), ())), precision=lax.Precision.HIGHEST,
                               preferred_element_type=F32)
    return lax.dot_general(a.astype(BF16), b.astype(BF16), (dims, ((), ())), preferred_element_type=F32)


class _Ops:
    def __init__(self, hi, differentiable):
        def nn(a, b):
            return _dg(a, b, _NN, hi)

        def nt(a, b):
            return _dg(a, b, _NT, hi)

        def tn(a, b):
            return _dg(a, b, _TN, hi)

        if differentiable:
            dnn = jax.custom_vjp(nn)
            dnn.defvjp(lambda a, b: (nn(a, b), (a, b)), lambda r, g: (nt(g, r[1]), tn(r[0], g)))
            dnt = jax.custom_vjp(nt)
            dnt.defvjp(lambda a, b: (nt(a, b), (a, b)), lambda r, g: (nn(g, r[1]), tn(g, r[0])))
            dtn = jax.custom_vjp(tn)
            dtn.defvjp(lambda a, b: (tn(a, b), (a, b)), lambda r, g: (nt(r[1], g), nn(r[0], g)))
            nn, nt, tn = dnn, dnt, dtn
        self.nn, self.nt, self.tn = nn, nt, tn


class _OpSet:
    def __init__(self, differentiable):
        self.b = _Ops(False, differentiable)
        self.h = _Ops(True, differentiable)


_PLAIN = _OpSet(False)
_DIFF = _OpSet(True)


def _params(sem):
    return pltpu.CompilerParams(dimension_semantics=sem)


def _pick(n, cands):
    for c in cands:
        if n % c == 0:
            return c
    return n


def matmul(a, b, form, out_dtype, name, res=None):
    if form == "nn":
        (m, k), (k2, n) = a.shape, b.shape
    elif form == "nt":
        (m, k), (n, k2) = a.shape, b.shape
    else:
        (k, m), (k2, n) = a.shape, b.shape
    assert k == k2, (a.shape, b.shape, form)
    tm = m if m <= 512 else _pick(m, (512, 384, 256, 128))
    tn = n if n <= 1024 else _pick(n, (1024, 768, 512, 384, 256, 128))
    tk = k if k <= 1024 else _pick(k, (1024, 768, 512, 384, 256, 128))
    nk = k // tk
    dims = {"nn": _NN, "nt": _NT, "tn": _TN}[form]

    a_spec = {"nn": pl.BlockSpec((tm, tk), lambda i, j, kk: (i, kk)),
              "nt": pl.BlockSpec((tm, tk), lambda i, j, kk: (i, kk)),
              "tn": pl.BlockSpec((tk, tm), lambda i, j, kk: (kk, i))}[form]
    b_spec = {"nn": pl.BlockSpec((tk, tn), lambda i, j, kk: (kk, j)),
              "nt": pl.BlockSpec((tn, tk), lambda i, j, kk: (j, kk)),
              "tn": pl.BlockSpec((tk, tn), lambda i, j, kk: (kk, j))}[form]
    c_spec = pl.BlockSpec((tm, tn), lambda i, j, kk: (i, j))
    has_res = res is not None

    def body(*refs):
        a_ref, b_ref = refs[0], refs[1]
        r_ref = refs[2] if has_res else None
        o_ref = refs[3] if has_res else refs[2]
        part = _dg(a_ref[...], b_ref[...], dims, False)

        def finish(acc):
            if has_res:
                acc = acc + r_ref[...].astype(F32)
            o_ref[...] = acc.astype(out_dtype)

        if nk == 1:
            finish(part)
        else:
            acc_ref = refs[-1]
            kk = pl.program_id(2)

            @pl.when(kk == 0)
            def _():
                acc_ref[...] = part

            @pl.when(jnp.logical_and(kk > 0, kk < nk - 1))
            def _():
                acc_ref[...] += part

            @pl.when(kk == nk - 1)
            def _():
                finish(acc_ref[...] + part)

    in_specs = [a_spec, b_spec] + ([c_spec] if has_res else [])
    args = (a, b) + ((res,) if has_res else ())
    return pl.pallas_call(
        body, name=name,
        out_shape=jax.ShapeDtypeStruct((m, n), out_dtype),
        grid=(m // tm, n // tn, nk),
        in_specs=in_specs, out_specs=c_spec,
        scratch_shapes=[pltpu.VMEM((tm, tn), F32)] if nk > 1 else [],
        compiler_params=_params(("parallel", "parallel", "arbitrary")),
    )(*args)


def make_mm(name, out_dtype, with_res=False):
    if with_res:
        @jax.custom_vjp
        def op(res, a, w):
            return matmul(a, w, "nn", out_dtype, name + "_f", res=res)

        def fwd(res, a, w):
            return op(res, a, w), (a, w)

        def bwd(saved, g):
            a, w = saved
            da = matmul(g, w, "nt", a.dtype, name + "_da")
            dw = matmul(a, g, "tn", w.dtype, name + "_dw")
            return g, da, dw
    else:
        @jax.custom_vjp
        def op(a, w):
            return matmul(a, w, "nn", out_dtype, name + "_f")

        def fwd(a, w):
            return op(a, w), (a, w)

        def bwd(saved, g):
            a, w = saved
            da = matmul(g, w, "nt", a.dtype, name + "_da")
            dw = matmul(a, g, "tn", w.dtype, name + "_dw")
            return da, dw
    op.defvjp(fwd, bwd)
    return op


def _kind(k):
    return (k, None) if isinstance(k, str) else k


def _tile_spec(kind, shape, tm, heads):
    k, d = _kind(kind)
    if k == "row":
        return pl.BlockSpec((tm, shape[1]), (lambda h, i: (i, 0)) if heads else (lambda i: (i, 0)))
    if k == "par":
        return pl.BlockSpec(tuple(shape), (lambda h, i: (0, 0)) if heads else (lambda i: (0, 0)))
    if k == "rowh":
        return pl.BlockSpec((tm, d), lambda h, i: (i, h))
    if k == "parh":
        return pl.BlockSpec((shape[0], d), lambda h, i: (0, h))
    raise ValueError(kind)


def _tile_grid(kinds, rows, tm, heads):
    n_rows = rows // tm
    return ((heads, n_rows) if heads else (n_rows,)), (1 if heads else 0)


def tile_fwd(fn, name, kinds, args, outs, rows, tm, heads):
    grid, row_axis = _tile_grid(kinds, rows, tm, heads)
    n_in = len(args)
    out_shapes = [jax.ShapeDtypeStruct((rows, w), dt) for (_, w, dt) in outs]

    def body(*refs):
        vals = [r[...].astype(F32) for r in refs[:n_in]]
        row0 = pl.program_id(row_axis) * tm
        res = fn(_PLAIN, row0, *vals)
        for o_ref, v in zip(refs[n_in:], res):
            o_ref[...] = v.astype(o_ref.dtype)

    return pl.pallas_call(
        body, name=name, out_shape=out_shapes, grid=grid,
        in_specs=[_tile_spec(k, a.shape, tm, heads) for k, a in zip(kinds, args)],
        out_specs=[_tile_spec(k, (rows, w), tm, heads) for (k, w, _) in outs],
        compiler_params=_params(("arbitrary",) * len(grid)),
    )(*args)


def tile_bwd(fn, name, kinds, args, diff, outs, cts, rows, tm, heads):
    grid, row_axis = _tile_grid(kinds, rows, tm, heads)
    n_in, n_ct = len(args), len(cts)
    diff_idx = [i for i, d in enumerate(diff) if d]
    g_shapes, g_specs = [], []
    for i in diff_idx:
        k, _ = _kind(kinds[i])
        dt = args[i].dtype if k in ("row", "rowh") else F32
        g_shapes.append(jax.ShapeDtypeStruct(args[i].shape, dt))
        g_specs.append(_tile_spec(kinds[i], args[i].shape, tm, heads))

    def body(*refs):
        in_refs, ct_refs, g_refs = refs[:n_in], refs[n_in:n_in + n_ct], refs[n_in + n_ct:]
        vals = [r[...].astype(F32) for r in in_refs]
        row_id = pl.program_id(row_axis)
        row0 = row_id * tm

        def f(*dvals):
            full = list(vals)
            for i, dv in zip(diff_idx, dvals):
                full[i] = dv
            return tuple(fn(_DIFF, row0, *full))

        _, vjp = jax.vjp(f, *[vals[i] for i in diff_idx])
        grads = vjp(tuple(c[...].astype(F32) for c in ct_refs))
        for g_ref, i, g in zip(g_refs, diff_idx, grads):
            k, _ = _kind(kinds[i])
            if k in ("row", "rowh"):
                g_ref[...] = g.astype(g_ref.dtype)
            else:
                first = row_id == 0
                if heads and k == "par":
                    first = jnp.logical_and(first, pl.program_id(0) == 0)

                @pl.when(first)
                def _(g_ref=g_ref, g=g):
                    g_ref[...] = g

                @pl.when(jnp.logical_not(first))
                def _(g_ref=g_ref, g=g):
                    g_ref[...] += g

    return pl.pallas_call(
        body, name=name, out_shape=g_shapes, grid=grid,
        in_specs=[_tile_spec(k, a.shape, tm, heads) for k, a in zip(kinds, args)]
        + [_tile_spec(k, (rows, w), tm, heads) for (k, w, _) in outs],
        out_specs=g_specs,
        compiler_params=_params(("arbitrary",) * len(grid)),
    )(*args, *cts)


def make_tile_op(fn, name, kinds, diff, outs, rows, tm, heads=0):
    tm = min(tm, rows)

    @jax.custom_vjp
    def op(*args):
        return tuple(tile_fwd(fn, name + "_f", kinds, args, outs, rows, tm, heads))

    def fwd(*args):
        return op(*args), args

    def bwd(args, cts):
        grads = tile_bwd(fn, name + "_b", kinds, args, diff, outs, cts, rows, tm, heads)
        it = iter(grads)
        res = []
        for a, d in zip(args, diff):
            res.append(next(it).astype(a.dtype) if d else None)
        return tuple(res)

    op.defvjp(fwd, bwd)
    return op


def _rms(x, g):
    return x * lax.rsqrt(jnp.mean(x * x, axis=-1, keepdims=True) + EPS) * g


def fn_rms(ops, row0, x, g):
    return (_rms(x, g),)


def fn_relu2(ops, row0, a):
    r = jnp.maximum(a, 0.0)
    return (r * r,)


def fn_mul(ops, row0, a, b):
    return (a * b,)


def fn_rope(ops, row0, x, xs, c, s):
    return (x * c + xs * s,)


def _softmax(s):
    m = lax.stop_gradient(jnp.max(s, axis=-1, keepdims=True))
    e = jnp.exp(s - m)
    return e / jnp.sum(e, axis=-1, keepdims=True)


def fn_xattn(ops, row0, q, k, v):
    s = ops.b.nt(q, k) * (X_HEAD_DIM ** -0.5)
    return (ops.b.nn(_softmax(s), v),)


def fn_mla_attn(ops, row0, qn, qr, kn, kr, v):
    s = (ops.b.nt(qn, kn) + ops.b.nt(qr, kr)) * ((MLA_NOPE + MLA_ROPE) ** -0.5)
    rows = row0 + lax.broadcasted_iota(jnp.int32, s.shape, 0)
    cols = lax.broadcasted_iota(jnp.int32, s.shape, 1)
    s = jnp.where(rows >= cols, s, NEG_BIG)
    return (ops.b.nn(_softmax(s), v),)


def _silu(x):
    return x * jax.nn.sigmoid(x)


def fn_gdn_prep(ops, row0, qc, kc, vc):
    q, k, v = _silu(qc), _silu(kc), _silu(vc)
    q = q * lax.rsqrt(jnp.sum(q * q, -1, keepdims=True) + EPS) * (GDN_DK ** -0.5)
    k = k * lax.rsqrt(jnp.sum(k * k, -1, keepdims=True) + EPS)
    return q, k, v


def fn_gdn_gates(ops, row0, ba, alog, dtb):
    width = GDN_HEADS * GDN_DK
    beta = jax.nn.sigmoid(ba)
    z = ba + dtb
    softplus = jnp.maximum(z, 0.0) + jnp.log1p(jnp.exp(-jnp.abs(z)))
    g = -jnp.exp(alog) * softplus
    r = lax.broadcasted_iota(jnp.int32, (LANES, width), 0)
    c = lax.broadcasted_iota(jnp.int32, (LANES, width), 1) // GDN_DK
    e_beta = (r == c).astype(F32)
    e_g = (r == c + GDN_HEADS).astype(F32)
    return ops.h.nn(beta, e_beta), ops.h.nn(g, e_g)


def fn_gdn_out(ops, row0, o, gate, g):
    return (_rms(o, g) * _silu(gate),)


def _shift_down(x, d, t_idx):
    if d == 0:
        return x
    return jnp.where(t_idx >= d, pltpu.roll(x, d, axis=0), 0.0)


def _shift_up(x, d, t_idx):
    if d == 0:
        return x
    n = x.shape[0]
    return jnp.where(t_idx < n - d, pltpu.roll(x, n - d, axis=0), 0.0)


def conv_fwd(x, w, name):
    s, c = x.shape
    kw = w.shape[0]
    tc = _pick(c, (256, 128))

    def body(x_ref, w_ref, y_ref):
        xv = x_ref[...]
        t_idx = lax.broadcasted_iota(jnp.int32, xv.shape, 0)
        acc = jnp.zeros_like(xv)
        for j in range(kw):
            acc = acc + w_ref[j:j + 1, :] * _shift_down(xv, kw - 1 - j, t_idx)
        y_ref[...] = acc

    return pl.pallas_call(
        body, name=name, out_shape=jax.ShapeDtypeStruct((s, c), F32), grid=(c // tc,),
        in_specs=[pl.BlockSpec((s, tc), lambda i: (0, i)), pl.BlockSpec((kw, tc), lambda i: (0, i))],
        out_specs=pl.BlockSpec((s, tc), lambda i: (0, i)),
        compiler_params=_params(("parallel",)),
    )(x, w)


def conv_bwd(x, w, dy, name):
    s, c = x.shape
    kw = w.shape[0]
    tc = _pick(c, (256, 128))

    def body(x_ref, w_ref, dy_ref, dx_ref, dw_ref):
        xv, dyv = x_ref[...], dy_ref[...]
        t_idx = lax.broadcasted_iota(jnp.int32, xv.shape, 0)
        dx = jnp.zeros_like(xv)
        for j in range(kw):
            d = kw - 1 - j
            dx = dx + w_ref[j:j + 1, :] * _shift_up(dyv, d, t_idx)
            dw_ref[j:j + 1, :] = jnp.sum(dyv * _shift_down(xv, d, t_idx), axis=0, keepdims=True)
        dx_ref[...] = dx

    return pl.pallas_call(
        body, name=name,
        out_shape=[jax.ShapeDtypeStruct((s, c), F32), jax.ShapeDtypeStruct((kw, c), F32)],
        grid=(c // tc,),
        in_specs=[pl.BlockSpec((s, tc), lambda i: (0, i)), pl.BlockSpec((kw, tc), lambda i: (0, i)),
                  pl.BlockSpec((s, tc), lambda i: (0, i))],
        out_specs=[pl.BlockSpec((s, tc), lambda i: (0, i)), pl.BlockSpec((kw, tc), lambda i: (0, i))],
        compiler_params=_params(("parallel",)),
    )(x, w, dy)


def make_conv(name):
    @jax.custom_vjp
    def op(x, w):
        return conv_fwd(x, w, name + "_f")

    def fwd(x, w):
        return op(x, w), (x, w)

    def bwd(saved, dy):
        dx, dw = conv_bwd(saved[0], saved[1], dy, name + "_b")
        return dx, dw

    op.defvjp(fwd, bwd)
    return op


def _gdn_consts():
    c, d = GDN_CHUNK, GDN_DK
    i = lax.broadcasted_iota(jnp.int32, (c, c), 0)
    j = lax.broadcasted_iota(jnp.int32, (c, c), 1)
    tri = i >= j
    return dict(
        tri=tri, strict=i > j,
        tri_f=tri.astype(F32),
        eye=(i == j).astype(F32),
        last_col=(j == c - 1).astype(F32),
        sel_lane0=(lax.broadcasted_iota(jnp.int32, (d, c), 0) == 0).astype(F32),
        all_lane0=(lax.broadcasted_iota(jnp.int32, (c, d), 1) == 0).astype(F32),
        last_row=(lax.broadcasted_iota(jnp.int32, (c, d), 0) == c - 1).astype(F32),
    )


def _gdn_chunk(ops, q, k, v, g, beta, state):
    b, h = ops.b, ops.h
    k_ = _gdn_consts()
    gc = h.nn(k_["tri_f"], g)
    col = h.nn(gc, k_["sel_lane0"])
    row = h.nt(k_["all_lane0"], gc)
    decay = jnp.where(k_["tri"], jnp.exp(jnp.where(k_["tri"], col - row, 0.0)), 0.0)
    kb = k * beta
    m = jnp.where(k_["strict"], b.nt(kb, k) * decay, 0.0)
    p = -m
    t = k_["eye"] + p
    for _ in range(int(math.log2(GDN_CHUNK)) - 1):
        p = h.nn(p, p)
        t = t + h.nn(t, p)
    egc = jnp.exp(gc)
    u = b.nn(t, v * beta)
    w = b.nn(t, kb * egc)
    attn = b.nt(q, k) * decay
    v_new = u - b.nn(w, state)
    o = b.nn(q * egc, state) + b.nn(attn, v_new)
    g_last = h.nn(k_["last_col"], gc)
    g_last_sq = h.tn(k_["last_row"], gc)
    new_state = state * jnp.exp(g_last_sq) + b.tn(k * jnp.exp(g_last - gc), v_new)
    return o, new_state


def _gdn_specs(s):
    d = GDN_DK
    seq = pl.BlockSpec((s, d), lambda h: (0, h))
    st = pl.BlockSpec((None, s // GDN_CHUNK, d, d), lambda h: (h, 0, 0, 0))
    return seq, st


def gdn_fwd(q, k, v, g, beta, name):
    s = q.shape[0]
    n_chunks = s // GDN_CHUNK
    seq, st = _gdn_specs(s)

    def body(q_ref, k_ref, v_ref, g_ref, b_ref, o_ref, st_ref):
        def step(ci, state):
            rows = pl.ds(pl.multiple_of(ci * GDN_CHUNK, GDN_CHUNK), GDN_CHUNK)
            st_ref[ci] = state
            o, new_state = _gdn_chunk(_PLAIN, q_ref[rows, :], k_ref[rows, :], v_ref[rows, :],
                                      g_ref[rows, :], b_ref[rows, :], state)
            o_ref[rows, :] = o
            return new_state

        lax.fori_loop(0, n_chunks, step, jnp.zeros((GDN_DK, GDN_DK), F32))

    return pl.pallas_call(
        body, name=name,
        out_shape=[jax.ShapeDtypeStruct(q.shape, F32),
                   jax.ShapeDtypeStruct((GDN_HEADS, n_chunks, GDN_DK, GDN_DK), F32)],
        grid=(GDN_HEADS,), in_specs=[seq] * 5, out_specs=[seq, st],
        compiler_params=_params(("parallel",)),
    )(q, k, v, g, beta)


def gdn_bwd(q, k, v, g, beta, states, do, name):
    s = q.shape[0]
    n_chunks = s // GDN_CHUNK
    seq, st = _gdn_specs(s)

    def body(q_ref, k_ref, v_ref, g_ref, b_ref, st_ref, do_ref, dq_ref, dk_ref, dv_ref, dg_ref, db_ref):
        def step(it, dstate):
            ci = n_chunks - 1 - it
            rows = pl.ds(pl.multiple_of(ci * GDN_CHUNK, GDN_CHUNK), GDN_CHUNK)
            prim = (q_ref[rows, :], k_ref[rows, :], v_ref[rows, :], g_ref[rows, :], b_ref[rows, :], st_ref[ci])
            _, vjp = jax.vjp(functools.partial(_gdn_chunk, _DIFF), *prim)
            dq, dk, dv, dg, db, dstate_in = vjp((do_ref[rows, :], dstate))
            dq_ref[rows, :] = dq
            dk_ref[rows, :] = dk
            dv_ref[rows, :] = dv
            dg_ref[rows, :] = dg
            db_ref[rows, :] = db
            return dstate_in

        lax.fori_loop(0, n_chunks, step, jnp.zeros((GDN_DK, GDN_DK), F32))

    return pl.pallas_call(
        body, name=name,
        out_shape=[jax.ShapeDtypeStruct(q.shape, F32)] * 5,
        grid=(GDN_HEADS,), in_specs=[seq] * 5 + [st, seq], out_specs=[seq] * 5,
        compiler_params=_params(("parallel",)),
    )(q, k, v, g, beta, states, do)


def make_gdn(name):
    @jax.custom_vjp
    def op(q, k, v, g, beta):
        return gdn_fwd(q, k, v, g, beta, name + "_f")[0]

    def fwd(q, k, v, g, beta):
        o, states = gdn_fwd(q, k, v, g, beta, name + "_f")
        return o, (q, k, v, g, beta, states)

    def bwd(saved, do):
        return tuple(gdn_bwd(*saved, do, name + "_b"))

    op.defvjp(fwd, bwd)
    return op


def loss_head(x, g, target, name):
    s, d = x.shape
    tm = min(256, s)

    def body(x_ref, g_ref, t_ref, loss_ref, dx_ref, dg_ref):
        tgt = t_ref[...]

        def f(xv, gv):
            err = _rms(xv, gv) - tgt
            per_row = jnp.mean(err * err, axis=-1, keepdims=True)
            return 0.5 * jnp.sum(per_row, axis=0, keepdims=True)

        val, vjp = jax.vjp(f, x_ref[...], g_ref[...])
        dx, dg = vjp(jnp.ones((1, 1), F32))
        dx_ref[...] = dx
        first = pl.program_id(0) == 0

        @pl.when(first)
        def _():
            dg_ref[...] = dg
            loss_ref[...] = jnp.broadcast_to(val, loss_ref.shape)

        @pl.when(jnp.logical_not(first))
        def _():
            dg_ref[...] += dg
            loss_ref[...] += jnp.broadcast_to(val, loss_ref.shape)

    row = pl.BlockSpec((tm, d), lambda i: (i, 0))
    vec = pl.BlockSpec((1, d), lambda i: (0, 0))
    return pl.pallas_call(
        body, name=name,
        out_shape=[jax.ShapeDtypeStruct((1, LANES), F32), jax.ShapeDtypeStruct((s, d), F32),
                   jax.ShapeDtypeStruct((1, d), F32)],
        grid=(s // tm,), in_specs=[row, vec, row],
        out_specs=[pl.BlockSpec((1, LANES), lambda i: (0, 0)), row, vec],
        compiler_params=_params(("arbitrary",)),
    )(x, g, target)


def adamw(g8, w, m, v, name):
    rows, width = w.shape
    tr = _pick(rows, (256, 128, 64, 32, 16, 8))

    def body(g_ref, w_ref, m_ref, v_ref, go_ref, d_ref, mo_ref, vo_ref):
        g = g_ref[0].astype(F32)
        for p in range(1, N_DEV):
            g = g + g_ref[p].astype(F32)
        m_new = ADAM_B1 * m_ref[...] + (1.0 - ADAM_B1) * g
        v_new = ADAM_B2 * v_ref[...] + (1.0 - ADAM_B2) * (g * g)
        m_hat = m_new / (1.0 - ADAM_B1 ** ADAM_STEP)
        v_hat = v_new / (1.0 - ADAM_B2 ** ADAM_STEP)
        go_ref[...] = g
        d_ref[...] = -ADAM_LR * (m_hat / (jnp.sqrt(v_hat) + ADAM_EPS) + ADAM_WD * w_ref[...])
        mo_ref[...] = m_new
        vo_ref[...] = v_new

    blk = pl.BlockSpec((tr, width), lambda i: (i, 0))
    return pl.pallas_call(
        body, name=name, out_shape=[jax.ShapeDtypeStruct((rows, width), F32)] * 4,
        grid=(rows // tr,),
        in_specs=[pl.BlockSpec((N_DEV, tr, width), lambda i: (0, i, 0)), blk, blk, blk],
        out_specs=[blk] * 4,
        compiler_params=_params(("parallel",)),
    )(g8, w, m, v)


def exchange(items, name):
    n_items = len(items)
    hbm = pl.BlockSpec(memory_space=pltpu.HBM)
    out_shapes = []
    for mode, a in items:
        shape = (N_DEV,) + tuple(a.shape) if mode == "gather" else tuple(a.shape)
        out_shapes.append(jax.ShapeDtypeStruct(shape, a.dtype))

    def body(*refs):
        in_refs, out_refs = refs[:n_items], refs[n_items:2 * n_items]
        send_sems, recv_sems, local_sems = refs[2 * n_items:]
        x, y, c = lax.axis_index("x"), lax.axis_index("y"), lax.axis_index("c")
        me = 4 * x + 2 * y + c

        def src(k, p):
            return in_refs[k] if items[k][0] == "gather" else in_refs[k].at[p]

        local = [pltpu.make_async_copy(src(k, me), out_refs[k].at[me], local_sems.at[k])
                 for k in range(n_items)]
        for cp in local:
            cp.start()
        sends, recvs = [], []
        for r in range(1, N_DEV):
            px = (1 - x) if r & 4 else x
            py = (1 - y) if r & 2 else y
            pc = (1 - c) if r & 1 else c
            p = 4 * px + 2 * py + pc
            for k in range(n_items):
                sends.append(pltpu.make_async_remote_copy(
                    src_ref=src(k, p), dst_ref=out_refs[k].at[me],
                    send_sem=send_sems.at[k, r - 1], recv_sem=recv_sems.at[k, r - 1],
                    device_id=(px, py, pc), device_id_type=pl.DeviceIdType.MESH))
                recvs.append(pltpu.make_async_remote_copy(
                    src_ref=src(k, p), dst_ref=out_refs[k].at[p],
                    send_sem=send_sems.at[k, r - 1], recv_sem=recv_sems.at[k, r - 1],
                    device_id=(px, py, pc), device_id_type=pl.DeviceIdType.MESH))
        for cp in sends:
            cp.start()
        for cp in recvs:
            cp.wait_recv()
        for cp in sends:
            cp.wait_send()
        for cp in local:
            cp.wait()

    return pl.pallas_call(
        body, name=name, out_shape=out_shapes,
        in_specs=[hbm] * n_items, out_specs=[hbm] * n_items,
        scratch_shapes=[pltpu.SemaphoreType.DMA((n_items, N_DEV - 1)),
                        pltpu.SemaphoreType.DMA((n_items, N_DEV - 1)),
                        pltpu.SemaphoreType.DMA((n_items,))],
    )(*[a for _, a in items])


BIG = [
    ("mla_w_in", 1), ("mla_w_uq", 2), ("mla_w_ukv", 2), ("mla_w_o", 1),
    ("gdn_w_in", 2), ("gdn_w_o", 1), ("sc_w_in", 2), ("sc_w_o", 1),
    ("xa_w_q", 1), ("xa_w_kv", 2), ("xa_w_o", 1), ("mlp_w1", 2), ("mlp_w2", 1),
]
TINY = [("mla_q_norm", 1), ("mla_kv_norm", 1), ("gdn_conv_w", 2), ("sc_conv_w", 2)]
REPL = ["gdn_a_log", "gdn_dt_bias", "gdn_o_norm", "norm_mix", "norm_mem", "norm_mlp", "mem_norm", "final_norm"]
WEIGHTS = ["mla_w_in", "mla_q_norm", "mla_kv_norm", "mla_w_uq", "mla_w_ukv", "mla_w_o", "gdn_w_in",
           "gdn_conv_w", "gdn_a_log", "gdn_dt_bias", "gdn_o_norm", "gdn_w_o", "sc_w_in", "sc_conv_w",
           "sc_w_o", "norm_mix", "norm_mem", "norm_mlp", "xa_w_q", "xa_w_kv", "xa_w_o", "mlp_w1",
           "mlp_w2", "mem_norm", "final_norm"]


def from_shards(a8, axis):
    a = jnp.moveaxis(a8, 0, axis)
    shp = a.shape
    return a.reshape(shp[:axis] + (shp[axis] * shp[axis + 1],) + shp[axis + 2:])


def to_shards(a, axis):
    shp = a.shape
    a = a.reshape(shp[:axis] + (N_DEV, shp[axis] // N_DEV) + shp[axis + 1:])
    return jnp.moveaxis(a, axis, 0)


def pack_rows(flat_list, width, row_mult):
    lead = flat_list[0].shape[:-1]
    total = sum(a.shape[-1] for a in flat_list)
    rows = -(-total // width)
    rows = -(-rows // row_mult) * row_mult
    pad = rows * width - total
    parts = list(flat_list)
    if pad:
        parts.append(jnp.zeros(lead + (pad,), flat_list[0].dtype))
    return jnp.concatenate(parts, axis=-1).reshape(lead + (rows, width))


def unpack_rows(packed, shapes):
    lead = packed.shape[:-2]
    flat = packed.reshape(lead + (-1,))
    out, off = [], 0
    for shp in shapes:
        n = math.prod(shp)
        out.append(flat[..., off:off + n].reshape(lead + tuple(shp)))
        off += n
    return out


def _swap_halves(w):
    half = w.shape[-1] // 2
    return jnp.concatenate([w[..., half:], w[..., :half]], axis=-1)


def _pad_last(w, n):
    return jnp.pad(w, [(0, 0)] * (w.ndim - 1) + [(0, n - w.shape[-1])])


def rms_op(name, rows, d, out_dtype, tm=256):
    return make_tile_op(fn_rms, name, ["row", "par"], [True, True], [("row", d, out_dtype)], rows, min(tm, rows))


def trunk(p, x, mem, cos, sin):
    s, d = x.shape
    m_len = mem.shape[0]
    hd = MLA_NOPE

    zeros = jnp.zeros((s, hd - MLA_ROPE), F32)
    rope_c = jnp.concatenate([cos, cos, zeros], axis=-1)
    rope_s = jnp.concatenate([-sin, sin, zeros], axis=-1)

    mem_n = rms_op("rms_memory", m_len, d, BF16)(mem, p["mem_norm"].reshape(1, d))[0]

    for i in range(DEPTH):
        j, kind = i // N_MIXERS, i % N_MIXERS
        tag = f"l{i}"
        h = rms_op(tag + "_rms_mix", s, d, BF16)(x, p["norm_mix"][i].reshape(1, d))[0]
        if kind == 0:
            w_in = p["mla_w_in"][j]
            w_cq = w_in[:, :MLA_Q_RANK]
            w_ckv = w_in[:, MLA_Q_RANK:MLA_Q_RANK + MLA_KV_RANK]
            w_kr = w_in[:, MLA_Q_RANK + MLA_KV_RANK:]
            w_z = jnp.concatenate([w_cq, w_ckv, _pad_last(w_kr, hd), _pad_last(_swap_halves(w_kr), hd)], axis=-1)
            z = make_mm(tag + "_mla_in", F32)(h, w_z)
            c_q, c_kv = z[:, :MLA_Q_RANK], z[:, MLA_Q_RANK:MLA_Q_RANK + MLA_KV_RANK]
            kr_raw, kr_swp = z[:, -2 * hd:-hd], z[:, -hd:]
            c_qn = rms_op(tag + "_rms_q", s, MLA_Q_RANK, BF16)(c_q, p["mla_q_norm"][j].reshape(1, -1))[0]
            c_kvn = rms_op(tag + "_rms_kv", s, MLA_KV_RANK, BF16)(c_kv, p["mla_kv_norm"][j].reshape(1, -1))[0]
            w_uq = p["mla_w_uq"][j].reshape(MLA_Q_RANK, MLA_HEADS, MLA_NOPE + MLA_ROPE)
            w_qn = w_uq[:, :, :MLA_NOPE].reshape(MLA_Q_RANK, -1)
            w_qr = w_uq[:, :, MLA_NOPE:]
            w_qr_p = _pad_last(w_qr, hd).reshape(MLA_Q_RANK, -1)
            w_qr_s = _pad_last(_swap_halves(w_qr), hd).reshape(MLA_Q_RANK, -1)
            qall = make_mm(tag + "_mla_uq", F32)(c_qn, jnp.concatenate([w_qn, w_qr_p, w_qr_s], axis=-1))
            nq = MLA_HEADS * hd
            q_nope, q_raw, q_swp = qall[:, :nq], qall[:, nq:2 * nq], qall[:, 2 * nq:]
            w_ukv = p["mla_w_ukv"][j].reshape(MLA_KV_RANK, MLA_HEADS, MLA_NOPE + MLA_V)
            w_kn = w_ukv[:, :, :MLA_NOPE].reshape(MLA_KV_RANK, -1)
            w_v = w_ukv[:, :, MLA_NOPE:].reshape(MLA_KV_RANK, -1)
            kv = make_mm(tag + "_mla_ukv", BF16)(c_kvn, jnp.concatenate([w_kn, w_v], axis=-1))
            k_nope, v = kv[:, :nq], kv[:, nq:]
            q_rope = make_tile_op(fn_rope, tag + "_rope_q", [("rowh", hd), ("rowh", hd), "row", "row"],
                                  [True, True, False, False], [(("rowh", hd), nq, F32)], s, 512, MLA_HEADS)(
                q_raw, q_swp, rope_c, rope_s)[0]
            k_rope = make_tile_op(fn_rope, tag + "_rope_k", ["row", "row", "row", "row"],
                                  [True, True, False, False], [("row", hd, F32)], s, 512)(
                kr_raw, kr_swp, rope_c, rope_s)[0]
            o = make_tile_op(fn_mla_attn, tag + "_mla_attn",
                             [("rowh", hd), ("rowh", hd), ("parh", hd), "par", ("parh", hd)],
                             [True] * 5, [(("rowh", hd), nq, BF16)], s, 256, MLA_HEADS)(
                q_nope, q_rope, k_nope, k_rope, v)[0]
            x = make_mm(tag + "_mla_o", F32, with_res=True)(x, o, p["mla_w_o"][j])
        elif kind == 1:
            nqkv = 3 * GDN_HEADS * GDN_DK
            ng = GDN_HEADS * GDN_DK
            w_in = p["gdn_w_in"][j]
            w_z = jnp.concatenate([w_in[:, :nqkv + ng], _pad_last(w_in[:, nqkv + ng:], LANES)], axis=-1)
            z = make_mm(tag + "_gdn_in", F32)(h, w_z)
            qkv_pre, gate, ba = z[:, :nqkv], z[:, nqkv:nqkv + ng], z[:, nqkv + ng:]
            qkv_c = make_conv(tag + "_gdn_conv")(qkv_pre, p["gdn_conv_w"][j])
            q, k, v = make_tile_op(fn_gdn_prep, tag + "_gdn_prep", [("rowh", GDN_DK)] * 3, [True] * 3,
                                   [(("rowh", GDN_DK), ng, F32)] * 3, s, 512, GDN_HEADS)(
                qkv_c[:, :ng], qkv_c[:, ng:2 * ng], qkv_c[:, 2 * ng:])
            alog = jnp.pad(p["gdn_a_log"][j].reshape(1, -1), ((0, 0), (GDN_HEADS, LANES - 2 * GDN_HEADS)))
            dtb = jnp.pad(p["gdn_dt_bias"][j].reshape(1, -1), ((0, 0), (GDN_HEADS, LANES - 2 * GDN_HEADS)))
            beta_b, g_b = make_tile_op(fn_gdn_gates, tag + "_gdn_gates", ["row", "par", "par"], [True] * 3,
                                       [("row", ng, F32)] * 2, s, 256)(ba, alog, dtb)
            o = make_gdn(tag + "_gdn_core")(q, k, v, g_b, beta_b)
            o = make_tile_op(fn_gdn_out, tag + "_gdn_out", [("rowh", GDN_DK), ("rowh", GDN_DK), "par"],
                             [True] * 3, [(("rowh", GDN_DK), ng, BF16)], s, 512, GDN_HEADS)(
                o, gate, p["gdn_o_norm"][j].reshape(1, -1))[0]
            x = make_mm(tag + "_gdn_o", F32, with_res=True)(x, o, p["gdn_w_o"][j])
        else:
            z = make_mm(tag + "_sc_in", F32)(h, p["sc_w_in"][j])
            b_gate, c_gate, u = z[:, :d], z[:, d:2 * d], z[:, 2 * d:]
            cu = make_tile_op(fn_mul, tag + "_sc_cu", ["row", "row"], [True, True], [("row", d, F32)], s, 512)(
                c_gate, u)[0]
            cv = make_conv(tag + "_sc_conv")(cu, p["sc_conv_w"][j])
            yv = make_tile_op(fn_mul, tag + "_sc_gate", ["row", "row"], [True, True], [("row", d, BF16)], s, 512)(
                b_gate, cv)[0]
            x = make_mm(tag + "_sc_o", F32, with_res=True)(x, yv, p["sc_w_o"][j])

        hx = rms_op(tag + "_rms_mem", s, d, BF16)(x, p["norm_mem"][i].reshape(1, d))[0]
        q = make_mm(tag + "_xa_q", BF16)(hx, p["xa_w_q"][i])
        kv = make_mm(tag + "_xa_kv", BF16)(mem_n, p["xa_w_kv"][i])
        o = make_tile_op(fn_xattn, tag + "_xattn",
                         [("rowh", X_HEAD_DIM), ("parh", X_HEAD_DIM), ("parh", X_HEAD_DIM)], [True] * 3,
                         [(("rowh", X_HEAD_DIM), d, BF16)], s, 512, X_HEADS)(q, kv[:, :d], kv[:, d:])[0]
        x = make_mm(tag + "_xa_o", F32, with_res=True)(x, o, p["xa_w_o"][i])

        hm = rms_op(tag + "_rms_mlp", s, d, BF16)(x, p["norm_mlp"][i].reshape(1, d))[0]
        a = make_mm(tag + "_mlp_1", BF16)(hm, p["mlp_w1"][i])
        bsq = make_tile_op(fn_relu2, tag + "_relu2", ["row"], [True], [("row", a.shape[1], BF16)], s, 256)(a)[0]
        x = make_mm(tag + "_mlp_2", F32, with_res=True)(x, bsq, p["mlp_w2"][i])
    return x


def kernel(x, mem, positions, mla_w_in, mla_q_norm, mla_kv_norm, mla_w_uq, mla_w_ukv, mla_w_o, gdn_w_in, gdn_conv_w, gdn_a_log, gdn_dt_bias, gdn_o_norm, gdn_w_o, sc_w_in, sc_conv_w, sc_w_o, norm_mix, norm_mem, norm_mlp, xa_w_q, xa_w_kv, xa_w_o, mlp_w1, mlp_w2, mem_norm, final_norm, loss_target, m_mla_w_in, m_mla_q_norm, m_mla_kv_norm, m_mla_w_uq, m_mla_w_ukv, m_mla_w_o, m_gdn_w_in, m_gdn_conv_w, m_gdn_a_log, m_gdn_dt_bias, m_gdn_o_norm, m_gdn_w_o, m_sc_w_in, m_sc_conv_w, m_sc_w_o, m_norm_mix, m_norm_mem, m_norm_mlp, m_xa_w_q, m_xa_w_kv, m_xa_w_o, m_mlp_w1, m_mlp_w2, m_mem_norm, m_final_norm, v_mla_w_in, v_mla_q_norm, v_mla_kv_norm, v_mla_w_uq, v_mla_w_ukv, v_mla_w_o, v_gdn_w_in, v_gdn_conv_w, v_gdn_a_log, v_gdn_dt_bias, v_gdn_o_norm, v_gdn_w_o, v_sc_w_in, v_sc_conv_w, v_sc_w_o, v_norm_mix, v_norm_mem, v_norm_mlp, v_xa_w_q, v_xa_w_kv, v_xa_w_o, v_mlp_w1, v_mlp_w2, v_mem_norm, v_final_norm):
    args = locals()
    w_loc = {n: args[n] for n in WEIGHTS}
    m_loc = {n: args["m_" + n] for n in WEIGHTS}
    v_loc = {n: args["v_" + n] for n in WEIGHTS}
    me = 4 * lax.axis_index("x") + 2 * lax.axis_index("y") + lax.axis_index("c")

    big_pack = pack_rows([w_loc[n].astype(BF16).reshape(-1) for n, _ in BIG], PACK_W, 16)
    tiny_pack = pack_rows([w_loc[n].reshape(-1) for n, _ in TINY], LANES, 8)
    big_all, tiny_all = exchange([("gather", big_pack), ("gather", tiny_pack)], "gather_weights")
    params = {}
    for (n, ax), a8 in zip(BIG, unpack_rows(big_all, [w_loc[n].shape for n, _ in BIG])):
        params[n] = from_shards(a8, ax)
    for (n, ax), a8 in zip(TINY, unpack_rows(tiny_all, [w_loc[n].shape for n, _ in TINY])):
        params[n] = from_shards(a8, ax)
    for n in REPL:
        params[n] = w_loc[n]

    inv_freq = ROPE_THETA ** (-jnp.arange(0, MLA_ROPE, 2, dtype=F32) / MLA_ROPE)
    ang = positions[0].astype(F32)[:, None] * inv_freq
    cos, sin = jnp.cos(ang), jnp.sin(ang)
    trunk_params = {n: a for n, a in params.items() if n != "final_norm"}
    x_out, vjp = jax.vjp(lambda p, xx: trunk(p, xx, mem[0], cos, sin), trunk_params, x[0])
    loss_vec, dx_out, d_final = loss_head(x_out, params["final_norm"].reshape(1, -1), loss_target[0], "loss_head")
    grads, grad_x = vjp(dx_out)
    grads = dict(grads)
    grads["final_norm"] = d_final.reshape(-1)

    big_g = pack_rows([to_shards(grads[n], ax).astype(BF16).reshape(N_DEV, -1) for n, ax in BIG], PACK_W, 16)
    small_names = [n for n, _ in TINY] + REPL
    small_g = pack_rows([loss_vec[0, :1]] + [grads[n].astype(F32).reshape(-1) for n in small_names], PACK_W, 8)
    big_recv, small_recv = exchange([("scatter", big_g), ("gather", small_g)], "exchange_grads")

    def pack_local(d):
        return pack_rows([d[n].reshape(-1) for n, _ in BIG], PACK_W, 16)

    outs_big = adamw(big_recv, pack_local(w_loc), pack_local(m_loc), pack_local(v_loc), "adamw_big")
    big_shapes = [w_loc[n].shape for n, _ in BIG]
    res = {}
    for kind, packed in zip(("grad", "delta", "m", "v"), outs_big):
        for (n, _), a in zip(BIG, unpack_rows(packed, big_shapes)):
            res[(kind, n)] = a

    def full_small(d):
        parts = [jnp.zeros((1,), F32)]
        for n, ax in TINY:
            full_shape = params[n].shape
            start = [0] * len(full_shape)
            start[ax] = me * d[n].shape[ax]
            parts.append(lax.dynamic_update_slice(jnp.zeros(full_shape, F32), d[n], start).reshape(-1))
        parts += [d[n].reshape(-1) for n in REPL]
        return pack_rows(parts, PACK_W, 8)

    outs_small = adamw(small_recv, full_small(w_loc), full_small(m_loc), full_small(v_loc), "adamw_small")
    small_shapes = [(1,)] + [params[n].shape for n, _ in TINY] + [w_loc[n].shape for n in REPL]
    loss = None
    for kind, packed in zip(("grad", "delta", "m", "v"), outs_small):
        parts = unpack_rows(packed, small_shapes)
        if kind == "grad":
            loss = parts[0][0]
        for (n, ax), a in zip(TINY, parts[1:1 + len(TINY)]):
            start = [0] * a.ndim
            start[ax] = me * w_loc[n].shape[ax]
            res[(kind, n)] = lax.dynamic_slice(a, start, w_loc[n].shape)
        for n, a in zip(REPL, parts[1 + len(TINY):]):
            res[(kind, n)] = a

    out = [loss, grad_x[None]]
    for kind in ("grad", "delta", "m", "v"):
        out += [res[(kind, n)] for n in WEIGHTS]
    return tuple(out)
```

```python
import functools
import math

import jax
import jax.numpy as jnp
from jax import lax
from jax.experimental import pallas as pl
from jax.experimental.pallas import tpu as pltpu

F32 = jnp.float32
BF16 = jnp.bfloat16

N_DEV = 8
LANES = 128
EPS = 1e-6
ROPE_THETA = 10000.0
MLA_HEADS, MLA_NOPE, MLA_ROPE, MLA_V = 8, 128, 64, 128
MLA_Q_RANK, MLA_KV_RANK = 384, 256
GDN_HEADS, GDN_DK, GDN_CONV, GDN_CHUNK = 8, 128, 4, 64
X_HEADS, X_HEAD_DIM = 4, 256
DEPTH, N_MIXERS = 4, 3
ADAM_LR, ADAM_B1, ADAM_B2, ADAM_EPS, ADAM_WD, ADAM_STEP = 0.001, 0.9, 0.999, 1e-08, 0.01, 10
NEG_BIG = -1e30
PACK_W = 1024


_NN = ((1,), (0,))
_NT = ((1,), (1,))
_TN = ((0,), (0,))


def _dg(a, b, dims, hi):
    if hi:
        return lax.dot_general(a, b, (dims, ((), ())), precision=lax.Precision.HIGHEST,
                               preferred_element_type=F32)
    return lax.dot_general(a.astype(BF16), b.astype(BF16), (dims, ((), ())), preferred_element_type=F32)


class _Ops:
    def __init__(self, hi, differentiable):
        def nn(a, b):
            return _dg(a, b, _NN, hi)

        def nt(a, b):
            return _dg(a, b, _NT, hi)

        def tn(a, b):
            return _dg(a, b, _TN, hi)

        if differentiable:
            dnn = jax.custom_vjp(nn)
            dnn.defvjp(lambda a, b: (nn(a, b), (a, b)), lambda r, g: (nt(g, r[1]), tn(r[0], g)))
            dnt = jax.custom_vjp(nt)
            dnt.defvjp(lambda a, b: (nt(a, b), (a, b)), lambda r, g: (nn(g, r[1]), tn(g, r[0])))
            dtn = jax.custom_vjp(tn)
            dtn.defvjp(lambda a, b: (tn(a, b), (a, b)), lambda r, g: (nt(r[1], g), nn(r[0], g)))
            nn, nt, tn = dnn, dnt, dtn
        self.nn, self.nt, self.tn = nn, nt, tn


class _OpSet:
    def __init__(self, differentiable):
        self.b = _Ops(False, differentiable)
        self.h = _Ops(True, differentiable)


_PLAIN = _OpSet(False)
_DIFF = _OpSet(True)


def _params(sem):
    return pltpu.CompilerParams(dimension_semantics=sem)


def _pick(n, cands):
    for c in cands:
        if n % c == 0:
            return c
    return n


def matmul(a, b, form, out_dtype, name, res=None, blocked=False):
    if form == "nn":
        m, k = a.shape
        k2, n = (b.shape[1], N_DEV * b.shape[2]) if blocked else b.shape
    elif form == "nt":
        m, k = a.shape
        n, k2 = (b.shape[1], N_DEV * b.shape[2]) if blocked else b.shape
    else:
        (k, m), (k2, n) = a.shape, b.shape
    assert k == k2, (a.shape, b.shape, form)
    tm = m if m <= 512 else _pick(m, (512, 384, 256, 128))
    tn = n if n <= 1024 else _pick(n, (1024, 768, 512, 384, 256, 128))
    tk = k if k <= 1024 else _pick(k, (1024, 768, 512, 384, 256, 128))
    if blocked and form in ("nn", "tn"):
        tn = n // N_DEV
    if blocked and form == "nt":
        tk = k // N_DEV
    nk = k // tk
    dims = {"nn": _NN, "nt": _NT, "tn": _TN}[form]

    a_spec = {"nn": pl.BlockSpec((tm, tk), lambda i, j, kk: (i, kk)),
              "nt": pl.BlockSpec((tm, tk), lambda i, j, kk: (i, kk)),
              "tn": pl.BlockSpec((tk, tm), lambda i, j, kk: (kk, i))}[form]
    if blocked and form == "nn":
        b_spec = pl.BlockSpec((None, tk, tn), lambda i, j, kk: (j, kk, 0))
    elif blocked and form == "nt":
        b_spec = pl.BlockSpec((None, tn, tk), lambda i, j, kk: (kk, j, 0))
    else:
        b_spec = {"nn": pl.BlockSpec((tk, tn), lambda i, j, kk: (kk, j)),
                  "nt": pl.BlockSpec((tn, tk), lambda i, j, kk: (j, kk)),
                  "tn": pl.BlockSpec((tk, tn), lambda i, j, kk: (kk, j))}[form]
    c_spec = pl.BlockSpec((tm, tn), lambda i, j, kk: (i, j))
    out_shape = jax.ShapeDtypeStruct((m, n), out_dtype)
    o_spec = c_spec
    if blocked and form == "tn":
        out_shape = jax.ShapeDtypeStruct((N_DEV, m, tn), out_dtype)
        o_spec = pl.BlockSpec((None, tm, tn), lambda i, j, kk: (j, i, 0))
    has_res = res is not None

    def body(*refs):
        a_ref, b_ref = refs[0], refs[1]
        r_ref = refs[2] if has_res else None
        o_ref = refs[3] if has_res else refs[2]
        part = _dg(a_ref[...], b_ref[...], dims, False)

        def finish(acc):
            if has_res:
                acc = acc + r_ref[...].astype(F32)
            o_ref[...] = acc.astype(out_dtype)

        if nk == 1:
            finish(part)
        else:
            acc_ref = refs[-1]
            kk = pl.program_id(2)

            @pl.when(kk == 0)
            def _():
                acc_ref[...] = part

            @pl.when(jnp.logical_and(kk > 0, kk < nk - 1))
            def _():
                acc_ref[...] += part

            @pl.when(kk == nk - 1)
            def _():
                finish(acc_ref[...] + part)

    in_specs = [a_spec, b_spec] + ([c_spec] if has_res else [])
    args = (a, b) + ((res,) if has_res else ())
    return pl.pallas_call(
        body, name=name,
        out_shape=out_shape,
        grid=(m // tm, n // tn, nk),
        in_specs=in_specs, out_specs=o_spec,
        scratch_shapes=[pltpu.VMEM((tm, tn), F32)] if nk > 1 else [],
        compiler_params=_params(("parallel", "parallel", "arbitrary")),
    )(*args)


def make_mm(name, out_dtype, with_res=False, blocked=False):
    def bwd_mm(a, w, g):
        da = matmul(g, w, "nt", a.dtype, name + "_da", blocked=blocked)
        dw = matmul(a, g, "tn", w.dtype, name + "_dw", blocked=blocked)
        return da, dw

    if with_res:
        @jax.custom_vjp
        def op(res, a, w):
            return matmul(a, w, "nn", out_dtype, name + "_f", res=res, blocked=blocked)

        def fwd(res, a, w):
            return op(res, a, w), (a, w)

        def bwd(saved, g):
            return (g,) + bwd_mm(*saved, g)
    else:
        @jax.custom_vjp
        def op(a, w):
            return matmul(a, w, "nn", out_dtype, name + "_f", blocked=blocked)

        def fwd(a, w):
            return op(a, w), (a, w)

        def bwd(saved, g):
            return bwd_mm(*saved, g)
    op.defvjp(fwd, bwd)
    return op


def _kind(k):
    if isinstance(k, str):
        return k, None, 1
    return k[0], k[1], (k[2] if len(k) > 2 else 1)


def _tile_spec(kind, shape, tm, heads):
    k, d, ns = _kind(kind)
    if k == "row":
        return pl.BlockSpec((tm, shape[1]), (lambda h, i: (i, 0)) if heads else (lambda i: (i, 0)))
    if k == "par":
        return pl.BlockSpec(tuple(shape), (lambda h, i: (0, 0)) if heads else (lambda i: (0, 0)))
    if k == "rowh":
        return pl.BlockSpec((tm, d * ns), lambda h, i: (i, h))
    if k == "parh":
        return pl.BlockSpec((shape[0], d * ns), lambda h, i: (0, h))
    raise ValueError(kind)


def _tile_grid(kinds, rows, tm, heads):
    n_rows = rows // tm
    return ((heads, n_rows) if heads else (n_rows,)), (1 if heads else 0)


def _split_vals(kinds, refs):
    vals, counts = [], []
    for kind, r in zip(kinds, refs):
        _, d, ns = _kind(kind)
        v = r[...].astype(F32)
        vals += [v] if ns == 1 else [v[:, p * d:(p + 1) * d] for p in range(ns)]
        counts.append(ns)
    return vals, counts


def tile_fwd(fn, name, kinds, args, outs, rows, tm, heads):
    grid, row_axis = _tile_grid(kinds, rows, tm, heads)
    n_in = len(args)
    out_shapes = [jax.ShapeDtypeStruct((rows, w), dt) for (_, w, dt) in outs]

    def body(*refs):
        vals, _ = _split_vals(kinds, refs[:n_in])
        row0 = pl.program_id(row_axis) * tm
        res = list(fn(_PLAIN, row0, *vals))
        for o_ref, (k, _, _) in zip(refs[n_in:], outs):
            pieces = [res.pop(0) for _ in range(_kind(k)[2])]
            v = pieces[0] if len(pieces) == 1 else jnp.concatenate(pieces, axis=-1)
            o_ref[...] = v.astype(o_ref.dtype)

    return pl.pallas_call(
        body, name=name, out_shape=out_shapes, grid=grid,
        in_specs=[_tile_spec(k, a.shape, tm, heads) for k, a in zip(kinds, args)],
        out_specs=[_tile_spec(k, (rows, w), tm, heads) for (k, w, _) in outs],
        compiler_params=_params(("arbitrary",) * len(grid)),
    )(*args)


def tile_bwd(fn, name, kinds, args, diff, outs, cts, rows, tm, heads):
    grid, row_axis = _tile_grid(kinds, rows, tm, heads)
    n_in, n_ct = len(args), len(cts)
    diff_idx = [i for i, d in enumerate(diff) if d]
    g_shapes, g_specs = [], []
    for i in diff_idx:
        k = _kind(kinds[i])[0]
        dt = args[i].dtype if k in ("row", "rowh") else F32
        g_shapes.append(jax.ShapeDtypeStruct(args[i].shape, dt))
        g_specs.append(_tile_spec(kinds[i], args[i].shape, tm, heads))

    def body(*refs):
        in_refs, ct_refs, g_refs = refs[:n_in], refs[n_in:n_in + n_ct], refs[n_in + n_ct:]
        vals, counts = _split_vals(kinds, in_refs)
        first_piece = [sum(counts[:i]) for i in range(n_in)]
        flat_diff = [first_piece[i] + p for i in diff_idx for p in range(counts[i])]
        row_id = pl.program_id(row_axis)
        row0 = row_id * tm

        def f(*dvals):
            full = list(vals)
            for i, dv in zip(flat_diff, dvals):
                full[i] = dv
            return tuple(fn(_DIFF, row0, *full))

        _, vjp = jax.vjp(f, *[vals[i] for i in flat_diff])
        ct_vals, _ = _split_vals([k for (k, _, _) in outs], ct_refs)
        flat_grads = list(vjp(tuple(ct_vals)))
        for g_ref, i in zip(g_refs, diff_idx):
            pieces = [flat_grads.pop(0) for _ in range(counts[i])]
            g = pieces[0] if len(pieces) == 1 else jnp.concatenate(pieces, axis=-1)
            k = _kind(kinds[i])[0]
            if k in ("row", "rowh"):
                g_ref[...] = g.astype(g_ref.dtype)
            else:
                first = row_id == 0
                if heads and k == "par":
                    first = jnp.logical_and(first, pl.program_id(0) == 0)

                @pl.when(first)
                def _(g_ref=g_ref, g=g):
                    g_ref[...] = g

                @pl.when(jnp.logical_not(first))
                def _(g_ref=g_ref, g=g):
                    g_ref[...] += g

    return pl.pallas_call(
        body, name=name, out_shape=g_shapes, grid=grid,
        in_specs=[_tile_spec(k, a.shape, tm, heads) for k, a in zip(kinds, args)]
        + [_tile_spec(k, (rows, w), tm, heads) for (k, w, _) in outs],
        out_specs=g_specs,
        compiler_params=_params(("arbitrary",) * len(grid)),
    )(*args, *cts)


def make_tile_op(fn, name, kinds, diff, outs, rows, tm, heads=0):
    tm = min(tm, rows)

    @jax.custom_vjp
    def op(*args):
        return tuple(tile_fwd(fn, name + "_f", kinds, args, outs, rows, tm, heads))

    def fwd(*args):
        return op(*args), args

    def bwd(args, cts):
        grads = tile_bwd(fn, name + "_b", kinds, args, diff, outs, cts, rows, tm, heads)
        it = iter(grads)
        res = []
        for a, d in zip(args, diff):
            res.append(next(it).astype(a.dtype) if d else None)
        return tuple(res)

    op.defvjp(fwd, bwd)
    return op


def _rms(x, g):
    return x * lax.rsqrt(jnp.mean(x * x, axis=-1, keepdims=True) + EPS) * g


def fn_rms(ops, row0, x, g):
    return (_rms(x, g),)


def fn_relu2(ops, row0, a):
    r = jnp.maximum(a, 0.0)
    return (r * r,)


def fn_mul(ops, row0, a, b):
    return (a * b,)


def fn_rope(ops, row0, *t):
    nh = (len(t) - 2) // 2
    c, s = t[-2], t[-1]
    return tuple(x * c + xs * s for x, xs in zip(t[:nh], t[nh:2 * nh]))


def _softmax(s):
    m = lax.stop_gradient(jnp.max(s, axis=-1, keepdims=True))
    e = jnp.exp(s - m)
    return e / jnp.sum(e, axis=-1, keepdims=True)


def fn_xattn(ops, row0, q, k, v):
    s = ops.b.nt(q, k) * (X_HEAD_DIM ** -0.5)
    return (ops.b.nn(_softmax(s), v),)


def fn_mla_attn(ops, row0, qn, qr, kn, v, kr):
    s = (ops.b.nt(qn, kn) + ops.b.nt(qr, kr)) * ((MLA_NOPE + MLA_ROPE) ** -0.5)
    rows = row0 + lax.broadcasted_iota(jnp.int32, s.shape, 0)
    cols = lax.broadcasted_iota(jnp.int32, s.shape, 1)
    s = jnp.where(rows >= cols, s, NEG_BIG)
    return (ops.b.nn(_softmax(s), v),)


def _silu(x):
    return x * jax.nn.sigmoid(x)


def fn_gdn_prep(ops, row0, *t):
    nh = len(t) // 3
    qs, ks, vs = [], [], []
    for qc, kc, vc in zip(t[:nh], t[nh:2 * nh], t[2 * nh:]):
        q, k = _silu(qc), _silu(kc)
        qs.append(q * lax.rsqrt(jnp.sum(q * q, -1, keepdims=True) + EPS) * (GDN_DK ** -0.5))
        ks.append(k * lax.rsqrt(jnp.sum(k * k, -1, keepdims=True) + EPS))
        vs.append(_silu(vc))
    return tuple(qs + ks + vs)


def fn_gdn_gates(ops, row0, ba, alog, dtb):
    width = GDN_HEADS * GDN_DK
    beta = jax.nn.sigmoid(ba)
    z = ba + dtb
    softplus = jnp.maximum(z, 0.0) + jnp.log1p(jnp.exp(-jnp.abs(z)))
    g = -jnp.exp(alog) * softplus
    r = lax.broadcasted_iota(jnp.int32, (LANES, width), 0)
    c = lax.broadcasted_iota(jnp.int32, (LANES, width), 1) // GDN_DK
    e_beta = (r == c).astype(F32)
    e_g = (r == c + GDN_HEADS).astype(F32)
    return ops.h.nn(beta, e_beta), ops.h.nn(g, e_g)


def fn_gdn_out(ops, row0, *t):
    nh = (len(t) - 1) // 2
    g = t[-1]
    return tuple(_rms(o, g) * _silu(gate) for o, gate in zip(t[:nh], t[nh:2 * nh]))


def _shift_down(x, d, t_idx):
    if d == 0:
        return x
    return jnp.where(t_idx >= d, pltpu.roll(x, d, axis=0), 0.0)


def _shift_up(x, d, t_idx):
    if d == 0:
        return x
    n = x.shape[0]
    return jnp.where(t_idx < n - d, pltpu.roll(x, n - d, axis=0), 0.0)


def conv_fwd(x, w, name):
    s, c = x.shape
    kw = w.shape[0]
    tc = _pick(c, (256, 128))

    def body(x_ref, w_ref, y_ref):
        xv = x_ref[...]
        t_idx = lax.broadcasted_iota(jnp.int32, xv.shape, 0)
        acc = jnp.zeros_like(xv)
        for j in range(kw):
            acc = acc + w_ref[j:j + 1, :] * _shift_down(xv, kw - 1 - j, t_idx)
        y_ref[...] = acc

    return pl.pallas_call(
        body, name=name, out_shape=jax.ShapeDtypeStruct((s, c), F32), grid=(c // tc,),
        in_specs=[pl.BlockSpec((s, tc), lambda i: (0, i)), pl.BlockSpec((kw, tc), lambda i: (0, i))],
        out_specs=pl.BlockSpec((s, tc), lambda i: (0, i)),
        compiler_params=_params(("parallel",)),
    )(x, w)


def conv_bwd(x, w, dy, name):
    s, c = x.shape
    kw = w.shape[0]
    tc = _pick(c, (256, 128))

    def body(x_ref, w_ref, dy_ref, dx_ref, dw_ref):
        xv, dyv = x_ref[...], dy_ref[...]
        t_idx = lax.broadcasted_iota(jnp.int32, xv.shape, 0)
        dx = jnp.zeros_like(xv)
        for j in range(kw):
            d = kw - 1 - j
            dx = dx + w_ref[j:j + 1, :] * _shift_up(dyv, d, t_idx)
            dw_ref[j:j + 1, :] = jnp.sum(dyv * _shift_down(xv, d, t_idx), axis=0, keepdims=True)
        dx_ref[...] = dx

    return pl.pallas_call(
        body, name=name,
        out_shape=[jax.ShapeDtypeStruct((s, c), F32), jax.ShapeDtypeStruct((kw, c), F32)],
        grid=(c // tc,),
        in_specs=[pl.BlockSpec((s, tc), lambda i: (0, i)), pl.BlockSpec((kw, tc), lambda i: (0, i)),
                  pl.BlockSpec((s, tc), lambda i: (0, i))],
        out_specs=[pl.BlockSpec((s, tc), lambda i: (0, i)), pl.BlockSpec((kw, tc), lambda i: (0, i))],
        compiler_params=_params(("parallel",)),
    )(x, w, dy)


def make_conv(name):
    @jax.custom_vjp
    def op(x, w):
        return conv_fwd(x, w, name + "_f")

    def fwd(x, w):
        return op(x, w), (x, w)

    def bwd(saved, dy):
        dx, dw = conv_bwd(saved[0], saved[1], dy, name + "_b")
        return dx, dw

    op.defvjp(fwd, bwd)
    return op


def _gdn_consts():
    c, d = GDN_CHUNK, GDN_DK
    i = lax.broadcasted_iota(jnp.int32, (c, c), 0)
    j = lax.broadcasted_iota(jnp.int32, (c, c), 1)
    tri = i >= j
    return dict(
        tri=tri, strict=i > j,
        tri_f=tri.astype(F32),
        eye=(i == j).astype(F32),
        last_col=(j == c - 1).astype(F32),
        sel_lane0=(lax.broadcasted_iota(jnp.int32, (d, c), 0) == 0).astype(F32),
        all_lane0=(lax.broadcasted_iota(jnp.int32, (c, d), 1) == 0).astype(F32),
        last_row=(lax.broadcasted_iota(jnp.int32, (c, d), 0) == c - 1).astype(F32),
    )


def _gdn_chunk(ops, q, k, v, g, beta, state):
    b, h = ops.b, ops.h
    k_ = _gdn_consts()
    gc = h.nn(k_["tri_f"], g)
    col = h.nn(gc, k_["sel_lane0"])
    row = h.nt(k_["all_lane0"], gc)
    decay = jnp.where(k_["tri"], jnp.exp(jnp.where(k_["tri"], col - row, 0.0)), 0.0)
    kb = k * beta
    m = jnp.where(k_["strict"], b.nt(kb, k) * decay, 0.0)
    p = -m
    t = k_["eye"] + p
    for _ in range(int(math.log2(GDN_CHUNK)) - 1):
        p = h.nn(p, p)
        t = t + h.nn(t, p)
    egc = jnp.exp(gc)
    u = b.nn(t, v * beta)
    w = b.nn(t, kb * egc)
    attn = b.nt(q, k) * decay
    v_new = u - b.nn(w, state)
    o = b.nn(q * egc, state) + b.nn(attn, v_new)
    g_last = h.nn(k_["last_col"], gc)
    g_last_sq = h.tn(k_["last_row"], gc)
    new_state = state * jnp.exp(g_last_sq) + b.tn(k * jnp.exp(g_last - gc), v_new)
    return o, new_state


GDN_HEAD_GROUP = 8
GDN_TILE_CHUNKS = 4


def _gdn_specs(s, reverse):
    d, hg = GDN_DK, GDN_HEAD_GROUP
    tile = min(GDN_TILE_CHUNKS * GDN_CHUNK, s)
    n_tiles = s // tile
    t_of = (lambda t: n_tiles - 1 - t) if reverse else (lambda t: t)
    seq = pl.BlockSpec((tile, hg * d), lambda grp, t: (t_of(t), grp))
    st = pl.BlockSpec((hg, tile // GDN_CHUNK, d, d), lambda grp, t: (grp, t_of(t), 0, 0))
    return seq, st, tile, n_tiles


def gdn_fwd(q, k, v, g, beta, name):
    s = q.shape[0]
    d, hg = GDN_DK, GDN_HEAD_GROUP
    seq, st, tile, n_tiles = _gdn_specs(s, False)

    def body(q_ref, k_ref, v_ref, g_ref, b_ref, o_ref, st_ref, state_scr):
        @pl.when(pl.program_id(1) == 0)
        def _():
            state_scr[...] = jnp.zeros_like(state_scr)

        def step(ci, carry):
            rows = pl.ds(pl.multiple_of(ci * GDN_CHUNK, GDN_CHUNK), GDN_CHUNK)
            for h in range(hg):
                cols = slice(h * d, (h + 1) * d)
                state = state_scr[h]
                st_ref[h, ci] = state
                o, new_state = _gdn_chunk(_PLAIN, q_ref[rows, cols], k_ref[rows, cols], v_ref[rows, cols],
                                          g_ref[rows, cols], b_ref[rows, cols], state)
                o_ref[rows, cols] = o
                state_scr[h] = new_state
            return carry

        lax.fori_loop(0, tile // GDN_CHUNK, step, 0)

    return pl.pallas_call(
        body, name=name,
        out_shape=[jax.ShapeDtypeStruct(q.shape, F32),
                   jax.ShapeDtypeStruct((GDN_HEADS, s // GDN_CHUNK, d, d), F32)],
        grid=(GDN_HEADS // hg, n_tiles), in_specs=[seq] * 5, out_specs=[seq, st],
        scratch_shapes=[pltpu.VMEM((hg, d, d), F32)],
        compiler_params=_params(("parallel", "arbitrary")),
    )(q, k, v, g, beta)


def gdn_bwd(q, k, v, g, beta, states, do, name):
    s = q.shape[0]
    d, hg = GDN_DK, GDN_HEAD_GROUP
    seq, st, tile, n_tiles = _gdn_specs(s, True)
    tile_chunks = tile // GDN_CHUNK

    def body(q_ref, k_ref, v_ref, g_ref, b_ref, st_ref, do_ref, dq_ref, dk_ref, dv_ref, dg_ref, db_ref, dstate_scr):
        @pl.when(pl.program_id(1) == 0)
        def _():
            dstate_scr[...] = jnp.zeros_like(dstate_scr)

        def step(it, carry):
            ci = tile_chunks - 1 - it
            rows = pl.ds(pl.multiple_of(ci * GDN_CHUNK, GDN_CHUNK), GDN_CHUNK)
            for h in range(hg):
                cols = slice(h * d, (h + 1) * d)
                prim = (q_ref[rows, cols], k_ref[rows, cols], v_ref[rows, cols], g_ref[rows, cols],
                        b_ref[rows, cols], st_ref[h, ci])
                _, vjp = jax.vjp(functools.partial(_gdn_chunk, _DIFF), *prim)
                dq, dk, dv, dg, db, dstate_in = vjp((do_ref[rows, cols], dstate_scr[h]))
                dq_ref[rows, cols] = dq
                dk_ref[rows, cols] = dk
                dv_ref[rows, cols] = dv
                dg_ref[rows, cols] = dg
                db_ref[rows, cols] = db
                dstate_scr[h] = dstate_in
            return carry

        lax.fori_loop(0, tile_chunks, step, 0)

    return pl.pallas_call(
        body, name=name,
        out_shape=[jax.ShapeDtypeStruct(q.shape, F32)] * 5,
        grid=(GDN_HEADS // hg, n_tiles), in_specs=[seq] * 5 + [st, seq], out_specs=[seq] * 5,
        scratch_shapes=[pltpu.VMEM((hg, d, d), F32)],
        compiler_params=_params(("parallel", "arbitrary")),
    )(q, k, v, g, beta, states, do)


def make_gdn(name):
    @jax.custom_vjp
    def op(q, k, v, g, beta):
        return gdn_fwd(q, k, v, g, beta, name + "_f")[0]

    def fwd(q, k, v, g, beta):
        o, states = gdn_fwd(q, k, v, g, beta, name + "_f")
        return o, (q, k, v, g, beta, states)

    def bwd(saved, do):
        return tuple(gdn_bwd(*saved, do, name + "_b"))

    op.defvjp(fwd, bwd)
    return op


def loss_head(x, g, target, name):
    s, d = x.shape
    tm = min(256, s)

    def body(x_ref, g_ref, t_ref, loss_ref, dx_ref, dg_ref):
        tgt = t_ref[...]

        def f(xv, gv):
            err = _rms(xv, gv) - tgt
            per_row = jnp.mean(err * err, axis=-1, keepdims=True)
            return 0.5 * jnp.sum(per_row, axis=0, keepdims=True)

        val, vjp = jax.vjp(f, x_ref[...], g_ref[...])
        dx, dg = vjp(jnp.ones((1, 1), F32))
        dx_ref[...] = dx
        first = pl.program_id(0) == 0

        @pl.when(first)
        def _():
            dg_ref[...] = dg
            loss_ref[...] = jnp.broadcast_to(val, loss_ref.shape)

        @pl.when(jnp.logical_not(first))
        def _():
            dg_ref[...] += dg
            loss_ref[...] += jnp.broadcast_to(val, loss_ref.shape)

    row = pl.BlockSpec((tm, d), lambda i: (i, 0))
    vec = pl.BlockSpec((1, d), lambda i: (0, 0))
    return pl.pallas_call(
        body, name=name,
        out_shape=[jax.ShapeDtypeStruct((1, LANES), F32), jax.ShapeDtypeStruct((s, d), F32),
                   jax.ShapeDtypeStruct((1, d), F32)],
        grid=(s // tm,), in_specs=[row, vec, row],
        out_specs=[pl.BlockSpec((1, LANES), lambda i: (0, 0)), row, vec],
        compiler_params=_params(("arbitrary",)),
    )(x, g, target)


def adamw(g8, w, m, v, layer, prev, name):
    n_layers, rows, width = w.shape
    tr = _pick(rows, (256, 128, 64, 32, 16, 8))

    def body(g_ref, w_ref, m_ref, v_ref, *rest):
        go_ref, d_ref, mo_ref, vo_ref = rest[-4:]
        g = g_ref[0].astype(F32)
        for p in range(1, N_DEV):
            g = g + g_ref[p].astype(F32)
        m_new = ADAM_B1 * m_ref[...] + (1.0 - ADAM_B1) * g
        v_new = ADAM_B2 * v_ref[...] + (1.0 - ADAM_B2) * (g * g)
        m_hat = m_new / (1.0 - ADAM_B1 ** ADAM_STEP)
        v_hat = v_new / (1.0 - ADAM_B2 ** ADAM_STEP)
        go_ref[...] = g
        d_ref[...] = -ADAM_LR * (m_hat / (jnp.sqrt(v_hat) + ADAM_EPS) + ADAM_WD * w_ref[...])
        mo_ref[...] = m_new
        vo_ref[...] = v_new

    blk = pl.BlockSpec((None, tr, width), lambda i: (layer, i, 0))
    carried = list(prev) if prev is not None else []
    return pl.pallas_call(
        body, name=name, out_shape=[jax.ShapeDtypeStruct((n_layers, rows, width), F32)] * 4,
        grid=(rows // tr,),
        in_specs=[pl.BlockSpec((N_DEV, tr, width), lambda i: (0, i, 0)), blk, blk, blk]
        + [pl.BlockSpec(memory_space=pl.ANY)] * len(carried),
        out_specs=[blk] * 4,
        input_output_aliases={4 + j: j for j in range(len(carried))},
        compiler_params=_params(("parallel",)),
    )(g8, w, m, v, *carried)


def exchange(items, name):
    n_items = len(items)
    hbm = pl.BlockSpec(memory_space=pltpu.HBM)
    out_shapes = []
    for mode, a in items:
        shape = (N_DEV,) + tuple(a.shape) if mode == "gather" else tuple(a.shape)
        out_shapes.append(jax.ShapeDtypeStruct(shape, a.dtype))

    def body(*refs):
        in_refs, out_refs = refs[:n_items], refs[n_items:2 * n_items]
        send_sems, recv_sems, local_sems = refs[2 * n_items:]
        x, y, c = lax.axis_index("x"), lax.axis_index("y"), lax.axis_index("c")
        me = 4 * x + 2 * y + c

        def src(k, p):
            return in_refs[k] if items[k][0] == "gather" else in_refs[k].at[p]

        local = [pltpu.make_async_copy(src(k, me), out_refs[k].at[me], local_sems.at[k])
                 for k in range(n_items)]
        for cp in local:
            cp.start()
        sends, recvs = [], []
        for r in range(1, N_DEV):
            px = (1 - x) if r & 4 else x
            py = (1 - y) if r & 2 else y
            pc = (1 - c) if r & 1 else c
            p = 4 * px + 2 * py + pc
            for k in range(n_items):
                sends.append(pltpu.make_async_remote_copy(
                    src_ref=src(k, p), dst_ref=out_refs[k].at[me],
                    send_sem=send_sems.at[k, r - 1], recv_sem=recv_sems.at[k, r - 1],
                    device_id=(px, py, pc), device_id_type=pl.DeviceIdType.MESH))
                recvs.append(pltpu.make_async_remote_copy(
                    src_ref=src(k, p), dst_ref=out_refs[k].at[p],
                    send_sem=send_sems.at[k, r - 1], recv_sem=recv_sems.at[k, r - 1],
                    device_id=(px, py, pc), device_id_type=pl.DeviceIdType.MESH))
        for cp in sends:
            cp.start()
        for cp in recvs:
            cp.wait_recv()
        for cp in sends:
            cp.wait_send()
        for cp in local:
            cp.wait()

    return pl.pallas_call(
        body, name=name, out_shape=out_shapes,
        in_specs=[hbm] * n_items, out_specs=[hbm] * n_items,
        scratch_shapes=[pltpu.SemaphoreType.DMA((n_items, N_DEV - 1)),
                        pltpu.SemaphoreType.DMA((n_items, N_DEV - 1)),
                        pltpu.SemaphoreType.DMA((n_items,))],
    )(*[a for _, a in items])


BIG = ["mla_w_in", "mla_w_uq", "mla_w_ukv", "mla_w_o", "gdn_w_in", "gdn_w_o", "sc_w_in", "sc_w_o",
       "xa_w_q", "xa_w_kv", "xa_w_o", "mlp_w1", "mlp_w2"]
TINY = [("mla_q_norm", 1), ("mla_kv_norm", 1), ("gdn_conv_w", 2), ("sc_conv_w", 2)]
REPL = ["gdn_a_log", "gdn_dt_bias", "gdn_o_norm", "norm_mix", "norm_mem", "norm_mlp", "mem_norm", "final_norm"]
WEIGHTS = ["mla_w_in", "mla_q_norm", "mla_kv_norm", "mla_w_uq", "mla_w_ukv", "mla_w_o", "gdn_w_in",
           "gdn_conv_w", "gdn_a_log", "gdn_dt_bias", "gdn_o_norm", "gdn_w_o", "sc_w_in", "sc_conv_w",
           "sc_w_o", "norm_mix", "norm_mem", "norm_mlp", "xa_w_q", "xa_w_kv", "xa_w_o", "mlp_w1",
           "mlp_w2", "mem_norm", "final_norm"]


def from_shards(a8, axis):
    a = jnp.moveaxis(a8, 0, axis)
    shp = a.shape
    return a.reshape(shp[:axis] + (shp[axis] * shp[axis + 1],) + shp[axis + 2:])


def pack_rows(flat_list, width, row_mult):
    lead = flat_list[0].shape[:-1]
    total = sum(a.shape[-1] for a in flat_list)
    rows = -(-total // width)
    rows = -(-rows // row_mult) * row_mult
    pad = rows * width - total
    parts = list(flat_list)
    if pad:
        parts.append(jnp.zeros(lead + (pad,), flat_list[0].dtype))
    return jnp.concatenate(parts, axis=-1).reshape(lead + (rows, width))


def unpack_rows(packed, shapes):
    lead = packed.shape[:-2]
    flat = packed.reshape(lead + (-1,))
    out, off = [], 0
    for shp in shapes:
        n = math.prod(shp)
        out.append(flat[..., off:off + n].reshape(lead + tuple(shp)))
        off += n
    return out


def _swap_halves(w):
    half = w.shape[-1] // 2
    return jnp.concatenate([w[..., half:], w[..., :half]], axis=-1)


def _pad_last(w, n):
    return jnp.pad(w, [(0, 0)] * (w.ndim - 1) + [(0, n - w.shape[-1])])


def rms_op(name, rows, d, out_dtype, tm=512):
    return make_tile_op(fn_rms, name, ["row", "par"], [True, True], [("row", d, out_dtype)], rows, min(tm, rows))


def _unblock(w8):
    return jnp.transpose(w8, (1, 0, 2)).reshape(w8.shape[1], -1)


def trunk(wts, p, x, mem, cos, sin):
    s, d = x.shape
    m_len = mem.shape[0]
    hd = MLA_NOPE

    def rows_of(name, layer):
        w8 = wts[name, layer]
        return w8.reshape(-1, w8.shape[-1])

    zeros = jnp.zeros((s, hd - MLA_ROPE), F32)
    rope_c = jnp.concatenate([cos, cos, zeros], axis=-1)
    rope_s = jnp.concatenate([-sin, sin, zeros], axis=-1)

    mem_n = rms_op("rms_memory", m_len, d, BF16)(mem, p["mem_norm"].reshape(1, d))[0]

    for i in range(DEPTH):
        j, kind = i // N_MIXERS, i % N_MIXERS
        tag = f"l{i}"
        h = rms_op(tag + "_rms_mix", s, d, BF16)(x, p["norm_mix"][i].reshape(1, d))[0]
        if kind == 0:
            w_in = rows_of("mla_w_in", j)
            w_cq = w_in[:, :MLA_Q_RANK]
            w_ckv = w_in[:, MLA_Q_RANK:MLA_Q_RANK + MLA_KV_RANK]
            w_kr = w_in[:, MLA_Q_RANK + MLA_KV_RANK:]
            w_z = jnp.concatenate([w_cq, w_ckv, _pad_last(w_kr, hd), _pad_last(_swap_halves(w_kr), hd)], axis=-1)
            z = make_mm(tag + "_mla_in", F32)(h, w_z)
            c_q, c_kv = z[:, :MLA_Q_RANK], z[:, MLA_Q_RANK:MLA_Q_RANK + MLA_KV_RANK]
            kr_raw, kr_swp = z[:, -2 * hd:-hd], z[:, -hd:]
            c_qn = rms_op(tag + "_rms_q", s, MLA_Q_RANK, BF16)(c_q, p["mla_q_norm"][j].reshape(1, -1))[0]
            c_kvn = rms_op(tag + "_rms_kv", s, MLA_KV_RANK, BF16)(c_kv, p["mla_kv_norm"][j].reshape(1, -1))[0]
            w_uq8 = wts["mla_w_uq", j]
            w_qn = _unblock(w_uq8[:, :, :MLA_NOPE])
            w_qr = w_uq8[:, :, MLA_NOPE:]
            w_qr_p = _unblock(_pad_last(w_qr, hd))
            w_qr_s = _unblock(_pad_last(_swap_halves(w_qr), hd))
            q_nope = make_mm(tag + "_mla_uq_n", BF16)(c_qn, w_qn)
            q_raw = make_mm(tag + "_mla_uq_r", F32)(c_qn, w_qr_p)
            q_swp = make_mm(tag + "_mla_uq_s", F32)(c_qn, w_qr_s)
            nq = MLA_HEADS * hd
            kv = make_mm(tag + "_mla_ukv", BF16, blocked=True)(c_kvn, wts["mla_w_ukv", j])
            heads_row = ("row", hd, MLA_HEADS)
            q_rope = make_tile_op(fn_rope, tag + "_rope_q", [heads_row, heads_row, "row", "row"],
                                  [True, True, False, False], [(heads_row, nq, F32)], s, 512)(
                q_raw, q_swp, rope_c, rope_s)[0]
            k_rope = make_tile_op(fn_rope, tag + "_rope_k", ["row", "row", "row", "row"],
                                  [True, True, False, False], [("row", hd, F32)], s, 1024)(
                kr_raw, kr_swp, rope_c, rope_s)[0]
            o = make_tile_op(fn_mla_attn, tag + "_mla_attn",
                             [("rowh", hd), ("rowh", hd), ("parh", hd, 2), "par"],
                             [True] * 4, [(("rowh", hd), nq, BF16)], s, 256, MLA_HEADS)(
                q_nope, q_rope, kv, k_rope)[0]
            x = make_mm(tag + "_mla_o", F32, with_res=True)(x, o, rows_of("mla_w_o", j))
        elif kind == 1:
            ng = GDN_HEADS * GDN_DK
            w_in = _unblock(wts["gdn_w_in", j])
            cw = p["gdn_conv_w"][j]
            conv_out = []
            for part, nm in enumerate(("q", "k", "v")):
                cols = slice(part * ng, (part + 1) * ng)
                pre = make_mm(f"{tag}_gdn_in_{nm}", F32)(h, w_in[:, cols])
                conv_out.append(make_conv(f"{tag}_gdn_conv_{nm}")(pre, cw[:, cols]))
            gate = make_mm(tag + "_gdn_in_g", F32)(h, w_in[:, 3 * ng:4 * ng])
            ba = make_mm(tag + "_gdn_in_ba", F32)(h, _pad_last(w_in[:, 4 * ng:], LANES))
            heads_row = ("row", GDN_DK, GDN_HEADS)
            q, k, v = make_tile_op(fn_gdn_prep, tag + "_gdn_prep", [heads_row] * 3, [True] * 3,
                                   [(heads_row, ng, F32)] * 3, s, 512)(*conv_out)
            alog = jnp.pad(p["gdn_a_log"][j].reshape(1, -1), ((0, 0), (GDN_HEADS, LANES - 2 * GDN_HEADS)))
            dtb = jnp.pad(p["gdn_dt_bias"][j].reshape(1, -1), ((0, 0), (GDN_HEADS, LANES - 2 * GDN_HEADS)))
            beta_b, g_b = make_tile_op(fn_gdn_gates, tag + "_gdn_gates", ["row", "par", "par"], [True] * 3,
                                       [("row", ng, F32)] * 2, s, 512)(ba, alog, dtb)
            o = make_gdn(tag + "_gdn_core")(q, k, v, g_b, beta_b)
            o = make_tile_op(fn_gdn_out, tag + "_gdn_out", [heads_row, heads_row, "par"],
                             [True] * 3, [(heads_row, ng, BF16)], s, 512)(
                o, gate, p["gdn_o_norm"][j].reshape(1, -1))[0]
            x = make_mm(tag + "_gdn_o", F32, with_res=True)(x, o, rows_of("gdn_w_o", j))
        else:
            w_in = _unblock(wts["sc_w_in", j])
            b_gate = make_mm(tag + "_sc_in_b", F32)(h, w_in[:, :d])
            c_gate = make_mm(tag + "_sc_in_c", F32)(h, w_in[:, d:2 * d])
            u = make_mm(tag + "_sc_in_u", F32)(h, w_in[:, 2 * d:])
            cu = make_tile_op(fn_mul, tag + "_sc_cu", ["row", "row"], [True, True], [("row", d, F32)], s, 512)(
                c_gate, u)[0]
            cv = make_conv(tag + "_sc_conv")(cu, p["sc_conv_w"][j])
            yv = make_tile_op(fn_mul, tag + "_sc_gate", ["row", "row"], [True, True], [("row", d, BF16)], s, 512)(
                b_gate, cv)[0]
            x = make_mm(tag + "_sc_o", F32, with_res=True)(x, yv, rows_of("sc_w_o", j))

        hx = rms_op(tag + "_rms_mem", s, d, BF16)(x, p["norm_mem"][i].reshape(1, d))[0]
        q = make_mm(tag + "_xa_q", BF16)(hx, rows_of("xa_w_q", i))
        kv = make_mm(tag + "_xa_kv", BF16, blocked=True)(mem_n, wts["xa_w_kv", i])
        o = make_tile_op(fn_xattn, tag + "_xattn",
                         [("rowh", X_HEAD_DIM), ("parh", X_HEAD_DIM), ("parh", X_HEAD_DIM)], [True] * 3,
                         [(("rowh", X_HEAD_DIM), d, BF16)], s, 1024, X_HEADS)(q, kv[:, :d], kv[:, d:])[0]
        x = make_mm(tag + "_xa_o", F32, with_res=True)(x, o, rows_of("xa_w_o", i))

        hm = rms_op(tag + "_rms_mlp", s, d, BF16)(x, p["norm_mlp"][i].reshape(1, d))[0]
        a = make_mm(tag + "_mlp_1", BF16, blocked=True)(hm, wts["mlp_w1", i])
        bsq = make_tile_op(fn_relu2, tag + "_relu2", ["row"], [True], [("row", a.shape[1], BF16)], s, 256)(a)[0]
        x = make_mm(tag + "_mlp_2", F32, with_res=True)(x, bsq, rows_of("mlp_w2", i))
    return x


def kernel(x, mem, positions, mla_w_in, mla_q_norm, mla_kv_norm, mla_w_uq, mla_w_ukv, mla_w_o, gdn_w_in, gdn_conv_w, gdn_a_log, gdn_dt_bias, gdn_o_norm, gdn_w_o, sc_w_in, sc_conv_w, sc_w_o, norm_mix, norm_mem, norm_mlp, xa_w_q, xa_w_kv, xa_w_o, mlp_w1, mlp_w2, mem_norm, final_norm, loss_target, m_mla_w_in, m_mla_q_norm, m_mla_kv_norm, m_mla_w_uq, m_mla_w_ukv, m_mla_w_o, m_gdn_w_in, m_gdn_conv_w, m_gdn_a_log, m_gdn_dt_bias, m_gdn_o_norm, m_gdn_w_o, m_sc_w_in, m_sc_conv_w, m_sc_w_o, m_norm_mix, m_norm_mem, m_norm_mlp, m_xa_w_q, m_xa_w_kv, m_xa_w_o, m_mlp_w1, m_mlp_w2, m_mem_norm, m_final_norm, v_mla_w_in, v_mla_q_norm, v_mla_kv_norm, v_mla_w_uq, v_mla_w_ukv, v_mla_w_o, v_gdn_w_in, v_gdn_conv_w, v_gdn_a_log, v_gdn_dt_bias, v_gdn_o_norm, v_gdn_w_o, v_sc_w_in, v_sc_conv_w, v_sc_w_o, v_norm_mix, v_norm_mem, v_norm_mlp, v_xa_w_q, v_xa_w_kv, v_xa_w_o, v_mlp_w1, v_mlp_w2, v_mem_norm, v_final_norm):
    args = locals()
    w_loc = {n: args[n] for n in WEIGHTS}
    m_loc = {n: args["m_" + n] for n in WEIGHTS}
    v_loc = {n: args["v_" + n] for n in WEIGHTS}
    me = 4 * lax.axis_index("x") + 2 * lax.axis_index("y") + lax.axis_index("c")

    units = [(n, layer) for n in BIG for layer in range(w_loc[n].shape[0])]
    w16 = {n: w_loc[n].astype(BF16) for n in BIG}
    tiny_pack = pack_rows([w_loc[n].reshape(-1) for n, _ in TINY], LANES, 8)
    gathered = exchange([("gather", w16[n][layer]) for n, layer in units] + [("gather", tiny_pack)],
                        "gather_weights")
    wts = dict(zip(units, gathered[:-1]))
    params = {}
    for (n, ax), a8 in zip(TINY, unpack_rows(gathered[-1], [w_loc[n].shape for n, _ in TINY])):
        params[n] = from_shards(a8, ax)
    for n in REPL:
        params[n] = w_loc[n]

    inv_freq = ROPE_THETA ** (-jnp.arange(0, MLA_ROPE, 2, dtype=F32) / MLA_ROPE)
    ang = positions[0].astype(F32)[:, None] * inv_freq
    cos, sin = jnp.cos(ang), jnp.sin(ang)
    trunk_params = {n: a for n, a in params.items() if n != "final_norm"}
    x_out, vjp = jax.vjp(lambda w_, p_, xx: trunk(w_, p_, xx, mem[0], cos, sin), wts, trunk_params, x[0])
    loss_vec, dx_out, d_final = loss_head(x_out, params["final_norm"].reshape(1, -1), loss_target[0], "loss_head")
    g_wts, grads, grad_x = vjp(dx_out)
    grads = dict(grads)
    grads["final_norm"] = d_final.reshape(-1)

    small_names = [n for n, _ in TINY] + REPL
    small_g = pack_rows([loss_vec[0, :1]] + [grads[n].astype(F32).reshape(-1) for n in small_names], PACK_W, 8)
    received = exchange([("scatter", g_wts[u]) for u in units] + [("gather", small_g)], "exchange_grads")
    g_recv = dict(zip(units, received[:-1]))
    small_recv = received[-1]

    res = {}
    for n in BIG:
        outs = None
        for layer in range(w_loc[n].shape[0]):
            outs = adamw(g_recv[n, layer], w_loc[n], m_loc[n], v_loc[n], layer, outs, f"adamw_{n}_{layer}")
        for kind, a in zip(("grad", "delta", "m", "v"), outs):
            res[(kind, n)] = a

    def full_small(d):
        parts = [jnp.zeros((1,), F32)]
        for n, ax in TINY:
            full_shape = params[n].shape
            start = [0] * len(full_shape)
            start[ax] = me * d[n].shape[ax]
            parts.append(lax.dynamic_update_slice(jnp.zeros(full_shape, F32), d[n], start).reshape(-1))
        parts += [d[n].reshape(-1) for n in REPL]
        return pack_rows(parts, PACK_W, 8)

    outs_small = adamw(small_recv, full_small(w_loc)[None], full_small(m_loc)[None], full_small(v_loc)[None],
                       0, None, "adamw_small")
    small_shapes = [(1,)] + [params[n].shape for n, _ in TINY] + [w_loc[n].shape for n in REPL]
    loss = None
    for kind, packed in zip(("grad", "delta", "m", "v"), outs_small):
        parts = unpack_rows(packed[0], small_shapes)
        if kind == "grad":
            loss = parts[0][0]
        for (n, ax), a in zip(TINY, parts[1:1 + len(TINY)]):
            start = [0] * a.ndim
            start[ax] = me * w_loc[n].shape[ax]
            res[(kind, n)] = lax.dynamic_slice(a, start, w_loc[n].shape)
        for n, a in zip(REPL, parts[1 + len(TINY):]):
            res[(kind, n)] = a

    out = [loss, grad_x[None]]
    for kind in ("grad", "delta", "m", "v"):
        out += [res[(kind, n)] for n in WEIGHTS]
    return tuple(out)
```

```python
import functools
import math

import jax
import jax.numpy as jnp
from jax import lax
from jax.experimental import pallas as pl
from jax.experimental.pallas import tpu as pltpu

F32 = jnp.float32
BF16 = jnp.bfloat16

N_DEV = 8
LANES = 128
EPS = 1e-6
ROPE_THETA = 10000.0
MLA_HEADS, MLA_NOPE, MLA_ROPE, MLA_V = 8, 128, 64, 128
MLA_Q_RANK, MLA_KV_RANK = 384, 256
GDN_HEADS, GDN_DK, GDN_CONV, GDN_CHUNK = 8, 128, 4, 64
X_HEADS, X_HEAD_DIM = 4, 256
DEPTH, N_MIXERS = 4, 3
ADAM_LR, ADAM_B1, ADAM_B2, ADAM_EPS, ADAM_WD, ADAM_STEP = 0.001, 0.9, 0.999, 1e-08, 0.01, 10
NEG_BIG = -1e30
PACK_W = 1024


_NN = ((1,), (0,))
_NT = ((1,), (1,))
_TN = ((0,), (0,))


def _dot(a, b, dims):
    return lax.dot_general(a, b, (dims, ((), ())), preferred_element_type=F32)


def _hi_lo(x):
    hi = x.astype(BF16)
    return hi, (x - hi.astype(F32)).astype(BF16)


def _split3(x):
    hi = x.astype(BF16)
    r = x - hi.astype(F32)
    mid = r.astype(BF16)
    return hi, mid, (r - mid.astype(F32)).astype(BF16)


def _dg(a, b, dims, prec):
    if prec == "h":
        return lax.dot_general(a, b, (dims, ((), ())), precision=lax.Precision.HIGHEST,
                               preferred_element_type=F32)
    if prec == "m":
        a_hi, a_lo = _hi_lo(a)
        b_hi, b_lo = _hi_lo(b)
        return _dot(a_hi, b_hi, dims) + _dot(a_hi, b_lo, dims) + _dot(a_lo, b_hi, dims)
    return _dot(a.astype(BF16), b.astype(BF16), dims)


def _dg_sel(sel, x, dims, sel_first):
    s16 = sel.astype(BF16)
    parts = [(_dot(s16, piece, dims) if sel_first else _dot(piece, s16, dims)) for piece in _split3(x)]
    return parts[0] + parts[1] + parts[2]


class _Ops:
    def __init__(self, prec, differentiable):
        def nn(a, b):
            return _dg(a, b, _NN, prec)

        def nt(a, b):
            return _dg(a, b, _NT, prec)

        def tn(a, b):
            return _dg(a, b, _TN, prec)

        if differentiable:
            dnn = jax.custom_vjp(nn)
            dnn.defvjp(lambda a, b: (nn(a, b), (a, b)), lambda r, g: (nt(g, r[1]), tn(r[0], g)))
            dnt = jax.custom_vjp(nt)
            dnt.defvjp(lambda a, b: (nt(a, b), (a, b)), lambda r, g: (nn(g, r[1]), tn(g, r[0])))
            dtn = jax.custom_vjp(tn)
            dtn.defvjp(lambda a, b: (tn(a, b), (a, b)), lambda r, g: (nt(r[1], g), nn(r[0], g)))
            nn, nt, tn = dnn, dnt, dtn
        self.nn, self.nt, self.tn = nn, nt, tn


class _SelOps:
    def __init__(self, differentiable):
        def sel_nn(sel, x):
            return _dg_sel(sel, x, _NN, True)

        def sel_nt(sel, x):
            return _dg_sel(sel, x, _NT, True)

        if differentiable:
            dnn = jax.custom_vjp(sel_nn)
            dnn.defvjp(lambda s, x: (sel_nn(s, x), s),
                       lambda s, g: (jnp.zeros_like(s), _dg_sel(s, g, _TN, True)))
            dnt = jax.custom_vjp(sel_nt)
            dnt.defvjp(lambda s, x: (sel_nt(s, x), s),
                       lambda s, g: (jnp.zeros_like(s), _dg_sel(s, g, _TN, False)))
            sel_nn, sel_nt = dnn, dnt
        self.sel_nn, self.sel_nt = sel_nn, sel_nt


class _OpSet:
    def __init__(self, differentiable):
        self.b = _Ops("b", differentiable)
        self.m = _Ops("m", differentiable)
        self.h = _Ops("h", differentiable)
        self.s = _SelOps(differentiable)


_PLAIN = _OpSet(False)
_DIFF = _OpSet(True)


def _params(sem):
    return pltpu.CompilerParams(dimension_semantics=sem)


def _pick(n, cands):
    for c in cands:
        if n % c == 0:
            return c
    return n


def matmul(a, b, form, out_dtype, name, res=None, blocked=False):
    if form == "nn":
        m, k = a.shape
        k2, n = (b.shape[1], N_DEV * b.shape[2]) if blocked else b.shape
    elif form == "nt":
        m, k = a.shape
        n, k2 = (b.shape[1], N_DEV * b.shape[2]) if blocked else b.shape
    else:
        (k, m), (k2, n) = a.shape, b.shape
    assert k == k2, (a.shape, b.shape, form)
    tm = m if m <= 512 else _pick(m, (512, 384, 256, 128))
    tn = n if n <= 1024 else _pick(n, (1024, 768, 512, 384, 256, 128))
    tk = k if k <= 1024 else _pick(k, (1024, 768, 512, 384, 256, 128))
    cb = nb = 1
    if blocked:
        cb = (k if form == "nt" else n) // N_DEV
        nb = _pick(N_DEV, tuple(c for c in (8, 4, 2, 1) if c * cb <= 1024))
        if form == "nt":
            tk = nb * cb
        else:
            tn = nb * cb
    nk = k // tk
    dims = {"nn": _NN, "nt": _NT, "tn": _TN}[form]

    a_spec = {"nn": pl.BlockSpec((tm, tk), lambda i, j, kk: (i, kk)),
              "nt": pl.BlockSpec((tm, tk), lambda i, j, kk: (i, kk)),
              "tn": pl.BlockSpec((tk, tm), lambda i, j, kk: (kk, i))}[form]
    if blocked and form == "nn":
        b_spec = pl.BlockSpec((nb, tk, cb), lambda i, j, kk: (j, kk, 0))
    elif blocked and form == "nt":
        b_spec = pl.BlockSpec((nb, tn, cb), lambda i, j, kk: (kk, j, 0))
    else:
        b_spec = {"nn": pl.BlockSpec((tk, tn), lambda i, j, kk: (kk, j)),
                  "nt": pl.BlockSpec((tn, tk), lambda i, j, kk: (j, kk)),
                  "tn": pl.BlockSpec((tk, tn), lambda i, j, kk: (kk, j))}[form]
    c_spec = pl.BlockSpec((tm, tn), lambda i, j, kk: (i, j))
    out_shape = jax.ShapeDtypeStruct((m, n), out_dtype)
    o_spec = c_spec
    blocked_out = blocked and form == "tn"
    if blocked_out:
        out_shape = jax.ShapeDtypeStruct((N_DEV, m, cb), out_dtype)
        o_spec = pl.BlockSpec((nb, tm, cb), lambda i, j, kk: (j, i, 0))
    has_res = res is not None

    def body(*refs):
        a_ref, b_ref = refs[0], refs[1]
        r_ref = refs[2] if has_res else None
        o_ref = refs[3] if has_res else refs[2]
        a_val = a_ref[...].astype(BF16)
        if blocked and form == "nn":
            part = jnp.concatenate([_dot(a_val, b_ref[t].astype(BF16), dims) for t in range(nb)], axis=-1)
        elif blocked and form == "nt":
            part = _dot(a_val[:, :cb], b_ref[0].astype(BF16), dims)
            for t in range(1, nb):
                part = part + _dot(a_val[:, t * cb:(t + 1) * cb], b_ref[t].astype(BF16), dims)
        else:
            part = _dot(a_val, b_ref[...].astype(BF16), dims)

        def finish(acc):
            if has_res:
                acc = acc + r_ref[...].astype(F32)
            if blocked_out:
                for t in range(nb):
                    o_ref[t] = acc[:, t * cb:(t + 1) * cb].astype(out_dtype)
            else:
                o_ref[...] = acc.astype(out_dtype)

        if nk == 1:
            finish(part)
        else:
            acc_ref = refs[-1]
            kk = pl.program_id(2)

            @pl.when(kk == 0)
            def _():
                acc_ref[...] = part

            @pl.when(jnp.logical_and(kk > 0, kk < nk - 1))
            def _():
                acc_ref[...] += part

            @pl.when(kk == nk - 1)
            def _():
                finish(acc_ref[...] + part)

    in_specs = [a_spec, b_spec] + ([c_spec] if has_res else [])
    args = (a, b) + ((res,) if has_res else ())
    return pl.pallas_call(
        body, name=name,
        out_shape=out_shape,
        grid=(m // tm, n // tn, nk),
        in_specs=in_specs, out_specs=o_spec,
        scratch_shapes=[pltpu.VMEM((tm, tn), F32)] if nk > 1 else [],
        compiler_params=_params(("parallel", "parallel", "arbitrary")),
    )(*args)


def make_mm(name, out_dtype, with_res=False, blocked=False):
    def bwd_mm(a, w, g):
        da = matmul(g, w, "nt", a.dtype, name + "_da", blocked=blocked)
        dw = matmul(a, g, "tn", w.dtype, name + "_dw", blocked=blocked)
        return da, dw

    if with_res:
        @jax.custom_vjp
        def op(res, a, w):
            return matmul(a, w, "nn", out_dtype, name + "_f", res=res, blocked=blocked)

        def fwd(res, a, w):
            return op(res, a, w), (a, w)

        def bwd(saved, g):
            return (g,) + bwd_mm(*saved, g)
    else:
        @jax.custom_vjp
        def op(a, w):
            return matmul(a, w, "nn", out_dtype, name + "_f", blocked=blocked)

        def fwd(a, w):
            return op(a, w), (a, w)

        def bwd(saved, g):
            return bwd_mm(*saved, g)
    op.defvjp(fwd, bwd)
    return op


def _kind(k):
    if isinstance(k, str):
        return k, None, 1
    return k[0], k[1], (k[2] if len(k) > 2 else 1)


def _tile_spec(kind, shape, tm, heads):
    k, d, ns = _kind(kind)
    if k == "row":
        return pl.BlockSpec((tm, shape[1]), (lambda h, i: (i, 0)) if heads else (lambda i: (i, 0)))
    if k == "par":
        return pl.BlockSpec(tuple(shape), (lambda h, i: (0, 0)) if heads else (lambda i: (0, 0)))
    if k == "rowh":
        return pl.BlockSpec((tm, d * ns), lambda h, i: (i, h))
    if k == "parh":
        return pl.BlockSpec((shape[0], d * ns), lambda h, i: (0, h))
    raise ValueError(kind)


def _tile_grid(rows, tm, heads):
    n_rows = rows // tm
    return ((heads, n_rows) if heads else (n_rows,)), (1 if heads else 0)


def _split_vals(kinds, refs):
    vals, counts = [], []
    for kind, r in zip(kinds, refs):
        _, d, ns = _kind(kind)
        v = r[...].astype(F32)
        vals += [v] if ns == 1 else [v[:, p * d:(p + 1) * d] for p in range(ns)]
        counts.append(ns)
    return vals, counts


def tile_fwd(fn, name, kinds, args, outs, rows, tm, heads):
    grid, row_axis = _tile_grid(rows, tm, heads)
    n_in = len(args)
    out_shapes = [jax.ShapeDtypeStruct((rows, w), dt) for (_, w, dt) in outs]

    def body(*refs):
        vals, _ = _split_vals(kinds, refs[:n_in])
        row0 = pl.program_id(row_axis) * tm
        res = list(fn(_PLAIN, row0, *vals))
        for o_ref, (k, _, _) in zip(refs[n_in:], outs):
            pieces = [res.pop(0) for _ in range(_kind(k)[2])]
            v = pieces[0] if len(pieces) == 1 else jnp.concatenate(pieces, axis=-1)
            o_ref[...] = v.astype(o_ref.dtype)

    return pl.pallas_call(
        body, name=name, out_shape=out_shapes, grid=grid,
        in_specs=[_tile_spec(k, a.shape, tm, heads) for k, a in zip(kinds, args)],
        out_specs=[_tile_spec(k, (rows, w), tm, heads) for (k, w, _) in outs],
        compiler_params=_params(("arbitrary",) * len(grid)),
    )(*args)


def tile_bwd(fn, name, kinds, args, diff, outs, cts, rows, tm, heads):
    grid, row_axis = _tile_grid(rows, tm, heads)
    n_in, n_ct = len(args), len(cts)
    diff_idx = [i for i, d in enumerate(diff) if d]
    g_shapes, g_specs = [], []
    for i in diff_idx:
        k = _kind(kinds[i])[0]
        dt = args[i].dtype if k in ("row", "rowh") else F32
        g_shapes.append(jax.ShapeDtypeStruct(args[i].shape, dt))
        g_specs.append(_tile_spec(kinds[i], args[i].shape, tm, heads))

    def body(*refs):
        in_refs, ct_refs, g_refs = refs[:n_in], refs[n_in:n_in + n_ct], refs[n_in + n_ct:]
        vals, counts = _split_vals(kinds, in_refs)
        first_piece = [sum(counts[:i]) for i in range(n_in)]
        flat_diff = [first_piece[i] + p for i in diff_idx for p in range(counts[i])]
        row_id = pl.program_id(row_axis)
        row0 = row_id * tm

        def f(*dvals):
            full = list(vals)
            for i, dv in zip(flat_diff, dvals):
                full[i] = dv
            return tuple(fn(_DIFF, row0, *full))

        _, vjp = jax.vjp(f, *[vals[i] for i in flat_diff])
        ct_vals, _ = _split_vals([k for (k, _, _) in outs], ct_refs)
        flat_grads = list(vjp(tuple(ct_vals)))
        for g_ref, i in zip(g_refs, diff_idx):
            pieces = [flat_grads.pop(0) for _ in range(counts[i])]
            g = pieces[0] if len(pieces) == 1 else jnp.concatenate(pieces, axis=-1)
            k = _kind(kinds[i])[0]
            if k in ("row", "rowh"):
                g_ref[...] = g.astype(g_ref.dtype)
            else:
                first = row_id == 0
                if heads and k == "par":
                    first = jnp.logical_and(first, pl.program_id(0) == 0)

                @pl.when(first)
                def _(g_ref=g_ref, g=g):
                    g_ref[...] = g

                @pl.when(jnp.logical_not(first))
                def _(g_ref=g_ref, g=g):
                    g_ref[...] += g

    return pl.pallas_call(
        body, name=name, out_shape=g_shapes, grid=grid,
        in_specs=[_tile_spec(k, a.shape, tm, heads) for k, a in zip(kinds, args)]
        + [_tile_spec(k, (rows, w), tm, heads) for (k, w, _) in outs],
        out_specs=g_specs,
        compiler_params=_params(("arbitrary",) * len(grid)),
    )(*args, *cts)


def make_tile_op(fn, name, kinds, diff, outs, rows, tm, heads=0):
    tm = min(tm, rows)

    @jax.custom_vjp
    def op(*args):
        return tuple(tile_fwd(fn, name + "_f", kinds, args, outs, rows, tm, heads))

    def fwd(*args):
        return op(*args), args

    def bwd(args, cts):
        grads = tile_bwd(fn, name + "_b", kinds, args, diff, outs, cts, rows, tm, heads)
        it = iter(grads)
        res = []
        for a, d in zip(args, diff):
            res.append(next(it).astype(a.dtype) if d else None)
        return tuple(res)

    op.defvjp(fwd, bwd)
    return op


def _rms(x, g):
    return x * lax.rsqrt(jnp.mean(x * x, axis=-1, keepdims=True) + EPS) * g


def fn_rms(ops, row0, x, g):
    return (_rms(x, g),)


def fn_relu2(ops, row0, a):
    r = jnp.maximum(a, 0.0)
    return (r * r,)


def fn_mul(ops, row0, a, b):
    return (a * b,)


def fn_rope(ops, row0, *t):
    nh = (len(t) - 2) // 2
    c, s = t[-2], t[-1]
    return tuple(x * c + xs * s for x, xs in zip(t[:nh], t[nh:2 * nh]))


def _softmax(s):
    m = lax.stop_gradient(jnp.max(s, axis=-1, keepdims=True))
    e = jnp.exp(s - m)
    return e / jnp.sum(e, axis=-1, keepdims=True)


def fn_xattn(ops, row0, q, k, v):
    s = ops.b.nt(q, k) * (X_HEAD_DIM ** -0.5)
    return (ops.b.nn(_softmax(s), v),)


def fn_mla_attn(ops, row0, qn, qr, kn, v, kr):
    s = (ops.b.nt(qn, kn) + ops.b.nt(qr, kr)) * ((MLA_NOPE + MLA_ROPE) ** -0.5)
    rows = row0 + lax.broadcasted_iota(jnp.int32, s.shape, 0)
    cols = lax.broadcasted_iota(jnp.int32, s.shape, 1)
    s = jnp.where(rows >= cols, s, NEG_BIG)
    return (ops.b.nn(_softmax(s), v),)


def _silu(x):
    return x * jax.nn.sigmoid(x)


def fn_gdn_prep(ops, row0, *t):
    nh = len(t) // 3
    qs, ks, vs = [], [], []
    for qc, kc, vc in zip(t[:nh], t[nh:2 * nh], t[2 * nh:]):
        q, k = _silu(qc), _silu(kc)
        qs.append(q * lax.rsqrt(jnp.sum(q * q, -1, keepdims=True) + EPS) * (GDN_DK ** -0.5))
        ks.append(k * lax.rsqrt(jnp.sum(k * k, -1, keepdims=True) + EPS))
        vs.append(_silu(vc))
    return tuple(qs + ks + vs)


def fn_gdn_gates(ops, row0, ba, alog, dtb):
    width = GDN_HEADS * GDN_DK
    beta = jax.nn.sigmoid(ba)
    z = ba + dtb
    softplus = jnp.maximum(z, 0.0) + jnp.log1p(jnp.exp(-jnp.abs(z)))
    g = -jnp.exp(alog) * softplus
    r = lax.broadcasted_iota(jnp.int32, (LANES, width), 0)
    c = lax.broadcasted_iota(jnp.int32, (LANES, width), 1) // GDN_DK
    e_beta = (r == c).astype(F32)
    e_g = (r == c + GDN_HEADS).astype(F32)
    return ops.h.nn(beta, e_beta), ops.h.nn(g, e_g)


def fn_gdn_out(ops, row0, *t):
    nh = (len(t) - 1) // 2
    g = t[-1]
    return tuple(_rms(o, g) * _silu(gate) for o, gate in zip(t[:nh], t[nh:2 * nh]))


def _shift_down(x, d, t_idx):
    if d == 0:
        return x
    return jnp.where(t_idx >= d, pltpu.roll(x, d, axis=0), 0.0)


def _shift_up(x, d, t_idx):
    if d == 0:
        return x
    n = x.shape[0]
    return jnp.where(t_idx < n - d, pltpu.roll(x, n - d, axis=0), 0.0)


def conv_fwd(x, w, name):
    s, c = x.shape
    kw = w.shape[0]
    tc = _pick(c, (256, 128))

    def body(x_ref, w_ref, y_ref):
        xv = x_ref[...]
        t_idx = lax.broadcasted_iota(jnp.int32, xv.shape, 0)
        acc = jnp.zeros_like(xv)
        for j in range(kw):
            acc = acc + w_ref[j:j + 1, :] * _shift_down(xv, kw - 1 - j, t_idx)
        y_ref[...] = acc

    return pl.pallas_call(
        body, name=name, out_shape=jax.ShapeDtypeStruct((s, c), F32), grid=(c // tc,),
        in_specs=[pl.BlockSpec((s, tc), lambda i: (0, i)), pl.BlockSpec((kw, tc), lambda i: (0, i))],
        out_specs=pl.BlockSpec((s, tc), lambda i: (0, i)),
        compiler_params=_params(("parallel",)),
    )(x, w)


def conv_bwd(x, w, dy, name):
    s, c = x.shape
    kw = w.shape[0]
    tc = _pick(c, (256, 128))

    def body(x_ref, w_ref, dy_ref, dx_ref, dw_ref):
        xv, dyv = x_ref[...], dy_ref[...]
        t_idx = lax.broadcasted_iota(jnp.int32, xv.shape, 0)
        dx = jnp.zeros_like(xv)
        for j in range(kw):
            d = kw - 1 - j
            dx = dx + w_ref[j:j + 1, :] * _shift_up(dyv, d, t_idx)
            dw_ref[j:j + 1, :] = jnp.sum(dyv * _shift_down(xv, d, t_idx), axis=0, keepdims=True)
        dx_ref[...] = dx

    return pl.pallas_call(
        body, name=name,
        out_shape=[jax.ShapeDtypeStruct((s, c), F32), jax.ShapeDtypeStruct((kw, c), F32)],
        grid=(c // tc,),
        in_specs=[pl.BlockSpec((s, tc), lambda i: (0, i)), pl.BlockSpec((kw, tc), lambda i: (0, i)),
                  pl.BlockSpec((s, tc), lambda i: (0, i))],
        out_specs=[pl.BlockSpec((s, tc), lambda i: (0, i)), pl.BlockSpec((kw, tc), lambda i: (0, i))],
        compiler_params=_params(("parallel",)),
    )(x, w, dy)


def make_conv(name):
    @jax.custom_vjp
    def op(x, w):
        return conv_fwd(x, w, name + "_f")

    def fwd(x, w):
        return op(x, w), (x, w)

    def bwd(saved, dy):
        dx, dw = conv_bwd(saved[0], saved[1], dy, name + "_b")
        return dx, dw

    op.defvjp(fwd, bwd)
    return op


def _gdn_consts():
    c, d = GDN_CHUNK, GDN_DK
    i = lax.broadcasted_iota(jnp.int32, (c, c), 0)
    j = lax.broadcasted_iota(jnp.int32, (c, c), 1)
    tri = i >= j
    return dict(
        tri=tri, strict=i > j,
        tri_f=tri.astype(F32),
        eye=(i == j).astype(F32),
        lane0=(lax.broadcasted_iota(jnp.int32, (c, d), 1) == 0).astype(F32),
        last_row=(lax.broadcasted_iota(jnp.int32, (c, d), 0) == c - 1).astype(F32),
    )


def _gdn_chunk(ops, q, k, v, g, beta, state):
    b, m, sel = ops.b, ops.m, ops.s
    c, d = GDN_CHUNK, GDN_DK
    k_ = _gdn_consts()
    gc = sel.sel_nn(k_["tri_f"], g)
    col = jnp.broadcast_to(jnp.sum(gc * k_["lane0"], axis=1, keepdims=True), (c, c))
    row = sel.sel_nt(k_["lane0"], gc)
    decay = jnp.where(k_["tri"], jnp.exp(jnp.where(k_["tri"], col - row, 0.0)), 0.0)
    kb = k * beta
    mm_ = jnp.where(k_["strict"], b.nt(kb, k) * decay, 0.0)
    p = -mm_
    t = k_["eye"] + p
    for _ in range(int(math.log2(GDN_CHUNK)) - 1):
        p = m.nn(p, p)
        t = t + m.nn(t, p)
    egc = jnp.exp(gc)
    u = b.nn(t, v * beta)
    w = b.nn(t, kb * egc)
    attn = b.nt(q, k) * decay
    v_new = u - b.nn(w, state)
    o = b.nn(q * egc, state) + b.nn(attn, v_new)
    g_last = jnp.sum(gc * k_["last_row"], axis=0, keepdims=True)
    new_state = (state * jnp.exp(jnp.broadcast_to(g_last, (d, d)))
                 + b.tn(k * jnp.exp(jnp.broadcast_to(g_last, (c, d)) - gc), v_new))
    return o, new_state


GDN_HEAD_GROUP = 8
GDN_TILE_CHUNKS = 4


def _gdn_specs(s, reverse):
    d, hg = GDN_DK, GDN_HEAD_GROUP
    tile = min(GDN_TILE_CHUNKS * GDN_CHUNK, s)
    n_tiles = s // tile
    t_of = (lambda t: n_tiles - 1 - t) if reverse else (lambda t: t)
    seq = pl.BlockSpec((tile, hg * d), lambda grp, t: (t_of(t), grp))
    st = pl.BlockSpec((hg, tile // GDN_CHUNK, d, d), lambda grp, t: (grp, t_of(t), 0, 0))
    return seq, st, tile, n_tiles


def gdn_fwd(q, k, v, g, beta, name):
    s = q.shape[0]
    d, hg = GDN_DK, GDN_HEAD_GROUP
    seq, st, tile, n_tiles = _gdn_specs(s, False)

    def body(q_ref, k_ref, v_ref, g_ref, b_ref, o_ref, st_ref, state_scr):
        @pl.when(pl.program_id(1) == 0)
        def _():
            state_scr[...] = jnp.zeros_like(state_scr)

        def step(ci, carry):
            rows = pl.ds(pl.multiple_of(ci * GDN_CHUNK, GDN_CHUNK), GDN_CHUNK)
            for h in range(hg):
                cols = slice(h * d, (h + 1) * d)
                state = state_scr[h]
                st_ref[h, ci] = state
                o, new_state = _gdn_chunk(_PLAIN, q_ref[rows, cols], k_ref[rows, cols], v_ref[rows, cols],
                                          g_ref[rows, cols], b_ref[rows, cols], state)
                o_ref[rows, cols] = o
                state_scr[h] = new_state
            return carry

        lax.fori_loop(0, tile // GDN_CHUNK, step, 0)

    return pl.pallas_call(
        body, name=name,
        out_shape=[jax.ShapeDtypeStruct(q.shape, F32),
                   jax.ShapeDtypeStruct((GDN_HEADS, s // GDN_CHUNK, d, d), F32)],
        grid=(GDN_HEADS // hg, n_tiles), in_specs=[seq] * 5, out_specs=[seq, st],
        scratch_shapes=[pltpu.VMEM((hg, d, d), F32)],
        compiler_params=_params(("parallel", "arbitrary")),
    )(q, k, v, g, beta)


def gdn_bwd(q, k, v, g, beta, states, do, name):
    s = q.shape[0]
    d, hg = GDN_DK, GDN_HEAD_GROUP
    seq, st, tile, n_tiles = _gdn_specs(s, True)
    tile_chunks = tile // GDN_CHUNK

    def body(q_ref, k_ref, v_ref, g_ref, b_ref, st_ref, do_ref, dq_ref, dk_ref, dv_ref, dg_ref, db_ref, dstate_scr):
        @pl.when(pl.program_id(1) == 0)
        def _():
            dstate_scr[...] = jnp.zeros_like(dstate_scr)

        def step(it, carry):
            ci = tile_chunks - 1 - it
            rows = pl.ds(pl.multiple_of(ci * GDN_CHUNK, GDN_CHUNK), GDN_CHUNK)
            for h in range(hg):
                cols = slice(h * d, (h + 1) * d)
                prim = (q_ref[rows, cols], k_ref[rows, cols], v_ref[rows, cols], g_ref[rows, cols],
                        b_ref[rows, cols], st_ref[h, ci])
                _, vjp = jax.vjp(functools.partial(_gdn_chunk, _DIFF), *prim)
                dq, dk, dv, dg, db, dstate_in = vjp((do_ref[rows, cols], dstate_scr[h]))
                dq_ref[rows, cols] = dq
                dk_ref[rows, cols] = dk
                dv_ref[rows, cols] = dv
                dg_ref[rows, cols] = dg
                db_ref[rows, cols] = db
                dstate_scr[h] = dstate_in
            return carry

        lax.fori_loop(0, tile_chunks, step, 0)

    return pl.pallas_call(
        body, name=name,
        out_shape=[jax.ShapeDtypeStruct(q.shape, F32)] * 5,
        grid=(GDN_HEADS // hg, n_tiles), in_specs=[seq] * 5 + [st, seq], out_specs=[seq] * 5,
        scratch_shapes=[pltpu.VMEM((hg, d, d), F32)],
        compiler_params=_params(("parallel", "arbitrary")),
    )(q, k, v, g, beta, states, do)


def make_gdn(name):
    @jax.custom_vjp
    def op(q, k, v, g, beta):
        return gdn_fwd(q, k, v, g, beta, name + "_f")[0]

    def fwd(q, k, v, g, beta):
        o, states = gdn_fwd(q, k, v, g, beta, name + "_f")
        return o, (q, k, v, g, beta, states)

    def bwd(saved, do):
        return tuple(gdn_bwd(*saved, do, name + "_b"))

    op.defvjp(fwd, bwd)
    return op


def loss_head(x, g, target, name):
    s, d = x.shape
    tm = min(256, s)

    def body(x_ref, g_ref, t_ref, loss_ref, dx_ref, dg_ref):
        tgt = t_ref[...]

        def f(xv, gv):
            err = _rms(xv, gv) - tgt
            per_row = jnp.mean(err * err, axis=-1, keepdims=True)
            return 0.5 * jnp.sum(per_row, axis=0, keepdims=True)

        val, vjp = jax.vjp(f, x_ref[...], g_ref[...])
        dx, dg = vjp(jnp.ones((1, 1), F32))
        dx_ref[...] = dx
        first = pl.program_id(0) == 0

        @pl.when(first)
        def _():
            dg_ref[...] = dg
            loss_ref[...] = jnp.broadcast_to(val, loss_ref.shape)

        @pl.when(jnp.logical_not(first))
        def _():
            dg_ref[...] += dg
            loss_ref[...] += jnp.broadcast_to(val, loss_ref.shape)

    row = pl.BlockSpec((tm, d), lambda i: (i, 0))
    vec = pl.BlockSpec((1, d), lambda i: (0, 0))
    return pl.pallas_call(
        body, name=name,
        out_shape=[jax.ShapeDtypeStruct((1, LANES), F32), jax.ShapeDtypeStruct((s, d), F32),
                   jax.ShapeDtypeStruct((1, d), F32)],
        grid=(s // tm,), in_specs=[row, vec, row],
        out_specs=[pl.BlockSpec((1, LANES), lambda i: (0, 0)), row, vec],
        compiler_params=_params(("arbitrary",)),
    )(x, g, target)


def adamw(g8, w, m, v, layer, prev, name):
    n_layers, rows, width = w.shape
    tr = _pick(rows, (256, 128, 64, 32, 16, 8))

    def body(g_ref, w_ref, m_ref, v_ref, *rest):
        go_ref, d_ref, mo_ref, vo_ref = rest[-4:]
        g = g_ref[0].astype(F32)
        for p in range(1, N_DEV):
            g = g + g_ref[p].astype(F32)
        m_new = ADAM_B1 * m_ref[...] + (1.0 - ADAM_B1) * g
        v_new = ADAM_B2 * v_ref[...] + (1.0 - ADAM_B2) * (g * g)
        m_hat = m_new / (1.0 - ADAM_B1 ** ADAM_STEP)
        v_hat = v_new / (1.0 - ADAM_B2 ** ADAM_STEP)
        go_ref[...] = g
        d_ref[...] = -ADAM_LR * (m_hat / (jnp.sqrt(v_hat) + ADAM_EPS) + ADAM_WD * w_ref[...])
        mo_ref[...] = m_new
        vo_ref[...] = v_new

    blk = pl.BlockSpec((None, tr, width), lambda i: (layer, i, 0))
    carried = list(prev) if prev is not None else []
    return pl.pallas_call(
        body, name=name, out_shape=[jax.ShapeDtypeStruct((n_layers, rows, width), F32)] * 4,
        grid=(rows // tr,),
        in_specs=[pl.BlockSpec((N_DEV, tr, width), lambda i: (0, i, 0)), blk, blk, blk]
        + [pl.BlockSpec(memory_space=pl.ANY)] * len(carried),
        out_specs=[blk] * 4,
        input_output_aliases={4 + j: j for j in range(len(carried))},
        compiler_params=_params(("parallel",)),
    )(g8, w, m, v, *carried)


_HBM = pl.BlockSpec(memory_space=pltpu.HBM)
_SEM = pl.BlockSpec(memory_space=pltpu.SEMAPHORE)
_EFFECT = pltpu.SideEffectType.DATAFLOW_SIDE_EFFECTING


def _exchange_copies(mode, src_refs, land_refs, send_sems, recv_sems, local_sems):
    x, y, c = lax.axis_index("x"), lax.axis_index("y"), lax.axis_index("c")
    me = 4 * x + 2 * y + c
    n = len(src_refs)

    def src(k, p):
        return src_refs[k] if mode == "gather" else src_refs[k].at[p]

    local = [pltpu.make_async_copy(src(k, me), land_refs[k].at[me], local_sems.at[k]) for k in range(n)]
    sends, recvs = [], []
    for k in range(n):
        for r in range(1, N_DEV):
            px = (1 - x) if r & 4 else x
            py = (1 - y) if r & 2 else y
            pc = (1 - c) if r & 1 else c
            p = 4 * px + 2 * py + pc
            sem = k * (N_DEV - 1) + r - 1
            sends.append(pltpu.make_async_remote_copy(
                src_ref=src(k, p), dst_ref=land_refs[k].at[me],
                send_sem=send_sems.at[sem], recv_sem=recv_sems.at[sem],
                device_id=(px, py, pc), device_id_type=pl.DeviceIdType.MESH))
            recvs.append(pltpu.make_async_remote_copy(
                src_ref=src(k, p), dst_ref=land_refs[k].at[p],
                send_sem=send_sems.at[sem], recv_sem=recv_sems.at[sem],
                device_id=(px, py, pc), device_id_type=pl.DeviceIdType.MESH))
    return local, sends, recvs


def exchange_start(mode, arrays, name):
    n = len(arrays)
    land_shapes = [((N_DEV,) + tuple(a.shape)) if mode == "gather" else tuple(a.shape) for a in arrays]
    lands = [pltpu.with_memory_space_constraint(lax.empty(shp, a.dtype), pltpu.HBM)
             for shp, a in zip(land_shapes, arrays)]
    srcs = [pltpu.with_memory_space_constraint(a, pltpu.HBM) for a in arrays]

    def body(*refs):
        src_refs, land_refs = refs[:n], refs[n:2 * n]
        send_sems, recv_sems, local_sems = refs[2 * n:2 * n + 3]
        token = refs[-1]
        local, sends, _ = _exchange_copies(mode, src_refs, land_refs, send_sems, recv_sems, local_sems)
        for cp in local + sends:
            cp.start()
        token[...] = jnp.zeros_like(token)

    n_sem = n * (N_DEV - 1)
    out = pl.pallas_call(
        body, name=name,
        out_shape=(pltpu.SemaphoreType.DMA((n_sem,)), pltpu.SemaphoreType.DMA((n_sem,)),
                   pltpu.SemaphoreType.DMA((n,)),
                   *[pltpu.HBM(a.shape, a.dtype) for a in arrays],
                   *[pltpu.HBM(shp, a.dtype) for shp, a in zip(land_shapes, arrays)],
                   jax.ShapeDtypeStruct((8, LANES), F32)),
        in_specs=[_HBM] * (2 * n),
        out_specs=(_SEM, _SEM, _SEM, *[_HBM] * (2 * n), pl.BlockSpec(memory_space=pltpu.VMEM)),
        input_output_aliases={i: 3 + i for i in range(2 * n)},
        compiler_params=pltpu.CompilerParams(has_side_effects=_EFFECT),
    )(*srcs, *lands)
    handle = dict(mode=mode, sems=out[:3], srcs=out[3:3 + n], lands=out[3 + n:3 + 2 * n])
    return handle, out[-1]


def exchange_wait(handle, after, name):
    mode, srcs, lands = handle["mode"], list(handle["srcs"]), list(handle["lands"])
    n = len(srcs)

    def body(*refs):
        src_refs, land_refs = refs[:n], refs[n:2 * n]
        send_sems, recv_sems, local_sems = refs[2 * n:2 * n + 3]
        local, sends, recvs = _exchange_copies(mode, src_refs, land_refs, send_sems, recv_sems, local_sems)
        for cp in sends:
            cp.wait_send()
        for cp in recvs:
            cp.wait_recv()
        for cp in local:
            cp.wait()

    out = pl.pallas_call(
        body, name=name,
        out_shape=(*[pltpu.HBM(a.shape, a.dtype) for a in srcs], *[pltpu.HBM(a.shape, a.dtype) for a in lands]),
        in_specs=[_HBM] * (2 * n) + [_SEM] * 3 + [pl.BlockSpec(memory_space=pl.ANY)],
        out_specs=tuple([_HBM] * (2 * n)),
        input_output_aliases={i: i for i in range(2 * n)},
        compiler_params=pltpu.CompilerParams(has_side_effects=_EFFECT),
    )(*srcs, *lands, *handle["sems"], after)
    return list(out[n:])


BIG = ["mla_w_in", "mla_w_uq", "mla_w_ukv", "mla_w_o", "gdn_w_in", "gdn_w_o", "sc_w_in", "sc_w_o",
       "xa_w_q", "xa_w_kv", "xa_w_o", "mlp_w1", "mlp_w2"]
TINY = [("mla_q_norm", 1), ("mla_kv_norm", 1), ("gdn_conv_w", 2), ("sc_conv_w", 2)]
REPL = ["gdn_a_log", "gdn_dt_bias", "gdn_o_norm", "norm_mix", "norm_mem", "norm_mlp", "mem_norm", "final_norm"]
WEIGHTS = ["mla_w_in", "mla_q_norm", "mla_kv_norm", "mla_w_uq", "mla_w_ukv", "mla_w_o", "gdn_w_in",
           "gdn_conv_w", "gdn_a_log", "gdn_dt_bias", "gdn_o_norm", "gdn_w_o", "sc_w_in", "sc_conv_w",
           "sc_w_o", "norm_mix", "norm_mem", "norm_mlp", "xa_w_q", "xa_w_kv", "xa_w_o", "mlp_w1",
           "mlp_w2", "mem_norm", "final_norm"]
MIXER_WEIGHTS = (["mla_w_in", "mla_w_uq", "mla_w_ukv", "mla_w_o"], ["gdn_w_in", "gdn_w_o"], ["sc_w_in", "sc_w_o"])
MIXER_PARAMS = (["norm_mix", "mla_q_norm", "mla_kv_norm"],
                ["norm_mix", "gdn_conv_w", "gdn_a_log", "gdn_dt_bias", "gdn_o_norm"],
                ["norm_mix", "sc_conv_w"])


def from_shards(a8, axis):
    a = jnp.moveaxis(a8, 0, axis)
    shp = a.shape
    return a.reshape(shp[:axis] + (shp[axis] * shp[axis + 1],) + shp[axis + 2:])


def pack_rows(flat_list, width, row_mult):
    total = sum(a.shape[-1] for a in flat_list)
    rows = -(-total // width)
    rows = -(-rows // row_mult) * row_mult
    pad = rows * width - total
    parts = list(flat_list)
    if pad:
        parts.append(jnp.zeros((pad,), flat_list[0].dtype))
    return jnp.concatenate(parts, axis=-1).reshape(rows, width)


def unpack_rows(packed, shapes):
    lead = packed.shape[:-2]
    flat = packed.reshape(lead + (-1,))
    out, off = [], 0
    for shp in shapes:
        n = math.prod(shp)
        out.append(flat[..., off:off + n].reshape(lead + tuple(shp)))
        off += n
    return out


def _swap_halves(w):
    half = w.shape[-1] // 2
    return jnp.concatenate([w[..., half:], w[..., :half]], axis=-1)


def _pad_last(w, n):
    return jnp.pad(w, [(0, 0)] * (w.ndim - 1) + [(0, n - w.shape[-1])])


def _unblock(w8):
    return jnp.transpose(w8, (1, 0, 2)).reshape(w8.shape[1], -1)


def _stack_rows(w8):
    return w8.reshape(-1, w8.shape[-1])


def rms_op(name, rows, d, out_dtype, tm=512):
    return make_tile_op(fn_rms, name, ["row", "par"], [True, True], [("row", d, out_dtype)], rows, min(tm, rows))


def seg_memory(p, mem):
    return rms_op("rms_memory", mem.shape[0], mem.shape[1], BF16)(mem, p["mem_norm"].reshape(1, -1))[0]


def seg_mixer(i, wts, p, x, rope_c, rope_s):
    s, d = x.shape
    j, kind = i // N_MIXERS, i % N_MIXERS
    tag = f"l{i}"
    hd = MLA_NOPE
    h = rms_op(tag + "_rms_mix", s, d, BF16)(x, p["norm_mix"][i].reshape(1, d))[0]
    if kind == 0:
        w_in = _stack_rows(wts["mla_w_in"])
        w_cq = w_in[:, :MLA_Q_RANK]
        w_ckv = w_in[:, MLA_Q_RANK:MLA_Q_RANK + MLA_KV_RANK]
        w_kr = w_in[:, MLA_Q_RANK + MLA_KV_RANK:]
        w_z = jnp.concatenate([w_cq, w_ckv, _pad_last(w_kr, hd), _pad_last(_swap_halves(w_kr), hd)], axis=-1)
        z = make_mm(tag + "_mla_in", F32)(h, w_z)
        c_q, c_kv = z[:, :MLA_Q_RANK], z[:, MLA_Q_RANK:MLA_Q_RANK + MLA_KV_RANK]
        kr_raw, kr_swp = z[:, -2 * hd:-hd], z[:, -hd:]
        c_qn = rms_op(tag + "_rms_q", s, MLA_Q_RANK, BF16)(c_q, p["mla_q_norm"][j].reshape(1, -1))[0]
        c_kvn = rms_op(tag + "_rms_kv", s, MLA_KV_RANK, BF16)(c_kv, p["mla_kv_norm"][j].reshape(1, -1))[0]
        w_uq8 = wts["mla_w_uq"]
        w_qn = _unblock(w_uq8[:, :, :MLA_NOPE])
        w_qr = w_uq8[:, :, MLA_NOPE:]
        w_qr_p = _unblock(_pad_last(w_qr, hd))
        w_qr_s = _unblock(_pad_last(_swap_halves(w_qr), hd))
        q_nope = make_mm(tag + "_mla_uq_n", BF16)(c_qn, w_qn)
        q_raw = make_mm(tag + "_mla_uq_r", F32)(c_qn, w_qr_p)
        q_swp = make_mm(tag + "_mla_uq_s", F32)(c_qn, w_qr_s)
        nq = MLA_HEADS * hd
        kv = make_mm(tag + "_mla_ukv", BF16, blocked=True)(c_kvn, wts["mla_w_ukv"])
        heads_row = ("row", hd, MLA_HEADS)
        q_rope = make_tile_op(fn_rope, tag + "_rope_q", [heads_row, heads_row, "row", "row"],
                              [True, True, False, False], [(heads_row, nq, F32)], s, 512)(
            q_raw, q_swp, rope_c, rope_s)[0]
        k_rope = make_tile_op(fn_rope, tag + "_rope_k", ["row", "row", "row", "row"],
                              [True, True, False, False], [("row", hd, F32)], s, 1024)(
            kr_raw, kr_swp, rope_c, rope_s)[0]
        o = make_tile_op(fn_mla_attn, tag + "_mla_attn",
                         [("rowh", hd), ("rowh", hd), ("parh", hd, 2), "par"],
                         [True] * 4, [(("rowh", hd), nq, BF16)], s, 256, MLA_HEADS)(
            q_nope, q_rope, kv, k_rope)[0]
        return make_mm(tag + "_mla_o", F32, with_res=True)(x, o, _stack_rows(wts["mla_w_o"]))
    if kind == 1:
        ng = GDN_HEADS * GDN_DK
        w_in = _unblock(wts["gdn_w_in"])
        cw = p["gdn_conv_w"][j]
        conv_out = []
        for part, nm in enumerate(("q", "k", "v")):
            cols = slice(part * ng, (part + 1) * ng)
            pre = make_mm(f"{tag}_gdn_in_{nm}", F32)(h, w_in[:, cols])
            conv_out.append(make_conv(f"{tag}_gdn_conv_{nm}")(pre, cw[:, cols]))
        gate = make_mm(tag + "_gdn_in_g", F32)(h, w_in[:, 3 * ng:4 * ng])
        ba = make_mm(tag + "_gdn_in_ba", F32)(h, _pad_last(w_in[:, 4 * ng:], LANES))
        heads_row = ("row", GDN_DK, GDN_HEADS)
        q, k, v = make_tile_op(fn_gdn_prep, tag + "_gdn_prep", [heads_row] * 3, [True] * 3,
                               [(heads_row, ng, F32)] * 3, s, 512)(*conv_out)
        alog = jnp.pad(p["gdn_a_log"][j].reshape(1, -1), ((0, 0), (GDN_HEADS, LANES - 2 * GDN_HEADS)))
        dtb = jnp.pad(p["gdn_dt_bias"][j].reshape(1, -1), ((0, 0), (GDN_HEADS, LANES - 2 * GDN_HEADS)))
        beta_b, g_b = make_tile_op(fn_gdn_gates, tag + "_gdn_gates", ["row", "par", "par"], [True] * 3,
                                   [("row", ng, F32)] * 2, s, 512)(ba, alog, dtb)
        o = make_gdn(tag + "_gdn_core")(q, k, v, g_b, beta_b)
        o = make_tile_op(fn_gdn_out, tag + "_gdn_out", [heads_row, heads_row, "par"],
                         [True] * 3, [(heads_row, ng, BF16)], s, 512)(
            o, gate, p["gdn_o_norm"][j].reshape(1, -1))[0]
        return make_mm(tag + "_gdn_o", F32, with_res=True)(x, o, _stack_rows(wts["gdn_w_o"]))
    w_in = _unblock(wts["sc_w_in"])
    b_gate = make_mm(tag + "_sc_in_b", F32)(h, w_in[:, :d])
    c_gate = make_mm(tag + "_sc_in_c", F32)(h, w_in[:, d:2 * d])
    u = make_mm(tag + "_sc_in_u", F32)(h, w_in[:, 2 * d:])
    cu = make_tile_op(fn_mul, tag + "_sc_cu", ["row", "row"], [True, True], [("row", d, F32)], s, 512)(
        c_gate, u)[0]
    cv = make_conv(tag + "_sc_conv")(cu, p["sc_conv_w"][j])
    yv = make_tile_op(fn_mul, tag + "_sc_gate", ["row", "row"], [True, True], [("row", d, BF16)], s, 512)(
        b_gate, cv)[0]
    return make_mm(tag + "_sc_o", F32, with_res=True)(x, yv, _stack_rows(wts["sc_w_o"]))


def seg_xattn(i, wts, p, x, mem_n):
    s, d = x.shape
    tag = f"l{i}"
    hx = rms_op(tag + "_rms_mem", s, d, BF16)(x, p["norm_mem"][i].reshape(1, d))[0]
    q = make_mm(tag + "_xa_q", BF16)(hx, _stack_rows(wts["xa_w_q"]))
    kv = make_mm(tag + "_xa_kv", BF16, blocked=True)(mem_n, wts["xa_w_kv"])
    o = make_tile_op(fn_xattn, tag + "_xattn",
                     [("rowh", X_HEAD_DIM), ("parh", X_HEAD_DIM), ("parh", X_HEAD_DIM)], [True] * 3,
                     [(("rowh", X_HEAD_DIM), d, BF16)], s, 1024, X_HEADS)(q, kv[:, :d], kv[:, d:])[0]
    return make_mm(tag + "_xa_o", F32, with_res=True)(x, o, _stack_rows(wts["xa_w_o"]))


def seg_mlp(i, wts, p, x):
    s, d = x.shape
    tag = f"l{i}"
    hm = rms_op(tag + "_rms_mlp", s, d, BF16)(x, p["norm_mlp"][i].reshape(1, d))[0]
    a = make_mm(tag + "_mlp_1", BF16, blocked=True)(hm, wts["mlp_w1"])
    bsq = make_tile_op(fn_relu2, tag + "_relu2", ["row"], [True], [("row", a.shape[1], BF16)], s, 256)(a)[0]
    return make_mm(tag + "_mlp_2", F32, with_res=True)(x, bsq, _stack_rows(wts["mlp_w2"]))


def segments():
    segs = []
    for i in range(DEPTH):
        j, kind = i // N_MIXERS, i % N_MIXERS
        segs.append((f"l{i}_mixer", [(n, j) for n in MIXER_WEIGHTS[kind]], MIXER_PARAMS[kind], "mixer"))
        segs.append((f"l{i}_xattn", [(n, i) for n in ("xa_w_q", "xa_w_kv", "xa_w_o")], ["norm_mem"], "xattn"))
        segs.append((f"l{i}_mlp", [(n, i) for n in ("mlp_w1", "mlp_w2")], ["norm_mlp"], "mlp"))
    return segs


def run_segment(index, kind, wts, p, x, mem_n, rope_c, rope_s):
    layer = index // 3
    if kind == "mixer":
        return seg_mixer(layer, wts, p, x, rope_c, rope_s)
    if kind == "xattn":
        return seg_xattn(layer, wts, p, x, mem_n)
    return seg_mlp(layer, wts, p, x)


def rope_tables(positions):
    inv_freq = ROPE_THETA ** (-jnp.arange(0, MLA_ROPE, 2, dtype=F32) / MLA_ROPE)
    ang = positions.astype(F32)[:, None] * inv_freq
    cos, sin = jnp.cos(ang), jnp.sin(ang)
    zeros = jnp.zeros((positions.shape[0], MLA_NOPE - MLA_ROPE), F32)
    return jnp.concatenate([cos, cos, zeros], axis=-1), jnp.concatenate([-sin, sin, zeros], axis=-1)


def kernel(x, mem, positions, mla_w_in, mla_q_norm, mla_kv_norm, mla_w_uq, mla_w_ukv, mla_w_o, gdn_w_in, gdn_conv_w, gdn_a_log, gdn_dt_bias, gdn_o_norm, gdn_w_o, sc_w_in, sc_conv_w, sc_w_o, norm_mix, norm_mem, norm_mlp, xa_w_q, xa_w_kv, xa_w_o, mlp_w1, mlp_w2, mem_norm, final_norm, loss_target, m_mla_w_in, m_mla_q_norm, m_mla_kv_norm, m_mla_w_uq, m_mla_w_ukv, m_mla_w_o, m_gdn_w_in, m_gdn_conv_w, m_gdn_a_log, m_gdn_dt_bias, m_gdn_o_norm, m_gdn_w_o, m_sc_w_in, m_sc_conv_w, m_sc_w_o, m_norm_mix, m_norm_mem, m_norm_mlp, m_xa_w_q, m_xa_w_kv, m_xa_w_o, m_mlp_w1, m_mlp_w2, m_mem_norm, m_final_norm, v_mla_w_in, v_mla_q_norm, v_mla_kv_norm, v_mla_w_uq, v_mla_w_ukv, v_mla_w_o, v_gdn_w_in, v_gdn_conv_w, v_gdn_a_log, v_gdn_dt_bias, v_gdn_o_norm, v_gdn_w_o, v_sc_w_in, v_sc_conv_w, v_sc_w_o, v_norm_mix, v_norm_mem, v_norm_mlp, v_xa_w_q, v_xa_w_kv, v_xa_w_o, v_mlp_w1, v_mlp_w2, v_mem_norm, v_final_norm):
    args = locals()
    w_loc = {n: args[n] for n in WEIGHTS}
    m_loc = {n: args["m_" + n] for n in WEIGHTS}
    v_loc = {n: args["v_" + n] for n in WEIGHTS}
    me = 4 * lax.axis_index("x") + 2 * lax.axis_index("y") + lax.axis_index("c")
    segs = segments()

    w16 = {n: w_loc[n].astype(BF16) for n in BIG}
    tiny_pack = pack_rows([w_loc[n].reshape(-1) for n, _ in TINY], LANES, 8)
    tiny_handle, _ = exchange_start("gather", [tiny_pack], "gather_start_tiny")
    gather_handles = []
    for tag, units, _, _ in segs:
        handle, _ = exchange_start("gather", [w16[n][layer] for n, layer in units], f"gather_start_{tag}")
        gather_handles.append(handle)

    x_cur = x[0]
    rope_c, rope_s = rope_tables(positions[0])
    tiny_all = exchange_wait(tiny_handle, x_cur, "gather_wait_tiny")[0]
    params = {}
    for (n, ax), a8 in zip(TINY, unpack_rows(tiny_all, [w_loc[n].shape for n, _ in TINY])):
        params[n] = from_shards(a8, ax)
    for n in REPL:
        params[n] = w_loc[n]

    mem_n, vjp_memory = jax.vjp(lambda p_: seg_memory(p_, mem[0]), {"mem_norm": params["mem_norm"]})
    vjps = []
    for index, ((tag, units, p_names, kind), handle) in enumerate(zip(segs, gather_handles)):
        landed = exchange_wait(handle, x_cur, f"gather_wait_{tag}")
        wts = {n: a for (n, _), a in zip(units, landed)}
        p_seg = {n: params[n] for n in p_names}
        x_cur, vjp_seg = jax.vjp(
            lambda w_, p_, x_, m_, index=index, kind=kind: run_segment(index, kind, w_, p_, x_, m_, rope_c, rope_s),
            wts, p_seg, x_cur, mem_n)
        vjps.append(vjp_seg)

    loss_vec, g_x, d_final = loss_head(x_cur, params["final_norm"].reshape(1, -1), loss_target[0], "loss_head")

    grads = {n: jnp.zeros_like(params[n]) for n in params}
    grads["final_norm"] = d_final.reshape(-1)
    g_mem_n = jnp.zeros_like(mem_n)
    scatter_handles = []
    for (tag, units, _, _), vjp_seg in zip(reversed(segs), reversed(vjps)):
        g_wts, g_p, g_x, g_m = vjp_seg(g_x)
        for n, g in g_p.items():
            grads[n] = grads[n] + g
        g_mem_n = g_mem_n + g_m
        handle, token = exchange_start("scatter", [g_wts[n] for n, _ in units], f"scatter_start_{tag}")
        scatter_handles.append((units, handle))
        g_x = g_x + token[0, 0]
    grads["mem_norm"] = grads["mem_norm"] + vjp_memory(g_mem_n)[0]["mem_norm"]

    small_names = [n for n, _ in TINY] + REPL
    small_g = pack_rows([loss_vec[0, :1]] + [grads[n].astype(F32).reshape(-1) for n in small_names], PACK_W, 8)
    small_handle, _ = exchange_start("gather", [small_g], "gather_start_small_grads")

    g_recv = {}
    for units, handle in scatter_handles:
        landed = exchange_wait(handle, g_x, f"scatter_wait_{units[0][0]}_{units[0][1]}")
        g_recv.update(dict(zip(units, landed)))
    small_recv = exchange_wait(small_handle, g_x, "gather_wait_small_grads")[0]

    res = {}
    for n in BIG:
        outs = None
        for layer in range(w_loc[n].shape[0]):
            outs = adamw(g_recv[n, layer], w_loc[n], m_loc[n], v_loc[n], layer, outs, f"adamw_{n}_{layer}")
        for kind, a in zip(("grad", "delta", "m", "v"), outs):
            res[(kind, n)] = a

    def full_small(d):
        parts = [jnp.zeros((1,), F32)]
        for n, ax in TINY:
            full_shape = params[n].shape
            start = [0] * len(full_shape)
            start[ax] = me * d[n].shape[ax]
            parts.append(lax.dynamic_update_slice(jnp.zeros(full_shape, F32), d[n], start).reshape(-1))
        parts += [d[n].reshape(-1) for n in REPL]
        return pack_rows(parts, PACK_W, 8)

    outs_small = adamw(small_recv, full_small(w_loc)[None], full_small(m_loc)[None], full_small(v_loc)[None],
                       0, None, "adamw_small")
    small_shapes = [(1,)] + [params[n].shape for n, _ in TINY] + [w_loc[n].shape for n in REPL]
    loss = None
    for kind, packed in zip(("grad", "delta", "m", "v"), outs_small):
        parts = unpack_rows(packed[0], small_shapes)
        if kind == "grad":
            loss = parts[0][0]
        for (n, ax), a in zip(TINY, parts[1:1 + len(TINY)]):
            start = [0] * a.ndim
            start[ax] = me * w_loc[n].shape[ax]
            res[(kind, n)] = lax.dynamic_slice(a, start, w_loc[n].shape)
        for n, a in zip(REPL, parts[1 + len(TINY):]):
            res[(kind, n)] = a

    out = [loss, g_x[None]]
    for kind in ("grad", "delta", "m", "v"):
        out += [res[(kind, n)] for n in WEIGHTS]
    return tuple(out)
```

```python
import functools
import math

import jax
import jax.numpy as jnp
from jax import lax
from jax.experimental import pallas as pl
from jax.experimental.pallas import tpu as pltpu

F32 = jnp.float32
BF16 = jnp.bfloat16

N_DEV = 8
LANES = 128
EPS = 1e-6
ROPE_THETA = 10000.0
MLA_HEADS, MLA_NOPE, MLA_ROPE, MLA_V = 8, 128, 64, 128
MLA_Q_RANK, MLA_KV_RANK = 384, 256
GDN_HEADS, GDN_DK, GDN_CONV, GDN_CHUNK = 8, 128, 4, 64
X_HEADS, X_HEAD_DIM = 4, 256
DEPTH, N_MIXERS = 4, 3
ADAM_LR, ADAM_B1, ADAM_B2, ADAM_EPS, ADAM_WD, ADAM_STEP = 0.001, 0.9, 0.999, 1e-08, 0.01, 10
NEG_BIG = -1e30
PACK_W = 1024


_NN = (((1,), (0,)), ((), ()))
_NT = (((1,), (1,)), ((), ()))
_TN = (((0,), (0,)), ((), ()))
_NN3 = (((2,), (1,)), ((0,), (0,)))
_NT3 = (((2,), (2,)), ((0,), (0,)))
_TN3 = (((1,), (1,)), ((0,), (0,)))


def _dot(a, b, dims):
    return lax.dot_general(a, b, dims, preferred_element_type=F32)


def _hi_lo(x):
    hi = x.astype(BF16)
    return hi, (x - hi.astype(F32)).astype(BF16)


def _split3(x):
    hi = x.astype(BF16)
    r = x - hi.astype(F32)
    mid = r.astype(BF16)
    return hi, mid, (r - mid.astype(F32)).astype(BF16)


def _dg(a, b, dims, prec):
    if prec == "h":
        return lax.dot_general(a, b, dims, precision=lax.Precision.HIGHEST, preferred_element_type=F32)
    if prec == "m":
        a_hi, a_lo = _hi_lo(a)
        b_hi, b_lo = _hi_lo(b)
        return _dot(a_hi, b_hi, dims) + _dot(a_hi, b_lo, dims) + _dot(a_lo, b_hi, dims)
    return _dot(a.astype(BF16), b.astype(BF16), dims)


def _dg_sel(sel, x, dims, sel_first):
    s16 = sel.astype(BF16)
    parts = [(_dot(s16, piece, dims) if sel_first else _dot(piece, s16, dims)) for piece in _split3(x)]
    return parts[0] + parts[1] + parts[2]


class _Ops:
    def __init__(self, prec, differentiable, batched=False):
        d_nn, d_nt, d_tn = (_NN3, _NT3, _TN3) if batched else (_NN, _NT, _TN)

        def nn(a, b):
            return _dg(a, b, d_nn, prec)

        def nt(a, b):
            return _dg(a, b, d_nt, prec)

        def tn(a, b):
            return _dg(a, b, d_tn, prec)

        if differentiable:
            dnn = jax.custom_vjp(nn)
            dnn.defvjp(lambda a, b: (nn(a, b), (a, b)), lambda r, g: (nt(g, r[1]), tn(r[0], g)))
            dnt = jax.custom_vjp(nt)
            dnt.defvjp(lambda a, b: (nt(a, b), (a, b)), lambda r, g: (nn(g, r[1]), tn(g, r[0])))
            dtn = jax.custom_vjp(tn)
            dtn.defvjp(lambda a, b: (tn(a, b), (a, b)), lambda r, g: (nt(r[1], g), nn(r[0], g)))
            nn, nt, tn = dnn, dnt, dtn
        self.nn, self.nt, self.tn = nn, nt, tn


class _SelOps:
    def __init__(self, differentiable, batched=False):
        d_nn, d_nt, d_tn = (_NN3, _NT3, _TN3) if batched else (_NN, _NT, _TN)

        def sel_nn(sel, x):
            return _dg_sel(sel, x, d_nn, True)

        def sel_nt(sel, x):
            return _dg_sel(sel, x, d_nt, True)

        if differentiable:
            dnn = jax.custom_vjp(sel_nn)
            dnn.defvjp(lambda s, x: (sel_nn(s, x), s),
                       lambda s, g: (jnp.zeros_like(s), _dg_sel(s, g, d_tn, True)))
            dnt = jax.custom_vjp(sel_nt)
            dnt.defvjp(lambda s, x: (sel_nt(s, x), s),
                       lambda s, g: (jnp.zeros_like(s), _dg_sel(s, g, d_tn, False)))
            sel_nn, sel_nt = dnn, dnt
        self.sel_nn, self.sel_nt = sel_nn, sel_nt


class _OpSet:
    def __init__(self, differentiable):
        self.b = _Ops("b", differentiable)
        self.h = _Ops("h", differentiable)
        self.bb = _Ops("b", differentiable, batched=True)
        self.bm = _Ops("m", differentiable, batched=True)
        self.bs = _SelOps(differentiable, batched=True)


_PLAIN = _OpSet(False)
_DIFF = _OpSet(True)


def _params(sem):
    return pltpu.CompilerParams(dimension_semantics=sem)


BLOCK_BYTES = 4 * 1024 * 1024


def _pick(n, cands):
    for c in cands:
        if n % c == 0:
            return c
    return n


def _tile(n, cap):
    if n <= cap:
        return n
    return _pick(n, tuple(c for c in (2048, 1024, 768, 512, 384, 256, 128) if c <= cap))


def matmul(a, b, form, out_dtype, name, res=None, blocked=False):
    if form == "nn":
        m, k = a.shape
        k2, n = (b.shape[1], N_DEV * b.shape[2]) if blocked else b.shape
    elif form == "nt":
        m, k = a.shape
        n, k2 = (b.shape[1], N_DEV * b.shape[2]) if blocked else b.shape
    else:
        (k, m), (k2, n) = a.shape, b.shape
    assert k == k2, (a.shape, b.shape, form)
    tk = k if k <= 2048 else _tile(k, 1024)
    cb = nb = 1
    if blocked:
        cb = (k if form == "nt" else n) // N_DEV
        nb = _pick(N_DEV, tuple(c for c in (8, 4, 2, 1) if c * cb <= 1024))
    if blocked and form == "nt":
        tk = nb * cb
    if blocked and form != "nt":
        tn = nb * cb
    else:
        tn = _tile(n, min(1024, BLOCK_BYTES // (tk * b.dtype.itemsize)))
    out_elems = BLOCK_BYTES // 2 if (out_dtype == BF16 and res is None) else BLOCK_BYTES // 4
    tm = _tile(m, min(BLOCK_BYTES // (tk * a.dtype.itemsize), out_elems // tn))
    nk = k // tk
    dims = {"nn": _NN, "nt": _NT, "tn": _TN}[form]

    a_spec = {"nn": pl.BlockSpec((tm, tk), lambda i, j, kk: (i, kk)),
              "nt": pl.BlockSpec((tm, tk), lambda i, j, kk: (i, kk)),
              "tn": pl.BlockSpec((tk, tm), lambda i, j, kk: (kk, i))}[form]
    if blocked and form == "nn":
        b_spec = pl.BlockSpec((nb, tk, cb), lambda i, j, kk: (j, kk, 0))
    elif blocked and form == "nt":
        b_spec = pl.BlockSpec((nb, tn, cb), lambda i, j, kk: (kk, j, 0))
    else:
        b_spec = {"nn": pl.BlockSpec((tk, tn), lambda i, j, kk: (kk, j)),
                  "nt": pl.BlockSpec((tn, tk), lambda i, j, kk: (j, kk)),
                  "tn": pl.BlockSpec((tk, tn), lambda i, j, kk: (kk, j))}[form]
    c_spec = pl.BlockSpec((tm, tn), lambda i, j, kk: (i, j))
    out_shape = jax.ShapeDtypeStruct((m, n), out_dtype)
    o_spec = c_spec
    blocked_out = blocked and form == "tn"
    if blocked_out:
        out_shape = jax.ShapeDtypeStruct((N_DEV, m, cb), out_dtype)
        o_spec = pl.BlockSpec((nb, tm, cb), lambda i, j, kk: (j, i, 0))
    has_res = res is not None

    def body(*refs):
        a_ref, b_ref = refs[0], refs[1]
        r_ref = refs[2] if has_res else None
        o_ref = refs[3] if has_res else refs[2]
        a_val = a_ref[...].astype(BF16)
        if blocked and form == "nn":
            part = jnp.concatenate([_dot(a_val, b_ref[t].astype(BF16), dims) for t in range(nb)], axis=-1)
        elif blocked and form == "nt":
            part = _dot(a_val[:, :cb], b_ref[0].astype(BF16), dims)
            for t in range(1, nb):
                part = part + _dot(a_val[:, t * cb:(t + 1) * cb], b_ref[t].astype(BF16), dims)
        else:
            part = _dot(a_val, b_ref[...].astype(BF16), dims)

        def finish(acc):
            if has_res:
                acc = acc + r_ref[...].astype(F32)
            if blocked_out:
                for t in range(nb):
                    o_ref[t] = acc[:, t * cb:(t + 1) * cb].astype(out_dtype)
            else:
                o_ref[...] = acc.astype(out_dtype)

        if nk == 1:
            finish(part)
        else:
            acc_ref = refs[-1]
            kk = pl.program_id(2)

            @pl.when(kk == 0)
            def _():
                acc_ref[...] = part

            @pl.when(jnp.logical_and(kk > 0, kk < nk - 1))
            def _():
                acc_ref[...] += part

            @pl.when(kk == nk - 1)
            def _():
                finish(acc_ref[...] + part)

    in_specs = [a_spec, b_spec] + ([c_spec] if has_res else [])
    args = (a, b) + ((res,) if has_res else ())
    return pl.pallas_call(
        body, name=name,
        out_shape=out_shape,
        grid=(m // tm, n // tn, nk),
        in_specs=in_specs, out_specs=o_spec,
        scratch_shapes=[pltpu.VMEM((tm, tn), F32)] if nk > 1 else [],
        compiler_params=_params(("parallel", "parallel", "arbitrary")),
    )(*args)


def make_mm(name, out_dtype, with_res=False, blocked=False):
    def bwd_mm(a, w, g):
        da = matmul(g, w, "nt", a.dtype, name + "_da", blocked=blocked)
        dw = matmul(a, g, "tn", w.dtype, name + "_dw", blocked=blocked)
        return da, dw

    if with_res:
        @jax.custom_vjp
        def op(res, a, w):
            return matmul(a, w, "nn", out_dtype, name + "_f", res=res, blocked=blocked)

        def fwd(res, a, w):
            return op(res, a, w), (a, w)

        def bwd(saved, g):
            return (g,) + bwd_mm(*saved, g)
    else:
        @jax.custom_vjp
        def op(a, w):
            return matmul(a, w, "nn", out_dtype, name + "_f", blocked=blocked)

        def fwd(a, w):
            return op(a, w), (a, w)

        def bwd(saved, g):
            return bwd_mm(*saved, g)
    op.defvjp(fwd, bwd)
    return op


def _kind(k):
    if isinstance(k, str):
        return k, None, 1
    return k[0], k[1], (k[2] if len(k) > 2 else 1)


def _tile_spec(kind, shape, tm, heads):
    k, d, ns = _kind(kind)
    if k == "row":
        return pl.BlockSpec((tm, shape[1]), (lambda h, i: (i, 0)) if heads else (lambda i: (i, 0)))
    if k == "par":
        return pl.BlockSpec(tuple(shape), (lambda h, i: (0, 0)) if heads else (lambda i: (0, 0)))
    if k == "rowh":
        return pl.BlockSpec((tm, d * ns), lambda h, i: (i, h))
    if k == "parh":
        return pl.BlockSpec((shape[0], d * ns), lambda h, i: (0, h))
    raise ValueError(kind)


def _tile_grid(rows, tm, heads):
    n_rows = rows // tm
    return ((heads, n_rows) if heads else (n_rows,)), (1 if heads else 0)


def _split_vals(kinds, refs):
    vals, counts = [], []
    for kind, r in zip(kinds, refs):
        _, d, ns = _kind(kind)
        v = r[...].astype(F32)
        vals += [v] if ns == 1 else [v[:, p * d:(p + 1) * d] for p in range(ns)]
        counts.append(ns)
    return vals, counts


def tile_fwd(fn, name, kinds, args, outs, rows, tm, heads):
    grid, row_axis = _tile_grid(rows, tm, heads)
    n_in = len(args)
    out_shapes = [jax.ShapeDtypeStruct((rows, w), dt) for (_, w, dt) in outs]

    def body(*refs):
        vals, _ = _split_vals(kinds, refs[:n_in])
        row0 = pl.program_id(row_axis) * tm
        res = list(fn(_PLAIN, row0, *vals))
        for o_ref, (k, _, _) in zip(refs[n_in:], outs):
            pieces = [res.pop(0) for _ in range(_kind(k)[2])]
            v = pieces[0] if len(pieces) == 1 else jnp.concatenate(pieces, axis=-1)
            o_ref[...] = v.astype(o_ref.dtype)

    return pl.pallas_call(
        body, name=name, out_shape=out_shapes, grid=grid,
        in_specs=[_tile_spec(k, a.shape, tm, heads) for k, a in zip(kinds, args)],
        out_specs=[_tile_spec(k, (rows, w), tm, heads) for (k, w, _) in outs],
        compiler_params=_params(("arbitrary",) * len(grid)),
    )(*args)


def tile_bwd(fn, name, kinds, args, diff, outs, cts, rows, tm, heads):
    grid, row_axis = _tile_grid(rows, tm, heads)
    n_in, n_ct = len(args), len(cts)
    diff_idx = [i for i, d in enumerate(diff) if d]
    g_shapes, g_specs = [], []
    for i in diff_idx:
        k = _kind(kinds[i])[0]
        dt = args[i].dtype if k in ("row", "rowh") else F32
        g_shapes.append(jax.ShapeDtypeStruct(args[i].shape, dt))
        g_specs.append(_tile_spec(kinds[i], args[i].shape, tm, heads))

    def body(*refs):
        in_refs, ct_refs, g_refs = refs[:n_in], refs[n_in:n_in + n_ct], refs[n_in + n_ct:]
        vals, counts = _split_vals(kinds, in_refs)
        first_piece = [sum(counts[:i]) for i in range(n_in)]
        flat_diff = [first_piece[i] + p for i in diff_idx for p in range(counts[i])]
        row_id = pl.program_id(row_axis)
        row0 = row_id * tm

        def f(*dvals):
            full = list(vals)
            for i, dv in zip(flat_diff, dvals):
                full[i] = dv
            return tuple(fn(_DIFF, row0, *full))

        _, vjp = jax.vjp(f, *[vals[i] for i in flat_diff])
        ct_vals, _ = _split_vals([k for (k, _, _) in outs], ct_refs)
        flat_grads = list(vjp(tuple(ct_vals)))
        for g_ref, i in zip(g_refs, diff_idx):
            pieces = [flat_grads.pop(0) for _ in range(counts[i])]
            g = pieces[0] if len(pieces) == 1 else jnp.concatenate(pieces, axis=-1)
            k = _kind(kinds[i])[0]
            if k in ("row", "rowh"):
                g_ref[...] = g.astype(g_ref.dtype)
            else:
                first = row_id == 0
                if heads and k == "par":
                    first = jnp.logical_and(first, pl.program_id(0) == 0)

                @pl.when(first)
                def _(g_ref=g_ref, g=g):
                    g_ref[...] = g

                @pl.when(jnp.logical_not(first))
                def _(g_ref=g_ref, g=g):
                    g_ref[...] += g

    return pl.pallas_call(
        body, name=name, out_shape=g_shapes, grid=grid,
        in_specs=[_tile_spec(k, a.shape, tm, heads) for k, a in zip(kinds, args)]
        + [_tile_spec(k, (rows, w), tm, heads) for (k, w, _) in outs],
        out_specs=g_specs,
        compiler_params=_params(("arbitrary",) * len(grid)),
    )(*args, *cts)


def make_tile_op(fn, name, kinds, diff, outs, rows, tm, heads=0):
    tm = min(tm, rows)

    @jax.custom_vjp
    def op(*args):
        return tuple(tile_fwd(fn, name + "_f", kinds, args, outs, rows, tm, heads))

    def fwd(*args):
        return op(*args), args

    def bwd(args, cts):
        grads = tile_bwd(fn, name + "_b", kinds, args, diff, outs, cts, rows, tm, heads)
        it = iter(grads)
        res = []
        for a, d in zip(args, diff):
            res.append(next(it).astype(a.dtype) if d else None)
        return tuple(res)

    op.defvjp(fwd, bwd)
    return op


def _rms(x, g):
    return x * lax.rsqrt(jnp.mean(x * x, axis=-1, keepdims=True) + EPS) * g


def fn_rms(ops, row0, x, g):
    return (_rms(x, g),)


def fn_relu2(ops, row0, a):
    r = jnp.maximum(a, 0.0)
    return (r * r,)


def fn_mul(ops, row0, a, b):
    return (a * b,)


def fn_rope(ops, row0, *t):
    nh = (len(t) - 2) // 2
    c, s = t[-2], t[-1]
    return tuple(x * c + xs * s for x, xs in zip(t[:nh], t[nh:2 * nh]))


def _softmax(s):
    m = lax.stop_gradient(jnp.max(s, axis=-1, keepdims=True))
    e = jnp.exp(s - m)
    return e / jnp.sum(e, axis=-1, keepdims=True)


def fn_xattn(ops, row0, q, k, v):
    s = ops.b.nt(q, k) * (X_HEAD_DIM ** -0.5)
    return (ops.b.nn(_softmax(s), v),)


def fn_mla_attn(ops, row0, qn, qr, kn, v, kr):
    s = (ops.b.nt(qn, kn) + ops.b.nt(qr, kr)) * ((MLA_NOPE + MLA_ROPE) ** -0.5)
    rows = row0 + lax.broadcasted_iota(jnp.int32, s.shape, 0)
    cols = lax.broadcasted_iota(jnp.int32, s.shape, 1)
    s = jnp.where(rows >= cols, s, NEG_BIG)
    return (ops.b.nn(_softmax(s), v),)


def _silu(x):
    return x * jax.nn.sigmoid(x)


def fn_gdn_prep(ops, row0, *t):
    nh = len(t) // 3
    qs, ks, vs = [], [], []
    for qc, kc, vc in zip(t[:nh], t[nh:2 * nh], t[2 * nh:]):
        q, k = _silu(qc), _silu(kc)
        qs.append(q * lax.rsqrt(jnp.sum(q * q, -1, keepdims=True) + EPS) * (GDN_DK ** -0.5))
        ks.append(k * lax.rsqrt(jnp.sum(k * k, -1, keepdims=True) + EPS))
        vs.append(_silu(vc))
    return tuple(qs + ks + vs)


def fn_gdn_gates(ops, row0, ba, alog, dtb):
    width = GDN_HEADS * GDN_DK
    beta = jax.nn.sigmoid(ba)
    z = ba + dtb
    softplus = jnp.maximum(z, 0.0) + jnp.log1p(jnp.exp(-jnp.abs(z)))
    g = -jnp.exp(alog) * softplus
    r = lax.broadcasted_iota(jnp.int32, (LANES, width), 0)
    c = lax.broadcasted_iota(jnp.int32, (LANES, width), 1) // GDN_DK
    e_beta = (r == c).astype(F32)
    e_g = (r == c + GDN_HEADS).astype(F32)
    return ops.h.nn(beta, e_beta), ops.h.nn(g, e_g)


def fn_gdn_out(ops, row0, *t):
    nh = (len(t) - 1) // 2
    g = t[-1]
    return tuple(_rms(o, g) * _silu(gate) for o, gate in zip(t[:nh], t[nh:2 * nh]))


def _shift_down(x, d, t_idx):
    if d == 0:
        return x
    return jnp.where(t_idx >= d, pltpu.roll(x, d, axis=0), 0.0)


def _shift_up(x, d, t_idx):
    if d == 0:
        return x
    n = x.shape[0]
    return jnp.where(t_idx < n - d, pltpu.roll(x, n - d, axis=0), 0.0)


def conv_fwd(x, w, name):
    s, c = x.shape
    kw = w.shape[0]
    tc = _pick(c, (256, 128))

    def body(x_ref, w_ref, y_ref):
        xv = x_ref[...]
        t_idx = lax.broadcasted_iota(jnp.int32, xv.shape, 0)
        acc = jnp.zeros_like(xv)
        for j in range(kw):
            acc = acc + w_ref[j:j + 1, :] * _shift_down(xv, kw - 1 - j, t_idx)
        y_ref[...] = acc

    return pl.pallas_call(
        body, name=name, out_shape=jax.ShapeDtypeStruct((s, c), F32), grid=(c // tc,),
        in_specs=[pl.BlockSpec((s, tc), lambda i: (0, i)), pl.BlockSpec((kw, tc), lambda i: (0, i))],
        out_specs=pl.BlockSpec((s, tc), lambda i: (0, i)),
        compiler_params=_params(("parallel",)),
    )(x, w)


def conv_bwd(x, w, dy, name):
    s, c = x.shape
    kw = w.shape[0]
    tc = _pick(c, (256, 128))

    def body(x_ref, w_ref, dy_ref, dx_ref, dw_ref):
        xv, dyv = x_ref[...], dy_ref[...]
        t_idx = lax.broadcasted_iota(jnp.int32, xv.shape, 0)
        dx = jnp.zeros_like(xv)
        for j in range(kw):
            d = kw - 1 - j
            dx = dx + w_ref[j:j + 1, :] * _shift_up(dyv, d, t_idx)
            dw_ref[j:j + 1, :] = jnp.sum(dyv * _shift_down(xv, d, t_idx), axis=0, keepdims=True)
        dx_ref[...] = dx

    return pl.pallas_call(
        body, name=name,
        out_shape=[jax.ShapeDtypeStruct((s, c), F32), jax.ShapeDtypeStruct((kw, c), F32)],
        grid=(c // tc,),
        in_specs=[pl.BlockSpec((s, tc), lambda i: (0, i)), pl.BlockSpec((kw, tc), lambda i: (0, i)),
                  pl.BlockSpec((s, tc), lambda i: (0, i))],
        out_specs=[pl.BlockSpec((s, tc), lambda i: (0, i)), pl.BlockSpec((kw, tc), lambda i: (0, i))],
        compiler_params=_params(("parallel",)),
    )(x, w, dy)


def make_conv(name):
    @jax.custom_vjp
    def op(x, w):
        return conv_fwd(x, w, name + "_f")

    def fwd(x, w):
        return op(x, w), (x, w)

    def bwd(saved, dy):
        dx, dw = conv_bwd(saved[0], saved[1], dy, name + "_b")
        return dx, dw

    op.defvjp(fwd, bwd)
    return op


def _gdn_consts():
    c, d = GDN_CHUNK, GDN_DK
    i = lax.broadcasted_iota(jnp.int32, (c, c), 0)
    j = lax.broadcasted_iota(jnp.int32, (c, c), 1)
    tri = i >= j
    return dict(
        tri=tri, strict=i > j,
        tri_f=tri.astype(F32),
        eye=(i == j).astype(F32),
        lane0=(lax.broadcasted_iota(jnp.int32, (c, d), 1) == 0).astype(F32),
        last_row=(lax.broadcasted_iota(jnp.int32, (c, d), 0) == c - 1).astype(F32),
    )


def _gdn_chunk(ops, q, k, v, g, beta, state):
    b, m, sel = ops.bb, ops.bm, ops.bs
    nh, c, d = q.shape[0], GDN_CHUNK, GDN_DK
    k_ = _gdn_consts()

    def per_head(a):
        return jnp.broadcast_to(a, (nh,) + a.shape)

    gc = sel.sel_nn(per_head(k_["tri_f"]), g)
    col = jnp.broadcast_to(jnp.sum(gc * k_["lane0"], axis=2, keepdims=True), (nh, c, c))
    row = sel.sel_nt(per_head(k_["lane0"]), gc)
    decay = jnp.where(k_["tri"], jnp.exp(jnp.where(k_["tri"], col - row, 0.0)), 0.0)
    kb = k * beta
    mm_ = jnp.where(k_["strict"], b.nt(kb, k) * decay, 0.0)
    p = -mm_
    t = k_["eye"] + p
    for _ in range(int(math.log2(GDN_CHUNK)) - 1):
        p = m.nn(p, p)
        t = t + m.nn(t, p)
    egc = jnp.exp(gc)
    u = b.nn(t, v * beta)
    w = b.nn(t, kb * egc)
    attn = b.nt(q, k) * decay
    v_new = u - b.nn(w, state)
    o = b.nn(q * egc, state) + b.nn(attn, v_new)
    g_last = jnp.sum(gc * k_["last_row"], axis=1, keepdims=True)
    new_state = (state * jnp.exp(jnp.broadcast_to(g_last, (nh, d, d)))
                 + b.tn(k * jnp.exp(jnp.broadcast_to(g_last, (nh, c, d)) - gc), v_new))
    return o, new_state


GDN_HEAD_GROUP = 8
GDN_TILE_CHUNKS = 4


def _heads_of(ref, rows, n_heads):
    d = GDN_DK
    return jnp.stack([ref[rows, h * d:(h + 1) * d] for h in range(n_heads)])


def _gdn_specs(s, reverse):
    d, hg = GDN_DK, GDN_HEAD_GROUP
    tile = min(GDN_TILE_CHUNKS * GDN_CHUNK, s)
    n_tiles = s // tile
    t_of = (lambda t: n_tiles - 1 - t) if reverse else (lambda t: t)
    seq = pl.BlockSpec((tile, hg * d), lambda grp, t: (t_of(t), grp))
    st = pl.BlockSpec((hg, tile // GDN_CHUNK, d, d), lambda grp, t: (grp, t_of(t), 0, 0))
    return seq, st, tile, n_tiles


def gdn_fwd(q, k, v, g, beta, name):
    s = q.shape[0]
    d, hg = GDN_DK, GDN_HEAD_GROUP
    seq, st, tile, n_tiles = _gdn_specs(s, False)

    def body(q_ref, k_ref, v_ref, g_ref, b_ref, o_ref, st_ref, state_scr):
        @pl.when(pl.program_id(1) == 0)
        def _():
            state_scr[...] = jnp.zeros_like(state_scr)

        def step(ci, carry):
            rows = pl.ds(pl.multiple_of(ci * GDN_CHUNK, GDN_CHUNK), GDN_CHUNK)
            state = state_scr[...]
            for h in range(hg):
                st_ref[h, ci] = state[h]
            o, new_state = _gdn_chunk(_PLAIN, *[_heads_of(r, rows, hg) for r in (q_ref, k_ref, v_ref, g_ref, b_ref)],
                                      state)
            for h in range(hg):
                o_ref[rows, h * d:(h + 1) * d] = o[h]
            state_scr[...] = new_state
            return carry

        lax.fori_loop(0, tile // GDN_CHUNK, step, 0)

    return pl.pallas_call(
        body, name=name,
        out_shape=[jax.ShapeDtypeStruct(q.shape, F32),
                   jax.ShapeDtypeStruct((GDN_HEADS, s // GDN_CHUNK, d, d), F32)],
        grid=(GDN_HEADS // hg, n_tiles), in_specs=[seq] * 5, out_specs=[seq, st],
        scratch_shapes=[pltpu.VMEM((hg, d, d), F32)],
        compiler_params=_params(("parallel", "arbitrary")),
    )(q, k, v, g, beta)


def gdn_bwd(q, k, v, g, beta, states, do, name):
    s = q.shape[0]
    d, hg = GDN_DK, GDN_HEAD_GROUP
    seq, st, tile, n_tiles = _gdn_specs(s, True)
    tile_chunks = tile // GDN_CHUNK

    def body(q_ref, k_ref, v_ref, g_ref, b_ref, st_ref, do_ref, dq_ref, dk_ref, dv_ref, dg_ref, db_ref, dstate_scr):
        @pl.when(pl.program_id(1) == 0)
        def _():
            dstate_scr[...] = jnp.zeros_like(dstate_scr)

        def step(it, carry):
            ci = tile_chunks - 1 - it
            rows = pl.ds(pl.multiple_of(ci * GDN_CHUNK, GDN_CHUNK), GDN_CHUNK)
            prim = [_heads_of(r, rows, hg) for r in (q_ref, k_ref, v_ref, g_ref, b_ref)]
            prim.append(jnp.stack([st_ref[h, ci] for h in range(hg)]))
            _, vjp = jax.vjp(functools.partial(_gdn_chunk, _DIFF), *prim)
            grads = vjp((_heads_of(do_ref, rows, hg), dstate_scr[...]))
            for g_ref_out, gr in zip((dq_ref, dk_ref, dv_ref, dg_ref, db_ref), grads[:5]):
                for h in range(hg):
                    g_ref_out[rows, h * d:(h + 1) * d] = gr[h]
            dstate_scr[...] = grads[5]
            return carry

        lax.fori_loop(0, tile_chunks, step, 0)

    return pl.pallas_call(
        body, name=name,
        out_shape=[jax.ShapeDtypeStruct(q.shape, F32)] * 5,
        grid=(GDN_HEADS // hg, n_tiles), in_specs=[seq] * 5 + [st, seq], out_specs=[seq] * 5,
        scratch_shapes=[pltpu.VMEM((hg, d, d), F32)],
        compiler_params=_params(("parallel", "arbitrary")),
    )(q, k, v, g, beta, states, do)


def make_gdn(name):
    @jax.custom_vjp
    def op(q, k, v, g, beta):
        return gdn_fwd(q, k, v, g, beta, name + "_f")[0]

    def fwd(q, k, v, g, beta):
        o, states = gdn_fwd(q, k, v, g, beta, name + "_f")
        return o, (q, k, v, g, beta, states)

    def bwd(saved, do):
        return tuple(gdn_bwd(*saved, do, name + "_b"))

    op.defvjp(fwd, bwd)
    return op


def loss_head(x, g, target, name):
    s, d = x.shape
    tm = min(256, s)

    def body(x_ref, g_ref, t_ref, loss_ref, dx_ref, dg_ref):
        tgt = t_ref[...]

        def f(xv, gv):
            err = _rms(xv, gv) - tgt
            per_row = jnp.mean(err * err, axis=-1, keepdims=True)
            return 0.5 * jnp.sum(per_row, axis=0, keepdims=True)

        val, vjp = jax.vjp(f, x_ref[...], g_ref[...])
        dx, dg = vjp(jnp.ones((1, 1), F32))
        dx_ref[...] = dx
        first = pl.program_id(0) == 0

        @pl.when(first)
        def _():
            dg_ref[...] = dg
            loss_ref[...] = jnp.broadcast_to(val, loss_ref.shape)

        @pl.when(jnp.logical_not(first))
        def _():
            dg_ref[...] += dg
            loss_ref[...] += jnp.broadcast_to(val, loss_ref.shape)

    row = pl.BlockSpec((tm, d), lambda i: (i, 0))
    vec = pl.BlockSpec((1, d), lambda i: (0, 0))
    return pl.pallas_call(
        body, name=name,
        out_shape=[jax.ShapeDtypeStruct((1, LANES), F32), jax.ShapeDtypeStruct((s, d), F32),
                   jax.ShapeDtypeStruct((1, d), F32)],
        grid=(s // tm,), in_specs=[row, vec, row],
        out_specs=[pl.BlockSpec((1, LANES), lambda i: (0, 0)), row, vec],
        compiler_params=_params(("arbitrary",)),
    )(x, g, target)


def adamw(g8, w, m, v, layer, prev, name):
    n_layers, rows, width = w.shape
    tr = _pick(rows, (256, 128, 64, 32, 16, 8))

    def body(g_ref, w_ref, m_ref, v_ref, *rest):
        go_ref, d_ref, mo_ref, vo_ref = rest[-4:]
        g = g_ref[0].astype(F32)
        for p in range(1, N_DEV):
            g = g + g_ref[p].astype(F32)
        m_new = ADAM_B1 * m_ref[...] + (1.0 - ADAM_B1) * g
        v_new = ADAM_B2 * v_ref[...] + (1.0 - ADAM_B2) * (g * g)
        m_hat = m_new / (1.0 - ADAM_B1 ** ADAM_STEP)
        v_hat = v_new / (1.0 - ADAM_B2 ** ADAM_STEP)
        go_ref[...] = g
        d_ref[...] = -ADAM_LR * (m_hat / (jnp.sqrt(v_hat) + ADAM_EPS) + ADAM_WD * w_ref[...])
        mo_ref[...] = m_new
        vo_ref[...] = v_new

    blk = pl.BlockSpec((None, tr, width), lambda i: (layer, i, 0))
    carried = list(prev) if prev is not None else []
    return pl.pallas_call(
        body, name=name, out_shape=[jax.ShapeDtypeStruct((n_layers, rows, width), F32)] * 4,
        grid=(rows // tr,),
        in_specs=[pl.BlockSpec((N_DEV, tr, width), lambda i: (0, i, 0)), blk, blk, blk]
        + [pl.BlockSpec(memory_space=pl.ANY)] * len(carried),
        out_specs=[blk] * 4,
        input_output_aliases={4 + j: j for j in range(len(carried))},
        compiler_params=_params(("parallel",)),
    )(g8, w, m, v, *carried)


_HBM = pl.BlockSpec(memory_space=pltpu.HBM)
_SEM = pl.BlockSpec(memory_space=pltpu.SEMAPHORE)
_EFFECT = pltpu.SideEffectType.DATAFLOW_SIDE_EFFECTING


def _exchange_copies(mode, src_refs, land_refs, send_sems, recv_sems, local_sems):
    x, y, c = lax.axis_index("x"), lax.axis_index("y"), lax.axis_index("c")
    me = 4 * x + 2 * y + c
    n = len(src_refs)

    def src(k, p):
        return src_refs[k] if mode == "gather" else src_refs[k].at[p]

    local = [pltpu.make_async_copy(src(k, me), land_refs[k].at[me], local_sems.at[k]) for k in range(n)]
    sends, recvs = [], []
    for k in range(n):
        for r in range(1, N_DEV):
            px = (1 - x) if r & 4 else x
            py = (1 - y) if r & 2 else y
            pc = (1 - c) if r & 1 else c
            p = 4 * px + 2 * py + pc
            sem = k * (N_DEV - 1) + r - 1
            sends.append(pltpu.make_async_remote_copy(
                src_ref=src(k, p), dst_ref=land_refs[k].at[me],
                send_sem=send_sems.at[sem], recv_sem=recv_sems.at[sem],
                device_id=(px, py, pc), device_id_type=pl.DeviceIdType.MESH))
            recvs.append(pltpu.make_async_remote_copy(
                src_ref=src(k, p), dst_ref=land_refs[k].at[p],
                send_sem=send_sems.at[sem], recv_sem=recv_sems.at[sem],
                device_id=(px, py, pc), device_id_type=pl.DeviceIdType.MESH))
    return local, sends, recvs


def exchange_start(mode, arrays, name):
    n = len(arrays)
    land_shapes = [((N_DEV,) + tuple(a.shape)) if mode == "gather" else tuple(a.shape) for a in arrays]
    lands = [pltpu.with_memory_space_constraint(lax.empty(shp, a.dtype), pltpu.HBM)
             for shp, a in zip(land_shapes, arrays)]
    srcs = [pltpu.with_memory_space_constraint(a, pltpu.HBM) for a in arrays]

    def body(*refs):
        src_refs, land_refs = refs[:n], refs[n:2 * n]
        send_sems, recv_sems, local_sems = refs[2 * n:2 * n + 3]
        token = refs[-1]
        local, sends, _ = _exchange_copies(mode, src_refs, land_refs, send_sems, recv_sems, local_sems)
        for cp in local + sends:
            cp.start()
        token[...] = jnp.zeros_like(token)

    n_sem = n * (N_DEV - 1)
    out = pl.pallas_call(
        body, name=name,
        out_shape=(pltpu.SemaphoreType.DMA((n_sem,)), pltpu.SemaphoreType.DMA((n_sem,)),
                   pltpu.SemaphoreType.DMA((n,)),
                   *[pltpu.HBM(a.shape, a.dtype) for a in arrays],
                   *[pltpu.HBM(shp, a.dtype) for shp, a in zip(land_shapes, arrays)],
                   jax.ShapeDtypeStruct((8, LANES), F32)),
        in_specs=[_HBM] * (2 * n),
        out_specs=(_SEM, _SEM, _SEM, *[_HBM] * (2 * n), pl.BlockSpec(memory_space=pltpu.VMEM)),
        input_output_aliases={i: 3 + i for i in range(2 * n)},
        compiler_params=pltpu.CompilerParams(has_side_effects=_EFFECT),
    )(*srcs, *lands)
    handle = dict(mode=mode, sems=out[:3], srcs=out[3:3 + n], lands=out[3 + n:3 + 2 * n])
    return handle, out[-1]


def exchange_wait(handle, after, name):
    mode, srcs, lands = handle["mode"], list(handle["srcs"]), list(handle["lands"])
    n = len(srcs)

    def body(*refs):
        src_refs, land_refs = refs[:n], refs[n:2 * n]
        send_sems, recv_sems, local_sems = refs[2 * n:2 * n + 3]
        local, sends, recvs = _exchange_copies(mode, src_refs, land_refs, send_sems, recv_sems, local_sems)
        for cp in sends:
            cp.wait_send()
        for cp in recvs:
            cp.wait_recv()
        for cp in local:
            cp.wait()

    out = pl.pallas_call(
        body, name=name,
        out_shape=(*[pltpu.HBM(a.shape, a.dtype) for a in srcs], *[pltpu.HBM(a.shape, a.dtype) for a in lands]),
        in_specs=[_HBM] * (2 * n) + [_SEM] * 3 + [pl.BlockSpec(memory_space=pl.ANY)],
        out_specs=tuple([_HBM] * (2 * n)),
        input_output_aliases={i: i for i in range(2 * n)},
        compiler_params=pltpu.CompilerParams(has_side_effects=_EFFECT),
    )(*srcs, *lands, *handle["sems"], after)
    return list(out[n:])


BIG = ["mla_w_in", "mla_w_uq", "mla_w_ukv", "mla_w_o", "gdn_w_in", "gdn_w_o", "sc_w_in", "sc_w_o",
       "xa_w_q", "xa_w_kv", "xa_w_o", "mlp_w1", "mlp_w2"]
TINY = [("mla_q_norm", 1), ("mla_kv_norm", 1), ("gdn_conv_w", 2), ("sc_conv_w", 2)]
REPL = ["gdn_a_log", "gdn_dt_bias", "gdn_o_norm", "norm_mix", "norm_mem", "norm_mlp", "mem_norm", "final_norm"]
WEIGHTS = ["mla_w_in", "mla_q_norm", "mla_kv_norm", "mla_w_uq", "mla_w_ukv", "mla_w_o", "gdn_w_in",
           "gdn_conv_w", "gdn_a_log", "gdn_dt_bias", "gdn_o_norm", "gdn_w_o", "sc_w_in", "sc_conv_w",
           "sc_w_o", "norm_mix", "norm_mem", "norm_mlp", "xa_w_q", "xa_w_kv", "xa_w_o", "mlp_w1",
           "mlp_w2", "mem_norm", "final_norm"]
MIXER_WEIGHTS = (["mla_w_in", "mla_w_uq", "mla_w_ukv", "mla_w_o"], ["gdn_w_in", "gdn_w_o"], ["sc_w_in", "sc_w_o"])
MIXER_PARAMS = (["norm_mix", "mla_q_norm", "mla_kv_norm"],
                ["norm_mix", "gdn_conv_w", "gdn_a_log", "gdn_dt_bias", "gdn_o_norm"],
                ["norm_mix", "sc_conv_w"])


def from_shards(a8, axis):
    a = jnp.moveaxis(a8, 0, axis)
    shp = a.shape
    return a.reshape(shp[:axis] + (shp[axis] * shp[axis + 1],) + shp[axis + 2:])


def pack_rows(flat_list, width, row_mult):
    total = sum(a.shape[-1] for a in flat_list)
    rows = -(-total // width)
    rows = -(-rows // row_mult) * row_mult
    pad = rows * width - total
    parts = list(flat_list)
    if pad:
        parts.append(jnp.zeros((pad,), flat_list[0].dtype))
    return jnp.concatenate(parts, axis=-1).reshape(rows, width)


def unpack_rows(packed, shapes):
    lead = packed.shape[:-2]
    flat = packed.reshape(lead + (-1,))
    out, off = [], 0
    for shp in shapes:
        n = math.prod(shp)
        out.append(flat[..., off:off + n].reshape(lead + tuple(shp)))
        off += n
    return out


def _swap_halves(w):
    half = w.shape[-1] // 2
    return jnp.concatenate([w[..., half:], w[..., :half]], axis=-1)


def _pad_last(w, n):
    return jnp.pad(w, [(0, 0)] * (w.ndim - 1) + [(0, n - w.shape[-1])])


def _unblock(w8):
    return jnp.transpose(w8, (1, 0, 2)).reshape(w8.shape[1], -1)


def _stack_rows(w8):
    return w8.reshape(-1, w8.shape[-1])


def rms_op(name, rows, d, out_dtype, tm=512):
    return make_tile_op(fn_rms, name, ["row", "par"], [True, True], [("row", d, out_dtype)], rows, min(tm, rows))


def seg_memory(p, mem):
    return rms_op("rms_memory", mem.shape[0], mem.shape[1], BF16)(mem, p["mem_norm"].reshape(1, -1))[0]


def seg_mixer(i, wts, p, x, rope_c, rope_s):
    s, d = x.shape
    j, kind = i // N_MIXERS, i % N_MIXERS
    tag = f"l{i}"
    hd = MLA_NOPE
    h = rms_op(tag + "_rms_mix", s, d, BF16)(x, p["norm_mix"][i].reshape(1, d))[0]
    if kind == 0:
        w_in = _stack_rows(wts["mla_w_in"])
        w_cq = w_in[:, :MLA_Q_RANK]
        w_ckv = w_in[:, MLA_Q_RANK:MLA_Q_RANK + MLA_KV_RANK]
        w_kr = w_in[:, MLA_Q_RANK + MLA_KV_RANK:]
        w_z = jnp.concatenate([w_cq, w_ckv, _pad_last(w_kr, hd), _pad_last(_swap_halves(w_kr), hd)], axis=-1)
        z = make_mm(tag + "_mla_in", F32)(h, w_z)
        c_q, c_kv = z[:, :MLA_Q_RANK], z[:, MLA_Q_RANK:MLA_Q_RANK + MLA_KV_RANK]
        kr_raw, kr_swp = z[:, -2 * hd:-hd], z[:, -hd:]
        c_qn = rms_op(tag + "_rms_q", s, MLA_Q_RANK, BF16)(c_q, p["mla_q_norm"][j].reshape(1, -1))[0]
        c_kvn = rms_op(tag + "_rms_kv", s, MLA_KV_RANK, BF16)(c_kv, p["mla_kv_norm"][j].reshape(1, -1))[0]
        w_uq8 = wts["mla_w_uq"]
        w_qn = _unblock(w_uq8[:, :, :MLA_NOPE])
        w_qr = w_uq8[:, :, MLA_NOPE:]
        w_qr_p = _unblock(_pad_last(w_qr, hd))
        w_qr_s = _unblock(_pad_last(_swap_halves(w_qr), hd))
        q_nope = make_mm(tag + "_mla_uq_n", BF16)(c_qn, w_qn)
        q_raw = make_mm(tag + "_mla_uq_r", F32)(c_qn, w_qr_p)
        q_swp = make_mm(tag + "_mla_uq_s", F32)(c_qn, w_qr_s)
        nq = MLA_HEADS * hd
        kv = make_mm(tag + "_mla_ukv", BF16, blocked=True)(c_kvn, wts["mla_w_ukv"])
        heads_row = ("row", hd, MLA_HEADS)
        q_rope = make_tile_op(fn_rope, tag + "_rope_q", [heads_row, heads_row, "row", "row"],
                              [True, True, False, False], [(heads_row, nq, F32)], s, 512)(
            q_raw, q_swp, rope_c, rope_s)[0]
        k_rope = make_tile_op(fn_rope, tag + "_rope_k", ["row", "row", "row", "row"],
                              [True, True, False, False], [("row", hd, F32)], s, 1024)(
            kr_raw, kr_swp, rope_c, rope_s)[0]
        o = make_tile_op(fn_mla_attn, tag + "_mla_attn",
                         [("rowh", hd), ("rowh", hd), ("parh", hd, 2), "par"],
                         [True] * 4, [(("rowh", hd), nq, BF16)], s, 256, MLA_HEADS)(
            q_nope, q_rope, kv, k_rope)[0]
        return make_mm(tag + "_mla_o", F32, with_res=True)(x, o, _stack_rows(wts["mla_w_o"]))
    if kind == 1:
        ng = GDN_HEADS * GDN_DK
        w_in = _unblock(wts["gdn_w_in"])
        cw = p["gdn_conv_w"][j]
        conv_out = []
        for part, nm in enumerate(("q", "k", "v")):
            cols = slice(part * ng, (part + 1) * ng)
            pre = make_mm(f"{tag}_gdn_in_{nm}", F32)(h, w_in[:, cols])
            conv_out.append(make_conv(f"{tag}_gdn_conv_{nm}")(pre, cw[:, cols]))
        gate = make_mm(tag + "_gdn_in_g", F32)(h, w_in[:, 3 * ng:4 * ng])
        ba = make_mm(tag + "_gdn_in_ba", F32)(h, _pad_last(w_in[:, 4 * ng:], LANES))
        heads_row = ("row", GDN_DK, GDN_HEADS)
        q, k, v = make_tile_op(fn_gdn_prep, tag + "_gdn_prep", [heads_row] * 3, [True] * 3,
                               [(heads_row, ng, F32)] * 3, s, 512)(*conv_out)
        alog = jnp.pad(p["gdn_a_log"][j].reshape(1, -1), ((0, 0), (GDN_HEADS, LANES - 2 * GDN_HEADS)))
        dtb = jnp.pad(p["gdn_dt_bias"][j].reshape(1, -1), ((0, 0), (GDN_HEADS, LANES - 2 * GDN_HEADS)))
        beta_b, g_b = make_tile_op(fn_gdn_gates, tag + "_gdn_gates", ["row", "par", "par"], [True] * 3,
                                   [("row", ng, F32)] * 2, s, 512)(ba, alog, dtb)
        o = make_gdn(tag + "_gdn_core")(q, k, v, g_b, beta_b)
        o = make_tile_op(fn_gdn_out, tag + "_gdn_out", [heads_row, heads_row, "par"],
                         [True] * 3, [(heads_row, ng, BF16)], s, 512)(
            o, gate, p["gdn_o_norm"][j].reshape(1, -1))[0]
        return make_mm(tag + "_gdn_o", F32, with_res=True)(x, o, _stack_rows(wts["gdn_w_o"]))
    w_in = _unblock(wts["sc_w_in"])
    b_gate = make_mm(tag + "_sc_in_b", F32)(h, w_in[:, :d])
    c_gate = make_mm(tag + "_sc_in_c", F32)(h, w_in[:, d:2 * d])
    u = make_mm(tag + "_sc_in_u", F32)(h, w_in[:, 2 * d:])
    cu = make_tile_op(fn_mul, tag + "_sc_cu", ["row", "row"], [True, True], [("row", d, F32)], s, 512)(
        c_gate, u)[0]
    cv = make_conv(tag + "_sc_conv")(cu, p["sc_conv_w"][j])
    yv = make_tile_op(fn_mul, tag + "_sc_gate", ["row", "row"], [True, True], [("row", d, BF16)], s, 512)(
        b_gate, cv)[0]
    return make_mm(tag + "_sc_o", F32, with_res=True)(x, yv, _stack_rows(wts["sc_w_o"]))


def seg_xattn(i, wts, p, x, mem_n):
    s, d = x.shape
    tag = f"l{i}"
    hx = rms_op(tag + "_rms_mem", s, d, BF16)(x, p["norm_mem"][i].reshape(1, d))[0]
    q = make_mm(tag + "_xa_q", BF16)(hx, _stack_rows(wts["xa_w_q"]))
    kv = make_mm(tag + "_xa_kv", BF16, blocked=True)(mem_n, wts["xa_w_kv"])
    o = make_tile_op(fn_xattn, tag + "_xattn",
                     [("rowh", X_HEAD_DIM), ("parh", X_HEAD_DIM), ("parh", X_HEAD_DIM)], [True] * 3,
                     [(("rowh", X_HEAD_DIM), d, BF16)], s, 1024, X_HEADS)(q, kv[:, :d], kv[:, d:])[0]
    return make_mm(tag + "_xa_o", F32, with_res=True)(x, o, _stack_rows(wts["xa_w_o"]))


def seg_mlp(i, wts, p, x):
    s, d = x.shape
    tag = f"l{i}"
    hm = rms_op(tag + "_rms_mlp", s, d, BF16)(x, p["norm_mlp"][i].reshape(1, d))[0]
    a = make_mm(tag + "_mlp_1", BF16, blocked=True)(hm, wts["mlp_w1"])
    bsq = make_tile_op(fn_relu2, tag + "_relu2", ["row"], [True], [("row", a.shape[1], BF16)], s, 256)(a)[0]
    return make_mm(tag + "_mlp_2", F32, with_res=True)(x, bsq, _stack_rows(wts["mlp_w2"]))


def segments():
    segs = []
    for i in range(DEPTH):
        j, kind = i // N_MIXERS, i % N_MIXERS
        segs.append((f"l{i}_mixer", [(n, j) for n in MIXER_WEIGHTS[kind]], MIXER_PARAMS[kind], "mixer"))
        segs.append((f"l{i}_xattn", [(n, i) for n in ("xa_w_q", "xa_w_kv", "xa_w_o")], ["norm_mem"], "xattn"))
        segs.append((f"l{i}_mlp", [(n, i) for n in ("mlp_w1", "mlp_w2")], ["norm_mlp"], "mlp"))
    return segs


def run_segment(index, kind, wts, p, x, mem_n, rope_c, rope_s):
    layer = index // 3
    if kind == "mixer":
        return seg_mixer(layer, wts, p, x, rope_c, rope_s)
    if kind == "xattn":
        return seg_xattn(layer, wts, p, x, mem_n)
    return seg_mlp(layer, wts, p, x)


def rope_tables(positions):
    inv_freq = ROPE_THETA ** (-jnp.arange(0, MLA_ROPE, 2, dtype=F32) / MLA_ROPE)
    ang = positions.astype(F32)[:, None] * inv_freq
    cos, sin = jnp.cos(ang), jnp.sin(ang)
    zeros = jnp.zeros((positions.shape[0], MLA_NOPE - MLA_ROPE), F32)
    return jnp.concatenate([cos, cos, zeros], axis=-1), jnp.concatenate([-sin, sin, zeros], axis=-1)


def kernel(x, mem, positions, mla_w_in, mla_q_norm, mla_kv_norm, mla_w_uq, mla_w_ukv, mla_w_o, gdn_w_in, gdn_conv_w, gdn_a_log, gdn_dt_bias, gdn_o_norm, gdn_w_o, sc_w_in, sc_conv_w, sc_w_o, norm_mix, norm_mem, norm_mlp, xa_w_q, xa_w_kv, xa_w_o, mlp_w1, mlp_w2, mem_norm, final_norm, loss_target, m_mla_w_in, m_mla_q_norm, m_mla_kv_norm, m_mla_w_uq, m_mla_w_ukv, m_mla_w_o, m_gdn_w_in, m_gdn_conv_w, m_gdn_a_log, m_gdn_dt_bias, m_gdn_o_norm, m_gdn_w_o, m_sc_w_in, m_sc_conv_w, m_sc_w_o, m_norm_mix, m_norm_mem, m_norm_mlp, m_xa_w_q, m_xa_w_kv, m_xa_w_o, m_mlp_w1, m_mlp_w2, m_mem_norm, m_final_norm, v_mla_w_in, v_mla_q_norm, v_mla_kv_norm, v_mla_w_uq, v_mla_w_ukv, v_mla_w_o, v_gdn_w_in, v_gdn_conv_w, v_gdn_a_log, v_gdn_dt_bias, v_gdn_o_norm, v_gdn_w_o, v_sc_w_in, v_sc_conv_w, v_sc_w_o, v_norm_mix, v_norm_mem, v_norm_mlp, v_xa_w_q, v_xa_w_kv, v_xa_w_o, v_mlp_w1, v_mlp_w2, v_mem_norm, v_final_norm):
    args = locals()
    w_loc = {n: args[n] for n in WEIGHTS}
    m_loc = {n: args["m_" + n] for n in WEIGHTS}
    v_loc = {n: args["v_" + n] for n in WEIGHTS}
    me = 4 * lax.axis_index("x") + 2 * lax.axis_index("y") + lax.axis_index("c")
    segs = segments()

    w16 = {n: w_loc[n].astype(BF16) for n in BIG}
    tiny_pack = pack_rows([w_loc[n].reshape(-1) for n, _ in TINY], LANES, 8)
    tiny_handle, _ = exchange_start("gather", [tiny_pack], "gather_start_tiny")
    gather_handles = []
    for tag, units, _, _ in segs:
        handle, _ = exchange_start("gather", [w16[n][layer] for n, layer in units], f"gather_start_{tag}")
        gather_handles.append(handle)

    x_cur = x[0]
    rope_c, rope_s = rope_tables(positions[0])
    tiny_all = exchange_wait(tiny_handle, x_cur, "gather_wait_tiny")[0]
    params = {}
    for (n, ax), a8 in zip(TINY, unpack_rows(tiny_all, [w_loc[n].shape for n, _ in TINY])):
        params[n] = from_shards(a8, ax)
    for n in REPL:
        params[n] = w_loc[n]

    mem_n, vjp_memory = jax.vjp(lambda p_: seg_memory(p_, mem[0]), {"mem_norm": params["mem_norm"]})
    vjps = []
    for index, ((tag, units, p_names, kind), handle) in enumerate(zip(segs, gather_handles)):
        landed = exchange_wait(handle, x_cur, f"gather_wait_{tag}")
        wts = {n: a for (n, _), a in zip(units, landed)}
        p_seg = {n: params[n] for n in p_names}
        x_cur, vjp_seg = jax.vjp(
            lambda w_, p_, x_, m_, index=index, kind=kind: run_segment(index, kind, w_, p_, x_, m_, rope_c, rope_s),
            wts, p_seg, x_cur, mem_n)
        vjps.append(vjp_seg)

    loss_vec, g_x, d_final = loss_head(x_cur, params["final_norm"].reshape(1, -1), loss_target[0], "loss_head")

    grads = {n: jnp.zeros_like(params[n]) for n in params}
    grads["final_norm"] = d_final.reshape(-1)
    g_mem_n = jnp.zeros_like(mem_n)
    scatter_handles = []
    for (tag, units, _, _), vjp_seg in zip(reversed(segs), reversed(vjps)):
        g_wts, g_p, g_x, g_m = vjp_seg(g_x)
        for n, g in g_p.items():
            grads[n] = grads[n] + g
        g_mem_n = g_mem_n + g_m
        handle, token = exchange_start("scatter", [g_wts[n] for n, _ in units], f"scatter_start_{tag}")
        scatter_handles.append((units, handle))
        g_x = g_x + token[0, 0]
    grads["mem_norm"] = grads["mem_norm"] + vjp_memory(g_mem_n)[0]["mem_norm"]

    small_names = [n for n, _ in TINY] + REPL
    small_g = pack_rows([loss_vec[0, :1]] + [grads[n].astype(F32).reshape(-1) for n in small_names], PACK_W, 8)
    small_handle, _ = exchange_start("gather", [small_g], "gather_start_small_grads")

    g_recv = {}
    for units, handle in scatter_handles:
        landed = exchange_wait(handle, g_x, f"scatter_wait_{units[0][0]}_{units[0][1]}")
        g_recv.update(dict(zip(units, landed)))
    small_recv = exchange_wait(small_handle, g_x, "gather_wait_small_grads")[0]

    res = {}
    for n in BIG:
        outs = None
        for layer in range(w_loc[n].shape[0]):
            outs = adamw(g_recv[n, layer], w_loc[n], m_loc[n], v_loc[n], layer, outs, f"adamw_{n}_{layer}")
        for kind, a in zip(("grad", "delta", "m", "v"), outs):
            res[(kind, n)] = a

    def full_small(d):
        parts = [jnp.zeros((1,), F32)]
        for n, ax in TINY:
            full_shape = params[n].shape
            start = [0] * len(full_shape)
            start[ax] = me * d[n].shape[ax]
            parts.append(lax.dynamic_update_slice(jnp.zeros(full_shape, F32), d[n], start).reshape(-1))
        parts += [d[n].reshape(-1) for n in REPL]
        return pack_rows(parts, PACK_W, 8)

    outs_small = adamw(small_recv, full_small(w_loc)[None], full_small(m_loc)[None], full_small(v_loc)[None],
                       0, None, "adamw_small")
    small_shapes = [(1,)] + [params[n].shape for n, _ in TINY] + [w_loc[n].shape for n in REPL]
    loss = None
    for kind, packed in zip(("grad", "delta", "m", "v"), outs_small):
        parts = unpack_rows(packed[0], small_shapes)
        if kind == "grad":
            loss = parts[0][0]
        for (n, ax), a in zip(TINY, parts[1:1 + len(TINY)]):
            start = [0] * a.ndim
            start[ax] = me * w_loc[n].shape[ax]
            res[(kind, n)] = lax.dynamic_slice(a, start, w_loc[n].shape)
        for n, a in zip(REPL, parts[1 + len(TINY):]):
            res[(kind, n)] = a

    out = [loss, g_x[None]]
    for kind in ("grad", "delta", "m", "v"):
        out += [res[(kind, n)] for n in WEIGHTS]
    return tuple(out)
```

```python
import functools
import math

import jax
import jax.numpy as jnp
from jax import lax
from jax.experimental import pallas as pl
from jax.experimental.pallas import tpu as pltpu

F32 = jnp.float32
BF16 = jnp.bfloat16

N_DEV = 8
LANES = 128
EPS = 1e-6
ROPE_THETA = 10000.0
MLA_HEADS, MLA_NOPE, MLA_ROPE, MLA_V = 8, 128, 64, 128
MLA_Q_RANK, MLA_KV_RANK = 384, 256
GDN_HEADS, GDN_DK, GDN_CONV, GDN_CHUNK = 8, 128, 4, 64
X_HEADS, X_HEAD_DIM = 4, 256
DEPTH, N_MIXERS = 4, 3
ADAM_LR, ADAM_B1, ADAM_B2, ADAM_EPS, ADAM_WD, ADAM_STEP = 0.001, 0.9, 0.999, 1e-08, 0.01, 10
MLA_QUERY_GROUPS = 4
NEG_BIG = -1e30
PACK_W = 1024


_NN = (((1,), (0,)), ((), ()))
_NT = (((1,), (1,)), ((), ()))
_TN = (((0,), (0,)), ((), ()))
_NN3 = (((2,), (1,)), ((0,), (0,)))
_NT3 = (((2,), (2,)), ((0,), (0,)))
_TN3 = (((1,), (1,)), ((0,), (0,)))


def _dot(a, b, dims):
    return lax.dot_general(a, b, dims, preferred_element_type=F32)


def _hi_lo(x):
    hi = x.astype(BF16)
    return hi, (x - hi.astype(F32)).astype(BF16)


def _split3(x):
    hi = x.astype(BF16)
    r = x - hi.astype(F32)
    mid = r.astype(BF16)
    return hi, mid, (r - mid.astype(F32)).astype(BF16)


def _dg(a, b, dims, prec):
    if prec == "h":
        return lax.dot_general(a, b, dims, precision=lax.Precision.HIGHEST, preferred_element_type=F32)
    if prec == "m":
        a_hi, a_lo = _hi_lo(a)
        b_hi, b_lo = _hi_lo(b)
        return _dot(a_hi, b_hi, dims) + _dot(a_hi, b_lo, dims) + _dot(a_lo, b_hi, dims)
    return _dot(a.astype(BF16), b.astype(BF16), dims)


def _dg_sel(sel, x, dims, sel_first):
    s16 = sel.astype(BF16)
    parts = [(_dot(s16, piece, dims) if sel_first else _dot(piece, s16, dims)) for piece in _split3(x)]
    return parts[0] + parts[1] + parts[2]


class _Ops:
    def __init__(self, prec, differentiable, batched=False):
        d_nn, d_nt, d_tn = (_NN3, _NT3, _TN3) if batched else (_NN, _NT, _TN)

        def nn(a, b):
            return _dg(a, b, d_nn, prec)

        def nt(a, b):
            return _dg(a, b, d_nt, prec)

        def tn(a, b):
            return _dg(a, b, d_tn, prec)

        if differentiable:
            dnn = jax.custom_vjp(nn)
            dnn.defvjp(lambda a, b: (nn(a, b), (a, b)), lambda r, g: (nt(g, r[1]), tn(r[0], g)))
            dnt = jax.custom_vjp(nt)
            dnt.defvjp(lambda a, b: (nt(a, b), (a, b)), lambda r, g: (nn(g, r[1]), tn(g, r[0])))
            dtn = jax.custom_vjp(tn)
            dtn.defvjp(lambda a, b: (tn(a, b), (a, b)), lambda r, g: (nt(r[1], g), nn(r[0], g)))
            nn, nt, tn = dnn, dnt, dtn
        self.nn, self.nt, self.tn = nn, nt, tn


class _SelOps:
    def __init__(self, differentiable, batched=False):
        d_nn, d_nt, d_tn = (_NN3, _NT3, _TN3) if batched else (_NN, _NT, _TN)

        def sel_nn(sel, x):
            return _dg_sel(sel, x, d_nn, True)

        def sel_nt(sel, x):
            return _dg_sel(sel, x, d_nt, True)

        if differentiable:
            dnn = jax.custom_vjp(sel_nn)
            dnn.defvjp(lambda s, x: (sel_nn(s, x), s),
                       lambda s, g: (jnp.zeros_like(s), _dg_sel(s, g, d_tn, True)))
            dnt = jax.custom_vjp(sel_nt)
            dnt.defvjp(lambda s, x: (sel_nt(s, x), s),
                       lambda s, g: (jnp.zeros_like(s), _dg_sel(s, g, d_tn, False)))
            sel_nn, sel_nt = dnn, dnt
        self.sel_nn, self.sel_nt = sel_nn, sel_nt


class _OpSet:
    def __init__(self, differentiable):
        self.b = _Ops("b", differentiable)
        self.h = _Ops("h", differentiable)
        self.bb = _Ops("b", differentiable, batched=True)
        self.bm = _Ops("m", differentiable, batched=True)
        self.bs = _SelOps(differentiable, batched=True)


_PLAIN = _OpSet(False)
_DIFF = _OpSet(True)


def _params(sem):
    return pltpu.CompilerParams(dimension_semantics=sem)


BLOCK_BYTES = 4 * 1024 * 1024


def _pick(n, cands):
    for c in cands:
        if n % c == 0:
            return c
    return n


def _tile(n, cap):
    if n <= cap:
        return n
    return _pick(n, tuple(c for c in (2048, 1024, 768, 512, 384, 256, 128) if c <= cap))


def matmul(a, b, form, out_dtype, name, res=None, blocked=False):
    if form == "nn":
        m, k = a.shape
        k2, n = (b.shape[1], N_DEV * b.shape[2]) if blocked else b.shape
    elif form == "nt":
        m, k = a.shape
        n, k2 = (b.shape[1], N_DEV * b.shape[2]) if blocked else b.shape
    else:
        (k, m), (k2, n) = a.shape, b.shape
    assert k == k2, (a.shape, b.shape, form)
    tk = k if k <= 2048 else _tile(k, 1024)
    cb = nb = 1
    if blocked:
        cb = (k if form == "nt" else n) // N_DEV
        nb = _pick(N_DEV, tuple(c for c in (8, 4, 2, 1) if c * cb <= 1024))
    if blocked and form == "nt":
        tk = nb * cb
    if blocked and form != "nt":
        tn = nb * cb
    else:
        tn = _tile(n, min(1024, BLOCK_BYTES // (tk * b.dtype.itemsize)))
    out_elems = BLOCK_BYTES // 2 if (out_dtype == BF16 and res is None) else BLOCK_BYTES // 4
    tm = _tile(m, min(BLOCK_BYTES // (tk * a.dtype.itemsize), out_elems // tn))
    nk = k // tk
    dims = {"nn": _NN, "nt": _NT, "tn": _TN}[form]

    a_spec = {"nn": pl.BlockSpec((tm, tk), lambda i, j, kk: (i, kk)),
              "nt": pl.BlockSpec((tm, tk), lambda i, j, kk: (i, kk)),
              "tn": pl.BlockSpec((tk, tm), lambda i, j, kk: (kk, i))}[form]
    if blocked and form == "nn":
        b_spec = pl.BlockSpec((nb, tk, cb), lambda i, j, kk: (j, kk, 0))
    elif blocked and form == "nt":
        b_spec = pl.BlockSpec((nb, tn, cb), lambda i, j, kk: (kk, j, 0))
    else:
        b_spec = {"nn": pl.BlockSpec((tk, tn), lambda i, j, kk: (kk, j)),
                  "nt": pl.BlockSpec((tn, tk), lambda i, j, kk: (j, kk)),
                  "tn": pl.BlockSpec((tk, tn), lambda i, j, kk: (kk, j))}[form]
    c_spec = pl.BlockSpec((tm, tn), lambda i, j, kk: (i, j))
    out_shape = jax.ShapeDtypeStruct((m, n), out_dtype)
    o_spec = c_spec
    blocked_out = blocked and form == "tn"
    if blocked_out:
        out_shape = jax.ShapeDtypeStruct((N_DEV, m, cb), out_dtype)
        o_spec = pl.BlockSpec((nb, tm, cb), lambda i, j, kk: (j, i, 0))
    has_res = res is not None

    def body(*refs):
        a_ref, b_ref = refs[0], refs[1]
        r_ref = refs[2] if has_res else None
        o_ref = refs[3] if has_res else refs[2]
        a_val = a_ref[...].astype(BF16)
        if blocked and form == "nn":
            part = jnp.concatenate([_dot(a_val, b_ref[t].astype(BF16), dims) for t in range(nb)], axis=-1)
        elif blocked and form == "nt":
            part = _dot(a_val[:, :cb], b_ref[0].astype(BF16), dims)
            for t in range(1, nb):
                part = part + _dot(a_val[:, t * cb:(t + 1) * cb], b_ref[t].astype(BF16), dims)
        else:
            part = _dot(a_val, b_ref[...].astype(BF16), dims)

        def finish(acc):
            if has_res:
                acc = acc + r_ref[...].astype(F32)
            if blocked_out:
                for t in range(nb):
                    o_ref[t] = acc[:, t * cb:(t + 1) * cb].astype(out_dtype)
            else:
                o_ref[...] = acc.astype(out_dtype)

        if nk == 1:
            finish(part)
        else:
            acc_ref = refs[-1]
            kk = pl.program_id(2)

            @pl.when(kk == 0)
            def _():
                acc_ref[...] = part

            @pl.when(jnp.logical_and(kk > 0, kk < nk - 1))
            def _():
                acc_ref[...] += part

            @pl.when(kk == nk - 1)
            def _():
                finish(acc_ref[...] + part)

    in_specs = [a_spec, b_spec] + ([c_spec] if has_res else [])
    args = (a, b) + ((res,) if has_res else ())
    return pl.pallas_call(
        body, name=name,
        out_shape=out_shape,
        grid=(m // tm, n // tn, nk),
        in_specs=in_specs, out_specs=o_spec,
        scratch_shapes=[pltpu.VMEM((tm, tn), F32)] if nk > 1 else [],
        compiler_params=_params(("parallel", "parallel", "arbitrary")),
    )(*args)


def make_mm(name, out_dtype, with_res=False, blocked=False):
    def bwd_mm(a, w, g):
        da = matmul(g, w, "nt", a.dtype, name + "_da", blocked=blocked)
        dw = matmul(a, g, "tn", w.dtype, name + "_dw", blocked=blocked)
        return da, dw

    if with_res:
        @jax.custom_vjp
        def op(res, a, w):
            return matmul(a, w, "nn", out_dtype, name + "_f", res=res, blocked=blocked)

        def fwd(res, a, w):
            return op(res, a, w), (a, w)

        def bwd(saved, g):
            return (g,) + bwd_mm(*saved, g)
    else:
        @jax.custom_vjp
        def op(a, w):
            return matmul(a, w, "nn", out_dtype, name + "_f", blocked=blocked)

        def fwd(a, w):
            return op(a, w), (a, w)

        def bwd(saved, g):
            return bwd_mm(*saved, g)
    op.defvjp(fwd, bwd)
    return op


def _kind(k):
    if isinstance(k, str):
        return k, None, 1
    return k[0], k[1], (k[2] if len(k) > 2 else 1)


def _tile_spec(kind, shape, tm, heads):
    k, d, ns = _kind(kind)
    if k == "row":
        return pl.BlockSpec((tm, shape[1]), (lambda h, i: (i, 0)) if heads else (lambda i: (i, 0)))
    if k == "par":
        return pl.BlockSpec(tuple(shape), (lambda h, i: (0, 0)) if heads else (lambda i: (0, 0)))
    if k == "rowh":
        return pl.BlockSpec((tm, d * ns), lambda h, i: (i, h))
    if k == "parh":
        return pl.BlockSpec((shape[0], d * ns), lambda h, i: (0, h))
    raise ValueError(kind)


def _tile_grid(rows, tm, heads):
    n_rows = rows // tm
    return ((heads, n_rows) if heads else (n_rows,)), (1 if heads else 0)


def _split_vals(kinds, refs):
    vals, counts = [], []
    for kind, r in zip(kinds, refs):
        _, d, ns = _kind(kind)
        v = r[...].astype(F32)
        vals += [v] if ns == 1 else [v[:, p * d:(p + 1) * d] for p in range(ns)]
        counts.append(ns)
    return vals, counts


def tile_fwd(fn, name, kinds, args, outs, rows, tm, heads, row_base=0):
    grid, row_axis = _tile_grid(rows, tm, heads)
    n_in = len(args)
    out_shapes = [jax.ShapeDtypeStruct((rows, w), dt) for (_, w, dt) in outs]

    def body(*refs):
        vals, _ = _split_vals(kinds, refs[:n_in])
        row0 = row_base + pl.program_id(row_axis) * tm
        res = list(fn(_PLAIN, row0, *vals))
        for o_ref, (k, _, _) in zip(refs[n_in:], outs):
            pieces = [res.pop(0) for _ in range(_kind(k)[2])]
            v = pieces[0] if len(pieces) == 1 else jnp.concatenate(pieces, axis=-1)
            o_ref[...] = v.astype(o_ref.dtype)

    return pl.pallas_call(
        body, name=name, out_shape=out_shapes, grid=grid,
        in_specs=[_tile_spec(k, a.shape, tm, heads) for k, a in zip(kinds, args)],
        out_specs=[_tile_spec(k, (rows, w), tm, heads) for (k, w, _) in outs],
        compiler_params=_params(("arbitrary",) * len(grid)),
    )(*args)


def tile_bwd(fn, name, kinds, args, diff, outs, cts, rows, tm, heads, row_base=0):
    grid, row_axis = _tile_grid(rows, tm, heads)
    n_in, n_ct = len(args), len(cts)
    diff_idx = [i for i, d in enumerate(diff) if d]
    g_shapes, g_specs = [], []
    for i in diff_idx:
        k = _kind(kinds[i])[0]
        dt = args[i].dtype if k in ("row", "rowh") else F32
        g_shapes.append(jax.ShapeDtypeStruct(args[i].shape, dt))
        g_specs.append(_tile_spec(kinds[i], args[i].shape, tm, heads))

    def body(*refs):
        in_refs, ct_refs, g_refs = refs[:n_in], refs[n_in:n_in + n_ct], refs[n_in + n_ct:]
        vals, counts = _split_vals(kinds, in_refs)
        first_piece = [sum(counts[:i]) for i in range(n_in)]
        flat_diff = [first_piece[i] + p for i in diff_idx for p in range(counts[i])]
        row_id = pl.program_id(row_axis)
        row0 = row_base + row_id * tm

        def f(*dvals):
            full = list(vals)
            for i, dv in zip(flat_diff, dvals):
                full[i] = dv
            return tuple(fn(_DIFF, row0, *full))

        _, vjp = jax.vjp(f, *[vals[i] for i in flat_diff])
        ct_vals, _ = _split_vals([k for (k, _, _) in outs], ct_refs)
        flat_grads = list(vjp(tuple(ct_vals)))
        for g_ref, i in zip(g_refs, diff_idx):
            pieces = [flat_grads.pop(0) for _ in range(counts[i])]
            g = pieces[0] if len(pieces) == 1 else jnp.concatenate(pieces, axis=-1)
            k = _kind(kinds[i])[0]
            if k in ("row", "rowh"):
                g_ref[...] = g.astype(g_ref.dtype)
            else:
                first = row_id == 0
                if heads and k == "par":
                    first = jnp.logical_and(first, pl.program_id(0) == 0)

                @pl.when(first)
                def _(g_ref=g_ref, g=g):
                    g_ref[...] = g

                @pl.when(jnp.logical_not(first))
                def _(g_ref=g_ref, g=g):
                    g_ref[...] += g

    return pl.pallas_call(
        body, name=name, out_shape=g_shapes, grid=grid,
        in_specs=[_tile_spec(k, a.shape, tm, heads) for k, a in zip(kinds, args)]
        + [_tile_spec(k, (rows, w), tm, heads) for (k, w, _) in outs],
        out_specs=g_specs,
        compiler_params=_params(("arbitrary",) * len(grid)),
    )(*args, *cts)


def make_tile_op(fn, name, kinds, diff, outs, rows, tm, heads=0, row_base=0):
    tm = min(tm, rows)

    @jax.custom_vjp
    def op(*args):
        return tuple(tile_fwd(fn, name + "_f", kinds, args, outs, rows, tm, heads, row_base))

    def fwd(*args):
        return op(*args), args

    def bwd(args, cts):
        grads = tile_bwd(fn, name + "_b", kinds, args, diff, outs, cts, rows, tm, heads, row_base)
        it = iter(grads)
        res = []
        for a, d in zip(args, diff):
            res.append(next(it).astype(a.dtype) if d else None)
        return tuple(res)

    op.defvjp(fwd, bwd)
    return op


def _rms(x, g):
    return x * lax.rsqrt(jnp.mean(x * x, axis=-1, keepdims=True) + EPS) * g


def fn_rms(ops, row0, x, g):
    return (_rms(x, g),)


def fn_relu2(ops, row0, a):
    r = jnp.maximum(a, 0.0)
    return (r * r,)


def fn_mul(ops, row0, a, b):
    return (a * b,)


def fn_rope(ops, row0, *t):
    nh = (len(t) - 2) // 2
    c, s = t[-2], t[-1]
    return tuple(x * c + xs * s for x, xs in zip(t[:nh], t[nh:2 * nh]))


def _softmax(s):
    m = lax.stop_gradient(jnp.max(s, axis=-1, keepdims=True))
    e = jnp.exp(s - m)
    return e / jnp.sum(e, axis=-1, keepdims=True)


def fn_xattn(ops, row0, q, k, v):
    s = ops.b.nt(q, k) * (X_HEAD_DIM ** -0.5)
    return (ops.b.nn(_softmax(s), v),)


def fn_mla_attn(ops, row0, qn, qr, kn, v, kr):
    s = (ops.b.nt(qn, kn) + ops.b.nt(qr, kr)) * ((MLA_NOPE + MLA_ROPE) ** -0.5)
    rows = row0 + lax.broadcasted_iota(jnp.int32, s.shape, 0)
    cols = lax.broadcasted_iota(jnp.int32, s.shape, 1)
    s = jnp.where(rows >= cols, s, NEG_BIG)
    return (ops.b.nn(_softmax(s), v),)


def _silu(x):
    return x * jax.nn.sigmoid(x)


def fn_gdn_prep(ops, row0, *t):
    nh = len(t) // 3
    qs, ks, vs = [], [], []
    for qc, kc, vc in zip(t[:nh], t[nh:2 * nh], t[2 * nh:]):
        q, k = _silu(qc), _silu(kc)
        qs.append(q * lax.rsqrt(jnp.sum(q * q, -1, keepdims=True) + EPS) * (GDN_DK ** -0.5))
        ks.append(k * lax.rsqrt(jnp.sum(k * k, -1, keepdims=True) + EPS))
        vs.append(_silu(vc))
    return tuple(qs + ks + vs)


def fn_gdn_gates(ops, row0, ba, alog, dtb):
    width = GDN_HEADS * GDN_DK
    beta = jax.nn.sigmoid(ba)
    z = ba + dtb
    softplus = jnp.maximum(z, 0.0) + jnp.log1p(jnp.exp(-jnp.abs(z)))
    g = -jnp.exp(alog) * softplus
    r = lax.broadcasted_iota(jnp.int32, (LANES, width), 0)
    c = lax.broadcasted_iota(jnp.int32, (LANES, width), 1) // GDN_DK
    e_beta = (r == c).astype(F32)
    e_g = (r == c + GDN_HEADS).astype(F32)
    return ops.h.nn(beta, e_beta), ops.h.nn(g, e_g)


def fn_gdn_out(ops, row0, *t):
    nh = (len(t) - 1) // 2
    g = t[-1]
    return tuple(_rms(o, g) * _silu(gate) for o, gate in zip(t[:nh], t[nh:2 * nh]))


def _shift_down(x, d, t_idx):
    if d == 0:
        return x
    return jnp.where(t_idx >= d, pltpu.roll(x, d, axis=0), 0.0)


def _shift_up(x, d, t_idx):
    if d == 0:
        return x
    n = x.shape[0]
    return jnp.where(t_idx < n - d, pltpu.roll(x, n - d, axis=0), 0.0)


def conv_fwd(x, w, name):
    s, c = x.shape
    kw = w.shape[0]
    tc = _pick(c, (256, 128))

    def body(x_ref, w_ref, y_ref):
        xv = x_ref[...]
        t_idx = lax.broadcasted_iota(jnp.int32, xv.shape, 0)
        acc = jnp.zeros_like(xv)
        for j in range(kw):
            acc = acc + w_ref[j:j + 1, :] * _shift_down(xv, kw - 1 - j, t_idx)
        y_ref[...] = acc

    return pl.pallas_call(
        body, name=name, out_shape=jax.ShapeDtypeStruct((s, c), F32), grid=(c // tc,),
        in_specs=[pl.BlockSpec((s, tc), lambda i: (0, i)), pl.BlockSpec((kw, tc), lambda i: (0, i))],
        out_specs=pl.BlockSpec((s, tc), lambda i: (0, i)),
        compiler_params=_params(("parallel",)),
    )(x, w)


def conv_bwd(x, w, dy, name):
    s, c = x.shape
    kw = w.shape[0]
    tc = _pick(c, (256, 128))

    def body(x_ref, w_ref, dy_ref, dx_ref, dw_ref):
        xv, dyv = x_ref[...], dy_ref[...]
        t_idx = lax.broadcasted_iota(jnp.int32, xv.shape, 0)
        dx = jnp.zeros_like(xv)
        for j in range(kw):
            d = kw - 1 - j
            dx = dx + w_ref[j:j + 1, :] * _shift_up(dyv, d, t_idx)
            dw_ref[j:j + 1, :] = jnp.sum(dyv * _shift_down(xv, d, t_idx), axis=0, keepdims=True)
        dx_ref[...] = dx

    return pl.pallas_call(
        body, name=name,
        out_shape=[jax.ShapeDtypeStruct((s, c), F32), jax.ShapeDtypeStruct((kw, c), F32)],
        grid=(c // tc,),
        in_specs=[pl.BlockSpec((s, tc), lambda i: (0, i)), pl.BlockSpec((kw, tc), lambda i: (0, i)),
                  pl.BlockSpec((s, tc), lambda i: (0, i))],
        out_specs=[pl.BlockSpec((s, tc), lambda i: (0, i)), pl.BlockSpec((kw, tc), lambda i: (0, i))],
        compiler_params=_params(("parallel",)),
    )(x, w, dy)


def make_conv(name):
    @jax.custom_vjp
    def op(x, w):
        return conv_fwd(x, w, name + "_f")

    def fwd(x, w):
        return op(x, w), (x, w)

    def bwd(saved, dy):
        dx, dw = conv_bwd(saved[0], saved[1], dy, name + "_b")
        return dx, dw

    op.defvjp(fwd, bwd)
    return op


def _gdn_consts():
    c, d = GDN_CHUNK, GDN_DK
    i = lax.broadcasted_iota(jnp.int32, (c, c), 0)
    j = lax.broadcasted_iota(jnp.int32, (c, c), 1)
    tri = i >= j
    return dict(
        tri=tri, strict=i > j,
        tri_f=tri.astype(F32),
        eye=(i == j).astype(F32),
        lane0=(lax.broadcasted_iota(jnp.int32, (c, d), 1) == 0).astype(F32),
        last_row=(lax.broadcasted_iota(jnp.int32, (c, d), 0) == c - 1).astype(F32),
    )


def _gdn_chunk(ops, q, k, v, g, beta, state):
    b, m, sel = ops.bb, ops.bm, ops.bs
    nh, c, d = q.shape[0], GDN_CHUNK, GDN_DK
    k_ = _gdn_consts()

    def per_head(a):
        return jnp.broadcast_to(a, (nh,) + a.shape)

    gc = sel.sel_nn(per_head(k_["tri_f"]), g)
    col = jnp.broadcast_to(jnp.sum(gc * k_["lane0"], axis=2, keepdims=True), (nh, c, c))
    row = sel.sel_nt(per_head(k_["lane0"]), gc)
    decay = jnp.where(k_["tri"], jnp.exp(jnp.where(k_["tri"], col - row, 0.0)), 0.0)
    kb = k * beta
    mm_ = jnp.where(k_["strict"], b.nt(kb, k) * decay, 0.0)
    p = -mm_
    t = k_["eye"] + p
    for _ in range(int(math.log2(GDN_CHUNK)) - 1):
        p = m.nn(p, p)
        t = t + m.nn(t, p)
    egc = jnp.exp(gc)
    u = b.nn(t, v * beta)
    w = b.nn(t, kb * egc)
    attn = b.nt(q, k) * decay
    v_new = u - b.nn(w, state)
    o = b.nn(q * egc, state) + b.nn(attn, v_new)
    g_last = jnp.sum(gc * k_["last_row"], axis=1, keepdims=True)
    new_state = (state * jnp.exp(jnp.broadcast_to(g_last, (nh, d, d)))
                 + b.tn(k * jnp.exp(jnp.broadcast_to(g_last, (nh, c, d)) - gc), v_new))
    return o, new_state


GDN_HEAD_GROUP = 8
GDN_TILE_CHUNKS = 4


def _heads_of(ref, rows, n_heads):
    d = GDN_DK
    return jnp.stack([ref[rows, h * d:(h + 1) * d] for h in range(n_heads)])


def _gdn_specs(s, reverse):
    d, hg = GDN_DK, GDN_HEAD_GROUP
    tile = min(GDN_TILE_CHUNKS * GDN_CHUNK, s)
    n_tiles = s // tile
    t_of = (lambda t: n_tiles - 1 - t) if reverse else (lambda t: t)
    seq = pl.BlockSpec((tile, hg * d), lambda grp, t: (t_of(t), grp))
    st = pl.BlockSpec((hg, tile // GDN_CHUNK, d, d), lambda grp, t: (grp, t_of(t), 0, 0))
    return seq, st, tile, n_tiles


def gdn_fwd(q, k, v, g, beta, name):
    s = q.shape[0]
    d, hg = GDN_DK, GDN_HEAD_GROUP
    seq, st, tile, n_tiles = _gdn_specs(s, False)

    def body(q_ref, k_ref, v_ref, g_ref, b_ref, o_ref, st_ref, state_scr):
        @pl.when(pl.program_id(1) == 0)
        def _():
            state_scr[...] = jnp.zeros_like(state_scr)

        def step(ci, carry):
            rows = pl.ds(pl.multiple_of(ci * GDN_CHUNK, GDN_CHUNK), GDN_CHUNK)
            state = state_scr[...]
            for h in range(hg):
                st_ref[h, ci] = state[h]
            o, new_state = _gdn_chunk(_PLAIN, *[_heads_of(r, rows, hg) for r in (q_ref, k_ref, v_ref, g_ref, b_ref)],
                                      state)
            for h in range(hg):
                o_ref[rows, h * d:(h + 1) * d] = o[h]
            state_scr[...] = new_state
            return carry

        lax.fori_loop(0, tile // GDN_CHUNK, step, 0)

    return pl.pallas_call(
        body, name=name,
        out_shape=[jax.ShapeDtypeStruct(q.shape, F32),
                   jax.ShapeDtypeStruct((GDN_HEADS, s // GDN_CHUNK, d, d), F32)],
        grid=(GDN_HEADS // hg, n_tiles), in_specs=[seq] * 5, out_specs=[seq, st],
        scratch_shapes=[pltpu.VMEM((hg, d, d), F32)],
        compiler_params=_params(("parallel", "arbitrary")),
    )(q, k, v, g, beta)


def gdn_bwd(q, k, v, g, beta, states, do, name):
    s = q.shape[0]
    d, hg = GDN_DK, GDN_HEAD_GROUP
    seq, st, tile, n_tiles = _gdn_specs(s, True)
    tile_chunks = tile // GDN_CHUNK

    def body(q_ref, k_ref, v_ref, g_ref, b_ref, st_ref, do_ref, dq_ref, dk_ref, dv_ref, dg_ref, db_ref, dstate_scr):
        @pl.when(pl.program_id(1) == 0)
        def _():
            dstate_scr[...] = jnp.zeros_like(dstate_scr)

        def step(it, carry):
            ci = tile_chunks - 1 - it
            rows = pl.ds(pl.multiple_of(ci * GDN_CHUNK, GDN_CHUNK), GDN_CHUNK)
            prim = [_heads_of(r, rows, hg) for r in (q_ref, k_ref, v_ref, g_ref, b_ref)]
            prim.append(jnp.stack([st_ref[h, ci] for h in range(hg)]))
            _, vjp = jax.vjp(functools.partial(_gdn_chunk, _DIFF), *prim)
            grads = vjp((_heads_of(do_ref, rows, hg), dstate_scr[...]))
            for g_ref_out, gr in zip((dq_ref, dk_ref, dv_ref, dg_ref, db_ref), grads[:5]):
                for h in range(hg):
                    g_ref_out[rows, h * d:(h + 1) * d] = gr[h]
            dstate_scr[...] = grads[5]
            return carry

        lax.fori_loop(0, tile_chunks, step, 0)

    return pl.pallas_call(
        body, name=name,
        out_shape=[jax.ShapeDtypeStruct(q.shape, F32)] * 5,
        grid=(GDN_HEADS // hg, n_tiles), in_specs=[seq] * 5 + [st, seq], out_specs=[seq] * 5,
        scratch_shapes=[pltpu.VMEM((hg, d, d), F32)],
        compiler_params=_params(("parallel", "arbitrary")),
    )(q, k, v, g, beta, states, do)


def make_gdn(name):
    @jax.custom_vjp
    def op(q, k, v, g, beta):
        return gdn_fwd(q, k, v, g, beta, name + "_f")[0]

    def fwd(q, k, v, g, beta):
        o, states = gdn_fwd(q, k, v, g, beta, name + "_f")
        return o, (q, k, v, g, beta, states)

    def bwd(saved, do):
        return tuple(gdn_bwd(*saved, do, name + "_b"))

    op.defvjp(fwd, bwd)
    return op


def loss_head(x, g, target, name):
    s, d = x.shape
    tm = min(256, s)

    def body(x_ref, g_ref, t_ref, loss_ref, dx_ref, dg_ref):
        tgt = t_ref[...]

        def f(xv, gv):
            err = _rms(xv, gv) - tgt
            per_row = jnp.mean(err * err, axis=-1, keepdims=True)
            return 0.5 * jnp.sum(per_row, axis=0, keepdims=True)

        val, vjp = jax.vjp(f, x_ref[...], g_ref[...])
        dx, dg = vjp(jnp.ones((1, 1), F32))
        dx_ref[...] = dx
        first = pl.program_id(0) == 0

        @pl.when(first)
        def _():
            dg_ref[...] = dg
            loss_ref[...] = jnp.broadcast_to(val, loss_ref.shape)

        @pl.when(jnp.logical_not(first))
        def _():
            dg_ref[...] += dg
            loss_ref[...] += jnp.broadcast_to(val, loss_ref.shape)

    row = pl.BlockSpec((tm, d), lambda i: (i, 0))
    vec = pl.BlockSpec((1, d), lambda i: (0, 0))
    return pl.pallas_call(
        body, name=name,
        out_shape=[jax.ShapeDtypeStruct((1, LANES), F32), jax.ShapeDtypeStruct((s, d), F32),
                   jax.ShapeDtypeStruct((1, d), F32)],
        grid=(s // tm,), in_specs=[row, vec, row],
        out_specs=[pl.BlockSpec((1, LANES), lambda i: (0, 0)), row, vec],
        compiler_params=_params(("arbitrary",)),
    )(x, g, target)


def adamw(g8, w, m, v, layer, prev, name):
    n_layers, rows, width = w.shape
    tr = _pick(rows, (256, 128, 64, 32, 16, 8))

    def body(g_ref, w_ref, m_ref, v_ref, *rest):
        go_ref, d_ref, mo_ref, vo_ref = rest[-4:]
        g = g_ref[0].astype(F32)
        for p in range(1, N_DEV):
            g = g + g_ref[p].astype(F32)
        m_new = ADAM_B1 * m_ref[...] + (1.0 - ADAM_B1) * g
        v_new = ADAM_B2 * v_ref[...] + (1.0 - ADAM_B2) * (g * g)
        m_hat = m_new / (1.0 - ADAM_B1 ** ADAM_STEP)
        v_hat = v_new / (1.0 - ADAM_B2 ** ADAM_STEP)
        go_ref[...] = g
        d_ref[...] = -ADAM_LR * (m_hat / (jnp.sqrt(v_hat) + ADAM_EPS) + ADAM_WD * w_ref[...])
        mo_ref[...] = m_new
        vo_ref[...] = v_new

    blk = pl.BlockSpec((None, tr, width), lambda i: (layer, i, 0))
    carried = list(prev) if prev is not None else []
    return pl.pallas_call(
        body, name=name, out_shape=[jax.ShapeDtypeStruct((n_layers, rows, width), F32)] * 4,
        grid=(rows // tr,),
        in_specs=[pl.BlockSpec((N_DEV, tr, width), lambda i: (0, i, 0)), blk, blk, blk]
        + [pl.BlockSpec(memory_space=pl.ANY)] * len(carried),
        out_specs=[blk] * 4,
        input_output_aliases={4 + j: j for j in range(len(carried))},
        compiler_params=_params(("parallel",)),
    )(g8, w, m, v, *carried)


_HBM = pl.BlockSpec(memory_space=pltpu.HBM)
_SEM = pl.BlockSpec(memory_space=pltpu.SEMAPHORE)
_EFFECT = pltpu.SideEffectType.DATAFLOW_SIDE_EFFECTING


def _exchange_copies(mode, src_refs, land_refs, send_sems, recv_sems, local_sems):
    x, y, c = lax.axis_index("x"), lax.axis_index("y"), lax.axis_index("c")
    me = 4 * x + 2 * y + c
    n = len(src_refs)

    def src(k, p):
        return src_refs[k] if mode == "gather" else src_refs[k].at[p]

    local = [pltpu.make_async_copy(src(k, me), land_refs[k].at[me], local_sems.at[k]) for k in range(n)]
    sends, recvs = [], []
    for k in range(n):
        for r in range(1, N_DEV):
            px = (1 - x) if r & 4 else x
            py = (1 - y) if r & 2 else y
            pc = (1 - c) if r & 1 else c
            p = 4 * px + 2 * py + pc
            sem = k * (N_DEV - 1) + r - 1
            sends.append(pltpu.make_async_remote_copy(
                src_ref=src(k, p), dst_ref=land_refs[k].at[me],
                send_sem=send_sems.at[sem], recv_sem=recv_sems.at[sem],
                device_id=(px, py, pc), device_id_type=pl.DeviceIdType.MESH))
            recvs.append(pltpu.make_async_remote_copy(
                src_ref=src(k, p), dst_ref=land_refs[k].at[p],
                send_sem=send_sems.at[sem], recv_sem=recv_sems.at[sem],
                device_id=(px, py, pc), device_id_type=pl.DeviceIdType.MESH))
    return local, sends, recvs


def exchange_start(mode, arrays, name, after=None):
    n = len(arrays)
    extra = [] if after is None else [after]
    land_shapes = [((N_DEV,) + tuple(a.shape)) if mode == "gather" else tuple(a.shape) for a in arrays]
    lands = [pltpu.with_memory_space_constraint(lax.empty(shp, a.dtype), pltpu.HBM)
             for shp, a in zip(land_shapes, arrays)]
    srcs = [pltpu.with_memory_space_constraint(a, pltpu.HBM) for a in arrays]

    def body(*refs):
        src_refs, land_refs = refs[:n], refs[n:2 * n]
        first_out = 2 * n + len(extra)
        send_sems, recv_sems, local_sems = refs[first_out:first_out + 3]
        token = refs[-1]
        local, sends, _ = _exchange_copies(mode, src_refs, land_refs, send_sems, recv_sems, local_sems)
        for cp in local + sends:
            cp.start()
        token[...] = jnp.zeros_like(token)

    n_sem = n * (N_DEV - 1)
    out = pl.pallas_call(
        body, name=name,
        out_shape=(pltpu.SemaphoreType.DMA((n_sem,)), pltpu.SemaphoreType.DMA((n_sem,)),
                   pltpu.SemaphoreType.DMA((n,)),
                   *[pltpu.HBM(a.shape, a.dtype) for a in arrays],
                   *[pltpu.HBM(shp, a.dtype) for shp, a in zip(land_shapes, arrays)],
                   jax.ShapeDtypeStruct((8, LANES), F32)),
        in_specs=[_HBM] * (2 * n) + [pl.BlockSpec(memory_space=pl.ANY)] * len(extra),
        out_specs=(_SEM, _SEM, _SEM, *[_HBM] * (2 * n), pl.BlockSpec(memory_space=pltpu.VMEM)),
        input_output_aliases={i: 3 + i for i in range(2 * n)},
        compiler_params=pltpu.CompilerParams(has_side_effects=_EFFECT),
    )(*srcs, *lands, *extra)
    handle = dict(mode=mode, sems=out[:3], srcs=out[3:3 + n], lands=out[3 + n:3 + 2 * n])
    return handle, out[-1]


def exchange_wait(handle, after, name):
    mode, srcs, lands = handle["mode"], list(handle["srcs"]), list(handle["lands"])
    n = len(srcs)

    def body(*refs):
        src_refs, land_refs = refs[:n], refs[n:2 * n]
        send_sems, recv_sems, local_sems = refs[2 * n:2 * n + 3]
        local, sends, recvs = _exchange_copies(mode, src_refs, land_refs, send_sems, recv_sems, local_sems)
        for cp in sends:
            cp.wait_send()
        for cp in recvs:
            cp.wait_recv()
        for cp in local:
            cp.wait()

    out = pl.pallas_call(
        body, name=name,
        out_shape=(*[pltpu.HBM(a.shape, a.dtype) for a in srcs], *[pltpu.HBM(a.shape, a.dtype) for a in lands]),
        in_specs=[_HBM] * (2 * n) + [_SEM] * 3 + [pl.BlockSpec(memory_space=pl.ANY)],
        out_specs=tuple([_HBM] * (2 * n)),
        input_output_aliases={i: i for i in range(2 * n)},
        compiler_params=pltpu.CompilerParams(has_side_effects=_EFFECT),
    )(*srcs, *lands, *handle["sems"], after)
    return list(out[n:])


BIG = ["mla_w_in", "mla_w_uq", "mla_w_ukv", "mla_w_o", "gdn_w_in", "gdn_w_o", "sc_w_in", "sc_w_o",
       "xa_w_q", "xa_w_kv", "xa_w_o", "mlp_w1", "mlp_w2"]
TINY = [("mla_q_norm", 1), ("mla_kv_norm", 1), ("gdn_conv_w", 2), ("sc_conv_w", 2)]
REPL = ["gdn_a_log", "gdn_dt_bias", "gdn_o_norm", "norm_mix", "norm_mem", "norm_mlp", "mem_norm", "final_norm"]
WEIGHTS = ["mla_w_in", "mla_q_norm", "mla_kv_norm", "mla_w_uq", "mla_w_ukv", "mla_w_o", "gdn_w_in",
           "gdn_conv_w", "gdn_a_log", "gdn_dt_bias", "gdn_o_norm", "gdn_w_o", "sc_w_in", "sc_conv_w",
           "sc_w_o", "norm_mix", "norm_mem", "norm_mlp", "xa_w_q", "xa_w_kv", "xa_w_o", "mlp_w1",
           "mlp_w2", "mem_norm", "final_norm"]
MIXER_WEIGHTS = (["mla_w_in", "mla_w_uq", "mla_w_ukv", "mla_w_o"], ["gdn_w_in", "gdn_w_o"], ["sc_w_in", "sc_w_o"])
MIXER_PARAMS = (["norm_mix", "mla_q_norm", "mla_kv_norm"],
                ["norm_mix", "gdn_conv_w", "gdn_a_log", "gdn_dt_bias", "gdn_o_norm"],
                ["norm_mix", "sc_conv_w"])


def from_shards(a8, axis):
    a = jnp.moveaxis(a8, 0, axis)
    shp = a.shape
    return a.reshape(shp[:axis] + (shp[axis] * shp[axis + 1],) + shp[axis + 2:])


def pack_rows(flat_list, width, row_mult):
    total = sum(a.shape[-1] for a in flat_list)
    rows = -(-total // width)
    rows = -(-rows // row_mult) * row_mult
    pad = rows * width - total
    parts = list(flat_list)
    if pad:
        parts.append(jnp.zeros((pad,), flat_list[0].dtype))
    return jnp.concatenate(parts, axis=-1).reshape(rows, width)


def unpack_rows(packed, shapes):
    lead = packed.shape[:-2]
    flat = packed.reshape(lead + (-1,))
    out, off = [], 0
    for shp in shapes:
        n = math.prod(shp)
        out.append(flat[..., off:off + n].reshape(lead + tuple(shp)))
        off += n
    return out


def _swap_halves(w):
    half = w.shape[-1] // 2
    return jnp.concatenate([w[..., half:], w[..., :half]], axis=-1)


def _pad_last(w, n):
    return jnp.pad(w, [(0, 0)] * (w.ndim - 1) + [(0, n - w.shape[-1])])


def _unblock(w8):
    return jnp.transpose(w8, (1, 0, 2)).reshape(w8.shape[1], -1)


def _stack_rows(w8):
    return w8.reshape(-1, w8.shape[-1])


def rms_op(name, rows, d, out_dtype, tm=512):
    return make_tile_op(fn_rms, name, ["row", "par"], [True, True], [("row", d, out_dtype)], rows, min(tm, rows))


def seg_memory(p, mem):
    return rms_op("rms_memory", mem.shape[0], mem.shape[1], BF16)(mem, p["mem_norm"].reshape(1, -1))[0]


def seg_mixer(i, wts, p, x, rope_c, rope_s):
    s, d = x.shape
    j, kind = i // N_MIXERS, i % N_MIXERS
    tag = f"l{i}"
    hd = MLA_NOPE
    h = rms_op(tag + "_rms_mix", s, d, BF16)(x, p["norm_mix"][i].reshape(1, d))[0]
    if kind == 0:
        w_in = _stack_rows(wts["mla_w_in"])
        w_cq = w_in[:, :MLA_Q_RANK]
        w_ckv = w_in[:, MLA_Q_RANK:MLA_Q_RANK + MLA_KV_RANK]
        w_kr = w_in[:, MLA_Q_RANK + MLA_KV_RANK:]
        w_z = jnp.concatenate([w_cq, w_ckv, _pad_last(w_kr, hd), _pad_last(_swap_halves(w_kr), hd)], axis=-1)
        z = make_mm(tag + "_mla_in", F32)(h, w_z)
        c_q, c_kv = z[:, :MLA_Q_RANK], z[:, MLA_Q_RANK:MLA_Q_RANK + MLA_KV_RANK]
        kr_raw, kr_swp = z[:, -2 * hd:-hd], z[:, -hd:]
        c_qn = rms_op(tag + "_rms_q", s, MLA_Q_RANK, BF16)(c_q, p["mla_q_norm"][j].reshape(1, -1))[0]
        c_kvn = rms_op(tag + "_rms_kv", s, MLA_KV_RANK, BF16)(c_kv, p["mla_kv_norm"][j].reshape(1, -1))[0]
        w_uq8 = wts["mla_w_uq"]
        w_qn = _unblock(w_uq8[:, :, :MLA_NOPE])
        w_qr = w_uq8[:, :, MLA_NOPE:]
        w_qr_p = _unblock(_pad_last(w_qr, hd))
        w_qr_s = _unblock(_pad_last(_swap_halves(w_qr), hd))
        q_nope = make_mm(tag + "_mla_uq_n", BF16)(c_qn, w_qn)
        q_raw = make_mm(tag + "_mla_uq_r", F32)(c_qn, w_qr_p)
        q_swp = make_mm(tag + "_mla_uq_s", F32)(c_qn, w_qr_s)
        nq = MLA_HEADS * hd
        kv = make_mm(tag + "_mla_ukv", BF16, blocked=True)(c_kvn, wts["mla_w_ukv"])
        heads_row = ("row", hd, MLA_HEADS)
        q_rope = make_tile_op(fn_rope, tag + "_rope_q", [heads_row, heads_row, "row", "row"],
                              [True, True, False, False], [(heads_row, nq, F32)], s, 512)(
            q_raw, q_swp, rope_c, rope_s)[0]
        k_rope = make_tile_op(fn_rope, tag + "_rope_k", ["row", "row", "row", "row"],
                              [True, True, False, False], [("row", hd, F32)], s, 1024)(
            kr_raw, kr_swp, rope_c, rope_s)[0]
        n_groups = MLA_QUERY_GROUPS if s % (MLA_QUERY_GROUPS * 256) == 0 else 1
        rows_g = s // n_groups
        o_groups = []
        for grp in range(n_groups):
            r0, r1 = grp * rows_g, (grp + 1) * rows_g
            o_groups.append(make_tile_op(
                fn_mla_attn, f"{tag}_mla_attn{grp}", [("rowh", hd), ("rowh", hd), ("parh", hd, 2), "par"],
                [True] * 4, [(("rowh", hd), nq, BF16)], rows_g, 256, MLA_HEADS, row_base=r0)(
                q_nope[r0:r1], q_rope[r0:r1], kv[:r1], k_rope[:r1])[0])
        o = jnp.concatenate(o_groups, axis=0)
        return make_mm(tag + "_mla_o", F32, with_res=True)(x, o, _stack_rows(wts["mla_w_o"]))
    if kind == 1:
        ng = GDN_HEADS * GDN_DK
        w_in = _unblock(wts["gdn_w_in"])
        cw = p["gdn_conv_w"][j]
        conv_out = []
        for part, nm in enumerate(("q", "k", "v")):
            cols = slice(part * ng, (part + 1) * ng)
            pre = make_mm(f"{tag}_gdn_in_{nm}", F32)(h, w_in[:, cols])
            conv_out.append(make_conv(f"{tag}_gdn_conv_{nm}")(pre, cw[:, cols]))
        gate = make_mm(tag + "_gdn_in_g", F32)(h, w_in[:, 3 * ng:4 * ng])
        ba = make_mm(tag + "_gdn_in_ba", F32)(h, _pad_last(w_in[:, 4 * ng:], LANES))
        heads_row = ("row", GDN_DK, GDN_HEADS)
        q, k, v = make_tile_op(fn_gdn_prep, tag + "_gdn_prep", [heads_row] * 3, [True] * 3,
                               [(heads_row, ng, F32)] * 3, s, 512)(*conv_out)
        alog = jnp.pad(p["gdn_a_log"][j].reshape(1, -1), ((0, 0), (GDN_HEADS, LANES - 2 * GDN_HEADS)))
        dtb = jnp.pad(p["gdn_dt_bias"][j].reshape(1, -1), ((0, 0), (GDN_HEADS, LANES - 2 * GDN_HEADS)))
        beta_b, g_b = make_tile_op(fn_gdn_gates, tag + "_gdn_gates", ["row", "par", "par"], [True] * 3,
                                   [("row", ng, F32)] * 2, s, 512)(ba, alog, dtb)
        o = make_gdn(tag + "_gdn_core")(q, k, v, g_b, beta_b)
        o = make_tile_op(fn_gdn_out, tag + "_gdn_out", [heads_row, heads_row, "par"],
                         [True] * 3, [(heads_row, ng, BF16)], s, 512)(
            o, gate, p["gdn_o_norm"][j].reshape(1, -1))[0]
        return make_mm(tag + "_gdn_o", F32, with_res=True)(x, o, _stack_rows(wts["gdn_w_o"]))
    w_in = _unblock(wts["sc_w_in"])
    b_gate = make_mm(tag + "_sc_in_b", F32)(h, w_in[:, :d])
    c_gate = make_mm(tag + "_sc_in_c", F32)(h, w_in[:, d:2 * d])
    u = make_mm(tag + "_sc_in_u", F32)(h, w_in[:, 2 * d:])
    cu = make_tile_op(fn_mul, tag + "_sc_cu", ["row", "row"], [True, True], [("row", d, F32)], s, 512)(
        c_gate, u)[0]
    cv = make_conv(tag + "_sc_conv")(cu, p["sc_conv_w"][j])
    yv = make_tile_op(fn_mul, tag + "_sc_gate", ["row", "row"], [True, True], [("row", d, BF16)], s, 512)(
        b_gate, cv)[0]
    return make_mm(tag + "_sc_o", F32, with_res=True)(x, yv, _stack_rows(wts["sc_w_o"]))


def seg_xattn(i, wts, p, x, mem_n):
    s, d = x.shape
    tag = f"l{i}"
    hx = rms_op(tag + "_rms_mem", s, d, BF16)(x, p["norm_mem"][i].reshape(1, d))[0]
    q = make_mm(tag + "_xa_q", BF16)(hx, _stack_rows(wts["xa_w_q"]))
    kv = make_mm(tag + "_xa_kv", BF16, blocked=True)(mem_n, wts["xa_w_kv"])
    o = make_tile_op(fn_xattn, tag + "_xattn",
                     [("rowh", X_HEAD_DIM), ("parh", X_HEAD_DIM), ("parh", X_HEAD_DIM)], [True] * 3,
                     [(("rowh", X_HEAD_DIM), d, BF16)], s, 1024, X_HEADS)(q, kv[:, :d], kv[:, d:])[0]
    return make_mm(tag + "_xa_o", F32, with_res=True)(x, o, _stack_rows(wts["xa_w_o"]))


def seg_mlp(i, wts, p, x):
    s, d = x.shape
    tag = f"l{i}"
    hm = rms_op(tag + "_rms_mlp", s, d, BF16)(x, p["norm_mlp"][i].reshape(1, d))[0]
    a = make_mm(tag + "_mlp_1", BF16, blocked=True)(hm, wts["mlp_w1"])
    bsq = make_tile_op(fn_relu2, tag + "_relu2", ["row"], [True], [("row", a.shape[1], BF16)], s, 256)(a)[0]
    return make_mm(tag + "_mlp_2", F32, with_res=True)(x, bsq, _stack_rows(wts["mlp_w2"]))


def segments():
    segs = []
    for i in range(DEPTH):
        j, kind = i // N_MIXERS, i % N_MIXERS
        segs.append((f"l{i}_mixer", [(n, j) for n in MIXER_WEIGHTS[kind]], MIXER_PARAMS[kind], "mixer"))
        segs.append((f"l{i}_xattn", [(n, i) for n in ("xa_w_q", "xa_w_kv", "xa_w_o")], ["norm_mem"], "xattn"))
        segs.append((f"l{i}_mlp", [(n, i) for n in ("mlp_w1", "mlp_w2")], ["norm_mlp"], "mlp"))
    return segs


def run_segment(index, kind, wts, p, x, mem_n, rope_c, rope_s):
    layer = index // 3
    if kind == "mixer":
        return seg_mixer(layer, wts, p, x, rope_c, rope_s)
    if kind == "xattn":
        return seg_xattn(layer, wts, p, x, mem_n)
    return seg_mlp(layer, wts, p, x)


def rope_tables(positions):
    inv_freq = ROPE_THETA ** (-jnp.arange(0, MLA_ROPE, 2, dtype=F32) / MLA_ROPE)
    ang = positions.astype(F32)[:, None] * inv_freq
    cos, sin = jnp.cos(ang), jnp.sin(ang)
    zeros = jnp.zeros((positions.shape[0], MLA_NOPE - MLA_ROPE), F32)
    return jnp.concatenate([cos, cos, zeros], axis=-1), jnp.concatenate([-sin, sin, zeros], axis=-1)


def kernel(x, mem, positions, mla_w_in, mla_q_norm, mla_kv_norm, mla_w_uq, mla_w_ukv, mla_w_o, gdn_w_in, gdn_conv_w, gdn_a_log, gdn_dt_bias, gdn_o_norm, gdn_w_o, sc_w_in, sc_conv_w, sc_w_o, norm_mix, norm_mem, norm_mlp, xa_w_q, xa_w_kv, xa_w_o, mlp_w1, mlp_w2, mem_norm, final_norm, loss_target, m_mla_w_in, m_mla_q_norm, m_mla_kv_norm, m_mla_w_uq, m_mla_w_ukv, m_mla_w_o, m_gdn_w_in, m_gdn_conv_w, m_gdn_a_log, m_gdn_dt_bias, m_gdn_o_norm, m_gdn_w_o, m_sc_w_in, m_sc_conv_w, m_sc_w_o, m_norm_mix, m_norm_mem, m_norm_mlp, m_xa_w_q, m_xa_w_kv, m_xa_w_o, m_mlp_w1, m_mlp_w2, m_mem_norm, m_final_norm, v_mla_w_in, v_mla_q_norm, v_mla_kv_norm, v_mla_w_uq, v_mla_w_ukv, v_mla_w_o, v_gdn_w_in, v_gdn_conv_w, v_gdn_a_log, v_gdn_dt_bias, v_gdn_o_norm, v_gdn_w_o, v_sc_w_in, v_sc_conv_w, v_sc_w_o, v_norm_mix, v_norm_mem, v_norm_mlp, v_xa_w_q, v_xa_w_kv, v_xa_w_o, v_mlp_w1, v_mlp_w2, v_mem_norm, v_final_norm):
    args = locals()
    w_loc = {n: args[n] for n in WEIGHTS}
    m_loc = {n: args["m_" + n] for n in WEIGHTS}
    v_loc = {n: args["v_" + n] for n in WEIGHTS}
    me = 4 * lax.axis_index("x") + 2 * lax.axis_index("y") + lax.axis_index("c")
    segs = segments()

    w16 = {n: w_loc[n].astype(BF16) for n in BIG}
    tiny_pack = pack_rows([w_loc[n].reshape(-1) for n, _ in TINY], LANES, 8)
    tiny_handle, token = exchange_start("gather", [tiny_pack], "gather_start_tiny")
    gather_handles = []
    for tag, units, _, _ in segs:
        handle, token = exchange_start("gather", [w16[n][layer] for n, layer in units], f"gather_start_{tag}",
                                       after=token)
        gather_handles.append(handle)

    x_cur = x[0]
    rope_c, rope_s = rope_tables(positions[0])
    tiny_all = exchange_wait(tiny_handle, token, "gather_wait_tiny")[0]
    params = {}
    for (n, ax), a8 in zip(TINY, unpack_rows(tiny_all, [w_loc[n].shape for n, _ in TINY])):
        params[n] = from_shards(a8, ax)
    for n in REPL:
        params[n] = w_loc[n]

    mem_n, vjp_memory = jax.vjp(lambda p_: seg_memory(p_, mem[0]), {"mem_norm": params["mem_norm"]})
    vjps = []
    for index, ((tag, units, p_names, kind), handle) in enumerate(zip(segs, gather_handles)):
        landed = exchange_wait(handle, token if index == 0 else x_cur, f"gather_wait_{tag}")
        wts = {n: a for (n, _), a in zip(units, landed)}
        p_seg = {n: params[n] for n in p_names}
        x_cur, vjp_seg = jax.vjp(
            lambda w_, p_, x_, m_, index=index, kind=kind: run_segment(index, kind, w_, p_, x_, m_, rope_c, rope_s),
            wts, p_seg, x_cur, mem_n)
        vjps.append(vjp_seg)

    loss_vec, g_x, d_final = loss_head(x_cur, params["final_norm"].reshape(1, -1), loss_target[0], "loss_head")

    grads = {n: jnp.zeros_like(params[n]) for n in params}
    grads["final_norm"] = d_final.reshape(-1)
    g_mem_n = jnp.zeros_like(mem_n)
    scatter_handles = []
    for (tag, units, _, _), vjp_seg in zip(reversed(segs), reversed(vjps)):
        g_wts, g_p, g_x, g_m = vjp_seg(g_x)
        for n, g in g_p.items():
            grads[n] = grads[n] + g
        g_mem_n = g_mem_n + g_m
        handle, token = exchange_start("scatter", [g_wts[n] for n, _ in units], f"scatter_start_{tag}")
        scatter_handles.append((units, handle))
        g_x = g_x + token[0, 0]
    grads["mem_norm"] = grads["mem_norm"] + vjp_memory(g_mem_n)[0]["mem_norm"]

    small_names = [n for n, _ in TINY] + REPL
    small_g = pack_rows([loss_vec[0, :1]] + [grads[n].astype(F32).reshape(-1) for n in small_names], PACK_W, 8)
    small_handle, _ = exchange_start("gather", [small_g], "gather_start_small_grads")

    g_recv = {}
    for units, handle in scatter_handles:
        landed = exchange_wait(handle, g_x, f"scatter_wait_{units[0][0]}_{units[0][1]}")
        g_recv.update(dict(zip(units, landed)))

    res = {}
    for n in BIG:
        outs = None
        for layer in range(w_loc[n].shape[0]):
            outs = adamw(g_recv[n, layer], w_loc[n], m_loc[n], v_loc[n], layer, outs, f"adamw_{n}_{layer}")
        for kind, a in zip(("grad", "delta", "m", "v"), outs):
            res[(kind, n)] = a
    small_recv = exchange_wait(small_handle, res[("grad", BIG[-1])], "gather_wait_small_grads")[0]

    def full_small(d):
        parts = [jnp.zeros((1,), F32)]
        for n, ax in TINY:
            full_shape = params[n].shape
            start = [0] * len(full_shape)
            start[ax] = me * d[n].shape[ax]
            parts.append(lax.dynamic_update_slice(jnp.zeros(full_shape, F32), d[n], start).reshape(-1))
        parts += [d[n].reshape(-1) for n in REPL]
        return pack_rows(parts, PACK_W, 8)

    outs_small = adamw(small_recv, full_small(w_loc)[None], full_small(m_loc)[None], full_small(v_loc)[None],
                       0, None, "adamw_small")
    small_shapes = [(1,)] + [params[n].shape for n, _ in TINY] + [w_loc[n].shape for n in REPL]
    loss = None
    for kind, packed in zip(("grad", "delta", "m", "v"), outs_small):
        parts = unpack_rows(packed[0], small_shapes)
        if kind == "grad":
            loss = parts[0][0]
        for (n, ax), a in zip(TINY, parts[1:1 + len(TINY)]):
            start = [0] * a.ndim
            start[ax] = me * w_loc[n].shape[ax]
            res[(kind, n)] = lax.dynamic_slice(a, start, w_loc[n].shape)
        for n, a in zip(REPL, parts[1 + len(TINY):]):
            res[(kind, n)] = a

    out = [loss, g_x[None]]
    for kind in ("grad", "delta", "m", "v"):
        out += [res[(kind, n)] for n in WEIGHTS]
    return tuple(out)
```

```python
import functools
import math

import jax
import jax.numpy as jnp
from jax import lax
from jax.experimental import pallas as pl
from jax.experimental.pallas import tpu as pltpu

F32 = jnp.float32
BF16 = jnp.bfloat16

N_DEV = 8
LANES = 128
EPS = 1e-6
ROPE_THETA = 10000.0
MLA_HEADS, MLA_NOPE, MLA_ROPE, MLA_V = 8, 128, 64, 128
MLA_Q_RANK, MLA_KV_RANK = 384, 256
GDN_HEADS, GDN_DK, GDN_CONV, GDN_CHUNK = 8, 128, 4, 64
X_HEADS, X_HEAD_DIM = 4, 256
DEPTH, N_MIXERS = 4, 3
ADAM_LR, ADAM_B1, ADAM_B2, ADAM_EPS, ADAM_WD, ADAM_STEP = 0.001, 0.9, 0.999, 1e-08, 0.01, 10
MLA_QUERY_GROUPS = 4
NEG_BIG = -1e30
PACK_W = 1024


_NN = (((1,), (0,)), ((), ()))
_NT = (((1,), (1,)), ((), ()))
_TN = (((0,), (0,)), ((), ()))
_NN3 = (((2,), (1,)), ((0,), (0,)))
_NT3 = (((2,), (2,)), ((0,), (0,)))
_TN3 = (((1,), (1,)), ((0,), (0,)))


def _dot(a, b, dims):
    return lax.dot_general(a, b, dims, preferred_element_type=F32)


def _hi_lo(x):
    hi = x.astype(BF16)
    return hi, (x - hi.astype(F32)).astype(BF16)


def _split3(x):
    hi = x.astype(BF16)
    r = x - hi.astype(F32)
    mid = r.astype(BF16)
    return hi, mid, (r - mid.astype(F32)).astype(BF16)


def _dg(a, b, dims, prec):
    if prec == "h":
        return lax.dot_general(a, b, dims, precision=lax.Precision.HIGHEST, preferred_element_type=F32)
    if prec == "m":
        a_hi, a_lo = _hi_lo(a)
        b_hi, b_lo = _hi_lo(b)
        return _dot(a_hi, b_hi, dims) + _dot(a_hi, b_lo, dims) + _dot(a_lo, b_hi, dims)
    return _dot(a.astype(BF16), b.astype(BF16), dims)


def _dg_sel(sel, x, dims, sel_first):
    s16 = sel.astype(BF16)
    parts = [(_dot(s16, piece, dims) if sel_first else _dot(piece, s16, dims)) for piece in _split3(x)]
    return parts[0] + parts[1] + parts[2]


class _Ops:
    def __init__(self, prec, differentiable, batched=False):
        d_nn, d_nt, d_tn = (_NN3, _NT3, _TN3) if batched else (_NN, _NT, _TN)

        def nn(a, b):
            return _dg(a, b, d_nn, prec)

        def nt(a, b):
            return _dg(a, b, d_nt, prec)

        def tn(a, b):
            return _dg(a, b, d_tn, prec)

        if differentiable:
            dnn = jax.custom_vjp(nn)
            dnn.defvjp(lambda a, b: (nn(a, b), (a, b)), lambda r, g: (nt(g, r[1]), tn(r[0], g)))
            dnt = jax.custom_vjp(nt)
            dnt.defvjp(lambda a, b: (nt(a, b), (a, b)), lambda r, g: (nn(g, r[1]), tn(g, r[0])))
            dtn = jax.custom_vjp(tn)
            dtn.defvjp(lambda a, b: (tn(a, b), (a, b)), lambda r, g: (nt(r[1], g), nn(r[0], g)))
            nn, nt, tn = dnn, dnt, dtn
        self.nn, self.nt, self.tn = nn, nt, tn


class _SelOps:
    def __init__(self, differentiable, batched=False):
        d_nn, d_nt, d_tn = (_NN3, _NT3, _TN3) if batched else (_NN, _NT, _TN)

        def sel_nn(sel, x):
            return _dg_sel(sel, x, d_nn, True)

        def sel_nt(sel, x):
            return _dg_sel(sel, x, d_nt, True)

        if differentiable:
            dnn = jax.custom_vjp(sel_nn)
            dnn.defvjp(lambda s, x: (sel_nn(s, x), s),
                       lambda s, g: (jnp.zeros_like(s), _dg_sel(s, g, d_tn, True)))
            dnt = jax.custom_vjp(sel_nt)
            dnt.defvjp(lambda s, x: (sel_nt(s, x), s),
                       lambda s, g: (jnp.zeros_like(s), _dg_sel(s, g, d_tn, False)))
            sel_nn, sel_nt = dnn, dnt
        self.sel_nn, self.sel_nt = sel_nn, sel_nt


class _OpSet:
    def __init__(self, differentiable):
        self.b = _Ops("b", differentiable)
        self.h = _Ops("h", differentiable)
        self.bb = _Ops("b", differentiable, batched=True)
        self.bm = _Ops("m", differentiable, batched=True)
        self.bs = _SelOps(differentiable, batched=True)


_PLAIN = _OpSet(False)
_DIFF = _OpSet(True)


def _params(sem):
    return pltpu.CompilerParams(dimension_semantics=sem)


BLOCK_BYTES = 4 * 1024 * 1024


def _pick(n, cands):
    for c in cands:
        if n % c == 0:
            return c
    return n


def _tile(n, cap):
    if n <= cap:
        return n
    return _pick(n, tuple(c for c in (2048, 1024, 768, 512, 384, 256, 128) if c <= cap))


def matmul(a, b, form, out_dtype, name, res=None, blocked=False, relu_gate=None, rms_gain=None, relu2_out=False):
    if form == "nn":
        m, k = a.shape
        k2, n = (b.shape[1], N_DEV * b.shape[2]) if blocked else b.shape
    elif form == "nt":
        m, k = a.shape
        n, k2 = (b.shape[1], N_DEV * b.shape[2]) if blocked else b.shape
    else:
        (k, m), (k2, n) = a.shape, b.shape
    assert k == k2, (a.shape, b.shape, form)
    tk = k if k <= 2048 else _tile(k, 1024)
    cb = nb = 1
    if blocked:
        cb = (k if form == "nt" else n) // N_DEV
        nb = _pick(N_DEV, tuple(c for c in (8, 4, 2, 1) if c * cb <= 1024))
    if blocked and form == "nt":
        tk = nb * cb
    if blocked and form != "nt":
        tn = nb * cb
    else:
        tn = _tile(n, min(1024, BLOCK_BYTES // (tk * b.dtype.itemsize)))
    out_elems = BLOCK_BYTES // 2 if (out_dtype == BF16 and res is None) else BLOCK_BYTES // 4
    tm = _tile(m, min(BLOCK_BYTES // (tk * a.dtype.itemsize), out_elems // tn))
    nk = k // tk
    dims = {"nn": _NN, "nt": _NT, "tn": _TN}[form]

    a_spec = {"nn": pl.BlockSpec((tm, tk), lambda i, j, kk: (i, kk)),
              "nt": pl.BlockSpec((tm, tk), lambda i, j, kk: (i, kk)),
              "tn": pl.BlockSpec((tk, tm), lambda i, j, kk: (kk, i))}[form]
    if blocked and form == "nn":
        b_spec = pl.BlockSpec((nb, tk, cb), lambda i, j, kk: (j, kk, 0))
    elif blocked and form == "nt":
        b_spec = pl.BlockSpec((nb, tn, cb), lambda i, j, kk: (kk, j, 0))
    else:
        b_spec = {"nn": pl.BlockSpec((tk, tn), lambda i, j, kk: (kk, j)),
                  "nt": pl.BlockSpec((tn, tk), lambda i, j, kk: (j, kk)),
                  "tn": pl.BlockSpec((tk, tn), lambda i, j, kk: (kk, j))}[form]
    c_spec = pl.BlockSpec((tm, tn), lambda i, j, kk: (i, j))
    out_shape = jax.ShapeDtypeStruct((m, n), out_dtype)
    o_spec = c_spec
    blocked_out = blocked and form == "tn"
    if blocked_out:
        out_shape = jax.ShapeDtypeStruct((N_DEV, m, cb), out_dtype)
        o_spec = pl.BlockSpec((nb, tm, cb), lambda i, j, kk: (j, i, 0))
    has_res, has_gate, has_gain = res is not None, relu_gate is not None, rms_gain is not None
    extras = [e for e in (res, relu_gate) if e is not None]
    n_in = 2 + len(extras) + has_gain
    second = has_gain or relu2_out
    assert not (second and (blocked_out or tn != n and has_gain))

    def body(*refs):
        a_ref, b_ref = refs[0], refs[1]
        r_ref = refs[2] if has_res else None
        gate_ref = refs[2 + has_res] if has_gate else None
        gain_ref = refs[n_in - 1] if has_gain else None
        o_ref = refs[n_in]
        a_val = a_ref[...].astype(BF16)
        if blocked and form == "nn":
            part = jnp.concatenate([_dot(a_val, b_ref[t].astype(BF16), dims) for t in range(nb)], axis=-1)
        elif blocked and form == "nt":
            part = _dot(a_val[:, :cb], b_ref[0].astype(BF16), dims)
            for t in range(1, nb):
                part = part + _dot(a_val[:, t * cb:(t + 1) * cb], b_ref[t].astype(BF16), dims)
        else:
            part = _dot(a_val, b_ref[...].astype(BF16), dims)

        def finish(acc):
            if has_res:
                acc = acc + r_ref[...].astype(F32)
            if has_gate:
                acc = acc * (2.0 * jnp.maximum(gate_ref[...].astype(F32), 0.0))
            if blocked_out:
                for t in range(nb):
                    o_ref[t] = acc[:, t * cb:(t + 1) * cb].astype(out_dtype)
            else:
                o_ref[...] = acc.astype(out_dtype)
            if has_gain:
                refs[n_in + 1][...] = _rms(acc, gain_ref[...]).astype(BF16)
            if relu2_out:
                r = jnp.maximum(acc.astype(out_dtype).astype(F32), 0.0)
                refs[n_in + 1][...] = (r * r).astype(BF16)

        if nk == 1:
            finish(part)
        else:
            acc_ref = refs[-1]
            kk = pl.program_id(2)

            @pl.when(kk == 0)
            def _():
                acc_ref[...] = part

            @pl.when(jnp.logical_and(kk > 0, kk < nk - 1))
            def _():
                acc_ref[...] += part

            @pl.when(kk == nk - 1)
            def _():
                finish(acc_ref[...] + part)

    in_specs = [a_spec, b_spec] + [c_spec] * len(extras)
    args = [a, b] + extras
    if has_gain:
        in_specs.append(pl.BlockSpec((1, tn), lambda i, j, kk: (0, j)))
        args.append(rms_gain)
    if second:
        out_shape = [out_shape, jax.ShapeDtypeStruct((m, n), BF16)]
        o_spec = [o_spec, c_spec]
    return pl.pallas_call(
        body, name=name,
        out_shape=out_shape,
        grid=(m // tm, n // tn, nk),
        in_specs=in_specs, out_specs=o_spec,
        scratch_shapes=[pltpu.VMEM((tm, tn), F32)] if nk > 1 else [],
        compiler_params=_params(("parallel", "parallel", "arbitrary")),
    )(*args)


def make_mm(name, out_dtype, with_res=False, blocked=False):
    def bwd_mm(a, w, g):
        da = matmul(g, w, "nt", a.dtype, name + "_da", blocked=blocked)
        dw = matmul(a, g, "tn", w.dtype, name + "_dw", blocked=blocked)
        return da, dw

    if with_res:
        @jax.custom_vjp
        def op(res, a, w):
            return matmul(a, w, "nn", out_dtype, name + "_f", res=res, blocked=blocked)

        def fwd(res, a, w):
            return op(res, a, w), (a, w)

        def bwd(saved, g):
            return (g,) + bwd_mm(*saved, g)
    else:
        @jax.custom_vjp
        def op(a, w):
            return matmul(a, w, "nn", out_dtype, name + "_f", blocked=blocked)

        def fwd(a, w):
            return op(a, w), (a, w)

        def bwd(saved, g):
            return bwd_mm(*saved, g)
    op.defvjp(fwd, bwd)
    return op


def _rms_fan_bwd(x_new, gain, dx, dh, name):
    rows, d = x_new.shape
    outs = [("row", d, F32), ("row", d, BF16)]
    return tile_bwd(fn_fan_rms, name, ["row", "par"], [x_new, gain], [True, True], outs, [dx, dh],
                    rows, min(512, rows), 0)


def make_mm_res_rms(name):
    @jax.custom_vjp
    def op(res, a, w, gain):
        return tuple(matmul(a, w, "nn", F32, name + "_f", res=res, rms_gain=gain))

    def fwd(res, a, w, gain):
        x_new, h = op(res, a, w, gain)
        return (x_new, h), (a, w, x_new, gain)

    def bwd(saved, cts):
        a, w, x_new, gain = saved
        dx, dgain = _rms_fan_bwd(x_new, gain, cts[0], cts[1], name + "_nb")
        da = matmul(dx, w, "nt", a.dtype, name + "_da")
        dw = matmul(a, dx, "tn", w.dtype, name + "_dw")
        return dx, da, dw, dgain

    op.defvjp(fwd, bwd)
    return op


def make_mlp(name, with_norm):
    def run(x, h, w1, w2, gain):
        a, bsq = matmul(h, w1, "nn", BF16, name + "_1_f", blocked=True, relu2_out=True)
        out = matmul(bsq, w2, "nn", F32, name + "_2_f", res=x, rms_gain=gain if with_norm else None)
        return (tuple(out) if with_norm else (out,)), a, bsq

    @jax.custom_vjp
    def op(x, h, w1, w2, gain):
        return run(x, h, w1, w2, gain)[0]

    def fwd(x, h, w1, w2, gain):
        out, a, bsq = run(x, h, w1, w2, gain)
        return out, (h, w1, w2, gain, a, bsq, out[0])

    def bwd(saved, cts):
        h, w1, w2, gain, a, bsq, x_new = saved
        if with_norm:
            dx, dgain = _rms_fan_bwd(x_new, gain, cts[0], cts[1], name + "_nb")
        else:
            dx, dgain = cts[0], None
        da = matmul(dx, w2, "nt", BF16, name + "_2_da", relu_gate=a)
        dw2 = matmul(bsq, dx, "tn", w2.dtype, name + "_2_dw")
        dh = matmul(da, w1, "nt", h.dtype, name + "_1_da", blocked=True)
        dw1 = matmul(h, da, "tn", w1.dtype, name + "_1_dw", blocked=True)
        return dx, dh, dw1, dw2, dgain

    op.defvjp(fwd, bwd)
    return op


def _kind(k):
    if isinstance(k, str):
        return k, None, 1
    return k[0], k[1], (k[2] if len(k) > 2 else 1)


def _tile_spec(kind, shape, tm, heads):
    k, d, ns = _kind(kind)
    if k == "row":
        return pl.BlockSpec((tm, shape[1]), (lambda h, i: (i, 0)) if heads else (lambda i: (i, 0)))
    if k == "par":
        return pl.BlockSpec(tuple(shape), (lambda h, i: (0, 0)) if heads else (lambda i: (0, 0)))
    if k == "rowh":
        return pl.BlockSpec((tm, d * ns), lambda h, i: (i, h))
    if k == "parh":
        return pl.BlockSpec((shape[0], d * ns), lambda h, i: (0, h))
    raise ValueError(kind)


def _tile_grid(rows, tm, heads):
    n_rows = rows // tm
    return ((heads, n_rows) if heads else (n_rows,)), (1 if heads else 0)


def _split_vals(kinds, refs):
    vals, counts = [], []
    for kind, r in zip(kinds, refs):
        _, d, ns = _kind(kind)
        v = r[...].astype(F32)
        vals += [v] if ns == 1 else [v[:, p * d:(p + 1) * d] for p in range(ns)]
        counts.append(ns)
    return vals, counts


def tile_fwd(fn, name, kinds, args, outs, rows, tm, heads, row_base=0):
    grid, row_axis = _tile_grid(rows, tm, heads)
    n_in = len(args)
    out_shapes = [jax.ShapeDtypeStruct((rows, w), dt) for (_, w, dt) in outs]

    def body(*refs):
        vals, _ = _split_vals(kinds, refs[:n_in])
        row0 = row_base + pl.program_id(row_axis) * tm
        res = list(fn(_PLAIN, row0, *vals))
        for o_ref, (k, _, _) in zip(refs[n_in:], outs):
            pieces = [res.pop(0) for _ in range(_kind(k)[2])]
            v = pieces[0] if len(pieces) == 1 else jnp.concatenate(pieces, axis=-1)
            o_ref[...] = v.astype(o_ref.dtype)

    return pl.pallas_call(
        body, name=name, out_shape=out_shapes, grid=grid,
        in_specs=[_tile_spec(k, a.shape, tm, heads) for k, a in zip(kinds, args)],
        out_specs=[_tile_spec(k, (rows, w), tm, heads) for (k, w, _) in outs],
        compiler_params=_params(("arbitrary",) * len(grid)),
    )(*args)


def tile_bwd(fn, name, kinds, args, diff, outs, cts, rows, tm, heads, row_base=0):
    grid, row_axis = _tile_grid(rows, tm, heads)
    n_in, n_ct = len(args), len(cts)
    diff_idx = [i for i, d in enumerate(diff) if d]
    g_shapes, g_specs = [], []
    for i in diff_idx:
        k = _kind(kinds[i])[0]
        dt = args[i].dtype if k in ("row", "rowh") else F32
        g_shapes.append(jax.ShapeDtypeStruct(args[i].shape, dt))
        g_specs.append(_tile_spec(kinds[i], args[i].shape, tm, heads))

    def body(*refs):
        in_refs, ct_refs, g_refs = refs[:n_in], refs[n_in:n_in + n_ct], refs[n_in + n_ct:]
        vals, counts = _split_vals(kinds, in_refs)
        first_piece = [sum(counts[:i]) for i in range(n_in)]
        flat_diff = [first_piece[i] + p for i in diff_idx for p in range(counts[i])]
        row_id = pl.program_id(row_axis)
        row0 = row_base + row_id * tm

        def f(*dvals):
            full = list(vals)
            for i, dv in zip(flat_diff, dvals):
                full[i] = dv
            return tuple(fn(_DIFF, row0, *full))

        _, vjp = jax.vjp(f, *[vals[i] for i in flat_diff])
        ct_vals, _ = _split_vals([k for (k, _, _) in outs], ct_refs)
        flat_grads = list(vjp(tuple(ct_vals)))
        for g_ref, i in zip(g_refs, diff_idx):
            pieces = [flat_grads.pop(0) for _ in range(counts[i])]
            g = pieces[0] if len(pieces) == 1 else jnp.concatenate(pieces, axis=-1)
            k = _kind(kinds[i])[0]
            if k in ("row", "rowh"):
                g_ref[...] = g.astype(g_ref.dtype)
            else:
                first = row_id == 0
                if heads and k == "par":
                    first = jnp.logical_and(first, pl.program_id(0) == 0)

                @pl.when(first)
                def _(g_ref=g_ref, g=g):
                    g_ref[...] = g

                @pl.when(jnp.logical_not(first))
                def _(g_ref=g_ref, g=g):
                    g_ref[...] += g

    return pl.pallas_call(
        body, name=name, out_shape=g_shapes, grid=grid,
        in_specs=[_tile_spec(k, a.shape, tm, heads) for k, a in zip(kinds, args)]
        + [_tile_spec(k, (rows, w), tm, heads) for (k, w, _) in outs],
        out_specs=g_specs,
        compiler_params=_params(("arbitrary",) * len(grid)),
    )(*args, *cts)


def make_tile_op(fn, name, kinds, diff, outs, rows, tm, heads=0, row_base=0):
    tm = min(tm, rows)

    @jax.custom_vjp
    def op(*args):
        return tuple(tile_fwd(fn, name + "_f", kinds, args, outs, rows, tm, heads, row_base))

    def fwd(*args):
        return op(*args), args

    def bwd(args, cts):
        grads = tile_bwd(fn, name + "_b", kinds, args, diff, outs, cts, rows, tm, heads, row_base)
        it = iter(grads)
        res = []
        for a, d in zip(args, diff):
            res.append(next(it).astype(a.dtype) if d else None)
        return tuple(res)

    op.defvjp(fwd, bwd)
    return op


def _rms(x, g):
    return x * lax.rsqrt(jnp.mean(x * x, axis=-1, keepdims=True) + EPS) * g


def fn_rms(ops, row0, x, g):
    return (_rms(x, g),)


def fn_fan_rms(ops, row0, x, g):
    return x, _rms(x, g)


def fn_mul(ops, row0, a, b):
    return (a * b,)


def fn_rope(ops, row0, *t):
    nh = (len(t) - 2) // 2
    c, s = t[-2], t[-1]
    return tuple(x * c + xs * s for x, xs in zip(t[:nh], t[nh:2 * nh]))


def _softmax(s):
    m = lax.stop_gradient(jnp.max(s, axis=-1, keepdims=True))
    e = jnp.exp(s - m)
    return e / jnp.sum(e, axis=-1, keepdims=True)


def fn_xattn(ops, row0, q, k, v):
    s = ops.b.nt(q, k) * (X_HEAD_DIM ** -0.5)
    return (ops.b.nn(_softmax(s), v),)


def fn_mla_attn(ops, row0, qn, qr, kn, v, kr):
    s = (ops.b.nt(qn, kn) + ops.b.nt(qr, kr)) * ((MLA_NOPE + MLA_ROPE) ** -0.5)
    rows = row0 + lax.broadcasted_iota(jnp.int32, s.shape, 0)
    cols = lax.broadcasted_iota(jnp.int32, s.shape, 1)
    s = jnp.where(rows >= cols, s, NEG_BIG)
    return (ops.b.nn(_softmax(s), v),)


def _silu(x):
    return x * jax.nn.sigmoid(x)


def fn_gdn_prep(ops, row0, *t):
    nh = len(t) // 3
    qs, ks, vs = [], [], []
    for qc, kc, vc in zip(t[:nh], t[nh:2 * nh], t[2 * nh:]):
        q, k = _silu(qc), _silu(kc)
        qs.append(q * lax.rsqrt(jnp.sum(q * q, -1, keepdims=True) + EPS) * (GDN_DK ** -0.5))
        ks.append(k * lax.rsqrt(jnp.sum(k * k, -1, keepdims=True) + EPS))
        vs.append(_silu(vc))
    return tuple(qs + ks + vs)


def fn_gdn_gates(ops, row0, ba, alog, dtb):
    width = GDN_HEADS * GDN_DK
    beta = jax.nn.sigmoid(ba)
    z = ba + dtb
    softplus = jnp.maximum(z, 0.0) + jnp.log1p(jnp.exp(-jnp.abs(z)))
    g = -jnp.exp(alog) * softplus
    r = lax.broadcasted_iota(jnp.int32, (LANES, width), 0)
    c = lax.broadcasted_iota(jnp.int32, (LANES, width), 1) // GDN_DK
    e_beta = (r == c).astype(F32)
    e_g = (r == c + GDN_HEADS).astype(F32)
    return ops.h.nn(beta, e_beta), ops.h.nn(g, e_g)


def fn_gdn_out(ops, row0, *t):
    nh = (len(t) - 1) // 2
    g = t[-1]
    return tuple(_rms(o, g) * _silu(gate) for o, gate in zip(t[:nh], t[nh:2 * nh]))


def _shift_down(x, d, t_idx):
    if d == 0:
        return x
    return jnp.where(t_idx >= d, pltpu.roll(x, d, axis=0), 0.0)


def _shift_up(x, d, t_idx):
    if d == 0:
        return x
    n = x.shape[0]
    return jnp.where(t_idx < n - d, pltpu.roll(x, n - d, axis=0), 0.0)


def conv_fwd(x, w, name):
    s, c = x.shape
    kw = w.shape[0]
    tc = _pick(c, (256, 128))

    def body(x_ref, w_ref, y_ref):
        xv = x_ref[...]
        t_idx = lax.broadcasted_iota(jnp.int32, xv.shape, 0)
        acc = jnp.zeros_like(xv)
        for j in range(kw):
            acc = acc + w_ref[j:j + 1, :] * _shift_down(xv, kw - 1 - j, t_idx)
        y_ref[...] = acc

    return pl.pallas_call(
        body, name=name, out_shape=jax.ShapeDtypeStruct((s, c), F32), grid=(c // tc,),
        in_specs=[pl.BlockSpec((s, tc), lambda i: (0, i)), pl.BlockSpec((kw, tc), lambda i: (0, i))],
        out_specs=pl.BlockSpec((s, tc), lambda i: (0, i)),
        compiler_params=_params(("parallel",)),
    )(x, w)


def conv_bwd(x, w, dy, name):
    s, c = x.shape
    kw = w.shape[0]
    tc = _pick(c, (256, 128))

    def body(x_ref, w_ref, dy_ref, dx_ref, dw_ref):
        xv, dyv = x_ref[...], dy_ref[...]
        t_idx = lax.broadcasted_iota(jnp.int32, xv.shape, 0)
        dx = jnp.zeros_like(xv)
        for j in range(kw):
            d = kw - 1 - j
            dx = dx + w_ref[j:j + 1, :] * _shift_up(dyv, d, t_idx)
            dw_ref[j:j + 1, :] = jnp.sum(dyv * _shift_down(xv, d, t_idx), axis=0, keepdims=True)
        dx_ref[...] = dx

    return pl.pallas_call(
        body, name=name,
        out_shape=[jax.ShapeDtypeStruct((s, c), F32), jax.ShapeDtypeStruct((kw, c), F32)],
        grid=(c // tc,),
        in_specs=[pl.BlockSpec((s, tc), lambda i: (0, i)), pl.BlockSpec((kw, tc), lambda i: (0, i)),
                  pl.BlockSpec((s, tc), lambda i: (0, i))],
        out_specs=[pl.BlockSpec((s, tc), lambda i: (0, i)), pl.BlockSpec((kw, tc), lambda i: (0, i))],
        compiler_params=_params(("parallel",)),
    )(x, w, dy)


def make_conv(name):
    @jax.custom_vjp
    def op(x, w):
        return conv_fwd(x, w, name + "_f")

    def fwd(x, w):
        return op(x, w), (x, w)

    def bwd(saved, dy):
        dx, dw = conv_bwd(saved[0], saved[1], dy, name + "_b")
        return dx, dw

    op.defvjp(fwd, bwd)
    return op


def _gdn_consts():
    c, d = GDN_CHUNK, GDN_DK
    i = lax.broadcasted_iota(jnp.int32, (c, c), 0)
    j = lax.broadcasted_iota(jnp.int32, (c, c), 1)
    tri = i >= j
    return dict(
        tri=tri, strict=i > j,
        tri_f=tri.astype(F32),
        eye=(i == j).astype(F32),
        lane0=(lax.broadcasted_iota(jnp.int32, (c, d), 1) == 0).astype(F32),
        last_row=(lax.broadcasted_iota(jnp.int32, (c, d), 0) == c - 1).astype(F32),
    )


def _gdn_chunk(ops, q, k, v, g, beta, state):
    b, m, sel = ops.bb, ops.bm, ops.bs
    nh, c, d = q.shape[0], GDN_CHUNK, GDN_DK
    k_ = _gdn_consts()

    def per_head(a):
        return jnp.broadcast_to(a, (nh,) + a.shape)

    gc = sel.sel_nn(per_head(k_["tri_f"]), g)
    col = jnp.broadcast_to(jnp.sum(gc * k_["lane0"], axis=2, keepdims=True), (nh, c, c))
    row = sel.sel_nt(per_head(k_["lane0"]), gc)
    decay = jnp.where(k_["tri"], jnp.exp(jnp.where(k_["tri"], col - row, 0.0)), 0.0)
    kb = k * beta
    mm_ = jnp.where(k_["strict"], b.nt(kb, k) * decay, 0.0)
    p = -mm_
    t = k_["eye"] + p
    for _ in range(int(math.log2(GDN_CHUNK)) - 1):
        p = m.nn(p, p)
        t = t + m.nn(t, p)
    egc = jnp.exp(gc)
    u = b.nn(t, v * beta)
    w = b.nn(t, kb * egc)
    attn = b.nt(q, k) * decay
    v_new = u - b.nn(w, state)
    o = b.nn(q * egc, state) + b.nn(attn, v_new)
    g_last = jnp.sum(gc * k_["last_row"], axis=1, keepdims=True)
    new_state = (state * jnp.exp(jnp.broadcast_to(g_last, (nh, d, d)))
                 + b.tn(k * jnp.exp(jnp.broadcast_to(g_last, (nh, c, d)) - gc), v_new))
    return o, new_state


GDN_HEAD_GROUP = 8
GDN_TILE_CHUNKS = 4


def _heads_of(ref, rows, n_heads):
    d = GDN_DK
    return jnp.stack([ref[rows, h * d:(h + 1) * d] for h in range(n_heads)])


def _gdn_specs(s, reverse):
    d, hg = GDN_DK, GDN_HEAD_GROUP
    tile = min(GDN_TILE_CHUNKS * GDN_CHUNK, s)
    n_tiles = s // tile
    t_of = (lambda t: n_tiles - 1 - t) if reverse else (lambda t: t)
    seq = pl.BlockSpec((tile, hg * d), lambda grp, t: (t_of(t), grp))
    st = pl.BlockSpec((hg, tile // GDN_CHUNK, d, d), lambda grp, t: (grp, t_of(t), 0, 0))
    return seq, st, tile, n_tiles


def gdn_fwd(q, k, v, g, beta, name):
    s = q.shape[0]
    d, hg = GDN_DK, GDN_HEAD_GROUP
    seq, st, tile, n_tiles = _gdn_specs(s, False)

    def body(q_ref, k_ref, v_ref, g_ref, b_ref, o_ref, st_ref, state_scr):
        @pl.when(pl.program_id(1) == 0)
        def _():
            state_scr[...] = jnp.zeros_like(state_scr)

        def step(ci, carry):
            rows = pl.ds(pl.multiple_of(ci * GDN_CHUNK, GDN_CHUNK), GDN_CHUNK)
            state = state_scr[...]
            for h in range(hg):
                st_ref[h, ci] = state[h]
            o, new_state = _gdn_chunk(_PLAIN, *[_heads_of(r, rows, hg) for r in (q_ref, k_ref, v_ref, g_ref, b_ref)],
                                      state)
            for h in range(hg):
                o_ref[rows, h * d:(h + 1) * d] = o[h]
            state_scr[...] = new_state
            return carry

        lax.fori_loop(0, tile // GDN_CHUNK, step, 0)

    return pl.pallas_call(
        body, name=name,
        out_shape=[jax.ShapeDtypeStruct(q.shape, F32),
                   jax.ShapeDtypeStruct((GDN_HEADS, s // GDN_CHUNK, d, d), F32)],
        grid=(GDN_HEADS // hg, n_tiles), in_specs=[seq] * 5, out_specs=[seq, st],
        scratch_shapes=[pltpu.VMEM((hg, d, d), F32)],
        compiler_params=_params(("parallel", "arbitrary")),
    )(q, k, v, g, beta)


def gdn_bwd(q, k, v, g, beta, states, do, name):
    s = q.shape[0]
    d, hg = GDN_DK, GDN_HEAD_GROUP
    seq, st, tile, n_tiles = _gdn_specs(s, True)
    tile_chunks = tile // GDN_CHUNK

    def body(q_ref, k_ref, v_ref, g_ref, b_ref, st_ref, do_ref, dq_ref, dk_ref, dv_ref, dg_ref, db_ref, dstate_scr):
        @pl.when(pl.program_id(1) == 0)
        def _():
            dstate_scr[...] = jnp.zeros_like(dstate_scr)

        def step(it, carry):
            ci = tile_chunks - 1 - it
            rows = pl.ds(pl.multiple_of(ci * GDN_CHUNK, GDN_CHUNK), GDN_CHUNK)
            prim = [_heads_of(r, rows, hg) for r in (q_ref, k_ref, v_ref, g_ref, b_ref)]
            prim.append(jnp.stack([st_ref[h, ci] for h in range(hg)]))
            _, vjp = jax.vjp(functools.partial(_gdn_chunk, _DIFF), *prim)
            grads = vjp((_heads_of(do_ref, rows, hg), dstate_scr[...]))
            for g_ref_out, gr in zip((dq_ref, dk_ref, dv_ref, dg_ref, db_ref), grads[:5]):
                for h in range(hg):
                    g_ref_out[rows, h * d:(h + 1) * d] = gr[h]
            dstate_scr[...] = grads[5]
            return carry

        lax.fori_loop(0, tile_chunks, step, 0)

    return pl.pallas_call(
        body, name=name,
        out_shape=[jax.ShapeDtypeStruct(q.shape, F32)] * 5,
        grid=(GDN_HEADS // hg, n_tiles), in_specs=[seq] * 5 + [st, seq], out_specs=[seq] * 5,
        scratch_shapes=[pltpu.VMEM((hg, d, d), F32)],
        compiler_params=_params(("parallel", "arbitrary")),
    )(q, k, v, g, beta, states, do)


def make_gdn(name):
    @jax.custom_vjp
    def op(q, k, v, g, beta):
        return gdn_fwd(q, k, v, g, beta, name + "_f")[0]

    def fwd(q, k, v, g, beta):
        o, states = gdn_fwd(q, k, v, g, beta, name + "_f")
        return o, (q, k, v, g, beta, states)

    def bwd(saved, do):
        return tuple(gdn_bwd(*saved, do, name + "_b"))

    op.defvjp(fwd, bwd)
    return op


def loss_head(x, g, target, name):
    s, d = x.shape
    tm = min(256, s)

    def body(x_ref, g_ref, t_ref, loss_ref, dx_ref, dg_ref):
        tgt = t_ref[...]

        def f(xv, gv):
            err = _rms(xv, gv) - tgt
            per_row = jnp.mean(err * err, axis=-1, keepdims=True)
            return 0.5 * jnp.sum(per_row, axis=0, keepdims=True)

        val, vjp = jax.vjp(f, x_ref[...], g_ref[...])
        dx, dg = vjp(jnp.ones((1, 1), F32))
        dx_ref[...] = dx
        first = pl.program_id(0) == 0

        @pl.when(first)
        def _():
            dg_ref[...] = dg
            loss_ref[...] = jnp.broadcast_to(val, loss_ref.shape)

        @pl.when(jnp.logical_not(first))
        def _():
            dg_ref[...] += dg
            loss_ref[...] += jnp.broadcast_to(val, loss_ref.shape)

    row = pl.BlockSpec((tm, d), lambda i: (i, 0))
    vec = pl.BlockSpec((1, d), lambda i: (0, 0))
    return pl.pallas_call(
        body, name=name,
        out_shape=[jax.ShapeDtypeStruct((1, LANES), F32), jax.ShapeDtypeStruct((s, d), F32),
                   jax.ShapeDtypeStruct((1, d), F32)],
        grid=(s // tm,), in_specs=[row, vec, row],
        out_specs=[pl.BlockSpec((1, LANES), lambda i: (0, 0)), row, vec],
        compiler_params=_params(("arbitrary",)),
    )(x, g, target)


def adamw(g8, w, m, v, layer, prev, name):
    n_layers, rows, width = w.shape
    tr = _pick(rows, (256, 128, 64, 32, 16, 8))

    def body(g_ref, w_ref, m_ref, v_ref, *rest):
        go_ref, d_ref, mo_ref, vo_ref = rest[-4:]
        g = g_ref[0].astype(F32)
        for p in range(1, N_DEV):
            g = g + g_ref[p].astype(F32)
        m_new = ADAM_B1 * m_ref[...] + (1.0 - ADAM_B1) * g
        v_new = ADAM_B2 * v_ref[...] + (1.0 - ADAM_B2) * (g * g)
        m_hat = m_new / (1.0 - ADAM_B1 ** ADAM_STEP)
        v_hat = v_new / (1.0 - ADAM_B2 ** ADAM_STEP)
        go_ref[...] = g
        d_ref[...] = -ADAM_LR * (m_hat / (jnp.sqrt(v_hat) + ADAM_EPS) + ADAM_WD * w_ref[...])
        mo_ref[...] = m_new
        vo_ref[...] = v_new

    blk = pl.BlockSpec((None, tr, width), lambda i: (layer, i, 0))
    carried = list(prev) if prev is not None else []
    return pl.pallas_call(
        body, name=name, out_shape=[jax.ShapeDtypeStruct((n_layers, rows, width), F32)] * 4,
        grid=(rows // tr,),
        in_specs=[pl.BlockSpec((N_DEV, tr, width), lambda i: (0, i, 0)), blk, blk, blk]
        + [pl.BlockSpec(memory_space=pl.ANY)] * len(carried),
        out_specs=[blk] * 4,
        input_output_aliases={4 + j: j for j in range(len(carried))},
        compiler_params=_params(("parallel",)),
    )(g8, w, m, v, *carried)


_HBM = pl.BlockSpec(memory_space=pltpu.HBM)
_SEM = pl.BlockSpec(memory_space=pltpu.SEMAPHORE)
_EFFECT = pltpu.SideEffectType.DATAFLOW_SIDE_EFFECTING


def _exchange_copies(mode, src_refs, land_refs, send_sems, recv_sems, local_sems):
    x, y, c = lax.axis_index("x"), lax.axis_index("y"), lax.axis_index("c")
    me = 4 * x + 2 * y + c
    n = len(src_refs)

    def src(k, p):
        return src_refs[k] if mode == "gather" else src_refs[k].at[p]

    local = [pltpu.make_async_copy(src(k, me), land_refs[k].at[me], local_sems.at[k]) for k in range(n)]
    sends, recvs = [], []
    for k in range(n):
        for r in range(1, N_DEV):
            px = (1 - x) if r & 4 else x
            py = (1 - y) if r & 2 else y
            pc = (1 - c) if r & 1 else c
            p = 4 * px + 2 * py + pc
            sem = k * (N_DEV - 1) + r - 1
            sends.append(pltpu.make_async_remote_copy(
                src_ref=src(k, p), dst_ref=land_refs[k].at[me],
                send_sem=send_sems.at[sem], recv_sem=recv_sems.at[sem],
                device_id=(px, py, pc), device_id_type=pl.DeviceIdType.MESH))
            recvs.append(pltpu.make_async_remote_copy(
                src_ref=src(k, p), dst_ref=land_refs[k].at[p],
                send_sem=send_sems.at[sem], recv_sem=recv_sems.at[sem],
                device_id=(px, py, pc), device_id_type=pl.DeviceIdType.MESH))
    return local, sends, recvs


def exchange_start(mode, arrays, name, after=None, carry=()):
    n, nc = len(arrays), len(carry)
    extra = [] if after is None else [after]
    land_shapes = [((N_DEV,) + tuple(a.shape)) if mode == "gather" else tuple(a.shape) for a in arrays]
    lands = [pltpu.with_memory_space_constraint(lax.empty(shp, a.dtype), pltpu.HBM)
             for shp, a in zip(land_shapes, arrays)]
    srcs = [pltpu.with_memory_space_constraint(a, pltpu.HBM) for a in arrays]
    carried = [pltpu.with_memory_space_constraint(a, pltpu.HBM) for a in carry]

    def body(*refs):
        src_refs, land_refs = refs[:n], refs[n:2 * n]
        first_out = 2 * n + nc + len(extra)
        send_sems, recv_sems, local_sems = refs[first_out:first_out + 3]
        token = refs[-1]
        local, sends, _ = _exchange_copies(mode, src_refs, land_refs, send_sems, recv_sems, local_sems)
        for cp in local + sends:
            cp.start()
        token[...] = jnp.zeros_like(token)

    n_sem = n * (N_DEV - 1)
    out = pl.pallas_call(
        body, name=name,
        out_shape=(pltpu.SemaphoreType.DMA((n_sem,)), pltpu.SemaphoreType.DMA((n_sem,)),
                   pltpu.SemaphoreType.DMA((n,)),
                   *[pltpu.HBM(a.shape, a.dtype) for a in arrays],
                   *[pltpu.HBM(shp, a.dtype) for shp, a in zip(land_shapes, arrays)],
                   *[pltpu.HBM(a.shape, a.dtype) for a in carry],
                   jax.ShapeDtypeStruct((8, LANES), F32)),
        in_specs=[_HBM] * (2 * n + nc) + [pl.BlockSpec(memory_space=pl.ANY)] * len(extra),
        out_specs=(_SEM, _SEM, _SEM, *[_HBM] * (2 * n + nc), pl.BlockSpec(memory_space=pltpu.VMEM)),
        input_output_aliases={i: 3 + i for i in range(2 * n + nc)},
        compiler_params=pltpu.CompilerParams(has_side_effects=_EFFECT),
    )(*srcs, *lands, *carried, *extra)
    handle = dict(mode=mode, sems=out[:3], srcs=out[3:3 + n], lands=out[3 + n:3 + 2 * n])
    return handle, out[-1], list(out[3 + 2 * n:3 + 2 * n + nc])


def exchange_wait(handle, after, name):
    mode, srcs, lands = handle["mode"], list(handle["srcs"]), list(handle["lands"])
    n = len(srcs)

    def body(*refs):
        src_refs, land_refs = refs[:n], refs[n:2 * n]
        send_sems, recv_sems, local_sems = refs[2 * n:2 * n + 3]
        local, sends, recvs = _exchange_copies(mode, src_refs, land_refs, send_sems, recv_sems, local_sems)
        for cp in sends:
            cp.wait_send()
        for cp in recvs:
            cp.wait_recv()
        for cp in local:
            cp.wait()

    out = pl.pallas_call(
        body, name=name,
        out_shape=(*[pltpu.HBM(a.shape, a.dtype) for a in srcs], *[pltpu.HBM(a.shape, a.dtype) for a in lands]),
        in_specs=[_HBM] * (2 * n) + [_SEM] * 3 + [pl.BlockSpec(memory_space=pl.ANY)],
        out_specs=tuple([_HBM] * (2 * n)),
        input_output_aliases={i: i for i in range(2 * n)},
        compiler_params=pltpu.CompilerParams(has_side_effects=_EFFECT),
    )(*srcs, *lands, *handle["sems"], after)
    return list(out[n:])


BIG = ["mla_w_in", "mla_w_uq", "mla_w_ukv", "mla_w_o", "gdn_w_in", "gdn_w_o", "sc_w_in", "sc_w_o",
       "xa_w_q", "xa_w_kv", "xa_w_o", "mlp_w1", "mlp_w2"]
TINY = [("mla_q_norm", 1), ("mla_kv_norm", 1), ("gdn_conv_w", 2), ("sc_conv_w", 2)]
REPL = ["gdn_a_log", "gdn_dt_bias", "gdn_o_norm", "norm_mix", "norm_mem", "norm_mlp", "mem_norm", "final_norm"]
WEIGHTS = ["mla_w_in", "mla_q_norm", "mla_kv_norm", "mla_w_uq", "mla_w_ukv", "mla_w_o", "gdn_w_in",
           "gdn_conv_w", "gdn_a_log", "gdn_dt_bias", "gdn_o_norm", "gdn_w_o", "sc_w_in", "sc_conv_w",
           "sc_w_o", "norm_mix", "norm_mem", "norm_mlp", "xa_w_q", "xa_w_kv", "xa_w_o", "mlp_w1",
           "mlp_w2", "mem_norm", "final_norm"]
MIXER_WEIGHTS = (["mla_w_in", "mla_w_uq", "mla_w_ukv", "mla_w_o"], ["gdn_w_in", "gdn_w_o"], ["sc_w_in", "sc_w_o"])
MIXER_PARAMS = (["norm_mem", "mla_q_norm", "mla_kv_norm"],
                ["norm_mem", "gdn_conv_w", "gdn_a_log", "gdn_dt_bias", "gdn_o_norm"],
                ["norm_mem", "sc_conv_w"])


def from_shards(a8, axis):
    a = jnp.moveaxis(a8, 0, axis)
    shp = a.shape
    return a.reshape(shp[:axis] + (shp[axis] * shp[axis + 1],) + shp[axis + 2:])


def pack_rows(flat_list, width, row_mult):
    total = sum(a.shape[-1] for a in flat_list)
    rows = -(-total // width)
    rows = -(-rows // row_mult) * row_mult
    pad = rows * width - total
    parts = list(flat_list)
    if pad:
        parts.append(jnp.zeros((pad,), flat_list[0].dtype))
    return jnp.concatenate(parts, axis=-1).reshape(rows, width)


def unpack_rows(packed, shapes):
    lead = packed.shape[:-2]
    flat = packed.reshape(lead + (-1,))
    out, off = [], 0
    for shp in shapes:
        n = math.prod(shp)
        out.append(flat[..., off:off + n].reshape(lead + tuple(shp)))
        off += n
    return out


def _swap_halves(w):
    half = w.shape[-1] // 2
    return jnp.concatenate([w[..., half:], w[..., :half]], axis=-1)


def _pad_last(w, n):
    return jnp.pad(w, [(0, 0)] * (w.ndim - 1) + [(0, n - w.shape[-1])])


def _unblock(w8):
    return jnp.transpose(w8, (1, 0, 2)).reshape(w8.shape[1], -1)


def _stack_rows(w8):
    return w8.reshape(-1, w8.shape[-1])


def rms_op(name, rows, d, out_dtype, tm=512):
    return make_tile_op(fn_rms, name, ["row", "par"], [True, True], [("row", d, out_dtype)], rows, min(tm, rows))


def seg_memory(p, mem):
    return rms_op("rms_memory", mem.shape[0], mem.shape[1], BF16)(mem, p["mem_norm"].reshape(1, -1))[0]


def seg_mixer(i, wts, p, x, h, rope_c, rope_s):
    s, d = x.shape
    j, kind = i // N_MIXERS, i % N_MIXERS
    tag = f"l{i}"
    hd = MLA_NOPE
    next_gain = p["norm_mem"][i].reshape(1, d)
    if kind == 0:
        w_in = _stack_rows(wts["mla_w_in"])
        w_cq = w_in[:, :MLA_Q_RANK]
        w_ckv = w_in[:, MLA_Q_RANK:MLA_Q_RANK + MLA_KV_RANK]
        w_kr = w_in[:, MLA_Q_RANK + MLA_KV_RANK:]
        w_z = jnp.concatenate([w_cq, w_ckv, _pad_last(w_kr, hd), _pad_last(_swap_halves(w_kr), hd)], axis=-1)
        z = make_mm(tag + "_mla_in", F32)(h, w_z)
        c_q, c_kv = z[:, :MLA_Q_RANK], z[:, MLA_Q_RANK:MLA_Q_RANK + MLA_KV_RANK]
        kr_raw, kr_swp = z[:, -2 * hd:-hd], z[:, -hd:]
        c_qn = rms_op(tag + "_rms_q", s, MLA_Q_RANK, BF16)(c_q, p["mla_q_norm"][j].reshape(1, -1))[0]
        c_kvn = rms_op(tag + "_rms_kv", s, MLA_KV_RANK, BF16)(c_kv, p["mla_kv_norm"][j].reshape(1, -1))[0]
        w_uq8 = wts["mla_w_uq"]
        w_qn = _unblock(w_uq8[:, :, :MLA_NOPE])
        w_qr = w_uq8[:, :, MLA_NOPE:]
        w_qr_p = _unblock(_pad_last(w_qr, hd))
        w_qr_s = _unblock(_pad_last(_swap_halves(w_qr), hd))
        q_nope = make_mm(tag + "_mla_uq_n", BF16)(c_qn, w_qn)
        q_raw = make_mm(tag + "_mla_uq_r", F32)(c_qn, w_qr_p)
        q_swp = make_mm(tag + "_mla_uq_s", F32)(c_qn, w_qr_s)
        nq = MLA_HEADS * hd
        kv = make_mm(tag + "_mla_ukv", BF16, blocked=True)(c_kvn, wts["mla_w_ukv"])
        heads_row = ("row", hd, MLA_HEADS)
        q_rope = make_tile_op(fn_rope, tag + "_rope_q", [heads_row, heads_row, "row", "row"],
                              [True, True, False, False], [(heads_row, nq, F32)], s, 512)(
            q_raw, q_swp, rope_c, rope_s)[0]
        k_rope = make_tile_op(fn_rope, tag + "_rope_k", ["row", "row", "row", "row"],
                              [True, True, False, False], [("row", hd, F32)], s, 1024)(
            kr_raw, kr_swp, rope_c, rope_s)[0]
        n_groups = MLA_QUERY_GROUPS if s % (MLA_QUERY_GROUPS * 256) == 0 else 1
        rows_g = s // n_groups
        o_groups = []
        for grp in range(n_groups):
            r0, r1 = grp * rows_g, (grp + 1) * rows_g
            o_groups.append(make_tile_op(
                fn_mla_attn, f"{tag}_mla_attn{grp}", [("rowh", hd), ("rowh", hd), ("parh", hd, 2), "par"],
                [True] * 4, [(("rowh", hd), nq, BF16)], rows_g, 256, MLA_HEADS, row_base=r0)(
                q_nope[r0:r1], q_rope[r0:r1], kv[:r1], k_rope[:r1])[0])
        o = jnp.concatenate(o_groups, axis=0)
        return make_mm_res_rms(tag + "_mla_o")(x, o, _stack_rows(wts["mla_w_o"]), next_gain)
    if kind == 1:
        ng = GDN_HEADS * GDN_DK
        w_in = _unblock(wts["gdn_w_in"])
        cw = p["gdn_conv_w"][j]
        conv_out = []
        for part, nm in enumerate(("q", "k", "v")):
            cols = slice(part * ng, (part + 1) * ng)
            pre = make_mm(f"{tag}_gdn_in_{nm}", F32)(h, w_in[:, cols])
            conv_out.append(make_conv(f"{tag}_gdn_conv_{nm}")(pre, cw[:, cols]))
        gate = make_mm(tag + "_gdn_in_g", F32)(h, w_in[:, 3 * ng:4 * ng])
        ba = make_mm(tag + "_gdn_in_ba", F32)(h, _pad_last(w_in[:, 4 * ng:], LANES))
        heads_row = ("row", GDN_DK, GDN_HEADS)
        q, k, v = make_tile_op(fn_gdn_prep, tag + "_gdn_prep", [heads_row] * 3, [True] * 3,
                               [(heads_row, ng, F32)] * 3, s, 512)(*conv_out)
        alog = jnp.pad(p["gdn_a_log"][j].reshape(1, -1), ((0, 0), (GDN_HEADS, LANES - 2 * GDN_HEADS)))
        dtb = jnp.pad(p["gdn_dt_bias"][j].reshape(1, -1), ((0, 0), (GDN_HEADS, LANES - 2 * GDN_HEADS)))
        beta_b, g_b = make_tile_op(fn_gdn_gates, tag + "_gdn_gates", ["row", "par", "par"], [True] * 3,
                                   [("row", ng, F32)] * 2, s, 512)(ba, alog, dtb)
        o = make_gdn(tag + "_gdn_core")(q, k, v, g_b, beta_b)
        o = make_tile_op(fn_gdn_out, tag + "_gdn_out", [heads_row, heads_row, "par"],
                         [True] * 3, [(heads_row, ng, BF16)], s, 512)(
            o, gate, p["gdn_o_norm"][j].reshape(1, -1))[0]
        return make_mm_res_rms(tag + "_gdn_o")(x, o, _stack_rows(wts["gdn_w_o"]), next_gain)
    w_in = _unblock(wts["sc_w_in"])
    b_gate = make_mm(tag + "_sc_in_b", F32)(h, w_in[:, :d])
    c_gate = make_mm(tag + "_sc_in_c", F32)(h, w_in[:, d:2 * d])
    u = make_mm(tag + "_sc_in_u", F32)(h, w_in[:, 2 * d:])
    cu = make_tile_op(fn_mul, tag + "_sc_cu", ["row", "row"], [True, True], [("row", d, F32)], s, 512)(
        c_gate, u)[0]
    cv = make_conv(tag + "_sc_conv")(cu, p["sc_conv_w"][j])
    yv = make_tile_op(fn_mul, tag + "_sc_gate", ["row", "row"], [True, True], [("row", d, BF16)], s, 512)(
        b_gate, cv)[0]
    return make_mm_res_rms(tag + "_sc_o")(x, yv, _stack_rows(wts["sc_w_o"]), next_gain)


def seg_xattn(i, wts, p, x, hx, mem_n):
    s, d = x.shape
    tag = f"l{i}"
    q = make_mm(tag + "_xa_q", BF16)(hx, _stack_rows(wts["xa_w_q"]))
    kv = make_mm(tag + "_xa_kv", BF16, blocked=True)(mem_n, wts["xa_w_kv"])
    o = make_tile_op(fn_xattn, tag + "_xattn",
                     [("rowh", X_HEAD_DIM), ("parh", X_HEAD_DIM), ("parh", X_HEAD_DIM)], [True] * 3,
                     [(("rowh", X_HEAD_DIM), d, BF16)], s, 1024, X_HEADS)(q, kv[:, :d], kv[:, d:])[0]
    return make_mm_res_rms(tag + "_xa_o")(x, o, _stack_rows(wts["xa_w_o"]), p["norm_mlp"][i].reshape(1, d))


def seg_mlp(i, wts, p, x, hm):
    d = x.shape[1]
    gain = p["norm_mix"][i + 1].reshape(1, d) if i + 1 < DEPTH else None
    return make_mlp(f"l{i}_mlp", gain is not None)(x, hm, wts["mlp_w1"], _stack_rows(wts["mlp_w2"]), gain)


def segments():
    segs = []
    for i in range(DEPTH):
        j, kind = i // N_MIXERS, i % N_MIXERS
        segs.append((f"l{i}_mixer", [(n, j) for n in MIXER_WEIGHTS[kind]], MIXER_PARAMS[kind], "mixer"))
        segs.append((f"l{i}_xattn", [(n, i) for n in ("xa_w_q", "xa_w_kv", "xa_w_o")], ["norm_mlp"], "xattn"))
        segs.append((f"l{i}_mlp", [(n, i) for n in ("mlp_w1", "mlp_w2")], ["norm_mix"] if i + 1 < DEPTH else [],
                     "mlp"))
    return segs


def run_segment(index, kind, wts, p, x, h, mem_n, rope_c, rope_s):
    layer = index // 3
    if kind == "mixer":
        return seg_mixer(layer, wts, p, x, h, rope_c, rope_s)
    if kind == "xattn":
        return seg_xattn(layer, wts, p, x, h, mem_n)
    return seg_mlp(layer, wts, p, x, h)


def rope_tables(positions):
    inv_freq = ROPE_THETA ** (-jnp.arange(0, MLA_ROPE, 2, dtype=F32) / MLA_ROPE)
    ang = positions.astype(F32)[:, None] * inv_freq
    cos, sin = jnp.cos(ang), jnp.sin(ang)
    zeros = jnp.zeros((positions.shape[0], MLA_NOPE - MLA_ROPE), F32)
    return jnp.concatenate([cos, cos, zeros], axis=-1), jnp.concatenate([-sin, sin, zeros], axis=-1)


def kernel(x, mem, positions, mla_w_in, mla_q_norm, mla_kv_norm, mla_w_uq, mla_w_ukv, mla_w_o, gdn_w_in, gdn_conv_w, gdn_a_log, gdn_dt_bias, gdn_o_norm, gdn_w_o, sc_w_in, sc_conv_w, sc_w_o, norm_mix, norm_mem, norm_mlp, xa_w_q, xa_w_kv, xa_w_o, mlp_w1, mlp_w2, mem_norm, final_norm, loss_target, m_mla_w_in, m_mla_q_norm, m_mla_kv_norm, m_mla_w_uq, m_mla_w_ukv, m_mla_w_o, m_gdn_w_in, m_gdn_conv_w, m_gdn_a_log, m_gdn_dt_bias, m_gdn_o_norm, m_gdn_w_o, m_sc_w_in, m_sc_conv_w, m_sc_w_o, m_norm_mix, m_norm_mem, m_norm_mlp, m_xa_w_q, m_xa_w_kv, m_xa_w_o, m_mlp_w1, m_mlp_w2, m_mem_norm, m_final_norm, v_mla_w_in, v_mla_q_norm, v_mla_kv_norm, v_mla_w_uq, v_mla_w_ukv, v_mla_w_o, v_gdn_w_in, v_gdn_conv_w, v_gdn_a_log, v_gdn_dt_bias, v_gdn_o_norm, v_gdn_w_o, v_sc_w_in, v_sc_conv_w, v_sc_w_o, v_norm_mix, v_norm_mem, v_norm_mlp, v_xa_w_q, v_xa_w_kv, v_xa_w_o, v_mlp_w1, v_mlp_w2, v_mem_norm, v_final_norm):
    args = locals()
    w_loc = {n: args[n] for n in WEIGHTS}
    m_loc = {n: args["m_" + n] for n in WEIGHTS}
    v_loc = {n: args["v_" + n] for n in WEIGHTS}
    me = 4 * lax.axis_index("x") + 2 * lax.axis_index("y") + lax.axis_index("c")
    segs = segments()

    w16 = {n: w_loc[n].astype(BF16) for n in BIG}
    tiny_pack = pack_rows([w_loc[n].reshape(-1) for n, _ in TINY], LANES, 8)
    tiny_handle, token, _ = exchange_start("gather", [tiny_pack], "gather_start_tiny")
    gather_handles = []
    for tag, units, _, _ in segs:
        handle, token, _ = exchange_start("gather", [w16[n][layer] for n, layer in units],
                                          f"gather_start_{tag}", after=token)
        gather_handles.append(handle)

    x_cur = x[0]
    rope_c, rope_s = rope_tables(positions[0])
    tiny_all = exchange_wait(tiny_handle, token, "gather_wait_tiny")[0]
    params = {}
    for (n, ax), a8 in zip(TINY, unpack_rows(tiny_all, [w_loc[n].shape for n, _ in TINY])):
        params[n] = from_shards(a8, ax)
    for n in REPL:
        params[n] = w_loc[n]

    mem_n, vjp_memory = jax.vjp(lambda p_: seg_memory(p_, mem[0]), {"mem_norm": params["mem_norm"]})
    h_cur, vjp_first_norm = jax.vjp(
        lambda p_, x_: rms_op("l0_rms_mix", x_.shape[0], x_.shape[1], BF16)(x_, p_["norm_mix"][0].reshape(1, -1))[0],
        {"norm_mix": params["norm_mix"]}, x_cur)
    vjps = []
    for index, ((tag, units, p_names, kind), handle) in enumerate(zip(segs, gather_handles)):
        landed = exchange_wait(handle, token if index == 0 else x_cur, f"gather_wait_{tag}")
        wts = {n: a for (n, _), a in zip(units, landed)}
        p_seg = {n: params[n] for n in p_names}
        outs, vjp_seg = jax.vjp(
            lambda w_, p_, x_, h_, m_, index=index, kind=kind:
            run_segment(index, kind, w_, p_, x_, h_, m_, rope_c, rope_s),
            wts, p_seg, x_cur, h_cur, mem_n)
        x_cur, h_cur = outs[0], (outs[1] if len(outs) > 1 else None)
        vjps.append(vjp_seg)

    loss_vec, g_x, d_final = loss_head(x_cur, params["final_norm"].reshape(1, -1), loss_target[0], "loss_head")

    grads = {n: jnp.zeros_like(params[n]) for n in params}
    grads["final_norm"] = d_final.reshape(-1)
    g_mem_n = jnp.zeros_like(mem_n)
    g_h = None
    scatter_handles = []
    for (tag, units, _, _), vjp_seg in zip(reversed(segs), reversed(vjps)):
        g_wts, g_p, g_x, g_h, g_m = vjp_seg((g_x,) if g_h is None else (g_x, g_h))
        for n, g in g_p.items():
            grads[n] = grads[n] + g
        g_mem_n = g_mem_n + g_m
        handle, _, (g_x, g_h) = exchange_start("scatter", [g_wts[n] for n, _ in units], f"scatter_start_{tag}",
                                               carry=[g_x, g_h])
        scatter_handles.append((units, handle))
    grads["mem_norm"] = grads["mem_norm"] + vjp_memory(g_mem_n)[0]["mem_norm"]
    g_first, g_x_norm = vjp_first_norm(g_h)
    grads["norm_mix"] = grads["norm_mix"] + g_first["norm_mix"]
    g_x = g_x + g_x_norm

    small_names = [n for n, _ in TINY] + REPL
    small_g = pack_rows([loss_vec[0, :1]] + [grads[n].astype(F32).reshape(-1) for n in small_names], PACK_W, 8)
    small_handle, _, _ = exchange_start("gather", [small_g], "gather_start_small_grads")

    g_recv = {}
    for units, handle in scatter_handles:
        landed = exchange_wait(handle, g_x, f"scatter_wait_{units[0][0]}_{units[0][1]}")
        g_recv.update(dict(zip(units, landed)))

    res = {}
    for n in BIG:
        outs = None
        for layer in range(w_loc[n].shape[0]):
            outs = adamw(g_recv[n, layer], w_loc[n], m_loc[n], v_loc[n], layer, outs, f"adamw_{n}_{layer}")
        for kind, a in zip(("grad", "delta", "m", "v"), outs):
            res[(kind, n)] = a
    small_recv = exchange_wait(small_handle, res[("grad", BIG[-1])], "gather_wait_small_grads")[0]

    def full_small(d):
        parts = [jnp.zeros((1,), F32)]
        for n, ax in TINY:
            full_shape = params[n].shape
            start = [0] * len(full_shape)
            start[ax] = me * d[n].shape[ax]
            parts.append(lax.dynamic_update_slice(jnp.zeros(full_shape, F32), d[n], start).reshape(-1))
        parts += [d[n].reshape(-1) for n in REPL]
        return pack_rows(parts, PACK_W, 8)

    outs_small = adamw(small_recv, full_small(w_loc)[None], full_small(m_loc)[None], full_small(v_loc)[None],
                       0, None, "adamw_small")
    small_shapes = [(1,)] + [params[n].shape for n, _ in TINY] + [w_loc[n].shape for n in REPL]
    loss = None
    for kind, packed in zip(("grad", "delta", "m", "v"), outs_small):
        parts = unpack_rows(packed[0], small_shapes)
        if kind == "grad":
            loss = parts[0][0]
        for (n, ax), a in zip(TINY, parts[1:1 + len(TINY)]):
            start = [0] * a.ndim
            start[ax] = me * w_loc[n].shape[ax]
            res[(kind, n)] = lax.dynamic_slice(a, start, w_loc[n].shape)
        for n, a in zip(REPL, parts[1 + len(TINY):]):
            res[(kind, n)] = a

    out = [loss, g_x[None]]
    for kind in ("grad", "delta", "m", "v"):
        out += [res[(kind, n)] for n in WEIGHTS]
    return tuple(out)
```

```python
import functools
import math

import jax
import jax.numpy as jnp
from jax import lax
from jax.experimental import pallas as pl
from jax.experimental.pallas import tpu as pltpu

F32 = jnp.float32
BF16 = jnp.bfloat16

N_DEV = 8
LANES = 128
EPS = 1e-6
ROPE_THETA = 10000.0
MLA_HEADS, MLA_NOPE, MLA_ROPE, MLA_V = 8, 128, 64, 128
MLA_Q_RANK, MLA_KV_RANK = 384, 256
GDN_HEADS, GDN_DK, GDN_CONV, GDN_CHUNK = 8, 128, 4, 64
X_HEADS, X_HEAD_DIM = 4, 256
DEPTH, N_MIXERS = 4, 3
ADAM_LR, ADAM_B1, ADAM_B2, ADAM_EPS, ADAM_WD, ADAM_STEP = 0.001, 0.9, 0.999, 1e-08, 0.01, 10
MLA_QUERY_GROUPS = 4
NEG_BIG = -1e30
PACK_W = 1024


_NN = (((1,), (0,)), ((), ()))
_NT = (((1,), (1,)), ((), ()))
_TN = (((0,), (0,)), ((), ()))
_NN3 = (((2,), (1,)), ((0,), (0,)))
_NT3 = (((2,), (2,)), ((0,), (0,)))
_TN3 = (((1,), (1,)), ((0,), (0,)))


def _dot(a, b, dims):
    return lax.dot_general(a, b, dims, preferred_element_type=F32)


def _hi_lo(x):
    hi = x.astype(BF16)
    return hi, (x - hi.astype(F32)).astype(BF16)


def _split3(x):
    hi = x.astype(BF16)
    r = x - hi.astype(F32)
    mid = r.astype(BF16)
    return hi, mid, (r - mid.astype(F32)).astype(BF16)


def _dg(a, b, dims, prec):
    if prec == "h":
        return lax.dot_general(a, b, dims, precision=lax.Precision.HIGHEST, preferred_element_type=F32)
    if prec == "m":
        a_hi, a_lo = _hi_lo(a)
        b_hi, b_lo = _hi_lo(b)
        return _dot(a_hi, b_hi, dims) + _dot(a_hi, b_lo, dims) + _dot(a_lo, b_hi, dims)
    return _dot(a.astype(BF16), b.astype(BF16), dims)


def _dg_sel(sel, x, dims, sel_first):
    s16 = sel.astype(BF16)
    parts = [(_dot(s16, piece, dims) if sel_first else _dot(piece, s16, dims)) for piece in _split3(x)]
    return parts[0] + parts[1] + parts[2]


class _Ops:
    def __init__(self, prec, differentiable, batched=False):
        d_nn, d_nt, d_tn = (_NN3, _NT3, _TN3) if batched else (_NN, _NT, _TN)

        def nn(a, b):
            return _dg(a, b, d_nn, prec)

        def nt(a, b):
            return _dg(a, b, d_nt, prec)

        def tn(a, b):
            return _dg(a, b, d_tn, prec)

        if differentiable:
            dnn = jax.custom_vjp(nn)
            dnn.defvjp(lambda a, b: (nn(a, b), (a, b)), lambda r, g: (nt(g, r[1]), tn(r[0], g)))
            dnt = jax.custom_vjp(nt)
            dnt.defvjp(lambda a, b: (nt(a, b), (a, b)), lambda r, g: (nn(g, r[1]), tn(g, r[0])))
            dtn = jax.custom_vjp(tn)
            dtn.defvjp(lambda a, b: (tn(a, b), (a, b)), lambda r, g: (nt(r[1], g), nn(r[0], g)))
            nn, nt, tn = dnn, dnt, dtn
        self.nn, self.nt, self.tn = nn, nt, tn


class _SelOps:
    def __init__(self, differentiable, batched=False):
        d_nn, d_nt, d_tn = (_NN3, _NT3, _TN3) if batched else (_NN, _NT, _TN)

        def sel_nn(sel, x):
            return _dg_sel(sel, x, d_nn, True)

        def sel_nt(sel, x):
            return _dg_sel(sel, x, d_nt, True)

        if differentiable:
            dnn = jax.custom_vjp(sel_nn)
            dnn.defvjp(lambda s, x: (sel_nn(s, x), s),
                       lambda s, g: (jnp.zeros_like(s), _dg_sel(s, g, d_tn, True)))
            dnt = jax.custom_vjp(sel_nt)
            dnt.defvjp(lambda s, x: (sel_nt(s, x), s),
                       lambda s, g: (jnp.zeros_like(s), _dg_sel(s, g, d_tn, False)))
            sel_nn, sel_nt = dnn, dnt
        self.sel_nn, self.sel_nt = sel_nn, sel_nt


class _OpSet:
    def __init__(self, differentiable):
        self.b = _Ops("b", differentiable)
        self.h = _Ops("h", differentiable)
        self.bb = _Ops("b", differentiable, batched=True)
        self.bm = _Ops("m", differentiable, batched=True)
        self.bs = _SelOps(differentiable, batched=True)


_PLAIN = _OpSet(False)
_DIFF = _OpSet(True)


def _params(sem):
    return pltpu.CompilerParams(dimension_semantics=sem)


BLOCK_BYTES = 4 * 1024 * 1024


def _pick(n, cands):
    for c in cands:
        if n % c == 0:
            return c
    return n


def _tile(n, cap):
    if n <= cap:
        return n
    return _pick(n, tuple(c for c in (2048, 1024, 768, 512, 384, 256, 128) if c <= cap))


def matmul(a, b, form, out_dtype, name, res=None, blocked=False, relu_gate=None, rms_gain=None, relu2_out=False):
    if form == "nn":
        m, k = a.shape
        k2, n = (b.shape[1], N_DEV * b.shape[2]) if blocked else b.shape
    elif form == "nt":
        m, k = a.shape
        n, k2 = (b.shape[1], N_DEV * b.shape[2]) if blocked else b.shape
    else:
        (k, m), (k2, n) = a.shape, b.shape
    assert k == k2, (a.shape, b.shape, form)
    tk = k if k <= 2048 else _tile(k, 1024)
    cb = nb = 1
    if blocked:
        cb = (k if form == "nt" else n) // N_DEV
        nb = _pick(N_DEV, tuple(c for c in (8, 4, 2, 1) if c * cb <= 1024))
    if blocked and form == "nt":
        tk = nb * cb
    if blocked and form != "nt":
        tn = nb * cb
    else:
        tn = _tile(n, min(1024, BLOCK_BYTES // (tk * b.dtype.itemsize)))
    out_elems = BLOCK_BYTES // 2 if (out_dtype == BF16 and res is None) else BLOCK_BYTES // 4
    tm = _tile(m, min(BLOCK_BYTES // (tk * a.dtype.itemsize), out_elems // tn))
    nk = k // tk
    dims = {"nn": _NN, "nt": _NT, "tn": _TN}[form]

    a_spec = {"nn": pl.BlockSpec((tm, tk), lambda i, j, kk: (i, kk)),
              "nt": pl.BlockSpec((tm, tk), lambda i, j, kk: (i, kk)),
              "tn": pl.BlockSpec((tk, tm), lambda i, j, kk: (kk, i))}[form]
    if blocked and form == "nn":
        b_spec = pl.BlockSpec((nb, tk, cb), lambda i, j, kk: (j, kk, 0))
    elif blocked and form == "nt":
        b_spec = pl.BlockSpec((nb, tn, cb), lambda i, j, kk: (kk, j, 0))
    else:
        b_spec = {"nn": pl.BlockSpec((tk, tn), lambda i, j, kk: (kk, j)),
                  "nt": pl.BlockSpec((tn, tk), lambda i, j, kk: (j, kk)),
                  "tn": pl.BlockSpec((tk, tn), lambda i, j, kk: (kk, j))}[form]
    c_spec = pl.BlockSpec((tm, tn), lambda i, j, kk: (i, j))
    out_shape = jax.ShapeDtypeStruct((m, n), out_dtype)
    o_spec = c_spec
    blocked_out = blocked and form == "tn"
    if blocked_out:
        out_shape = jax.ShapeDtypeStruct((N_DEV, m, cb), out_dtype)
        o_spec = pl.BlockSpec((nb, tm, cb), lambda i, j, kk: (j, i, 0))
    has_res, has_gate, has_gain = res is not None, relu_gate is not None, rms_gain is not None
    extras = [e for e in (res, relu_gate) if e is not None]
    n_in = 2 + len(extras) + has_gain
    second = has_gain or relu2_out
    assert not (second and (blocked_out or tn != n and has_gain))

    def body(*refs):
        a_ref, b_ref = refs[0], refs[1]
        r_ref = refs[2] if has_res else None
        gate_ref = refs[2 + has_res] if has_gate else None
        gain_ref = refs[n_in - 1] if has_gain else None
        o_ref = refs[n_in]
        a_val = a_ref[...].astype(BF16)
        if blocked and form == "nn":
            part = jnp.concatenate([_dot(a_val, b_ref[t].astype(BF16), dims) for t in range(nb)], axis=-1)
        elif blocked and form == "nt":
            part = _dot(a_val[:, :cb], b_ref[0].astype(BF16), dims)
            for t in range(1, nb):
                part = part + _dot(a_val[:, t * cb:(t + 1) * cb], b_ref[t].astype(BF16), dims)
        else:
            part = _dot(a_val, b_ref[...].astype(BF16), dims)

        def finish(acc):
            if has_res:
                acc = acc + r_ref[...].astype(F32)
            if has_gate:
                acc = acc * (2.0 * jnp.maximum(gate_ref[...].astype(F32), 0.0))
            if blocked_out:
                for t in range(nb):
                    o_ref[t] = acc[:, t * cb:(t + 1) * cb].astype(out_dtype)
            else:
                o_ref[...] = acc.astype(out_dtype)
            if has_gain:
                refs[n_in + 1][...] = _rms(acc, gain_ref[...]).astype(BF16)
            if relu2_out:
                r = jnp.maximum(acc.astype(out_dtype).astype(F32), 0.0)
                refs[n_in + 1][...] = (r * r).astype(BF16)

        if nk == 1:
            finish(part)
        else:
            acc_ref = refs[-1]
            kk = pl.program_id(2)

            @pl.when(kk == 0)
            def _():
                acc_ref[...] = part

            @pl.when(jnp.logical_and(kk > 0, kk < nk - 1))
            def _():
                acc_ref[...] += part

            @pl.when(kk == nk - 1)
            def _():
                finish(acc_ref[...] + part)

    in_specs = [a_spec, b_spec] + [c_spec] * len(extras)
    args = [a, b] + extras
    if has_gain:
        in_specs.append(pl.BlockSpec((1, tn), lambda i, j, kk: (0, j)))
        args.append(rms_gain)
    if second:
        out_shape = [out_shape, jax.ShapeDtypeStruct((m, n), BF16)]
        o_spec = [o_spec, c_spec]
    return pl.pallas_call(
        body, name=name,
        out_shape=out_shape,
        grid=(m // tm, n // tn, nk),
        in_specs=in_specs, out_specs=o_spec,
        scratch_shapes=[pltpu.VMEM((tm, tn), F32)] if nk > 1 else [],
        compiler_params=_params(("parallel", "parallel", "arbitrary")),
    )(*args)


def make_mm(name, out_dtype, with_res=False, blocked=False):
    def bwd_mm(a, w, g):
        da = matmul(g, w, "nt", a.dtype, name + "_da", blocked=blocked)
        dw = matmul(a, g, "tn", w.dtype, name + "_dw", blocked=blocked)
        return da, dw

    if with_res:
        @jax.custom_vjp
        def op(res, a, w):
            return matmul(a, w, "nn", out_dtype, name + "_f", res=res, blocked=blocked)

        def fwd(res, a, w):
            return op(res, a, w), (a, w)

        def bwd(saved, g):
            return (g,) + bwd_mm(*saved, g)
    else:
        @jax.custom_vjp
        def op(a, w):
            return matmul(a, w, "nn", out_dtype, name + "_f", blocked=blocked)

        def fwd(a, w):
            return op(a, w), (a, w)

        def bwd(saved, g):
            return bwd_mm(*saved, g)
    op.defvjp(fwd, bwd)
    return op


def _rms_fan_bwd(x_new, gain, dx, dh, name):
    rows, d = x_new.shape
    outs = [("row", d, F32), ("row", d, BF16)]
    return tile_bwd(fn_fan_rms, name, ["row", "par"], [x_new, gain], [True, True], outs, [dx, dh],
                    rows, min(512, rows), 0)


def make_mm_res_rms(name):
    @jax.custom_vjp
    def op(res, a, w, gain):
        return tuple(matmul(a, w, "nn", F32, name + "_f", res=res, rms_gain=gain))

    def fwd(res, a, w, gain):
        x_new, h = op(res, a, w, gain)
        return (x_new, h), (a, w, x_new, gain)

    def bwd(saved, cts):
        a, w, x_new, gain = saved
        dx, dgain = _rms_fan_bwd(x_new, gain, cts[0], cts[1], name + "_nb")
        da = matmul(dx, w, "nt", a.dtype, name + "_da")
        dw = matmul(a, dx, "tn", w.dtype, name + "_dw")
        return dx, da, dw, dgain

    op.defvjp(fwd, bwd)
    return op


def make_mlp(name, with_norm):
    def run(x, h, w1, w2, gain):
        a, bsq = matmul(h, w1, "nn", BF16, name + "_1_f", blocked=True, relu2_out=True)
        out = matmul(bsq, w2, "nn", F32, name + "_2_f", res=x, rms_gain=gain if with_norm else None)
        return (tuple(out) if with_norm else (out,)), a, bsq

    @jax.custom_vjp
    def op(x, h, w1, w2, gain):
        return run(x, h, w1, w2, gain)[0]

    def fwd(x, h, w1, w2, gain):
        out, a, bsq = run(x, h, w1, w2, gain)
        return out, (h, w1, w2, gain, a, bsq, out[0])

    def bwd(saved, cts):
        h, w1, w2, gain, a, bsq, x_new = saved
        if with_norm:
            dx, dgain = _rms_fan_bwd(x_new, gain, cts[0], cts[1], name + "_nb")
        else:
            dx, dgain = cts[0], None
        da = matmul(dx, w2, "nt", BF16, name + "_2_da", relu_gate=a)
        dw2 = matmul(bsq, dx, "tn", w2.dtype, name + "_2_dw")
        dh = matmul(da, w1, "nt", h.dtype, name + "_1_da", blocked=True)
        dw1 = matmul(h, da, "tn", w1.dtype, name + "_1_dw", blocked=True)
        return dx, dh, dw1, dw2, dgain

    op.defvjp(fwd, bwd)
    return op


def _kind(k):
    if isinstance(k, str):
        return k, None, 1
    return k[0], k[1], (k[2] if len(k) > 2 else 1)


def _tile_spec(kind, shape, tm, heads):
    k, d, ns = _kind(kind)
    if k == "row":
        return pl.BlockSpec((tm, shape[1]), (lambda h, i: (i, 0)) if heads else (lambda i: (i, 0)))
    if k == "par":
        return pl.BlockSpec(tuple(shape), (lambda h, i: (0, 0)) if heads else (lambda i: (0, 0)))
    if k == "rowh":
        return pl.BlockSpec((tm, d * ns), lambda h, i: (i, h))
    if k == "parh":
        return pl.BlockSpec((shape[0], d * ns), lambda h, i: (0, h))
    raise ValueError(kind)


def _tile_grid(rows, tm, heads):
    n_rows = rows // tm
    return ((heads, n_rows) if heads else (n_rows,)), (1 if heads else 0)


def _split_vals(kinds, refs):
    vals, counts = [], []
    for kind, r in zip(kinds, refs):
        _, d, ns = _kind(kind)
        v = r[...].astype(F32)
        vals += [v] if ns == 1 else [v[:, p * d:(p + 1) * d] for p in range(ns)]
        counts.append(ns)
    return vals, counts


def tile_fwd(fn, name, kinds, args, outs, rows, tm, heads, row_base=0):
    grid, row_axis = _tile_grid(rows, tm, heads)
    n_in = len(args)
    out_shapes = [jax.ShapeDtypeStruct((rows, w), dt) for (_, w, dt) in outs]

    def body(*refs):
        vals, _ = _split_vals(kinds, refs[:n_in])
        row0 = row_base + pl.program_id(row_axis) * tm
        res = list(fn(_PLAIN, row0, *vals))
        for o_ref, (k, _, _) in zip(refs[n_in:], outs):
            pieces = [res.pop(0) for _ in range(_kind(k)[2])]
            v = pieces[0] if len(pieces) == 1 else jnp.concatenate(pieces, axis=-1)
            o_ref[...] = v.astype(o_ref.dtype)

    return pl.pallas_call(
        body, name=name, out_shape=out_shapes, grid=grid,
        in_specs=[_tile_spec(k, a.shape, tm, heads) for k, a in zip(kinds, args)],
        out_specs=[_tile_spec(k, (rows, w), tm, heads) for (k, w, _) in outs],
        compiler_params=_params(("arbitrary",) * len(grid)),
    )(*args)


def tile_bwd(fn, name, kinds, args, diff, outs, cts, rows, tm, heads, row_base=0):
    grid, row_axis = _tile_grid(rows, tm, heads)
    n_in, n_ct = len(args), len(cts)
    diff_idx = [i for i, d in enumerate(diff) if d]
    g_shapes, g_specs = [], []
    for i in diff_idx:
        k = _kind(kinds[i])[0]
        dt = args[i].dtype if k in ("row", "rowh") else F32
        g_shapes.append(jax.ShapeDtypeStruct(args[i].shape, dt))
        g_specs.append(_tile_spec(kinds[i], args[i].shape, tm, heads))

    def body(*refs):
        in_refs, ct_refs, g_refs = refs[:n_in], refs[n_in:n_in + n_ct], refs[n_in + n_ct:]
        vals, counts = _split_vals(kinds, in_refs)
        first_piece = [sum(counts[:i]) for i in range(n_in)]
        flat_diff = [first_piece[i] + p for i in diff_idx for p in range(counts[i])]
        row_id = pl.program_id(row_axis)
        row0 = row_base + row_id * tm

        def f(*dvals):
            full = list(vals)
            for i, dv in zip(flat_diff, dvals):
                full[i] = dv
            return tuple(fn(_DIFF, row0, *full))

        _, vjp = jax.vjp(f, *[vals[i] for i in flat_diff])
        ct_vals, _ = _split_vals([k for (k, _, _) in outs], ct_refs)
        flat_grads = list(vjp(tuple(ct_vals)))
        for g_ref, i in zip(g_refs, diff_idx):
            pieces = [flat_grads.pop(0) for _ in range(counts[i])]
            g = pieces[0] if len(pieces) == 1 else jnp.concatenate(pieces, axis=-1)
            k = _kind(kinds[i])[0]
            if k in ("row", "rowh"):
                g_ref[...] = g.astype(g_ref.dtype)
            else:
                first = row_id == 0
                if heads and k == "par":
                    first = jnp.logical_and(first, pl.program_id(0) == 0)

                @pl.when(first)
                def _(g_ref=g_ref, g=g):
                    g_ref[...] = g

                @pl.when(jnp.logical_not(first))
                def _(g_ref=g_ref, g=g):
                    g_ref[...] += g

    return pl.pallas_call(
        body, name=name, out_shape=g_shapes, grid=grid,
        in_specs=[_tile_spec(k, a.shape, tm, heads) for k, a in zip(kinds, args)]
        + [_tile_spec(k, (rows, w), tm, heads) for (k, w, _) in outs],
        out_specs=g_specs,
        compiler_params=_params(("arbitrary",) * len(grid)),
    )(*args, *cts)


def make_tile_op(fn, name, kinds, diff, outs, rows, tm, heads=0, row_base=0):
    tm = min(tm, rows)

    @jax.custom_vjp
    def op(*args):
        return tuple(tile_fwd(fn, name + "_f", kinds, args, outs, rows, tm, heads, row_base))

    def fwd(*args):
        return op(*args), args

    def bwd(args, cts):
        grads = tile_bwd(fn, name + "_b", kinds, args, diff, outs, cts, rows, tm, heads, row_base)
        it = iter(grads)
        res = []
        for a, d in zip(args, diff):
            res.append(next(it).astype(a.dtype) if d else None)
        return tuple(res)

    op.defvjp(fwd, bwd)
    return op


def _rms(x, g):
    return x * lax.rsqrt(jnp.mean(x * x, axis=-1, keepdims=True) + EPS) * g


def fn_rms(ops, row0, x, g):
    return (_rms(x, g),)


def fn_fan_rms(ops, row0, x, g):
    return x, _rms(x, g)


def fn_mul(ops, row0, a, b):
    return (a * b,)


def fn_rope(ops, row0, *t):
    nh = (len(t) - 2) // 2
    c, s = t[-2], t[-1]
    return tuple(x * c + xs * s for x, xs in zip(t[:nh], t[nh:2 * nh]))


def _softmax(s):
    m = lax.stop_gradient(jnp.max(s, axis=-1, keepdims=True))
    e = jnp.exp(s - m)
    return e / jnp.sum(e, axis=-1, keepdims=True)


def fn_xattn(ops, row0, q, k, v):
    s = ops.b.nt(q, k) * (X_HEAD_DIM ** -0.5)
    return (ops.b.nn(_softmax(s), v),)


def fn_mla_attn(ops, row0, qn, qr, kn, v, kr):
    s = (ops.b.nt(qn, kn) + ops.b.nt(qr, kr)) * ((MLA_NOPE + MLA_ROPE) ** -0.5)
    rows = row0 + lax.broadcasted_iota(jnp.int32, s.shape, 0)
    cols = lax.broadcasted_iota(jnp.int32, s.shape, 1)
    s = jnp.where(rows >= cols, s, NEG_BIG)
    return (ops.b.nn(_softmax(s), v),)


def _silu(x):
    return x * jax.nn.sigmoid(x)


def fn_gdn_prep(ops, row0, *t):
    nh = len(t) // 3
    qs, ks, vs = [], [], []
    for qc, kc, vc in zip(t[:nh], t[nh:2 * nh], t[2 * nh:]):
        q, k = _silu(qc), _silu(kc)
        qs.append(q * lax.rsqrt(jnp.sum(q * q, -1, keepdims=True) + EPS) * (GDN_DK ** -0.5))
        ks.append(k * lax.rsqrt(jnp.sum(k * k, -1, keepdims=True) + EPS))
        vs.append(_silu(vc))
    return tuple(qs + ks + vs)


def fn_gdn_gates(ops, row0, ba, alog, dtb):
    width = GDN_HEADS * GDN_DK
    beta = jax.nn.sigmoid(ba)
    z = ba + dtb
    softplus = jnp.maximum(z, 0.0) + jnp.log1p(jnp.exp(-jnp.abs(z)))
    g = -jnp.exp(alog) * softplus
    r = lax.broadcasted_iota(jnp.int32, (LANES, width), 0)
    c = lax.broadcasted_iota(jnp.int32, (LANES, width), 1) // GDN_DK
    e_beta = (r == c).astype(F32)
    e_g = (r == c + GDN_HEADS).astype(F32)
    return ops.h.nn(beta, e_beta), ops.h.nn(g, e_g)


def fn_gdn_out(ops, row0, *t):
    nh = (len(t) - 1) // 2
    g = t[-1]
    return tuple(_rms(o, g) * _silu(gate) for o, gate in zip(t[:nh], t[nh:2 * nh]))


def _shift_down(x, d, t_idx):
    if d == 0:
        return x
    return jnp.where(t_idx >= d, pltpu.roll(x, d, axis=0), 0.0)


def _shift_up(x, d, t_idx):
    if d == 0:
        return x
    n = x.shape[0]
    return jnp.where(t_idx < n - d, pltpu.roll(x, n - d, axis=0), 0.0)


def conv_fwd(x, w, name):
    s, c = x.shape
    kw = w.shape[0]
    tc = _pick(c, (256, 128))

    def body(x_ref, w_ref, y_ref):
        xv = x_ref[...]
        t_idx = lax.broadcasted_iota(jnp.int32, xv.shape, 0)
        acc = jnp.zeros_like(xv)
        for j in range(kw):
            acc = acc + w_ref[j:j + 1, :] * _shift_down(xv, kw - 1 - j, t_idx)
        y_ref[...] = acc

    return pl.pallas_call(
        body, name=name, out_shape=jax.ShapeDtypeStruct((s, c), F32), grid=(c // tc,),
        in_specs=[pl.BlockSpec((s, tc), lambda i: (0, i)), pl.BlockSpec((kw, tc), lambda i: (0, i))],
        out_specs=pl.BlockSpec((s, tc), lambda i: (0, i)),
        compiler_params=_params(("parallel",)),
    )(x, w)


def conv_bwd(x, w, dy, name):
    s, c = x.shape
    kw = w.shape[0]
    tc = _pick(c, (256, 128))

    def body(x_ref, w_ref, dy_ref, dx_ref, dw_ref):
        xv, dyv = x_ref[...], dy_ref[...]
        t_idx = lax.broadcasted_iota(jnp.int32, xv.shape, 0)
        dx = jnp.zeros_like(xv)
        for j in range(kw):
            d = kw - 1 - j
            dx = dx + w_ref[j:j + 1, :] * _shift_up(dyv, d, t_idx)
            dw_ref[j:j + 1, :] = jnp.sum(dyv * _shift_down(xv, d, t_idx), axis=0, keepdims=True)
        dx_ref[...] = dx

    return pl.pallas_call(
        body, name=name,
        out_shape=[jax.ShapeDtypeStruct((s, c), F32), jax.ShapeDtypeStruct((kw, c), F32)],
        grid=(c // tc,),
        in_specs=[pl.BlockSpec((s, tc), lambda i: (0, i)), pl.BlockSpec((kw, tc), lambda i: (0, i)),
                  pl.BlockSpec((s, tc), lambda i: (0, i))],
        out_specs=[pl.BlockSpec((s, tc), lambda i: (0, i)), pl.BlockSpec((kw, tc), lambda i: (0, i))],
        compiler_params=_params(("parallel",)),
    )(x, w, dy)


def make_conv(name):
    @jax.custom_vjp
    def op(x, w):
        return conv_fwd(x, w, name + "_f")

    def fwd(x, w):
        return op(x, w), (x, w)

    def bwd(saved, dy):
        dx, dw = conv_bwd(saved[0], saved[1], dy, name + "_b")
        return dx, dw

    op.defvjp(fwd, bwd)
    return op


def _gdn_consts():
    c, d = GDN_CHUNK, GDN_DK
    i = lax.broadcasted_iota(jnp.int32, (c, c), 0)
    j = lax.broadcasted_iota(jnp.int32, (c, c), 1)
    tri = i >= j
    return dict(
        tri=tri, strict=i > j,
        tri_f=tri.astype(F32),
        eye=(i == j).astype(F32),
        lane0=(lax.broadcasted_iota(jnp.int32, (c, d), 1) == 0).astype(F32),
        last_row=(lax.broadcasted_iota(jnp.int32, (c, d), 0) == c - 1).astype(F32),
    )


def _gdn_chunk(ops, q, k, v, g, beta, state):
    b, m, sel = ops.bb, ops.bm, ops.bs
    nh, c, d = q.shape[0], GDN_CHUNK, GDN_DK
    k_ = _gdn_consts()

    def per_head(a):
        return jnp.broadcast_to(a, (nh,) + a.shape)

    gc = sel.sel_nn(per_head(k_["tri_f"]), g)
    col = jnp.broadcast_to(jnp.sum(gc * k_["lane0"], axis=2, keepdims=True), (nh, c, c))
    row = sel.sel_nt(per_head(k_["lane0"]), gc)
    decay = jnp.where(k_["tri"], jnp.exp(jnp.where(k_["tri"], col - row, 0.0)), 0.0)
    kb = k * beta
    mm_ = jnp.where(k_["strict"], b.nt(kb, k) * decay, 0.0)
    p = -mm_
    t = k_["eye"] + p
    for _ in range(int(math.log2(GDN_CHUNK)) - 1):
        p = m.nn(p, p)
        t = t + m.nn(t, p)
    egc = jnp.exp(gc)
    u = b.nn(t, v * beta)
    w = b.nn(t, kb * egc)
    attn = b.nt(q, k) * decay
    v_new = u - b.nn(w, state)
    o = b.nn(q * egc, state) + b.nn(attn, v_new)
    g_last = jnp.sum(gc * k_["last_row"], axis=1, keepdims=True)
    new_state = (state * jnp.exp(jnp.broadcast_to(g_last, (nh, d, d)))
                 + b.tn(k * jnp.exp(jnp.broadcast_to(g_last, (nh, c, d)) - gc), v_new))
    return o, new_state


GDN_HEAD_GROUP = 8
GDN_TILE_CHUNKS = 4


def _heads_of(ref, rows, n_heads):
    d = GDN_DK
    return jnp.stack([ref[rows, h * d:(h + 1) * d] for h in range(n_heads)])


def _gdn_specs(s, reverse):
    d, hg = GDN_DK, GDN_HEAD_GROUP
    tile = min(GDN_TILE_CHUNKS * GDN_CHUNK, s)
    n_tiles = s // tile
    t_of = (lambda t: n_tiles - 1 - t) if reverse else (lambda t: t)
    seq = pl.BlockSpec((tile, hg * d), lambda grp, t: (t_of(t), grp))
    st = pl.BlockSpec((hg, tile // GDN_CHUNK, d, d), lambda grp, t: (grp, t_of(t), 0, 0))
    return seq, st, tile, n_tiles


def gdn_fwd(q, k, v, g, beta, name):
    s = q.shape[0]
    d, hg = GDN_DK, GDN_HEAD_GROUP
    seq, st, tile, n_tiles = _gdn_specs(s, False)

    def body(q_ref, k_ref, v_ref, g_ref, b_ref, o_ref, st_ref, state_scr):
        @pl.when(pl.program_id(1) == 0)
        def _():
            state_scr[...] = jnp.zeros_like(state_scr)

        def step(ci, carry):
            rows = pl.ds(pl.multiple_of(ci * GDN_CHUNK, GDN_CHUNK), GDN_CHUNK)
            state = state_scr[...]
            for h in range(hg):
                st_ref[h, ci] = state[h]
            o, new_state = _gdn_chunk(_PLAIN, *[_heads_of(r, rows, hg) for r in (q_ref, k_ref, v_ref, g_ref, b_ref)],
                                      state)
            for h in range(hg):
                o_ref[rows, h * d:(h + 1) * d] = o[h]
            state_scr[...] = new_state
            return carry

        lax.fori_loop(0, tile // GDN_CHUNK, step, 0)

    return pl.pallas_call(
        body, name=name,
        out_shape=[jax.ShapeDtypeStruct(q.shape, F32),
                   jax.ShapeDtypeStruct((GDN_HEADS, s // GDN_CHUNK, d, d), F32)],
        grid=(GDN_HEADS // hg, n_tiles), in_specs=[seq] * 5, out_specs=[seq, st],
        scratch_shapes=[pltpu.VMEM((hg, d, d), F32)],
        compiler_params=_params(("parallel", "arbitrary")),
    )(q, k, v, g, beta)


def gdn_bwd(q, k, v, g, beta, states, do, name):
    s = q.shape[0]
    d, hg = GDN_DK, GDN_HEAD_GROUP
    seq, st, tile, n_tiles = _gdn_specs(s, True)
    tile_chunks = tile // GDN_CHUNK

    def body(q_ref, k_ref, v_ref, g_ref, b_ref, st_ref, do_ref, dq_ref, dk_ref, dv_ref, dg_ref, db_ref, dstate_scr):
        @pl.when(pl.program_id(1) == 0)
        def _():
            dstate_scr[...] = jnp.zeros_like(dstate_scr)

        def step(it, carry):
            ci = tile_chunks - 1 - it
            rows = pl.ds(pl.multiple_of(ci * GDN_CHUNK, GDN_CHUNK), GDN_CHUNK)
            prim = [_heads_of(r, rows, hg) for r in (q_ref, k_ref, v_ref, g_ref, b_ref)]
            prim.append(jnp.stack([st_ref[h, ci] for h in range(hg)]))
            _, vjp = jax.vjp(functools.partial(_gdn_chunk, _DIFF), *prim)
            grads = vjp((_heads_of(do_ref, rows, hg), dstate_scr[...]))
            for g_ref_out, gr in zip((dq_ref, dk_ref, dv_ref, dg_ref, db_ref), grads[:5]):
                for h in range(hg):
                    g_ref_out[rows, h * d:(h + 1) * d] = gr[h]
            dstate_scr[...] = grads[5]
            return carry

        lax.fori_loop(0, tile_chunks, step, 0)

    return pl.pallas_call(
        body, name=name,
        out_shape=[jax.ShapeDtypeStruct(q.shape, F32)] * 5,
        grid=(GDN_HEADS // hg, n_tiles), in_specs=[seq] * 5 + [st, seq], out_specs=[seq] * 5,
        scratch_shapes=[pltpu.VMEM((hg, d, d), F32)],
        compiler_params=_params(("parallel", "arbitrary")),
    )(q, k, v, g, beta, states, do)


def make_gdn(name):
    @jax.custom_vjp
    def op(q, k, v, g, beta):
        return gdn_fwd(q, k, v, g, beta, name + "_f")[0]

    def fwd(q, k, v, g, beta):
        o, states = gdn_fwd(q, k, v, g, beta, name + "_f")
        return o, (q, k, v, g, beta, states)

    def bwd(saved, do):
        return tuple(gdn_bwd(*saved, do, name + "_b"))

    op.defvjp(fwd, bwd)
    return op


def loss_head(x, g, target, name):
    s, d = x.shape
    tm = min(256, s)

    def body(x_ref, g_ref, t_ref, loss_ref, dx_ref, dg_ref):
        tgt = t_ref[...]

        def f(xv, gv):
            err = _rms(xv, gv) - tgt
            per_row = jnp.mean(err * err, axis=-1, keepdims=True)
            return 0.5 * jnp.sum(per_row, axis=0, keepdims=True)

        val, vjp = jax.vjp(f, x_ref[...], g_ref[...])
        dx, dg = vjp(jnp.ones((1, 1), F32))
        dx_ref[...] = dx
        first = pl.program_id(0) == 0

        @pl.when(first)
        def _():
            dg_ref[...] = dg
            loss_ref[...] = jnp.broadcast_to(val, loss_ref.shape)

        @pl.when(jnp.logical_not(first))
        def _():
            dg_ref[...] += dg
            loss_ref[...] += jnp.broadcast_to(val, loss_ref.shape)

    row = pl.BlockSpec((tm, d), lambda i: (i, 0))
    vec = pl.BlockSpec((1, d), lambda i: (0, 0))
    return pl.pallas_call(
        body, name=name,
        out_shape=[jax.ShapeDtypeStruct((1, LANES), F32), jax.ShapeDtypeStruct((s, d), F32),
                   jax.ShapeDtypeStruct((1, d), F32)],
        grid=(s // tm,), in_specs=[row, vec, row],
        out_specs=[pl.BlockSpec((1, LANES), lambda i: (0, 0)), row, vec],
        compiler_params=_params(("arbitrary",)),
    )(x, g, target)


def adamw(g8, w, m, v, layer, prev, name):
    n_layers, rows, width = w.shape
    tr = _pick(rows, (256, 128, 64, 32, 16, 8))

    def body(g_ref, w_ref, m_ref, v_ref, *rest):
        go_ref, d_ref, mo_ref, vo_ref = rest[-4:]
        g = g_ref[0].astype(F32)
        for p in range(1, N_DEV):
            g = g + g_ref[p].astype(F32)
        m_new = ADAM_B1 * m_ref[...] + (1.0 - ADAM_B1) * g
        v_new = ADAM_B2 * v_ref[...] + (1.0 - ADAM_B2) * (g * g)
        m_hat = m_new / (1.0 - ADAM_B1 ** ADAM_STEP)
        v_hat = v_new / (1.0 - ADAM_B2 ** ADAM_STEP)
        go_ref[...] = g
        d_ref[...] = -ADAM_LR * (m_hat / (jnp.sqrt(v_hat) + ADAM_EPS) + ADAM_WD * w_ref[...])
        mo_ref[...] = m_new
        vo_ref[...] = v_new

    blk = pl.BlockSpec((None, tr, width), lambda i: (layer, i, 0))
    carried = list(prev) if prev is not None else []
    return pl.pallas_call(
        body, name=name, out_shape=[jax.ShapeDtypeStruct((n_layers, rows, width), F32)] * 4,
        grid=(rows // tr,),
        in_specs=[pl.BlockSpec((N_DEV, tr, width), lambda i: (0, i, 0)), blk, blk, blk]
        + [pl.BlockSpec(memory_space=pl.ANY)] * len(carried),
        out_specs=[blk] * 4,
        input_output_aliases={4 + j: j for j in range(len(carried))},
        compiler_params=_params(("parallel",)),
    )(g8, w, m, v, *carried)


_HBM = pl.BlockSpec(memory_space=pltpu.HBM)
_SEM = pl.BlockSpec(memory_space=pltpu.SEMAPHORE)
_EFFECT = pltpu.SideEffectType.DATAFLOW_SIDE_EFFECTING


def _exchange_copies(mode, src_refs, land_refs, send_sems, recv_sems, local_sems):
    x, y, c = lax.axis_index("x"), lax.axis_index("y"), lax.axis_index("c")
    me = 4 * x + 2 * y + c
    n = len(src_refs)

    def src(k, p):
        return src_refs[k] if mode == "gather" else src_refs[k].at[p]

    local = [pltpu.make_async_copy(src(k, me), land_refs[k].at[me], local_sems.at[k]) for k in range(n)]
    sends, recvs = [], []
    for k in range(n):
        for r in range(1, N_DEV):
            px = (1 - x) if r & 4 else x
            py = (1 - y) if r & 2 else y
            pc = (1 - c) if r & 1 else c
            p = 4 * px + 2 * py + pc
            sem = k * (N_DEV - 1) + r - 1
            sends.append(pltpu.make_async_remote_copy(
                src_ref=src(k, p), dst_ref=land_refs[k].at[me],
                send_sem=send_sems.at[sem], recv_sem=recv_sems.at[sem],
                device_id=(px, py, pc), device_id_type=pl.DeviceIdType.MESH))
            recvs.append(pltpu.make_async_remote_copy(
                src_ref=src(k, p), dst_ref=land_refs[k].at[p],
                send_sem=send_sems.at[sem], recv_sem=recv_sems.at[sem],
                device_id=(px, py, pc), device_id_type=pl.DeviceIdType.MESH))
    return local, sends, recvs


def exchange_start(mode, arrays, name, after=None, carry=()):
    n, nc = len(arrays), len(carry)
    extra = [] if after is None else [after]
    land_shapes = [((N_DEV,) + tuple(a.shape)) if mode == "gather" else tuple(a.shape) for a in arrays]
    lands = [pltpu.with_memory_space_constraint(lax.empty(shp, a.dtype), pltpu.HBM)
             for shp, a in zip(land_shapes, arrays)]
    srcs = [pltpu.with_memory_space_constraint(a, pltpu.HBM) for a in arrays]
    carried = [pltpu.with_memory_space_constraint(a, pltpu.HBM) for a in carry]

    def body(*refs):
        src_refs, land_refs = refs[:n], refs[n:2 * n]
        first_out = 2 * n + nc + len(extra)
        send_sems, recv_sems, local_sems = refs[first_out:first_out + 3]
        token = refs[-1]
        local, sends, _ = _exchange_copies(mode, src_refs, land_refs, send_sems, recv_sems, local_sems)
        for cp in local + sends:
            cp.start()
        token[...] = jnp.zeros_like(token)

    n_sem = n * (N_DEV - 1)
    out = pl.pallas_call(
        body, name=name,
        out_shape=(pltpu.SemaphoreType.DMA((n_sem,)), pltpu.SemaphoreType.DMA((n_sem,)),
                   pltpu.SemaphoreType.DMA((n,)),
                   *[pltpu.HBM(a.shape, a.dtype) for a in arrays],
                   *[pltpu.HBM(shp, a.dtype) for shp, a in zip(land_shapes, arrays)],
                   *[pltpu.HBM(a.shape, a.dtype) for a in carry],
                   jax.ShapeDtypeStruct((8, LANES), F32)),
        in_specs=[_HBM] * (2 * n + nc) + [pl.BlockSpec(memory_space=pl.ANY)] * len(extra),
        out_specs=(_SEM, _SEM, _SEM, *[_HBM] * (2 * n + nc), pl.BlockSpec(memory_space=pltpu.VMEM)),
        input_output_aliases={i: 3 + i for i in range(2 * n + nc)},
        compiler_params=pltpu.CompilerParams(has_side_effects=_EFFECT),
    )(*srcs, *lands, *carried, *extra)
    handle = dict(mode=mode, sems=out[:3], srcs=out[3:3 + n], lands=out[3 + n:3 + 2 * n])
    return handle, out[-1], list(out[3 + 2 * n:3 + 2 * n + nc])


def exchange_wait(handle, after, name):
    mode, srcs, lands = handle["mode"], list(handle["srcs"]), list(handle["lands"])
    n = len(srcs)

    def body(*refs):
        src_refs, land_refs = refs[:n], refs[n:2 * n]
        send_sems, recv_sems, local_sems = refs[2 * n:2 * n + 3]
        local, sends, recvs = _exchange_copies(mode, src_refs, land_refs, send_sems, recv_sems, local_sems)
        for cp in sends:
            cp.wait_send()
        for cp in recvs:
            cp.wait_recv()
        for cp in local:
            cp.wait()

    out = pl.pallas_call(
        body, name=name,
        out_shape=(*[pltpu.HBM(a.shape, a.dtype) for a in srcs], *[pltpu.HBM(a.shape, a.dtype) for a in lands]),
        in_specs=[_HBM] * (2 * n) + [_SEM] * 3 + [pl.BlockSpec(memory_space=pl.ANY)],
        out_specs=tuple([_HBM] * (2 * n)),
        input_output_aliases={i: i for i in range(2 * n)},
        compiler_params=pltpu.CompilerParams(has_side_effects=_EFFECT),
    )(*srcs, *lands, *handle["sems"], after)
    return list(out[n:])


_ICI_RELATIONS = (2, 4, 6)


def _mesh_place():
    x, y, c = lax.axis_index("x"), lax.axis_index("y"), lax.axis_index("c")

    def peer(r):
        px = (1 - x) if r & 4 else x
        py = (1 - y) if r & 2 else y
        pc = (1 - c) if r & 1 else c
        return (px, py, pc), 4 * px + 2 * py + pc

    return 4 * x + 2 * y + c, peer


def _remote(src, dst, send_sem, recv_sem, device):
    return pltpu.make_async_remote_copy(src_ref=src, dst_ref=dst, send_sem=send_sem, recv_sem=recv_sem,
                                        device_id=device, device_id_type=pl.DeviceIdType.MESH)


def gather2_start(groups, name):
    flat = [a for g in groups for a in g]
    n = len(flat)
    lands = [pltpu.with_memory_space_constraint(lax.empty((N_DEV,) + tuple(a.shape), a.dtype), pltpu.HBM) for a in flat]
    srcs = [pltpu.with_memory_space_constraint(a, pltpu.HBM) for a in flat]
    n_rel = 1 + len(_ICI_RELATIONS)

    def body(*refs):
        src_refs, land_refs = refs[:n], refs[n:2 * n]
        sem_refs = refs[2 * n:2 * n + 4 * len(groups)]
        me, peer = _mesh_place()
        k = 0
        for gi, g in enumerate(groups):
            send_sems, recv_sib, recv_ici, local_sems = sem_refs[4 * gi:4 * gi + 4]
            for j in range(len(g)):
                pltpu.make_async_copy(src_refs[k], land_refs[k].at[me], local_sems.at[j]).start()
                dev, _ = peer(1)
                _remote(src_refs[k], land_refs[k].at[me], send_sems.at[n_rel * j], recv_sib.at[j], dev).start()
                for t, r in enumerate(_ICI_RELATIONS):
                    dev, _ = peer(r)
                    _remote(src_refs[k], land_refs[k].at[me], send_sems.at[n_rel * j + 1 + t],
                            recv_ici.at[len(_ICI_RELATIONS) * j + t], dev).start()
                k += 1
        refs[-1][...] = jnp.zeros_like(refs[-1])

    sem_shapes = []
    for g in groups:
        sem_shapes += [pltpu.SemaphoreType.DMA((n_rel * len(g),)), pltpu.SemaphoreType.DMA((len(g),)),
                       pltpu.SemaphoreType.DMA((len(_ICI_RELATIONS) * len(g),)), pltpu.SemaphoreType.DMA((len(g),))]
    out = pl.pallas_call(
        body, name=name,
        out_shape=(*sem_shapes, *[pltpu.HBM(a.shape, a.dtype) for a in flat],
                   *[pltpu.HBM((N_DEV,) + tuple(a.shape), a.dtype) for a in flat],
                   jax.ShapeDtypeStruct((8, LANES), F32)),
        in_specs=[_HBM] * (2 * n),
        out_specs=(*[_SEM] * len(sem_shapes), *[_HBM] * (2 * n), pl.BlockSpec(memory_space=pltpu.VMEM)),
        input_output_aliases={i: len(sem_shapes) + i for i in range(2 * n)},
        compiler_params=pltpu.CompilerParams(has_side_effects=_EFFECT),
    )(*srcs, *lands)
    handles, k, base = [], 0, len(sem_shapes)
    for gi, g in enumerate(groups):
        handles.append(dict(sems=out[4 * gi:4 * gi + 4], srcs=out[base + k:base + k + len(g)],
                            lands=out[base + n + k:base + n + k + len(g)]))
        k += len(g)
    return handles, out[-1]


def gather2_forward(handle, after, name, carry=()):
    lands, nc = list(handle["lands"]), len(carry)
    n, n_ici = len(lands), len(_ICI_RELATIONS)
    carried = [pltpu.with_memory_space_constraint(a, pltpu.HBM) for a in carry]

    def body(*refs):
        land_refs = refs[:n]
        recv_ici = refs[n + nc]
        fwd_send, fwd_recv = refs[n + nc + 2], refs[n + nc + 3]
        me, peer = _mesh_place()
        sibling, _ = peer(1)
        for j in range(n):
            for t, r in enumerate(_ICI_RELATIONS):
                dev, p = peer(r)
                landed = land_refs[j].at[p]
                _remote(landed, landed, fwd_send.at[n_ici * j + t], recv_ici.at[n_ici * j + t], dev).wait_recv()
                _remote(landed, landed, fwd_send.at[n_ici * j + t], fwd_recv.at[n_ici * j + t], sibling).start()

    out = pl.pallas_call(
        body, name=name,
        out_shape=(pltpu.SemaphoreType.DMA((n_ici * n,)), pltpu.SemaphoreType.DMA((n_ici * n,)),
                   *[pltpu.HBM(a.shape, a.dtype) for a in lands], *[pltpu.HBM(a.shape, a.dtype) for a in carry]),
        in_specs=[_HBM] * (n + nc) + [_SEM, pl.BlockSpec(memory_space=pl.ANY)],
        out_specs=(_SEM, _SEM, *[_HBM] * (n + nc)),
        input_output_aliases={i: 2 + i for i in range(n + nc)},
        compiler_params=pltpu.CompilerParams(has_side_effects=_EFFECT),
    )(*lands, *carried, handle["sems"][2], after)
    new_handle = dict(sems=handle["sems"], srcs=handle["srcs"], lands=out[2:2 + n], fwd=out[:2])
    return new_handle, list(out[2 + n:])


def gather2_wait(handle, after, name):
    srcs, lands = list(handle["srcs"]), list(handle["lands"])
    n, n_ici = len(srcs), len(_ICI_RELATIONS)
    n_rel = 1 + n_ici
    send_all, recv_sibling, _, local_all = handle["sems"]

    def body(*refs):
        src_refs, land_refs = refs[:n], refs[n:2 * n]
        send_sems, recv_sib, local_sems, fwd_send, fwd_recv = refs[2 * n:2 * n + 5]
        me, peer = _mesh_place()
        sibling, sib = peer(1)
        for j in range(n):
            pltpu.make_async_copy(src_refs[j], land_refs[j].at[me], local_sems.at[j]).wait()
            _remote(src_refs[j], land_refs[j].at[sib], send_sems.at[n_rel * j], recv_sib.at[j], sibling).wait()
            for t, r in enumerate(_ICI_RELATIONS):
                dev, p = peer(r)
                _remote(src_refs[j], land_refs[j].at[me], send_sems.at[n_rel * j + 1 + t],
                        recv_sib.at[j], dev).wait_send()
                _, p_sib = peer(r ^ 1)
                _remote(land_refs[j].at[p], land_refs[j].at[p_sib], fwd_send.at[n_ici * j + t],
                        fwd_recv.at[n_ici * j + t], sibling).wait()

    out = pl.pallas_call(
        body, name=name,
        out_shape=(*[pltpu.HBM(a.shape, a.dtype) for a in srcs], *[pltpu.HBM(a.shape, a.dtype) for a in lands]),
        in_specs=[_HBM] * (2 * n) + [_SEM] * 5 + [pl.BlockSpec(memory_space=pl.ANY)],
        out_specs=tuple([_HBM] * (2 * n)),
        input_output_aliases={i: i for i in range(2 * n)},
        compiler_params=pltpu.CompilerParams(has_side_effects=_EFFECT),
    )(*srcs, *lands, send_all, recv_sibling, local_all, *handle["fwd"], after)
    return list(out[n:])


BIG = ["mla_w_in", "mla_w_uq", "mla_w_ukv", "mla_w_o", "gdn_w_in", "gdn_w_o", "sc_w_in", "sc_w_o",
       "xa_w_q", "xa_w_kv", "xa_w_o", "mlp_w1", "mlp_w2"]
TINY = [("mla_q_norm", 1), ("mla_kv_norm", 1), ("gdn_conv_w", 2), ("sc_conv_w", 2)]
REPL = ["gdn_a_log", "gdn_dt_bias", "gdn_o_norm", "norm_mix", "norm_mem", "norm_mlp", "mem_norm", "final_norm"]
WEIGHTS = ["mla_w_in", "mla_q_norm", "mla_kv_norm", "mla_w_uq", "mla_w_ukv", "mla_w_o", "gdn_w_in",
           "gdn_conv_w", "gdn_a_log", "gdn_dt_bias", "gdn_o_norm", "gdn_w_o", "sc_w_in", "sc_conv_w",
           "sc_w_o", "norm_mix", "norm_mem", "norm_mlp", "xa_w_q", "xa_w_kv", "xa_w_o", "mlp_w1",
           "mlp_w2", "mem_norm", "final_norm"]
MIXER_WEIGHTS = (["mla_w_in", "mla_w_uq", "mla_w_ukv", "mla_w_o"], ["gdn_w_in", "gdn_w_o"], ["sc_w_in", "sc_w_o"])
MIXER_PARAMS = (["norm_mem", "mla_q_norm", "mla_kv_norm"],
                ["norm_mem", "gdn_conv_w", "gdn_a_log", "gdn_dt_bias", "gdn_o_norm"],
                ["norm_mem", "sc_conv_w"])


def from_shards(a8, axis):
    a = jnp.moveaxis(a8, 0, axis)
    shp = a.shape
    return a.reshape(shp[:axis] + (shp[axis] * shp[axis + 1],) + shp[axis + 2:])


def pack_rows(flat_list, width, row_mult):
    total = sum(a.shape[-1] for a in flat_list)
    rows = -(-total // width)
    rows = -(-rows // row_mult) * row_mult
    pad = rows * width - total
    parts = list(flat_list)
    if pad:
        parts.append(jnp.zeros((pad,), flat_list[0].dtype))
    return jnp.concatenate(parts, axis=-1).reshape(rows, width)


def unpack_rows(packed, shapes):
    lead = packed.shape[:-2]
    flat = packed.reshape(lead + (-1,))
    out, off = [], 0
    for shp in shapes:
        n = math.prod(shp)
        out.append(flat[..., off:off + n].reshape(lead + tuple(shp)))
        off += n
    return out


def _swap_halves(w):
    half = w.shape[-1] // 2
    return jnp.concatenate([w[..., half:], w[..., :half]], axis=-1)


def _pad_last(w, n):
    return jnp.pad(w, [(0, 0)] * (w.ndim - 1) + [(0, n - w.shape[-1])])


def _unblock(w8):
    return jnp.transpose(w8, (1, 0, 2)).reshape(w8.shape[1], -1)


def _stack_rows(w8):
    return w8.reshape(-1, w8.shape[-1])


def rms_op(name, rows, d, out_dtype, tm=512):
    return make_tile_op(fn_rms, name, ["row", "par"], [True, True], [("row", d, out_dtype)], rows, min(tm, rows))


def seg_memory(p, mem):
    return rms_op("rms_memory", mem.shape[0], mem.shape[1], BF16)(mem, p["mem_norm"].reshape(1, -1))[0]


def seg_mixer(i, wts, p, x, h, rope_c, rope_s):
    s, d = x.shape
    j, kind = i // N_MIXERS, i % N_MIXERS
    tag = f"l{i}"
    hd = MLA_NOPE
    next_gain = p["norm_mem"][i].reshape(1, d)
    if kind == 0:
        w_in = _stack_rows(wts["mla_w_in"])
        w_cq = w_in[:, :MLA_Q_RANK]
        w_ckv = w_in[:, MLA_Q_RANK:MLA_Q_RANK + MLA_KV_RANK]
        w_kr = w_in[:, MLA_Q_RANK + MLA_KV_RANK:]
        w_z = jnp.concatenate([w_cq, w_ckv, _pad_last(w_kr, hd), _pad_last(_swap_halves(w_kr), hd)], axis=-1)
        z = make_mm(tag + "_mla_in", F32)(h, w_z)
        c_q, c_kv = z[:, :MLA_Q_RANK], z[:, MLA_Q_RANK:MLA_Q_RANK + MLA_KV_RANK]
        kr_raw, kr_swp = z[:, -2 * hd:-hd], z[:, -hd:]
        c_qn = rms_op(tag + "_rms_q", s, MLA_Q_RANK, BF16)(c_q, p["mla_q_norm"][j].reshape(1, -1))[0]
        c_kvn = rms_op(tag + "_rms_kv", s, MLA_KV_RANK, BF16)(c_kv, p["mla_kv_norm"][j].reshape(1, -1))[0]
        w_uq8 = wts["mla_w_uq"]
        w_qn = _unblock(w_uq8[:, :, :MLA_NOPE])
        w_qr = w_uq8[:, :, MLA_NOPE:]
        w_qr_p = _unblock(_pad_last(w_qr, hd))
        w_qr_s = _unblock(_pad_last(_swap_halves(w_qr), hd))
        q_nope = make_mm(tag + "_mla_uq_n", BF16)(c_qn, w_qn)
        q_raw = make_mm(tag + "_mla_uq_r", F32)(c_qn, w_qr_p)
        q_swp = make_mm(tag + "_mla_uq_s", F32)(c_qn, w_qr_s)
        nq = MLA_HEADS * hd
        kv = make_mm(tag + "_mla_ukv", BF16, blocked=True)(c_kvn, wts["mla_w_ukv"])
        heads_row = ("row", hd, MLA_HEADS)
        q_rope = make_tile_op(fn_rope, tag + "_rope_q", [heads_row, heads_row, "row", "row"],
                              [True, True, False, False], [(heads_row, nq, F32)], s, 512)(
            q_raw, q_swp, rope_c, rope_s)[0]
        k_rope = make_tile_op(fn_rope, tag + "_rope_k", ["row", "row", "row", "row"],
                              [True, True, False, False], [("row", hd, F32)], s, 1024)(
            kr_raw, kr_swp, rope_c, rope_s)[0]
        n_groups = MLA_QUERY_GROUPS if s % (MLA_QUERY_GROUPS * 256) == 0 else 1
        rows_g = s // n_groups
        o_groups = []
        for grp in range(n_groups):
            r0, r1 = grp * rows_g, (grp + 1) * rows_g
            o_groups.append(make_tile_op(
                fn_mla_attn, f"{tag}_mla_attn{grp}", [("rowh", hd), ("rowh", hd), ("parh", hd, 2), "par"],
                [True] * 4, [(("rowh", hd), nq, BF16)], rows_g, 256, MLA_HEADS, row_base=r0)(
                q_nope[r0:r1], q_rope[r0:r1], kv[:r1], k_rope[:r1])[0])
        o = jnp.concatenate(o_groups, axis=0)
        return make_mm_res_rms(tag + "_mla_o")(x, o, _stack_rows(wts["mla_w_o"]), next_gain)
    if kind == 1:
        ng = GDN_HEADS * GDN_DK
        w_in = _unblock(wts["gdn_w_in"])
        cw = p["gdn_conv_w"][j]
        conv_out = []
        for part, nm in enumerate(("q", "k", "v")):
            cols = slice(part * ng, (part + 1) * ng)
            pre = make_mm(f"{tag}_gdn_in_{nm}", F32)(h, w_in[:, cols])
            conv_out.append(make_conv(f"{tag}_gdn_conv_{nm}")(pre, cw[:, cols]))
        gate = make_mm(tag + "_gdn_in_g", F32)(h, w_in[:, 3 * ng:4 * ng])
        ba = make_mm(tag + "_gdn_in_ba", F32)(h, _pad_last(w_in[:, 4 * ng:], LANES))
        heads_row = ("row", GDN_DK, GDN_HEADS)
        q, k, v = make_tile_op(fn_gdn_prep, tag + "_gdn_prep", [heads_row] * 3, [True] * 3,
                               [(heads_row, ng, F32)] * 3, s, 512)(*conv_out)
        alog = jnp.pad(p["gdn_a_log"][j].reshape(1, -1), ((0, 0), (GDN_HEADS, LANES - 2 * GDN_HEADS)))
        dtb = jnp.pad(p["gdn_dt_bias"][j].reshape(1, -1), ((0, 0), (GDN_HEADS, LANES - 2 * GDN_HEADS)))
        beta_b, g_b = make_tile_op(fn_gdn_gates, tag + "_gdn_gates", ["row", "par", "par"], [True] * 3,
                                   [("row", ng, F32)] * 2, s, 512)(ba, alog, dtb)
        o = make_gdn(tag + "_gdn_core")(q, k, v, g_b, beta_b)
        o = make_tile_op(fn_gdn_out, tag + "_gdn_out", [heads_row, heads_row, "par"],
                         [True] * 3, [(heads_row, ng, BF16)], s, 512)(
            o, gate, p["gdn_o_norm"][j].reshape(1, -1))[0]
        return make_mm_res_rms(tag + "_gdn_o")(x, o, _stack_rows(wts["gdn_w_o"]), next_gain)
    w_in = _unblock(wts["sc_w_in"])
    b_gate = make_mm(tag + "_sc_in_b", F32)(h, w_in[:, :d])
    c_gate = make_mm(tag + "_sc_in_c", F32)(h, w_in[:, d:2 * d])
    u = make_mm(tag + "_sc_in_u", F32)(h, w_in[:, 2 * d:])
    cu = make_tile_op(fn_mul, tag + "_sc_cu", ["row", "row"], [True, True], [("row", d, F32)], s, 512)(
        c_gate, u)[0]
    cv = make_conv(tag + "_sc_conv")(cu, p["sc_conv_w"][j])
    yv = make_tile_op(fn_mul, tag + "_sc_gate", ["row", "row"], [True, True], [("row", d, BF16)], s, 512)(
        b_gate, cv)[0]
    return make_mm_res_rms(tag + "_sc_o")(x, yv, _stack_rows(wts["sc_w_o"]), next_gain)


def seg_xattn(i, wts, p, x, hx, mem_n):
    s, d = x.shape
    tag = f"l{i}"
    q = make_mm(tag + "_xa_q", BF16)(hx, _stack_rows(wts["xa_w_q"]))
    kv = make_mm(tag + "_xa_kv", BF16, blocked=True)(mem_n, wts["xa_w_kv"])
    o = make_tile_op(fn_xattn, tag + "_xattn",
                     [("rowh", X_HEAD_DIM), ("parh", X_HEAD_DIM), ("parh", X_HEAD_DIM)], [True] * 3,
                     [(("rowh", X_HEAD_DIM), d, BF16)], s, 1024, X_HEADS)(q, kv[:, :d], kv[:, d:])[0]
    return make_mm_res_rms(tag + "_xa_o")(x, o, _stack_rows(wts["xa_w_o"]), p["norm_mlp"][i].reshape(1, d))


def seg_mlp(i, wts, p, x, hm):
    d = x.shape[1]
    gain = p["norm_mix"][i + 1].reshape(1, d) if i + 1 < DEPTH else None
    return make_mlp(f"l{i}_mlp", gain is not None)(x, hm, wts["mlp_w1"], _stack_rows(wts["mlp_w2"]), gain)


def segments():
    segs = []
    for i in range(DEPTH):
        j, kind = i // N_MIXERS, i % N_MIXERS
        segs.append((f"l{i}_mixer", [(n, j) for n in MIXER_WEIGHTS[kind]], MIXER_PARAMS[kind], "mixer"))
        segs.append((f"l{i}_xattn", [(n, i) for n in ("xa_w_q", "xa_w_kv", "xa_w_o")], ["norm_mlp"], "xattn"))
        segs.append((f"l{i}_mlp", [(n, i) for n in ("mlp_w1", "mlp_w2")], ["norm_mix"] if i + 1 < DEPTH else [],
                     "mlp"))
    return segs


def run_segment(index, kind, wts, p, x, h, mem_n, rope_c, rope_s):
    layer = index // 3
    if kind == "mixer":
        return seg_mixer(layer, wts, p, x, h, rope_c, rope_s)
    if kind == "xattn":
        return seg_xattn(layer, wts, p, x, h, mem_n)
    return seg_mlp(layer, wts, p, x, h)


def rope_tables(positions):
    inv_freq = ROPE_THETA ** (-jnp.arange(0, MLA_ROPE, 2, dtype=F32) / MLA_ROPE)
    ang = positions.astype(F32)[:, None] * inv_freq
    cos, sin = jnp.cos(ang), jnp.sin(ang)
    zeros = jnp.zeros((positions.shape[0], MLA_NOPE - MLA_ROPE), F32)
    return jnp.concatenate([cos, cos, zeros], axis=-1), jnp.concatenate([-sin, sin, zeros], axis=-1)


def kernel(x, mem, positions, mla_w_in, mla_q_norm, mla_kv_norm, mla_w_uq, mla_w_ukv, mla_w_o, gdn_w_in, gdn_conv_w, gdn_a_log, gdn_dt_bias, gdn_o_norm, gdn_w_o, sc_w_in, sc_conv_w, sc_w_o, norm_mix, norm_mem, norm_mlp, xa_w_q, xa_w_kv, xa_w_o, mlp_w1, mlp_w2, mem_norm, final_norm, loss_target, m_mla_w_in, m_mla_q_norm, m_mla_kv_norm, m_mla_w_uq, m_mla_w_ukv, m_mla_w_o, m_gdn_w_in, m_gdn_conv_w, m_gdn_a_log, m_gdn_dt_bias, m_gdn_o_norm, m_gdn_w_o, m_sc_w_in, m_sc_conv_w, m_sc_w_o, m_norm_mix, m_norm_mem, m_norm_mlp, m_xa_w_q, m_xa_w_kv, m_xa_w_o, m_mlp_w1, m_mlp_w2, m_mem_norm, m_final_norm, v_mla_w_in, v_mla_q_norm, v_mla_kv_norm, v_mla_w_uq, v_mla_w_ukv, v_mla_w_o, v_gdn_w_in, v_gdn_conv_w, v_gdn_a_log, v_gdn_dt_bias, v_gdn_o_norm, v_gdn_w_o, v_sc_w_in, v_sc_conv_w, v_sc_w_o, v_norm_mix, v_norm_mem, v_norm_mlp, v_xa_w_q, v_xa_w_kv, v_xa_w_o, v_mlp_w1, v_mlp_w2, v_mem_norm, v_final_norm):
    args = locals()
    w_loc = {n: args[n] for n in WEIGHTS}
    m_loc = {n: args["m_" + n] for n in WEIGHTS}
    v_loc = {n: args["v_" + n] for n in WEIGHTS}
    me = 4 * lax.axis_index("x") + 2 * lax.axis_index("y") + lax.axis_index("c")
    segs = segments()

    w16 = {n: w_loc[n].astype(BF16) for n in BIG}
    tiny_pack = pack_rows([w_loc[n].reshape(-1) for n, _ in TINY], LANES, 8)
    gather_handles, token = gather2_start(
        [[tiny_pack]] + [[w16[n][layer] for n, layer in units] for _, units, _, _ in segs], "gather_start")

    x_cur = x[0]
    rope_c, rope_s = rope_tables(positions[0])
    tiny_handle, _ = gather2_forward(gather_handles[0], token, "gather_forward_tiny")
    tiny_all = gather2_wait(tiny_handle, token, "gather_wait_tiny")[0]
    gather_handles = gather_handles[1:]
    params = {}
    for (n, ax), a8 in zip(TINY, unpack_rows(tiny_all, [w_loc[n].shape for n, _ in TINY])):
        params[n] = from_shards(a8, ax)
    for n in REPL:
        params[n] = w_loc[n]

    mem_n, vjp_memory = jax.vjp(lambda p_: seg_memory(p_, mem[0]), {"mem_norm": params["mem_norm"]})
    h_cur, vjp_first_norm = jax.vjp(
        lambda p_, x_: rms_op("l0_rms_mix", x_.shape[0], x_.shape[1], BF16)(x_, p_["norm_mix"][0].reshape(1, -1))[0],
        {"norm_mix": params["norm_mix"]}, x_cur)
    vjps = []
    forwarded, _ = gather2_forward(gather_handles[0], token, f"gather_forward_{segs[0][0]}")
    for index, (tag, units, p_names, kind) in enumerate(segs):
        landed = gather2_wait(forwarded, token if index == 0 else x_cur, f"gather_wait_{tag}")
        if index + 1 < len(segs):
            forwarded, (x_cur, h_cur) = gather2_forward(gather_handles[index + 1], landed[0],
                                                        f"gather_forward_{segs[index + 1][0]}", carry=[x_cur, h_cur])
        wts = {n: a for (n, _), a in zip(units, landed)}
        p_seg = {n: params[n] for n in p_names}
        outs, vjp_seg = jax.vjp(
            lambda w_, p_, x_, h_, m_, index=index, kind=kind:
            run_segment(index, kind, w_, p_, x_, h_, m_, rope_c, rope_s),
            wts, p_seg, x_cur, h_cur, mem_n)
        x_cur, h_cur = outs[0], (outs[1] if len(outs) > 1 else None)
        vjps.append(vjp_seg)

    loss_vec, g_x, d_final = loss_head(x_cur, params["final_norm"].reshape(1, -1), loss_target[0], "loss_head")

    grads = {n: jnp.zeros_like(params[n]) for n in params}
    grads["final_norm"] = d_final.reshape(-1)
    g_mem_n = jnp.zeros_like(mem_n)
    g_h = None
    scatter_handles = []
    for (tag, units, _, _), vjp_seg in zip(reversed(segs), reversed(vjps)):
        g_wts, g_p, g_x, g_h, g_m = vjp_seg((g_x,) if g_h is None else (g_x, g_h))
        for n, g in g_p.items():
            grads[n] = grads[n] + g
        g_mem_n = g_mem_n + g_m
        handle, _, (g_x, g_h) = exchange_start("scatter", [g_wts[n] for n, _ in units], f"scatter_start_{tag}",
                                               carry=[g_x, g_h])
        scatter_handles.append((units, handle))
    grads["mem_norm"] = grads["mem_norm"] + vjp_memory(g_mem_n)[0]["mem_norm"]
    g_first, g_x_norm = vjp_first_norm(g_h)
    grads["norm_mix"] = grads["norm_mix"] + g_first["norm_mix"]
    g_x = g_x + g_x_norm

    small_names = [n for n, _ in TINY] + REPL
    small_g = pack_rows([loss_vec[0, :1]] + [grads[n].astype(F32).reshape(-1) for n in small_names], PACK_W, 8)
    small_handle, _, _ = exchange_start("gather", [small_g], "gather_start_small_grads")

    g_recv = {}
    for units, handle in scatter_handles:
        landed = exchange_wait(handle, g_x, f"scatter_wait_{units[0][0]}_{units[0][1]}")
        g_recv.update(dict(zip(units, landed)))

    res = {}
    for n in BIG:
        outs = None
        for layer in range(w_loc[n].shape[0]):
            outs = adamw(g_recv[n, layer], w_loc[n], m_loc[n], v_loc[n], layer, outs, f"adamw_{n}_{layer}")
        for kind, a in zip(("grad", "delta", "m", "v"), outs):
            res[(kind, n)] = a
    small_recv = exchange_wait(small_handle, res[("grad", BIG[-1])], "gather_wait_small_grads")[0]

    def full_small(d):
        parts = [jnp.zeros((1,), F32)]
        for n, ax in TINY:
            full_shape = params[n].shape
            start = [0] * len(full_shape)
            start[ax] = me * d[n].shape[ax]
            parts.append(lax.dynamic_update_slice(jnp.zeros(full_shape, F32), d[n], start).reshape(-1))
        parts += [d[n].reshape(-1) for n in REPL]
        return pack_rows(parts, PACK_W, 8)

    outs_small = adamw(small_recv, full_small(w_loc)[None], full_small(m_loc)[None], full_small(v_loc)[None],
                       0, None, "adamw_small")
    small_shapes = [(1,)] + [params[n].shape for n, _ in TINY] + [w_loc[n].shape for n in REPL]
    loss = None
    for kind, packed in zip(("grad", "delta", "m", "v"), outs_small):
        parts = unpack_rows(packed[0], small_shapes)
        if kind == "grad":
            loss = parts[0][0]
        for (n, ax), a in zip(TINY, parts[1:1 + len(TINY)]):
            start = [0] * a.ndim
            start[ax] = me * w_loc[n].shape[ax]
            res[(kind, n)] = lax.dynamic_slice(a, start, w_loc[n].shape)
        for n, a in zip(REPL, parts[1 + len(TINY):]):
            res[(kind, n)] = a

    out = [loss, g_x[None]]
    for kind in ("grad", "delta", "m", "v"):
        out += [res[(kind, n)] for n in WEIGHTS]
    return tuple(out)
```

```python
import functools
import math

import jax
import jax.numpy as jnp
from jax import lax
from jax.experimental import pallas as pl
from jax.experimental.pallas import tpu as pltpu

F32 = jnp.float32
BF16 = jnp.bfloat16

N_DEV = 8
LANES = 128
EPS = 1e-6
ROPE_THETA = 10000.0
MLA_HEADS, MLA_NOPE, MLA_ROPE, MLA_V = 8, 128, 64, 128
MLA_Q_RANK, MLA_KV_RANK = 384, 256
GDN_HEADS, GDN_DK, GDN_CONV, GDN_CHUNK = 8, 128, 4, 64
X_HEADS, X_HEAD_DIM = 4, 256
DEPTH, N_MIXERS = 4, 3
ADAM_LR, ADAM_B1, ADAM_B2, ADAM_EPS, ADAM_WD, ADAM_STEP = 0.001, 0.9, 0.999, 1e-08, 0.01, 10
MLA_QUERY_GROUPS = 4
NEG_BIG = -1e30
PACK_W = 1024


_NN = (((1,), (0,)), ((), ()))
_NT = (((1,), (1,)), ((), ()))
_TN = (((0,), (0,)), ((), ()))
_NN3 = (((2,), (1,)), ((0,), (0,)))
_NT3 = (((2,), (2,)), ((0,), (0,)))
_TN3 = (((1,), (1,)), ((0,), (0,)))


def _dot(a, b, dims):
    return lax.dot_general(a, b, dims, preferred_element_type=F32)


def _hi_lo(x):
    hi = x.astype(BF16)
    return hi, (x - hi.astype(F32)).astype(BF16)


def _split3(x):
    hi = x.astype(BF16)
    r = x - hi.astype(F32)
    mid = r.astype(BF16)
    return hi, mid, (r - mid.astype(F32)).astype(BF16)


def _dg(a, b, dims, prec):
    if prec == "h":
        return lax.dot_general(a, b, dims, precision=lax.Precision.HIGHEST, preferred_element_type=F32)
    if prec == "m":
        a_hi, a_lo = _hi_lo(a)
        b_hi, b_lo = _hi_lo(b)
        return _dot(a_hi, b_hi, dims) + _dot(a_hi, b_lo, dims) + _dot(a_lo, b_hi, dims)
    return _dot(a.astype(BF16), b.astype(BF16), dims)


def _dg_sel(sel, x, dims, sel_first):
    s16 = sel.astype(BF16)
    parts = [(_dot(s16, piece, dims) if sel_first else _dot(piece, s16, dims)) for piece in _split3(x)]
    return parts[0] + parts[1] + parts[2]


class _Ops:
    def __init__(self, prec, differentiable, batched=False):
        d_nn, d_nt, d_tn = (_NN3, _NT3, _TN3) if batched else (_NN, _NT, _TN)

        def nn(a, b):
            return _dg(a, b, d_nn, prec)

        def nt(a, b):
            return _dg(a, b, d_nt, prec)

        def tn(a, b):
            return _dg(a, b, d_tn, prec)

        if differentiable:
            dnn = jax.custom_vjp(nn)
            dnn.defvjp(lambda a, b: (nn(a, b), (a, b)), lambda r, g: (nt(g, r[1]), tn(r[0], g)))
            dnt = jax.custom_vjp(nt)
            dnt.defvjp(lambda a, b: (nt(a, b), (a, b)), lambda r, g: (nn(g, r[1]), tn(g, r[0])))
            dtn = jax.custom_vjp(tn)
            dtn.defvjp(lambda a, b: (tn(a, b), (a, b)), lambda r, g: (nt(r[1], g), nn(r[0], g)))
            nn, nt, tn = dnn, dnt, dtn
        self.nn, self.nt, self.tn = nn, nt, tn


class _SelOps:
    def __init__(self, differentiable, batched=False):
        d_nn, d_nt, d_tn = (_NN3, _NT3, _TN3) if batched else (_NN, _NT, _TN)

        def sel_nn(sel, x):
            return _dg_sel(sel, x, d_nn, True)

        def sel_nt(sel, x):
            return _dg_sel(sel, x, d_nt, True)

        if differentiable:
            dnn = jax.custom_vjp(sel_nn)
            dnn.defvjp(lambda s, x: (sel_nn(s, x), s),
                       lambda s, g: (jnp.zeros_like(s), _dg_sel(s, g, d_tn, True)))
            dnt = jax.custom_vjp(sel_nt)
            dnt.defvjp(lambda s, x: (sel_nt(s, x), s),
                       lambda s, g: (jnp.zeros_like(s), _dg_sel(s, g, d_tn, False)))
            sel_nn, sel_nt = dnn, dnt
        self.sel_nn, self.sel_nt = sel_nn, sel_nt


class _OpSet:
    def __init__(self, differentiable):
        self.b = _Ops("b", differentiable)
        self.h = _Ops("h", differentiable)
        self.bb = _Ops("b", differentiable, batched=True)
        self.bm = _Ops("m", differentiable, batched=True)
        self.bs = _SelOps(differentiable, batched=True)


_PLAIN = _OpSet(False)
_DIFF = _OpSet(True)


def _params(sem):
    return pltpu.CompilerParams(dimension_semantics=sem)


BLOCK_BYTES = 4 * 1024 * 1024


def _pick(n, cands):
    for c in cands:
        if n % c == 0:
            return c
    return n


def _tile(n, cap):
    if n <= cap:
        return n
    return _pick(n, tuple(c for c in (2048, 1024, 768, 512, 384, 256, 128) if c <= cap))


def matmul(a, b, form, out_dtype, name, res=None, blocked=False, relu_gate=None, rms_gain=None, relu2_out=False):
    if form == "nn":
        m, k = a.shape
        k2, n = (b.shape[1], N_DEV * b.shape[2]) if blocked else b.shape
    elif form == "nt":
        m, k = a.shape
        n, k2 = (b.shape[1], N_DEV * b.shape[2]) if blocked else b.shape
    else:
        (k, m), (k2, n) = a.shape, b.shape
    assert k == k2, (a.shape, b.shape, form)
    tk = k if k <= 2048 else _tile(k, 1024)
    cb = nb = 1
    if blocked:
        cb = (k if form == "nt" else n) // N_DEV
        nb = _pick(N_DEV, tuple(c for c in (8, 4, 2, 1) if c * cb <= 1024))
    if blocked and form == "nt":
        tk = nb * cb
    if blocked and form != "nt":
        tn = nb * cb
    else:
        tn = _tile(n, min(1024, BLOCK_BYTES // (tk * b.dtype.itemsize)))
    out_elems = BLOCK_BYTES // 2 if (out_dtype == BF16 and res is None) else BLOCK_BYTES // 4
    tm = _tile(m, min(BLOCK_BYTES // (tk * a.dtype.itemsize), out_elems // tn))
    nk = k // tk
    dims = {"nn": _NN, "nt": _NT, "tn": _TN}[form]

    a_spec = {"nn": pl.BlockSpec((tm, tk), lambda i, j, kk: (i, kk)),
              "nt": pl.BlockSpec((tm, tk), lambda i, j, kk: (i, kk)),
              "tn": pl.BlockSpec((tk, tm), lambda i, j, kk: (kk, i))}[form]
    if blocked and form == "nn":
        b_spec = pl.BlockSpec((nb, tk, cb), lambda i, j, kk: (j, kk, 0))
    elif blocked and form == "nt":
        b_spec = pl.BlockSpec((nb, tn, cb), lambda i, j, kk: (kk, j, 0))
    else:
        b_spec = {"nn": pl.BlockSpec((tk, tn), lambda i, j, kk: (kk, j)),
                  "nt": pl.BlockSpec((tn, tk), lambda i, j, kk: (j, kk)),
                  "tn": pl.BlockSpec((tk, tn), lambda i, j, kk: (kk, j))}[form]
    c_spec = pl.BlockSpec((tm, tn), lambda i, j, kk: (i, j))
    out_shape = jax.ShapeDtypeStruct((m, n), out_dtype)
    o_spec = c_spec
    blocked_out = blocked and form == "tn"
    if blocked_out:
        out_shape = jax.ShapeDtypeStruct((N_DEV, m, cb), out_dtype)
        o_spec = pl.BlockSpec((nb, tm, cb), lambda i, j, kk: (j, i, 0))
    has_res, has_gate, has_gain = res is not None, relu_gate is not None, rms_gain is not None
    extras = [e for e in (res, relu_gate) if e is not None]
    n_in = 2 + len(extras) + has_gain
    second = has_gain or relu2_out
    assert not (second and (blocked_out or tn != n and has_gain))

    def body(*refs):
        a_ref, b_ref = refs[0], refs[1]
        r_ref = refs[2] if has_res else None
        gate_ref = refs[2 + has_res] if has_gate else None
        gain_ref = refs[n_in - 1] if has_gain else None
        o_ref = refs[n_in]
        a_val = a_ref[...].astype(BF16)
        if blocked and form == "nn":
            part = jnp.concatenate([_dot(a_val, b_ref[t].astype(BF16), dims) for t in range(nb)], axis=-1)
        elif blocked and form == "nt":
            part = _dot(a_val[:, :cb], b_ref[0].astype(BF16), dims)
            for t in range(1, nb):
                part = part + _dot(a_val[:, t * cb:(t + 1) * cb], b_ref[t].astype(BF16), dims)
        else:
            part = _dot(a_val, b_ref[...].astype(BF16), dims)

        def finish(acc):
            if has_res:
                acc = acc + r_ref[...].astype(F32)
            if has_gate:
                acc = acc * (2.0 * jnp.maximum(gate_ref[...].astype(F32), 0.0))
            if blocked_out:
                for t in range(nb):
                    o_ref[t] = acc[:, t * cb:(t + 1) * cb].astype(out_dtype)
            else:
                o_ref[...] = acc.astype(out_dtype)
            if has_gain:
                refs[n_in + 1][...] = _rms(acc, gain_ref[...]).astype(BF16)
            if relu2_out:
                r = jnp.maximum(acc.astype(out_dtype).astype(F32), 0.0)
                refs[n_in + 1][...] = (r * r).astype(BF16)

        if nk == 1:
            finish(part)
        else:
            acc_ref = refs[-1]
            kk = pl.program_id(2)

            @pl.when(kk == 0)
            def _():
                acc_ref[...] = part

            @pl.when(jnp.logical_and(kk > 0, kk < nk - 1))
            def _():
                acc_ref[...] += part

            @pl.when(kk == nk - 1)
            def _():
                finish(acc_ref[...] + part)

    in_specs = [a_spec, b_spec] + [c_spec] * len(extras)
    args = [a, b] + extras
    if has_gain:
        in_specs.append(pl.BlockSpec((1, tn), lambda i, j, kk: (0, j)))
        args.append(rms_gain)
    if second:
        out_shape = [out_shape, jax.ShapeDtypeStruct((m, n), BF16)]
        o_spec = [o_spec, c_spec]
    return pl.pallas_call(
        body, name=name,
        out_shape=out_shape,
        grid=(m // tm, n // tn, nk),
        in_specs=in_specs, out_specs=o_spec,
        scratch_shapes=[pltpu.VMEM((tm, tn), F32)] if nk > 1 else [],
        compiler_params=_params(("parallel", "parallel", "arbitrary")),
    )(*args)


def make_mm(name, out_dtype, with_res=False, blocked=False):
    def bwd_mm(a, w, g):
        da = matmul(g, w, "nt", a.dtype, name + "_da", blocked=blocked)
        dw = matmul(a, g, "tn", w.dtype, name + "_dw", blocked=blocked)
        return da, dw

    if with_res:
        @jax.custom_vjp
        def op(res, a, w):
            return matmul(a, w, "nn", out_dtype, name + "_f", res=res, blocked=blocked)

        def fwd(res, a, w):
            return op(res, a, w), (a, w)

        def bwd(saved, g):
            return (g,) + bwd_mm(*saved, g)
    else:
        @jax.custom_vjp
        def op(a, w):
            return matmul(a, w, "nn", out_dtype, name + "_f", blocked=blocked)

        def fwd(a, w):
            return op(a, w), (a, w)

        def bwd(saved, g):
            return bwd_mm(*saved, g)
    op.defvjp(fwd, bwd)
    return op


def _rms_fan_bwd(x_new, gain, dx, dh, name):
    rows, d = x_new.shape
    outs = [("row", d, F32), ("row", d, BF16)]
    return tile_bwd(fn_fan_rms, name, ["row", "par"], [x_new, gain], [True, True], outs, [dx, dh],
                    rows, min(512, rows), 0)


def make_mm_res_rms(name):
    @jax.custom_vjp
    def op(res, a, w, gain):
        return tuple(matmul(a, w, "nn", F32, name + "_f", res=res, rms_gain=gain))

    def fwd(res, a, w, gain):
        x_new, h = op(res, a, w, gain)
        return (x_new, h), (a, w, x_new, gain)

    def bwd(saved, cts):
        a, w, x_new, gain = saved
        dx, dgain = _rms_fan_bwd(x_new, gain, cts[0], cts[1], name + "_nb")
        da = matmul(dx, w, "nt", a.dtype, name + "_da")
        dw = matmul(a, dx, "tn", w.dtype, name + "_dw")
        return dx, da, dw, dgain

    op.defvjp(fwd, bwd)
    return op


def make_mlp(name, with_norm):
    def run(x, h, w1, w2, gain):
        a, bsq = matmul(h, w1, "nn", BF16, name + "_1_f", blocked=True, relu2_out=True)
        out = matmul(bsq, w2, "nn", F32, name + "_2_f", res=x, rms_gain=gain if with_norm else None)
        return (tuple(out) if with_norm else (out,)), a, bsq

    @jax.custom_vjp
    def op(x, h, w1, w2, gain):
        return run(x, h, w1, w2, gain)[0]

    def fwd(x, h, w1, w2, gain):
        out, a, bsq = run(x, h, w1, w2, gain)
        return out, (h, w1, w2, gain, a, bsq, out[0])

    def bwd(saved, cts):
        h, w1, w2, gain, a, bsq, x_new = saved
        if with_norm:
            dx, dgain = _rms_fan_bwd(x_new, gain, cts[0], cts[1], name + "_nb")
        else:
            dx, dgain = cts[0], None
        da = matmul(dx, w2, "nt", BF16, name + "_2_da", relu_gate=a)
        dw2 = matmul(bsq, dx, "tn", w2.dtype, name + "_2_dw")
        dh = matmul(da, w1, "nt", h.dtype, name + "_1_da", blocked=True)
        dw1 = matmul(h, da, "tn", w1.dtype, name + "_1_dw", blocked=True)
        return dx, dh, dw1, dw2, dgain

    op.defvjp(fwd, bwd)
    return op


def _kind(k):
    if isinstance(k, str):
        return k, None, 1
    return k[0], k[1], (k[2] if len(k) > 2 else 1)


def _tile_spec(kind, shape, tm, heads):
    k, d, ns = _kind(kind)
    if k == "row":
        return pl.BlockSpec((tm, shape[1]), (lambda h, i: (i, 0)) if heads else (lambda i: (i, 0)))
    if k == "par":
        return pl.BlockSpec(tuple(shape), (lambda h, i: (0, 0)) if heads else (lambda i: (0, 0)))
    if k == "rowh":
        return pl.BlockSpec((tm, d * ns), lambda h, i: (i, h))
    if k == "parh":
        return pl.BlockSpec((shape[0], d * ns), lambda h, i: (0, h))
    raise ValueError(kind)


def _tile_grid(rows, tm, heads):
    n_rows = rows // tm
    return ((heads, n_rows) if heads else (n_rows,)), (1 if heads else 0)


def _split_vals(kinds, refs):
    vals, counts = [], []
    for kind, r in zip(kinds, refs):
        _, d, ns = _kind(kind)
        v = r[...].astype(F32)
        vals += [v] if ns == 1 else [v[:, p * d:(p + 1) * d] for p in range(ns)]
        counts.append(ns)
    return vals, counts


def tile_fwd(fn, name, kinds, args, outs, rows, tm, heads, row_base=0):
    grid, row_axis = _tile_grid(rows, tm, heads)
    n_in = len(args)
    out_shapes = [jax.ShapeDtypeStruct((rows, w), dt) for (_, w, dt) in outs]

    def body(*refs):
        vals, _ = _split_vals(kinds, refs[:n_in])
        row0 = row_base + pl.program_id(row_axis) * tm
        res = list(fn(_PLAIN, row0, *vals))
        for o_ref, (k, _, _) in zip(refs[n_in:], outs):
            pieces = [res.pop(0) for _ in range(_kind(k)[2])]
            v = pieces[0] if len(pieces) == 1 else jnp.concatenate(pieces, axis=-1)
            o_ref[...] = v.astype(o_ref.dtype)

    return pl.pallas_call(
        body, name=name, out_shape=out_shapes, grid=grid,
        in_specs=[_tile_spec(k, a.shape, tm, heads) for k, a in zip(kinds, args)],
        out_specs=[_tile_spec(k, (rows, w), tm, heads) for (k, w, _) in outs],
        compiler_params=_params(("arbitrary",) * len(grid)),
    )(*args)


def tile_bwd(fn, name, kinds, args, diff, outs, cts, rows, tm, heads, row_base=0):
    grid, row_axis = _tile_grid(rows, tm, heads)
    n_in, n_ct = len(args), len(cts)
    diff_idx = [i for i, d in enumerate(diff) if d]
    g_shapes, g_specs = [], []
    for i in diff_idx:
        k = _kind(kinds[i])[0]
        dt = args[i].dtype if k in ("row", "rowh") else F32
        g_shapes.append(jax.ShapeDtypeStruct(args[i].shape, dt))
        g_specs.append(_tile_spec(kinds[i], args[i].shape, tm, heads))

    def body(*refs):
        in_refs, ct_refs, g_refs = refs[:n_in], refs[n_in:n_in + n_ct], refs[n_in + n_ct:]
        vals, counts = _split_vals(kinds, in_refs)
        first_piece = [sum(counts[:i]) for i in range(n_in)]
        flat_diff = [first_piece[i] + p for i in diff_idx for p in range(counts[i])]
        row_id = pl.program_id(row_axis)
        row0 = row_base + row_id * tm

        def f(*dvals):
            full = list(vals)
            for i, dv in zip(flat_diff, dvals):
                full[i] = dv
            return tuple(fn(_DIFF, row0, *full))

        _, vjp = jax.vjp(f, *[vals[i] for i in flat_diff])
        ct_vals, _ = _split_vals([k for (k, _, _) in outs], ct_refs)
        flat_grads = list(vjp(tuple(ct_vals)))
        for g_ref, i in zip(g_refs, diff_idx):
            pieces = [flat_grads.pop(0) for _ in range(counts[i])]
            g = pieces[0] if len(pieces) == 1 else jnp.concatenate(pieces, axis=-1)
            k = _kind(kinds[i])[0]
            if k in ("row", "rowh"):
                g_ref[...] = g.astype(g_ref.dtype)
            else:
                first = row_id == 0
                if heads and k == "par":
                    first = jnp.logical_and(first, pl.program_id(0) == 0)

                @pl.when(first)
                def _(g_ref=g_ref, g=g):
                    g_ref[...] = g

                @pl.when(jnp.logical_not(first))
                def _(g_ref=g_ref, g=g):
                    g_ref[...] += g

    return pl.pallas_call(
        body, name=name, out_shape=g_shapes, grid=grid,
        in_specs=[_tile_spec(k, a.shape, tm, heads) for k, a in zip(kinds, args)]
        + [_tile_spec(k, (rows, w), tm, heads) for (k, w, _) in outs],
        out_specs=g_specs,
        compiler_params=_params(("arbitrary",) * len(grid)),
    )(*args, *cts)


def make_tile_op(fn, name, kinds, diff, outs, rows, tm, heads=0, row_base=0):
    tm = min(tm, rows)

    @jax.custom_vjp
    def op(*args):
        return tuple(tile_fwd(fn, name + "_f", kinds, args, outs, rows, tm, heads, row_base))

    def fwd(*args):
        return op(*args), args

    def bwd(args, cts):
        grads = tile_bwd(fn, name + "_b", kinds, args, diff, outs, cts, rows, tm, heads, row_base)
        it = iter(grads)
        res = []
        for a, d in zip(args, diff):
            res.append(next(it).astype(a.dtype) if d else None)
        return tuple(res)

    op.defvjp(fwd, bwd)
    return op


def _rms(x, g):
    return x * lax.rsqrt(jnp.mean(x * x, axis=-1, keepdims=True) + EPS) * g


def fn_rms(ops, row0, x, g):
    return (_rms(x, g),)


def fn_fan_rms(ops, row0, x, g):
    return x, _rms(x, g)


def fn_mul(ops, row0, a, b):
    return (a * b,)


def fn_rope(ops, row0, *t):
    nh = (len(t) - 2) // 2
    c, s = t[-2], t[-1]
    return tuple(x * c + xs * s for x, xs in zip(t[:nh], t[nh:2 * nh]))


def _softmax(s):
    m = lax.stop_gradient(jnp.max(s, axis=-1, keepdims=True))
    e = jnp.exp(s - m)
    return e / jnp.sum(e, axis=-1, keepdims=True)


def fn_xattn(ops, row0, q, k, v):
    s = ops.b.nt(q, k) * (X_HEAD_DIM ** -0.5)
    return (ops.b.nn(_softmax(s), v),)


def _silu(x):
    return x * jax.nn.sigmoid(x)


def fn_gdn_prep(ops, row0, *t):
    nh = len(t) // 3
    qs, ks, vs = [], [], []
    for qc, kc, vc in zip(t[:nh], t[nh:2 * nh], t[2 * nh:]):
        q, k = _silu(qc), _silu(kc)
        qs.append(q * lax.rsqrt(jnp.sum(q * q, -1, keepdims=True) + EPS) * (GDN_DK ** -0.5))
        ks.append(k * lax.rsqrt(jnp.sum(k * k, -1, keepdims=True) + EPS))
        vs.append(_silu(vc))
    return tuple(qs + ks + vs)


def fn_gdn_gates(ops, row0, ba, alog, dtb):
    width = GDN_HEADS * GDN_DK
    beta = jax.nn.sigmoid(ba)
    z = ba + dtb
    softplus = jnp.maximum(z, 0.0) + jnp.log1p(jnp.exp(-jnp.abs(z)))
    g = -jnp.exp(alog) * softplus
    r = lax.broadcasted_iota(jnp.int32, (LANES, width), 0)
    c = lax.broadcasted_iota(jnp.int32, (LANES, width), 1) // GDN_DK
    e_beta = (r == c).astype(F32)
    e_g = (r == c + GDN_HEADS).astype(F32)
    return ops.h.nn(beta, e_beta), ops.h.nn(g, e_g)


def fn_gdn_out(ops, row0, *t):
    nh = (len(t) - 1) // 2
    g = t[-1]
    return tuple(_rms(o, g) * _silu(gate) for o, gate in zip(t[:nh], t[nh:2 * nh]))


def fn_mla_attn(ops, row0, qn, qr, kn, v, kr):
    s = (ops.b.nt(qn, kn) + ops.b.nt(qr, kr)) * ((MLA_NOPE + MLA_ROPE) ** -0.5)
    rows = row0 + lax.broadcasted_iota(jnp.int32, s.shape, 0)
    cols = lax.broadcasted_iota(jnp.int32, s.shape, 1)
    s = jnp.where(rows >= cols, s, NEG_BIG)
    return (ops.b.nn(_softmax(s), v),)


def _shift_down(x, d, t_idx):
    if d == 0:
        return x
    return jnp.where(t_idx >= d, pltpu.roll(x, d, axis=0), 0.0)


def _shift_up(x, d, t_idx):
    if d == 0:
        return x
    n = x.shape[0]
    return jnp.where(t_idx < n - d, pltpu.roll(x, n - d, axis=0), 0.0)


def conv_fwd(x, w, name):
    s, c = x.shape
    kw = w.shape[0]
    tc = _pick(c, (256, 128))

    def body(x_ref, w_ref, y_ref):
        xv = x_ref[...]
        t_idx = lax.broadcasted_iota(jnp.int32, xv.shape, 0)
        acc = jnp.zeros_like(xv)
        for j in range(kw):
            acc = acc + w_ref[j:j + 1, :] * _shift_down(xv, kw - 1 - j, t_idx)
        y_ref[...] = acc

    return pl.pallas_call(
        body, name=name, out_shape=jax.ShapeDtypeStruct((s, c), F32), grid=(c // tc,),
        in_specs=[pl.BlockSpec((s, tc), lambda i: (0, i)), pl.BlockSpec((kw, tc), lambda i: (0, i))],
        out_specs=pl.BlockSpec((s, tc), lambda i: (0, i)),
        compiler_params=_params(("parallel",)),
    )(x, w)


def conv_bwd(x, w, dy, name):
    s, c = x.shape
    kw = w.shape[0]
    tc = _pick(c, (256, 128))

    def body(x_ref, w_ref, dy_ref, dx_ref, dw_ref):
        xv, dyv = x_ref[...], dy_ref[...]
        t_idx = lax.broadcasted_iota(jnp.int32, xv.shape, 0)
        dx = jnp.zeros_like(xv)
        for j in range(kw):
            d = kw - 1 - j
            dx = dx + w_ref[j:j + 1, :] * _shift_up(dyv, d, t_idx)
            dw_ref[j:j + 1, :] = jnp.sum(dyv * _shift_down(xv, d, t_idx), axis=0, keepdims=True)
        dx_ref[...] = dx

    return pl.pallas_call(
        body, name=name,
        out_shape=[jax.ShapeDtypeStruct((s, c), F32), jax.ShapeDtypeStruct((kw, c), F32)],
        grid=(c // tc,),
        in_specs=[pl.BlockSpec((s, tc), lambda i: (0, i)), pl.BlockSpec((kw, tc), lambda i: (0, i)),
                  pl.BlockSpec((s, tc), lambda i: (0, i))],
        out_specs=[pl.BlockSpec((s, tc), lambda i: (0, i)), pl.BlockSpec((kw, tc), lambda i: (0, i))],
        compiler_params=_params(("parallel",)),
    )(x, w, dy)


def make_conv(name):
    @jax.custom_vjp
    def op(x, w):
        return conv_fwd(x, w, name + "_f")

    def fwd(x, w):
        return op(x, w), (x, w)

    def bwd(saved, dy):
        dx, dw = conv_bwd(saved[0], saved[1], dy, name + "_b")
        return dx, dw

    op.defvjp(fwd, bwd)
    return op


def _gdn_consts():
    c, d = GDN_CHUNK, GDN_DK
    i = lax.broadcasted_iota(jnp.int32, (c, c), 0)
    j = lax.broadcasted_iota(jnp.int32, (c, c), 1)
    tri = i >= j
    return dict(
        tri=tri, strict=i > j,
        tri_f=tri.astype(F32),
        eye=(i == j).astype(F32),
        lane0=(lax.broadcasted_iota(jnp.int32, (c, d), 1) == 0).astype(F32),
        last_row=(lax.broadcasted_iota(jnp.int32, (c, d), 0) == c - 1).astype(F32),
    )


def _gdn_chunk(ops, q, k, v, g, beta, state):
    b, m, sel = ops.bb, ops.bm, ops.bs
    nh, c, d = q.shape[0], GDN_CHUNK, GDN_DK
    k_ = _gdn_consts()

    def per_head(a):
        return jnp.broadcast_to(a, (nh,) + a.shape)

    gc = sel.sel_nn(per_head(k_["tri_f"]), g)
    col = jnp.broadcast_to(jnp.sum(gc * k_["lane0"], axis=2, keepdims=True), (nh, c, c))
    row = sel.sel_nt(per_head(k_["lane0"]), gc)
    decay = jnp.where(k_["tri"], jnp.exp(jnp.where(k_["tri"], col - row, 0.0)), 0.0)
    kb = k * beta
    mm_ = jnp.where(k_["strict"], b.nt(kb, k) * decay, 0.0)
    p = -mm_
    t = k_["eye"] + p
    for _ in range(int(math.log2(GDN_CHUNK)) - 1):
        p = m.nn(p, p)
        t = t + m.nn(t, p)
    egc = jnp.exp(gc)
    u = b.nn(t, v * beta)
    w = b.nn(t, kb * egc)
    attn = b.nt(q, k) * decay
    v_new = u - b.nn(w, state)
    o = b.nn(q * egc, state) + b.nn(attn, v_new)
    g_last = jnp.sum(gc * k_["last_row"], axis=1, keepdims=True)
    new_state = (state * jnp.exp(jnp.broadcast_to(g_last, (nh, d, d)))
                 + b.tn(k * jnp.exp(jnp.broadcast_to(g_last, (nh, c, d)) - gc), v_new))
    return o, new_state


GDN_HEAD_GROUP = 8
GDN_TILE_CHUNKS = 4


def _heads_of(ref, rows, n_heads):
    d = GDN_DK
    return jnp.stack([ref[rows, h * d:(h + 1) * d] for h in range(n_heads)])


def _gdn_specs(s, reverse):
    d, hg = GDN_DK, GDN_HEAD_GROUP
    tile = min(GDN_TILE_CHUNKS * GDN_CHUNK, s)
    n_tiles = s // tile
    t_of = (lambda t: n_tiles - 1 - t) if reverse else (lambda t: t)
    seq = pl.BlockSpec((tile, hg * d), lambda grp, t: (t_of(t), grp))
    st = pl.BlockSpec((hg, tile // GDN_CHUNK, d, d), lambda grp, t: (grp, t_of(t), 0, 0))
    return seq, st, tile, n_tiles


def gdn_fwd(q, k, v, g, beta, name):
    s = q.shape[0]
    d, hg = GDN_DK, GDN_HEAD_GROUP
    seq, st, tile, n_tiles = _gdn_specs(s, False)

    def body(q_ref, k_ref, v_ref, g_ref, b_ref, o_ref, st_ref, state_scr):
        @pl.when(pl.program_id(1) == 0)
        def _():
            state_scr[...] = jnp.zeros_like(state_scr)

        def step(ci, carry):
            rows = pl.ds(pl.multiple_of(ci * GDN_CHUNK, GDN_CHUNK), GDN_CHUNK)
            state = state_scr[...]
            for h in range(hg):
                st_ref[h, ci] = state[h]
            o, new_state = _gdn_chunk(_PLAIN, *[_heads_of(r, rows, hg) for r in (q_ref, k_ref, v_ref, g_ref, b_ref)],
                                      state)
            for h in range(hg):
                o_ref[rows, h * d:(h + 1) * d] = o[h]
            state_scr[...] = new_state
            return carry

        lax.fori_loop(0, tile // GDN_CHUNK, step, 0)

    return pl.pallas_call(
        body, name=name,
        out_shape=[jax.ShapeDtypeStruct(q.shape, F32),
                   jax.ShapeDtypeStruct((GDN_HEADS, s // GDN_CHUNK, d, d), F32)],
        grid=(GDN_HEADS // hg, n_tiles), in_specs=[seq] * 5, out_specs=[seq, st],
        scratch_shapes=[pltpu.VMEM((hg, d, d), F32)],
        compiler_params=_params(("parallel", "arbitrary")),
    )(q, k, v, g, beta)


def gdn_bwd(q, k, v, g, beta, states, do, name):
    s = q.shape[0]
    d, hg = GDN_DK, GDN_HEAD_GROUP
    seq, st, tile, n_tiles = _gdn_specs(s, True)
    tile_chunks = tile // GDN_CHUNK

    def body(q_ref, k_ref, v_ref, g_ref, b_ref, st_ref, do_ref, dq_ref, dk_ref, dv_ref, dg_ref, db_ref, dstate_scr):
        @pl.when(pl.program_id(1) == 0)
        def _():
            dstate_scr[...] = jnp.zeros_like(dstate_scr)

        def step(it, carry):
            ci = tile_chunks - 1 - it
            rows = pl.ds(pl.multiple_of(ci * GDN_CHUNK, GDN_CHUNK), GDN_CHUNK)
            prim = [_heads_of(r, rows, hg) for r in (q_ref, k_ref, v_ref, g_ref, b_ref)]
            prim.append(jnp.stack([st_ref[h, ci] for h in range(hg)]))
            _, vjp = jax.vjp(functools.partial(_gdn_chunk, _DIFF), *prim)
            grads = vjp((_heads_of(do_ref, rows, hg), dstate_scr[...]))
            for g_ref_out, gr in zip((dq_ref, dk_ref, dv_ref, dg_ref, db_ref), grads[:5]):
                for h in range(hg):
                    g_ref_out[rows, h * d:(h + 1) * d] = gr[h]
            dstate_scr[...] = grads[5]
            return carry

        lax.fori_loop(0, tile_chunks, step, 0)

    return pl.pallas_call(
        body, name=name,
        out_shape=[jax.ShapeDtypeStruct(q.shape, F32)] * 5,
        grid=(GDN_HEADS // hg, n_tiles), in_specs=[seq] * 5 + [st, seq], out_specs=[seq] * 5,
        scratch_shapes=[pltpu.VMEM((hg, d, d), F32)],
        compiler_params=_params(("parallel", "arbitrary")),
    )(q, k, v, g, beta, states, do)


def make_gdn(name):
    @jax.custom_vjp
    def op(q, k, v, g, beta):
        return gdn_fwd(q, k, v, g, beta, name + "_f")[0]

    def fwd(q, k, v, g, beta):
        o, states = gdn_fwd(q, k, v, g, beta, name + "_f")
        return o, (q, k, v, g, beta, states)

    def bwd(saved, do):
        return tuple(gdn_bwd(*saved, do, name + "_b"))

    op.defvjp(fwd, bwd)
    return op


def loss_head(x, g, target, name):
    s, d = x.shape
    tm = min(256, s)

    def body(x_ref, g_ref, t_ref, loss_ref, dx_ref, dg_ref):
        tgt = t_ref[...]

        def f(xv, gv):
            err = _rms(xv, gv) - tgt
            per_row = jnp.mean(err * err, axis=-1, keepdims=True)
            return 0.5 * jnp.sum(per_row, axis=0, keepdims=True)

        val, vjp = jax.vjp(f, x_ref[...], g_ref[...])
        dx, dg = vjp(jnp.ones((1, 1), F32))
        dx_ref[...] = dx
        first = pl.program_id(0) == 0

        @pl.when(first)
        def _():
            dg_ref[...] = dg
            loss_ref[...] = jnp.broadcast_to(val, loss_ref.shape)

        @pl.when(jnp.logical_not(first))
        def _():
            dg_ref[...] += dg
            loss_ref[...] += jnp.broadcast_to(val, loss_ref.shape)

    row = pl.BlockSpec((tm, d), lambda i: (i, 0))
    vec = pl.BlockSpec((1, d), lambda i: (0, 0))
    return pl.pallas_call(
        body, name=name,
        out_shape=[jax.ShapeDtypeStruct((1, LANES), F32), jax.ShapeDtypeStruct((s, d), F32),
                   jax.ShapeDtypeStruct((1, d), F32)],
        grid=(s // tm,), in_specs=[row, vec, row],
        out_specs=[pl.BlockSpec((1, LANES), lambda i: (0, 0)), row, vec],
        compiler_params=_params(("arbitrary",)),
    )(x, g, target)


def adamw(g8, w, m, v, layer, prev, name):
    n_layers, rows, width = w.shape
    tr = _pick(rows, (256, 128, 64, 32, 16, 8))

    def body(g_ref, w_ref, m_ref, v_ref, *rest):
        go_ref, d_ref, mo_ref, vo_ref = rest[-4:]
        g = g_ref[0].astype(F32)
        for p in range(1, N_DEV):
            g = g + g_ref[p].astype(F32)
        m_new = ADAM_B1 * m_ref[...] + (1.0 - ADAM_B1) * g
        v_new = ADAM_B2 * v_ref[...] + (1.0 - ADAM_B2) * (g * g)
        m_hat = m_new / (1.0 - ADAM_B1 ** ADAM_STEP)
        v_hat = v_new / (1.0 - ADAM_B2 ** ADAM_STEP)
        go_ref[...] = g
        d_ref[...] = -ADAM_LR * (m_hat / (jnp.sqrt(v_hat) + ADAM_EPS) + ADAM_WD * w_ref[...])
        mo_ref[...] = m_new
        vo_ref[...] = v_new

    blk = pl.BlockSpec((None, tr, width), lambda i: (layer, i, 0))
    carried = list(prev) if prev is not None else []
    return pl.pallas_call(
        body, name=name, out_shape=[jax.ShapeDtypeStruct((n_layers, rows, width), F32)] * 4,
        grid=(rows // tr,),
        in_specs=[pl.BlockSpec((N_DEV, tr, width), lambda i: (0, i, 0)), blk, blk, blk]
        + [pl.BlockSpec(memory_space=pl.ANY)] * len(carried),
        out_specs=[blk] * 4,
        input_output_aliases={4 + j: j for j in range(len(carried))},
        compiler_params=_params(("parallel",)),
    )(g8, w, m, v, *carried)


_HBM = pl.BlockSpec(memory_space=pltpu.HBM)
_SEM = pl.BlockSpec(memory_space=pltpu.SEMAPHORE)
_EFFECT = pltpu.SideEffectType.DATAFLOW_SIDE_EFFECTING


def _exchange_copies(mode, src_refs, land_refs, send_sems, recv_sems, local_sems):
    x, y, c = lax.axis_index("x"), lax.axis_index("y"), lax.axis_index("c")
    me = 4 * x + 2 * y + c
    n = len(src_refs)

    def src(k, p):
        return src_refs[k] if mode == "gather" else src_refs[k].at[p]

    local = [pltpu.make_async_copy(src(k, me), land_refs[k].at[me], local_sems.at[k]) for k in range(n)]
    sends, recvs = [], []
    for k in range(n):
        for r in range(1, N_DEV):
            px = (1 - x) if r & 4 else x
            py = (1 - y) if r & 2 else y
            pc = (1 - c) if r & 1 else c
            p = 4 * px + 2 * py + pc
            sem = k * (N_DEV - 1) + r - 1
            sends.append(pltpu.make_async_remote_copy(
                src_ref=src(k, p), dst_ref=land_refs[k].at[me],
                send_sem=send_sems.at[sem], recv_sem=recv_sems.at[sem],
                device_id=(px, py, pc), device_id_type=pl.DeviceIdType.MESH))
            recvs.append(pltpu.make_async_remote_copy(
                src_ref=src(k, p), dst_ref=land_refs[k].at[p],
                send_sem=send_sems.at[sem], recv_sem=recv_sems.at[sem],
                device_id=(px, py, pc), device_id_type=pl.DeviceIdType.MESH))
    return local, sends, recvs


def exchange_start(mode, arrays, name, carry=()):
    n, nc = len(arrays), len(carry)
    land_shapes = [((N_DEV,) + tuple(a.shape)) if mode == "gather" else tuple(a.shape) for a in arrays]
    lands = [pltpu.with_memory_space_constraint(lax.empty(shp, a.dtype), pltpu.HBM)
             for shp, a in zip(land_shapes, arrays)]
    srcs = [pltpu.with_memory_space_constraint(a, pltpu.HBM) for a in arrays]
    carried = [pltpu.with_memory_space_constraint(a, pltpu.HBM) for a in carry]

    def body(*refs):
        src_refs, land_refs = refs[:n], refs[n:2 * n]
        first_out = 2 * n + nc
        send_sems, recv_sems, local_sems = refs[first_out:first_out + 3]
        token = refs[-1]
        local, sends, _ = _exchange_copies(mode, src_refs, land_refs, send_sems, recv_sems, local_sems)
        for cp in local + sends:
            cp.start()
        token[...] = jnp.zeros_like(token)

    n_sem = n * (N_DEV - 1)
    out = pl.pallas_call(
        body, name=name,
        out_shape=(pltpu.SemaphoreType.DMA((n_sem,)), pltpu.SemaphoreType.DMA((n_sem,)),
                   pltpu.SemaphoreType.DMA((n,)),
                   *[pltpu.HBM(a.shape, a.dtype) for a in arrays],
                   *[pltpu.HBM(shp, a.dtype) for shp, a in zip(land_shapes, arrays)],
                   *[pltpu.HBM(a.shape, a.dtype) for a in carry],
                   jax.ShapeDtypeStruct((8, LANES), F32)),
        in_specs=[_HBM] * (2 * n + nc),
        out_specs=(_SEM, _SEM, _SEM, *[_HBM] * (2 * n + nc), pl.BlockSpec(memory_space=pltpu.VMEM)),
        input_output_aliases={i: 3 + i for i in range(2 * n + nc)},
        compiler_params=pltpu.CompilerParams(has_side_effects=_EFFECT),
    )(*srcs, *lands, *carried)
    handle = dict(mode=mode, sems=out[:3], srcs=out[3:3 + n], lands=out[3 + n:3 + 2 * n])
    return handle, out[-1], list(out[3 + 2 * n:3 + 2 * n + nc])


def exchange_wait(handle, after, name):
    mode, srcs, lands = handle["mode"], list(handle["srcs"]), list(handle["lands"])
    n = len(srcs)

    def body(*refs):
        src_refs, land_refs = refs[:n], refs[n:2 * n]
        send_sems, recv_sems, local_sems = refs[2 * n:2 * n + 3]
        local, sends, recvs = _exchange_copies(mode, src_refs, land_refs, send_sems, recv_sems, local_sems)
        for cp in sends:
            cp.wait_send()
        for cp in recvs:
            cp.wait_recv()
        for cp in local:
            cp.wait()

    out = pl.pallas_call(
        body, name=name,
        out_shape=(*[pltpu.HBM(a.shape, a.dtype) for a in srcs], *[pltpu.HBM(a.shape, a.dtype) for a in lands]),
        in_specs=[_HBM] * (2 * n) + [_SEM] * 3 + [pl.BlockSpec(memory_space=pl.ANY)],
        out_specs=tuple([_HBM] * (2 * n)),
        input_output_aliases={i: i for i in range(2 * n)},
        compiler_params=pltpu.CompilerParams(has_side_effects=_EFFECT),
    )(*srcs, *lands, *handle["sems"], after)
    return list(out[n:])


_ICI_RELATIONS = (2, 4, 6)


def _mesh_place():
    x, y, c = lax.axis_index("x"), lax.axis_index("y"), lax.axis_index("c")

    def peer(r):
        px = (1 - x) if r & 4 else x
        py = (1 - y) if r & 2 else y
        pc = (1 - c) if r & 1 else c
        return (px, py, pc), 4 * px + 2 * py + pc

    return 4 * x + 2 * y + c, peer


def _remote(src, dst, send_sem, recv_sem, device):
    return pltpu.make_async_remote_copy(src_ref=src, dst_ref=dst, send_sem=send_sem, recv_sem=recv_sem,
                                        device_id=device, device_id_type=pl.DeviceIdType.MESH)


def gather2_start(groups, name):
    flat = [a for g in groups for a in g]
    n = len(flat)
    lands = [pltpu.with_memory_space_constraint(lax.empty((N_DEV,) + tuple(a.shape), a.dtype), pltpu.HBM) for a in flat]
    srcs = [pltpu.with_memory_space_constraint(a, pltpu.HBM) for a in flat]
    n_rel = 1 + len(_ICI_RELATIONS)

    def body(*refs):
        src_refs, land_refs = refs[:n], refs[n:2 * n]
        sem_refs = refs[2 * n:2 * n + 4 * len(groups)]
        me, peer = _mesh_place()
        k = 0
        for gi, g in enumerate(groups):
            send_sems, recv_sib, recv_ici, local_sems = sem_refs[4 * gi:4 * gi + 4]
            for j in range(len(g)):
                pltpu.make_async_copy(src_refs[k], land_refs[k].at[me], local_sems.at[j]).start()
                dev, _ = peer(1)
                _remote(src_refs[k], land_refs[k].at[me], send_sems.at[n_rel * j], recv_sib.at[j], dev).start()
                for t, r in enumerate(_ICI_RELATIONS):
                    dev, _ = peer(r)
                    _remote(src_refs[k], land_refs[k].at[me], send_sems.at[n_rel * j + 1 + t],
                            recv_ici.at[len(_ICI_RELATIONS) * j + t], dev).start()
                k += 1
        refs[-1][...] = jnp.zeros_like(refs[-1])

    sem_shapes = []
    for g in groups:
        sem_shapes += [pltpu.SemaphoreType.DMA((n_rel * len(g),)), pltpu.SemaphoreType.DMA((len(g),)),
                       pltpu.SemaphoreType.DMA((len(_ICI_RELATIONS) * len(g),)), pltpu.SemaphoreType.DMA((len(g),))]
    out = pl.pallas_call(
        body, name=name,
        out_shape=(*sem_shapes, *[pltpu.HBM(a.shape, a.dtype) for a in flat],
                   *[pltpu.HBM((N_DEV,) + tuple(a.shape), a.dtype) for a in flat],
                   jax.ShapeDtypeStruct((8, LANES), F32)),
        in_specs=[_HBM] * (2 * n),
        out_specs=(*[_SEM] * len(sem_shapes), *[_HBM] * (2 * n), pl.BlockSpec(memory_space=pltpu.VMEM)),
        input_output_aliases={i: len(sem_shapes) + i for i in range(2 * n)},
        compiler_params=pltpu.CompilerParams(has_side_effects=_EFFECT),
    )(*srcs, *lands)
    handles, k, base = [], 0, len(sem_shapes)
    for gi, g in enumerate(groups):
        handles.append(dict(sems=out[4 * gi:4 * gi + 4], srcs=out[base + k:base + k + len(g)],
                            lands=out[base + n + k:base + n + k + len(g)]))
        k += len(g)
    return handles, out[-1]


def gather2_forward(handle, after, name, carry=()):
    lands, nc = list(handle["lands"]), len(carry)
    n, n_ici = len(lands), len(_ICI_RELATIONS)
    carried = [pltpu.with_memory_space_constraint(a, pltpu.HBM) for a in carry]

    def body(*refs):
        land_refs = refs[:n]
        recv_ici = refs[n + nc]
        fwd_send, fwd_recv = refs[n + nc + 2], refs[n + nc + 3]
        me, peer = _mesh_place()
        sibling, _ = peer(1)
        for j in range(n):
            for t, r in enumerate(_ICI_RELATIONS):
                dev, p = peer(r)
                landed = land_refs[j].at[p]
                _remote(landed, landed, fwd_send.at[n_ici * j + t], recv_ici.at[n_ici * j + t], dev).wait_recv()
                _remote(landed, landed, fwd_send.at[n_ici * j + t], fwd_recv.at[n_ici * j + t], sibling).start()

    out = pl.pallas_call(
        body, name=name,
        out_shape=(pltpu.SemaphoreType.DMA((n_ici * n,)), pltpu.SemaphoreType.DMA((n_ici * n,)),
                   *[pltpu.HBM(a.shape, a.dtype) for a in lands], *[pltpu.HBM(a.shape, a.dtype) for a in carry]),
        in_specs=[_HBM] * (n + nc) + [_SEM, pl.BlockSpec(memory_space=pl.ANY)],
        out_specs=(_SEM, _SEM, *[_HBM] * (n + nc)),
        input_output_aliases={i: 2 + i for i in range(n + nc)},
        compiler_params=pltpu.CompilerParams(has_side_effects=_EFFECT),
    )(*lands, *carried, handle["sems"][2], after)
    new_handle = dict(sems=handle["sems"], srcs=handle["srcs"], lands=out[2:2 + n], fwd=out[:2])
    return new_handle, list(out[2 + n:])


def gather2_wait(handle, after, name):
    srcs, lands = list(handle["srcs"]), list(handle["lands"])
    n, n_ici = len(srcs), len(_ICI_RELATIONS)
    n_rel = 1 + n_ici
    send_all, recv_sibling, _, local_all = handle["sems"]

    def body(*refs):
        src_refs, land_refs = refs[:n], refs[n:2 * n]
        send_sems, recv_sib, local_sems, fwd_send, fwd_recv = refs[2 * n:2 * n + 5]
        me, peer = _mesh_place()
        sibling, sib = peer(1)
        for j in range(n):
            pltpu.make_async_copy(src_refs[j], land_refs[j].at[me], local_sems.at[j]).wait()
            _remote(src_refs[j], land_refs[j].at[sib], send_sems.at[n_rel * j], recv_sib.at[j], sibling).wait()
            for t, r in enumerate(_ICI_RELATIONS):
                dev, p = peer(r)
                _remote(src_refs[j], land_refs[j].at[me], send_sems.at[n_rel * j + 1 + t],
                        recv_sib.at[j], dev).wait_send()
                _, p_sib = peer(r ^ 1)
                _remote(land_refs[j].at[p], land_refs[j].at[p_sib], fwd_send.at[n_ici * j + t],
                        fwd_recv.at[n_ici * j + t], sibling).wait()

    out = pl.pallas_call(
        body, name=name,
        out_shape=(*[pltpu.HBM(a.shape, a.dtype) for a in srcs], *[pltpu.HBM(a.shape, a.dtype) for a in lands]),
        in_specs=[_HBM] * (2 * n) + [_SEM] * 5 + [pl.BlockSpec(memory_space=pl.ANY)],
        out_specs=tuple([_HBM] * (2 * n)),
        input_output_aliases={i: i for i in range(2 * n)},
        compiler_params=pltpu.CompilerParams(has_side_effects=_EFFECT),
    )(*srcs, *lands, send_all, recv_sibling, local_all, *handle["fwd"], after)
    return list(out[n:])


BIG = ["mla_w_in", "mla_w_uq", "mla_w_ukv", "mla_w_o", "gdn_w_in", "gdn_w_o", "sc_w_in", "sc_w_o",
       "xa_w_q", "xa_w_kv", "xa_w_o", "mlp_w1", "mlp_w2"]
TINY = [("mla_q_norm", 1), ("mla_kv_norm", 1), ("gdn_conv_w", 2), ("sc_conv_w", 2)]
REPL = ["gdn_a_log", "gdn_dt_bias", "gdn_o_norm", "norm_mix", "norm_mem", "norm_mlp", "mem_norm", "final_norm"]
WEIGHTS = ["mla_w_in", "mla_q_norm", "mla_kv_norm", "mla_w_uq", "mla_w_ukv", "mla_w_o", "gdn_w_in",
           "gdn_conv_w", "gdn_a_log", "gdn_dt_bias", "gdn_o_norm", "gdn_w_o", "sc_w_in", "sc_conv_w",
           "sc_w_o", "norm_mix", "norm_mem", "norm_mlp", "xa_w_q", "xa_w_kv", "xa_w_o", "mlp_w1",
           "mlp_w2", "mem_norm", "final_norm"]
MIXER_WEIGHTS = (["mla_w_in", "mla_w_uq", "mla_w_ukv", "mla_w_o"], ["gdn_w_in", "gdn_w_o"], ["sc_w_in", "sc_w_o"])
MIXER_PARAMS = (["norm_mem", "mla_q_norm", "mla_kv_norm"],
                ["norm_mem", "gdn_conv_w", "gdn_a_log", "gdn_dt_bias", "gdn_o_norm"],
                ["norm_mem", "sc_conv_w"])


def from_shards(a8, axis):
    a = jnp.moveaxis(a8, 0, axis)
    shp = a.shape
    return a.reshape(shp[:axis] + (shp[axis] * shp[axis + 1],) + shp[axis + 2:])


def pack_rows(flat_list, width, row_mult):
    total = sum(a.shape[-1] for a in flat_list)
    rows = -(-total // width)
    rows = -(-rows // row_mult) * row_mult
    pad = rows * width - total
    parts = list(flat_list)
    if pad:
        parts.append(jnp.zeros((pad,), flat_list[0].dtype))
    return jnp.concatenate(parts, axis=-1).reshape(rows, width)


def unpack_rows(packed, shapes):
    lead = packed.shape[:-2]
    flat = packed.reshape(lead + (-1,))
    out, off = [], 0
    for shp in shapes:
        n = math.prod(shp)
        out.append(flat[..., off:off + n].reshape(lead + tuple(shp)))
        off += n
    return out


def _swap_halves(w):
    half = w.shape[-1] // 2
    return jnp.concatenate([w[..., half:], w[..., :half]], axis=-1)


def _pad_last(w, n):
    return jnp.pad(w, [(0, 0)] * (w.ndim - 1) + [(0, n - w.shape[-1])])


def _unblock(w8):
    return jnp.transpose(w8, (1, 0, 2)).reshape(w8.shape[1], -1)


def _stack_rows(w8):
    return w8.reshape(-1, w8.shape[-1])


def rms_op(name, rows, d, out_dtype, tm=512):
    return make_tile_op(fn_rms, name, ["row", "par"], [True, True], [("row", d, out_dtype)], rows, min(tm, rows))


def seg_memory(p, mem):
    return rms_op("rms_memory", mem.shape[0], mem.shape[1], BF16)(mem, p["mem_norm"].reshape(1, -1))[0]


def seg_mixer(i, wts, p, x, h, rope_c, rope_s):
    s, d = x.shape
    j, kind = i // N_MIXERS, i % N_MIXERS
    tag = f"l{i}"
    hd = MLA_NOPE
    next_gain = p["norm_mem"][i].reshape(1, d)
    if kind == 0:
        w_in = _stack_rows(wts["mla_w_in"])
        w_cq = w_in[:, :MLA_Q_RANK]
        w_ckv = w_in[:, MLA_Q_RANK:MLA_Q_RANK + MLA_KV_RANK]
        w_kr = w_in[:, MLA_Q_RANK + MLA_KV_RANK:]
        w_z = jnp.concatenate([w_cq, w_ckv, _pad_last(w_kr, hd), _pad_last(_swap_halves(w_kr), hd)], axis=-1)
        z = make_mm(tag + "_mla_in", F32)(h, w_z)
        c_q, c_kv = z[:, :MLA_Q_RANK], z[:, MLA_Q_RANK:MLA_Q_RANK + MLA_KV_RANK]
        kr_raw, kr_swp = z[:, -2 * hd:-hd], z[:, -hd:]
        c_qn = rms_op(tag + "_rms_q", s, MLA_Q_RANK, BF16)(c_q, p["mla_q_norm"][j].reshape(1, -1))[0]
        c_kvn = rms_op(tag + "_rms_kv", s, MLA_KV_RANK, BF16)(c_kv, p["mla_kv_norm"][j].reshape(1, -1))[0]
        w_uq8 = wts["mla_w_uq"]
        w_qn = _unblock(w_uq8[:, :, :MLA_NOPE])
        w_qr = w_uq8[:, :, MLA_NOPE:]
        w_qr_p = _unblock(_pad_last(w_qr, hd))
        w_qr_s = _unblock(_pad_last(_swap_halves(w_qr), hd))
        q_nope = make_mm(tag + "_mla_uq_n", BF16)(c_qn, w_qn)
        q_raw = make_mm(tag + "_mla_uq_r", F32)(c_qn, w_qr_p)
        q_swp = make_mm(tag + "_mla_uq_s", F32)(c_qn, w_qr_s)
        nq = MLA_HEADS * hd
        kv = make_mm(tag + "_mla_ukv", BF16, blocked=True)(c_kvn, wts["mla_w_ukv"])
        heads_row = ("row", hd, MLA_HEADS)
        q_rope = make_tile_op(fn_rope, tag + "_rope_q", [heads_row, heads_row, "row", "row"],
                              [True, True, False, False], [(heads_row, nq, F32)], s, 512)(
            q_raw, q_swp, rope_c, rope_s)[0]
        k_rope = make_tile_op(fn_rope, tag + "_rope_k", ["row", "row", "row", "row"],
                              [True, True, False, False], [("row", hd, F32)], s, 1024)(
            kr_raw, kr_swp, rope_c, rope_s)[0]
        n_groups = MLA_QUERY_GROUPS if s % (MLA_QUERY_GROUPS * 256) == 0 else 1
        rows_g = s // n_groups
        o_groups = []
        for grp in range(n_groups):
            r0, r1 = grp * rows_g, (grp + 1) * rows_g
            o_groups.append(make_tile_op(
                fn_mla_attn, f"{tag}_mla_attn{grp}", [("rowh", hd), ("rowh", hd), ("parh", hd, 2), "par"],
                [True] * 4, [(("rowh", hd), nq, BF16)], rows_g, 256, MLA_HEADS, row_base=r0)(
                q_nope[r0:r1], q_rope[r0:r1], kv[:r1], k_rope[:r1])[0])
        o = jnp.concatenate(o_groups, axis=0)
        return make_mm_res_rms(tag + "_mla_o")(x, o, _stack_rows(wts["mla_w_o"]), next_gain)
    if kind == 1:
        ng = GDN_HEADS * GDN_DK
        w_in = _unblock(wts["gdn_w_in"])
        cw = p["gdn_conv_w"][j]
        conv_out = []
        for part, nm in enumerate(("q", "k", "v")):
            cols = slice(part * ng, (part + 1) * ng)
            pre = make_mm(f"{tag}_gdn_in_{nm}", F32)(h, w_in[:, cols])
            conv_out.append(make_conv(f"{tag}_gdn_conv_{nm}")(pre, cw[:, cols]))
        gate = make_mm(tag + "_gdn_in_g", F32)(h, w_in[:, 3 * ng:4 * ng])
        ba = make_mm(tag + "_gdn_in_ba", F32)(h, _pad_last(w_in[:, 4 * ng:], LANES))
        heads_row = ("row", GDN_DK, GDN_HEADS)
        q, k, v = make_tile_op(fn_gdn_prep, tag + "_gdn_prep", [heads_row] * 3, [True] * 3,
                               [(heads_row, ng, F32)] * 3, s, 512)(*conv_out)
        alog = jnp.pad(p["gdn_a_log"][j].reshape(1, -1), ((0, 0), (GDN_HEADS, LANES - 2 * GDN_HEADS)))
        dtb = jnp.pad(p["gdn_dt_bias"][j].reshape(1, -1), ((0, 0), (GDN_HEADS, LANES - 2 * GDN_HEADS)))
        beta_b, g_b = make_tile_op(fn_gdn_gates, tag + "_gdn_gates", ["row", "par", "par"], [True] * 3,
                                   [("row", ng, F32)] * 2, s, 512)(ba, alog, dtb)
        o = make_gdn(tag + "_gdn_core")(q, k, v, g_b, beta_b)
        o = make_tile_op(fn_gdn_out, tag + "_gdn_out", [heads_row, heads_row, "par"],
                         [True] * 3, [(heads_row, ng, BF16)], s, 512)(
            o, gate, p["gdn_o_norm"][j].reshape(1, -1))[0]
        return make_mm_res_rms(tag + "_gdn_o")(x, o, _stack_rows(wts["gdn_w_o"]), next_gain)
    w_in = _unblock(wts["sc_w_in"])
    b_gate = make_mm(tag + "_sc_in_b", F32)(h, w_in[:, :d])
    c_gate = make_mm(tag + "_sc_in_c", F32)(h, w_in[:, d:2 * d])
    u = make_mm(tag + "_sc_in_u", F32)(h, w_in[:, 2 * d:])
    cu = make_tile_op(fn_mul, tag + "_sc_cu", ["row", "row"], [True, True], [("row", d, F32)], s, 512)(
        c_gate, u)[0]
    cv = make_conv(tag + "_sc_conv")(cu, p["sc_conv_w"][j])
    yv = make_tile_op(fn_mul, tag + "_sc_gate", ["row", "row"], [True, True], [("row", d, BF16)], s, 512)(
        b_gate, cv)[0]
    return make_mm_res_rms(tag + "_sc_o")(x, yv, _stack_rows(wts["sc_w_o"]), next_gain)


def seg_xattn(i, wts, p, x, hx, mem_n):
    s, d = x.shape
    tag = f"l{i}"
    q = make_mm(tag + "_xa_q", BF16)(hx, _stack_rows(wts["xa_w_q"]))
    kv = make_mm(tag + "_xa_kv", BF16, blocked=True)(mem_n, wts["xa_w_kv"])
    o = make_tile_op(fn_xattn, tag + "_xattn",
                     [("rowh", X_HEAD_DIM), ("parh", X_HEAD_DIM), ("parh", X_HEAD_DIM)], [True] * 3,
                     [(("rowh", X_HEAD_DIM), d, BF16)], s, 1024, X_HEADS)(q, kv[:, :d], kv[:, d:])[0]
    return make_mm_res_rms(tag + "_xa_o")(x, o, _stack_rows(wts["xa_w_o"]), p["norm_mlp"][i].reshape(1, d))


def seg_mlp(i, wts, p, x, hm):
    d = x.shape[1]
    gain = p["norm_mix"][i + 1].reshape(1, d) if i + 1 < DEPTH else None
    return make_mlp(f"l{i}_mlp", gain is not None)(x, hm, wts["mlp_w1"], _stack_rows(wts["mlp_w2"]), gain)


def segments():
    segs = []
    for i in range(DEPTH):
        j, kind = i // N_MIXERS, i % N_MIXERS
        segs.append((f"l{i}_mixer", [(n, j) for n in MIXER_WEIGHTS[kind]], MIXER_PARAMS[kind], "mixer"))
        segs.append((f"l{i}_xattn", [(n, i) for n in ("xa_w_q", "xa_w_kv", "xa_w_o")], ["norm_mlp"], "xattn"))
        segs.append((f"l{i}_mlp", [(n, i) for n in ("mlp_w1", "mlp_w2")], ["norm_mix"] if i + 1 < DEPTH else [],
                     "mlp"))
    return segs


def run_segment(index, kind, wts, p, x, h, mem_n, rope_c, rope_s):
    layer = index // 3
    if kind == "mixer":
        return seg_mixer(layer, wts, p, x, h, rope_c, rope_s)
    if kind == "xattn":
        return seg_xattn(layer, wts, p, x, h, mem_n)
    return seg_mlp(layer, wts, p, x, h)


def rope_tables(positions):
    inv_freq = ROPE_THETA ** (-jnp.arange(0, MLA_ROPE, 2, dtype=F32) / MLA_ROPE)
    ang = positions.astype(F32)[:, None] * inv_freq
    cos, sin = jnp.cos(ang), jnp.sin(ang)
    zeros = jnp.zeros((positions.shape[0], MLA_NOPE - MLA_ROPE), F32)
    return jnp.concatenate([cos, cos, zeros], axis=-1), jnp.concatenate([-sin, sin, zeros], axis=-1)


def kernel(x, mem, positions, mla_w_in, mla_q_norm, mla_kv_norm, mla_w_uq, mla_w_ukv, mla_w_o, gdn_w_in, gdn_conv_w, gdn_a_log, gdn_dt_bias, gdn_o_norm, gdn_w_o, sc_w_in, sc_conv_w, sc_w_o, norm_mix, norm_mem, norm_mlp, xa_w_q, xa_w_kv, xa_w_o, mlp_w1, mlp_w2, mem_norm, final_norm, loss_target, m_mla_w_in, m_mla_q_norm, m_mla_kv_norm, m_mla_w_uq, m_mla_w_ukv, m_mla_w_o, m_gdn_w_in, m_gdn_conv_w, m_gdn_a_log, m_gdn_dt_bias, m_gdn_o_norm, m_gdn_w_o, m_sc_w_in, m_sc_conv_w, m_sc_w_o, m_norm_mix, m_norm_mem, m_norm_mlp, m_xa_w_q, m_xa_w_kv, m_xa_w_o, m_mlp_w1, m_mlp_w2, m_mem_norm, m_final_norm, v_mla_w_in, v_mla_q_norm, v_mla_kv_norm, v_mla_w_uq, v_mla_w_ukv, v_mla_w_o, v_gdn_w_in, v_gdn_conv_w, v_gdn_a_log, v_gdn_dt_bias, v_gdn_o_norm, v_gdn_w_o, v_sc_w_in, v_sc_conv_w, v_sc_w_o, v_norm_mix, v_norm_mem, v_norm_mlp, v_xa_w_q, v_xa_w_kv, v_xa_w_o, v_mlp_w1, v_mlp_w2, v_mem_norm, v_final_norm):
    args = locals()
    w_loc = {n: args[n] for n in WEIGHTS}
    m_loc = {n: args["m_" + n] for n in WEIGHTS}
    v_loc = {n: args["v_" + n] for n in WEIGHTS}
    me = 4 * lax.axis_index("x") + 2 * lax.axis_index("y") + lax.axis_index("c")
    segs = segments()

    w16 = {n: w_loc[n].astype(BF16) for n in BIG}
    tiny_pack = pack_rows([w_loc[n].reshape(-1) for n, _ in TINY], LANES, 8)
    gather_handles, token = gather2_start(
        [[tiny_pack]] + [[w16[n][layer] for n, layer in units] for _, units, _, _ in segs], "gather_start")

    x_cur = x[0]
    rope_c, rope_s = rope_tables(positions[0])
    tiny_handle, _ = gather2_forward(gather_handles[0], token, "gather_forward_tiny")
    tiny_all = gather2_wait(tiny_handle, token, "gather_wait_tiny")[0]
    gather_handles = gather_handles[1:]
    params = {}
    for (n, ax), a8 in zip(TINY, unpack_rows(tiny_all, [w_loc[n].shape for n, _ in TINY])):
        params[n] = from_shards(a8, ax)
    for n in REPL:
        params[n] = w_loc[n]

    mem_n, vjp_memory = jax.vjp(lambda p_: seg_memory(p_, mem[0]), {"mem_norm": params["mem_norm"]})
    h_cur, vjp_first_norm = jax.vjp(
        lambda p_, x_: rms_op("l0_rms_mix", x_.shape[0], x_.shape[1], BF16)(x_, p_["norm_mix"][0].reshape(1, -1))[0],
        {"norm_mix": params["norm_mix"]}, x_cur)
    vjps = []
    forwarded, _ = gather2_forward(gather_handles[0], token, f"gather_forward_{segs[0][0]}")
    for index, (tag, units, p_names, kind) in enumerate(segs):
        landed = gather2_wait(forwarded, token if index == 0 else x_cur, f"gather_wait_{tag}")
        wts = {n: a for (n, _), a in zip(units, landed)}
        p_seg = {n: params[n] for n in p_names}
        if index + 1 < len(segs):
            forwarded, (p_seg[p_names[0]],) = gather2_forward(
                gather_handles[index + 1], landed[0], f"gather_forward_{segs[index + 1][0]}",
                carry=[p_seg[p_names[0]]])
        outs, vjp_seg = jax.vjp(
            lambda w_, p_, x_, h_, m_, index=index, kind=kind:
            run_segment(index, kind, w_, p_, x_, h_, m_, rope_c, rope_s),
            wts, p_seg, x_cur, h_cur, mem_n)
        x_cur, h_cur = outs[0], (outs[1] if len(outs) > 1 else None)
        vjps.append(vjp_seg)

    loss_vec, g_x, d_final = loss_head(x_cur, params["final_norm"].reshape(1, -1), loss_target[0], "loss_head")

    grads = {n: jnp.zeros_like(params[n]) for n in params}
    grads["final_norm"] = d_final.reshape(-1)
    g_mem_n = jnp.zeros_like(mem_n)
    g_h = None
    scatter_handles = []
    for (tag, units, _, _), vjp_seg in zip(reversed(segs), reversed(vjps)):
        g_wts, g_p, g_x, g_h, g_m = vjp_seg((g_x,) if g_h is None else (g_x, g_h))
        for n, g in g_p.items():
            grads[n] = grads[n] + g
        g_mem_n = g_mem_n + g_m
        handle, _, (g_h,) = exchange_start("scatter", [g_wts[n] for n, _ in units], f"scatter_start_{tag}",
                                           carry=[g_h])
        scatter_handles.append((units, handle))
    grads["mem_norm"] = grads["mem_norm"] + vjp_memory(g_mem_n)[0]["mem_norm"]
    g_first, g_x_norm = vjp_first_norm(g_h)
    grads["norm_mix"] = grads["norm_mix"] + g_first["norm_mix"]
    g_x = g_x + g_x_norm

    small_names = [n for n, _ in TINY] + REPL
    small_g = pack_rows([loss_vec[0, :1]] + [grads[n].astype(F32).reshape(-1) for n in small_names], PACK_W, 8)
    small_handle, _, _ = exchange_start("gather", [small_g], "gather_start_small_grads")

    g_recv = {}
    for units, handle in scatter_handles:
        landed = exchange_wait(handle, g_x, f"scatter_wait_{units[0][0]}_{units[0][1]}")
        g_recv.update(dict(zip(units, landed)))

    res = {}
    for n in BIG:
        outs = None
        for layer in range(w_loc[n].shape[0]):
            outs = adamw(g_recv[n, layer], w_loc[n], m_loc[n], v_loc[n], layer, outs, f"adamw_{n}_{layer}")
        for kind, a in zip(("grad", "delta", "m", "v"), outs):
            res[(kind, n)] = a
    small_recv = exchange_wait(small_handle, res[("grad", BIG[-1])], "gather_wait_small_grads")[0]

    def full_small(d):
        parts = [jnp.zeros((1,), F32)]
        for n, ax in TINY:
            full_shape = params[n].shape
            start = [0] * len(full_shape)
            start[ax] = me * d[n].shape[ax]
            parts.append(lax.dynamic_update_slice(jnp.zeros(full_shape, F32), d[n], start).reshape(-1))
        parts += [d[n].reshape(-1) for n in REPL]
        return pack_rows(parts, PACK_W, 8)

    outs_small = adamw(small_recv, full_small(w_loc)[None], full_small(m_loc)[None], full_small(v_loc)[None],
                       0, None, "adamw_small")
    small_shapes = [(1,)] + [params[n].shape for n, _ in TINY] + [w_loc[n].shape for n in REPL]
    loss = None
    for kind, packed in zip(("grad", "delta", "m", "v"), outs_small):
        parts = unpack_rows(packed[0], small_shapes)
        if kind == "grad":
            loss = parts[0][0]
        for (n, ax), a in zip(TINY, parts[1:1 + len(TINY)]):
            start = [0] * a.ndim
            start[ax] = me * w_loc[n].shape[ax]
            res[(kind, n)] = lax.dynamic_slice(a, start, w_loc[n].shape)
        for n, a in zip(REPL, parts[1 + len(TINY):]):
            res[(kind, n)] = a

    out = [loss, g_x[None]]
    for kind in ("grad", "delta", "m", "v"):
        out += [res[(kind, n)] for n in WEIGHTS]
    return tuple(out)
```

```python
import functools
import math

import jax
import jax.numpy as jnp
from jax import lax
from jax.experimental import pallas as pl
from jax.experimental.pallas import tpu as pltpu

F32 = jnp.float32
BF16 = jnp.bfloat16

N_DEV = 8
LANES = 128
EPS = 1e-6
ROPE_THETA = 10000.0
MLA_HEADS, MLA_NOPE, MLA_ROPE, MLA_V = 8, 128, 64, 128
MLA_Q_RANK, MLA_KV_RANK = 384, 256
GDN_HEADS, GDN_DK, GDN_CONV, GDN_CHUNK = 8, 128, 4, 64
X_HEADS, X_HEAD_DIM = 4, 256
DEPTH, N_MIXERS = 4, 3
ADAM_LR, ADAM_B1, ADAM_B2, ADAM_EPS, ADAM_WD, ADAM_STEP = 0.001, 0.9, 0.999, 1e-08, 0.01, 10
MLA_QUERY_GROUPS = 4
NEG_BIG = -1e30
PACK_W = 1024


_NN = (((1,), (0,)), ((), ()))
_NT = (((1,), (1,)), ((), ()))
_TN = (((0,), (0,)), ((), ()))
_NN3 = (((2,), (1,)), ((0,), (0,)))
_NT3 = (((2,), (2,)), ((0,), (0,)))
_TN3 = (((1,), (1,)), ((0,), (0,)))


def _dot(a, b, dims):
    return lax.dot_general(a, b, dims, preferred_element_type=F32)


def _hi_lo(x):
    hi = x.astype(BF16)
    return hi, (x - hi.astype(F32)).astype(BF16)


def _split3(x):
    hi = x.astype(BF16)
    r = x - hi.astype(F32)
    mid = r.astype(BF16)
    return hi, mid, (r - mid.astype(F32)).astype(BF16)


def _dg(a, b, dims, prec):
    if prec == "h":
        return lax.dot_general(a, b, dims, precision=lax.Precision.HIGHEST, preferred_element_type=F32)
    if prec == "m":
        a_hi, a_lo = _hi_lo(a)
        b_hi, b_lo = _hi_lo(b)
        return _dot(a_hi, b_hi, dims) + _dot(a_hi, b_lo, dims) + _dot(a_lo, b_hi, dims)
    return _dot(a.astype(BF16), b.astype(BF16), dims)


def _dg_sel(sel, x, dims, sel_first):
    s16 = sel.astype(BF16)
    parts = [(_dot(s16, piece, dims) if sel_first else _dot(piece, s16, dims)) for piece in _split3(x)]
    return parts[0] + parts[1] + parts[2]


class _Ops:
    def __init__(self, prec, differentiable, batched=False):
        d_nn, d_nt, d_tn = (_NN3, _NT3, _TN3) if batched else (_NN, _NT, _TN)

        def nn(a, b):
            return _dg(a, b, d_nn, prec)

        def nt(a, b):
            return _dg(a, b, d_nt, prec)

        def tn(a, b):
            return _dg(a, b, d_tn, prec)

        if differentiable:
            dnn = jax.custom_vjp(nn)
            dnn.defvjp(lambda a, b: (nn(a, b), (a, b)), lambda r, g: (nt(g, r[1]), tn(r[0], g)))
            dnt = jax.custom_vjp(nt)
            dnt.defvjp(lambda a, b: (nt(a, b), (a, b)), lambda r, g: (nn(g, r[1]), tn(g, r[0])))
            dtn = jax.custom_vjp(tn)
            dtn.defvjp(lambda a, b: (tn(a, b), (a, b)), lambda r, g: (nt(r[1], g), nn(r[0], g)))
            nn, nt, tn = dnn, dnt, dtn
        self.nn, self.nt, self.tn = nn, nt, tn


class _SelOps:
    def __init__(self, differentiable, batched=False):
        d_nn, d_nt, d_tn = (_NN3, _NT3, _TN3) if batched else (_NN, _NT, _TN)

        def sel_nn(sel, x):
            return _dg_sel(sel, x, d_nn, True)

        def sel_nt(sel, x):
            return _dg_sel(sel, x, d_nt, True)

        if differentiable:
            dnn = jax.custom_vjp(sel_nn)
            dnn.defvjp(lambda s, x: (sel_nn(s, x), s),
                       lambda s, g: (jnp.zeros_like(s), _dg_sel(s, g, d_tn, True)))
            dnt = jax.custom_vjp(sel_nt)
            dnt.defvjp(lambda s, x: (sel_nt(s, x), s),
                       lambda s, g: (jnp.zeros_like(s), _dg_sel(s, g, d_tn, False)))
            sel_nn, sel_nt = dnn, dnt
        self.sel_nn, self.sel_nt = sel_nn, sel_nt


class _OpSet:
    def __init__(self, differentiable):
        self.b = _Ops("b", differentiable)
        self.h = _Ops("h", differentiable)
        self.bb = _Ops("b", differentiable, batched=True)
        self.bm = _Ops("m", differentiable, batched=True)
        self.bs = _SelOps(differentiable, batched=True)


_PLAIN = _OpSet(False)
_DIFF = _OpSet(True)


def _params(sem):
    return pltpu.CompilerParams(dimension_semantics=sem)


BLOCK_BYTES = 4 * 1024 * 1024


def _pick(n, cands):
    for c in cands:
        if n % c == 0:
            return c
    return n


def _tile(n, cap):
    if n <= cap:
        return n
    return _pick(n, tuple(c for c in (2048, 1024, 768, 512, 384, 256, 128) if c <= cap))


def matmul(a, b, form, out_dtype, name, res=None, blocked=False, relu_gate=None, rms_gain=None, relu2_out=False):
    if form == "nn":
        m, k = a.shape
        k2, n = (b.shape[1], N_DEV * b.shape[2]) if blocked else b.shape
    elif form == "nt":
        m, k = a.shape
        n, k2 = (b.shape[1], N_DEV * b.shape[2]) if blocked else b.shape
    else:
        (k, m), (k2, n) = a.shape, b.shape
    assert k == k2, (a.shape, b.shape, form)
    tk = k if k <= 2048 else _tile(k, 1024)
    cb = nb = 1
    if blocked:
        cb = (k if form == "nt" else n) // N_DEV
        nb = _pick(N_DEV, tuple(c for c in (8, 4, 2, 1) if c * cb <= 1024))
    if blocked and form == "nt":
        tk = nb * cb
    if blocked and form != "nt":
        tn = nb * cb
    else:
        tn = _tile(n, min(1024, BLOCK_BYTES // (tk * b.dtype.itemsize)))
    out_elems = BLOCK_BYTES // 2 if (out_dtype == BF16 and res is None) else BLOCK_BYTES // 4
    tm = _tile(m, min(BLOCK_BYTES // (tk * a.dtype.itemsize), out_elems // tn))
    nk = k // tk
    dims = {"nn": _NN, "nt": _NT, "tn": _TN}[form]

    a_spec = {"nn": pl.BlockSpec((tm, tk), lambda i, j, kk: (i, kk)),
              "nt": pl.BlockSpec((tm, tk), lambda i, j, kk: (i, kk)),
              "tn": pl.BlockSpec((tk, tm), lambda i, j, kk: (kk, i))}[form]
    if blocked and form == "nn":
        b_spec = pl.BlockSpec((nb, tk, cb), lambda i, j, kk: (j, kk, 0))
    elif blocked and form == "nt":
        b_spec = pl.BlockSpec((nb, tn, cb), lambda i, j, kk: (kk, j, 0))
    else:
        b_spec = {"nn": pl.BlockSpec((tk, tn), lambda i, j, kk: (kk, j)),
                  "nt": pl.BlockSpec((tn, tk), lambda i, j, kk: (j, kk)),
                  "tn": pl.BlockSpec((tk, tn), lambda i, j, kk: (kk, j))}[form]
    c_spec = pl.BlockSpec((tm, tn), lambda i, j, kk: (i, j))
    out_shape = jax.ShapeDtypeStruct((m, n), out_dtype)
    o_spec = c_spec
    blocked_out = blocked and form == "tn"
    if blocked_out:
        out_shape = jax.ShapeDtypeStruct((N_DEV, m, cb), out_dtype)
        o_spec = pl.BlockSpec((nb, tm, cb), lambda i, j, kk: (j, i, 0))
    has_res, has_gate, has_gain = res is not None, relu_gate is not None, rms_gain is not None
    extras = [e for e in (res, relu_gate) if e is not None]
    n_in = 2 + len(extras) + has_gain
    second = has_gain or relu2_out
    assert not (second and (blocked_out or tn != n and has_gain))

    def body(*refs):
        a_ref, b_ref = refs[0], refs[1]
        r_ref = refs[2] if has_res else None
        gate_ref = refs[2 + has_res] if has_gate else None
        gain_ref = refs[n_in - 1] if has_gain else None
        o_ref = refs[n_in]
        a_val = a_ref[...].astype(BF16)
        if blocked and form == "nn":
            part = jnp.concatenate([_dot(a_val, b_ref[t].astype(BF16), dims) for t in range(nb)], axis=-1)
        elif blocked and form == "nt":
            part = _dot(a_val[:, :cb], b_ref[0].astype(BF16), dims)
            for t in range(1, nb):
                part = part + _dot(a_val[:, t * cb:(t + 1) * cb], b_ref[t].astype(BF16), dims)
        else:
            part = _dot(a_val, b_ref[...].astype(BF16), dims)

        def finish(acc):
            if has_res:
                acc = acc + r_ref[...].astype(F32)
            if has_gate:
                acc = acc * (2.0 * jnp.maximum(gate_ref[...].astype(F32), 0.0))
            if blocked_out:
                for t in range(nb):
                    o_ref[t] = acc[:, t * cb:(t + 1) * cb].astype(out_dtype)
            else:
                o_ref[...] = acc.astype(out_dtype)
            if has_gain:
                refs[n_in + 1][...] = _rms(acc, gain_ref[...]).astype(BF16)
            if relu2_out:
                r = jnp.maximum(acc.astype(out_dtype).astype(F32), 0.0)
                refs[n_in + 1][...] = (r * r).astype(BF16)

        if nk == 1:
            finish(part)
        else:
            acc_ref = refs[-1]
            kk = pl.program_id(2)

            @pl.when(kk == 0)
            def _():
                acc_ref[...] = part

            @pl.when(jnp.logical_and(kk > 0, kk < nk - 1))
            def _():
                acc_ref[...] += part

            @pl.when(kk == nk - 1)
            def _():
                finish(acc_ref[...] + part)

    in_specs = [a_spec, b_spec] + [c_spec] * len(extras)
    args = [a, b] + extras
    if has_gain:
        in_specs.append(pl.BlockSpec((1, tn), lambda i, j, kk: (0, j)))
        args.append(rms_gain)
    if second:
        out_shape = [out_shape, jax.ShapeDtypeStruct((m, n), BF16)]
        o_spec = [o_spec, c_spec]
    return pl.pallas_call(
        body, name=name,
        out_shape=out_shape,
        grid=(m // tm, n // tn, nk),
        in_specs=in_specs, out_specs=o_spec,
        scratch_shapes=[pltpu.VMEM((tm, tn), F32)] if nk > 1 else [],
        compiler_params=_params(("parallel", "parallel", "arbitrary")),
    )(*args)


def make_mm(name, out_dtype, with_res=False, blocked=False):
    def bwd_mm(a, w, g):
        da = matmul(g, w, "nt", a.dtype, name + "_da", blocked=blocked)
        dw = matmul(a, g, "tn", w.dtype, name + "_dw", blocked=blocked)
        return da, dw

    if with_res:
        @jax.custom_vjp
        def op(res, a, w):
            return matmul(a, w, "nn", out_dtype, name + "_f", res=res, blocked=blocked)

        def fwd(res, a, w):
            return op(res, a, w), (a, w)

        def bwd(saved, g):
            return (g,) + bwd_mm(*saved, g)
    else:
        @jax.custom_vjp
        def op(a, w):
            return matmul(a, w, "nn", out_dtype, name + "_f", blocked=blocked)

        def fwd(a, w):
            return op(a, w), (a, w)

        def bwd(saved, g):
            return bwd_mm(*saved, g)
    op.defvjp(fwd, bwd)
    return op


def _rms_fan_bwd(x_new, gain, dx, dh, name):
    rows, d = x_new.shape
    outs = [("row", d, F32), ("row", d, BF16)]
    return tile_bwd(fn_fan_rms, name, ["row", "par"], [x_new, gain], [True, True], outs, [dx, dh],
                    rows, min(512, rows), 0)


def make_mm_res_rms(name):
    @jax.custom_vjp
    def op(res, a, w, gain):
        return tuple(matmul(a, w, "nn", F32, name + "_f", res=res, rms_gain=gain))

    def fwd(res, a, w, gain):
        x_new, h = op(res, a, w, gain)
        return (x_new, h), (a, w, x_new, gain)

    def bwd(saved, cts):
        a, w, x_new, gain = saved
        dx, dgain = _rms_fan_bwd(x_new, gain, cts[0], cts[1], name + "_nb")
        da = matmul(dx, w, "nt", a.dtype, name + "_da")
        dw = matmul(a, dx, "tn", w.dtype, name + "_dw")
        return dx, da, dw, dgain

    op.defvjp(fwd, bwd)
    return op


def make_mlp(name, with_norm):
    def run(x, h, w1, w2, gain):
        a, bsq = matmul(h, w1, "nn", BF16, name + "_1_f", blocked=True, relu2_out=True)
        out = matmul(bsq, w2, "nn", F32, name + "_2_f", res=x, rms_gain=gain if with_norm else None)
        return (tuple(out) if with_norm else (out,)), a, bsq

    @jax.custom_vjp
    def op(x, h, w1, w2, gain):
        return run(x, h, w1, w2, gain)[0]

    def fwd(x, h, w1, w2, gain):
        out, a, bsq = run(x, h, w1, w2, gain)
        return out, (h, w1, w2, gain, a, bsq, out[0])

    def bwd(saved, cts):
        h, w1, w2, gain, a, bsq, x_new = saved
        if with_norm:
            dx, dgain = _rms_fan_bwd(x_new, gain, cts[0], cts[1], name + "_nb")
        else:
            dx, dgain = cts[0], None
        da = matmul(dx, w2, "nt", BF16, name + "_2_da", relu_gate=a)
        dw2 = matmul(bsq, dx, "tn", w2.dtype, name + "_2_dw")
        dh = matmul(da, w1, "nt", h.dtype, name + "_1_da", blocked=True)
        dw1 = matmul(h, da, "tn", w1.dtype, name + "_1_dw", blocked=True)
        return dx, dh, dw1, dw2, dgain

    op.defvjp(fwd, bwd)
    return op


def _kind(k):
    if isinstance(k, str):
        return k, None, 1
    return k[0], k[1], (k[2] if len(k) > 2 else 1)


def _tile_spec(kind, shape, tm, heads):
    k, d, ns = _kind(kind)
    if k == "row":
        return pl.BlockSpec((tm, shape[1]), (lambda h, i: (i, 0)) if heads else (lambda i: (i, 0)))
    if k == "par":
        return pl.BlockSpec(tuple(shape), (lambda h, i: (0, 0)) if heads else (lambda i: (0, 0)))
    if k == "rowh":
        return pl.BlockSpec((tm, d * ns), lambda h, i: (i, h))
    if k == "parh":
        return pl.BlockSpec((shape[0], d * ns), lambda h, i: (0, h))
    raise ValueError(kind)


def _tile_grid(rows, tm, heads):
    n_rows = rows // tm
    return ((heads, n_rows) if heads else (n_rows,)), (1 if heads else 0)


def _split_vals(kinds, refs):
    vals, counts = [], []
    for kind, r in zip(kinds, refs):
        _, d, ns = _kind(kind)
        v = r[...].astype(F32)
        vals += [v] if ns == 1 else [v[:, p * d:(p + 1) * d] for p in range(ns)]
        counts.append(ns)
    return vals, counts


def tile_fwd(fn, name, kinds, args, outs, rows, tm, heads, row_base=0):
    grid, row_axis = _tile_grid(rows, tm, heads)
    n_in = len(args)
    out_shapes = [jax.ShapeDtypeStruct((rows, w), dt) for (_, w, dt) in outs]

    def body(*refs):
        vals, _ = _split_vals(kinds, refs[:n_in])
        row0 = row_base + pl.program_id(row_axis) * tm
        res = list(fn(_PLAIN, row0, *vals))
        for o_ref, (k, _, _) in zip(refs[n_in:], outs):
            pieces = [res.pop(0) for _ in range(_kind(k)[2])]
            v = pieces[0] if len(pieces) == 1 else jnp.concatenate(pieces, axis=-1)
            o_ref[...] = v.astype(o_ref.dtype)

    return pl.pallas_call(
        body, name=name, out_shape=out_shapes, grid=grid,
        in_specs=[_tile_spec(k, a.shape, tm, heads) for k, a in zip(kinds, args)],
        out_specs=[_tile_spec(k, (rows, w), tm, heads) for (k, w, _) in outs],
        compiler_params=_params(("arbitrary",) * len(grid)),
    )(*args)


def tile_bwd(fn, name, kinds, args, diff, outs, cts, rows, tm, heads, row_base=0):
    grid, row_axis = _tile_grid(rows, tm, heads)
    n_in, n_ct = len(args), len(cts)
    diff_idx = [i for i, d in enumerate(diff) if d]
    g_shapes, g_specs = [], []
    for i in diff_idx:
        k = _kind(kinds[i])[0]
        dt = args[i].dtype if k in ("row", "rowh") else F32
        g_shapes.append(jax.ShapeDtypeStruct(args[i].shape, dt))
        g_specs.append(_tile_spec(kinds[i], args[i].shape, tm, heads))

    def body(*refs):
        in_refs, ct_refs, g_refs = refs[:n_in], refs[n_in:n_in + n_ct], refs[n_in + n_ct:]
        vals, counts = _split_vals(kinds, in_refs)
        first_piece = [sum(counts[:i]) for i in range(n_in)]
        flat_diff = [first_piece[i] + p for i in diff_idx for p in range(counts[i])]
        row_id = pl.program_id(row_axis)
        row0 = row_base + row_id * tm

        def f(*dvals):
            full = list(vals)
            for i, dv in zip(flat_diff, dvals):
                full[i] = dv
            return tuple(fn(_DIFF, row0, *full))

        _, vjp = jax.vjp(f, *[vals[i] for i in flat_diff])
        ct_vals, _ = _split_vals([k for (k, _, _) in outs], ct_refs)
        flat_grads = list(vjp(tuple(ct_vals)))
        for g_ref, i in zip(g_refs, diff_idx):
            pieces = [flat_grads.pop(0) for _ in range(counts[i])]
            g = pieces[0] if len(pieces) == 1 else jnp.concatenate(pieces, axis=-1)
            k = _kind(kinds[i])[0]
            if k in ("row", "rowh"):
                g_ref[...] = g.astype(g_ref.dtype)
            else:
                first = row_id == 0
                if heads and k == "par":
                    first = jnp.logical_and(first, pl.program_id(0) == 0)

                @pl.when(first)
                def _(g_ref=g_ref, g=g):
                    g_ref[...] = g

                @pl.when(jnp.logical_not(first))
                def _(g_ref=g_ref, g=g):
                    g_ref[...] += g

    return pl.pallas_call(
        body, name=name, out_shape=g_shapes, grid=grid,
        in_specs=[_tile_spec(k, a.shape, tm, heads) for k, a in zip(kinds, args)]
        + [_tile_spec(k, (rows, w), tm, heads) for (k, w, _) in outs],
        out_specs=g_specs,
        compiler_params=_params(("arbitrary",) * len(grid)),
    )(*args, *cts)


def make_tile_op(fn, name, kinds, diff, outs, rows, tm, heads=0, row_base=0):
    tm = min(tm, rows)

    @jax.custom_vjp
    def op(*args):
        return tuple(tile_fwd(fn, name + "_f", kinds, args, outs, rows, tm, heads, row_base))

    def fwd(*args):
        return op(*args), args

    def bwd(args, cts):
        grads = tile_bwd(fn, name + "_b", kinds, args, diff, outs, cts, rows, tm, heads, row_base)
        it = iter(grads)
        res = []
        for a, d in zip(args, diff):
            res.append(next(it).astype(a.dtype) if d else None)
        return tuple(res)

    op.defvjp(fwd, bwd)
    return op


def _rms(x, g):
    return x * lax.rsqrt(jnp.mean(x * x, axis=-1, keepdims=True) + EPS) * g


def fn_rms(ops, row0, x, g):
    return (_rms(x, g),)


def fn_fan_rms(ops, row0, x, g):
    return x, _rms(x, g)


def fn_mul(ops, row0, a, b):
    return (a * b,)


def fn_rope(ops, row0, *t):
    nh = (len(t) - 2) // 2
    c, s = t[-2], t[-1]
    return tuple(x * c + xs * s for x, xs in zip(t[:nh], t[nh:2 * nh]))


def _softmax(s):
    m = lax.stop_gradient(jnp.max(s, axis=-1, keepdims=True))
    e = jnp.exp(s - m)
    return e / jnp.sum(e, axis=-1, keepdims=True)


def fn_xattn(ops, row0, q, k, v):
    s = ops.b.nt(q, k) * (X_HEAD_DIM ** -0.5)
    return (ops.b.nn(_softmax(s), v),)


def _silu(x):
    return x * jax.nn.sigmoid(x)


def fn_gdn_prep(ops, row0, *t):
    nh = len(t) // 3
    qs, ks, vs = [], [], []
    for qc, kc, vc in zip(t[:nh], t[nh:2 * nh], t[2 * nh:]):
        q, k = _silu(qc), _silu(kc)
        qs.append(q * lax.rsqrt(jnp.sum(q * q, -1, keepdims=True) + EPS) * (GDN_DK ** -0.5))
        ks.append(k * lax.rsqrt(jnp.sum(k * k, -1, keepdims=True) + EPS))
        vs.append(_silu(vc))
    return tuple(qs + ks + vs)


def fn_gdn_gates(ops, row0, ba, alog, dtb):
    width = GDN_HEADS * GDN_DK
    beta = jax.nn.sigmoid(ba)
    z = ba + dtb
    softplus = jnp.maximum(z, 0.0) + jnp.log1p(jnp.exp(-jnp.abs(z)))
    g = -jnp.exp(alog) * softplus
    r = lax.broadcasted_iota(jnp.int32, (LANES, width), 0)
    c = lax.broadcasted_iota(jnp.int32, (LANES, width), 1) // GDN_DK
    e_beta = (r == c).astype(F32)
    e_g = (r == c + GDN_HEADS).astype(F32)
    return ops.h.nn(beta, e_beta), ops.h.nn(g, e_g)


def fn_gdn_out(ops, row0, *t):
    nh = (len(t) - 1) // 2
    g = t[-1]
    return tuple(_rms(o, g) * _silu(gate) for o, gate in zip(t[:nh], t[nh:2 * nh]))


def fn_mla_attn(ops, row0, qn, qr, kn, v, kr):
    s = (ops.b.nt(qn, kn) + ops.b.nt(qr, kr)) * ((MLA_NOPE + MLA_ROPE) ** -0.5)
    rows = row0 + lax.broadcasted_iota(jnp.int32, s.shape, 0)
    cols = lax.broadcasted_iota(jnp.int32, s.shape, 1)
    s = jnp.where(rows >= cols, s, NEG_BIG)
    return (ops.b.nn(_softmax(s), v),)


def _shift_down(x, d, t_idx):
    if d == 0:
        return x
    return jnp.where(t_idx >= d, pltpu.roll(x, d, axis=0), 0.0)


def _shift_up(x, d, t_idx):
    if d == 0:
        return x
    n = x.shape[0]
    return jnp.where(t_idx < n - d, pltpu.roll(x, n - d, axis=0), 0.0)


def conv_fwd(x, w, name):
    s, c = x.shape
    kw = w.shape[0]
    tc = _pick(c, (256, 128))

    def body(x_ref, w_ref, y_ref):
        xv = x_ref[...]
        t_idx = lax.broadcasted_iota(jnp.int32, xv.shape, 0)
        acc = jnp.zeros_like(xv)
        for j in range(kw):
            acc = acc + w_ref[j:j + 1, :] * _shift_down(xv, kw - 1 - j, t_idx)
        y_ref[...] = acc

    return pl.pallas_call(
        body, name=name, out_shape=jax.ShapeDtypeStruct((s, c), F32), grid=(c // tc,),
        in_specs=[pl.BlockSpec((s, tc), lambda i: (0, i)), pl.BlockSpec((kw, tc), lambda i: (0, i))],
        out_specs=pl.BlockSpec((s, tc), lambda i: (0, i)),
        compiler_params=_params(("parallel",)),
    )(x, w)


def conv_bwd(x, w, dy, name):
    s, c = x.shape
    kw = w.shape[0]
    tc = _pick(c, (256, 128))

    def body(x_ref, w_ref, dy_ref, dx_ref, dw_ref):
        xv, dyv = x_ref[...], dy_ref[...]
        t_idx = lax.broadcasted_iota(jnp.int32, xv.shape, 0)
        dx = jnp.zeros_like(xv)
        for j in range(kw):
            d = kw - 1 - j
            dx = dx + w_ref[j:j + 1, :] * _shift_up(dyv, d, t_idx)
            dw_ref[j:j + 1, :] = jnp.sum(dyv * _shift_down(xv, d, t_idx), axis=0, keepdims=True)
        dx_ref[...] = dx

    return pl.pallas_call(
        body, name=name,
        out_shape=[jax.ShapeDtypeStruct((s, c), F32), jax.ShapeDtypeStruct((kw, c), F32)],
        grid=(c // tc,),
        in_specs=[pl.BlockSpec((s, tc), lambda i: (0, i)), pl.BlockSpec((kw, tc), lambda i: (0, i)),
                  pl.BlockSpec((s, tc), lambda i: (0, i))],
        out_specs=[pl.BlockSpec((s, tc), lambda i: (0, i)), pl.BlockSpec((kw, tc), lambda i: (0, i))],
        compiler_params=_params(("parallel",)),
    )(x, w, dy)


def make_conv(name):
    @jax.custom_vjp
    def op(x, w):
        return conv_fwd(x, w, name + "_f")

    def fwd(x, w):
        return op(x, w), (x, w)

    def bwd(saved, dy):
        dx, dw = conv_bwd(saved[0], saved[1], dy, name + "_b")
        return dx, dw

    op.defvjp(fwd, bwd)
    return op


def _gdn_consts():
    c, d = GDN_CHUNK, GDN_DK
    i = lax.broadcasted_iota(jnp.int32, (c, c), 0)
    j = lax.broadcasted_iota(jnp.int32, (c, c), 1)
    tri = i >= j
    return dict(
        tri=tri, strict=i > j,
        tri_f=tri.astype(F32),
        eye=(i == j).astype(F32),
        lane0=(lax.broadcasted_iota(jnp.int32, (c, d), 1) == 0).astype(F32),
        last_row=(lax.broadcasted_iota(jnp.int32, (c, d), 0) == c - 1).astype(F32),
    )


def _gdn_chunk(ops, q, k, v, g, beta, state):
    b, m, sel = ops.bb, ops.bm, ops.bs
    nh, c, d = q.shape[0], GDN_CHUNK, GDN_DK
    k_ = _gdn_consts()

    def per_head(a):
        return jnp.broadcast_to(a, (nh,) + a.shape)

    gc = sel.sel_nn(per_head(k_["tri_f"]), g)
    col = jnp.broadcast_to(jnp.sum(gc * k_["lane0"], axis=2, keepdims=True), (nh, c, c))
    row = sel.sel_nt(per_head(k_["lane0"]), gc)
    decay = jnp.where(k_["tri"], jnp.exp(jnp.where(k_["tri"], col - row, 0.0)), 0.0)
    kb = k * beta
    mm_ = jnp.where(k_["strict"], b.nt(kb, k) * decay, 0.0)
    p = -mm_
    t = k_["eye"] + p
    for _ in range(int(math.log2(GDN_CHUNK)) - 1):
        p = m.nn(p, p)
        t = t + m.nn(t, p)
    egc = jnp.exp(gc)
    u = b.nn(t, v * beta)
    w = b.nn(t, kb * egc)
    attn = b.nt(q, k) * decay
    v_new = u - b.nn(w, state)
    o = b.nn(q * egc, state) + b.nn(attn, v_new)
    g_last = jnp.sum(gc * k_["last_row"], axis=1, keepdims=True)
    new_state = (state * jnp.exp(jnp.broadcast_to(g_last, (nh, d, d)))
                 + b.tn(k * jnp.exp(jnp.broadcast_to(g_last, (nh, c, d)) - gc), v_new))
    return o, new_state


GDN_HEAD_GROUP = 8
GDN_TILE_CHUNKS = 4


def _heads_of(ref, rows, n_heads):
    d = GDN_DK
    return jnp.stack([ref[rows, h * d:(h + 1) * d] for h in range(n_heads)])


def _gdn_specs(s, reverse):
    d, hg = GDN_DK, GDN_HEAD_GROUP
    tile = min(GDN_TILE_CHUNKS * GDN_CHUNK, s)
    n_tiles = s // tile
    t_of = (lambda t: n_tiles - 1 - t) if reverse else (lambda t: t)
    seq = pl.BlockSpec((tile, hg * d), lambda grp, t: (t_of(t), grp))
    st = pl.BlockSpec((hg, tile // GDN_CHUNK, d, d), lambda grp, t: (grp, t_of(t), 0, 0))
    return seq, st, tile, n_tiles


def gdn_fwd(q, k, v, g, beta, name):
    s = q.shape[0]
    d, hg = GDN_DK, GDN_HEAD_GROUP
    seq, st, tile, n_tiles = _gdn_specs(s, False)

    def body(q_ref, k_ref, v_ref, g_ref, b_ref, o_ref, st_ref, state_scr):
        @pl.when(pl.program_id(1) == 0)
        def _():
            state_scr[...] = jnp.zeros_like(state_scr)

        def step(ci, carry):
            rows = pl.ds(pl.multiple_of(ci * GDN_CHUNK, GDN_CHUNK), GDN_CHUNK)
            state = state_scr[...]
            for h in range(hg):
                st_ref[h, ci] = state[h]
            o, new_state = _gdn_chunk(_PLAIN, *[_heads_of(r, rows, hg) for r in (q_ref, k_ref, v_ref, g_ref, b_ref)],
                                      state)
            for h in range(hg):
                o_ref[rows, h * d:(h + 1) * d] = o[h]
            state_scr[...] = new_state
            return carry

        lax.fori_loop(0, tile // GDN_CHUNK, step, 0)

    return pl.pallas_call(
        body, name=name,
        out_shape=[jax.ShapeDtypeStruct(q.shape, F32),
                   jax.ShapeDtypeStruct((GDN_HEADS, s // GDN_CHUNK, d, d), F32)],
        grid=(GDN_HEADS // hg, n_tiles), in_specs=[seq] * 5, out_specs=[seq, st],
        scratch_shapes=[pltpu.VMEM((hg, d, d), F32)],
        compiler_params=_params(("parallel", "arbitrary")),
    )(q, k, v, g, beta)


def gdn_bwd(q, k, v, g, beta, states, do, name):
    s = q.shape[0]
    d, hg = GDN_DK, GDN_HEAD_GROUP
    seq, st, tile, n_tiles = _gdn_specs(s, True)
    tile_chunks = tile // GDN_CHUNK

    def body(q_ref, k_ref, v_ref, g_ref, b_ref, st_ref, do_ref, dq_ref, dk_ref, dv_ref, dg_ref, db_ref, dstate_scr):
        @pl.when(pl.program_id(1) == 0)
        def _():
            dstate_scr[...] = jnp.zeros_like(dstate_scr)

        def step(it, carry):
            ci = tile_chunks - 1 - it
            rows = pl.ds(pl.multiple_of(ci * GDN_CHUNK, GDN_CHUNK), GDN_CHUNK)
            prim = [_heads_of(r, rows, hg) for r in (q_ref, k_ref, v_ref, g_ref, b_ref)]
            prim.append(jnp.stack([st_ref[h, ci] for h in range(hg)]))
            _, vjp = jax.vjp(functools.partial(_gdn_chunk, _DIFF), *prim)
            grads = vjp((_heads_of(do_ref, rows, hg), dstate_scr[...]))
            for g_ref_out, gr in zip((dq_ref, dk_ref, dv_ref, dg_ref, db_ref), grads[:5]):
                for h in range(hg):
                    g_ref_out[rows, h * d:(h + 1) * d] = gr[h]
            dstate_scr[...] = grads[5]
            return carry

        lax.fori_loop(0, tile_chunks, step, 0)

    return pl.pallas_call(
        body, name=name,
        out_shape=[jax.ShapeDtypeStruct(q.shape, F32)] * 5,
        grid=(GDN_HEADS // hg, n_tiles), in_specs=[seq] * 5 + [st, seq], out_specs=[seq] * 5,
        scratch_shapes=[pltpu.VMEM((hg, d, d), F32)],
        compiler_params=_params(("parallel", "arbitrary")),
    )(q, k, v, g, beta, states, do)


def make_gdn(name):
    @jax.custom_vjp
    def op(q, k, v, g, beta):
        return gdn_fwd(q, k, v, g, beta, name + "_f")[0]

    def fwd(q, k, v, g, beta):
        o, states = gdn_fwd(q, k, v, g, beta, name + "_f")
        return o, (q, k, v, g, beta, states)

    def bwd(saved, do):
        return tuple(gdn_bwd(*saved, do, name + "_b"))

    op.defvjp(fwd, bwd)
    return op


def loss_head(x, g, target, name):
    s, d = x.shape
    tm = min(256, s)

    def body(x_ref, g_ref, t_ref, loss_ref, dx_ref, dg_ref):
        tgt = t_ref[...]

        def f(xv, gv):
            err = _rms(xv, gv) - tgt
            per_row = jnp.mean(err * err, axis=-1, keepdims=True)
            return 0.5 * jnp.sum(per_row, axis=0, keepdims=True)

        val, vjp = jax.vjp(f, x_ref[...], g_ref[...])
        dx, dg = vjp(jnp.ones((1, 1), F32))
        dx_ref[...] = dx
        first = pl.program_id(0) == 0

        @pl.when(first)
        def _():
            dg_ref[...] = dg
            loss_ref[...] = jnp.broadcast_to(val, loss_ref.shape)

        @pl.when(jnp.logical_not(first))
        def _():
            dg_ref[...] += dg
            loss_ref[...] += jnp.broadcast_to(val, loss_ref.shape)

    row = pl.BlockSpec((tm, d), lambda i: (i, 0))
    vec = pl.BlockSpec((1, d), lambda i: (0, 0))
    return pl.pallas_call(
        body, name=name,
        out_shape=[jax.ShapeDtypeStruct((1, LANES), F32), jax.ShapeDtypeStruct((s, d), F32),
                   jax.ShapeDtypeStruct((1, d), F32)],
        grid=(s // tm,), in_specs=[row, vec, row],
        out_specs=[pl.BlockSpec((1, LANES), lambda i: (0, 0)), row, vec],
        compiler_params=_params(("arbitrary",)),
    )(x, g, target)


def adamw(g8, w, m, v, layer, prev, name):
    n_layers, rows, width = w.shape
    tr = _pick(rows, (256, 128, 64, 32, 16, 8))

    def body(g_ref, w_ref, m_ref, v_ref, *rest):
        go_ref, d_ref, mo_ref, vo_ref = rest[-4:]
        g = g_ref[0].astype(F32)
        for p in range(1, N_DEV):
            g = g + g_ref[p].astype(F32)
        m_new = ADAM_B1 * m_ref[...] + (1.0 - ADAM_B1) * g
        v_new = ADAM_B2 * v_ref[...] + (1.0 - ADAM_B2) * (g * g)
        m_hat = m_new / (1.0 - ADAM_B1 ** ADAM_STEP)
        v_hat = v_new / (1.0 - ADAM_B2 ** ADAM_STEP)
        go_ref[...] = g
        d_ref[...] = -ADAM_LR * (m_hat / (jnp.sqrt(v_hat) + ADAM_EPS) + ADAM_WD * w_ref[...])
        mo_ref[...] = m_new
        vo_ref[...] = v_new

    blk = pl.BlockSpec((None, tr, width), lambda i: (layer, i, 0))
    carried = list(prev) if prev is not None else []
    return pl.pallas_call(
        body, name=name, out_shape=[jax.ShapeDtypeStruct((n_layers, rows, width), F32)] * 4,
        grid=(rows // tr,),
        in_specs=[pl.BlockSpec((N_DEV, tr, width), lambda i: (0, i, 0)), blk, blk, blk]
        + [pl.BlockSpec(memory_space=pl.ANY)] * len(carried),
        out_specs=[blk] * 4,
        input_output_aliases={4 + j: j for j in range(len(carried))},
        compiler_params=_params(("parallel",)),
    )(g8, w, m, v, *carried)


_HBM = pl.BlockSpec(memory_space=pltpu.HBM)
_SEM = pl.BlockSpec(memory_space=pltpu.SEMAPHORE)
_EFFECT = pltpu.SideEffectType.DATAFLOW_SIDE_EFFECTING


def _exchange_copies(mode, src_refs, land_refs, send_sems, recv_sems, local_sems):
    x, y, c = lax.axis_index("x"), lax.axis_index("y"), lax.axis_index("c")
    me = 4 * x + 2 * y + c
    n = len(src_refs)

    def src(k, p):
        return src_refs[k] if mode == "gather" else src_refs[k].at[p]

    local = [pltpu.make_async_copy(src(k, me), land_refs[k].at[me], local_sems.at[k]) for k in range(n)]
    sends, recvs = [], []
    for k in range(n):
        for r in range(1, N_DEV):
            px = (1 - x) if r & 4 else x
            py = (1 - y) if r & 2 else y
            pc = (1 - c) if r & 1 else c
            p = 4 * px + 2 * py + pc
            sem = k * (N_DEV - 1) + r - 1
            sends.append(pltpu.make_async_remote_copy(
                src_ref=src(k, p), dst_ref=land_refs[k].at[me],
                send_sem=send_sems.at[sem], recv_sem=recv_sems.at[sem],
                device_id=(px, py, pc), device_id_type=pl.DeviceIdType.MESH))
            recvs.append(pltpu.make_async_remote_copy(
                src_ref=src(k, p), dst_ref=land_refs[k].at[p],
                send_sem=send_sems.at[sem], recv_sem=recv_sems.at[sem],
                device_id=(px, py, pc), device_id_type=pl.DeviceIdType.MESH))
    return local, sends, recvs


def exchange_start(mode, arrays, name, carry=()):
    n, nc = len(arrays), len(carry)
    land_shapes = [((N_DEV,) + tuple(a.shape)) if mode == "gather" else tuple(a.shape) for a in arrays]
    lands = [pltpu.with_memory_space_constraint(lax.empty(shp, a.dtype), pltpu.HBM)
             for shp, a in zip(land_shapes, arrays)]
    srcs = [pltpu.with_memory_space_constraint(a, pltpu.HBM) for a in arrays]
    carried = [pltpu.with_memory_space_constraint(a, pltpu.HBM) for a in carry]

    def body(*refs):
        src_refs, land_refs = refs[:n], refs[n:2 * n]
        first_out = 2 * n + nc
        send_sems, recv_sems, local_sems = refs[first_out:first_out + 3]
        token = refs[-1]
        local, sends, _ = _exchange_copies(mode, src_refs, land_refs, send_sems, recv_sems, local_sems)
        for cp in local + sends:
            cp.start()
        token[...] = jnp.zeros_like(token)

    n_sem = n * (N_DEV - 1)
    out = pl.pallas_call(
        body, name=name,
        out_shape=(pltpu.SemaphoreType.DMA((n_sem,)), pltpu.SemaphoreType.DMA((n_sem,)),
                   pltpu.SemaphoreType.DMA((n,)),
                   *[pltpu.HBM(a.shape, a.dtype) for a in arrays],
                   *[pltpu.HBM(shp, a.dtype) for shp, a in zip(land_shapes, arrays)],
                   *[pltpu.HBM(a.shape, a.dtype) for a in carry],
                   jax.ShapeDtypeStruct((8, LANES), F32)),
        in_specs=[_HBM] * (2 * n + nc),
        out_specs=(_SEM, _SEM, _SEM, *[_HBM] * (2 * n + nc), pl.BlockSpec(memory_space=pltpu.VMEM)),
        input_output_aliases={i: 3 + i for i in range(2 * n + nc)},
        compiler_params=pltpu.CompilerParams(has_side_effects=_EFFECT),
    )(*srcs, *lands, *carried)
    handle = dict(mode=mode, sems=out[:3], srcs=out[3:3 + n], lands=out[3 + n:3 + 2 * n])
    return handle, out[-1], list(out[3 + 2 * n:3 + 2 * n + nc])


def exchange_wait(handle, after, name):
    mode, srcs, lands = handle["mode"], list(handle["srcs"]), list(handle["lands"])
    n = len(srcs)

    def body(*refs):
        src_refs, land_refs = refs[:n], refs[n:2 * n]
        send_sems, recv_sems, local_sems = refs[2 * n:2 * n + 3]
        local, sends, recvs = _exchange_copies(mode, src_refs, land_refs, send_sems, recv_sems, local_sems)
        for cp in sends:
            cp.wait_send()
        for cp in recvs:
            cp.wait_recv()
        for cp in local:
            cp.wait()

    out = pl.pallas_call(
        body, name=name,
        out_shape=(*[pltpu.HBM(a.shape, a.dtype) for a in srcs], *[pltpu.HBM(a.shape, a.dtype) for a in lands]),
        in_specs=[_HBM] * (2 * n) + [_SEM] * 3 + [pl.BlockSpec(memory_space=pl.ANY)],
        out_specs=tuple([_HBM] * (2 * n)),
        input_output_aliases={i: i for i in range(2 * n)},
        compiler_params=pltpu.CompilerParams(has_side_effects=_EFFECT),
    )(*srcs, *lands, *handle["sems"], after)
    return list(out[n:])


_ICI_RELATIONS = (2, 4, 6)


def _mesh_place():
    x, y, c = lax.axis_index("x"), lax.axis_index("y"), lax.axis_index("c")

    def peer(r):
        px = (1 - x) if r & 4 else x
        py = (1 - y) if r & 2 else y
        pc = (1 - c) if r & 1 else c
        return (px, py, pc), 4 * px + 2 * py + pc

    return 4 * x + 2 * y + c, peer


def _remote(src, dst, send_sem, recv_sem, device):
    return pltpu.make_async_remote_copy(src_ref=src, dst_ref=dst, send_sem=send_sem, recv_sem=recv_sem,
                                        device_id=device, device_id_type=pl.DeviceIdType.MESH)


def gather2_start(groups, name):
    flat = [a for g in groups for a in g]
    n = len(flat)
    lands = [pltpu.with_memory_space_constraint(lax.empty((N_DEV,) + tuple(a.shape), a.dtype), pltpu.HBM) for a in flat]
    srcs = [pltpu.with_memory_space_constraint(a, pltpu.HBM) for a in flat]
    n_rel = 1 + len(_ICI_RELATIONS)

    def body(*refs):
        src_refs, land_refs = refs[:n], refs[n:2 * n]
        sem_refs = refs[2 * n:2 * n + 4 * len(groups)]
        me, peer = _mesh_place()
        k = 0
        for gi, g in enumerate(groups):
            send_sems, recv_sib, recv_ici, local_sems = sem_refs[4 * gi:4 * gi + 4]
            for j in range(len(g)):
                pltpu.make_async_copy(src_refs[k], land_refs[k].at[me], local_sems.at[j]).start()
                dev, _ = peer(1)
                _remote(src_refs[k], land_refs[k].at[me], send_sems.at[n_rel * j], recv_sib.at[j], dev).start()
                for t, r in enumerate(_ICI_RELATIONS):
                    dev, _ = peer(r)
                    _remote(src_refs[k], land_refs[k].at[me], send_sems.at[n_rel * j + 1 + t],
                            recv_ici.at[len(_ICI_RELATIONS) * j + t], dev).start()
                k += 1
        refs[-1][...] = jnp.zeros_like(refs[-1])

    sem_shapes = []
    for g in groups:
        sem_shapes += [pltpu.SemaphoreType.DMA((n_rel * len(g),)), pltpu.SemaphoreType.DMA((len(g),)),
                       pltpu.SemaphoreType.DMA((len(_ICI_RELATIONS) * len(g),)), pltpu.SemaphoreType.DMA((len(g),))]
    out = pl.pallas_call(
        body, name=name,
        out_shape=(*sem_shapes, *[pltpu.HBM(a.shape, a.dtype) for a in flat],
                   *[pltpu.HBM((N_DEV,) + tuple(a.shape), a.dtype) for a in flat],
                   jax.ShapeDtypeStruct((8, LANES), F32)),
        in_specs=[_HBM] * (2 * n),
        out_specs=(*[_SEM] * len(sem_shapes), *[_HBM] * (2 * n), pl.BlockSpec(memory_space=pltpu.VMEM)),
        input_output_aliases={i: len(sem_shapes) + i for i in range(2 * n)},
        compiler_params=pltpu.CompilerParams(has_side_effects=_EFFECT),
    )(*srcs, *lands)
    handles, k, base = [], 0, len(sem_shapes)
    for gi, g in enumerate(groups):
        handles.append(dict(sems=out[4 * gi:4 * gi + 4], srcs=out[base + k:base + k + len(g)],
                            lands=out[base + n + k:base + n + k + len(g)]))
        k += len(g)
    return handles, out[-1]


def gather2_forward(handle, after, name, carry=()):
    lands, nc = list(handle["lands"]), len(carry)
    n, n_ici = len(lands), len(_ICI_RELATIONS)
    carried = [pltpu.with_memory_space_constraint(a, pltpu.HBM) for a in carry]

    def body(*refs):
        land_refs = refs[:n]
        recv_ici = refs[n + nc]
        fwd_send, fwd_recv = refs[n + nc + 2], refs[n + nc + 3]
        me, peer = _mesh_place()
        sibling, _ = peer(1)
        for j in range(n):
            for t, r in enumerate(_ICI_RELATIONS):
                dev, p = peer(r)
                landed = land_refs[j].at[p]
                _remote(landed, landed, fwd_send.at[n_ici * j + t], recv_ici.at[n_ici * j + t], dev).wait_recv()
                _remote(landed, landed, fwd_send.at[n_ici * j + t], fwd_recv.at[n_ici * j + t], sibling).start()

    out = pl.pallas_call(
        body, name=name,
        out_shape=(pltpu.SemaphoreType.DMA((n_ici * n,)), pltpu.SemaphoreType.DMA((n_ici * n,)),
                   *[pltpu.HBM(a.shape, a.dtype) for a in lands], *[pltpu.HBM(a.shape, a.dtype) for a in carry]),
        in_specs=[_HBM] * (n + nc) + [_SEM, pl.BlockSpec(memory_space=pl.ANY)],
        out_specs=(_SEM, _SEM, *[_HBM] * (n + nc)),
        input_output_aliases={i: 2 + i for i in range(n + nc)},
        compiler_params=pltpu.CompilerParams(has_side_effects=_EFFECT),
    )(*lands, *carried, handle["sems"][2], after)
    new_handle = dict(sems=handle["sems"], srcs=handle["srcs"], lands=out[2:2 + n], fwd=out[:2])
    return new_handle, list(out[2 + n:])


def gather2_wait(handle, after, name):
    srcs, lands = list(handle["srcs"]), list(handle["lands"])
    n, n_ici = len(srcs), len(_ICI_RELATIONS)
    n_rel = 1 + n_ici
    send_all, recv_sibling, _, local_all = handle["sems"]

    def body(*refs):
        src_refs, land_refs = refs[:n], refs[n:2 * n]
        send_sems, recv_sib, local_sems, fwd_send, fwd_recv = refs[2 * n:2 * n + 5]
        me, peer = _mesh_place()
        sibling, sib = peer(1)
        for j in range(n):
            pltpu.make_async_copy(src_refs[j], land_refs[j].at[me], local_sems.at[j]).wait()
            _remote(src_refs[j], land_refs[j].at[sib], send_sems.at[n_rel * j], recv_sib.at[j], sibling).wait()
            for t, r in enumerate(_ICI_RELATIONS):
                dev, p = peer(r)
                _remote(src_refs[j], land_refs[j].at[me], send_sems.at[n_rel * j + 1 + t],
                        recv_sib.at[j], dev).wait_send()
                _, p_sib = peer(r ^ 1)
                _remote(land_refs[j].at[p], land_refs[j].at[p_sib], fwd_send.at[n_ici * j + t],
                        fwd_recv.at[n_ici * j + t], sibling).wait()

    out = pl.pallas_call(
        body, name=name,
        out_shape=(*[pltpu.HBM(a.shape, a.dtype) for a in srcs], *[pltpu.HBM(a.shape, a.dtype) for a in lands]),
        in_specs=[_HBM] * (2 * n) + [_SEM] * 5 + [pl.BlockSpec(memory_space=pl.ANY)],
        out_specs=tuple([_HBM] * (2 * n)),
        input_output_aliases={i: i for i in range(2 * n)},
        compiler_params=pltpu.CompilerParams(has_side_effects=_EFFECT),
    )(*srcs, *lands, send_all, recv_sibling, local_all, *handle["fwd"], after)
    return list(out[n:])


BIG = ["mla_w_in", "mla_w_uq", "mla_w_ukv", "mla_w_o", "gdn_w_in", "gdn_w_o", "sc_w_in", "sc_w_o",
       "xa_w_q", "xa_w_kv", "xa_w_o", "mlp_w1", "mlp_w2"]
TINY = [("mla_q_norm", 1), ("mla_kv_norm", 1), ("gdn_conv_w", 2), ("sc_conv_w", 2)]
REPL = ["gdn_a_log", "gdn_dt_bias", "gdn_o_norm", "norm_mix", "norm_mem", "norm_mlp", "mem_norm", "final_norm"]
WEIGHTS = ["mla_w_in", "mla_q_norm", "mla_kv_norm", "mla_w_uq", "mla_w_ukv", "mla_w_o", "gdn_w_in",
           "gdn_conv_w", "gdn_a_log", "gdn_dt_bias", "gdn_o_norm", "gdn_w_o", "sc_w_in", "sc_conv_w",
           "sc_w_o", "norm_mix", "norm_mem", "norm_mlp", "xa_w_q", "xa_w_kv", "xa_w_o", "mlp_w1",
           "mlp_w2", "mem_norm", "final_norm"]
MIXER_WEIGHTS = (["mla_w_in", "mla_w_uq", "mla_w_ukv", "mla_w_o"], ["gdn_w_in", "gdn_w_o"], ["sc_w_in", "sc_w_o"])
MIXER_PARAMS = (["norm_mem", "mla_q_norm", "mla_kv_norm"],
                ["norm_mem", "gdn_conv_w", "gdn_a_log", "gdn_dt_bias", "gdn_o_norm"],
                ["norm_mem", "sc_conv_w"])


def from_shards(a8, axis):
    a = jnp.moveaxis(a8, 0, axis)
    shp = a.shape
    return a.reshape(shp[:axis] + (shp[axis] * shp[axis + 1],) + shp[axis + 2:])


def pack_rows(flat_list, width, row_mult):
    total = sum(a.shape[-1] for a in flat_list)
    rows = -(-total // width)
    rows = -(-rows // row_mult) * row_mult
    pad = rows * width - total
    parts = list(flat_list)
    if pad:
        parts.append(jnp.zeros((pad,), flat_list[0].dtype))
    return jnp.concatenate(parts, axis=-1).reshape(rows, width)


def unpack_rows(packed, shapes):
    lead = packed.shape[:-2]
    flat = packed.reshape(lead + (-1,))
    out, off = [], 0
    for shp in shapes:
        n = math.prod(shp)
        out.append(flat[..., off:off + n].reshape(lead + tuple(shp)))
        off += n
    return out


def _swap_halves(w):
    half = w.shape[-1] // 2
    return jnp.concatenate([w[..., half:], w[..., :half]], axis=-1)


def _pad_last(w, n):
    return jnp.pad(w, [(0, 0)] * (w.ndim - 1) + [(0, n - w.shape[-1])])


def _unblock(w8):
    return jnp.transpose(w8, (1, 0, 2)).reshape(w8.shape[1], -1)


def _stack_rows(w8):
    return w8.reshape(-1, w8.shape[-1])


def rms_op(name, rows, d, out_dtype):
    tm = rows if rows * d * 4 <= BLOCK_BYTES else 512
    return make_tile_op(fn_rms, name, ["row", "par"], [True, True], [("row", d, out_dtype)], rows, min(tm, rows))


def seg_memory(p, mem):
    return rms_op("rms_memory", mem.shape[0], mem.shape[1], BF16)(mem, p["mem_norm"].reshape(1, -1))[0]


def seg_mixer(i, wts, p, x, h, rope_c, rope_s):
    s, d = x.shape
    j, kind = i // N_MIXERS, i % N_MIXERS
    tag = f"l{i}"
    hd = MLA_NOPE
    next_gain = p["norm_mem"][i].reshape(1, d)
    if kind == 0:
        w_in = _stack_rows(wts["mla_w_in"])
        w_cq = w_in[:, :MLA_Q_RANK]
        w_ckv = w_in[:, MLA_Q_RANK:MLA_Q_RANK + MLA_KV_RANK]
        w_kr = w_in[:, MLA_Q_RANK + MLA_KV_RANK:]
        w_z = jnp.concatenate([w_cq, w_ckv, _pad_last(w_kr, hd), _pad_last(_swap_halves(w_kr), hd)], axis=-1)
        z = make_mm(tag + "_mla_in", F32)(h, w_z)
        c_q, c_kv = z[:, :MLA_Q_RANK], z[:, MLA_Q_RANK:MLA_Q_RANK + MLA_KV_RANK]
        kr_raw, kr_swp = z[:, -2 * hd:-hd], z[:, -hd:]
        c_qn = rms_op(tag + "_rms_q", s, MLA_Q_RANK, BF16)(c_q, p["mla_q_norm"][j].reshape(1, -1))[0]
        c_kvn = rms_op(tag + "_rms_kv", s, MLA_KV_RANK, BF16)(c_kv, p["mla_kv_norm"][j].reshape(1, -1))[0]
        w_uq8 = wts["mla_w_uq"]
        w_qn = _unblock(w_uq8[:, :, :MLA_NOPE])
        w_qr = w_uq8[:, :, MLA_NOPE:]
        w_qr_p = _unblock(_pad_last(w_qr, hd))
        w_qr_s = _unblock(_pad_last(_swap_halves(w_qr), hd))
        q_nope = make_mm(tag + "_mla_uq_n", BF16)(c_qn, w_qn)
        q_raw_swp = make_mm(tag + "_mla_uq_r", F32)(c_qn, jnp.concatenate([w_qr_p, w_qr_s], axis=-1))
        nq = MLA_HEADS * hd
        kv = make_mm(tag + "_mla_ukv", BF16, blocked=True)(c_kvn, wts["mla_w_ukv"])
        heads_row = ("row", hd, MLA_HEADS)
        q_rope = make_tile_op(fn_rope, tag + "_rope_q", [("row", hd, 2 * MLA_HEADS), "row", "row"],
                              [True, False, False], [(heads_row, nq, F32)], s, 512)(
            q_raw_swp, rope_c, rope_s)[0]
        k_rope = make_tile_op(fn_rope, tag + "_rope_k", ["row", "row", "row", "row"],
                              [True, True, False, False], [("row", hd, F32)], s, 1024)(
            kr_raw, kr_swp, rope_c, rope_s)[0]
        n_groups = MLA_QUERY_GROUPS if s % (MLA_QUERY_GROUPS * 256) == 0 else 1
        rows_g = s // n_groups
        o_groups = []
        for grp in range(n_groups):
            r0, r1 = grp * rows_g, (grp + 1) * rows_g
            o_groups.append(make_tile_op(
                fn_mla_attn, f"{tag}_mla_attn{grp}", [("rowh", hd), ("rowh", hd), ("parh", hd, 2), "par"],
                [True] * 4, [(("rowh", hd), nq, BF16)], rows_g, 256, MLA_HEADS, row_base=r0)(
                q_nope[r0:r1], q_rope[r0:r1], kv[:r1], k_rope[:r1])[0])
        o = jnp.concatenate(o_groups, axis=0)
        return make_mm_res_rms(tag + "_mla_o")(x, o, _stack_rows(wts["mla_w_o"]), next_gain)
    if kind == 1:
        ng = GDN_HEADS * GDN_DK
        w_in = _unblock(wts["gdn_w_in"])
        cw = p["gdn_conv_w"][j]
        conv_out = []
        for part, nm in enumerate(("q", "k", "v")):
            cols = slice(part * ng, (part + 1) * ng)
            pre = make_mm(f"{tag}_gdn_in_{nm}", F32)(h, w_in[:, cols])
            conv_out.append(make_conv(f"{tag}_gdn_conv_{nm}")(pre, cw[:, cols]))
        gate = make_mm(tag + "_gdn_in_g", F32)(h, w_in[:, 3 * ng:4 * ng])
        ba = make_mm(tag + "_gdn_in_ba", F32)(h, _pad_last(w_in[:, 4 * ng:], LANES))
        heads_row = ("row", GDN_DK, GDN_HEADS)
        q, k, v = make_tile_op(fn_gdn_prep, tag + "_gdn_prep", [heads_row] * 3, [True] * 3,
                               [(heads_row, ng, F32)] * 3, s, 512)(*conv_out)
        alog = jnp.pad(p["gdn_a_log"][j].reshape(1, -1), ((0, 0), (GDN_HEADS, LANES - 2 * GDN_HEADS)))
        dtb = jnp.pad(p["gdn_dt_bias"][j].reshape(1, -1), ((0, 0), (GDN_HEADS, LANES - 2 * GDN_HEADS)))
        beta_b, g_b = make_tile_op(fn_gdn_gates, tag + "_gdn_gates", ["row", "par", "par"], [True] * 3,
                                   [("row", ng, F32)] * 2, s, 512)(ba, alog, dtb)
        o = make_gdn(tag + "_gdn_core")(q, k, v, g_b, beta_b)
        o = make_tile_op(fn_gdn_out, tag + "_gdn_out", [heads_row, heads_row, "par"],
                         [True] * 3, [(heads_row, ng, BF16)], s, 512)(
            o, gate, p["gdn_o_norm"][j].reshape(1, -1))[0]
        return make_mm_res_rms(tag + "_gdn_o")(x, o, _stack_rows(wts["gdn_w_o"]), next_gain)
    w_in = _unblock(wts["sc_w_in"])
    b_gate = make_mm(tag + "_sc_in_b", F32)(h, w_in[:, :d])
    c_gate = make_mm(tag + "_sc_in_c", F32)(h, w_in[:, d:2 * d])
    u = make_mm(tag + "_sc_in_u", F32)(h, w_in[:, 2 * d:])
    cu = make_tile_op(fn_mul, tag + "_sc_cu", ["row", "row"], [True, True], [("row", d, F32)], s, 512)(
        c_gate, u)[0]
    cv = make_conv(tag + "_sc_conv")(cu, p["sc_conv_w"][j])
    yv = make_tile_op(fn_mul, tag + "_sc_gate", ["row", "row"], [True, True], [("row", d, BF16)], s, 512)(
        b_gate, cv)[0]
    return make_mm_res_rms(tag + "_sc_o")(x, yv, _stack_rows(wts["sc_w_o"]), next_gain)


def seg_xattn(i, wts, p, x, hx, mem_n):
    s, d = x.shape
    tag = f"l{i}"
    q = make_mm(tag + "_xa_q", BF16)(hx, _stack_rows(wts["xa_w_q"]))
    kv = make_mm(tag + "_xa_kv", BF16, blocked=True)(mem_n, wts["xa_w_kv"])
    o = make_tile_op(fn_xattn, tag + "_xattn",
                     [("rowh", X_HEAD_DIM), ("parh", X_HEAD_DIM), ("parh", X_HEAD_DIM)], [True] * 3,
                     [(("rowh", X_HEAD_DIM), d, BF16)], s, 1024, X_HEADS)(q, kv[:, :d], kv[:, d:])[0]
    return make_mm_res_rms(tag + "_xa_o")(x, o, _stack_rows(wts["xa_w_o"]), p["norm_mlp"][i].reshape(1, d))


def seg_mlp(i, wts, p, x, hm):
    d = x.shape[1]
    gain = p["norm_mix"][i + 1].reshape(1, d) if i + 1 < DEPTH else None
    return make_mlp(f"l{i}_mlp", gain is not None)(x, hm, wts["mlp_w1"], _stack_rows(wts["mlp_w2"]), gain)


def segments():
    segs = []
    for i in range(DEPTH):
        j, kind = i // N_MIXERS, i % N_MIXERS
        segs.append((f"l{i}_mixer", [(n, j) for n in MIXER_WEIGHTS[kind]], MIXER_PARAMS[kind], "mixer"))
        segs.append((f"l{i}_xattn", [(n, i) for n in ("xa_w_q", "xa_w_kv", "xa_w_o")], ["norm_mlp"], "xattn"))
        segs.append((f"l{i}_mlp", [(n, i) for n in ("mlp_w1", "mlp_w2")], ["norm_mix"] if i + 1 < DEPTH else [],
                     "mlp"))
    return segs


def run_segment(index, kind, wts, p, x, h, mem_n, rope_c, rope_s):
    layer = index // 3
    if kind == "mixer":
        return seg_mixer(layer, wts, p, x, h, rope_c, rope_s)
    if kind == "xattn":
        return seg_xattn(layer, wts, p, x, h, mem_n)
    return seg_mlp(layer, wts, p, x, h)


def rope_tables(positions):
    inv_freq = ROPE_THETA ** (-jnp.arange(0, MLA_ROPE, 2, dtype=F32) / MLA_ROPE)
    ang = positions.astype(F32)[:, None] * inv_freq
    cos, sin = jnp.cos(ang), jnp.sin(ang)
    zeros = jnp.zeros((positions.shape[0], MLA_NOPE - MLA_ROPE), F32)
    return jnp.concatenate([cos, cos, zeros], axis=-1), jnp.concatenate([-sin, sin, zeros], axis=-1)


def kernel(x, mem, positions, mla_w_in, mla_q_norm, mla_kv_norm, mla_w_uq, mla_w_ukv, mla_w_o, gdn_w_in, gdn_conv_w, gdn_a_log, gdn_dt_bias, gdn_o_norm, gdn_w_o, sc_w_in, sc_conv_w, sc_w_o, norm_mix, norm_mem, norm_mlp, xa_w_q, xa_w_kv, xa_w_o, mlp_w1, mlp_w2, mem_norm, final_norm, loss_target, m_mla_w_in, m_mla_q_norm, m_mla_kv_norm, m_mla_w_uq, m_mla_w_ukv, m_mla_w_o, m_gdn_w_in, m_gdn_conv_w, m_gdn_a_log, m_gdn_dt_bias, m_gdn_o_norm, m_gdn_w_o, m_sc_w_in, m_sc_conv_w, m_sc_w_o, m_norm_mix, m_norm_mem, m_norm_mlp, m_xa_w_q, m_xa_w_kv, m_xa_w_o, m_mlp_w1, m_mlp_w2, m_mem_norm, m_final_norm, v_mla_w_in, v_mla_q_norm, v_mla_kv_norm, v_mla_w_uq, v_mla_w_ukv, v_mla_w_o, v_gdn_w_in, v_gdn_conv_w, v_gdn_a_log, v_gdn_dt_bias, v_gdn_o_norm, v_gdn_w_o, v_sc_w_in, v_sc_conv_w, v_sc_w_o, v_norm_mix, v_norm_mem, v_norm_mlp, v_xa_w_q, v_xa_w_kv, v_xa_w_o, v_mlp_w1, v_mlp_w2, v_mem_norm, v_final_norm):
    args = locals()
    w_loc = {n: args[n] for n in WEIGHTS}
    m_loc = {n: args["m_" + n] for n in WEIGHTS}
    v_loc = {n: args["v_" + n] for n in WEIGHTS}
    me = 4 * lax.axis_index("x") + 2 * lax.axis_index("y") + lax.axis_index("c")
    segs = segments()

    w16 = {n: w_loc[n].astype(BF16) for n in BIG}
    tiny_pack = pack_rows([w_loc[n].reshape(-1) for n, _ in TINY], LANES, 8)
    gather_handles, token = gather2_start(
        [[tiny_pack]] + [[w16[n][layer] for n, layer in units] for _, units, _, _ in segs], "gather_start")

    x_cur = x[0]
    rope_c, rope_s = rope_tables(positions[0])
    tiny_handle, _ = gather2_forward(gather_handles[0], token, "gather_forward_tiny")
    tiny_all = gather2_wait(tiny_handle, token, "gather_wait_tiny")[0]
    gather_handles = gather_handles[1:]
    params = {}
    for (n, ax), a8 in zip(TINY, unpack_rows(tiny_all, [w_loc[n].shape for n, _ in TINY])):
        params[n] = from_shards(a8, ax)
    for n in REPL:
        params[n] = w_loc[n]

    mem_n, vjp_memory = jax.vjp(lambda p_: seg_memory(p_, mem[0]), {"mem_norm": params["mem_norm"]})
    h_cur, vjp_first_norm = jax.vjp(
        lambda p_, x_: rms_op("l0_rms_mix", x_.shape[0], x_.shape[1], BF16)(x_, p_["norm_mix"][0].reshape(1, -1))[0],
        {"norm_mix": params["norm_mix"]}, x_cur)
    vjps = []
    forwarded, _ = gather2_forward(gather_handles[0], token, f"gather_forward_{segs[0][0]}")
    for index, (tag, units, p_names, kind) in enumerate(segs):
        landed = gather2_wait(forwarded, token if index == 0 else x_cur, f"gather_wait_{tag}")
        wts = {n: a for (n, _), a in zip(units, landed)}
        p_seg = {n: params[n] for n in p_names}
        if index + 1 < len(segs):
            forwarded, (p_seg[p_names[0]],) = gather2_forward(
                gather_handles[index + 1], landed[0], f"gather_forward_{segs[index + 1][0]}",
                carry=[p_seg[p_names[0]]])
        outs, vjp_seg = jax.vjp(
            lambda w_, p_, x_, h_, m_, index=index, kind=kind:
            run_segment(index, kind, w_, p_, x_, h_, m_, rope_c, rope_s),
            wts, p_seg, x_cur, h_cur, mem_n)
        x_cur, h_cur = outs[0], (outs[1] if len(outs) > 1 else None)
        vjps.append(vjp_seg)

    loss_vec, g_x, d_final = loss_head(x_cur, params["final_norm"].reshape(1, -1), loss_target[0], "loss_head")

    grads = {n: jnp.zeros_like(params[n]) for n in params}
    grads["final_norm"] = d_final.reshape(-1)
    g_mem_n = jnp.zeros_like(mem_n)
    g_h = None
    scatter_handles = []
    for (tag, units, _, _), vjp_seg in zip(reversed(segs), reversed(vjps)):
        g_wts, g_p, g_x, g_h, g_m = vjp_seg((g_x,) if g_h is None else (g_x, g_h))
        for n, g in g_p.items():
            grads[n] = grads[n] + g
        g_mem_n = g_mem_n + g_m
        handle, _, (g_h,) = exchange_start("scatter", [g_wts[n] for n, _ in units], f"scatter_start_{tag}",
                                           carry=[g_h])
        scatter_handles.append((units, handle))
    grads["mem_norm"] = grads["mem_norm"] + vjp_memory(g_mem_n)[0]["mem_norm"]
    g_first, g_x_norm = vjp_first_norm(g_h)
    grads["norm_mix"] = grads["norm_mix"] + g_first["norm_mix"]
    g_x = g_x + g_x_norm

    small_names = [n for n, _ in TINY] + REPL
    small_g = pack_rows([loss_vec[0, :1]] + [grads[n].astype(F32).reshape(-1) for n in small_names], PACK_W, 8)
    small_handle, _, _ = exchange_start("gather", [small_g], "gather_start_small_grads")

    g_recv = {}
    for units, handle in scatter_handles:
        landed = exchange_wait(handle, g_x, f"scatter_wait_{units[0][0]}_{units[0][1]}")
        g_recv.update(dict(zip(units, landed)))

    res = {}
    for n in BIG:
        outs = None
        for layer in range(w_loc[n].shape[0]):
            outs = adamw(g_recv[n, layer], w_loc[n], m_loc[n], v_loc[n], layer, outs, f"adamw_{n}_{layer}")
        for kind, a in zip(("grad", "delta", "m", "v"), outs):
            res[(kind, n)] = a
    small_recv = exchange_wait(small_handle, res[("grad", BIG[-1])], "gather_wait_small_grads")[0]

    def full_small(d):
        parts = [jnp.zeros((1,), F32)]
        for n, ax in TINY:
            full_shape = params[n].shape
            start = [0] * len(full_shape)
            start[ax] = me * d[n].shape[ax]
            parts.append(lax.dynamic_update_slice(jnp.zeros(full_shape, F32), d[n], start).reshape(-1))
        parts += [d[n].reshape(-1) for n in REPL]
        return pack_rows(parts, PACK_W, 8)

    outs_small = adamw(small_recv, full_small(w_loc)[None], full_small(m_loc)[None], full_small(v_loc)[None],
                       0, None, "adamw_small")
    small_shapes = [(1,)] + [params[n].shape for n, _ in TINY] + [w_loc[n].shape for n in REPL]
    loss = None
    for kind, packed in zip(("grad", "delta", "m", "v"), outs_small):
        parts = unpack_rows(packed[0], small_shapes)
        if kind == "grad":
            loss = parts[0][0]
        for (n, ax), a in zip(TINY, parts[1:1 + len(TINY)]):
            start = [0] * a.ndim
            start[ax] = me * w_loc[n].shape[ax]
            res[(kind, n)] = lax.dynamic_slice(a, start, w_loc[n].shape)
        for n, a in zip(REPL, parts[1 + len(TINY):]):
            res[(kind, n)] = a

    out = [loss, g_x[None]]
    for kind in ("grad", "delta", "m", "v"):
        out += [res[(kind, n)] for n in WEIGHTS]
    return tuple(out)
```

```python
import functools
import math

import jax
import jax.numpy as jnp
from jax import lax
from jax.experimental import pallas as pl
from jax.experimental.pallas import tpu as pltpu

F32 = jnp.float32
BF16 = jnp.bfloat16

N_DEV = 8
LANES = 128
EPS = 1e-6
ROPE_THETA = 10000.0
MLA_HEADS, MLA_NOPE, MLA_ROPE, MLA_V = 8, 128, 64, 128
MLA_Q_RANK, MLA_KV_RANK = 384, 256
GDN_HEADS, GDN_DK, GDN_CONV, GDN_CHUNK = 8, 128, 4, 64
X_HEADS, X_HEAD_DIM = 4, 256
DEPTH, N_MIXERS = 4, 3
ADAM_LR, ADAM_B1, ADAM_B2, ADAM_EPS, ADAM_WD, ADAM_STEP = 0.001, 0.9, 0.999, 1e-08, 0.01, 10
MLA_QUERY_GROUPS = 4
NEG_BIG = -1e30
PACK_W = 1024


_NN = (((1,), (0,)), ((), ()))
_NT = (((1,), (1,)), ((), ()))
_TN = (((0,), (0,)), ((), ()))
_NN3 = (((2,), (1,)), ((0,), (0,)))
_NT3 = (((2,), (2,)), ((0,), (0,)))
_TN3 = (((1,), (1,)), ((0,), (0,)))


def _dot(a, b, dims):
    return lax.dot_general(a, b, dims, preferred_element_type=F32)


def _hi_lo(x):
    hi = x.astype(BF16)
    return hi, (x - hi.astype(F32)).astype(BF16)


def _split3(x):
    hi = x.astype(BF16)
    r = x - hi.astype(F32)
    mid = r.astype(BF16)
    return hi, mid, (r - mid.astype(F32)).astype(BF16)


def _dg(a, b, dims, prec):
    if prec == "h":
        return lax.dot_general(a, b, dims, precision=lax.Precision.HIGHEST, preferred_element_type=F32)
    if prec == "m":
        a_hi, a_lo = _hi_lo(a)
        b_hi, b_lo = _hi_lo(b)
        return _dot(a_hi, b_hi, dims) + _dot(a_hi, b_lo, dims) + _dot(a_lo, b_hi, dims)
    return _dot(a.astype(BF16), b.astype(BF16), dims)


def _dg_sel(sel, x, dims, sel_first):
    s16 = sel.astype(BF16)
    parts = [(_dot(s16, piece, dims) if sel_first else _dot(piece, s16, dims)) for piece in _split3(x)]
    return parts[0] + parts[1] + parts[2]


class _Ops:
    def __init__(self, prec, differentiable, batched=False):
        d_nn, d_nt, d_tn = (_NN3, _NT3, _TN3) if batched else (_NN, _NT, _TN)

        def nn(a, b):
            return _dg(a, b, d_nn, prec)

        def nt(a, b):
            return _dg(a, b, d_nt, prec)

        def tn(a, b):
            return _dg(a, b, d_tn, prec)

        if differentiable:
            dnn = jax.custom_vjp(nn)
            dnn.defvjp(lambda a, b: (nn(a, b), (a, b)), lambda r, g: (nt(g, r[1]), tn(r[0], g)))
            dnt = jax.custom_vjp(nt)
            dnt.defvjp(lambda a, b: (nt(a, b), (a, b)), lambda r, g: (nn(g, r[1]), tn(g, r[0])))
            dtn = jax.custom_vjp(tn)
            dtn.defvjp(lambda a, b: (tn(a, b), (a, b)), lambda r, g: (nt(r[1], g), nn(r[0], g)))
            nn, nt, tn = dnn, dnt, dtn
        self.nn, self.nt, self.tn = nn, nt, tn


class _SelOps:
    def __init__(self, differentiable, batched=False):
        d_nn, d_nt, d_tn = (_NN3, _NT3, _TN3) if batched else (_NN, _NT, _TN)

        def sel_nn(sel, x):
            return _dg_sel(sel, x, d_nn, True)

        def sel_nt(sel, x):
            return _dg_sel(sel, x, d_nt, True)

        if differentiable:
            dnn = jax.custom_vjp(sel_nn)
            dnn.defvjp(lambda s, x: (sel_nn(s, x), s),
                       lambda s, g: (jnp.zeros_like(s), _dg_sel(s, g, d_tn, True)))
            dnt = jax.custom_vjp(sel_nt)
            dnt.defvjp(lambda s, x: (sel_nt(s, x), s),
                       lambda s, g: (jnp.zeros_like(s), _dg_sel(s, g, d_tn, False)))
            sel_nn, sel_nt = dnn, dnt
        self.sel_nn, self.sel_nt = sel_nn, sel_nt


class _OpSet:
    def __init__(self, differentiable):
        self.b = _Ops("b", differentiable)
        self.h = _Ops("h", differentiable)
        self.bb = _Ops("b", differentiable, batched=True)
        self.bm = _Ops("m", differentiable, batched=True)
        self.bs = _SelOps(differentiable, batched=True)


_PLAIN = _OpSet(False)
_DIFF = _OpSet(True)


def _params(sem):
    return pltpu.CompilerParams(dimension_semantics=sem)


BLOCK_BYTES = 4 * 1024 * 1024


def _pick(n, cands):
    for c in cands:
        if n % c == 0:
            return c
    return n


def _tile(n, cap):
    if n <= cap:
        return n
    return _pick(n, tuple(c for c in (2048, 1024, 768, 512, 384, 256, 128) if c <= cap))


def matmul(a, b, form, out_dtype, name, res=None, blocked=False, relu_gate=None, rms_gain=None, relu2_out=False):
    if form == "nn":
        m, k = a.shape
        k2, n = (b.shape[1], N_DEV * b.shape[2]) if blocked else b.shape
    elif form == "nt":
        m, k = a.shape
        n, k2 = (b.shape[1], N_DEV * b.shape[2]) if blocked else b.shape
    else:
        (k, m), (k2, n) = a.shape, b.shape
    assert k == k2, (a.shape, b.shape, form)
    tk = k if k <= 2048 else _tile(k, 1024)
    cb = nb = 1
    if blocked:
        cb = (k if form == "nt" else n) // N_DEV
        nb = _pick(N_DEV, tuple(c for c in (8, 4, 2, 1) if c * cb <= 1024))
    if blocked and form == "nt":
        tk = nb * cb
    if blocked and form != "nt":
        tn = nb * cb
    else:
        tn = _tile(n, min(1024, BLOCK_BYTES // (tk * b.dtype.itemsize)))
    out_elems = BLOCK_BYTES // 2 if (out_dtype == BF16 and res is None) else BLOCK_BYTES // 4
    tm = _tile(m, min(BLOCK_BYTES // (tk * a.dtype.itemsize), out_elems // tn))
    nk = k // tk
    dims = {"nn": _NN, "nt": _NT, "tn": _TN}[form]

    a_spec = {"nn": pl.BlockSpec((tm, tk), lambda i, j, kk: (i, kk)),
              "nt": pl.BlockSpec((tm, tk), lambda i, j, kk: (i, kk)),
              "tn": pl.BlockSpec((tk, tm), lambda i, j, kk: (kk, i))}[form]
    if blocked and form == "nn":
        b_spec = pl.BlockSpec((nb, tk, cb), lambda i, j, kk: (j, kk, 0))
    elif blocked and form == "nt":
        b_spec = pl.BlockSpec((nb, tn, cb), lambda i, j, kk: (kk, j, 0))
    else:
        b_spec = {"nn": pl.BlockSpec((tk, tn), lambda i, j, kk: (kk, j)),
                  "nt": pl.BlockSpec((tn, tk), lambda i, j, kk: (j, kk)),
                  "tn": pl.BlockSpec((tk, tn), lambda i, j, kk: (kk, j))}[form]
    c_spec = pl.BlockSpec((tm, tn), lambda i, j, kk: (i, j))
    out_shape = jax.ShapeDtypeStruct((m, n), out_dtype)
    o_spec = c_spec
    blocked_out = blocked and form == "tn"
    if blocked_out:
        out_shape = jax.ShapeDtypeStruct((N_DEV, m, cb), out_dtype)
        o_spec = pl.BlockSpec((nb, tm, cb), lambda i, j, kk: (j, i, 0))
    has_res, has_gate, has_gain = res is not None, relu_gate is not None, rms_gain is not None
    extras = [e for e in (res, relu_gate) if e is not None]
    n_in = 2 + len(extras) + has_gain
    second = has_gain or relu2_out
    assert not (second and (blocked_out or tn != n and has_gain))

    def body(*refs):
        a_ref, b_ref = refs[0], refs[1]
        r_ref = refs[2] if has_res else None
        gate_ref = refs[2 + has_res] if has_gate else None
        gain_ref = refs[n_in - 1] if has_gain else None
        o_ref = refs[n_in]
        a_val = a_ref[...].astype(BF16)
        if blocked and form == "nn":
            part = jnp.concatenate([_dot(a_val, b_ref[t].astype(BF16), dims) for t in range(nb)], axis=-1)
        elif blocked and form == "nt":
            part = _dot(a_val[:, :cb], b_ref[0].astype(BF16), dims)
            for t in range(1, nb):
                part = part + _dot(a_val[:, t * cb:(t + 1) * cb], b_ref[t].astype(BF16), dims)
        else:
            part = _dot(a_val, b_ref[...].astype(BF16), dims)

        def finish(acc):
            if has_res:
                acc = acc + r_ref[...].astype(F32)
            if has_gate:
                acc = acc * (2.0 * jnp.maximum(gate_ref[...].astype(F32), 0.0))
            if blocked_out:
                for t in range(nb):
                    o_ref[t] = acc[:, t * cb:(t + 1) * cb].astype(out_dtype)
            else:
                o_ref[...] = acc.astype(out_dtype)
            if has_gain:
                refs[n_in + 1][...] = _rms(acc, gain_ref[...]).astype(BF16)
            if relu2_out:
                r = jnp.maximum(acc.astype(out_dtype).astype(F32), 0.0)
                refs[n_in + 1][...] = (r * r).astype(BF16)

        if nk == 1:
            finish(part)
        else:
            acc_ref = refs[-1]
            kk = pl.program_id(2)

            @pl.when(kk == 0)
            def _():
                acc_ref[...] = part

            @pl.when(jnp.logical_and(kk > 0, kk < nk - 1))
            def _():
                acc_ref[...] += part

            @pl.when(kk == nk - 1)
            def _():
                finish(acc_ref[...] + part)

    in_specs = [a_spec, b_spec] + [c_spec] * len(extras)
    args = [a, b] + extras
    if has_gain:
        in_specs.append(pl.BlockSpec((1, tn), lambda i, j, kk: (0, j)))
        args.append(rms_gain)
    if second:
        out_shape = [out_shape, jax.ShapeDtypeStruct((m, n), BF16)]
        o_spec = [o_spec, c_spec]
    return pl.pallas_call(
        body, name=name,
        out_shape=out_shape,
        grid=(m // tm, n // tn, nk),
        in_specs=in_specs, out_specs=o_spec,
        scratch_shapes=[pltpu.VMEM((tm, tn), F32)] if nk > 1 else [],
        compiler_params=_params(("parallel", "parallel", "arbitrary")),
    )(*args)


def make_mm(name, out_dtype, with_res=False, blocked=False):
    def bwd_mm(a, w, g):
        da = matmul(g, w, "nt", a.dtype, name + "_da", blocked=blocked)
        dw = matmul(a, g, "tn", w.dtype, name + "_dw", blocked=blocked)
        return da, dw

    if with_res:
        @jax.custom_vjp
        def op(res, a, w):
            return matmul(a, w, "nn", out_dtype, name + "_f", res=res, blocked=blocked)

        def fwd(res, a, w):
            return op(res, a, w), (a, w)

        def bwd(saved, g):
            return (g,) + bwd_mm(*saved, g)
    else:
        @jax.custom_vjp
        def op(a, w):
            return matmul(a, w, "nn", out_dtype, name + "_f", blocked=blocked)

        def fwd(a, w):
            return op(a, w), (a, w)

        def bwd(saved, g):
            return bwd_mm(*saved, g)
    op.defvjp(fwd, bwd)
    return op


def _rms_fan_bwd(x_new, gain, dx, dh, name):
    rows, d = x_new.shape
    outs = [("row", d, F32), ("row", d, BF16)]
    return tile_bwd(fn_fan_rms, name, ["row", "par"], [x_new, gain], [True, True], outs, [dx, dh],
                    rows, min(512, rows), 0)


def make_mm_res_rms(name):
    @jax.custom_vjp
    def op(res, a, w, gain):
        return tuple(matmul(a, w, "nn", F32, name + "_f", res=res, rms_gain=gain))

    def fwd(res, a, w, gain):
        x_new, h = op(res, a, w, gain)
        return (x_new, h), (a, w, x_new, gain)

    def bwd(saved, cts):
        a, w, x_new, gain = saved
        dx, dgain = _rms_fan_bwd(x_new, gain, cts[0], cts[1], name + "_nb")
        da = matmul(dx, w, "nt", a.dtype, name + "_da")
        dw = matmul(a, dx, "tn", w.dtype, name + "_dw")
        return dx, da, dw, dgain

    op.defvjp(fwd, bwd)
    return op


def make_mlp(name, with_norm):
    def run(x, h, w1, w2, gain):
        a, bsq = matmul(h, w1, "nn", BF16, name + "_1_f", blocked=True, relu2_out=True)
        out = matmul(bsq, w2, "nn", F32, name + "_2_f", res=x, rms_gain=gain if with_norm else None)
        return (tuple(out) if with_norm else (out,)), a, bsq

    @jax.custom_vjp
    def op(x, h, w1, w2, gain):
        return run(x, h, w1, w2, gain)[0]

    def fwd(x, h, w1, w2, gain):
        out, a, bsq = run(x, h, w1, w2, gain)
        return out, (h, w1, w2, gain, a, bsq, out[0])

    def bwd(saved, cts):
        h, w1, w2, gain, a, bsq, x_new = saved
        if with_norm:
            dx, dgain = _rms_fan_bwd(x_new, gain, cts[0], cts[1], name + "_nb")
        else:
            dx, dgain = cts[0], None
        da = matmul(dx, w2, "nt", BF16, name + "_2_da", relu_gate=a)
        dw2 = matmul(bsq, dx, "tn", w2.dtype, name + "_2_dw")
        dh = matmul(da, w1, "nt", h.dtype, name + "_1_da", blocked=True)
        dw1 = matmul(h, da, "tn", w1.dtype, name + "_1_dw", blocked=True)
        return dx, dh, dw1, dw2, dgain

    op.defvjp(fwd, bwd)
    return op


def _kind(k):
    if isinstance(k, str):
        return k, None, 1
    return k[0], k[1], (k[2] if len(k) > 2 else 1)


def _tile_spec(kind, shape, tm, heads):
    k, d, ns = _kind(kind)
    if k == "row":
        return pl.BlockSpec((tm, shape[1]), (lambda h, i: (i, 0)) if heads else (lambda i: (i, 0)))
    if k == "par":
        return pl.BlockSpec(tuple(shape), (lambda h, i: (0, 0)) if heads else (lambda i: (0, 0)))
    if k == "rowh":
        return pl.BlockSpec((tm, d * ns), lambda h, i: (i, h))
    if k == "parh":
        return pl.BlockSpec((shape[0], d * ns), lambda h, i: (0, h))
    raise ValueError(kind)


def _tile_grid(rows, tm, heads):
    n_rows = rows // tm
    return ((heads, n_rows) if heads else (n_rows,)), (1 if heads else 0)


def _split_vals(kinds, refs):
    vals, counts = [], []
    for kind, r in zip(kinds, refs):
        _, d, ns = _kind(kind)
        v = r[...].astype(F32)
        vals += [v] if ns == 1 else [v[:, p * d:(p + 1) * d] for p in range(ns)]
        counts.append(ns)
    return vals, counts


def tile_fwd(fn, name, kinds, args, outs, rows, tm, heads, row_base=0):
    grid, row_axis = _tile_grid(rows, tm, heads)
    n_in = len(args)
    out_shapes = [jax.ShapeDtypeStruct((rows, w), dt) for (_, w, dt) in outs]

    def body(*refs):
        vals, _ = _split_vals(kinds, refs[:n_in])
        row0 = row_base + pl.program_id(row_axis) * tm
        res = list(fn(_PLAIN, row0, *vals))
        for o_ref, (k, _, _) in zip(refs[n_in:], outs):
            pieces = [res.pop(0) for _ in range(_kind(k)[2])]
            v = pieces[0] if len(pieces) == 1 else jnp.concatenate(pieces, axis=-1)
            o_ref[...] = v.astype(o_ref.dtype)

    return pl.pallas_call(
        body, name=name, out_shape=out_shapes, grid=grid,
        in_specs=[_tile_spec(k, a.shape, tm, heads) for k, a in zip(kinds, args)],
        out_specs=[_tile_spec(k, (rows, w), tm, heads) for (k, w, _) in outs],
        compiler_params=_params(("arbitrary",) * len(grid)),
    )(*args)


def tile_bwd(fn, name, kinds, args, diff, outs, cts, rows, tm, heads, row_base=0):
    grid, row_axis = _tile_grid(rows, tm, heads)
    n_in, n_ct = len(args), len(cts)
    diff_idx = [i for i, d in enumerate(diff) if d]
    g_shapes, g_specs = [], []
    for i in diff_idx:
        k = _kind(kinds[i])[0]
        dt = args[i].dtype if k in ("row", "rowh") else F32
        g_shapes.append(jax.ShapeDtypeStruct(args[i].shape, dt))
        g_specs.append(_tile_spec(kinds[i], args[i].shape, tm, heads))

    def body(*refs):
        in_refs, ct_refs, g_refs = refs[:n_in], refs[n_in:n_in + n_ct], refs[n_in + n_ct:]
        vals, counts = _split_vals(kinds, in_refs)
        first_piece = [sum(counts[:i]) for i in range(n_in)]
        flat_diff = [first_piece[i] + p for i in diff_idx for p in range(counts[i])]
        row_id = pl.program_id(row_axis)
        row0 = row_base + row_id * tm

        def f(*dvals):
            full = list(vals)
            for i, dv in zip(flat_diff, dvals):
                full[i] = dv
            return tuple(fn(_DIFF, row0, *full))

        _, vjp = jax.vjp(f, *[vals[i] for i in flat_diff])
        ct_vals, _ = _split_vals([k for (k, _, _) in outs], ct_refs)
        flat_grads = list(vjp(tuple(ct_vals)))
        for g_ref, i in zip(g_refs, diff_idx):
            pieces = [flat_grads.pop(0) for _ in range(counts[i])]
            g = pieces[0] if len(pieces) == 1 else jnp.concatenate(pieces, axis=-1)
            k = _kind(kinds[i])[0]
            if k in ("row", "rowh"):
                g_ref[...] = g.astype(g_ref.dtype)
            else:
                first = row_id == 0
                if heads and k == "par":
                    first = jnp.logical_and(first, pl.program_id(0) == 0)

                @pl.when(first)
                def _(g_ref=g_ref, g=g):
                    g_ref[...] = g

                @pl.when(jnp.logical_not(first))
                def _(g_ref=g_ref, g=g):
                    g_ref[...] += g

    return pl.pallas_call(
        body, name=name, out_shape=g_shapes, grid=grid,
        in_specs=[_tile_spec(k, a.shape, tm, heads) for k, a in zip(kinds, args)]
        + [_tile_spec(k, (rows, w), tm, heads) for (k, w, _) in outs],
        out_specs=g_specs,
        compiler_params=_params(("arbitrary",) * len(grid)),
    )(*args, *cts)


def make_tile_op(fn, name, kinds, diff, outs, rows, tm, heads=0, row_base=0):
    tm = min(tm, rows)

    @jax.custom_vjp
    def op(*args):
        return tuple(tile_fwd(fn, name + "_f", kinds, args, outs, rows, tm, heads, row_base))

    def fwd(*args):
        return op(*args), args

    def bwd(args, cts):
        grads = tile_bwd(fn, name + "_b", kinds, args, diff, outs, cts, rows, tm, heads, row_base)
        it = iter(grads)
        res = []
        for a, d in zip(args, diff):
            res.append(next(it).astype(a.dtype) if d else None)
        return tuple(res)

    op.defvjp(fwd, bwd)
    return op


def _rms(x, g):
    return x * lax.rsqrt(jnp.mean(x * x, axis=-1, keepdims=True) + EPS) * g


def fn_rms(ops, row0, x, g):
    return (_rms(x, g),)


def fn_fan_rms(ops, row0, x, g):
    return x, _rms(x, g)


def fn_mul(ops, row0, a, b):
    return (a * b,)


def fn_mla_down(ops, row0, h, c, s, w_cq, w_ckv, w_kr, w_krs, g_q, g_kv):
    c_qn = _rms(ops.b.nn(h, w_cq), g_q)
    c_kvn = _rms(ops.b.nn(h, w_ckv), g_kv)
    return c_qn, c_kvn, ops.b.nn(h, w_kr) * c + ops.b.nn(h, w_krs) * s


def fn_mla_up(ops, row0, c_qn, c_kvn, c, s, w_qn, w_qr, w_qrs, w_kv):
    c_all = jnp.concatenate([c] * MLA_HEADS, axis=-1)
    s_all = jnp.concatenate([s] * MLA_HEADS, axis=-1)
    q_rope = ops.b.nn(c_qn, w_qr) * c_all + ops.b.nn(c_qn, w_qrs) * s_all
    return ops.b.nn(c_qn, w_qn), q_rope, ops.b.nn(c_kvn, w_kv)


def _softmax(s):
    m = lax.stop_gradient(jnp.max(s, axis=-1, keepdims=True))
    e = jnp.exp(s - m)
    return e / jnp.sum(e, axis=-1, keepdims=True)


def fn_xattn(ops, row0, q, k, v):
    s = ops.b.nt(q, k) * (X_HEAD_DIM ** -0.5)
    return (ops.b.nn(_softmax(s), v),)


def _silu(x):
    return x * jax.nn.sigmoid(x)


def fn_gdn_prep(ops, row0, *t):
    nh = len(t) // 3
    qs, ks, vs = [], [], []
    for qc, kc, vc in zip(t[:nh], t[nh:2 * nh], t[2 * nh:]):
        q, k = _silu(qc), _silu(kc)
        qs.append(q * lax.rsqrt(jnp.sum(q * q, -1, keepdims=True) + EPS) * (GDN_DK ** -0.5))
        ks.append(k * lax.rsqrt(jnp.sum(k * k, -1, keepdims=True) + EPS))
        vs.append(_silu(vc))
    return tuple(qs + ks + vs)


def fn_gdn_gates(ops, row0, ba, alog, dtb):
    width = GDN_HEADS * GDN_DK
    beta = jax.nn.sigmoid(ba)
    z = ba + dtb
    softplus = jnp.maximum(z, 0.0) + jnp.log1p(jnp.exp(-jnp.abs(z)))
    g = -jnp.exp(alog) * softplus
    r = lax.broadcasted_iota(jnp.int32, (LANES, width), 0)
    c = lax.broadcasted_iota(jnp.int32, (LANES, width), 1) // GDN_DK
    e_beta = (r == c).astype(F32)
    e_g = (r == c + GDN_HEADS).astype(F32)
    return ops.h.nn(beta, e_beta), ops.h.nn(g, e_g)


def fn_gdn_out(ops, row0, *t):
    nh = (len(t) - 1) // 2
    g = t[-1]
    return tuple(_rms(o, g) * _silu(gate) for o, gate in zip(t[:nh], t[nh:2 * nh]))


def fn_mla_attn(ops, row0, qn, qr, kn, v, kr):
    s = (ops.b.nt(qn, kn) + ops.b.nt(qr, kr)) * ((MLA_NOPE + MLA_ROPE) ** -0.5)
    rows = row0 + lax.broadcasted_iota(jnp.int32, s.shape, 0)
    cols = lax.broadcasted_iota(jnp.int32, s.shape, 1)
    s = jnp.where(rows >= cols, s, NEG_BIG)
    return (ops.b.nn(_softmax(s), v),)


def _shift_down(x, d, t_idx):
    if d == 0:
        return x
    return jnp.where(t_idx >= d, pltpu.roll(x, d, axis=0), 0.0)


def _shift_up(x, d, t_idx):
    if d == 0:
        return x
    n = x.shape[0]
    return jnp.where(t_idx < n - d, pltpu.roll(x, n - d, axis=0), 0.0)


def conv_fwd(x, w, name):
    s, c = x.shape
    kw = w.shape[0]
    tc = _pick(c, (256, 128))

    def body(x_ref, w_ref, y_ref):
        xv = x_ref[...]
        t_idx = lax.broadcasted_iota(jnp.int32, xv.shape, 0)
        acc = jnp.zeros_like(xv)
        for j in range(kw):
            acc = acc + w_ref[j:j + 1, :] * _shift_down(xv, kw - 1 - j, t_idx)
        y_ref[...] = acc

    return pl.pallas_call(
        body, name=name, out_shape=jax.ShapeDtypeStruct((s, c), F32), grid=(c // tc,),
        in_specs=[pl.BlockSpec((s, tc), lambda i: (0, i)), pl.BlockSpec((kw, tc), lambda i: (0, i))],
        out_specs=pl.BlockSpec((s, tc), lambda i: (0, i)),
        compiler_params=_params(("parallel",)),
    )(x, w)


def conv_bwd(x, w, dy, name):
    s, c = x.shape
    kw = w.shape[0]
    tc = _pick(c, (256, 128))

    def body(x_ref, w_ref, dy_ref, dx_ref, dw_ref):
        xv, dyv = x_ref[...], dy_ref[...]
        t_idx = lax.broadcasted_iota(jnp.int32, xv.shape, 0)
        dx = jnp.zeros_like(xv)
        for j in range(kw):
            d = kw - 1 - j
            dx = dx + w_ref[j:j + 1, :] * _shift_up(dyv, d, t_idx)
            dw_ref[j:j + 1, :] = jnp.sum(dyv * _shift_down(xv, d, t_idx), axis=0, keepdims=True)
        dx_ref[...] = dx

    return pl.pallas_call(
        body, name=name,
        out_shape=[jax.ShapeDtypeStruct((s, c), F32), jax.ShapeDtypeStruct((kw, c), F32)],
        grid=(c // tc,),
        in_specs=[pl.BlockSpec((s, tc), lambda i: (0, i)), pl.BlockSpec((kw, tc), lambda i: (0, i)),
                  pl.BlockSpec((s, tc), lambda i: (0, i))],
        out_specs=[pl.BlockSpec((s, tc), lambda i: (0, i)), pl.BlockSpec((kw, tc), lambda i: (0, i))],
        compiler_params=_params(("parallel",)),
    )(x, w, dy)


def make_conv(name):
    @jax.custom_vjp
    def op(x, w):
        return conv_fwd(x, w, name + "_f")

    def fwd(x, w):
        return op(x, w), (x, w)

    def bwd(saved, dy):
        dx, dw = conv_bwd(saved[0], saved[1], dy, name + "_b")
        return dx, dw

    op.defvjp(fwd, bwd)
    return op


def _gdn_consts():
    c, d = GDN_CHUNK, GDN_DK
    i = lax.broadcasted_iota(jnp.int32, (c, c), 0)
    j = lax.broadcasted_iota(jnp.int32, (c, c), 1)
    tri = i >= j
    return dict(
        tri=tri, strict=i > j,
        tri_f=tri.astype(F32),
        eye=(i == j).astype(F32),
        lane0=(lax.broadcasted_iota(jnp.int32, (c, d), 1) == 0).astype(F32),
        last_row=(lax.broadcasted_iota(jnp.int32, (c, d), 0) == c - 1).astype(F32),
    )


def _gdn_chunk(ops, q, k, v, g, beta, state):
    b, m, sel = ops.bb, ops.bm, ops.bs
    nh, c, d = q.shape[0], GDN_CHUNK, GDN_DK
    k_ = _gdn_consts()

    def per_head(a):
        return jnp.broadcast_to(a, (nh,) + a.shape)

    gc = sel.sel_nn(per_head(k_["tri_f"]), g)
    col = jnp.broadcast_to(jnp.sum(gc * k_["lane0"], axis=2, keepdims=True), (nh, c, c))
    row = sel.sel_nt(per_head(k_["lane0"]), gc)
    decay = jnp.where(k_["tri"], jnp.exp(jnp.where(k_["tri"], col - row, 0.0)), 0.0)
    kb = k * beta
    mm_ = jnp.where(k_["strict"], b.nt(kb, k) * decay, 0.0)
    p = -mm_
    t = k_["eye"] + p
    for _ in range(int(math.log2(GDN_CHUNK)) - 1):
        p = m.nn(p, p)
        t = t + m.nn(t, p)
    egc = jnp.exp(gc)
    u = b.nn(t, v * beta)
    w = b.nn(t, kb * egc)
    attn = b.nt(q, k) * decay
    v_new = u - b.nn(w, state)
    o = b.nn(q * egc, state) + b.nn(attn, v_new)
    g_last = jnp.sum(gc * k_["last_row"], axis=1, keepdims=True)
    new_state = (state * jnp.exp(jnp.broadcast_to(g_last, (nh, d, d)))
                 + b.tn(k * jnp.exp(jnp.broadcast_to(g_last, (nh, c, d)) - gc), v_new))
    return o, new_state


GDN_HEAD_GROUP = 8
GDN_TILE_CHUNKS = 4


def _heads_of(ref, rows, n_heads):
    d = GDN_DK
    return jnp.stack([ref[rows, h * d:(h + 1) * d] for h in range(n_heads)])


def _gdn_specs(s, reverse):
    d, hg = GDN_DK, GDN_HEAD_GROUP
    tile = min(GDN_TILE_CHUNKS * GDN_CHUNK, s)
    n_tiles = s // tile
    t_of = (lambda t: n_tiles - 1 - t) if reverse else (lambda t: t)
    seq = pl.BlockSpec((tile, hg * d), lambda grp, t: (t_of(t), grp))
    st = pl.BlockSpec((hg, tile // GDN_CHUNK, d, d), lambda grp, t: (grp, t_of(t), 0, 0))
    return seq, st, tile, n_tiles


def gdn_fwd(q, k, v, g, beta, name):
    s = q.shape[0]
    d, hg = GDN_DK, GDN_HEAD_GROUP
    seq, st, tile, n_tiles = _gdn_specs(s, False)

    def body(q_ref, k_ref, v_ref, g_ref, b_ref, o_ref, st_ref, state_scr):
        @pl.when(pl.program_id(1) == 0)
        def _():
            state_scr[...] = jnp.zeros_like(state_scr)

        def step(ci, carry):
            rows = pl.ds(pl.multiple_of(ci * GDN_CHUNK, GDN_CHUNK), GDN_CHUNK)
            state = state_scr[...]
            for h in range(hg):
                st_ref[h, ci] = state[h]
            o, new_state = _gdn_chunk(_PLAIN, *[_heads_of(r, rows, hg) for r in (q_ref, k_ref, v_ref, g_ref, b_ref)],
                                      state)
            for h in range(hg):
                o_ref[rows, h * d:(h + 1) * d] = o[h]
            state_scr[...] = new_state
            return carry

        lax.fori_loop(0, tile // GDN_CHUNK, step, 0)

    return pl.pallas_call(
        body, name=name,
        out_shape=[jax.ShapeDtypeStruct(q.shape, F32),
                   jax.ShapeDtypeStruct((GDN_HEADS, s // GDN_CHUNK, d, d), F32)],
        grid=(GDN_HEADS // hg, n_tiles), in_specs=[seq] * 5, out_specs=[seq, st],
        scratch_shapes=[pltpu.VMEM((hg, d, d), F32)],
        compiler_params=_params(("parallel", "arbitrary")),
    )(q, k, v, g, beta)


def gdn_bwd(q, k, v, g, beta, states, do, name):
    s = q.shape[0]
    d, hg = GDN_DK, GDN_HEAD_GROUP
    seq, st, tile, n_tiles = _gdn_specs(s, True)
    tile_chunks = tile // GDN_CHUNK

    def body(q_ref, k_ref, v_ref, g_ref, b_ref, st_ref, do_ref, dq_ref, dk_ref, dv_ref, dg_ref, db_ref, dstate_scr):
        @pl.when(pl.program_id(1) == 0)
        def _():
            dstate_scr[...] = jnp.zeros_like(dstate_scr)

        def step(it, carry):
            ci = tile_chunks - 1 - it
            rows = pl.ds(pl.multiple_of(ci * GDN_CHUNK, GDN_CHUNK), GDN_CHUNK)
            prim = [_heads_of(r, rows, hg) for r in (q_ref, k_ref, v_ref, g_ref, b_ref)]
            prim.append(jnp.stack([st_ref[h, ci] for h in range(hg)]))
            _, vjp = jax.vjp(functools.partial(_gdn_chunk, _DIFF), *prim)
            grads = vjp((_heads_of(do_ref, rows, hg), dstate_scr[...]))
            for g_ref_out, gr in zip((dq_ref, dk_ref, dv_ref, dg_ref, db_ref), grads[:5]):
                for h in range(hg):
                    g_ref_out[rows, h * d:(h + 1) * d] = gr[h]
            dstate_scr[...] = grads[5]
            return carry

        lax.fori_loop(0, tile_chunks, step, 0)

    return pl.pallas_call(
        body, name=name,
        out_shape=[jax.ShapeDtypeStruct(q.shape, F32)] * 5,
        grid=(GDN_HEADS // hg, n_tiles), in_specs=[seq] * 5 + [st, seq], out_specs=[seq] * 5,
        scratch_shapes=[pltpu.VMEM((hg, d, d), F32)],
        compiler_params=_params(("parallel", "arbitrary")),
    )(q, k, v, g, beta, states, do)


def make_gdn(name):
    @jax.custom_vjp
    def op(q, k, v, g, beta):
        return gdn_fwd(q, k, v, g, beta, name + "_f")[0]

    def fwd(q, k, v, g, beta):
        o, states = gdn_fwd(q, k, v, g, beta, name + "_f")
        return o, (q, k, v, g, beta, states)

    def bwd(saved, do):
        return tuple(gdn_bwd(*saved, do, name + "_b"))

    op.defvjp(fwd, bwd)
    return op


def loss_head(x, g, target, name):
    s, d = x.shape
    tm = min(256, s)

    def body(x_ref, g_ref, t_ref, loss_ref, dx_ref, dg_ref):
        tgt = t_ref[...]

        def f(xv, gv):
            err = _rms(xv, gv) - tgt
            per_row = jnp.mean(err * err, axis=-1, keepdims=True)
            return 0.5 * jnp.sum(per_row, axis=0, keepdims=True)

        val, vjp = jax.vjp(f, x_ref[...], g_ref[...])
        dx, dg = vjp(jnp.ones((1, 1), F32))
        dx_ref[...] = dx
        first = pl.program_id(0) == 0

        @pl.when(first)
        def _():
            dg_ref[...] = dg
            loss_ref[...] = jnp.broadcast_to(val, loss_ref.shape)

        @pl.when(jnp.logical_not(first))
        def _():
            dg_ref[...] += dg
            loss_ref[...] += jnp.broadcast_to(val, loss_ref.shape)

    row = pl.BlockSpec((tm, d), lambda i: (i, 0))
    vec = pl.BlockSpec((1, d), lambda i: (0, 0))
    return pl.pallas_call(
        body, name=name,
        out_shape=[jax.ShapeDtypeStruct((1, LANES), F32), jax.ShapeDtypeStruct((s, d), F32),
                   jax.ShapeDtypeStruct((1, d), F32)],
        grid=(s // tm,), in_specs=[row, vec, row],
        out_specs=[pl.BlockSpec((1, LANES), lambda i: (0, 0)), row, vec],
        compiler_params=_params(("arbitrary",)),
    )(x, g, target)


def adamw(g8, w, m, v, layer, prev, name):
    n_layers, rows, width = w.shape
    tr = _pick(rows, (256, 128, 64, 32, 16, 8))

    def body(g_ref, w_ref, m_ref, v_ref, *rest):
        go_ref, d_ref, mo_ref, vo_ref = rest[-4:]
        g = g_ref[0].astype(F32)
        for p in range(1, N_DEV):
            g = g + g_ref[p].astype(F32)
        m_new = ADAM_B1 * m_ref[...] + (1.0 - ADAM_B1) * g
        v_new = ADAM_B2 * v_ref[...] + (1.0 - ADAM_B2) * (g * g)
        m_hat = m_new / (1.0 - ADAM_B1 ** ADAM_STEP)
        v_hat = v_new / (1.0 - ADAM_B2 ** ADAM_STEP)
        go_ref[...] = g
        d_ref[...] = -ADAM_LR * (m_hat / (jnp.sqrt(v_hat) + ADAM_EPS) + ADAM_WD * w_ref[...])
        mo_ref[...] = m_new
        vo_ref[...] = v_new

    blk = pl.BlockSpec((None, tr, width), lambda i: (layer, i, 0))
    carried = list(prev) if prev is not None else []
    return pl.pallas_call(
        body, name=name, out_shape=[jax.ShapeDtypeStruct((n_layers, rows, width), F32)] * 4,
        grid=(rows // tr,),
        in_specs=[pl.BlockSpec((N_DEV, tr, width), lambda i: (0, i, 0)), blk, blk, blk]
        + [pl.BlockSpec(memory_space=pl.ANY)] * len(carried),
        out_specs=[blk] * 4,
        input_output_aliases={4 + j: j for j in range(len(carried))},
        compiler_params=_params(("parallel",)),
    )(g8, w, m, v, *carried)


_HBM = pl.BlockSpec(memory_space=pltpu.HBM)
_SEM = pl.BlockSpec(memory_space=pltpu.SEMAPHORE)
_EFFECT = pltpu.SideEffectType.DATAFLOW_SIDE_EFFECTING


def _exchange_copies(mode, src_refs, land_refs, send_sems, recv_sems, local_sems):
    x, y, c = lax.axis_index("x"), lax.axis_index("y"), lax.axis_index("c")
    me = 4 * x + 2 * y + c
    n = len(src_refs)

    def src(k, p):
        return src_refs[k] if mode == "gather" else src_refs[k].at[p]

    local = [pltpu.make_async_copy(src(k, me), land_refs[k].at[me], local_sems.at[k]) for k in range(n)]
    sends, recvs = [], []
    for k in range(n):
        for r in range(1, N_DEV):
            px = (1 - x) if r & 4 else x
            py = (1 - y) if r & 2 else y
            pc = (1 - c) if r & 1 else c
            p = 4 * px + 2 * py + pc
            sem = k * (N_DEV - 1) + r - 1
            sends.append(pltpu.make_async_remote_copy(
                src_ref=src(k, p), dst_ref=land_refs[k].at[me],
                send_sem=send_sems.at[sem], recv_sem=recv_sems.at[sem],
                device_id=(px, py, pc), device_id_type=pl.DeviceIdType.MESH))
            recvs.append(pltpu.make_async_remote_copy(
                src_ref=src(k, p), dst_ref=land_refs[k].at[p],
                send_sem=send_sems.at[sem], recv_sem=recv_sems.at[sem],
                device_id=(px, py, pc), device_id_type=pl.DeviceIdType.MESH))
    return local, sends, recvs


def exchange_start(mode, arrays, name, carry=()):
    n, nc = len(arrays), len(carry)
    land_shapes = [((N_DEV,) + tuple(a.shape)) if mode == "gather" else tuple(a.shape) for a in arrays]
    lands = [pltpu.with_memory_space_constraint(lax.empty(shp, a.dtype), pltpu.HBM)
             for shp, a in zip(land_shapes, arrays)]
    srcs = [pltpu.with_memory_space_constraint(a, pltpu.HBM) for a in arrays]
    carried = [pltpu.with_memory_space_constraint(a, pltpu.HBM) for a in carry]

    def body(*refs):
        src_refs, land_refs = refs[:n], refs[n:2 * n]
        first_out = 2 * n + nc
        send_sems, recv_sems, local_sems = refs[first_out:first_out + 3]
        token = refs[-1]
        local, sends, _ = _exchange_copies(mode, src_refs, land_refs, send_sems, recv_sems, local_sems)
        for cp in local + sends:
            cp.start()
        token[...] = jnp.zeros_like(token)

    n_sem = n * (N_DEV - 1)
    out = pl.pallas_call(
        body, name=name,
        out_shape=(pltpu.SemaphoreType.DMA((n_sem,)), pltpu.SemaphoreType.DMA((n_sem,)),
                   pltpu.SemaphoreType.DMA((n,)),
                   *[pltpu.HBM(a.shape, a.dtype) for a in arrays],
                   *[pltpu.HBM(shp, a.dtype) for shp, a in zip(land_shapes, arrays)],
                   *[pltpu.HBM(a.shape, a.dtype) for a in carry],
                   jax.ShapeDtypeStruct((8, LANES), F32)),
        in_specs=[_HBM] * (2 * n + nc),
        out_specs=(_SEM, _SEM, _SEM, *[_HBM] * (2 * n + nc), pl.BlockSpec(memory_space=pltpu.VMEM)),
        input_output_aliases={i: 3 + i for i in range(2 * n + nc)},
        compiler_params=pltpu.CompilerParams(has_side_effects=_EFFECT),
    )(*srcs, *lands, *carried)
    handle = dict(mode=mode, sems=out[:3], srcs=out[3:3 + n], lands=out[3 + n:3 + 2 * n])
    return handle, out[-1], list(out[3 + 2 * n:3 + 2 * n + nc])


def exchange_wait(handle, after, name):
    mode, srcs, lands = handle["mode"], list(handle["srcs"]), list(handle["lands"])
    n = len(srcs)

    def body(*refs):
        src_refs, land_refs = refs[:n], refs[n:2 * n]
        send_sems, recv_sems, local_sems = refs[2 * n:2 * n + 3]
        local, sends, recvs = _exchange_copies(mode, src_refs, land_refs, send_sems, recv_sems, local_sems)
        for cp in sends:
            cp.wait_send()
        for cp in recvs:
            cp.wait_recv()
        for cp in local:
            cp.wait()

    out = pl.pallas_call(
        body, name=name,
        out_shape=(*[pltpu.HBM(a.shape, a.dtype) for a in srcs], *[pltpu.HBM(a.shape, a.dtype) for a in lands]),
        in_specs=[_HBM] * (2 * n) + [_SEM] * 3 + [pl.BlockSpec(memory_space=pl.ANY)],
        out_specs=tuple([_HBM] * (2 * n)),
        input_output_aliases={i: i for i in range(2 * n)},
        compiler_params=pltpu.CompilerParams(has_side_effects=_EFFECT),
    )(*srcs, *lands, *handle["sems"], after)
    return list(out[n:])


_ICI_RELATIONS = (2, 4, 6)


def _mesh_place():
    x, y, c = lax.axis_index("x"), lax.axis_index("y"), lax.axis_index("c")

    def peer(r):
        px = (1 - x) if r & 4 else x
        py = (1 - y) if r & 2 else y
        pc = (1 - c) if r & 1 else c
        return (px, py, pc), 4 * px + 2 * py + pc

    return 4 * x + 2 * y + c, peer


def _remote(src, dst, send_sem, recv_sem, device):
    return pltpu.make_async_remote_copy(src_ref=src, dst_ref=dst, send_sem=send_sem, recv_sem=recv_sem,
                                        device_id=device, device_id_type=pl.DeviceIdType.MESH)


def gather2_start(groups, name):
    flat = [a for g in groups for a in g]
    n = len(flat)
    lands = [pltpu.with_memory_space_constraint(lax.empty((N_DEV,) + tuple(a.shape), a.dtype), pltpu.HBM) for a in flat]
    srcs = [pltpu.with_memory_space_constraint(a, pltpu.HBM) for a in flat]
    n_rel = 1 + len(_ICI_RELATIONS)

    def body(*refs):
        src_refs, land_refs = refs[:n], refs[n:2 * n]
        sem_refs = refs[2 * n:2 * n + 4 * len(groups)]
        me, peer = _mesh_place()
        k = 0
        for gi, g in enumerate(groups):
            send_sems, recv_sib, recv_ici, local_sems = sem_refs[4 * gi:4 * gi + 4]
            for j in range(len(g)):
                pltpu.make_async_copy(src_refs[k], land_refs[k].at[me], local_sems.at[j]).start()
                dev, _ = peer(1)
                _remote(src_refs[k], land_refs[k].at[me], send_sems.at[n_rel * j], recv_sib.at[j], dev).start()
                for t, r in enumerate(_ICI_RELATIONS):
                    dev, _ = peer(r)
                    _remote(src_refs[k], land_refs[k].at[me], send_sems.at[n_rel * j + 1 + t],
                            recv_ici.at[len(_ICI_RELATIONS) * j + t], dev).start()
                k += 1
        refs[-1][...] = jnp.zeros_like(refs[-1])

    sem_shapes = []
    for g in groups:
        sem_shapes += [pltpu.SemaphoreType.DMA((n_rel * len(g),)), pltpu.SemaphoreType.DMA((len(g),)),
                       pltpu.SemaphoreType.DMA((len(_ICI_RELATIONS) * len(g),)), pltpu.SemaphoreType.DMA((len(g),))]
    out = pl.pallas_call(
        body, name=name,
        out_shape=(*sem_shapes, *[pltpu.HBM(a.shape, a.dtype) for a in flat],
                   *[pltpu.HBM((N_DEV,) + tuple(a.shape), a.dtype) for a in flat],
                   jax.ShapeDtypeStruct((8, LANES), F32)),
        in_specs=[_HBM] * (2 * n),
        out_specs=(*[_SEM] * len(sem_shapes), *[_HBM] * (2 * n), pl.BlockSpec(memory_space=pltpu.VMEM)),
        input_output_aliases={i: len(sem_shapes) + i for i in range(2 * n)},
        compiler_params=pltpu.CompilerParams(has_side_effects=_EFFECT),
    )(*srcs, *lands)
    handles, k, base = [], 0, len(sem_shapes)
    for gi, g in enumerate(groups):
        handles.append(dict(sems=out[4 * gi:4 * gi + 4], srcs=out[base + k:base + k + len(g)],
                            lands=out[base + n + k:base + n + k + len(g)]))
        k += len(g)
    return handles, out[-1]


def gather2_forward(handle, after, name, carry=()):
    lands, nc = list(handle["lands"]), len(carry)
    n, n_ici = len(lands), len(_ICI_RELATIONS)
    carried = [pltpu.with_memory_space_constraint(a, pltpu.HBM) for a in carry]

    def body(*refs):
        land_refs = refs[:n]
        recv_ici = refs[n + nc]
        fwd_send, fwd_recv = refs[n + nc + 2], refs[n + nc + 3]
        me, peer = _mesh_place()
        sibling, _ = peer(1)
        for j in range(n):
            for t, r in enumerate(_ICI_RELATIONS):
                dev, p = peer(r)
                landed = land_refs[j].at[p]
                _remote(landed, landed, fwd_send.at[n_ici * j + t], recv_ici.at[n_ici * j + t], dev).wait_recv()
                _remote(landed, landed, fwd_send.at[n_ici * j + t], fwd_recv.at[n_ici * j + t], sibling).start()

    out = pl.pallas_call(
        body, name=name,
        out_shape=(pltpu.SemaphoreType.DMA((n_ici * n,)), pltpu.SemaphoreType.DMA((n_ici * n,)),
                   *[pltpu.HBM(a.shape, a.dtype) for a in lands], *[pltpu.HBM(a.shape, a.dtype) for a in carry]),
        in_specs=[_HBM] * (n + nc) + [_SEM, pl.BlockSpec(memory_space=pl.ANY)],
        out_specs=(_SEM, _SEM, *[_HBM] * (n + nc)),
        input_output_aliases={i: 2 + i for i in range(n + nc)},
        compiler_params=pltpu.CompilerParams(has_side_effects=_EFFECT),
    )(*lands, *carried, handle["sems"][2], after)
    new_handle = dict(sems=handle["sems"], srcs=handle["srcs"], lands=out[2:2 + n], fwd=out[:2])
    return new_handle, list(out[2 + n:])


def gather2_wait(handle, after, name):
    srcs, lands = list(handle["srcs"]), list(handle["lands"])
    n, n_ici = len(srcs), len(_ICI_RELATIONS)
    n_rel = 1 + n_ici
    send_all, recv_sibling, _, local_all = handle["sems"]

    def body(*refs):
        src_refs, land_refs = refs[:n], refs[n:2 * n]
        send_sems, recv_sib, local_sems, fwd_send, fwd_recv = refs[2 * n:2 * n + 5]
        me, peer = _mesh_place()
        sibling, sib = peer(1)
        for j in range(n):
            pltpu.make_async_copy(src_refs[j], land_refs[j].at[me], local_sems.at[j]).wait()
            _remote(src_refs[j], land_refs[j].at[sib], send_sems.at[n_rel * j], recv_sib.at[j], sibling).wait()
            for t, r in enumerate(_ICI_RELATIONS):
                dev, p = peer(r)
                _remote(src_refs[j], land_refs[j].at[me], send_sems.at[n_rel * j + 1 + t],
                        recv_sib.at[j], dev).wait_send()
                _, p_sib = peer(r ^ 1)
                _remote(land_refs[j].at[p], land_refs[j].at[p_sib], fwd_send.at[n_ici * j + t],
                        fwd_recv.at[n_ici * j + t], sibling).wait()

    out = pl.pallas_call(
        body, name=name,
        out_shape=(*[pltpu.HBM(a.shape, a.dtype) for a in srcs], *[pltpu.HBM(a.shape, a.dtype) for a in lands]),
        in_specs=[_HBM] * (2 * n) + [_SEM] * 5 + [pl.BlockSpec(memory_space=pl.ANY)],
        out_specs=tuple([_HBM] * (2 * n)),
        input_output_aliases={i: i for i in range(2 * n)},
        compiler_params=pltpu.CompilerParams(has_side_effects=_EFFECT),
    )(*srcs, *lands, send_all, recv_sibling, local_all, *handle["fwd"], after)
    return list(out[n:])


BIG = ["mla_w_in", "mla_w_uq", "mla_w_ukv", "mla_w_o", "gdn_w_in", "gdn_w_o", "sc_w_in", "sc_w_o",
       "xa_w_q", "xa_w_kv", "xa_w_o", "mlp_w1", "mlp_w2"]
TINY = [("mla_q_norm", 1), ("mla_kv_norm", 1), ("gdn_conv_w", 2), ("sc_conv_w", 2)]
REPL = ["gdn_a_log", "gdn_dt_bias", "gdn_o_norm", "norm_mix", "norm_mem", "norm_mlp", "mem_norm", "final_norm"]
WEIGHTS = ["mla_w_in", "mla_q_norm", "mla_kv_norm", "mla_w_uq", "mla_w_ukv", "mla_w_o", "gdn_w_in",
           "gdn_conv_w", "gdn_a_log", "gdn_dt_bias", "gdn_o_norm", "gdn_w_o", "sc_w_in", "sc_conv_w",
           "sc_w_o", "norm_mix", "norm_mem", "norm_mlp", "xa_w_q", "xa_w_kv", "xa_w_o", "mlp_w1",
           "mlp_w2", "mem_norm", "final_norm"]
MIXER_WEIGHTS = (["mla_w_in", "mla_w_uq", "mla_w_ukv", "mla_w_o"], ["gdn_w_in", "gdn_w_o"], ["sc_w_in", "sc_w_o"])
MIXER_PARAMS = (["norm_mem", "mla_q_norm", "mla_kv_norm"],
                ["norm_mem", "gdn_conv_w", "gdn_a_log", "gdn_dt_bias", "gdn_o_norm"],
                ["norm_mem", "sc_conv_w"])


def from_shards(a8, axis):
    a = jnp.moveaxis(a8, 0, axis)
    shp = a.shape
    return a.reshape(shp[:axis] + (shp[axis] * shp[axis + 1],) + shp[axis + 2:])


def pack_rows(flat_list, width, row_mult):
    total = sum(a.shape[-1] for a in flat_list)
    rows = -(-total // width)
    rows = -(-rows // row_mult) * row_mult
    pad = rows * width - total
    parts = list(flat_list)
    if pad:
        parts.append(jnp.zeros((pad,), flat_list[0].dtype))
    return jnp.concatenate(parts, axis=-1).reshape(rows, width)


def unpack_rows(packed, shapes):
    lead = packed.shape[:-2]
    flat = packed.reshape(lead + (-1,))
    out, off = [], 0
    for shp in shapes:
        n = math.prod(shp)
        out.append(flat[..., off:off + n].reshape(lead + tuple(shp)))
        off += n
    return out


def _swap_halves(w):
    half = w.shape[-1] // 2
    return jnp.concatenate([w[..., half:], w[..., :half]], axis=-1)


def _pad_last(w, n):
    return jnp.pad(w, [(0, 0)] * (w.ndim - 1) + [(0, n - w.shape[-1])])


def _unblock(w8):
    return jnp.transpose(w8, (1, 0, 2)).reshape(w8.shape[1], -1)


def _stack_rows(w8):
    return w8.reshape(-1, w8.shape[-1])


def rms_op(name, rows, d, out_dtype):
    tm = rows if rows * d * 4 <= BLOCK_BYTES else 512
    return make_tile_op(fn_rms, name, ["row", "par"], [True, True], [("row", d, out_dtype)], rows, min(tm, rows))


def seg_memory(p, mem):
    return rms_op("rms_memory", mem.shape[0], mem.shape[1], BF16)(mem, p["mem_norm"].reshape(1, -1))[0]


def seg_mixer(i, wts, p, x, h, rope_c, rope_s):
    s, d = x.shape
    j, kind = i // N_MIXERS, i % N_MIXERS
    tag = f"l{i}"
    hd = MLA_NOPE
    next_gain = p["norm_mem"][i].reshape(1, d)
    if kind == 0:
        w_in = _stack_rows(wts["mla_w_in"])
        w_cq = w_in[:, :MLA_Q_RANK]
        w_ckv = w_in[:, MLA_Q_RANK:MLA_Q_RANK + MLA_KV_RANK]
        w_kr = w_in[:, MLA_Q_RANK + MLA_KV_RANK:]
        c_qn, c_kvn, k_rope = make_tile_op(
            fn_mla_down, tag + "_mla_down", ["row", "row", "row"] + ["par"] * 6, [True, False, False] + [True] * 6,
            [("row", MLA_Q_RANK, BF16), ("row", MLA_KV_RANK, BF16), ("row", hd, F32)], s, 256)(
            h, rope_c, rope_s, w_cq, w_ckv, _pad_last(w_kr, hd), _pad_last(_swap_halves(w_kr), hd),
            p["mla_q_norm"][j].reshape(1, -1), p["mla_kv_norm"][j].reshape(1, -1))
        w_uq8 = wts["mla_w_uq"]
        w_qn = _unblock(w_uq8[:, :, :MLA_NOPE])
        w_qr = w_uq8[:, :, MLA_NOPE:]
        w_qr_p = _unblock(_pad_last(w_qr, hd))
        w_qr_s = _unblock(_pad_last(_swap_halves(w_qr), hd))
        nq = MLA_HEADS * hd
        q_nope, q_rope, kv = make_tile_op(
            fn_mla_up, tag + "_mla_up", ["row", "row", "row", "row"] + ["par"] * 4, [True, True, False, False] + [True] * 4,
            [("row", nq, BF16), ("row", nq, F32), ("row", 2 * nq, BF16)], s, 256)(
            c_qn, c_kvn, rope_c, rope_s, w_qn, w_qr_p, w_qr_s, _unblock(wts["mla_w_ukv"]))
        n_groups = MLA_QUERY_GROUPS if s % (MLA_QUERY_GROUPS * 256) == 0 else 1
        rows_g = s // n_groups
        o_groups = []
        for grp in range(n_groups):
            r0, r1 = grp * rows_g, (grp + 1) * rows_g
            o_groups.append(make_tile_op(
                fn_mla_attn, f"{tag}_mla_attn{grp}", [("rowh", hd), ("rowh", hd), ("parh", hd, 2), "par"],
                [True] * 4, [(("rowh", hd), nq, BF16)], rows_g, 256, MLA_HEADS, row_base=r0)(
                q_nope[r0:r1], q_rope[r0:r1], kv[:r1], k_rope[:r1])[0])
        o = jnp.concatenate(o_groups, axis=0)
        return make_mm_res_rms(tag + "_mla_o")(x, o, _stack_rows(wts["mla_w_o"]), next_gain)
    if kind == 1:
        ng = GDN_HEADS * GDN_DK
        w_in = _unblock(wts["gdn_w_in"])
        cw = p["gdn_conv_w"][j]
        conv_out = []
        for part, nm in enumerate(("q", "k", "v")):
            cols = slice(part * ng, (part + 1) * ng)
            pre = make_mm(f"{tag}_gdn_in_{nm}", F32)(h, w_in[:, cols])
            conv_out.append(make_conv(f"{tag}_gdn_conv_{nm}")(pre, cw[:, cols]))
        gate = make_mm(tag + "_gdn_in_g", F32)(h, w_in[:, 3 * ng:4 * ng])
        ba = make_mm(tag + "_gdn_in_ba", F32)(h, _pad_last(w_in[:, 4 * ng:], LANES))
        heads_row = ("row", GDN_DK, GDN_HEADS)
        q, k, v = make_tile_op(fn_gdn_prep, tag + "_gdn_prep", [heads_row] * 3, [True] * 3,
                               [(heads_row, ng, F32)] * 3, s, 512)(*conv_out)
        alog = jnp.pad(p["gdn_a_log"][j].reshape(1, -1), ((0, 0), (GDN_HEADS, LANES - 2 * GDN_HEADS)))
        dtb = jnp.pad(p["gdn_dt_bias"][j].reshape(1, -1), ((0, 0), (GDN_HEADS, LANES - 2 * GDN_HEADS)))
        beta_b, g_b = make_tile_op(fn_gdn_gates, tag + "_gdn_gates", ["row", "par", "par"], [True] * 3,
                                   [("row", ng, F32)] * 2, s, 512)(ba, alog, dtb)
        o = make_gdn(tag + "_gdn_core")(q, k, v, g_b, beta_b)
        o = make_tile_op(fn_gdn_out, tag + "_gdn_out", [heads_row, heads_row, "par"],
                         [True] * 3, [(heads_row, ng, BF16)], s, 512)(
            o, gate, p["gdn_o_norm"][j].reshape(1, -1))[0]
        return make_mm_res_rms(tag + "_gdn_o")(x, o, _stack_rows(wts["gdn_w_o"]), next_gain)
    w_in = _unblock(wts["sc_w_in"])
    b_gate = make_mm(tag + "_sc_in_b", F32)(h, w_in[:, :d])
    c_gate = make_mm(tag + "_sc_in_c", F32)(h, w_in[:, d:2 * d])
    u = make_mm(tag + "_sc_in_u", F32)(h, w_in[:, 2 * d:])
    cu = make_tile_op(fn_mul, tag + "_sc_cu", ["row", "row"], [True, True], [("row", d, F32)], s, 512)(
        c_gate, u)[0]
    cv = make_conv(tag + "_sc_conv")(cu, p["sc_conv_w"][j])
    yv = make_tile_op(fn_mul, tag + "_sc_gate", ["row", "row"], [True, True], [("row", d, BF16)], s, 512)(
        b_gate, cv)[0]
    return make_mm_res_rms(tag + "_sc_o")(x, yv, _stack_rows(wts["sc_w_o"]), next_gain)


def seg_xattn(i, wts, p, x, hx, mem_n):
    s, d = x.shape
    tag = f"l{i}"
    q = make_mm(tag + "_xa_q", BF16)(hx, _stack_rows(wts["xa_w_q"]))
    kv = make_mm(tag + "_xa_kv", BF16, blocked=True)(mem_n, wts["xa_w_kv"])
    o = make_tile_op(fn_xattn, tag + "_xattn",
                     [("rowh", X_HEAD_DIM), ("parh", X_HEAD_DIM), ("parh", X_HEAD_DIM)], [True] * 3,
                     [(("rowh", X_HEAD_DIM), d, BF16)], s, 1024, X_HEADS)(q, kv[:, :d], kv[:, d:])[0]
    return make_mm_res_rms(tag + "_xa_o")(x, o, _stack_rows(wts["xa_w_o"]), p["norm_mlp"][i].reshape(1, d))


def seg_mlp(i, wts, p, x, hm):
    d = x.shape[1]
    gain = p["norm_mix"][i + 1].reshape(1, d) if i + 1 < DEPTH else None
    return make_mlp(f"l{i}_mlp", gain is not None)(x, hm, wts["mlp_w1"], _stack_rows(wts["mlp_w2"]), gain)


def segments():
    segs = []
    for i in range(DEPTH):
        j, kind = i // N_MIXERS, i % N_MIXERS
        segs.append((f"l{i}_mixer", [(n, j) for n in MIXER_WEIGHTS[kind]], MIXER_PARAMS[kind], "mixer"))
        segs.append((f"l{i}_xattn", [(n, i) for n in ("xa_w_q", "xa_w_kv", "xa_w_o")], ["norm_mlp"], "xattn"))
        segs.append((f"l{i}_mlp", [(n, i) for n in ("mlp_w1", "mlp_w2")], ["norm_mix"] if i + 1 < DEPTH else [],
                     "mlp"))
    return segs


def run_segment(index, kind, wts, p, x, h, mem_n, rope_c, rope_s):
    layer = index // 3
    if kind == "mixer":
        return seg_mixer(layer, wts, p, x, h, rope_c, rope_s)
    if kind == "xattn":
        return seg_xattn(layer, wts, p, x, h, mem_n)
    return seg_mlp(layer, wts, p, x, h)


def rope_tables(positions):
    inv_freq = ROPE_THETA ** (-jnp.arange(0, MLA_ROPE, 2, dtype=F32) / MLA_ROPE)
    ang = positions.astype(F32)[:, None] * inv_freq
    cos, sin = jnp.cos(ang), jnp.sin(ang)
    zeros = jnp.zeros((positions.shape[0], MLA_NOPE - MLA_ROPE), F32)
    return jnp.concatenate([cos, cos, zeros], axis=-1), jnp.concatenate([-sin, sin, zeros], axis=-1)


def kernel(x, mem, positions, mla_w_in, mla_q_norm, mla_kv_norm, mla_w_uq, mla_w_ukv, mla_w_o, gdn_w_in, gdn_conv_w, gdn_a_log, gdn_dt_bias, gdn_o_norm, gdn_w_o, sc_w_in, sc_conv_w, sc_w_o, norm_mix, norm_mem, norm_mlp, xa_w_q, xa_w_kv, xa_w_o, mlp_w1, mlp_w2, mem_norm, final_norm, loss_target, m_mla_w_in, m_mla_q_norm, m_mla_kv_norm, m_mla_w_uq, m_mla_w_ukv, m_mla_w_o, m_gdn_w_in, m_gdn_conv_w, m_gdn_a_log, m_gdn_dt_bias, m_gdn_o_norm, m_gdn_w_o, m_sc_w_in, m_sc_conv_w, m_sc_w_o, m_norm_mix, m_norm_mem, m_norm_mlp, m_xa_w_q, m_xa_w_kv, m_xa_w_o, m_mlp_w1, m_mlp_w2, m_mem_norm, m_final_norm, v_mla_w_in, v_mla_q_norm, v_mla_kv_norm, v_mla_w_uq, v_mla_w_ukv, v_mla_w_o, v_gdn_w_in, v_gdn_conv_w, v_gdn_a_log, v_gdn_dt_bias, v_gdn_o_norm, v_gdn_w_o, v_sc_w_in, v_sc_conv_w, v_sc_w_o, v_norm_mix, v_norm_mem, v_norm_mlp, v_xa_w_q, v_xa_w_kv, v_xa_w_o, v_mlp_w1, v_mlp_w2, v_mem_norm, v_final_norm):
    args = locals()
    w_loc = {n: args[n] for n in WEIGHTS}
    m_loc = {n: args["m_" + n] for n in WEIGHTS}
    v_loc = {n: args["v_" + n] for n in WEIGHTS}
    me = 4 * lax.axis_index("x") + 2 * lax.axis_index("y") + lax.axis_index("c")
    segs = segments()

    w16 = {n: w_loc[n].astype(BF16) for n in BIG}
    tiny_pack = pack_rows([w_loc[n].reshape(-1) for n, _ in TINY], LANES, 8)
    gather_handles, token = gather2_start(
        [[tiny_pack]] + [[w16[n][layer] for n, layer in units] for _, units, _, _ in segs], "gather_start")

    x_cur = x[0]
    rope_c, rope_s = rope_tables(positions[0])
    tiny_handle, _ = gather2_forward(gather_handles[0], token, "gather_forward_tiny")
    tiny_all = gather2_wait(tiny_handle, token, "gather_wait_tiny")[0]
    gather_handles = gather_handles[1:]
    params = {}
    for (n, ax), a8 in zip(TINY, unpack_rows(tiny_all, [w_loc[n].shape for n, _ in TINY])):
        params[n] = from_shards(a8, ax)
    for n in REPL:
        params[n] = w_loc[n]

    mem_n, vjp_memory = jax.vjp(lambda p_: seg_memory(p_, mem[0]), {"mem_norm": params["mem_norm"]})
    h_cur, vjp_first_norm = jax.vjp(
        lambda p_, x_: rms_op("l0_rms_mix", x_.shape[0], x_.shape[1], BF16)(x_, p_["norm_mix"][0].reshape(1, -1))[0],
        {"norm_mix": params["norm_mix"]}, x_cur)
    vjps = []
    forwarded, _ = gather2_forward(gather_handles[0], token, f"gather_forward_{segs[0][0]}")
    for index, (tag, units, p_names, kind) in enumerate(segs):
        landed = gather2_wait(forwarded, token if index == 0 else x_cur, f"gather_wait_{tag}")
        wts = {n: a for (n, _), a in zip(units, landed)}
        p_seg = {n: params[n] for n in p_names}
        if index + 1 < len(segs):
            forwarded, (p_seg[p_names[0]],) = gather2_forward(
                gather_handles[index + 1], landed[0], f"gather_forward_{segs[index + 1][0]}",
                carry=[p_seg[p_names[0]]])
        outs, vjp_seg = jax.vjp(
            lambda w_, p_, x_, h_, m_, index=index, kind=kind:
            run_segment(index, kind, w_, p_, x_, h_, m_, rope_c, rope_s),
            wts, p_seg, x_cur, h_cur, mem_n)
        x_cur, h_cur = outs[0], (outs[1] if len(outs) > 1 else None)
        vjps.append(vjp_seg)

    loss_vec, g_x, d_final = loss_head(x_cur, params["final_norm"].reshape(1, -1), loss_target[0], "loss_head")

    grads = {n: jnp.zeros_like(params[n]) for n in params}
    grads["final_norm"] = d_final.reshape(-1)
    g_mem_n = jnp.zeros_like(mem_n)
    g_h = None
    scatter_handles = []
    for (tag, units, _, _), vjp_seg in zip(reversed(segs), reversed(vjps)):
        g_wts, g_p, g_x, g_h, g_m = vjp_seg((g_x,) if g_h is None else (g_x, g_h))
        for n, g in g_p.items():
            grads[n] = grads[n] + g
        g_mem_n = g_mem_n + g_m
        handle, _, (g_h,) = exchange_start("scatter", [g_wts[n] for n, _ in units], f"scatter_start_{tag}",
                                           carry=[g_h])
        scatter_handles.append((units, handle))
    grads["mem_norm"] = grads["mem_norm"] + vjp_memory(g_mem_n)[0]["mem_norm"]
    g_first, g_x_norm = vjp_first_norm(g_h)
    grads["norm_mix"] = grads["norm_mix"] + g_first["norm_mix"]
    g_x = g_x + g_x_norm

    small_names = [n for n, _ in TINY] + REPL
    small_g = pack_rows([loss_vec[0, :1]] + [grads[n].astype(F32).reshape(-1) for n in small_names], PACK_W, 8)
    small_handle, _, _ = exchange_start("gather", [small_g], "gather_start_small_grads")

    g_recv = {}
    for units, handle in scatter_handles:
        landed = exchange_wait(handle, g_x, f"scatter_wait_{units[0][0]}_{units[0][1]}")
        g_recv.update(dict(zip(units, landed)))

    res = {}
    for n in BIG:
        outs = None
        for layer in range(w_loc[n].shape[0]):
            outs = adamw(g_recv[n, layer], w_loc[n], m_loc[n], v_loc[n], layer, outs, f"adamw_{n}_{layer}")
        for kind, a in zip(("grad", "delta", "m", "v"), outs):
            res[(kind, n)] = a
    small_recv = exchange_wait(small_handle, res[("grad", BIG[-1])], "gather_wait_small_grads")[0]

    def full_small(d):
        parts = [jnp.zeros((1,), F32)]
        for n, ax in TINY:
            full_shape = params[n].shape
            start = [0] * len(full_shape)
            start[ax] = me * d[n].shape[ax]
            parts.append(lax.dynamic_update_slice(jnp.zeros(full_shape, F32), d[n], start).reshape(-1))
        parts += [d[n].reshape(-1) for n in REPL]
        return pack_rows(parts, PACK_W, 8)

    outs_small = adamw(small_recv, full_small(w_loc)[None], full_small(m_loc)[None], full_small(v_loc)[None],
                       0, None, "adamw_small")
    small_shapes = [(1,)] + [params[n].shape for n, _ in TINY] + [w_loc[n].shape for n in REPL]
    loss = None
    for kind, packed in zip(("grad", "delta", "m", "v"), outs_small):
        parts = unpack_rows(packed[0], small_shapes)
        if kind == "grad":
            loss = parts[0][0]
        for (n, ax), a in zip(TINY, parts[1:1 + len(TINY)]):
            start = [0] * a.ndim
            start[ax] = me * w_loc[n].shape[ax]
            res[(kind, n)] = lax.dynamic_slice(a, start, w_loc[n].shape)
        for n, a in zip(REPL, parts[1 + len(TINY):]):
            res[(kind, n)] = a

    out = [loss, g_x[None]]
    for kind in ("grad", "delta", "m", "v"):
        out += [res[(kind, n)] for n in WEIGHTS]
    return tuple(out)
```

```python
import math

import jax
import jax.numpy as jnp
from jax import lax
from jax.experimental import pallas as pl
from jax.experimental.pallas import tpu as pltpu

F32 = jnp.float32
BF16 = jnp.bfloat16

N_DEV = 8
LANES = 128
EPS = 1e-6
ROPE_THETA = 10000.0
MLA_HEADS, MLA_NOPE, MLA_ROPE, MLA_V = 8, 128, 64, 128
MLA_Q_RANK, MLA_KV_RANK = 384, 256
GDN_HEADS, GDN_DK, GDN_CONV, GDN_CHUNK = 8, 128, 4, 64
X_HEADS, X_HEAD_DIM = 4, 256
DEPTH, N_MIXERS = 4, 3
ADAM_LR, ADAM_B1, ADAM_B2, ADAM_EPS, ADAM_WD, ADAM_STEP = 0.001, 0.9, 0.999, 1e-08, 0.01, 10
MLA_QUERY_GROUPS = 4
NEG_BIG = -1e30
PACK_W = 1024


_NN = (((1,), (0,)), ((), ()))
_NT = (((1,), (1,)), ((), ()))
_TN = (((0,), (0,)), ((), ()))
_NN3 = (((2,), (1,)), ((0,), (0,)))
_NT3 = (((2,), (2,)), ((0,), (0,)))
_TN3 = (((1,), (1,)), ((0,), (0,)))


def _dot(a, b, dims):
    return lax.dot_general(a, b, dims, preferred_element_type=F32)


def _hi_lo(x):
    hi = x.astype(BF16)
    return hi, (x - hi.astype(F32)).astype(BF16)


def _split3(x):
    hi = x.astype(BF16)
    r = x - hi.astype(F32)
    mid = r.astype(BF16)
    return hi, mid, (r - mid.astype(F32)).astype(BF16)


def _dg(a, b, dims, prec):
    if prec == "h":
        return lax.dot_general(a, b, dims, precision=lax.Precision.HIGHEST, preferred_element_type=F32)
    if prec == "m":
        a_hi, a_lo = _hi_lo(a)
        b_hi, b_lo = _hi_lo(b)
        return _dot(a_hi, b_hi, dims) + _dot(a_hi, b_lo, dims) + _dot(a_lo, b_hi, dims)
    return _dot(a.astype(BF16), b.astype(BF16), dims)


def _dg_sel(sel, x, dims, sel_first):
    s16 = sel.astype(BF16)
    parts = [(_dot(s16, piece, dims) if sel_first else _dot(piece, s16, dims)) for piece in _split3(x)]
    return parts[0] + parts[1] + parts[2]


class _Ops:
    def __init__(self, prec, differentiable, batched=False):
        d_nn, d_nt, d_tn = (_NN3, _NT3, _TN3) if batched else (_NN, _NT, _TN)

        def nn(a, b):
            return _dg(a, b, d_nn, prec)

        def nt(a, b):
            return _dg(a, b, d_nt, prec)

        def tn(a, b):
            return _dg(a, b, d_tn, prec)

        if differentiable:
            dnn = jax.custom_vjp(nn)
            dnn.defvjp(lambda a, b: (nn(a, b), (a, b)), lambda r, g: (nt(g, r[1]), tn(r[0], g)))
            dnt = jax.custom_vjp(nt)
            dnt.defvjp(lambda a, b: (nt(a, b), (a, b)), lambda r, g: (nn(g, r[1]), tn(g, r[0])))
            dtn = jax.custom_vjp(tn)
            dtn.defvjp(lambda a, b: (tn(a, b), (a, b)), lambda r, g: (nt(r[1], g), nn(r[0], g)))
            nn, nt, tn = dnn, dnt, dtn
        self.nn, self.nt, self.tn = nn, nt, tn


class _SelOps:
    def __init__(self, differentiable, batched=False):
        d_nn, d_nt, d_tn = (_NN3, _NT3, _TN3) if batched else (_NN, _NT, _TN)

        def sel_nn(sel, x):
            return _dg_sel(sel, x, d_nn, True)

        def sel_nt(sel, x):
            return _dg_sel(sel, x, d_nt, True)

        if differentiable:
            dnn = jax.custom_vjp(sel_nn)
            dnn.defvjp(lambda s, x: (sel_nn(s, x), s),
                       lambda s, g: (jnp.zeros_like(s), _dg_sel(s, g, d_tn, True)))
            dnt = jax.custom_vjp(sel_nt)
            dnt.defvjp(lambda s, x: (sel_nt(s, x), s),
                       lambda s, g: (jnp.zeros_like(s), _dg_sel(s, g, d_tn, False)))
            sel_nn, sel_nt = dnn, dnt
        self.sel_nn, self.sel_nt = sel_nn, sel_nt


class _OpSet:
    def __init__(self, differentiable):
        self.b = _Ops("b", differentiable)
        self.h = _Ops("h", differentiable)
        self.bb = _Ops("b", differentiable, batched=True)
        self.bm = _Ops("m", differentiable, batched=True)
        self.bs = _SelOps(differentiable, batched=True)


_PLAIN = _OpSet(False)
_DIFF = _OpSet(True)


def _params(sem):
    return pltpu.CompilerParams(dimension_semantics=sem)


BLOCK_BYTES = 4 * 1024 * 1024


def _pick(n, cands):
    for c in cands:
        if n % c == 0:
            return c
    return n


def _tile(n, cap):
    if n <= cap:
        return n
    return _pick(n, tuple(c for c in (2048, 1024, 768, 512, 384, 256, 128) if c <= cap))


def matmul(a, b, form, out_dtype, name, res=None, blocked=False, relu_gate=None, rms_gain=None, relu2_out=False):
    if form == "nn":
        m, k = a.shape
        k2, n = (b.shape[1], N_DEV * b.shape[2]) if blocked else b.shape
    elif form == "nt":
        m, k = a.shape
        n, k2 = (b.shape[1], N_DEV * b.shape[2]) if blocked else b.shape
    else:
        (k, m), (k2, n) = a.shape, b.shape
    assert k == k2, (a.shape, b.shape, form)
    tk = k if k <= 2048 else _tile(k, 1024)
    cb = nb = 1
    if blocked:
        cb = (k if form == "nt" else n) // N_DEV
        nb = _pick(N_DEV, tuple(c for c in (8, 4, 2, 1) if c * cb <= 1024))
    if blocked and form == "nt":
        tk = nb * cb
    if blocked and form != "nt":
        tn = nb * cb
    else:
        tn = _tile(n, min(1024, BLOCK_BYTES // (tk * b.dtype.itemsize)))
    out_elems = BLOCK_BYTES // 2 if (out_dtype == BF16 and res is None) else BLOCK_BYTES // 4
    tm = _tile(m, min(BLOCK_BYTES // (tk * a.dtype.itemsize), out_elems // tn))
    nk = k // tk
    dims = {"nn": _NN, "nt": _NT, "tn": _TN}[form]

    a_spec = {"nn": pl.BlockSpec((tm, tk), lambda i, j, kk: (i, kk)),
              "nt": pl.BlockSpec((tm, tk), lambda i, j, kk: (i, kk)),
              "tn": pl.BlockSpec((tk, tm), lambda i, j, kk: (kk, i))}[form]
    if blocked and form == "nn":
        b_spec = pl.BlockSpec((nb, tk, cb), lambda i, j, kk: (j, kk, 0))
    elif blocked and form == "nt":
        b_spec = pl.BlockSpec((nb, tn, cb), lambda i, j, kk: (kk, j, 0))
    else:
        b_spec = {"nn": pl.BlockSpec((tk, tn), lambda i, j, kk: (kk, j)),
                  "nt": pl.BlockSpec((tn, tk), lambda i, j, kk: (j, kk)),
                  "tn": pl.BlockSpec((tk, tn), lambda i, j, kk: (kk, j))}[form]
    c_spec = pl.BlockSpec((tm, tn), lambda i, j, kk: (i, j))
    out_shape = jax.ShapeDtypeStruct((m, n), out_dtype)
    o_spec = c_spec
    blocked_out = blocked and form == "tn"
    if blocked_out:
        out_shape = jax.ShapeDtypeStruct((N_DEV, m, cb), out_dtype)
        o_spec = pl.BlockSpec((nb, tm, cb), lambda i, j, kk: (j, i, 0))
    has_res, has_gate, has_gain = res is not None, relu_gate is not None, rms_gain is not None
    extras = [e for e in (res, relu_gate) if e is not None]
    n_in = 2 + len(extras) + has_gain
    second = has_gain or relu2_out
    assert not (second and (blocked_out or tn != n and has_gain))

    def body(*refs):
        a_ref, b_ref = refs[0], refs[1]
        r_ref = refs[2] if has_res else None
        gate_ref = refs[2 + has_res] if has_gate else None
        gain_ref = refs[n_in - 1] if has_gain else None
        o_ref = refs[n_in]
        a_val = a_ref[...].astype(BF16)
        if blocked and form == "nn":
            part = jnp.concatenate([_dot(a_val, b_ref[t].astype(BF16), dims) for t in range(nb)], axis=-1)
        elif blocked and form == "nt":
            part = _dot(a_val[:, :cb], b_ref[0].astype(BF16), dims)
            for t in range(1, nb):
                part = part + _dot(a_val[:, t * cb:(t + 1) * cb], b_ref[t].astype(BF16), dims)
        else:
            part = _dot(a_val, b_ref[...].astype(BF16), dims)

        def finish(acc):
            if has_res:
                acc = acc + r_ref[...].astype(F32)
            if has_gate:
                acc = acc * (2.0 * jnp.maximum(gate_ref[...].astype(F32), 0.0))
            if blocked_out:
                for t in range(nb):
                    o_ref[t] = acc[:, t * cb:(t + 1) * cb].astype(out_dtype)
            else:
                o_ref[...] = acc.astype(out_dtype)
            if has_gain:
                refs[n_in + 1][...] = _rms(acc, gain_ref[...]).astype(BF16)
            if relu2_out:
                r = jnp.maximum(acc.astype(out_dtype).astype(F32), 0.0)
                refs[n_in + 1][...] = (r * r).astype(BF16)

        if nk == 1:
            finish(part)
        else:
            acc_ref = refs[-1]
            kk = pl.program_id(2)

            @pl.when(kk == 0)
            def _():
                acc_ref[...] = part

            @pl.when(jnp.logical_and(kk > 0, kk < nk - 1))
            def _():
                acc_ref[...] += part

            @pl.when(kk == nk - 1)
            def _():
                finish(acc_ref[...] + part)

    in_specs = [a_spec, b_spec] + [c_spec] * len(extras)
    args = [a, b] + extras
    if has_gain:
        in_specs.append(pl.BlockSpec((1, tn), lambda i, j, kk: (0, j)))
        args.append(rms_gain)
    if second:
        out_shape = [out_shape, jax.ShapeDtypeStruct((m, n), BF16)]
        o_spec = [o_spec, c_spec]
    return pl.pallas_call(
        body, name=name,
        out_shape=out_shape,
        grid=(m // tm, n // tn, nk),
        in_specs=in_specs, out_specs=o_spec,
        scratch_shapes=[pltpu.VMEM((tm, tn), F32)] if nk > 1 else [],
        compiler_params=_params(("parallel", "parallel", "arbitrary")),
    )(*args)


def make_mm(name, out_dtype, with_res=False, blocked=False):
    def bwd_mm(a, w, g):
        da = matmul(g, w, "nt", a.dtype, name + "_da", blocked=blocked)
        dw = matmul(a, g, "tn", w.dtype, name + "_dw", blocked=blocked)
        return da, dw

    if with_res:
        @jax.custom_vjp
        def op(res, a, w):
            return matmul(a, w, "nn", out_dtype, name + "_f", res=res, blocked=blocked)

        def fwd(res, a, w):
            return op(res, a, w), (a, w)

        def bwd(saved, g):
            return (g,) + bwd_mm(*saved, g)
    else:
        @jax.custom_vjp
        def op(a, w):
            return matmul(a, w, "nn", out_dtype, name + "_f", blocked=blocked)

        def fwd(a, w):
            return op(a, w), (a, w)

        def bwd(saved, g):
            return bwd_mm(*saved, g)
    op.defvjp(fwd, bwd)
    return op


def _rms_fan_bwd(x_new, gain, dx, dh, name):
    rows, d = x_new.shape
    outs = [("row", d, F32), ("row", d, BF16)]
    return tile_bwd(fn_fan_rms, name, ["row", "par"], [x_new, gain], [True, True], outs, [dx, dh],
                    rows, min(512, rows), 0)


def make_mm_res_rms(name):
    @jax.custom_vjp
    def op(res, a, w, gain):
        return tuple(matmul(a, w, "nn", F32, name + "_f", res=res, rms_gain=gain))

    def fwd(res, a, w, gain):
        x_new, h = op(res, a, w, gain)
        return (x_new, h), (a, w, x_new, gain)

    def bwd(saved, cts):
        a, w, x_new, gain = saved
        dx, dgain = _rms_fan_bwd(x_new, gain, cts[0], cts[1], name + "_nb")
        da = matmul(dx, w, "nt", a.dtype, name + "_da")
        dw = matmul(a, dx, "tn", w.dtype, name + "_dw")
        return dx, da, dw, dgain

    op.defvjp(fwd, bwd)
    return op


def make_mlp(name, with_norm):
    def run(x, h, w1, w2, gain):
        a, bsq = matmul(h, w1, "nn", BF16, name + "_1_f", blocked=True, relu2_out=True)
        out = matmul(bsq, w2, "nn", F32, name + "_2_f", res=x, rms_gain=gain if with_norm else None)
        return (tuple(out) if with_norm else (out,)), a, bsq

    @jax.custom_vjp
    def op(x, h, w1, w2, gain):
        return run(x, h, w1, w2, gain)[0]

    def fwd(x, h, w1, w2, gain):
        out, a, bsq = run(x, h, w1, w2, gain)
        return out, (h, w1, w2, gain, a, bsq, out[0])

    def bwd(saved, cts):
        h, w1, w2, gain, a, bsq, x_new = saved
        if with_norm:
            dx, dgain = _rms_fan_bwd(x_new, gain, cts[0], cts[1], name + "_nb")
        else:
            dx, dgain = cts[0], None
        da = matmul(dx, w2, "nt", BF16, name + "_2_da", relu_gate=a)
        dw2 = matmul(bsq, dx, "tn", w2.dtype, name + "_2_dw")
        dh = matmul(da, w1, "nt", h.dtype, name + "_1_da", blocked=True)
        dw1 = matmul(h, da, "tn", w1.dtype, name + "_1_dw", blocked=True)
        return dx, dh, dw1, dw2, dgain

    op.defvjp(fwd, bwd)
    return op


def _kind(k):
    if isinstance(k, str):
        return k, None, 1
    return k[0], k[1], (k[2] if len(k) > 2 else 1)


def _tile_spec(kind, shape, tm, heads):
    k, d, ns = _kind(kind)
    if k == "row":
        return pl.BlockSpec((tm, shape[1]), (lambda h, i: (i, 0)) if heads else (lambda i: (i, 0)))
    if k == "par":
        return pl.BlockSpec(tuple(shape), (lambda h, i: (0, 0)) if heads else (lambda i: (0, 0)))
    if k == "rowh":
        return pl.BlockSpec((tm, d * ns), lambda h, i: (i, h))
    if k == "parh":
        return pl.BlockSpec((shape[0], d * ns), lambda h, i: (0, h))
    raise ValueError(kind)


def _tile_grid(rows, tm, heads):
    n_rows = rows // tm
    return ((heads, n_rows) if heads else (n_rows,)), (1 if heads else 0)


def _split_vals(kinds, refs):
    vals, counts = [], []
    for kind, r in zip(kinds, refs):
        _, d, ns = _kind(kind)
        v = r[...].astype(F32)
        vals += [v] if ns == 1 else [v[:, p * d:(p + 1) * d] for p in range(ns)]
        counts.append(ns)
    return vals, counts


def tile_fwd(fn, name, kinds, args, outs, rows, tm, heads, row_base=0):
    grid, row_axis = _tile_grid(rows, tm, heads)
    n_in = len(args)
    out_shapes = [jax.ShapeDtypeStruct((rows, w), dt) for (_, w, dt) in outs]

    def body(*refs):
        vals, _ = _split_vals(kinds, refs[:n_in])
        row0 = row_base + pl.program_id(row_axis) * tm
        res = list(fn(_PLAIN, row0, *vals))
        for o_ref, (k, _, _) in zip(refs[n_in:], outs):
            pieces = [res.pop(0) for _ in range(_kind(k)[2])]
            v = pieces[0] if len(pieces) == 1 else jnp.concatenate(pieces, axis=-1)
            o_ref[...] = v.astype(o_ref.dtype)

    return pl.pallas_call(
        body, name=name, out_shape=out_shapes, grid=grid,
        in_specs=[_tile_spec(k, a.shape, tm, heads) for k, a in zip(kinds, args)],
        out_specs=[_tile_spec(k, (rows, w), tm, heads) for (k, w, _) in outs],
        compiler_params=_params(("arbitrary",) * len(grid)),
    )(*args)


def tile_bwd(fn, name, kinds, args, diff, outs, cts, rows, tm, heads, row_base=0):
    grid, row_axis = _tile_grid(rows, tm, heads)
    n_in, n_ct = len(args), len(cts)
    diff_idx = [i for i, d in enumerate(diff) if d]
    g_shapes, g_specs = [], []
    for i in diff_idx:
        k = _kind(kinds[i])[0]
        dt = args[i].dtype if k in ("row", "rowh") else F32
        g_shapes.append(jax.ShapeDtypeStruct(args[i].shape, dt))
        g_specs.append(_tile_spec(kinds[i], args[i].shape, tm, heads))

    def body(*refs):
        in_refs, ct_refs, g_refs = refs[:n_in], refs[n_in:n_in + n_ct], refs[n_in + n_ct:]
        vals, counts = _split_vals(kinds, in_refs)
        first_piece = [sum(counts[:i]) for i in range(n_in)]
        flat_diff = [first_piece[i] + p for i in diff_idx for p in range(counts[i])]
        row_id = pl.program_id(row_axis)
        row0 = row_base + row_id * tm

        def f(*dvals):
            full = list(vals)
            for i, dv in zip(flat_diff, dvals):
                full[i] = dv
            return tuple(fn(_DIFF, row0, *full))

        _, vjp = jax.vjp(f, *[vals[i] for i in flat_diff])
        ct_vals, _ = _split_vals([k for (k, _, _) in outs], ct_refs)
        flat_grads = list(vjp(tuple(ct_vals)))
        for g_ref, i in zip(g_refs, diff_idx):
            pieces = [flat_grads.pop(0) for _ in range(counts[i])]
            g = pieces[0] if len(pieces) == 1 else jnp.concatenate(pieces, axis=-1)
            k = _kind(kinds[i])[0]
            if k in ("row", "rowh"):
                g_ref[...] = g.astype(g_ref.dtype)
            else:
                first = row_id == 0
                if heads and k == "par":
                    first = jnp.logical_and(first, pl.program_id(0) == 0)

                @pl.when(first)
                def _(g_ref=g_ref, g=g):
                    g_ref[...] = g

                @pl.when(jnp.logical_not(first))
                def _(g_ref=g_ref, g=g):
                    g_ref[...] += g

    return pl.pallas_call(
        body, name=name, out_shape=g_shapes, grid=grid,
        in_specs=[_tile_spec(k, a.shape, tm, heads) for k, a in zip(kinds, args)]
        + [_tile_spec(k, (rows, w), tm, heads) for (k, w, _) in outs],
        out_specs=g_specs,
        compiler_params=_params(("arbitrary",) * len(grid)),
    )(*args, *cts)


def make_tile_op(fn, name, kinds, diff, outs, rows, tm, heads=0, row_base=0):
    tm = min(tm, rows)

    @jax.custom_vjp
    def op(*args):
        return tuple(tile_fwd(fn, name + "_f", kinds, args, outs, rows, tm, heads, row_base))

    def fwd(*args):
        return op(*args), args

    def bwd(args, cts):
        grads = tile_bwd(fn, name + "_b", kinds, args, diff, outs, cts, rows, tm, heads, row_base)
        it = iter(grads)
        res = []
        for a, d in zip(args, diff):
            res.append(next(it).astype(a.dtype) if d else None)
        return tuple(res)

    op.defvjp(fwd, bwd)
    return op


def _rms(x, g):
    return x * lax.rsqrt(jnp.mean(x * x, axis=-1, keepdims=True) + EPS) * g


def fn_rms(ops, row0, x, g):
    return (_rms(x, g),)


def fn_fan_rms(ops, row0, x, g):
    return x, _rms(x, g)


def fn_mul(ops, row0, a, b):
    return (a * b,)


def fn_mla_down(ops, row0, h, c, s, w_cq, w_ckv, w_kr, w_krs, g_q, g_kv):
    c_qn = _rms(ops.b.nn(h, w_cq), g_q)
    c_kvn = _rms(ops.b.nn(h, w_ckv), g_kv)
    return c_qn, c_kvn, ops.b.nn(h, w_kr) * c + ops.b.nn(h, w_krs) * s


def fn_mla_up(ops, row0, c_qn, c_kvn, c, s, w_qn, w_qr, w_qrs, w_kv):
    c_all = jnp.concatenate([c] * MLA_HEADS, axis=-1)
    s_all = jnp.concatenate([s] * MLA_HEADS, axis=-1)
    q_rope = ops.b.nn(c_qn, w_qr) * c_all + ops.b.nn(c_qn, w_qrs) * s_all
    return ops.b.nn(c_qn, w_qn), q_rope, ops.b.nn(c_kvn, w_kv)


def _softmax(s):
    m = lax.stop_gradient(jnp.max(s, axis=-1, keepdims=True))
    e = jnp.exp(s - m)
    return e / jnp.sum(e, axis=-1, keepdims=True)


def fn_xattn(ops, row0, q, k, v):
    s = ops.b.nt(q, k) * (X_HEAD_DIM ** -0.5)
    return (ops.b.nn(_softmax(s), v),)


def _silu(x):
    return x * jax.nn.sigmoid(x)


def fn_gdn_prep(ops, row0, *t):
    nh = len(t) // 3
    qs, ks, vs = [], [], []
    for qc, kc, vc in zip(t[:nh], t[nh:2 * nh], t[2 * nh:]):
        q, k = _silu(qc), _silu(kc)
        qs.append(q * lax.rsqrt(jnp.sum(q * q, -1, keepdims=True) + EPS) * (GDN_DK ** -0.5))
        ks.append(k * lax.rsqrt(jnp.sum(k * k, -1, keepdims=True) + EPS))
        vs.append(_silu(vc))
    return tuple(qs + ks + vs)


def fn_gdn_gates(ops, row0, ba, alog, dtb):
    width = GDN_HEADS * GDN_DK
    beta = jax.nn.sigmoid(ba)
    z = ba + dtb
    softplus = jnp.maximum(z, 0.0) + jnp.log1p(jnp.exp(-jnp.abs(z)))
    g = -jnp.exp(alog) * softplus
    r = lax.broadcasted_iota(jnp.int32, (LANES, width), 0)
    c = lax.broadcasted_iota(jnp.int32, (LANES, width), 1) // GDN_DK
    e_beta = (r == c).astype(F32)
    e_g = (r == c + GDN_HEADS).astype(F32)
    return ops.h.nn(beta, e_beta), ops.h.nn(g, e_g)


def fn_gdn_out(ops, row0, *t):
    nh = (len(t) - 1) // 2
    g = t[-1]
    return tuple(_rms(o, g) * _silu(gate) for o, gate in zip(t[:nh], t[nh:2 * nh]))


def fn_mla_attn(ops, row0, qn, qr, kn, v, kr):
    s = (ops.b.nt(qn, kn) + ops.b.nt(qr, kr)) * ((MLA_NOPE + MLA_ROPE) ** -0.5)
    rows = row0 + lax.broadcasted_iota(jnp.int32, s.shape, 0)
    cols = lax.broadcasted_iota(jnp.int32, s.shape, 1)
    s = jnp.where(rows >= cols, s, NEG_BIG)
    return (ops.b.nn(_softmax(s), v),)


def _shift_down(x, d, t_idx):
    if d == 0:
        return x
    return jnp.where(t_idx >= d, pltpu.roll(x, d, axis=0), 0.0)


def _shift_up(x, d, t_idx):
    if d == 0:
        return x
    n = x.shape[0]
    return jnp.where(t_idx < n - d, pltpu.roll(x, n - d, axis=0), 0.0)


def conv_fwd(x, w, name):
    s, c = x.shape
    kw = w.shape[0]
    tc = _pick(c, (256, 128))

    def body(x_ref, w_ref, y_ref):
        xv = x_ref[...]
        t_idx = lax.broadcasted_iota(jnp.int32, xv.shape, 0)
        acc = jnp.zeros_like(xv)
        for j in range(kw):
            acc = acc + w_ref[j:j + 1, :] * _shift_down(xv, kw - 1 - j, t_idx)
        y_ref[...] = acc

    return pl.pallas_call(
        body, name=name, out_shape=jax.ShapeDtypeStruct((s, c), F32), grid=(c // tc,),
        in_specs=[pl.BlockSpec((s, tc), lambda i: (0, i)), pl.BlockSpec((kw, tc), lambda i: (0, i))],
        out_specs=pl.BlockSpec((s, tc), lambda i: (0, i)),
        compiler_params=_params(("parallel",)),
    )(x, w)


def conv_bwd(x, w, dy, name):
    s, c = x.shape
    kw = w.shape[0]
    tc = _pick(c, (256, 128))

    def body(x_ref, w_ref, dy_ref, dx_ref, dw_ref):
        xv, dyv = x_ref[...], dy_ref[...]
        t_idx = lax.broadcasted_iota(jnp.int32, xv.shape, 0)
        dx = jnp.zeros_like(xv)
        for j in range(kw):
            d = kw - 1 - j
            dx = dx + w_ref[j:j + 1, :] * _shift_up(dyv, d, t_idx)
            dw_ref[j:j + 1, :] = jnp.sum(dyv * _shift_down(xv, d, t_idx), axis=0, keepdims=True)
        dx_ref[...] = dx

    return pl.pallas_call(
        body, name=name,
        out_shape=[jax.ShapeDtypeStruct((s, c), F32), jax.ShapeDtypeStruct((kw, c), F32)],
        grid=(c // tc,),
        in_specs=[pl.BlockSpec((s, tc), lambda i: (0, i)), pl.BlockSpec((kw, tc), lambda i: (0, i)),
                  pl.BlockSpec((s, tc), lambda i: (0, i))],
        out_specs=[pl.BlockSpec((s, tc), lambda i: (0, i)), pl.BlockSpec((kw, tc), lambda i: (0, i))],
        compiler_params=_params(("parallel",)),
    )(x, w, dy)


def make_conv(name):
    @jax.custom_vjp
    def op(x, w):
        return conv_fwd(x, w, name + "_f")

    def fwd(x, w):
        return op(x, w), (x, w)

    def bwd(saved, dy):
        dx, dw = conv_bwd(saved[0], saved[1], dy, name + "_b")
        return dx, dw

    op.defvjp(fwd, bwd)
    return op


def _gdn_consts():
    c, d = GDN_CHUNK, GDN_DK
    i = lax.broadcasted_iota(jnp.int32, (c, c), 0)
    j = lax.broadcasted_iota(jnp.int32, (c, c), 1)
    tri = i >= j
    return dict(
        tri=tri, strict=i > j,
        tri_f=tri.astype(F32),
        eye=(i == j).astype(F32),
        lane0=(lax.broadcasted_iota(jnp.int32, (c, d), 1) == 0).astype(F32),
        last_row=(lax.broadcasted_iota(jnp.int32, (c, d), 0) == c - 1).astype(F32),
    )


def _inverse_given(m_ops):
    @jax.custom_vjp
    def given(mm_, t):
        return t

    def bwd(t, dt):
        return -m_ops.nt(m_ops.tn(t, dt), t), jnp.zeros_like(t)

    given.defvjp(lambda mm_, t: (t, t), bwd)
    return given


def _gdn_chunk(ops, q, k, v, g, beta, state, t_saved=None):
    b, m, sel = ops.bb, ops.bm, ops.bs
    nh, c, d = q.shape[0], GDN_CHUNK, GDN_DK
    k_ = _gdn_consts()

    def per_head(a):
        return jnp.broadcast_to(a, (nh,) + a.shape)

    gc = sel.sel_nn(per_head(k_["tri_f"]), g)
    col = jnp.broadcast_to(jnp.sum(gc * k_["lane0"], axis=2, keepdims=True), (nh, c, c))
    row = sel.sel_nt(per_head(k_["lane0"]), gc)
    decay = jnp.where(k_["tri"], jnp.exp(jnp.where(k_["tri"], col - row, 0.0)), 0.0)
    kb = k * beta
    mm_ = jnp.where(k_["strict"], b.nt(kb, k) * decay, 0.0)
    if t_saved is None:
        p = -mm_
        t = k_["eye"] + p
        for _ in range(int(math.log2(GDN_CHUNK)) - 1):
            p = m.nn(p, p)
            t = t + m.nn(t, p)
    else:
        t = _inverse_given(_PLAIN.bm)(mm_, t_saved)
    egc = jnp.exp(gc)
    u = b.nn(t, v * beta)
    w = b.nn(t, kb * egc)
    attn = b.nt(q, k) * decay
    v_new = u - b.nn(w, state)
    o = b.nn(q * egc, state) + b.nn(attn, v_new)
    g_last = jnp.sum(gc * k_["last_row"], axis=1, keepdims=True)
    new_state = (state * jnp.exp(jnp.broadcast_to(g_last, (nh, d, d)))
                 + b.tn(k * jnp.exp(jnp.broadcast_to(g_last, (nh, c, d)) - gc), v_new))
    return o, new_state, t


GDN_HEAD_GROUP = 8
GDN_TILE_CHUNKS = 4


def _heads_of(ref, rows, n_heads):
    d = GDN_DK
    return jnp.stack([ref[rows, h * d:(h + 1) * d] for h in range(n_heads)])


def _gdn_specs(s, reverse):
    d, hg = GDN_DK, GDN_HEAD_GROUP
    tile = min(GDN_TILE_CHUNKS * GDN_CHUNK, s)
    n_tiles = s // tile
    t_of = (lambda t: n_tiles - 1 - t) if reverse else (lambda t: t)
    seq = pl.BlockSpec((tile, hg * d), lambda grp, t: (t_of(t), grp))
    st = pl.BlockSpec((hg, tile // GDN_CHUNK, d, d), lambda grp, t: (grp, t_of(t), 0, 0))
    inv = pl.BlockSpec((hg, tile // GDN_CHUNK, GDN_CHUNK, GDN_CHUNK), lambda grp, t: (grp, t_of(t), 0, 0))
    return seq, st, inv, tile, n_tiles


def gdn_fwd(q, k, v, g, beta, name):
    s = q.shape[0]
    d, hg = GDN_DK, GDN_HEAD_GROUP
    seq, st, inv, tile, n_tiles = _gdn_specs(s, False)

    def body(q_ref, k_ref, v_ref, g_ref, b_ref, o_ref, st_ref, inv_ref, state_scr):
        @pl.when(pl.program_id(1) == 0)
        def _():
            state_scr[...] = jnp.zeros_like(state_scr)

        def step(ci, carry):
            rows = pl.ds(pl.multiple_of(ci * GDN_CHUNK, GDN_CHUNK), GDN_CHUNK)
            state = state_scr[...]
            for h in range(hg):
                st_ref[h, ci] = state[h]
            o, new_state, t = _gdn_chunk(_PLAIN, *[_heads_of(r, rows, hg) for r in (q_ref, k_ref, v_ref, g_ref, b_ref)],
                                         state)
            for h in range(hg):
                o_ref[rows, h * d:(h + 1) * d] = o[h]
                inv_ref[h, ci] = t[h]
            state_scr[...] = new_state
            return carry

        lax.fori_loop(0, tile // GDN_CHUNK, step, 0)

    return pl.pallas_call(
        body, name=name,
        out_shape=[jax.ShapeDtypeStruct(q.shape, F32),
                   jax.ShapeDtypeStruct((GDN_HEADS, s // GDN_CHUNK, d, d), F32),
                   jax.ShapeDtypeStruct((GDN_HEADS, s // GDN_CHUNK, GDN_CHUNK, GDN_CHUNK), F32)],
        grid=(GDN_HEADS // hg, n_tiles), in_specs=[seq] * 5, out_specs=[seq, st, inv],
        scratch_shapes=[pltpu.VMEM((hg, d, d), F32)],
        compiler_params=_params(("parallel", "arbitrary")),
    )(q, k, v, g, beta)


def gdn_bwd(q, k, v, g, beta, states, inverses, do, name):
    s = q.shape[0]
    d, hg = GDN_DK, GDN_HEAD_GROUP
    seq, st, inv, tile, n_tiles = _gdn_specs(s, True)
    tile_chunks = tile // GDN_CHUNK

    def body(q_ref, k_ref, v_ref, g_ref, b_ref, st_ref, inv_ref, do_ref, dq_ref, dk_ref, dv_ref, dg_ref, db_ref,
             dstate_scr):
        @pl.when(pl.program_id(1) == 0)
        def _():
            dstate_scr[...] = jnp.zeros_like(dstate_scr)

        def step(it, carry):
            ci = tile_chunks - 1 - it
            rows = pl.ds(pl.multiple_of(ci * GDN_CHUNK, GDN_CHUNK), GDN_CHUNK)
            prim = [_heads_of(r, rows, hg) for r in (q_ref, k_ref, v_ref, g_ref, b_ref)]
            prim.append(jnp.stack([st_ref[h, ci] for h in range(hg)]))
            t_saved = jnp.stack([inv_ref[h, ci] for h in range(hg)])
            _, vjp = jax.vjp(lambda *a: _gdn_chunk(_DIFF, *a, t_saved=t_saved)[:2], *prim)
            grads = vjp((_heads_of(do_ref, rows, hg), dstate_scr[...]))
            for g_ref_out, gr in zip((dq_ref, dk_ref, dv_ref, dg_ref, db_ref), grads[:5]):
                for h in range(hg):
                    g_ref_out[rows, h * d:(h + 1) * d] = gr[h]
            dstate_scr[...] = grads[5]
            return carry

        lax.fori_loop(0, tile_chunks, step, 0)

    return pl.pallas_call(
        body, name=name,
        out_shape=[jax.ShapeDtypeStruct(q.shape, F32)] * 5,
        grid=(GDN_HEADS // hg, n_tiles), in_specs=[seq] * 5 + [st, inv, seq], out_specs=[seq] * 5,
        scratch_shapes=[pltpu.VMEM((hg, d, d), F32)],
        compiler_params=_params(("parallel", "arbitrary")),
    )(q, k, v, g, beta, states, inverses, do)


def make_gdn(name):
    @jax.custom_vjp
    def op(q, k, v, g, beta):
        return gdn_fwd(q, k, v, g, beta, name + "_f")[0]

    def fwd(q, k, v, g, beta):
        o, states, inverses = gdn_fwd(q, k, v, g, beta, name + "_f")
        return o, (q, k, v, g, beta, states, inverses)

    def bwd(saved, do):
        return tuple(gdn_bwd(*saved, do, name + "_b"))

    op.defvjp(fwd, bwd)
    return op


def loss_head(x, g, target, name):
    s, d = x.shape
    tm = min(256, s)

    def body(x_ref, g_ref, t_ref, loss_ref, dx_ref, dg_ref):
        tgt = t_ref[...]

        def f(xv, gv):
            err = _rms(xv, gv) - tgt
            per_row = jnp.mean(err * err, axis=-1, keepdims=True)
            return 0.5 * jnp.sum(per_row, axis=0, keepdims=True)

        val, vjp = jax.vjp(f, x_ref[...], g_ref[...])
        dx, dg = vjp(jnp.ones((1, 1), F32))
        dx_ref[...] = dx
        first = pl.program_id(0) == 0

        @pl.when(first)
        def _():
            dg_ref[...] = dg
            loss_ref[...] = jnp.broadcast_to(val, loss_ref.shape)

        @pl.when(jnp.logical_not(first))
        def _():
            dg_ref[...] += dg
            loss_ref[...] += jnp.broadcast_to(val, loss_ref.shape)

    row = pl.BlockSpec((tm, d), lambda i: (i, 0))
    vec = pl.BlockSpec((1, d), lambda i: (0, 0))
    return pl.pallas_call(
        body, name=name,
        out_shape=[jax.ShapeDtypeStruct((1, LANES), F32), jax.ShapeDtypeStruct((s, d), F32),
                   jax.ShapeDtypeStruct((1, d), F32)],
        grid=(s // tm,), in_specs=[row, vec, row],
        out_specs=[pl.BlockSpec((1, LANES), lambda i: (0, 0)), row, vec],
        compiler_params=_params(("arbitrary",)),
    )(x, g, target)


def adamw(g8, w, m, v, layer, prev, name):
    n_layers, rows, width = w.shape
    tr = _pick(rows, (256, 128, 64, 32, 16, 8))

    def body(g_ref, w_ref, m_ref, v_ref, *rest):
        go_ref, d_ref, mo_ref, vo_ref = rest[-4:]
        g = g_ref[0].astype(F32)
        for p in range(1, N_DEV):
            g = g + g_ref[p].astype(F32)
        m_new = ADAM_B1 * m_ref[...] + (1.0 - ADAM_B1) * g
        v_new = ADAM_B2 * v_ref[...] + (1.0 - ADAM_B2) * (g * g)
        m_hat = m_new / (1.0 - ADAM_B1 ** ADAM_STEP)
        v_hat = v_new / (1.0 - ADAM_B2 ** ADAM_STEP)
        go_ref[...] = g
        d_ref[...] = -ADAM_LR * (m_hat / (jnp.sqrt(v_hat) + ADAM_EPS) + ADAM_WD * w_ref[...])
        mo_ref[...] = m_new
        vo_ref[...] = v_new

    blk = pl.BlockSpec((None, tr, width), lambda i: (layer, i, 0))
    carried = list(prev) if prev is not None else []
    return pl.pallas_call(
        body, name=name, out_shape=[jax.ShapeDtypeStruct((n_layers, rows, width), F32)] * 4,
        grid=(rows // tr,),
        in_specs=[pl.BlockSpec((N_DEV, tr, width), lambda i: (0, i, 0)), blk, blk, blk]
        + [pl.BlockSpec(memory_space=pl.ANY)] * len(carried),
        out_specs=[blk] * 4,
        input_output_aliases={4 + j: j for j in range(len(carried))},
        compiler_params=_params(("parallel",)),
    )(g8, w, m, v, *carried)


_HBM = pl.BlockSpec(memory_space=pltpu.HBM)
_SEM = pl.BlockSpec(memory_space=pltpu.SEMAPHORE)
_EFFECT = pltpu.SideEffectType.DATAFLOW_SIDE_EFFECTING


def _exchange_copies(mode, src_refs, land_refs, send_sems, recv_sems, local_sems):
    x, y, c = lax.axis_index("x"), lax.axis_index("y"), lax.axis_index("c")
    me = 4 * x + 2 * y + c
    n = len(src_refs)

    def src(k, p):
        return src_refs[k] if mode == "gather" else src_refs[k].at[p]

    local = [pltpu.make_async_copy(src(k, me), land_refs[k].at[me], local_sems.at[k]) for k in range(n)]
    sends, recvs = [], []
    for k in range(n):
        for r in range(1, N_DEV):
            px = (1 - x) if r & 4 else x
            py = (1 - y) if r & 2 else y
            pc = (1 - c) if r & 1 else c
            p = 4 * px + 2 * py + pc
            sem = k * (N_DEV - 1) + r - 1
            sends.append(pltpu.make_async_remote_copy(
                src_ref=src(k, p), dst_ref=land_refs[k].at[me],
                send_sem=send_sems.at[sem], recv_sem=recv_sems.at[sem],
                device_id=(px, py, pc), device_id_type=pl.DeviceIdType.MESH))
            recvs.append(pltpu.make_async_remote_copy(
                src_ref=src(k, p), dst_ref=land_refs[k].at[p],
                send_sem=send_sems.at[sem], recv_sem=recv_sems.at[sem],
                device_id=(px, py, pc), device_id_type=pl.DeviceIdType.MESH))
    return local, sends, recvs


def exchange_start(mode, arrays, name, carry=()):
    n, nc = len(arrays), len(carry)
    land_shapes = [((N_DEV,) + tuple(a.shape)) if mode == "gather" else tuple(a.shape) for a in arrays]
    lands = [pltpu.with_memory_space_constraint(lax.empty(shp, a.dtype), pltpu.HBM)
             for shp, a in zip(land_shapes, arrays)]
    srcs = [pltpu.with_memory_space_constraint(a, pltpu.HBM) for a in arrays]
    carried = [pltpu.with_memory_space_constraint(a, pltpu.HBM) for a in carry]

    def body(*refs):
        src_refs, land_refs = refs[:n], refs[n:2 * n]
        first_out = 2 * n + nc
        send_sems, recv_sems, local_sems = refs[first_out:first_out + 3]
        token = refs[-1]
        local, sends, _ = _exchange_copies(mode, src_refs, land_refs, send_sems, recv_sems, local_sems)
        for cp in local + sends:
            cp.start()
        token[...] = jnp.zeros_like(token)

    n_sem = n * (N_DEV - 1)
    out = pl.pallas_call(
        body, name=name,
        out_shape=(pltpu.SemaphoreType.DMA((n_sem,)), pltpu.SemaphoreType.DMA((n_sem,)),
                   pltpu.SemaphoreType.DMA((n,)),
                   *[pltpu.HBM(a.shape, a.dtype) for a in arrays],
                   *[pltpu.HBM(shp, a.dtype) for shp, a in zip(land_shapes, arrays)],
                   *[pltpu.HBM(a.shape, a.dtype) for a in carry],
                   jax.ShapeDtypeStruct((8, LANES), F32)),
        in_specs=[_HBM] * (2 * n + nc),
        out_specs=(_SEM, _SEM, _SEM, *[_HBM] * (2 * n + nc), pl.BlockSpec(memory_space=pltpu.VMEM)),
        input_output_aliases={i: 3 + i for i in range(2 * n + nc)},
        compiler_params=pltpu.CompilerParams(has_side_effects=_EFFECT),
    )(*srcs, *lands, *carried)
    handle = dict(mode=mode, sems=out[:3], srcs=out[3:3 + n], lands=out[3 + n:3 + 2 * n])
    return handle, out[-1], list(out[3 + 2 * n:3 + 2 * n + nc])


def exchange_wait(handle, after, name):
    mode, srcs, lands = handle["mode"], list(handle["srcs"]), list(handle["lands"])
    n = len(srcs)

    def body(*refs):
        src_refs, land_refs = refs[:n], refs[n:2 * n]
        send_sems, recv_sems, local_sems = refs[2 * n:2 * n + 3]
        local, sends, recvs = _exchange_copies(mode, src_refs, land_refs, send_sems, recv_sems, local_sems)
        for cp in sends:
            cp.wait_send()
        for cp in recvs:
            cp.wait_recv()
        for cp in local:
            cp.wait()

    out = pl.pallas_call(
        body, name=name,
        out_shape=(*[pltpu.HBM(a.shape, a.dtype) for a in srcs], *[pltpu.HBM(a.shape, a.dtype) for a in lands]),
        in_specs=[_HBM] * (2 * n) + [_SEM] * 3 + [pl.BlockSpec(memory_space=pl.ANY)],
        out_specs=tuple([_HBM] * (2 * n)),
        input_output_aliases={i: i for i in range(2 * n)},
        compiler_params=pltpu.CompilerParams(has_side_effects=_EFFECT),
    )(*srcs, *lands, *handle["sems"], after)
    return list(out[n:])


_ICI_RELATIONS = (2, 4, 6)


def _mesh_place():
    x, y, c = lax.axis_index("x"), lax.axis_index("y"), lax.axis_index("c")

    def peer(r):
        px = (1 - x) if r & 4 else x
        py = (1 - y) if r & 2 else y
        pc = (1 - c) if r & 1 else c
        return (px, py, pc), 4 * px + 2 * py + pc

    return 4 * x + 2 * y + c, peer


def _remote(src, dst, send_sem, recv_sem, device):
    return pltpu.make_async_remote_copy(src_ref=src, dst_ref=dst, send_sem=send_sem, recv_sem=recv_sem,
                                        device_id=device, device_id_type=pl.DeviceIdType.MESH)


def gather2_start(groups, name):
    flat = [a for g in groups for a in g]
    n = len(flat)
    lands = [pltpu.with_memory_space_constraint(lax.empty((N_DEV,) + tuple(a.shape), a.dtype), pltpu.HBM) for a in flat]
    srcs = [pltpu.with_memory_space_constraint(a, pltpu.HBM) for a in flat]
    n_rel = 1 + len(_ICI_RELATIONS)

    def body(*refs):
        src_refs, land_refs = refs[:n], refs[n:2 * n]
        sem_refs = refs[2 * n:2 * n + 4 * len(groups)]
        me, peer = _mesh_place()
        k = 0
        for gi, g in enumerate(groups):
            send_sems, recv_sib, recv_ici, local_sems = sem_refs[4 * gi:4 * gi + 4]
            for j in range(len(g)):
                pltpu.make_async_copy(src_refs[k], land_refs[k].at[me], local_sems.at[j]).start()
                dev, _ = peer(1)
                _remote(src_refs[k], land_refs[k].at[me], send_sems.at[n_rel * j], recv_sib.at[j], dev).start()
                for t, r in enumerate(_ICI_RELATIONS):
                    dev, _ = peer(r)
                    _remote(src_refs[k], land_refs[k].at[me], send_sems.at[n_rel * j + 1 + t],
                            recv_ici.at[len(_ICI_RELATIONS) * j + t], dev).start()
                k += 1
        refs[-1][...] = jnp.zeros_like(refs[-1])

    sem_shapes = []
    for g in groups:
        sem_shapes += [pltpu.SemaphoreType.DMA((n_rel * len(g),)), pltpu.SemaphoreType.DMA((len(g),)),
                       pltpu.SemaphoreType.DMA((len(_ICI_RELATIONS) * len(g),)), pltpu.SemaphoreType.DMA((len(g),))]
    out = pl.pallas_call(
        body, name=name,
        out_shape=(*sem_shapes, *[pltpu.HBM(a.shape, a.dtype) for a in flat],
                   *[pltpu.HBM((N_DEV,) + tuple(a.shape), a.dtype) for a in flat],
                   jax.ShapeDtypeStruct((8, LANES), F32)),
        in_specs=[_HBM] * (2 * n),
        out_specs=(*[_SEM] * len(sem_shapes), *[_HBM] * (2 * n), pl.BlockSpec(memory_space=pltpu.VMEM)),
        input_output_aliases={i: len(sem_shapes) + i for i in range(2 * n)},
        compiler_params=pltpu.CompilerParams(has_side_effects=_EFFECT),
    )(*srcs, *lands)
    handles, k, base = [], 0, len(sem_shapes)
    for gi, g in enumerate(groups):
        handles.append(dict(sems=out[4 * gi:4 * gi + 4], srcs=out[base + k:base + k + len(g)],
                            lands=out[base + n + k:base + n + k + len(g)]))
        k += len(g)
    return handles, out[-1]


def gather2_forward(handle, after, name, carry=()):
    lands, nc = list(handle["lands"]), len(carry)
    n, n_ici = len(lands), len(_ICI_RELATIONS)
    carried = [pltpu.with_memory_space_constraint(a, pltpu.HBM) for a in carry]

    def body(*refs):
        land_refs = refs[:n]
        recv_ici = refs[n + nc]
        fwd_send, fwd_recv = refs[n + nc + 2], refs[n + nc + 3]
        me, peer = _mesh_place()
        sibling, _ = peer(1)
        for j in range(n):
            for t, r in enumerate(_ICI_RELATIONS):
                dev, p = peer(r)
                landed = land_refs[j].at[p]
                _remote(landed, landed, fwd_send.at[n_ici * j + t], recv_ici.at[n_ici * j + t], dev).wait_recv()
                _remote(landed, landed, fwd_send.at[n_ici * j + t], fwd_recv.at[n_ici * j + t], sibling).start()

    out = pl.pallas_call(
        body, name=name,
        out_shape=(pltpu.SemaphoreType.DMA((n_ici * n,)), pltpu.SemaphoreType.DMA((n_ici * n,)),
                   *[pltpu.HBM(a.shape, a.dtype) for a in lands], *[pltpu.HBM(a.shape, a.dtype) for a in carry]),
        in_specs=[_HBM] * (n + nc) + [_SEM, pl.BlockSpec(memory_space=pl.ANY)],
        out_specs=(_SEM, _SEM, *[_HBM] * (n + nc)),
        input_output_aliases={i: 2 + i for i in range(n + nc)},
        compiler_params=pltpu.CompilerParams(has_side_effects=_EFFECT),
    )(*lands, *carried, handle["sems"][2], after)
    new_handle = dict(sems=handle["sems"], srcs=handle["srcs"], lands=out[2:2 + n], fwd=out[:2])
    return new_handle, list(out[2 + n:])


def gather2_wait(handle, after, name):
    srcs, lands = list(handle["srcs"]), list(handle["lands"])
    n, n_ici = len(srcs), len(_ICI_RELATIONS)
    n_rel = 1 + n_ici
    send_all, recv_sibling, _, local_all = handle["sems"]

    def body(*refs):
        src_refs, land_refs = refs[:n], refs[n:2 * n]
        send_sems, recv_sib, local_sems, fwd_send, fwd_recv = refs[2 * n:2 * n + 5]
        me, peer = _mesh_place()
        sibling, sib = peer(1)
        for j in range(n):
            pltpu.make_async_copy(src_refs[j], land_refs[j].at[me], local_sems.at[j]).wait()
            _remote(src_refs[j], land_refs[j].at[sib], send_sems.at[n_rel * j], recv_sib.at[j], sibling).wait()
            for t, r in enumerate(_ICI_RELATIONS):
                dev, p = peer(r)
                _remote(src_refs[j], land_refs[j].at[me], send_sems.at[n_rel * j + 1 + t],
                        recv_sib.at[j], dev).wait_send()
                _, p_sib = peer(r ^ 1)
                _remote(land_refs[j].at[p], land_refs[j].at[p_sib], fwd_send.at[n_ici * j + t],
                        fwd_recv.at[n_ici * j + t], sibling).wait()

    out = pl.pallas_call(
        body, name=name,
        out_shape=(*[pltpu.HBM(a.shape, a.dtype) for a in srcs], *[pltpu.HBM(a.shape, a.dtype) for a in lands]),
        in_specs=[_HBM] * (2 * n) + [_SEM] * 5 + [pl.BlockSpec(memory_space=pl.ANY)],
        out_specs=tuple([_HBM] * (2 * n)),
        input_output_aliases={i: i for i in range(2 * n)},
        compiler_params=pltpu.CompilerParams(has_side_effects=_EFFECT),
    )(*srcs, *lands, send_all, recv_sibling, local_all, *handle["fwd"], after)
    return list(out[n:])


BIG = ["mla_w_in", "mla_w_uq", "mla_w_ukv", "mla_w_o", "gdn_w_in", "gdn_w_o", "sc_w_in", "sc_w_o",
       "xa_w_q", "xa_w_kv", "xa_w_o", "mlp_w1", "mlp_w2"]
TINY = [("mla_q_norm", 1), ("mla_kv_norm", 1), ("gdn_conv_w", 2), ("sc_conv_w", 2)]
REPL = ["gdn_a_log", "gdn_dt_bias", "gdn_o_norm", "norm_mix", "norm_mem", "norm_mlp", "mem_norm", "final_norm"]
WEIGHTS = ["mla_w_in", "mla_q_norm", "mla_kv_norm", "mla_w_uq", "mla_w_ukv", "mla_w_o", "gdn_w_in",
           "gdn_conv_w", "gdn_a_log", "gdn_dt_bias", "gdn_o_norm", "gdn_w_o", "sc_w_in", "sc_conv_w",
           "sc_w_o", "norm_mix", "norm_mem", "norm_mlp", "xa_w_q", "xa_w_kv", "xa_w_o", "mlp_w1",
           "mlp_w2", "mem_norm", "final_norm"]
MIXER_WEIGHTS = (["mla_w_in", "mla_w_uq", "mla_w_ukv", "mla_w_o"], ["gdn_w_in", "gdn_w_o"], ["sc_w_in", "sc_w_o"])
MIXER_PARAMS = (["norm_mem", "mla_q_norm", "mla_kv_norm"],
                ["norm_mem", "gdn_conv_w", "gdn_a_log", "gdn_dt_bias", "gdn_o_norm"],
                ["norm_mem", "sc_conv_w"])


def from_shards(a8, axis):
    a = jnp.moveaxis(a8, 0, axis)
    shp = a.shape
    return a.reshape(shp[:axis] + (shp[axis] * shp[axis + 1],) + shp[axis + 2:])


def pack_rows(flat_list, width, row_mult):
    total = sum(a.shape[-1] for a in flat_list)
    rows = -(-total // width)
    rows = -(-rows // row_mult) * row_mult
    pad = rows * width - total
    parts = list(flat_list)
    if pad:
        parts.append(jnp.zeros((pad,), flat_list[0].dtype))
    return jnp.concatenate(parts, axis=-1).reshape(rows, width)


def unpack_rows(packed, shapes):
    lead = packed.shape[:-2]
    flat = packed.reshape(lead + (-1,))
    out, off = [], 0
    for shp in shapes:
        n = math.prod(shp)
        out.append(flat[..., off:off + n].reshape(lead + tuple(shp)))
        off += n
    return out


def _swap_halves(w):
    half = w.shape[-1] // 2
    return jnp.concatenate([w[..., half:], w[..., :half]], axis=-1)


def _pad_last(w, n):
    return jnp.pad(w, [(0, 0)] * (w.ndim - 1) + [(0, n - w.shape[-1])])


def _unblock(w8):
    return jnp.transpose(w8, (1, 0, 2)).reshape(w8.shape[1], -1)


def _stack_rows(w8):
    return w8.reshape(-1, w8.shape[-1])


def rms_op(name, rows, d, out_dtype):
    tm = rows if rows * d * 4 <= BLOCK_BYTES else 512
    return make_tile_op(fn_rms, name, ["row", "par"], [True, True], [("row", d, out_dtype)], rows, min(tm, rows))


def seg_memory(p, mem):
    return rms_op("rms_memory", mem.shape[0], mem.shape[1], BF16)(mem, p["mem_norm"].reshape(1, -1))[0]


def seg_mixer(i, wts, p, x, h, rope_c, rope_s):
    s, d = x.shape
    j, kind = i // N_MIXERS, i % N_MIXERS
    tag = f"l{i}"
    hd = MLA_NOPE
    next_gain = p["norm_mem"][i].reshape(1, d)
    if kind == 0:
        w_in = _stack_rows(wts["mla_w_in"])
        w_cq = w_in[:, :MLA_Q_RANK]
        w_ckv = w_in[:, MLA_Q_RANK:MLA_Q_RANK + MLA_KV_RANK]
        w_kr = w_in[:, MLA_Q_RANK + MLA_KV_RANK:]
        c_qn, c_kvn, k_rope = make_tile_op(
            fn_mla_down, tag + "_mla_down", ["row", "row", "row"] + ["par"] * 6, [True, False, False] + [True] * 6,
            [("row", MLA_Q_RANK, BF16), ("row", MLA_KV_RANK, BF16), ("row", hd, F32)], s, 256)(
            h, rope_c, rope_s, w_cq, w_ckv, _pad_last(w_kr, hd), _pad_last(_swap_halves(w_kr), hd),
            p["mla_q_norm"][j].reshape(1, -1), p["mla_kv_norm"][j].reshape(1, -1))
        w_uq8 = wts["mla_w_uq"]
        w_qn = _unblock(w_uq8[:, :, :MLA_NOPE])
        w_qr = w_uq8[:, :, MLA_NOPE:]
        w_qr_p = _unblock(_pad_last(w_qr, hd))
        w_qr_s = _unblock(_pad_last(_swap_halves(w_qr), hd))
        nq = MLA_HEADS * hd
        q_nope, q_rope, kv = make_tile_op(
            fn_mla_up, tag + "_mla_up", ["row", "row", "row", "row"] + ["par"] * 4, [True, True, False, False] + [True] * 4,
            [("row", nq, BF16), ("row", nq, F32), ("row", 2 * nq, BF16)], s, 256)(
            c_qn, c_kvn, rope_c, rope_s, w_qn, w_qr_p, w_qr_s, _unblock(wts["mla_w_ukv"]))
        n_groups = MLA_QUERY_GROUPS if s % (MLA_QUERY_GROUPS * 256) == 0 else 1
        rows_g = s // n_groups
        o_groups = []
        for grp in range(n_groups):
            r0, r1 = grp * rows_g, (grp + 1) * rows_g
            o_groups.append(make_tile_op(
                fn_mla_attn, f"{tag}_mla_attn{grp}", [("rowh", hd), ("rowh", hd), ("parh", hd, 2), "par"],
                [True] * 4, [(("rowh", hd), nq, BF16)], rows_g, 256, MLA_HEADS, row_base=r0)(
                q_nope[r0:r1], q_rope[r0:r1], kv[:r1], k_rope[:r1])[0])
        o = jnp.concatenate(o_groups, axis=0)
        return make_mm_res_rms(tag + "_mla_o")(x, o, _stack_rows(wts["mla_w_o"]), next_gain)
    if kind == 1:
        ng = GDN_HEADS * GDN_DK
        w_in = _unblock(wts["gdn_w_in"])
        cw = p["gdn_conv_w"][j]
        conv_out = []
        for part, nm in enumerate(("q", "k", "v")):
            cols = slice(part * ng, (part + 1) * ng)
            pre = make_mm(f"{tag}_gdn_in_{nm}", F32)(h, w_in[:, cols])
            conv_out.append(make_conv(f"{tag}_gdn_conv_{nm}")(pre, cw[:, cols]))
        gate = make_mm(tag + "_gdn_in_g", F32)(h, w_in[:, 3 * ng:4 * ng])
        ba = make_mm(tag + "_gdn_in_ba", F32)(h, _pad_last(w_in[:, 4 * ng:], LANES))
        heads_row = ("row", GDN_DK, GDN_HEADS)
        q, k, v = make_tile_op(fn_gdn_prep, tag + "_gdn_prep", [heads_row] * 3, [True] * 3,
                               [(heads_row, ng, F32)] * 3, s, 512)(*conv_out)
        alog = jnp.pad(p["gdn_a_log"][j].reshape(1, -1), ((0, 0), (GDN_HEADS, LANES - 2 * GDN_HEADS)))
        dtb = jnp.pad(p["gdn_dt_bias"][j].reshape(1, -1), ((0, 0), (GDN_HEADS, LANES - 2 * GDN_HEADS)))
        beta_b, g_b = make_tile_op(fn_gdn_gates, tag + "_gdn_gates", ["row", "par", "par"], [True] * 3,
                                   [("row", ng, F32)] * 2, s, 512)(ba, alog, dtb)
        o = make_gdn(tag + "_gdn_core")(q, k, v, g_b, beta_b)
        o = make_tile_op(fn_gdn_out, tag + "_gdn_out", [heads_row, heads_row, "par"],
                         [True] * 3, [(heads_row, ng, BF16)], s, 512)(
            o, gate, p["gdn_o_norm"][j].reshape(1, -1))[0]
        return make_mm_res_rms(tag + "_gdn_o")(x, o, _stack_rows(wts["gdn_w_o"]), next_gain)
    w_in = _unblock(wts["sc_w_in"])
    b_gate = make_mm(tag + "_sc_in_b", F32)(h, w_in[:, :d])
    c_gate = make_mm(tag + "_sc_in_c", F32)(h, w_in[:, d:2 * d])
    u = make_mm(tag + "_sc_in_u", F32)(h, w_in[:, 2 * d:])
    cu = make_tile_op(fn_mul, tag + "_sc_cu", ["row", "row"], [True, True], [("row", d, F32)], s, 512)(
        c_gate, u)[0]
    cv = make_conv(tag + "_sc_conv")(cu, p["sc_conv_w"][j])
    yv = make_tile_op(fn_mul, tag + "_sc_gate", ["row", "row"], [True, True], [("row", d, BF16)], s, 512)(
        b_gate, cv)[0]
    return make_mm_res_rms(tag + "_sc_o")(x, yv, _stack_rows(wts["sc_w_o"]), next_gain)


def seg_xattn(i, wts, p, x, hx, mem_n):
    s, d = x.shape
    tag = f"l{i}"
    q = make_mm(tag + "_xa_q", BF16)(hx, _stack_rows(wts["xa_w_q"]))
    kv = make_mm(tag + "_xa_kv", BF16, blocked=True)(mem_n, wts["xa_w_kv"])
    o = make_tile_op(fn_xattn, tag + "_xattn",
                     [("rowh", X_HEAD_DIM), ("parh", X_HEAD_DIM), ("parh", X_HEAD_DIM)], [True] * 3,
                     [(("rowh", X_HEAD_DIM), d, BF16)], s, 1024, X_HEADS)(q, kv[:, :d], kv[:, d:])[0]
    return make_mm_res_rms(tag + "_xa_o")(x, o, _stack_rows(wts["xa_w_o"]), p["norm_mlp"][i].reshape(1, d))


def seg_mlp(i, wts, p, x, hm):
    d = x.shape[1]
    gain = p["norm_mix"][i + 1].reshape(1, d) if i + 1 < DEPTH else None
    return make_mlp(f"l{i}_mlp", gain is not None)(x, hm, wts["mlp_w1"], _stack_rows(wts["mlp_w2"]), gain)


def segments():
    segs = []
    for i in range(DEPTH):
        j, kind = i // N_MIXERS, i % N_MIXERS
        segs.append((f"l{i}_mixer", [(n, j) for n in MIXER_WEIGHTS[kind]], MIXER_PARAMS[kind], "mixer"))
        segs.append((f"l{i}_xattn", [(n, i) for n in ("xa_w_q", "xa_w_kv", "xa_w_o")], ["norm_mlp"], "xattn"))
        segs.append((f"l{i}_mlp", [(n, i) for n in ("mlp_w1", "mlp_w2")], ["norm_mix"] if i + 1 < DEPTH else [],
                     "mlp"))
    return segs


def run_segment(index, kind, wts, p, x, h, mem_n, rope_c, rope_s):
    layer = index // 3
    if kind == "mixer":
        return seg_mixer(layer, wts, p, x, h, rope_c, rope_s)
    if kind == "xattn":
        return seg_xattn(layer, wts, p, x, h, mem_n)
    return seg_mlp(layer, wts, p, x, h)


def rope_tables(positions):
    inv_freq = ROPE_THETA ** (-jnp.arange(0, MLA_ROPE, 2, dtype=F32) / MLA_ROPE)
    ang = positions.astype(F32)[:, None] * inv_freq
    cos, sin = jnp.cos(ang), jnp.sin(ang)
    zeros = jnp.zeros((positions.shape[0], MLA_NOPE - MLA_ROPE), F32)
    return jnp.concatenate([cos, cos, zeros], axis=-1), jnp.concatenate([-sin, sin, zeros], axis=-1)


def kernel(x, mem, positions, mla_w_in, mla_q_norm, mla_kv_norm, mla_w_uq, mla_w_ukv, mla_w_o, gdn_w_in, gdn_conv_w, gdn_a_log, gdn_dt_bias, gdn_o_norm, gdn_w_o, sc_w_in, sc_conv_w, sc_w_o, norm_mix, norm_mem, norm_mlp, xa_w_q, xa_w_kv, xa_w_o, mlp_w1, mlp_w2, mem_norm, final_norm, loss_target, m_mla_w_in, m_mla_q_norm, m_mla_kv_norm, m_mla_w_uq, m_mla_w_ukv, m_mla_w_o, m_gdn_w_in, m_gdn_conv_w, m_gdn_a_log, m_gdn_dt_bias, m_gdn_o_norm, m_gdn_w_o, m_sc_w_in, m_sc_conv_w, m_sc_w_o, m_norm_mix, m_norm_mem, m_norm_mlp, m_xa_w_q, m_xa_w_kv, m_xa_w_o, m_mlp_w1, m_mlp_w2, m_mem_norm, m_final_norm, v_mla_w_in, v_mla_q_norm, v_mla_kv_norm, v_mla_w_uq, v_mla_w_ukv, v_mla_w_o, v_gdn_w_in, v_gdn_conv_w, v_gdn_a_log, v_gdn_dt_bias, v_gdn_o_norm, v_gdn_w_o, v_sc_w_in, v_sc_conv_w, v_sc_w_o, v_norm_mix, v_norm_mem, v_norm_mlp, v_xa_w_q, v_xa_w_kv, v_xa_w_o, v_mlp_w1, v_mlp_w2, v_mem_norm, v_final_norm):
    args = locals()
    w_loc = {n: args[n] for n in WEIGHTS}
    m_loc = {n: args["m_" + n] for n in WEIGHTS}
    v_loc = {n: args["v_" + n] for n in WEIGHTS}
    me = 4 * lax.axis_index("x") + 2 * lax.axis_index("y") + lax.axis_index("c")
    segs = segments()

    w16 = {n: w_loc[n].astype(BF16) for n in BIG}
    tiny_pack = pack_rows([w_loc[n].reshape(-1) for n, _ in TINY], LANES, 8)
    gather_handles, token = gather2_start(
        [[tiny_pack]] + [[w16[n][layer] for n, layer in units] for _, units, _, _ in segs], "gather_start")

    x_cur = x[0]
    rope_c, rope_s = rope_tables(positions[0])
    tiny_handle, _ = gather2_forward(gather_handles[0], token, "gather_forward_tiny")
    tiny_all = gather2_wait(tiny_handle, token, "gather_wait_tiny")[0]
    gather_handles = gather_handles[1:]
    params = {}
    for (n, ax), a8 in zip(TINY, unpack_rows(tiny_all, [w_loc[n].shape for n, _ in TINY])):
        params[n] = from_shards(a8, ax)
    for n in REPL:
        params[n] = w_loc[n]

    mem_n, vjp_memory = jax.vjp(lambda p_: seg_memory(p_, mem[0]), {"mem_norm": params["mem_norm"]})
    h_cur, vjp_first_norm = jax.vjp(
        lambda p_, x_: rms_op("l0_rms_mix", x_.shape[0], x_.shape[1], BF16)(x_, p_["norm_mix"][0].reshape(1, -1))[0],
        {"norm_mix": params["norm_mix"]}, x_cur)
    vjps = []
    forwarded, _ = gather2_forward(gather_handles[0], token, f"gather_forward_{segs[0][0]}")
    for index, (tag, units, p_names, kind) in enumerate(segs):
        landed = gather2_wait(forwarded, token if index == 0 else x_cur, f"gather_wait_{tag}")
        wts = {n: a for (n, _), a in zip(units, landed)}
        p_seg = {n: params[n] for n in p_names}
        if index + 1 < len(segs):
            forwarded, (p_seg[p_names[0]],) = gather2_forward(
                gather_handles[index + 1], landed[0], f"gather_forward_{segs[index + 1][0]}",
                carry=[p_seg[p_names[0]]])
        outs, vjp_seg = jax.vjp(
            lambda w_, p_, x_, h_, m_, index=index, kind=kind:
            run_segment(index, kind, w_, p_, x_, h_, m_, rope_c, rope_s),
            wts, p_seg, x_cur, h_cur, mem_n)
        x_cur, h_cur = outs[0], (outs[1] if len(outs) > 1 else None)
        vjps.append(vjp_seg)

    loss_vec, g_x, d_final = loss_head(x_cur, params["final_norm"].reshape(1, -1), loss_target[0], "loss_head")

    grads = {n: jnp.zeros_like(params[n]) for n in params}
    grads["final_norm"] = d_final.reshape(-1)
    g_mem_n = jnp.zeros_like(mem_n)
    g_h = None
    scatter_handles = []
    for (tag, units, _, _), vjp_seg in zip(reversed(segs), reversed(vjps)):
        g_wts, g_p, g_x, g_h, g_m = vjp_seg((g_x,) if g_h is None else (g_x, g_h))
        for n, g in g_p.items():
            grads[n] = grads[n] + g
        g_mem_n = g_mem_n + g_m
        handle, _, (g_h,) = exchange_start("scatter", [g_wts[n] for n, _ in units], f"scatter_start_{tag}",
                                           carry=[g_h])
        scatter_handles.append((units, handle))
    grads["mem_norm"] = grads["mem_norm"] + vjp_memory(g_mem_n)[0]["mem_norm"]
    g_first, g_x_norm = vjp_first_norm(g_h)
    grads["norm_mix"] = grads["norm_mix"] + g_first["norm_mix"]
    g_x = g_x + g_x_norm

    small_names = [n for n, _ in TINY] + REPL
    small_g = pack_rows([loss_vec[0, :1]] + [grads[n].astype(F32).reshape(-1) for n in small_names], PACK_W, 8)
    small_handle, _, _ = exchange_start("gather", [small_g], "gather_start_small_grads")

    g_recv = {}
    for units, handle in scatter_handles:
        landed = exchange_wait(handle, g_x, f"scatter_wait_{units[0][0]}_{units[0][1]}")
        g_recv.update(dict(zip(units, landed)))

    res = {}
    for n in BIG:
        outs = None
        for layer in range(w_loc[n].shape[0]):
            outs = adamw(g_recv[n, layer], w_loc[n], m_loc[n], v_loc[n], layer, outs, f"adamw_{n}_{layer}")
        for kind, a in zip(("grad", "delta", "m", "v"), outs):
            res[(kind, n)] = a
    small_recv = exchange_wait(small_handle, res[("grad", BIG[-1])], "gather_wait_small_grads")[0]

    def full_small(d):
        parts = [jnp.zeros((1,), F32)]
        for n, ax in TINY:
            full_shape = params[n].shape
            start = [0] * len(full_shape)
            start[ax] = me * d[n].shape[ax]
            parts.append(lax.dynamic_update_slice(jnp.zeros(full_shape, F32), d[n], start).reshape(-1))
        parts += [d[n].reshape(-1) for n in REPL]
        return pack_rows(parts, PACK_W, 8)

    outs_small = adamw(small_recv, full_small(w_loc)[None], full_small(m_loc)[None], full_small(v_loc)[None],
                       0, None, "adamw_small")
    small_shapes = [(1,)] + [params[n].shape for n, _ in TINY] + [w_loc[n].shape for n in REPL]
    loss = None
    for kind, packed in zip(("grad", "delta", "m", "v"), outs_small):
        parts = unpack_rows(packed[0], small_shapes)
        if kind == "grad":
            loss = parts[0][0]
        for (n, ax), a in zip(TINY, parts[1:1 + len(TINY)]):
            start = [0] * a.ndim
            start[ax] = me * w_loc[n].shape[ax]
            res[(kind, n)] = lax.dynamic_slice(a, start, w_loc[n].shape)
        for n, a in zip(REPL, parts[1 + len(TINY):]):
            res[(kind, n)] = a

    out = [loss, g_x[None]]
    for kind in ("grad", "delta", "m", "v"):
        out += [res[(kind, n)] for n in WEIGHTS]
    return tuple(out)
```

```python
import math

import jax
import jax.numpy as jnp
from jax import lax
from jax.experimental import pallas as pl
from jax.experimental.pallas import tpu as pltpu

F32 = jnp.float32
BF16 = jnp.bfloat16

N_DEV = 8
LANES = 128
EPS = 1e-6
ROPE_THETA = 10000.0
MLA_HEADS, MLA_NOPE, MLA_ROPE, MLA_V = 8, 128, 64, 128
MLA_Q_RANK, MLA_KV_RANK = 384, 256
GDN_HEADS, GDN_DK, GDN_CONV, GDN_CHUNK = 8, 128, 4, 64
X_HEADS, X_HEAD_DIM = 4, 256
DEPTH, N_MIXERS = 4, 3
ADAM_LR, ADAM_B1, ADAM_B2, ADAM_EPS, ADAM_WD, ADAM_STEP = 0.001, 0.9, 0.999, 1e-08, 0.01, 10
MLA_QUERY_GROUPS = 4
NEG_BIG = -1e30
PACK_W = 1024


_NN = (((1,), (0,)), ((), ()))
_NT = (((1,), (1,)), ((), ()))
_TN = (((0,), (0,)), ((), ()))
_NN3 = (((2,), (1,)), ((0,), (0,)))
_NT3 = (((2,), (2,)), ((0,), (0,)))
_TN3 = (((1,), (1,)), ((0,), (0,)))


def _dot(a, b, dims):
    return lax.dot_general(a, b, dims, preferred_element_type=F32)


def _hi_lo(x):
    hi = x.astype(BF16)
    return hi, (x - hi.astype(F32)).astype(BF16)


def _split3(x):
    hi = x.astype(BF16)
    r = x - hi.astype(F32)
    mid = r.astype(BF16)
    return hi, mid, (r - mid.astype(F32)).astype(BF16)


def _dg(a, b, dims, prec):
    if prec == "h":
        return lax.dot_general(a, b, dims, precision=lax.Precision.HIGHEST, preferred_element_type=F32)
    if prec == "m":
        a_hi, a_lo = _hi_lo(a)
        b_hi, b_lo = _hi_lo(b)
        return _dot(a_hi, b_hi, dims) + _dot(a_hi, b_lo, dims) + _dot(a_lo, b_hi, dims)
    return _dot(a.astype(BF16), b.astype(BF16), dims)


def _dg_sel(sel, x, dims, sel_first):
    s16 = sel.astype(BF16)
    parts = [(_dot(s16, piece, dims) if sel_first else _dot(piece, s16, dims)) for piece in _split3(x)]
    return parts[0] + parts[1] + parts[2]


class _Ops:
    def __init__(self, prec, differentiable, batched=False):
        d_nn, d_nt, d_tn = (_NN3, _NT3, _TN3) if batched else (_NN, _NT, _TN)

        def nn(a, b):
            return _dg(a, b, d_nn, prec)

        def nt(a, b):
            return _dg(a, b, d_nt, prec)

        def tn(a, b):
            return _dg(a, b, d_tn, prec)

        if differentiable:
            dnn = jax.custom_vjp(nn)
            dnn.defvjp(lambda a, b: (nn(a, b), (a, b)), lambda r, g: (nt(g, r[1]), tn(r[0], g)))
            dnt = jax.custom_vjp(nt)
            dnt.defvjp(lambda a, b: (nt(a, b), (a, b)), lambda r, g: (nn(g, r[1]), tn(g, r[0])))
            dtn = jax.custom_vjp(tn)
            dtn.defvjp(lambda a, b: (tn(a, b), (a, b)), lambda r, g: (nt(r[1], g), nn(r[0], g)))
            nn, nt, tn = dnn, dnt, dtn
        self.nn, self.nt, self.tn = nn, nt, tn


class _SelOps:
    def __init__(self, differentiable, batched=False):
        d_nn, d_nt, d_tn = (_NN3, _NT3, _TN3) if batched else (_NN, _NT, _TN)

        def sel_nn(sel, x):
            return _dg_sel(sel, x, d_nn, True)

        def sel_nt(sel, x):
            return _dg_sel(sel, x, d_nt, True)

        if differentiable:
            dnn = jax.custom_vjp(sel_nn)
            dnn.defvjp(lambda s, x: (sel_nn(s, x), s),
                       lambda s, g: (jnp.zeros_like(s), _dg_sel(s, g, d_tn, True)))
            dnt = jax.custom_vjp(sel_nt)
            dnt.defvjp(lambda s, x: (sel_nt(s, x), s),
                       lambda s, g: (jnp.zeros_like(s), _dg_sel(s, g, d_tn, False)))
            sel_nn, sel_nt = dnn, dnt
        self.sel_nn, self.sel_nt = sel_nn, sel_nt


class _OpSet:
    def __init__(self, differentiable):
        self.b = _Ops("b", differentiable)
        self.h = _Ops("h", differentiable)
        self.bb = _Ops("b", differentiable, batched=True)
        self.bm = _Ops("m", differentiable, batched=True)
        self.bs = _SelOps(differentiable, batched=True)


_PLAIN = _OpSet(False)
_DIFF = _OpSet(True)


def _params(sem):
    return pltpu.CompilerParams(dimension_semantics=sem)


BLOCK_BYTES = 4 * 1024 * 1024


def _pick(n, cands):
    for c in cands:
        if n % c == 0:
            return c
    return n


def _tile(n, cap):
    if n <= cap:
        return n
    return _pick(n, tuple(c for c in (2048, 1024, 768, 512, 384, 256, 128) if c <= cap))


def matmul(a, b, form, out_dtype, name, res=None, blocked=False, relu_gate=None, rms_gain=None, relu2_out=False):
    if form == "nn":
        m, k = a.shape
        k2, n = (b.shape[1], N_DEV * b.shape[2]) if blocked else b.shape
    elif form == "nt":
        m, k = a.shape
        n, k2 = (b.shape[1], N_DEV * b.shape[2]) if blocked else b.shape
    else:
        (k, m), (k2, n) = a.shape, b.shape
    assert k == k2, (a.shape, b.shape, form)
    tk = k if k <= 2048 else _tile(k, 1024)
    cb = nb = 1
    if blocked:
        cb = (k if form == "nt" else n) // N_DEV
        nb = _pick(N_DEV, tuple(c for c in (8, 4, 2, 1) if c * cb <= 1024))
    if blocked and form == "nt":
        tk = nb * cb
    if blocked and form != "nt":
        tn = nb * cb
    else:
        tn = _tile(n, min(1024, BLOCK_BYTES // (tk * b.dtype.itemsize)))
    out_elems = BLOCK_BYTES // 2 if (out_dtype == BF16 and res is None) else BLOCK_BYTES // 4
    tm = _tile(m, min(BLOCK_BYTES // (tk * a.dtype.itemsize), out_elems // tn))
    nk = k // tk
    dims = {"nn": _NN, "nt": _NT, "tn": _TN}[form]

    a_spec = {"nn": pl.BlockSpec((tm, tk), lambda i, j, kk: (i, kk)),
              "nt": pl.BlockSpec((tm, tk), lambda i, j, kk: (i, kk)),
              "tn": pl.BlockSpec((tk, tm), lambda i, j, kk: (kk, i))}[form]
    if blocked and form == "nn":
        b_spec = pl.BlockSpec((nb, tk, cb), lambda i, j, kk: (j, kk, 0))
    elif blocked and form == "nt":
        b_spec = pl.BlockSpec((nb, tn, cb), lambda i, j, kk: (kk, j, 0))
    else:
        b_spec = {"nn": pl.BlockSpec((tk, tn), lambda i, j, kk: (kk, j)),
                  "nt": pl.BlockSpec((tn, tk), lambda i, j, kk: (j, kk)),
                  "tn": pl.BlockSpec((tk, tn), lambda i, j, kk: (kk, j))}[form]
    c_spec = pl.BlockSpec((tm, tn), lambda i, j, kk: (i, j))
    out_shape = jax.ShapeDtypeStruct((m, n), out_dtype)
    o_spec = c_spec
    blocked_out = blocked and form == "tn"
    if blocked_out:
        out_shape = jax.ShapeDtypeStruct((N_DEV, m, cb), out_dtype)
        o_spec = pl.BlockSpec((nb, tm, cb), lambda i, j, kk: (j, i, 0))
    has_res, has_gate, has_gain = res is not None, relu_gate is not None, rms_gain is not None
    extras = [e for e in (res, relu_gate) if e is not None]
    n_in = 2 + len(extras) + has_gain
    second = has_gain or relu2_out
    assert not (second and (blocked_out or tn != n and has_gain))

    def body(*refs):
        a_ref, b_ref = refs[0], refs[1]
        r_ref = refs[2] if has_res else None
        gate_ref = refs[2 + has_res] if has_gate else None
        gain_ref = refs[n_in - 1] if has_gain else None
        o_ref = refs[n_in]
        a_val = a_ref[...].astype(BF16)
        if blocked and form == "nn":
            part = jnp.concatenate([_dot(a_val, b_ref[t].astype(BF16), dims) for t in range(nb)], axis=-1)
        elif blocked and form == "nt":
            part = _dot(a_val[:, :cb], b_ref[0].astype(BF16), dims)
            for t in range(1, nb):
                part = part + _dot(a_val[:, t * cb:(t + 1) * cb], b_ref[t].astype(BF16), dims)
        else:
            part = _dot(a_val, b_ref[...].astype(BF16), dims)

        def finish(acc):
            if has_res:
                acc = acc + r_ref[...].astype(F32)
            if has_gate:
                acc = acc * (2.0 * jnp.maximum(gate_ref[...].astype(F32), 0.0))
            if blocked_out:
                for t in range(nb):
                    o_ref[t] = acc[:, t * cb:(t + 1) * cb].astype(out_dtype)
            else:
                o_ref[...] = acc.astype(out_dtype)
            if has_gain:
                refs[n_in + 1][...] = _rms(acc, gain_ref[...]).astype(BF16)
            if relu2_out:
                r = jnp.maximum(acc.astype(out_dtype).astype(F32), 0.0)
                refs[n_in + 1][...] = (r * r).astype(BF16)

        if nk == 1:
            finish(part)
        else:
            acc_ref = refs[-1]
            kk = pl.program_id(2)

            @pl.when(kk == 0)
            def _():
                acc_ref[...] = part

            @pl.when(jnp.logical_and(kk > 0, kk < nk - 1))
            def _():
                acc_ref[...] += part

            @pl.when(kk == nk - 1)
            def _():
                finish(acc_ref[...] + part)

    in_specs = [a_spec, b_spec] + [c_spec] * len(extras)
    args = [a, b] + extras
    if has_gain:
        in_specs.append(pl.BlockSpec((1, tn), lambda i, j, kk: (0, j)))
        args.append(rms_gain)
    if second:
        out_shape = [out_shape, jax.ShapeDtypeStruct((m, n), BF16)]
        o_spec = [o_spec, c_spec]
    return pl.pallas_call(
        body, name=name,
        out_shape=out_shape,
        grid=(m // tm, n // tn, nk),
        in_specs=in_specs, out_specs=o_spec,
        scratch_shapes=[pltpu.VMEM((tm, tn), F32)] if nk > 1 else [],
        compiler_params=_params(("parallel", "parallel", "arbitrary")),
    )(*args)


def make_mm(name, out_dtype, with_res=False, blocked=False):
    def bwd_mm(a, w, g):
        da = matmul(g, w, "nt", a.dtype, name + "_da", blocked=blocked)
        dw = matmul(a, g, "tn", w.dtype, name + "_dw", blocked=blocked)
        return da, dw

    if with_res:
        @jax.custom_vjp
        def op(res, a, w):
            return matmul(a, w, "nn", out_dtype, name + "_f", res=res, blocked=blocked)

        def fwd(res, a, w):
            return op(res, a, w), (a, w)

        def bwd(saved, g):
            return (g,) + bwd_mm(*saved, g)
    else:
        @jax.custom_vjp
        def op(a, w):
            return matmul(a, w, "nn", out_dtype, name + "_f", blocked=blocked)

        def fwd(a, w):
            return op(a, w), (a, w)

        def bwd(saved, g):
            return bwd_mm(*saved, g)
    op.defvjp(fwd, bwd)
    return op


def _rms_fan_bwd(x_new, gain, dx, dh, name):
    rows, d = x_new.shape
    outs = [("row", d, F32), ("row", d, BF16)]
    return tile_bwd(fn_fan_rms, name, ["row", "par"], [x_new, gain], [True, True], outs, [dx, dh],
                    rows, min(512, rows), 0)


def make_mm_res_rms(name):
    @jax.custom_vjp
    def op(res, a, w, gain):
        return tuple(matmul(a, w, "nn", F32, name + "_f", res=res, rms_gain=gain))

    def fwd(res, a, w, gain):
        x_new, h = op(res, a, w, gain)
        return (x_new, h), (a, w, x_new, gain)

    def bwd(saved, cts):
        a, w, x_new, gain = saved
        dx, dgain = _rms_fan_bwd(x_new, gain, cts[0], cts[1], name + "_nb")
        da = matmul(dx, w, "nt", a.dtype, name + "_da")
        dw = matmul(a, dx, "tn", w.dtype, name + "_dw")
        return dx, da, dw, dgain

    op.defvjp(fwd, bwd)
    return op


def make_mlp(name, with_norm):
    def run(x, h, w1, w2, gain):
        a, bsq = matmul(h, w1, "nn", BF16, name + "_1_f", blocked=True, relu2_out=True)
        out = matmul(bsq, w2, "nn", F32, name + "_2_f", res=x, rms_gain=gain if with_norm else None)
        return (tuple(out) if with_norm else (out,)), a, bsq

    @jax.custom_vjp
    def op(x, h, w1, w2, gain):
        return run(x, h, w1, w2, gain)[0]

    def fwd(x, h, w1, w2, gain):
        out, a, bsq = run(x, h, w1, w2, gain)
        return out, (h, w1, w2, gain, a, bsq, out[0])

    def bwd(saved, cts):
        h, w1, w2, gain, a, bsq, x_new = saved
        if with_norm:
            dx, dgain = _rms_fan_bwd(x_new, gain, cts[0], cts[1], name + "_nb")
        else:
            dx, dgain = cts[0], None
        da = matmul(dx, w2, "nt", BF16, name + "_2_da", relu_gate=a)
        dw2 = matmul(bsq, dx, "tn", w2.dtype, name + "_2_dw")
        dh = matmul(da, w1, "nt", h.dtype, name + "_1_da", blocked=True)
        dw1 = matmul(h, da, "tn", w1.dtype, name + "_1_dw", blocked=True)
        return dx, dh, dw1, dw2, dgain

    op.defvjp(fwd, bwd)
    return op


def _kind(k):
    if isinstance(k, str):
        return k, None, 1
    return k[0], k[1], (k[2] if len(k) > 2 else 1)


def _tile_spec(kind, shape, tm, heads):
    k, d, ns = _kind(kind)
    if k == "row":
        return pl.BlockSpec((tm, shape[1]), (lambda h, i: (i, 0)) if heads else (lambda i: (i, 0)))
    if k == "par":
        return pl.BlockSpec(tuple(shape), (lambda h, i: (0, 0)) if heads else (lambda i: (0, 0)))
    if k == "rowh":
        return pl.BlockSpec((tm, d * ns), lambda h, i: (i, h))
    if k == "parh":
        return pl.BlockSpec((shape[0], d * ns), lambda h, i: (0, h))
    raise ValueError(kind)


def _tile_grid(rows, tm, heads):
    n_rows = rows // tm
    return ((heads, n_rows) if heads else (n_rows,)), (1 if heads else 0)


def _split_vals(kinds, refs):
    vals, counts = [], []
    for kind, r in zip(kinds, refs):
        _, d, ns = _kind(kind)
        v = r[...].astype(F32)
        vals += [v] if ns == 1 else [v[:, p * d:(p + 1) * d] for p in range(ns)]
        counts.append(ns)
    return vals, counts


def tile_fwd(fn, name, kinds, args, outs, rows, tm, heads, row_base=0):
    grid, row_axis = _tile_grid(rows, tm, heads)
    n_in = len(args)
    out_shapes = [jax.ShapeDtypeStruct((rows, w), dt) for (_, w, dt) in outs]

    def body(*refs):
        vals, _ = _split_vals(kinds, refs[:n_in])
        row0 = row_base + pl.program_id(row_axis) * tm
        res = list(fn(_PLAIN, row0, *vals))
        for o_ref, (k, _, _) in zip(refs[n_in:], outs):
            pieces = [res.pop(0) for _ in range(_kind(k)[2])]
            v = pieces[0] if len(pieces) == 1 else jnp.concatenate(pieces, axis=-1)
            o_ref[...] = v.astype(o_ref.dtype)

    return pl.pallas_call(
        body, name=name, out_shape=out_shapes, grid=grid,
        in_specs=[_tile_spec(k, a.shape, tm, heads) for k, a in zip(kinds, args)],
        out_specs=[_tile_spec(k, (rows, w), tm, heads) for (k, w, _) in outs],
        compiler_params=_params(("arbitrary",) * len(grid)),
    )(*args)


def tile_bwd(fn, name, kinds, args, diff, outs, cts, rows, tm, heads, row_base=0):
    grid, row_axis = _tile_grid(rows, tm, heads)
    n_in, n_ct = len(args), len(cts)
    diff_idx = [i for i, d in enumerate(diff) if d]
    g_shapes, g_specs = [], []
    for i in diff_idx:
        k = _kind(kinds[i])[0]
        dt = args[i].dtype if k in ("row", "rowh") else F32
        g_shapes.append(jax.ShapeDtypeStruct(args[i].shape, dt))
        g_specs.append(_tile_spec(kinds[i], args[i].shape, tm, heads))

    def body(*refs):
        in_refs, ct_refs, g_refs = refs[:n_in], refs[n_in:n_in + n_ct], refs[n_in + n_ct:]
        vals, counts = _split_vals(kinds, in_refs)
        first_piece = [sum(counts[:i]) for i in range(n_in)]
        flat_diff = [first_piece[i] + p for i in diff_idx for p in range(counts[i])]
        row_id = pl.program_id(row_axis)
        row0 = row_base + row_id * tm

        def f(*dvals):
            full = list(vals)
            for i, dv in zip(flat_diff, dvals):
                full[i] = dv
            return tuple(fn(_DIFF, row0, *full))

        _, vjp = jax.vjp(f, *[vals[i] for i in flat_diff])
        ct_vals, _ = _split_vals([k for (k, _, _) in outs], ct_refs)
        flat_grads = list(vjp(tuple(ct_vals)))
        for g_ref, i in zip(g_refs, diff_idx):
            pieces = [flat_grads.pop(0) for _ in range(counts[i])]
            g = pieces[0] if len(pieces) == 1 else jnp.concatenate(pieces, axis=-1)
            k = _kind(kinds[i])[0]
            if k in ("row", "rowh"):
                g_ref[...] = g.astype(g_ref.dtype)
            else:
                first = row_id == 0
                if heads and k == "par":
                    first = jnp.logical_and(first, pl.program_id(0) == 0)

                @pl.when(first)
                def _(g_ref=g_ref, g=g):
                    g_ref[...] = g

                @pl.when(jnp.logical_not(first))
                def _(g_ref=g_ref, g=g):
                    g_ref[...] += g

    return pl.pallas_call(
        body, name=name, out_shape=g_shapes, grid=grid,
        in_specs=[_tile_spec(k, a.shape, tm, heads) for k, a in zip(kinds, args)]
        + [_tile_spec(k, (rows, w), tm, heads) for (k, w, _) in outs],
        out_specs=g_specs,
        compiler_params=_params(("arbitrary",) * len(grid)),
    )(*args, *cts)


def make_tile_op(fn, name, kinds, diff, outs, rows, tm, heads=0, row_base=0):
    tm = min(tm, rows)

    @jax.custom_vjp
    def op(*args):
        return tuple(tile_fwd(fn, name + "_f", kinds, args, outs, rows, tm, heads, row_base))

    def fwd(*args):
        return op(*args), args

    def bwd(args, cts):
        grads = tile_bwd(fn, name + "_b", kinds, args, diff, outs, cts, rows, tm, heads, row_base)
        it = iter(grads)
        res = []
        for a, d in zip(args, diff):
            res.append(next(it).astype(a.dtype) if d else None)
        return tuple(res)

    op.defvjp(fwd, bwd)
    return op


def _rms(x, g):
    return x * lax.rsqrt(jnp.mean(x * x, axis=-1, keepdims=True) + EPS) * g


def fn_rms(ops, row0, x, g):
    return (_rms(x, g),)


def fn_fan_rms(ops, row0, x, g):
    return x, _rms(x, g)


def fn_mul(ops, row0, a, b):
    return (a * b,)


def fn_mla_down(ops, row0, h, c, s, w_cq, w_ckv, w_kr, w_krs, g_q, g_kv):
    c_qn = _rms(ops.b.nn(h, w_cq), g_q)
    c_kvn = _rms(ops.b.nn(h, w_ckv), g_kv)
    return c_qn, c_kvn, ops.b.nn(h, w_kr) * c + ops.b.nn(h, w_krs) * s


def fn_mla_up(ops, row0, c_qn, c_kvn, c, s, w_qn, w_qr, w_qrs, w_kv):
    c_all = jnp.concatenate([c] * MLA_HEADS, axis=-1)
    s_all = jnp.concatenate([s] * MLA_HEADS, axis=-1)
    q_rope = ops.b.nn(c_qn, w_qr) * c_all + ops.b.nn(c_qn, w_qrs) * s_all
    return ops.b.nn(c_qn, w_qn), q_rope, ops.b.nn(c_kvn, w_kv)


def _softmax(s):
    m = lax.stop_gradient(jnp.max(s, axis=-1, keepdims=True))
    e = jnp.exp(s - m)
    return e / jnp.sum(e, axis=-1, keepdims=True)


def fn_xattn(ops, row0, hx, *t):
    w_q, k, v = t[:X_HEADS], t[X_HEADS:2 * X_HEADS], t[2 * X_HEADS:]
    outs = []
    for w_h, k_h, v_h in zip(w_q, k, v):
        s = ops.b.nt(ops.b.nn(hx, w_h), k_h) * (X_HEAD_DIM ** -0.5)
        outs.append(ops.b.nn(_softmax(s), v_h))
    return tuple(outs)


def _silu(x):
    return x * jax.nn.sigmoid(x)


def fn_gdn_prep(ops, row0, *t):
    nh = len(t) // 3
    qs, ks, vs = [], [], []
    for qc, kc, vc in zip(t[:nh], t[nh:2 * nh], t[2 * nh:]):
        q, k = _silu(qc), _silu(kc)
        qs.append(q * lax.rsqrt(jnp.sum(q * q, -1, keepdims=True) + EPS) * (GDN_DK ** -0.5))
        ks.append(k * lax.rsqrt(jnp.sum(k * k, -1, keepdims=True) + EPS))
        vs.append(_silu(vc))
    return tuple(qs + ks + vs)


def fn_gdn_gates(ops, row0, ba, alog, dtb):
    width = GDN_HEADS * GDN_DK
    beta = jax.nn.sigmoid(ba)
    z = ba + dtb
    softplus = jnp.maximum(z, 0.0) + jnp.log1p(jnp.exp(-jnp.abs(z)))
    g = -jnp.exp(alog) * softplus
    r = lax.broadcasted_iota(jnp.int32, (LANES, width), 0)
    c = lax.broadcasted_iota(jnp.int32, (LANES, width), 1) // GDN_DK
    e_beta = (r == c).astype(F32)
    e_g = (r == c + GDN_HEADS).astype(F32)
    return ops.h.nn(beta, e_beta), ops.h.nn(g, e_g)


def fn_gdn_out(ops, row0, *t):
    nh = (len(t) - 1) // 2
    g = t[-1]
    return tuple(_rms(o, g) * _silu(gate) for o, gate in zip(t[:nh], t[nh:2 * nh]))


def fn_mla_attn(ops, row0, qn, qr, kn, v, kr):
    s = (ops.b.nt(qn, kn) + ops.b.nt(qr, kr)) * ((MLA_NOPE + MLA_ROPE) ** -0.5)
    rows = row0 + lax.broadcasted_iota(jnp.int32, s.shape, 0)
    cols = lax.broadcasted_iota(jnp.int32, s.shape, 1)
    s = jnp.where(rows >= cols, s, NEG_BIG)
    return (ops.b.nn(_softmax(s), v),)


def _shift_down(x, d, t_idx):
    if d == 0:
        return x
    return jnp.where(t_idx >= d, pltpu.roll(x, d, axis=0), 0.0)


def _shift_up(x, d, t_idx):
    if d == 0:
        return x
    n = x.shape[0]
    return jnp.where(t_idx < n - d, pltpu.roll(x, n - d, axis=0), 0.0)


def conv_fwd(x, w, name):
    s, c = x.shape
    kw = w.shape[0]
    tc = _pick(c, (256, 128))

    def body(x_ref, w_ref, y_ref):
        xv = x_ref[...]
        t_idx = lax.broadcasted_iota(jnp.int32, xv.shape, 0)
        acc = jnp.zeros_like(xv)
        for j in range(kw):
            acc = acc + w_ref[j:j + 1, :] * _shift_down(xv, kw - 1 - j, t_idx)
        y_ref[...] = acc

    return pl.pallas_call(
        body, name=name, out_shape=jax.ShapeDtypeStruct((s, c), F32), grid=(c // tc,),
        in_specs=[pl.BlockSpec((s, tc), lambda i: (0, i)), pl.BlockSpec((kw, tc), lambda i: (0, i))],
        out_specs=pl.BlockSpec((s, tc), lambda i: (0, i)),
        compiler_params=_params(("parallel",)),
    )(x, w)


def conv_bwd(x, w, dy, name):
    s, c = x.shape
    kw = w.shape[0]
    tc = _pick(c, (256, 128))

    def body(x_ref, w_ref, dy_ref, dx_ref, dw_ref):
        xv, dyv = x_ref[...], dy_ref[...]
        t_idx = lax.broadcasted_iota(jnp.int32, xv.shape, 0)
        dx = jnp.zeros_like(xv)
        for j in range(kw):
            d = kw - 1 - j
            dx = dx + w_ref[j:j + 1, :] * _shift_up(dyv, d, t_idx)
            dw_ref[j:j + 1, :] = jnp.sum(dyv * _shift_down(xv, d, t_idx), axis=0, keepdims=True)
        dx_ref[...] = dx

    return pl.pallas_call(
        body, name=name,
        out_shape=[jax.ShapeDtypeStruct((s, c), F32), jax.ShapeDtypeStruct((kw, c), F32)],
        grid=(c // tc,),
        in_specs=[pl.BlockSpec((s, tc), lambda i: (0, i)), pl.BlockSpec((kw, tc), lambda i: (0, i)),
                  pl.BlockSpec((s, tc), lambda i: (0, i))],
        out_specs=[pl.BlockSpec((s, tc), lambda i: (0, i)), pl.BlockSpec((kw, tc), lambda i: (0, i))],
        compiler_params=_params(("parallel",)),
    )(x, w, dy)


def make_conv(name):
    @jax.custom_vjp
    def op(x, w):
        return conv_fwd(x, w, name + "_f")

    def fwd(x, w):
        return op(x, w), (x, w)

    def bwd(saved, dy):
        dx, dw = conv_bwd(saved[0], saved[1], dy, name + "_b")
        return dx, dw

    op.defvjp(fwd, bwd)
    return op


def _gdn_consts():
    c, d = GDN_CHUNK, GDN_DK
    i = lax.broadcasted_iota(jnp.int32, (c, c), 0)
    j = lax.broadcasted_iota(jnp.int32, (c, c), 1)
    tri = i >= j
    return dict(
        tri=tri, strict=i > j,
        tri_f=tri.astype(F32),
        eye=(i == j).astype(F32),
        lane0=(lax.broadcasted_iota(jnp.int32, (c, d), 1) == 0).astype(F32),
        last_row=(lax.broadcasted_iota(jnp.int32, (c, d), 0) == c - 1).astype(F32),
    )


def _inverse_given(m_ops):
    @jax.custom_vjp
    def given(mm_, t):
        return t

    def bwd(t, dt):
        return -m_ops.nt(m_ops.tn(t, dt), t), jnp.zeros_like(t)

    given.defvjp(lambda mm_, t: (t, t), bwd)
    return given


def _gdn_chunk(ops, q, k, v, g, beta, state, t_saved=None):
    b, m, sel = ops.bb, ops.bm, ops.bs
    nh, c, d = q.shape[0], GDN_CHUNK, GDN_DK
    k_ = _gdn_consts()

    def per_head(a):
        return jnp.broadcast_to(a, (nh,) + a.shape)

    gc = sel.sel_nn(per_head(k_["tri_f"]), g)
    col = jnp.broadcast_to(jnp.sum(gc * k_["lane0"], axis=2, keepdims=True), (nh, c, c))
    row = sel.sel_nt(per_head(k_["lane0"]), gc)
    decay = jnp.where(k_["tri"], jnp.exp(jnp.where(k_["tri"], col - row, 0.0)), 0.0)
    kb = k * beta
    mm_ = jnp.where(k_["strict"], b.nt(kb, k) * decay, 0.0)
    if t_saved is None:
        p = -mm_
        t = k_["eye"] + p
        for _ in range(int(math.log2(GDN_CHUNK)) - 1):
            p = m.nn(p, p)
            t = t + m.nn(t, p)
    else:
        t = _inverse_given(_PLAIN.bm)(mm_, t_saved)
    egc = jnp.exp(gc)
    u = b.nn(t, v * beta)
    w = b.nn(t, kb * egc)
    attn = b.nt(q, k) * decay
    v_new = u - b.nn(w, state)
    o = b.nn(q * egc, state) + b.nn(attn, v_new)
    g_last = jnp.sum(gc * k_["last_row"], axis=1, keepdims=True)
    new_state = (state * jnp.exp(jnp.broadcast_to(g_last, (nh, d, d)))
                 + b.tn(k * jnp.exp(jnp.broadcast_to(g_last, (nh, c, d)) - gc), v_new))
    return o, new_state, t


GDN_HEAD_GROUP = 8
GDN_TILE_CHUNKS = 4


def _heads_of(ref, rows, n_heads):
    d = GDN_DK
    return jnp.stack([ref[rows, h * d:(h + 1) * d] for h in range(n_heads)])


def _gdn_specs(s, reverse):
    d, hg = GDN_DK, GDN_HEAD_GROUP
    tile = min(GDN_TILE_CHUNKS * GDN_CHUNK, s)
    n_tiles = s // tile
    t_of = (lambda t: n_tiles - 1 - t) if reverse else (lambda t: t)
    seq = pl.BlockSpec((tile, hg * d), lambda grp, t: (t_of(t), grp))
    st = pl.BlockSpec((hg, tile // GDN_CHUNK, d, d), lambda grp, t: (grp, t_of(t), 0, 0))
    inv = pl.BlockSpec((hg, tile // GDN_CHUNK, GDN_CHUNK, GDN_CHUNK), lambda grp, t: (grp, t_of(t), 0, 0))
    return seq, st, inv, tile, n_tiles


def gdn_fwd(q, k, v, g, beta, name):
    s = q.shape[0]
    d, hg = GDN_DK, GDN_HEAD_GROUP
    seq, st, inv, tile, n_tiles = _gdn_specs(s, False)

    def body(q_ref, k_ref, v_ref, g_ref, b_ref, o_ref, st_ref, inv_ref, state_scr):
        @pl.when(pl.program_id(1) == 0)
        def _():
            state_scr[...] = jnp.zeros_like(state_scr)

        def step(ci, carry):
            rows = pl.ds(pl.multiple_of(ci * GDN_CHUNK, GDN_CHUNK), GDN_CHUNK)
            state = state_scr[...]
            for h in range(hg):
                st_ref[h, ci] = state[h]
            o, new_state, t = _gdn_chunk(_PLAIN, *[_heads_of(r, rows, hg) for r in (q_ref, k_ref, v_ref, g_ref, b_ref)],
                                         state)
            for h in range(hg):
                o_ref[rows, h * d:(h + 1) * d] = o[h]
                inv_ref[h, ci] = t[h]
            state_scr[...] = new_state
            return carry

        lax.fori_loop(0, tile // GDN_CHUNK, step, 0)

    return pl.pallas_call(
        body, name=name,
        out_shape=[jax.ShapeDtypeStruct(q.shape, F32),
                   jax.ShapeDtypeStruct((GDN_HEADS, s // GDN_CHUNK, d, d), F32),
                   jax.ShapeDtypeStruct((GDN_HEADS, s // GDN_CHUNK, GDN_CHUNK, GDN_CHUNK), F32)],
        grid=(GDN_HEADS // hg, n_tiles), in_specs=[seq] * 5, out_specs=[seq, st, inv],
        scratch_shapes=[pltpu.VMEM((hg, d, d), F32)],
        compiler_params=_params(("parallel", "arbitrary")),
    )(q, k, v, g, beta)


def gdn_bwd(q, k, v, g, beta, states, inverses, do, name):
    s = q.shape[0]
    d, hg = GDN_DK, GDN_HEAD_GROUP
    seq, st, inv, tile, n_tiles = _gdn_specs(s, True)
    tile_chunks = tile // GDN_CHUNK

    def body(q_ref, k_ref, v_ref, g_ref, b_ref, st_ref, inv_ref, do_ref, dq_ref, dk_ref, dv_ref, dg_ref, db_ref,
             dstate_scr):
        @pl.when(pl.program_id(1) == 0)
        def _():
            dstate_scr[...] = jnp.zeros_like(dstate_scr)

        def step(it, carry):
            ci = tile_chunks - 1 - it
            rows = pl.ds(pl.multiple_of(ci * GDN_CHUNK, GDN_CHUNK), GDN_CHUNK)
            prim = [_heads_of(r, rows, hg) for r in (q_ref, k_ref, v_ref, g_ref, b_ref)]
            prim.append(jnp.stack([st_ref[h, ci] for h in range(hg)]))
            t_saved = jnp.stack([inv_ref[h, ci] for h in range(hg)])
            _, vjp = jax.vjp(lambda *a: _gdn_chunk(_DIFF, *a, t_saved=t_saved)[:2], *prim)
            grads = vjp((_heads_of(do_ref, rows, hg), dstate_scr[...]))
            for g_ref_out, gr in zip((dq_ref, dk_ref, dv_ref, dg_ref, db_ref), grads[:5]):
                for h in range(hg):
                    g_ref_out[rows, h * d:(h + 1) * d] = gr[h]
            dstate_scr[...] = grads[5]
            return carry

        lax.fori_loop(0, tile_chunks, step, 0)

    return pl.pallas_call(
        body, name=name,
        out_shape=[jax.ShapeDtypeStruct(q.shape, F32)] * 5,
        grid=(GDN_HEADS // hg, n_tiles), in_specs=[seq] * 5 + [st, inv, seq], out_specs=[seq] * 5,
        scratch_shapes=[pltpu.VMEM((hg, d, d), F32)],
        compiler_params=_params(("parallel", "arbitrary")),
    )(q, k, v, g, beta, states, inverses, do)


def make_gdn(name):
    @jax.custom_vjp
    def op(q, k, v, g, beta):
        return gdn_fwd(q, k, v, g, beta, name + "_f")[0]

    def fwd(q, k, v, g, beta):
        o, states, inverses = gdn_fwd(q, k, v, g, beta, name + "_f")
        return o, (q, k, v, g, beta, states, inverses)

    def bwd(saved, do):
        return tuple(gdn_bwd(*saved, do, name + "_b"))

    op.defvjp(fwd, bwd)
    return op


def loss_head(x, g, target, name):
    s, d = x.shape
    tm = min(256, s)

    def body(x_ref, g_ref, t_ref, loss_ref, dx_ref, dg_ref):
        tgt = t_ref[...]

        def f(xv, gv):
            err = _rms(xv, gv) - tgt
            per_row = jnp.mean(err * err, axis=-1, keepdims=True)
            return 0.5 * jnp.sum(per_row, axis=0, keepdims=True)

        val, vjp = jax.vjp(f, x_ref[...], g_ref[...])
        dx, dg = vjp(jnp.ones((1, 1), F32))
        dx_ref[...] = dx
        first = pl.program_id(0) == 0

        @pl.when(first)
        def _():
            dg_ref[...] = dg
            loss_ref[...] = jnp.broadcast_to(val, loss_ref.shape)

        @pl.when(jnp.logical_not(first))
        def _():
            dg_ref[...] += dg
            loss_ref[...] += jnp.broadcast_to(val, loss_ref.shape)

    row = pl.BlockSpec((tm, d), lambda i: (i, 0))
    vec = pl.BlockSpec((1, d), lambda i: (0, 0))
    return pl.pallas_call(
        body, name=name,
        out_shape=[jax.ShapeDtypeStruct((1, LANES), F32), jax.ShapeDtypeStruct((s, d), F32),
                   jax.ShapeDtypeStruct((1, d), F32)],
        grid=(s // tm,), in_specs=[row, vec, row],
        out_specs=[pl.BlockSpec((1, LANES), lambda i: (0, 0)), row, vec],
        compiler_params=_params(("arbitrary",)),
    )(x, g, target)


def adamw(g8, w, m, v, layer, prev, name):
    n_layers, rows, width = w.shape
    tr = _pick(rows, (256, 128, 64, 32, 16, 8))

    def body(g_ref, w_ref, m_ref, v_ref, *rest):
        go_ref, d_ref, mo_ref, vo_ref = rest[-4:]
        g = g_ref[0].astype(F32)
        for p in range(1, N_DEV):
            g = g + g_ref[p].astype(F32)
        m_new = ADAM_B1 * m_ref[...] + (1.0 - ADAM_B1) * g
        v_new = ADAM_B2 * v_ref[...] + (1.0 - ADAM_B2) * (g * g)
        m_hat = m_new / (1.0 - ADAM_B1 ** ADAM_STEP)
        v_hat = v_new / (1.0 - ADAM_B2 ** ADAM_STEP)
        go_ref[...] = g
        d_ref[...] = -ADAM_LR * (m_hat / (jnp.sqrt(v_hat) + ADAM_EPS) + ADAM_WD * w_ref[...])
        mo_ref[...] = m_new
        vo_ref[...] = v_new

    blk = pl.BlockSpec((None, tr, width), lambda i: (layer, i, 0))
    carried = list(prev) if prev is not None else []
    return pl.pallas_call(
        body, name=name, out_shape=[jax.ShapeDtypeStruct((n_layers, rows, width), F32)] * 4,
        grid=(rows // tr,),
        in_specs=[pl.BlockSpec((N_DEV, tr, width), lambda i: (0, i, 0)), blk, blk, blk]
        + [pl.BlockSpec(memory_space=pl.ANY)] * len(carried),
        out_specs=[blk] * 4,
        input_output_aliases={4 + j: j for j in range(len(carried))},
        compiler_params=_params(("parallel",)),
    )(g8, w, m, v, *carried)


_HBM = pl.BlockSpec(memory_space=pltpu.HBM)
_SEM = pl.BlockSpec(memory_space=pltpu.SEMAPHORE)
_EFFECT = pltpu.SideEffectType.DATAFLOW_SIDE_EFFECTING


def _exchange_copies(mode, src_refs, land_refs, send_sems, recv_sems, local_sems):
    x, y, c = lax.axis_index("x"), lax.axis_index("y"), lax.axis_index("c")
    me = 4 * x + 2 * y + c
    n = len(src_refs)

    def src(k, p):
        return src_refs[k] if mode == "gather" else src_refs[k].at[p]

    local = [pltpu.make_async_copy(src(k, me), land_refs[k].at[me], local_sems.at[k]) for k in range(n)]
    sends, recvs = [], []
    for k in range(n):
        for r in range(1, N_DEV):
            px = (1 - x) if r & 4 else x
            py = (1 - y) if r & 2 else y
            pc = (1 - c) if r & 1 else c
            p = 4 * px + 2 * py + pc
            sem = k * (N_DEV - 1) + r - 1
            sends.append(pltpu.make_async_remote_copy(
                src_ref=src(k, p), dst_ref=land_refs[k].at[me],
                send_sem=send_sems.at[sem], recv_sem=recv_sems.at[sem],
                device_id=(px, py, pc), device_id_type=pl.DeviceIdType.MESH))
            recvs.append(pltpu.make_async_remote_copy(
                src_ref=src(k, p), dst_ref=land_refs[k].at[p],
                send_sem=send_sems.at[sem], recv_sem=recv_sems.at[sem],
                device_id=(px, py, pc), device_id_type=pl.DeviceIdType.MESH))
    return local, sends, recvs


def exchange_start(mode, arrays, name, carry=()):
    n, nc = len(arrays), len(carry)
    land_shapes = [((N_DEV,) + tuple(a.shape)) if mode == "gather" else tuple(a.shape) for a in arrays]
    lands = [pltpu.with_memory_space_constraint(lax.empty(shp, a.dtype), pltpu.HBM)
             for shp, a in zip(land_shapes, arrays)]
    srcs = [pltpu.with_memory_space_constraint(a, pltpu.HBM) for a in arrays]
    carried = [pltpu.with_memory_space_constraint(a, pltpu.HBM) for a in carry]

    def body(*refs):
        src_refs, land_refs = refs[:n], refs[n:2 * n]
        first_out = 2 * n + nc
        send_sems, recv_sems, local_sems = refs[first_out:first_out + 3]
        token = refs[-1]
        local, sends, _ = _exchange_copies(mode, src_refs, land_refs, send_sems, recv_sems, local_sems)
        for cp in local + sends:
            cp.start()
        token[...] = jnp.zeros_like(token)

    n_sem = n * (N_DEV - 1)
    out = pl.pallas_call(
        body, name=name,
        out_shape=(pltpu.SemaphoreType.DMA((n_sem,)), pltpu.SemaphoreType.DMA((n_sem,)),
                   pltpu.SemaphoreType.DMA((n,)),
                   *[pltpu.HBM(a.shape, a.dtype) for a in arrays],
                   *[pltpu.HBM(shp, a.dtype) for shp, a in zip(land_shapes, arrays)],
                   *[pltpu.HBM(a.shape, a.dtype) for a in carry],
                   jax.ShapeDtypeStruct((8, LANES), F32)),
        in_specs=[_HBM] * (2 * n + nc),
        out_specs=(_SEM, _SEM, _SEM, *[_HBM] * (2 * n + nc), pl.BlockSpec(memory_space=pltpu.VMEM)),
        input_output_aliases={i: 3 + i for i in range(2 * n + nc)},
        compiler_params=pltpu.CompilerParams(has_side_effects=_EFFECT),
    )(*srcs, *lands, *carried)
    handle = dict(mode=mode, sems=out[:3], srcs=out[3:3 + n], lands=out[3 + n:3 + 2 * n])
    return handle, out[-1], list(out[3 + 2 * n:3 + 2 * n + nc])


def exchange_wait(handle, after, name):
    mode, srcs, lands = handle["mode"], list(handle["srcs"]), list(handle["lands"])
    n = len(srcs)

    def body(*refs):
        src_refs, land_refs = refs[:n], refs[n:2 * n]
        send_sems, recv_sems, local_sems = refs[2 * n:2 * n + 3]
        local, sends, recvs = _exchange_copies(mode, src_refs, land_refs, send_sems, recv_sems, local_sems)
        for cp in sends:
            cp.wait_send()
        for cp in recvs:
            cp.wait_recv()
        for cp in local:
            cp.wait()

    out = pl.pallas_call(
        body, name=name,
        out_shape=(*[pltpu.HBM(a.shape, a.dtype) for a in srcs], *[pltpu.HBM(a.shape, a.dtype) for a in lands]),
        in_specs=[_HBM] * (2 * n) + [_SEM] * 3 + [pl.BlockSpec(memory_space=pl.ANY)],
        out_specs=tuple([_HBM] * (2 * n)),
        input_output_aliases={i: i for i in range(2 * n)},
        compiler_params=pltpu.CompilerParams(has_side_effects=_EFFECT),
    )(*srcs, *lands, *handle["sems"], after)
    return list(out[n:])


_ICI_RELATIONS = (2, 4, 6)


def _mesh_place():
    x, y, c = lax.axis_index("x"), lax.axis_index("y"), lax.axis_index("c")

    def peer(r):
        px = (1 - x) if r & 4 else x
        py = (1 - y) if r & 2 else y
        pc = (1 - c) if r & 1 else c
        return (px, py, pc), 4 * px + 2 * py + pc

    return 4 * x + 2 * y + c, peer


def _remote(src, dst, send_sem, recv_sem, device):
    return pltpu.make_async_remote_copy(src_ref=src, dst_ref=dst, send_sem=send_sem, recv_sem=recv_sem,
                                        device_id=device, device_id_type=pl.DeviceIdType.MESH)


def gather2_start(groups, name):
    flat = [a for g in groups for a in g]
    n = len(flat)
    lands = [pltpu.with_memory_space_constraint(lax.empty((N_DEV,) + tuple(a.shape), a.dtype), pltpu.HBM) for a in flat]
    srcs = [pltpu.with_memory_space_constraint(a, pltpu.HBM) for a in flat]
    n_rel = 1 + len(_ICI_RELATIONS)

    def body(*refs):
        src_refs, land_refs = refs[:n], refs[n:2 * n]
        sem_refs = refs[2 * n:2 * n + 4 * len(groups)]
        me, peer = _mesh_place()
        k = 0
        for gi, g in enumerate(groups):
            send_sems, recv_sib, recv_ici, local_sems = sem_refs[4 * gi:4 * gi + 4]
            for j in range(len(g)):
                pltpu.make_async_copy(src_refs[k], land_refs[k].at[me], local_sems.at[j]).start()
                dev, _ = peer(1)
                _remote(src_refs[k], land_refs[k].at[me], send_sems.at[n_rel * j], recv_sib.at[j], dev).start()
                for t, r in enumerate(_ICI_RELATIONS):
                    dev, _ = peer(r)
                    _remote(src_refs[k], land_refs[k].at[me], send_sems.at[n_rel * j + 1 + t],
                            recv_ici.at[len(_ICI_RELATIONS) * j + t], dev).start()
                k += 1
        refs[-1][...] = jnp.zeros_like(refs[-1])

    sem_shapes = []
    for g in groups:
        sem_shapes += [pltpu.SemaphoreType.DMA((n_rel * len(g),)), pltpu.SemaphoreType.DMA((len(g),)),
                       pltpu.SemaphoreType.DMA((len(_ICI_RELATIONS) * len(g),)), pltpu.SemaphoreType.DMA((len(g),))]
    out = pl.pallas_call(
        body, name=name,
        out_shape=(*sem_shapes, *[pltpu.HBM(a.shape, a.dtype) for a in flat],
                   *[pltpu.HBM((N_DEV,) + tuple(a.shape), a.dtype) for a in flat],
                   jax.ShapeDtypeStruct((8, LANES), F32)),
        in_specs=[_HBM] * (2 * n),
        out_specs=(*[_SEM] * len(sem_shapes), *[_HBM] * (2 * n), pl.BlockSpec(memory_space=pltpu.VMEM)),
        input_output_aliases={i: len(sem_shapes) + i for i in range(2 * n)},
        compiler_params=pltpu.CompilerParams(has_side_effects=_EFFECT),
    )(*srcs, *lands)
    handles, k, base = [], 0, len(sem_shapes)
    for gi, g in enumerate(groups):
        handles.append(dict(sems=out[4 * gi:4 * gi + 4], srcs=out[base + k:base + k + len(g)],
                            lands=out[base + n + k:base + n + k + len(g)]))
        k += len(g)
    return handles, out[-1]


def gather2_forward(handle, after, name, carry=()):
    lands, nc = list(handle["lands"]), len(carry)
    n, n_ici = len(lands), len(_ICI_RELATIONS)
    carried = [pltpu.with_memory_space_constraint(a, pltpu.HBM) for a in carry]

    def body(*refs):
        land_refs = refs[:n]
        recv_ici = refs[n + nc]
        fwd_send, fwd_recv = refs[n + nc + 2], refs[n + nc + 3]
        me, peer = _mesh_place()
        sibling, _ = peer(1)
        for j in range(n):
            for t, r in enumerate(_ICI_RELATIONS):
                dev, p = peer(r)
                landed = land_refs[j].at[p]
                _remote(landed, landed, fwd_send.at[n_ici * j + t], recv_ici.at[n_ici * j + t], dev).wait_recv()
                _remote(landed, landed, fwd_send.at[n_ici * j + t], fwd_recv.at[n_ici * j + t], sibling).start()

    out = pl.pallas_call(
        body, name=name,
        out_shape=(pltpu.SemaphoreType.DMA((n_ici * n,)), pltpu.SemaphoreType.DMA((n_ici * n,)),
                   *[pltpu.HBM(a.shape, a.dtype) for a in lands], *[pltpu.HBM(a.shape, a.dtype) for a in carry]),
        in_specs=[_HBM] * (n + nc) + [_SEM, pl.BlockSpec(memory_space=pl.ANY)],
        out_specs=(_SEM, _SEM, *[_HBM] * (n + nc)),
        input_output_aliases={i: 2 + i for i in range(n + nc)},
        compiler_params=pltpu.CompilerParams(has_side_effects=_EFFECT),
    )(*lands, *carried, handle["sems"][2], after)
    new_handle = dict(sems=handle["sems"], srcs=handle["srcs"], lands=out[2:2 + n], fwd=out[:2])
    return new_handle, list(out[2 + n:])


def gather2_wait(handle, after, name):
    srcs, lands = list(handle["srcs"]), list(handle["lands"])
    n, n_ici = len(srcs), len(_ICI_RELATIONS)
    n_rel = 1 + n_ici
    send_all, recv_sibling, _, local_all = handle["sems"]

    def body(*refs):
        src_refs, land_refs = refs[:n], refs[n:2 * n]
        send_sems, recv_sib, local_sems, fwd_send, fwd_recv = refs[2 * n:2 * n + 5]
        me, peer = _mesh_place()
        sibling, sib = peer(1)
        for j in range(n):
            pltpu.make_async_copy(src_refs[j], land_refs[j].at[me], local_sems.at[j]).wait()
            _remote(src_refs[j], land_refs[j].at[sib], send_sems.at[n_rel * j], recv_sib.at[j], sibling).wait()
            for t, r in enumerate(_ICI_RELATIONS):
                dev, p = peer(r)
                _remote(src_refs[j], land_refs[j].at[me], send_sems.at[n_rel * j + 1 + t],
                        recv_sib.at[j], dev).wait_send()
                _, p_sib = peer(r ^ 1)
                _remote(land_refs[j].at[p], land_refs[j].at[p_sib], fwd_send.at[n_ici * j + t],
                        fwd_recv.at[n_ici * j + t], sibling).wait()

    out = pl.pallas_call(
        body, name=name,
        out_shape=(*[pltpu.HBM(a.shape, a.dtype) for a in srcs], *[pltpu.HBM(a.shape, a.dtype) for a in lands]),
        in_specs=[_HBM] * (2 * n) + [_SEM] * 5 + [pl.BlockSpec(memory_space=pl.ANY)],
        out_specs=tuple([_HBM] * (2 * n)),
        input_output_aliases={i: i for i in range(2 * n)},
        compiler_params=pltpu.CompilerParams(has_side_effects=_EFFECT),
    )(*srcs, *lands, send_all, recv_sibling, local_all, *handle["fwd"], after)
    return list(out[n:])


BIG = ["mla_w_in", "mla_w_uq", "mla_w_ukv", "mla_w_o", "gdn_w_in", "gdn_w_o", "sc_w_in", "sc_w_o",
       "xa_w_q", "xa_w_kv", "xa_w_o", "mlp_w1", "mlp_w2"]
TINY = [("mla_q_norm", 1), ("mla_kv_norm", 1), ("gdn_conv_w", 2), ("sc_conv_w", 2)]
REPL = ["gdn_a_log", "gdn_dt_bias", "gdn_o_norm", "norm_mix", "norm_mem", "norm_mlp", "mem_norm", "final_norm"]
WEIGHTS = ["mla_w_in", "mla_q_norm", "mla_kv_norm", "mla_w_uq", "mla_w_ukv", "mla_w_o", "gdn_w_in",
           "gdn_conv_w", "gdn_a_log", "gdn_dt_bias", "gdn_o_norm", "gdn_w_o", "sc_w_in", "sc_conv_w",
           "sc_w_o", "norm_mix", "norm_mem", "norm_mlp", "xa_w_q", "xa_w_kv", "xa_w_o", "mlp_w1",
           "mlp_w2", "mem_norm", "final_norm"]
MIXER_WEIGHTS = (["mla_w_in", "mla_w_uq", "mla_w_ukv", "mla_w_o"], ["gdn_w_in", "gdn_w_o"], ["sc_w_in", "sc_w_o"])
MIXER_PARAMS = (["norm_mem", "mla_q_norm", "mla_kv_norm"],
                ["norm_mem", "gdn_conv_w", "gdn_a_log", "gdn_dt_bias", "gdn_o_norm"],
                ["norm_mem", "sc_conv_w"])


def from_shards(a8, axis):
    a = jnp.moveaxis(a8, 0, axis)
    shp = a.shape
    return a.reshape(shp[:axis] + (shp[axis] * shp[axis + 1],) + shp[axis + 2:])


def pack_rows(flat_list, width, row_mult):
    total = sum(a.shape[-1] for a in flat_list)
    rows = -(-total // width)
    rows = -(-rows // row_mult) * row_mult
    pad = rows * width - total
    parts = list(flat_list)
    if pad:
        parts.append(jnp.zeros((pad,), flat_list[0].dtype))
    return jnp.concatenate(parts, axis=-1).reshape(rows, width)


def unpack_rows(packed, shapes):
    lead = packed.shape[:-2]
    flat = packed.reshape(lead + (-1,))
    out, off = [], 0
    for shp in shapes:
        n = math.prod(shp)
        out.append(flat[..., off:off + n].reshape(lead + tuple(shp)))
        off += n
    return out


def _swap_halves(w):
    half = w.shape[-1] // 2
    return jnp.concatenate([w[..., half:], w[..., :half]], axis=-1)


def _pad_last(w, n):
    return jnp.pad(w, [(0, 0)] * (w.ndim - 1) + [(0, n - w.shape[-1])])


def _unblock(w8):
    return jnp.transpose(w8, (1, 0, 2)).reshape(w8.shape[1], -1)


def _stack_rows(w8):
    return w8.reshape(-1, w8.shape[-1])


def rms_op(name, rows, d, out_dtype):
    tm = rows if rows * d * 4 <= BLOCK_BYTES else 512
    return make_tile_op(fn_rms, name, ["row", "par"], [True, True], [("row", d, out_dtype)], rows, min(tm, rows))


def seg_memory(p, mem):
    return rms_op("rms_memory", mem.shape[0], mem.shape[1], BF16)(mem, p["mem_norm"].reshape(1, -1))[0]


def seg_mixer(i, wts, p, x, h, rope_c, rope_s):
    s, d = x.shape
    j, kind = i // N_MIXERS, i % N_MIXERS
    tag = f"l{i}"
    hd = MLA_NOPE
    next_gain = p["norm_mem"][i].reshape(1, d)
    if kind == 0:
        w_in = _stack_rows(wts["mla_w_in"])
        w_cq = w_in[:, :MLA_Q_RANK]
        w_ckv = w_in[:, MLA_Q_RANK:MLA_Q_RANK + MLA_KV_RANK]
        w_kr = w_in[:, MLA_Q_RANK + MLA_KV_RANK:]
        c_qn, c_kvn, k_rope = make_tile_op(
            fn_mla_down, tag + "_mla_down", ["row", "row", "row"] + ["par"] * 6, [True, False, False] + [True] * 6,
            [("row", MLA_Q_RANK, BF16), ("row", MLA_KV_RANK, BF16), ("row", hd, F32)], s, 256)(
            h, rope_c, rope_s, w_cq, w_ckv, _pad_last(w_kr, hd), _pad_last(_swap_halves(w_kr), hd),
            p["mla_q_norm"][j].reshape(1, -1), p["mla_kv_norm"][j].reshape(1, -1))
        w_uq8 = wts["mla_w_uq"]
        w_qn = _unblock(w_uq8[:, :, :MLA_NOPE])
        w_qr = w_uq8[:, :, MLA_NOPE:]
        w_qr_p = _unblock(_pad_last(w_qr, hd))
        w_qr_s = _unblock(_pad_last(_swap_halves(w_qr), hd))
        nq = MLA_HEADS * hd
        q_nope, q_rope, kv = make_tile_op(
            fn_mla_up, tag + "_mla_up", ["row", "row", "row", "row"] + ["par"] * 4, [True, True, False, False] + [True] * 4,
            [("row", nq, BF16), ("row", nq, F32), ("row", 2 * nq, BF16)], s, 256)(
            c_qn, c_kvn, rope_c, rope_s, w_qn, w_qr_p, w_qr_s, _unblock(wts["mla_w_ukv"]))
        n_groups = MLA_QUERY_GROUPS if s % (MLA_QUERY_GROUPS * 256) == 0 else 1
        rows_g = s // n_groups
        o_groups = []
        for grp in range(n_groups):
            r0, r1 = grp * rows_g, (grp + 1) * rows_g
            o_groups.append(make_tile_op(
                fn_mla_attn, f"{tag}_mla_attn{grp}", [("rowh", hd), ("rowh", hd), ("parh", hd, 2), "par"],
                [True] * 4, [(("rowh", hd), nq, BF16)], rows_g, 256, MLA_HEADS, row_base=r0)(
                q_nope[r0:r1], q_rope[r0:r1], kv[:r1], k_rope[:r1])[0])
        o = jnp.concatenate(o_groups, axis=0)
        return make_mm_res_rms(tag + "_mla_o")(x, o, _stack_rows(wts["mla_w_o"]), next_gain)
    if kind == 1:
        ng = GDN_HEADS * GDN_DK
        w_in = _unblock(wts["gdn_w_in"])
        cw = p["gdn_conv_w"][j]
        conv_out = []
        for part, nm in enumerate(("q", "k", "v")):
            cols = slice(part * ng, (part + 1) * ng)
            pre = make_mm(f"{tag}_gdn_in_{nm}", F32)(h, w_in[:, cols])
            conv_out.append(make_conv(f"{tag}_gdn_conv_{nm}")(pre, cw[:, cols]))
        gate = make_mm(tag + "_gdn_in_g", F32)(h, w_in[:, 3 * ng:4 * ng])
        ba = make_mm(tag + "_gdn_in_ba", F32)(h, _pad_last(w_in[:, 4 * ng:], LANES))
        heads_row = ("row", GDN_DK, GDN_HEADS)
        q, k, v = make_tile_op(fn_gdn_prep, tag + "_gdn_prep", [heads_row] * 3, [True] * 3,
                               [(heads_row, ng, F32)] * 3, s, 512)(*conv_out)
        alog = jnp.pad(p["gdn_a_log"][j].reshape(1, -1), ((0, 0), (GDN_HEADS, LANES - 2 * GDN_HEADS)))
        dtb = jnp.pad(p["gdn_dt_bias"][j].reshape(1, -1), ((0, 0), (GDN_HEADS, LANES - 2 * GDN_HEADS)))
        beta_b, g_b = make_tile_op(fn_gdn_gates, tag + "_gdn_gates", ["row", "par", "par"], [True] * 3,
                                   [("row", ng, F32)] * 2, s, 512)(ba, alog, dtb)
        o = make_gdn(tag + "_gdn_core")(q, k, v, g_b, beta_b)
        o = make_tile_op(fn_gdn_out, tag + "_gdn_out", [heads_row, heads_row, "par"],
                         [True] * 3, [(heads_row, ng, BF16)], s, 512)(
            o, gate, p["gdn_o_norm"][j].reshape(1, -1))[0]
        return make_mm_res_rms(tag + "_gdn_o")(x, o, _stack_rows(wts["gdn_w_o"]), next_gain)
    w_in = _unblock(wts["sc_w_in"])
    b_gate = make_mm(tag + "_sc_in_b", F32)(h, w_in[:, :d])
    c_gate = make_mm(tag + "_sc_in_c", F32)(h, w_in[:, d:2 * d])
    u = make_mm(tag + "_sc_in_u", F32)(h, w_in[:, 2 * d:])
    cu = make_tile_op(fn_mul, tag + "_sc_cu", ["row", "row"], [True, True], [("row", d, F32)], s, 512)(
        c_gate, u)[0]
    cv = make_conv(tag + "_sc_conv")(cu, p["sc_conv_w"][j])
    yv = make_tile_op(fn_mul, tag + "_sc_gate", ["row", "row"], [True, True], [("row", d, BF16)], s, 512)(
        b_gate, cv)[0]
    return make_mm_res_rms(tag + "_sc_o")(x, yv, _stack_rows(wts["sc_w_o"]), next_gain)


def seg_xattn(i, wts, p, x, hx, mem_n):
    s, d = x.shape
    tag = f"l{i}"
    kv = make_mm(tag + "_xa_kv", BF16, blocked=True)(mem_n, wts["xa_w_kv"])
    heads = ("row", X_HEAD_DIM, X_HEADS)
    o = make_tile_op(fn_xattn, tag + "_xattn",
                     ["row", ("par", X_HEAD_DIM, X_HEADS), ("par", X_HEAD_DIM, 2 * X_HEADS)], [True] * 3,
                     [(heads, d, BF16)], s, 512)(hx, _stack_rows(wts["xa_w_q"]), kv)[0]
    return make_mm_res_rms(tag + "_xa_o")(x, o, _stack_rows(wts["xa_w_o"]), p["norm_mlp"][i].reshape(1, d))


def seg_mlp(i, wts, p, x, hm):
    d = x.shape[1]
    gain = p["norm_mix"][i + 1].reshape(1, d) if i + 1 < DEPTH else None
    return make_mlp(f"l{i}_mlp", gain is not None)(x, hm, wts["mlp_w1"], _stack_rows(wts["mlp_w2"]), gain)


def segments():
    segs = []
    for i in range(DEPTH):
        j, kind = i // N_MIXERS, i % N_MIXERS
        segs.append((f"l{i}_mixer", [(n, j) for n in MIXER_WEIGHTS[kind]], MIXER_PARAMS[kind], "mixer"))
        segs.append((f"l{i}_xattn", [(n, i) for n in ("xa_w_q", "xa_w_kv", "xa_w_o")], ["norm_mlp"], "xattn"))
        segs.append((f"l{i}_mlp", [(n, i) for n in ("mlp_w1", "mlp_w2")], ["norm_mix"] if i + 1 < DEPTH else [],
                     "mlp"))
    return segs


def run_segment(index, kind, wts, p, x, h, mem_n, rope_c, rope_s):
    layer = index // 3
    if kind == "mixer":
        return seg_mixer(layer, wts, p, x, h, rope_c, rope_s)
    if kind == "xattn":
        return seg_xattn(layer, wts, p, x, h, mem_n)
    return seg_mlp(layer, wts, p, x, h)


def rope_tables(positions):
    inv_freq = ROPE_THETA ** (-jnp.arange(0, MLA_ROPE, 2, dtype=F32) / MLA_ROPE)
    ang = positions.astype(F32)[:, None] * inv_freq
    cos, sin = jnp.cos(ang), jnp.sin(ang)
    zeros = jnp.zeros((positions.shape[0], MLA_NOPE - MLA_ROPE), F32)
    return jnp.concatenate([cos, cos, zeros], axis=-1), jnp.concatenate([-sin, sin, zeros], axis=-1)


def kernel(x, mem, positions, mla_w_in, mla_q_norm, mla_kv_norm, mla_w_uq, mla_w_ukv, mla_w_o, gdn_w_in, gdn_conv_w, gdn_a_log, gdn_dt_bias, gdn_o_norm, gdn_w_o, sc_w_in, sc_conv_w, sc_w_o, norm_mix, norm_mem, norm_mlp, xa_w_q, xa_w_kv, xa_w_o, mlp_w1, mlp_w2, mem_norm, final_norm, loss_target, m_mla_w_in, m_mla_q_norm, m_mla_kv_norm, m_mla_w_uq, m_mla_w_ukv, m_mla_w_o, m_gdn_w_in, m_gdn_conv_w, m_gdn_a_log, m_gdn_dt_bias, m_gdn_o_norm, m_gdn_w_o, m_sc_w_in, m_sc_conv_w, m_sc_w_o, m_norm_mix, m_norm_mem, m_norm_mlp, m_xa_w_q, m_xa_w_kv, m_xa_w_o, m_mlp_w1, m_mlp_w2, m_mem_norm, m_final_norm, v_mla_w_in, v_mla_q_norm, v_mla_kv_norm, v_mla_w_uq, v_mla_w_ukv, v_mla_w_o, v_gdn_w_in, v_gdn_conv_w, v_gdn_a_log, v_gdn_dt_bias, v_gdn_o_norm, v_gdn_w_o, v_sc_w_in, v_sc_conv_w, v_sc_w_o, v_norm_mix, v_norm_mem, v_norm_mlp, v_xa_w_q, v_xa_w_kv, v_xa_w_o, v_mlp_w1, v_mlp_w2, v_mem_norm, v_final_norm):
    args = locals()
    w_loc = {n: args[n] for n in WEIGHTS}
    m_loc = {n: args["m_" + n] for n in WEIGHTS}
    v_loc = {n: args["v_" + n] for n in WEIGHTS}
    me = 4 * lax.axis_index("x") + 2 * lax.axis_index("y") + lax.axis_index("c")
    segs = segments()

    w16 = {n: w_loc[n].astype(BF16) for n in BIG}
    tiny_pack = pack_rows([w_loc[n].reshape(-1) for n, _ in TINY], LANES, 8)
    gather_handles, token = gather2_start(
        [[tiny_pack]] + [[w16[n][layer] for n, layer in units] for _, units, _, _ in segs], "gather_start")

    x_cur = x[0]
    rope_c, rope_s = rope_tables(positions[0])
    tiny_handle, _ = gather2_forward(gather_handles[0], token, "gather_forward_tiny")
    tiny_all = gather2_wait(tiny_handle, token, "gather_wait_tiny")[0]
    gather_handles = gather_handles[1:]
    params = {}
    for (n, ax), a8 in zip(TINY, unpack_rows(tiny_all, [w_loc[n].shape for n, _ in TINY])):
        params[n] = from_shards(a8, ax)
    for n in REPL:
        params[n] = w_loc[n]

    mem_n, vjp_memory = jax.vjp(lambda p_: seg_memory(p_, mem[0]), {"mem_norm": params["mem_norm"]})
    h_cur, vjp_first_norm = jax.vjp(
        lambda p_, x_: rms_op("l0_rms_mix", x_.shape[0], x_.shape[1], BF16)(x_, p_["norm_mix"][0].reshape(1, -1))[0],
        {"norm_mix": params["norm_mix"]}, x_cur)
    vjps = []
    forwarded, _ = gather2_forward(gather_handles[0], token, f"gather_forward_{segs[0][0]}")
    for index, (tag, units, p_names, kind) in enumerate(segs):
        landed = gather2_wait(forwarded, token if index == 0 else x_cur, f"gather_wait_{tag}")
        wts = {n: a for (n, _), a in zip(units, landed)}
        p_seg = {n: params[n] for n in p_names}
        if index + 1 < len(segs):
            forwarded, (p_seg[p_names[0]],) = gather2_forward(
                gather_handles[index + 1], landed[0], f"gather_forward_{segs[index + 1][0]}",
                carry=[p_seg[p_names[0]]])
        outs, vjp_seg = jax.vjp(
            lambda w_, p_, x_, h_, m_, index=index, kind=kind:
            run_segment(index, kind, w_, p_, x_, h_, m_, rope_c, rope_s),
            wts, p_seg, x_cur, h_cur, mem_n)
        x_cur, h_cur = outs[0], (outs[1] if len(outs) > 1 else None)
        vjps.append(vjp_seg)

    loss_vec, g_x, d_final = loss_head(x_cur, params["final_norm"].reshape(1, -1), loss_target[0], "loss_head")

    grads = {n: jnp.zeros_like(params[n]) for n in params}
    grads["final_norm"] = d_final.reshape(-1)
    g_mem_n = jnp.zeros_like(mem_n)
    g_h = None
    scatter_handles = []
    for (tag, units, _, _), vjp_seg in zip(reversed(segs), reversed(vjps)):
        g_wts, g_p, g_x, g_h, g_m = vjp_seg((g_x,) if g_h is None else (g_x, g_h))
        for n, g in g_p.items():
            grads[n] = grads[n] + g
        g_mem_n = g_mem_n + g_m
        handle, _, (g_h,) = exchange_start("scatter", [g_wts[n] for n, _ in units], f"scatter_start_{tag}",
                                           carry=[g_h])
        scatter_handles.append((units, handle))
    grads["mem_norm"] = grads["mem_norm"] + vjp_memory(g_mem_n)[0]["mem_norm"]
    g_first, g_x_norm = vjp_first_norm(g_h)
    grads["norm_mix"] = grads["norm_mix"] + g_first["norm_mix"]
    g_x = g_x + g_x_norm

    small_names = [n for n, _ in TINY] + REPL
    small_g = pack_rows([loss_vec[0, :1]] + [grads[n].astype(F32).reshape(-1) for n in small_names], PACK_W, 8)
    small_handle, _, _ = exchange_start("gather", [small_g], "gather_start_small_grads")

    g_recv = {}
    for units, handle in scatter_handles:
        landed = exchange_wait(handle, g_x, f"scatter_wait_{units[0][0]}_{units[0][1]}")
        g_recv.update(dict(zip(units, landed)))

    res = {}
    for n in BIG:
        outs = None
        for layer in range(w_loc[n].shape[0]):
            outs = adamw(g_recv[n, layer], w_loc[n], m_loc[n], v_loc[n], layer, outs, f"adamw_{n}_{layer}")
        for kind, a in zip(("grad", "delta", "m", "v"), outs):
            res[(kind, n)] = a
    small_recv = exchange_wait(small_handle, res[("grad", BIG[-1])], "gather_wait_small_grads")[0]

    def full_small(d):
        parts = [jnp.zeros((1,), F32)]
        for n, ax in TINY:
            full_shape = params[n].shape
            start = [0] * len(full_shape)
            start[ax] = me * d[n].shape[ax]
            parts.append(lax.dynamic_update_slice(jnp.zeros(full_shape, F32), d[n], start).reshape(-1))
        parts += [d[n].reshape(-1) for n in REPL]
        return pack_rows(parts, PACK_W, 8)

    outs_small = adamw(small_recv, full_small(w_loc)[None], full_small(m_loc)[None], full_small(v_loc)[None],
                       0, None, "adamw_small")
    small_shapes = [(1,)] + [params[n].shape for n, _ in TINY] + [w_loc[n].shape for n in REPL]
    loss = None
    for kind, packed in zip(("grad", "delta", "m", "v"), outs_small):
        parts = unpack_rows(packed[0], small_shapes)
        if kind == "grad":
            loss = parts[0][0]
        for (n, ax), a in zip(TINY, parts[1:1 + len(TINY)]):
            start = [0] * a.ndim
            start[ax] = me * w_loc[n].shape[ax]
            res[(kind, n)] = lax.dynamic_slice(a, start, w_loc[n].shape)
        for n, a in zip(REPL, parts[1 + len(TINY):]):
            res[(kind, n)] = a

    out = [loss, g_x[None]]
    for kind in ("grad", "delta", "m", "v"):
        out += [res[(kind, n)] for n in WEIGHTS]
    return tuple(out)
```

```python
import math

import jax
import jax.numpy as jnp
from jax import lax
from jax.experimental import pallas as pl
from jax.experimental.pallas import tpu as pltpu

F32 = jnp.float32
BF16 = jnp.bfloat16

N_DEV = 8
LANES = 128
EPS = 1e-6
ROPE_THETA = 10000.0
MLA_HEADS, MLA_NOPE, MLA_ROPE, MLA_V = 8, 128, 64, 128
MLA_Q_RANK, MLA_KV_RANK = 384, 256
GDN_HEADS, GDN_DK, GDN_CONV, GDN_CHUNK = 8, 128, 4, 64
X_HEADS, X_HEAD_DIM = 4, 256
DEPTH, N_MIXERS = 4, 3
ADAM_LR, ADAM_B1, ADAM_B2, ADAM_EPS, ADAM_WD, ADAM_STEP = 0.001, 0.9, 0.999, 1e-08, 0.01, 10
MLA_QUERY_GROUPS = 4
NEG_BIG = -1e30
PACK_W = 1024


_NN = (((1,), (0,)), ((), ()))
_NT = (((1,), (1,)), ((), ()))
_TN = (((0,), (0,)), ((), ()))
_NN3 = (((2,), (1,)), ((0,), (0,)))
_NT3 = (((2,), (2,)), ((0,), (0,)))
_TN3 = (((1,), (1,)), ((0,), (0,)))


def _dot(a, b, dims):
    return lax.dot_general(a, b, dims, preferred_element_type=F32)


def _hi_lo(x):
    hi = x.astype(BF16)
    return hi, (x - hi.astype(F32)).astype(BF16)


def _split3(x):
    hi = x.astype(BF16)
    r = x - hi.astype(F32)
    mid = r.astype(BF16)
    return hi, mid, (r - mid.astype(F32)).astype(BF16)


def _dg(a, b, dims, prec):
    if prec == "h":
        return lax.dot_general(a, b, dims, precision=lax.Precision.HIGHEST, preferred_element_type=F32)
    if prec == "m":
        a_hi, a_lo = _hi_lo(a)
        b_hi, b_lo = _hi_lo(b)
        return _dot(a_hi, b_hi, dims) + _dot(a_hi, b_lo, dims) + _dot(a_lo, b_hi, dims)
    return _dot(a.astype(BF16), b.astype(BF16), dims)


def _dg_sel(sel, x, dims, sel_first):
    s16 = sel.astype(BF16)
    parts = [(_dot(s16, piece, dims) if sel_first else _dot(piece, s16, dims)) for piece in _split3(x)]
    return parts[0] + parts[1] + parts[2]


class _Ops:
    def __init__(self, prec, differentiable, batched=False):
        d_nn, d_nt, d_tn = (_NN3, _NT3, _TN3) if batched else (_NN, _NT, _TN)

        def nn(a, b):
            return _dg(a, b, d_nn, prec)

        def nt(a, b):
            return _dg(a, b, d_nt, prec)

        def tn(a, b):
            return _dg(a, b, d_tn, prec)

        if differentiable:
            dnn = jax.custom_vjp(nn)
            dnn.defvjp(lambda a, b: (nn(a, b), (a, b)), lambda r, g: (nt(g, r[1]), tn(r[0], g)))
            dnt = jax.custom_vjp(nt)
            dnt.defvjp(lambda a, b: (nt(a, b), (a, b)), lambda r, g: (nn(g, r[1]), tn(g, r[0])))
            dtn = jax.custom_vjp(tn)
            dtn.defvjp(lambda a, b: (tn(a, b), (a, b)), lambda r, g: (nt(r[1], g), nn(r[0], g)))
            nn, nt, tn = dnn, dnt, dtn
        self.nn, self.nt, self.tn = nn, nt, tn


class _SelOps:
    def __init__(self, differentiable, batched=False):
        d_nn, d_nt, d_tn = (_NN3, _NT3, _TN3) if batched else (_NN, _NT, _TN)

        def sel_nn(sel, x):
            return _dg_sel(sel, x, d_nn, True)

        def sel_nt(sel, x):
            return _dg_sel(sel, x, d_nt, True)

        if differentiable:
            dnn = jax.custom_vjp(sel_nn)
            dnn.defvjp(lambda s, x: (sel_nn(s, x), s),
                       lambda s, g: (jnp.zeros_like(s), _dg_sel(s, g, d_tn, True)))
            dnt = jax.custom_vjp(sel_nt)
            dnt.defvjp(lambda s, x: (sel_nt(s, x), s),
                       lambda s, g: (jnp.zeros_like(s), _dg_sel(s, g, d_tn, False)))
            sel_nn, sel_nt = dnn, dnt
        self.sel_nn, self.sel_nt = sel_nn, sel_nt


class _OpSet:
    def __init__(self, differentiable):
        self.b = _Ops("b", differentiable)
        self.h = _Ops("h", differentiable)
        self.bb = _Ops("b", differentiable, batched=True)
        self.bm = _Ops("m", differentiable, batched=True)
        self.bs = _SelOps(differentiable, batched=True)


_PLAIN = _OpSet(False)
_DIFF = _OpSet(True)


def _params(sem):
    return pltpu.CompilerParams(dimension_semantics=sem)


BLOCK_BYTES = 4 * 1024 * 1024


def _pick(n, cands):
    for c in cands:
        if n % c == 0:
            return c
    return n


def _tile(n, cap):
    if n <= cap:
        return n
    return _pick(n, tuple(c for c in (2048, 1024, 768, 512, 384, 256, 128) if c <= cap))


def matmul(a, b, form, out_dtype, name, res=None, blocked=False, relu_gate=None, rms_gain=None, relu2_out=False):
    if form == "nn":
        m, k = a.shape
        k2, n = (b.shape[1], N_DEV * b.shape[2]) if blocked else b.shape
    elif form == "nt":
        m, k = a.shape
        n, k2 = (b.shape[1], N_DEV * b.shape[2]) if blocked else b.shape
    else:
        (k, m), (k2, n) = a.shape, b.shape
    assert k == k2, (a.shape, b.shape, form)
    tk = k if k <= 2048 else _tile(k, 1024)
    cb = nb = 1
    if blocked:
        cb = (k if form == "nt" else n) // N_DEV
        nb = _pick(N_DEV, tuple(c for c in (8, 4, 2, 1) if c * cb <= 1024))
    if blocked and form == "nt":
        tk = nb * cb
    if blocked and form != "nt":
        tn = nb * cb
    else:
        tn = _tile(n, min(1024, BLOCK_BYTES // (tk * b.dtype.itemsize)))
    out_elems = BLOCK_BYTES // 2 if (out_dtype == BF16 and res is None) else BLOCK_BYTES // 4
    tm = _tile(m, min(BLOCK_BYTES // (tk * a.dtype.itemsize), out_elems // tn))
    nk = k // tk
    dims = {"nn": _NN, "nt": _NT, "tn": _TN}[form]

    a_spec = {"nn": pl.BlockSpec((tm, tk), lambda i, j, kk: (i, kk)),
              "nt": pl.BlockSpec((tm, tk), lambda i, j, kk: (i, kk)),
              "tn": pl.BlockSpec((tk, tm), lambda i, j, kk: (kk, i))}[form]
    if blocked and form == "nn":
        b_spec = pl.BlockSpec((nb, tk, cb), lambda i, j, kk: (j, kk, 0))
    elif blocked and form == "nt":
        b_spec = pl.BlockSpec((nb, tn, cb), lambda i, j, kk: (kk, j, 0))
    else:
        b_spec = {"nn": pl.BlockSpec((tk, tn), lambda i, j, kk: (kk, j)),
                  "nt": pl.BlockSpec((tn, tk), lambda i, j, kk: (j, kk)),
                  "tn": pl.BlockSpec((tk, tn), lambda i, j, kk: (kk, j))}[form]
    c_spec = pl.BlockSpec((tm, tn), lambda i, j, kk: (i, j))
    out_shape = jax.ShapeDtypeStruct((m, n), out_dtype)
    o_spec = c_spec
    blocked_out = blocked and form == "tn"
    if blocked_out:
        out_shape = jax.ShapeDtypeStruct((N_DEV, m, cb), out_dtype)
        o_spec = pl.BlockSpec((nb, tm, cb), lambda i, j, kk: (j, i, 0))
    has_res, has_gate, has_gain = res is not None, relu_gate is not None, rms_gain is not None
    extras = [e for e in (res, relu_gate) if e is not None]
    n_in = 2 + len(extras) + has_gain
    second = has_gain or relu2_out
    assert not (second and (blocked_out or tn != n and has_gain))

    def body(*refs):
        a_ref, b_ref = refs[0], refs[1]
        r_ref = refs[2] if has_res else None
        gate_ref = refs[2 + has_res] if has_gate else None
        gain_ref = refs[n_in - 1] if has_gain else None
        o_ref = refs[n_in]
        a_val = a_ref[...].astype(BF16)
        if blocked and form == "nn":
            part = jnp.concatenate([_dot(a_val, b_ref[t].astype(BF16), dims) for t in range(nb)], axis=-1)
        elif blocked and form == "nt":
            part = _dot(a_val[:, :cb], b_ref[0].astype(BF16), dims)
            for t in range(1, nb):
                part = part + _dot(a_val[:, t * cb:(t + 1) * cb], b_ref[t].astype(BF16), dims)
        else:
            part = _dot(a_val, b_ref[...].astype(BF16), dims)

        def finish(acc):
            if has_res:
                acc = acc + r_ref[...].astype(F32)
            if has_gate:
                acc = acc * (2.0 * jnp.maximum(gate_ref[...].astype(F32), 0.0))
            if blocked_out:
                for t in range(nb):
                    o_ref[t] = acc[:, t * cb:(t + 1) * cb].astype(out_dtype)
            else:
                o_ref[...] = acc.astype(out_dtype)
            if has_gain:
                refs[n_in + 1][...] = _rms(acc, gain_ref[...]).astype(BF16)
            if relu2_out:
                r = jnp.maximum(acc.astype(out_dtype).astype(F32), 0.0)
                refs[n_in + 1][...] = (r * r).astype(BF16)

        if nk == 1:
            finish(part)
        else:
            acc_ref = refs[-1]
            kk = pl.program_id(2)

            @pl.when(kk == 0)
            def _():
                acc_ref[...] = part

            @pl.when(jnp.logical_and(kk > 0, kk < nk - 1))
            def _():
                acc_ref[...] += part

            @pl.when(kk == nk - 1)
            def _():
                finish(acc_ref[...] + part)

    in_specs = [a_spec, b_spec] + [c_spec] * len(extras)
    args = [a, b] + extras
    if has_gain:
        in_specs.append(pl.BlockSpec((1, tn), lambda i, j, kk: (0, j)))
        args.append(rms_gain)
    if second:
        out_shape = [out_shape, jax.ShapeDtypeStruct((m, n), BF16)]
        o_spec = [o_spec, c_spec]
    return pl.pallas_call(
        body, name=name,
        out_shape=out_shape,
        grid=(m // tm, n // tn, nk),
        in_specs=in_specs, out_specs=o_spec,
        scratch_shapes=[pltpu.VMEM((tm, tn), F32)] if nk > 1 else [],
        compiler_params=_params(("parallel", "parallel", "arbitrary")),
    )(*args)


def make_mm(name, out_dtype, with_res=False, blocked=False):
    def bwd_mm(a, w, g):
        da = matmul(g, w, "nt", a.dtype, name + "_da", blocked=blocked)
        dw = matmul(a, g, "tn", w.dtype, name + "_dw", blocked=blocked)
        return da, dw

    if with_res:
        @jax.custom_vjp
        def op(res, a, w):
            return matmul(a, w, "nn", out_dtype, name + "_f", res=res, blocked=blocked)

        def fwd(res, a, w):
            return op(res, a, w), (a, w)

        def bwd(saved, g):
            return (g,) + bwd_mm(*saved, g)
    else:
        @jax.custom_vjp
        def op(a, w):
            return matmul(a, w, "nn", out_dtype, name + "_f", blocked=blocked)

        def fwd(a, w):
            return op(a, w), (a, w)

        def bwd(saved, g):
            return bwd_mm(*saved, g)
    op.defvjp(fwd, bwd)
    return op


def _rms_fan_bwd(x_new, gain, dx, dh, name):
    rows, d = x_new.shape
    outs = [("row", d, F32), ("row", d, BF16)]
    return tile_bwd(fn_fan_rms, name, ["row", "par"], [x_new, gain], [True, True], outs, [dx, dh],
                    rows, min(512, rows), 0, bf16_copy_of=0)


def make_mm_res_rms(name):
    @jax.custom_vjp
    def op(res, a, w, gain):
        return tuple(matmul(a, w, "nn", F32, name + "_f", res=res, rms_gain=gain))

    def fwd(res, a, w, gain):
        x_new, h = op(res, a, w, gain)
        return (x_new, h), (a, w, x_new, gain)

    def bwd(saved, cts):
        a, w, x_new, gain = saved
        dx, dgain, dx16 = _rms_fan_bwd(x_new, gain, cts[0], cts[1], name + "_nb")
        da = matmul(dx16, w, "nt", a.dtype, name + "_da")
        dw = matmul(a, dx16, "tn", w.dtype, name + "_dw")
        return dx, da, dw, dgain

    op.defvjp(fwd, bwd)
    return op


def make_mlp(name, with_norm):
    def run(x, h, w1, w2, gain):
        a, bsq = matmul(h, w1, "nn", BF16, name + "_1_f", blocked=True, relu2_out=True)
        out = matmul(bsq, w2, "nn", F32, name + "_2_f", res=x, rms_gain=gain if with_norm else None)
        return (tuple(out) if with_norm else (out,)), a, bsq

    @jax.custom_vjp
    def op(x, h, w1, w2, gain):
        return run(x, h, w1, w2, gain)[0]

    def fwd(x, h, w1, w2, gain):
        out, a, bsq = run(x, h, w1, w2, gain)
        return out, (h, w1, w2, gain, a, bsq, out[0])

    def bwd(saved, cts):
        h, w1, w2, gain, a, bsq, x_new = saved
        if with_norm:
            dx, dgain, dx16 = _rms_fan_bwd(x_new, gain, cts[0], cts[1], name + "_nb")
        else:
            dx, dgain, dx16 = cts[0], None, cts[0]
        da = matmul(dx16, w2, "nt", BF16, name + "_2_da", relu_gate=a)
        dw2 = matmul(bsq, dx16, "tn", w2.dtype, name + "_2_dw")
        dh = matmul(da, w1, "nt", h.dtype, name + "_1_da", blocked=True)
        dw1 = matmul(h, da, "tn", w1.dtype, name + "_1_dw", blocked=True)
        return dx, dh, dw1, dw2, dgain

    op.defvjp(fwd, bwd)
    return op


def _kind(k):
    if isinstance(k, str):
        return k, None, 1
    return k[0], k[1], (k[2] if len(k) > 2 else 1)


def _tile_spec(kind, shape, tm, heads):
    k, d, ns = _kind(kind)
    if k == "row":
        return pl.BlockSpec((tm, shape[1]), (lambda h, i: (i, 0)) if heads else (lambda i: (i, 0)))
    if k == "par":
        return pl.BlockSpec(tuple(shape), (lambda h, i: (0, 0)) if heads else (lambda i: (0, 0)))
    if k == "rowh":
        return pl.BlockSpec((tm, d * ns), lambda h, i: (i, h))
    if k == "parh":
        return pl.BlockSpec((shape[0], d * ns), lambda h, i: (0, h))
    raise ValueError(kind)


def _tile_grid(rows, tm, heads):
    n_rows = rows // tm
    return ((heads, n_rows) if heads else (n_rows,)), (1 if heads else 0)


def _split_vals(kinds, refs):
    vals, counts = [], []
    for kind, r in zip(kinds, refs):
        _, d, ns = _kind(kind)
        v = r[...].astype(F32)
        vals += [v] if ns == 1 else [v[:, p * d:(p + 1) * d] for p in range(ns)]
        counts.append(ns)
    return vals, counts


def tile_fwd(fn, name, kinds, args, outs, rows, tm, heads, row_base=0):
    grid, row_axis = _tile_grid(rows, tm, heads)
    n_in = len(args)
    out_shapes = [jax.ShapeDtypeStruct((rows, w), dt) for (_, w, dt) in outs]

    def body(*refs):
        vals, _ = _split_vals(kinds, refs[:n_in])
        row0 = row_base + pl.program_id(row_axis) * tm
        res = list(fn(_PLAIN, row0, *vals))
        for o_ref, (k, _, _) in zip(refs[n_in:], outs):
            pieces = [res.pop(0) for _ in range(_kind(k)[2])]
            v = pieces[0] if len(pieces) == 1 else jnp.concatenate(pieces, axis=-1)
            o_ref[...] = v.astype(o_ref.dtype)

    return pl.pallas_call(
        body, name=name, out_shape=out_shapes, grid=grid,
        in_specs=[_tile_spec(k, a.shape, tm, heads) for k, a in zip(kinds, args)],
        out_specs=[_tile_spec(k, (rows, w), tm, heads) for (k, w, _) in outs],
        compiler_params=_params(("arbitrary",) * len(grid)),
    )(*args)


def tile_bwd(fn, name, kinds, args, diff, outs, cts, rows, tm, heads, row_base=0, bf16_copy_of=None):
    grid, row_axis = _tile_grid(rows, tm, heads)
    n_in, n_ct = len(args), len(cts)
    diff_idx = [i for i, d in enumerate(diff) if d]
    g_shapes, g_specs = [], []
    for i in diff_idx:
        k = _kind(kinds[i])[0]
        dt = args[i].dtype if k in ("row", "rowh") else F32
        g_shapes.append(jax.ShapeDtypeStruct(args[i].shape, dt))
        g_specs.append(_tile_spec(kinds[i], args[i].shape, tm, heads))
    if bf16_copy_of is not None:
        g_shapes.append(jax.ShapeDtypeStruct(args[bf16_copy_of].shape, BF16))
        g_specs.append(_tile_spec(kinds[bf16_copy_of], args[bf16_copy_of].shape, tm, heads))

    def body(*refs):
        in_refs, ct_refs, g_refs = refs[:n_in], refs[n_in:n_in + n_ct], refs[n_in + n_ct:]
        vals, counts = _split_vals(kinds, in_refs)
        first_piece = [sum(counts[:i]) for i in range(n_in)]
        flat_diff = [first_piece[i] + p for i in diff_idx for p in range(counts[i])]
        row_id = pl.program_id(row_axis)
        row0 = row_base + row_id * tm

        def f(*dvals):
            full = list(vals)
            for i, dv in zip(flat_diff, dvals):
                full[i] = dv
            return tuple(fn(_DIFF, row0, *full))

        _, vjp = jax.vjp(f, *[vals[i] for i in flat_diff])
        ct_vals, _ = _split_vals([k for (k, _, _) in outs], ct_refs)
        flat_grads = list(vjp(tuple(ct_vals)))
        for g_ref, i in zip(g_refs, diff_idx):
            pieces = [flat_grads.pop(0) for _ in range(counts[i])]
            g = pieces[0] if len(pieces) == 1 else jnp.concatenate(pieces, axis=-1)
            k = _kind(kinds[i])[0]
            if k in ("row", "rowh"):
                g_ref[...] = g.astype(g_ref.dtype)
                if i == bf16_copy_of:
                    g_refs[-1][...] = g.astype(BF16)
            else:
                first = row_id == 0
                if heads and k == "par":
                    first = jnp.logical_and(first, pl.program_id(0) == 0)

                @pl.when(first)
                def _(g_ref=g_ref, g=g):
                    g_ref[...] = g

                @pl.when(jnp.logical_not(first))
                def _(g_ref=g_ref, g=g):
                    g_ref[...] += g

    return pl.pallas_call(
        body, name=name, out_shape=g_shapes, grid=grid,
        in_specs=[_tile_spec(k, a.shape, tm, heads) for k, a in zip(kinds, args)]
        + [_tile_spec(k, (rows, w), tm, heads) for (k, w, _) in outs],
        out_specs=g_specs,
        compiler_params=_params(("arbitrary",) * len(grid)),
    )(*args, *cts)


def make_tile_op(fn, name, kinds, diff, outs, rows, tm, heads=0, row_base=0):
    tm = min(tm, rows)

    @jax.custom_vjp
    def op(*args):
        return tuple(tile_fwd(fn, name + "_f", kinds, args, outs, rows, tm, heads, row_base))

    def fwd(*args):
        return op(*args), args

    def bwd(args, cts):
        grads = tile_bwd(fn, name + "_b", kinds, args, diff, outs, cts, rows, tm, heads, row_base)
        it = iter(grads)
        res = []
        for a, d in zip(args, diff):
            res.append(next(it).astype(a.dtype) if d else None)
        return tuple(res)

    op.defvjp(fwd, bwd)
    return op


def _rms(x, g):
    return x * lax.rsqrt(jnp.mean(x * x, axis=-1, keepdims=True) + EPS) * g


def fn_rms(ops, row0, x, g):
    return (_rms(x, g),)


def fn_fan_rms(ops, row0, x, g):
    return x, _rms(x, g)


def fn_mul(ops, row0, a, b):
    return (a * b,)


def fn_mla_down(ops, row0, h, c, s, w_cq, w_ckv, w_kr, w_krs, g_q, g_kv):
    c_qn = _rms(ops.b.nn(h, w_cq), g_q)
    c_kvn = _rms(ops.b.nn(h, w_ckv), g_kv)
    return c_qn, c_kvn, ops.b.nn(h, w_kr) * c + ops.b.nn(h, w_krs) * s


def fn_mla_up(ops, row0, c_qn, c_kvn, c, s, w_qn, w_qr, w_qrs, w_kv):
    c_all = jnp.concatenate([c] * MLA_HEADS, axis=-1)
    s_all = jnp.concatenate([s] * MLA_HEADS, axis=-1)
    q_rope = ops.b.nn(c_qn, w_qr) * c_all + ops.b.nn(c_qn, w_qrs) * s_all
    return ops.b.nn(c_qn, w_qn), q_rope, ops.b.nn(c_kvn, w_kv)


def _softmax(s):
    m = lax.stop_gradient(jnp.max(s, axis=-1, keepdims=True))
    e = jnp.exp(s - m)
    return e / jnp.sum(e, axis=-1, keepdims=True)


def fn_xattn(ops, row0, hx, *t):
    w_q, k, v = t[:X_HEADS], t[X_HEADS:2 * X_HEADS], t[2 * X_HEADS:]
    outs = []
    for w_h, k_h, v_h in zip(w_q, k, v):
        s = ops.b.nt(ops.b.nn(hx, w_h), k_h) * (X_HEAD_DIM ** -0.5)
        outs.append(ops.b.nn(_softmax(s), v_h))
    return tuple(outs)


def _silu(x):
    return x * jax.nn.sigmoid(x)


def fn_gdn_prep(ops, row0, *t):
    nh = len(t) // 3
    qs, ks, vs = [], [], []
    for qc, kc, vc in zip(t[:nh], t[nh:2 * nh], t[2 * nh:]):
        q, k = _silu(qc), _silu(kc)
        qs.append(q * lax.rsqrt(jnp.sum(q * q, -1, keepdims=True) + EPS) * (GDN_DK ** -0.5))
        ks.append(k * lax.rsqrt(jnp.sum(k * k, -1, keepdims=True) + EPS))
        vs.append(_silu(vc))
    return tuple(qs + ks + vs)


def fn_gdn_gates(ops, row0, ba, alog, dtb):
    width = GDN_HEADS * GDN_DK
    beta = jax.nn.sigmoid(ba)
    z = ba + dtb
    softplus = jnp.maximum(z, 0.0) + jnp.log1p(jnp.exp(-jnp.abs(z)))
    g = -jnp.exp(alog) * softplus
    r = lax.broadcasted_iota(jnp.int32, (LANES, width), 0)
    c = lax.broadcasted_iota(jnp.int32, (LANES, width), 1) // GDN_DK
    e_beta = (r == c).astype(F32)
    e_g = (r == c + GDN_HEADS).astype(F32)
    return ops.h.nn(beta, e_beta), ops.h.nn(g, e_g)


def fn_gdn_out(ops, row0, *t):
    nh = (len(t) - 1) // 2
    g = t[-1]
    return tuple(_rms(o, g) * _silu(gate) for o, gate in zip(t[:nh], t[nh:2 * nh]))


def fn_mla_attn(ops, row0, qn, qr, kn, v, kr):
    s = (ops.b.nt(qn, kn) + ops.b.nt(qr, kr)) * ((MLA_NOPE + MLA_ROPE) ** -0.5)
    rows = row0 + lax.broadcasted_iota(jnp.int32, s.shape, 0)
    cols = lax.broadcasted_iota(jnp.int32, s.shape, 1)
    s = jnp.where(rows >= cols, s, NEG_BIG)
    return (ops.b.nn(_softmax(s), v),)


def _shift_down(x, d, t_idx):
    if d == 0:
        return x
    return jnp.where(t_idx >= d, pltpu.roll(x, d, axis=0), 0.0)


def _shift_up(x, d, t_idx):
    if d == 0:
        return x
    n = x.shape[0]
    return jnp.where(t_idx < n - d, pltpu.roll(x, n - d, axis=0), 0.0)


def conv_fwd(x, w, name):
    s, c = x.shape
    kw = w.shape[0]
    tc = _pick(c, (256, 128))

    def body(x_ref, w_ref, y_ref):
        xv = x_ref[...]
        t_idx = lax.broadcasted_iota(jnp.int32, xv.shape, 0)
        acc = jnp.zeros_like(xv)
        for j in range(kw):
            acc = acc + w_ref[j:j + 1, :] * _shift_down(xv, kw - 1 - j, t_idx)
        y_ref[...] = acc

    return pl.pallas_call(
        body, name=name, out_shape=jax.ShapeDtypeStruct((s, c), F32), grid=(c // tc,),
        in_specs=[pl.BlockSpec((s, tc), lambda i: (0, i)), pl.BlockSpec((kw, tc), lambda i: (0, i))],
        out_specs=pl.BlockSpec((s, tc), lambda i: (0, i)),
        compiler_params=_params(("parallel",)),
    )(x, w)


def conv_bwd(x, w, dy, name):
    s, c = x.shape
    kw = w.shape[0]
    tc = _pick(c, (256, 128))

    def body(x_ref, w_ref, dy_ref, dx_ref, dw_ref):
        xv, dyv = x_ref[...], dy_ref[...]
        t_idx = lax.broadcasted_iota(jnp.int32, xv.shape, 0)
        dx = jnp.zeros_like(xv)
        for j in range(kw):
            d = kw - 1 - j
            dx = dx + w_ref[j:j + 1, :] * _shift_up(dyv, d, t_idx)
            dw_ref[j:j + 1, :] = jnp.sum(dyv * _shift_down(xv, d, t_idx), axis=0, keepdims=True)
        dx_ref[...] = dx

    return pl.pallas_call(
        body, name=name,
        out_shape=[jax.ShapeDtypeStruct((s, c), F32), jax.ShapeDtypeStruct((kw, c), F32)],
        grid=(c // tc,),
        in_specs=[pl.BlockSpec((s, tc), lambda i: (0, i)), pl.BlockSpec((kw, tc), lambda i: (0, i)),
                  pl.BlockSpec((s, tc), lambda i: (0, i))],
        out_specs=[pl.BlockSpec((s, tc), lambda i: (0, i)), pl.BlockSpec((kw, tc), lambda i: (0, i))],
        compiler_params=_params(("parallel",)),
    )(x, w, dy)


def make_conv(name):
    @jax.custom_vjp
    def op(x, w):
        return conv_fwd(x, w, name + "_f")

    def fwd(x, w):
        return op(x, w), (x, w)

    def bwd(saved, dy):
        dx, dw = conv_bwd(saved[0], saved[1], dy, name + "_b")
        return dx, dw

    op.defvjp(fwd, bwd)
    return op


def _gdn_consts():
    c, d = GDN_CHUNK, GDN_DK
    i = lax.broadcasted_iota(jnp.int32, (c, c), 0)
    j = lax.broadcasted_iota(jnp.int32, (c, c), 1)
    tri = i >= j
    return dict(
        tri=tri, strict=i > j,
        tri_f=tri.astype(F32),
        eye=(i == j).astype(F32),
        lane0=(lax.broadcasted_iota(jnp.int32, (c, d), 1) == 0).astype(F32),
        last_row=(lax.broadcasted_iota(jnp.int32, (c, d), 0) == c - 1).astype(F32),
    )


def _inverse_given(m_ops):
    @jax.custom_vjp
    def given(mm_, t):
        return t

    def bwd(t, dt):
        return -m_ops.nt(m_ops.tn(t, dt), t), jnp.zeros_like(t)

    given.defvjp(lambda mm_, t: (t, t), bwd)
    return given


def _gdn_chunk(ops, q, k, v, g, beta, state, t_saved=None):
    b, m, sel = ops.bb, ops.bm, ops.bs
    nh, c, d = q.shape[0], GDN_CHUNK, GDN_DK
    k_ = _gdn_consts()

    def per_head(a):
        return jnp.broadcast_to(a, (nh,) + a.shape)

    gc = sel.sel_nn(per_head(k_["tri_f"]), g)
    col = jnp.broadcast_to(jnp.sum(gc * k_["lane0"], axis=2, keepdims=True), (nh, c, c))
    row = sel.sel_nt(per_head(k_["lane0"]), gc)
    decay = jnp.where(k_["tri"], jnp.exp(jnp.where(k_["tri"], col - row, 0.0)), 0.0)
    kb = k * beta
    mm_ = jnp.where(k_["strict"], b.nt(kb, k) * decay, 0.0)
    if t_saved is None:
        p = -mm_
        t = k_["eye"] + p
        for _ in range(int(math.log2(GDN_CHUNK)) - 1):
            p = m.nn(p, p)
            t = t + m.nn(t, p)
    else:
        t = _inverse_given(_PLAIN.bm)(mm_, t_saved)
    egc = jnp.exp(gc)
    u = b.nn(t, v * beta)
    w = b.nn(t, kb * egc)
    attn = b.nt(q, k) * decay
    v_new = u - b.nn(w, state)
    o = b.nn(q * egc, state) + b.nn(attn, v_new)
    g_last = jnp.sum(gc * k_["last_row"], axis=1, keepdims=True)
    new_state = (state * jnp.exp(jnp.broadcast_to(g_last, (nh, d, d)))
                 + b.tn(k * jnp.exp(jnp.broadcast_to(g_last, (nh, c, d)) - gc), v_new))
    return o, new_state, t


GDN_HEAD_GROUP = 8
GDN_TILE_CHUNKS = 4


def _heads_of(ref, rows, n_heads):
    d = GDN_DK
    return jnp.stack([ref[rows, h * d:(h + 1) * d] for h in range(n_heads)])


def _gdn_specs(s, reverse):
    d, hg = GDN_DK, GDN_HEAD_GROUP
    tile = min(GDN_TILE_CHUNKS * GDN_CHUNK, s)
    n_tiles = s // tile
    t_of = (lambda t: n_tiles - 1 - t) if reverse else (lambda t: t)
    seq = pl.BlockSpec((tile, hg * d), lambda grp, t: (t_of(t), grp))
    st = pl.BlockSpec((hg, tile // GDN_CHUNK, d, d), lambda grp, t: (grp, t_of(t), 0, 0))
    inv = pl.BlockSpec((hg, tile // GDN_CHUNK, GDN_CHUNK, GDN_CHUNK), lambda grp, t: (grp, t_of(t), 0, 0))
    return seq, st, inv, tile, n_tiles


def gdn_fwd(q, k, v, g, beta, name):
    s = q.shape[0]
    d, hg = GDN_DK, GDN_HEAD_GROUP
    seq, st, inv, tile, n_tiles = _gdn_specs(s, False)

    def body(q_ref, k_ref, v_ref, g_ref, b_ref, o_ref, st_ref, inv_ref, state_scr):
        @pl.when(pl.program_id(1) == 0)
        def _():
            state_scr[...] = jnp.zeros_like(state_scr)

        def step(ci, carry):
            rows = pl.ds(pl.multiple_of(ci * GDN_CHUNK, GDN_CHUNK), GDN_CHUNK)
            state = state_scr[...]
            for h in range(hg):
                st_ref[h, ci] = state[h]
            o, new_state, t = _gdn_chunk(_PLAIN, *[_heads_of(r, rows, hg) for r in (q_ref, k_ref, v_ref, g_ref, b_ref)],
                                         state)
            for h in range(hg):
                o_ref[rows, h * d:(h + 1) * d] = o[h]
                inv_ref[h, ci] = t[h]
            state_scr[...] = new_state
            return carry

        lax.fori_loop(0, tile // GDN_CHUNK, step, 0)

    return pl.pallas_call(
        body, name=name,
        out_shape=[jax.ShapeDtypeStruct(q.shape, F32),
                   jax.ShapeDtypeStruct((GDN_HEADS, s // GDN_CHUNK, d, d), F32),
                   jax.ShapeDtypeStruct((GDN_HEADS, s // GDN_CHUNK, GDN_CHUNK, GDN_CHUNK), F32)],
        grid=(GDN_HEADS // hg, n_tiles), in_specs=[seq] * 5, out_specs=[seq, st, inv],
        scratch_shapes=[pltpu.VMEM((hg, d, d), F32)],
        compiler_params=_params(("parallel", "arbitrary")),
    )(q, k, v, g, beta)


def gdn_bwd(q, k, v, g, beta, states, inverses, do, name):
    s = q.shape[0]
    d, hg = GDN_DK, GDN_HEAD_GROUP
    seq, st, inv, tile, n_tiles = _gdn_specs(s, True)
    tile_chunks = tile // GDN_CHUNK

    def body(q_ref, k_ref, v_ref, g_ref, b_ref, st_ref, inv_ref, do_ref, dq_ref, dk_ref, dv_ref, dg_ref, db_ref,
             dstate_scr):
        @pl.when(pl.program_id(1) == 0)
        def _():
            dstate_scr[...] = jnp.zeros_like(dstate_scr)

        def step(it, carry):
            ci = tile_chunks - 1 - it
            rows = pl.ds(pl.multiple_of(ci * GDN_CHUNK, GDN_CHUNK), GDN_CHUNK)
            prim = [_heads_of(r, rows, hg) for r in (q_ref, k_ref, v_ref, g_ref, b_ref)]
            prim.append(jnp.stack([st_ref[h, ci] for h in range(hg)]))
            t_saved = jnp.stack([inv_ref[h, ci] for h in range(hg)])
            _, vjp = jax.vjp(lambda *a: _gdn_chunk(_DIFF, *a, t_saved=t_saved)[:2], *prim)
            grads = vjp((_heads_of(do_ref, rows, hg), dstate_scr[...]))
            for g_ref_out, gr in zip((dq_ref, dk_ref, dv_ref, dg_ref, db_ref), grads[:5]):
                for h in range(hg):
                    g_ref_out[rows, h * d:(h + 1) * d] = gr[h]
            dstate_scr[...] = grads[5]
            return carry

        lax.fori_loop(0, tile_chunks, step, 0)

    return pl.pallas_call(
        body, name=name,
        out_shape=[jax.ShapeDtypeStruct(q.shape, F32)] * 5,
        grid=(GDN_HEADS // hg, n_tiles), in_specs=[seq] * 5 + [st, inv, seq], out_specs=[seq] * 5,
        scratch_shapes=[pltpu.VMEM((hg, d, d), F32)],
        compiler_params=_params(("parallel", "arbitrary")),
    )(q, k, v, g, beta, states, inverses, do)


def make_gdn(name):
    @jax.custom_vjp
    def op(q, k, v, g, beta):
        return gdn_fwd(q, k, v, g, beta, name + "_f")[0]

    def fwd(q, k, v, g, beta):
        o, states, inverses = gdn_fwd(q, k, v, g, beta, name + "_f")
        return o, (q, k, v, g, beta, states, inverses)

    def bwd(saved, do):
        return tuple(gdn_bwd(*saved, do, name + "_b"))

    op.defvjp(fwd, bwd)
    return op


def loss_head(x, g, target, name):
    s, d = x.shape
    tm = min(256, s)

    def body(x_ref, g_ref, t_ref, loss_ref, dx_ref, dg_ref):
        tgt = t_ref[...]

        def f(xv, gv):
            err = _rms(xv, gv) - tgt
            per_row = jnp.mean(err * err, axis=-1, keepdims=True)
            return 0.5 * jnp.sum(per_row, axis=0, keepdims=True)

        val, vjp = jax.vjp(f, x_ref[...], g_ref[...])
        dx, dg = vjp(jnp.ones((1, 1), F32))
        dx_ref[...] = dx
        first = pl.program_id(0) == 0

        @pl.when(first)
        def _():
            dg_ref[...] = dg
            loss_ref[...] = jnp.broadcast_to(val, loss_ref.shape)

        @pl.when(jnp.logical_not(first))
        def _():
            dg_ref[...] += dg
            loss_ref[...] += jnp.broadcast_to(val, loss_ref.shape)

    row = pl.BlockSpec((tm, d), lambda i: (i, 0))
    vec = pl.BlockSpec((1, d), lambda i: (0, 0))
    return pl.pallas_call(
        body, name=name,
        out_shape=[jax.ShapeDtypeStruct((1, LANES), F32), jax.ShapeDtypeStruct((s, d), F32),
                   jax.ShapeDtypeStruct((1, d), F32)],
        grid=(s // tm,), in_specs=[row, vec, row],
        out_specs=[pl.BlockSpec((1, LANES), lambda i: (0, 0)), row, vec],
        compiler_params=_params(("arbitrary",)),
    )(x, g, target)


def adamw(g8, w, m, v, layer, prev, name):
    n_layers, rows, width = w.shape
    tr = _pick(rows, (256, 128, 64, 32, 16, 8))

    def body(g_ref, w_ref, m_ref, v_ref, *rest):
        go_ref, d_ref, mo_ref, vo_ref = rest[-4:]
        g = g_ref[0].astype(F32)
        for p in range(1, N_DEV):
            g = g + g_ref[p].astype(F32)
        m_new = ADAM_B1 * m_ref[...] + (1.0 - ADAM_B1) * g
        v_new = ADAM_B2 * v_ref[...] + (1.0 - ADAM_B2) * (g * g)
        m_hat = m_new / (1.0 - ADAM_B1 ** ADAM_STEP)
        v_hat = v_new / (1.0 - ADAM_B2 ** ADAM_STEP)
        go_ref[...] = g
        d_ref[...] = -ADAM_LR * (m_hat / (jnp.sqrt(v_hat) + ADAM_EPS) + ADAM_WD * w_ref[...])
        mo_ref[...] = m_new
        vo_ref[...] = v_new

    blk = pl.BlockSpec((None, tr, width), lambda i: (layer, i, 0))
    carried = list(prev) if prev is not None else []
    return pl.pallas_call(
        body, name=name, out_shape=[jax.ShapeDtypeStruct((n_layers, rows, width), F32)] * 4,
        grid=(rows // tr,),
        in_specs=[pl.BlockSpec((N_DEV, tr, width), lambda i: (0, i, 0)), blk, blk, blk]
        + [pl.BlockSpec(memory_space=pl.ANY)] * len(carried),
        out_specs=[blk] * 4,
        input_output_aliases={4 + j: j for j in range(len(carried))},
        compiler_params=_params(("parallel",)),
    )(g8, w, m, v, *carried)


_HBM = pl.BlockSpec(memory_space=pltpu.HBM)
_SEM = pl.BlockSpec(memory_space=pltpu.SEMAPHORE)
_EFFECT = pltpu.SideEffectType.DATAFLOW_SIDE_EFFECTING


def _exchange_copies(mode, src_refs, land_refs, send_sems, recv_sems, local_sems):
    x, y, c = lax.axis_index("x"), lax.axis_index("y"), lax.axis_index("c")
    me = 4 * x + 2 * y + c
    n = len(src_refs)

    def src(k, p):
        return src_refs[k] if mode == "gather" else src_refs[k].at[p]

    local = [pltpu.make_async_copy(src(k, me), land_refs[k].at[me], local_sems.at[k]) for k in range(n)]
    sends, recvs = [], []
    for k in range(n):
        for r in range(1, N_DEV):
            px = (1 - x) if r & 4 else x
            py = (1 - y) if r & 2 else y
            pc = (1 - c) if r & 1 else c
            p = 4 * px + 2 * py + pc
            sem = k * (N_DEV - 1) + r - 1
            sends.append(pltpu.make_async_remote_copy(
                src_ref=src(k, p), dst_ref=land_refs[k].at[me],
                send_sem=send_sems.at[sem], recv_sem=recv_sems.at[sem],
                device_id=(px, py, pc), device_id_type=pl.DeviceIdType.MESH))
            recvs.append(pltpu.make_async_remote_copy(
                src_ref=src(k, p), dst_ref=land_refs[k].at[p],
                send_sem=send_sems.at[sem], recv_sem=recv_sems.at[sem],
                device_id=(px, py, pc), device_id_type=pl.DeviceIdType.MESH))
    return local, sends, recvs


def exchange_start(mode, arrays, name, carry=()):
    n, nc = len(arrays), len(carry)
    land_shapes = [((N_DEV,) + tuple(a.shape)) if mode == "gather" else tuple(a.shape) for a in arrays]
    lands = [pltpu.with_memory_space_constraint(lax.empty(shp, a.dtype), pltpu.HBM)
             for shp, a in zip(land_shapes, arrays)]
    srcs = [pltpu.with_memory_space_constraint(a, pltpu.HBM) for a in arrays]
    carried = [pltpu.with_memory_space_constraint(a, pltpu.HBM) for a in carry]

    def body(*refs):
        src_refs, land_refs = refs[:n], refs[n:2 * n]
        first_out = 2 * n + nc
        send_sems, recv_sems, local_sems = refs[first_out:first_out + 3]
        token = refs[-1]
        local, sends, _ = _exchange_copies(mode, src_refs, land_refs, send_sems, recv_sems, local_sems)
        for cp in local + sends:
            cp.start()
        token[...] = jnp.zeros_like(token)

    n_sem = n * (N_DEV - 1)
    out = pl.pallas_call(
        body, name=name,
        out_shape=(pltpu.SemaphoreType.DMA((n_sem,)), pltpu.SemaphoreType.DMA((n_sem,)),
                   pltpu.SemaphoreType.DMA((n,)),
                   *[pltpu.HBM(a.shape, a.dtype) for a in arrays],
                   *[pltpu.HBM(shp, a.dtype) for shp, a in zip(land_shapes, arrays)],
                   *[pltpu.HBM(a.shape, a.dtype) for a in carry],
                   jax.ShapeDtypeStruct((8, LANES), F32)),
        in_specs=[_HBM] * (2 * n + nc),
        out_specs=(_SEM, _SEM, _SEM, *[_HBM] * (2 * n + nc), pl.BlockSpec(memory_space=pltpu.VMEM)),
        input_output_aliases={i: 3 + i for i in range(2 * n + nc)},
        compiler_params=pltpu.CompilerParams(has_side_effects=_EFFECT),
    )(*srcs, *lands, *carried)
    handle = dict(mode=mode, sems=out[:3], srcs=out[3:3 + n], lands=out[3 + n:3 + 2 * n])
    return handle, out[-1], list(out[3 + 2 * n:3 + 2 * n + nc])


def exchange_wait(handle, after, name):
    mode, srcs, lands = handle["mode"], list(handle["srcs"]), list(handle["lands"])
    n = len(srcs)

    def body(*refs):
        src_refs, land_refs = refs[:n], refs[n:2 * n]
        send_sems, recv_sems, local_sems = refs[2 * n:2 * n + 3]
        local, sends, recvs = _exchange_copies(mode, src_refs, land_refs, send_sems, recv_sems, local_sems)
        for cp in sends:
            cp.wait_send()
        for cp in recvs:
            cp.wait_recv()
        for cp in local:
            cp.wait()

    out = pl.pallas_call(
        body, name=name,
        out_shape=(*[pltpu.HBM(a.shape, a.dtype) for a in srcs], *[pltpu.HBM(a.shape, a.dtype) for a in lands]),
        in_specs=[_HBM] * (2 * n) + [_SEM] * 3 + [pl.BlockSpec(memory_space=pl.ANY)],
        out_specs=tuple([_HBM] * (2 * n)),
        input_output_aliases={i: i for i in range(2 * n)},
        compiler_params=pltpu.CompilerParams(has_side_effects=_EFFECT),
    )(*srcs, *lands, *handle["sems"], after)
    return list(out[n:])


_ICI_RELATIONS = (2, 4, 6)


def _mesh_place():
    x, y, c = lax.axis_index("x"), lax.axis_index("y"), lax.axis_index("c")

    def peer(r):
        px = (1 - x) if r & 4 else x
        py = (1 - y) if r & 2 else y
        pc = (1 - c) if r & 1 else c
        return (px, py, pc), 4 * px + 2 * py + pc

    return 4 * x + 2 * y + c, peer


def _remote(src, dst, send_sem, recv_sem, device):
    return pltpu.make_async_remote_copy(src_ref=src, dst_ref=dst, send_sem=send_sem, recv_sem=recv_sem,
                                        device_id=device, device_id_type=pl.DeviceIdType.MESH)


def gather2_start(groups, name):
    flat = [a for g in groups for a in g]
    n = len(flat)
    lands = [pltpu.with_memory_space_constraint(lax.empty((N_DEV,) + tuple(a.shape), a.dtype), pltpu.HBM) for a in flat]
    srcs = [pltpu.with_memory_space_constraint(a, pltpu.HBM) for a in flat]
    n_rel = 1 + len(_ICI_RELATIONS)

    def body(*refs):
        src_refs, land_refs = refs[:n], refs[n:2 * n]
        sem_refs = refs[2 * n:2 * n + 4 * len(groups)]
        me, peer = _mesh_place()
        k = 0
        for gi, g in enumerate(groups):
            send_sems, recv_sib, recv_ici, local_sems = sem_refs[4 * gi:4 * gi + 4]
            for j in range(len(g)):
                pltpu.make_async_copy(src_refs[k], land_refs[k].at[me], local_sems.at[j]).start()
                dev, _ = peer(1)
                _remote(src_refs[k], land_refs[k].at[me], send_sems.at[n_rel * j], recv_sib.at[j], dev).start()
                for t, r in enumerate(_ICI_RELATIONS):
                    dev, _ = peer(r)
                    _remote(src_refs[k], land_refs[k].at[me], send_sems.at[n_rel * j + 1 + t],
                            recv_ici.at[len(_ICI_RELATIONS) * j + t], dev).start()
                k += 1
        refs[-1][...] = jnp.zeros_like(refs[-1])

    sem_shapes = []
    for g in groups:
        sem_shapes += [pltpu.SemaphoreType.DMA((n_rel * len(g),)), pltpu.SemaphoreType.DMA((len(g),)),
                       pltpu.SemaphoreType.DMA((len(_ICI_RELATIONS) * len(g),)), pltpu.SemaphoreType.DMA((len(g),))]
    out = pl.pallas_call(
        body, name=name,
        out_shape=(*sem_shapes, *[pltpu.HBM(a.shape, a.dtype) for a in flat],
                   *[pltpu.HBM((N_DEV,) + tuple(a.shape), a.dtype) for a in flat],
                   jax.ShapeDtypeStruct((8, LANES), F32)),
        in_specs=[_HBM] * (2 * n),
        out_specs=(*[_SEM] * len(sem_shapes), *[_HBM] * (2 * n), pl.BlockSpec(memory_space=pltpu.VMEM)),
        input_output_aliases={i: len(sem_shapes) + i for i in range(2 * n)},
        compiler_params=pltpu.CompilerParams(has_side_effects=_EFFECT),
    )(*srcs, *lands)
    handles, k, base = [], 0, len(sem_shapes)
    for gi, g in enumerate(groups):
        handles.append(dict(sems=out[4 * gi:4 * gi + 4], srcs=out[base + k:base + k + len(g)],
                            lands=out[base + n + k:base + n + k + len(g)]))
        k += len(g)
    return handles, out[-1]


def gather2_forward(handle, after, name, carry=()):
    lands, nc = list(handle["lands"]), len(carry)
    n, n_ici = len(lands), len(_ICI_RELATIONS)
    carried = [pltpu.with_memory_space_constraint(a, pltpu.HBM) for a in carry]

    def body(*refs):
        land_refs = refs[:n]
        recv_ici = refs[n + nc]
        fwd_send, fwd_recv = refs[n + nc + 2], refs[n + nc + 3]
        me, peer = _mesh_place()
        sibling, _ = peer(1)
        for j in range(n):
            for t, r in enumerate(_ICI_RELATIONS):
                dev, p = peer(r)
                landed = land_refs[j].at[p]
                _remote(landed, landed, fwd_send.at[n_ici * j + t], recv_ici.at[n_ici * j + t], dev).wait_recv()
                _remote(landed, landed, fwd_send.at[n_ici * j + t], fwd_recv.at[n_ici * j + t], sibling).start()

    out = pl.pallas_call(
        body, name=name,
        out_shape=(pltpu.SemaphoreType.DMA((n_ici * n,)), pltpu.SemaphoreType.DMA((n_ici * n,)),
                   *[pltpu.HBM(a.shape, a.dtype) for a in lands], *[pltpu.HBM(a.shape, a.dtype) for a in carry]),
        in_specs=[_HBM] * (n + nc) + [_SEM, pl.BlockSpec(memory_space=pl.ANY)],
        out_specs=(_SEM, _SEM, *[_HBM] * (n + nc)),
        input_output_aliases={i: 2 + i for i in range(n + nc)},
        compiler_params=pltpu.CompilerParams(has_side_effects=_EFFECT),
    )(*lands, *carried, handle["sems"][2], after)
    new_handle = dict(sems=handle["sems"], srcs=handle["srcs"], lands=out[2:2 + n], fwd=out[:2])
    return new_handle, list(out[2 + n:])


def gather2_wait(handle, after, name):
    srcs, lands = list(handle["srcs"]), list(handle["lands"])
    n, n_ici = len(srcs), len(_ICI_RELATIONS)
    n_rel = 1 + n_ici
    send_all, recv_sibling, _, local_all = handle["sems"]

    def body(*refs):
        src_refs, land_refs = refs[:n], refs[n:2 * n]
        send_sems, recv_sib, local_sems, fwd_send, fwd_recv = refs[2 * n:2 * n + 5]
        me, peer = _mesh_place()
        sibling, sib = peer(1)
        for j in range(n):
            pltpu.make_async_copy(src_refs[j], land_refs[j].at[me], local_sems.at[j]).wait()
            _remote(src_refs[j], land_refs[j].at[sib], send_sems.at[n_rel * j], recv_sib.at[j], sibling).wait()
            for t, r in enumerate(_ICI_RELATIONS):
                dev, p = peer(r)
                _remote(src_refs[j], land_refs[j].at[me], send_sems.at[n_rel * j + 1 + t],
                        recv_sib.at[j], dev).wait_send()
                _, p_sib = peer(r ^ 1)
                _remote(land_refs[j].at[p], land_refs[j].at[p_sib], fwd_send.at[n_ici * j + t],
                        fwd_recv.at[n_ici * j + t], sibling).wait()

    out = pl.pallas_call(
        body, name=name,
        out_shape=(*[pltpu.HBM(a.shape, a.dtype) for a in srcs], *[pltpu.HBM(a.shape, a.dtype) for a in lands]),
        in_specs=[_HBM] * (2 * n) + [_SEM] * 5 + [pl.BlockSpec(memory_space=pl.ANY)],
        out_specs=tuple([_HBM] * (2 * n)),
        input_output_aliases={i: i for i in range(2 * n)},
        compiler_params=pltpu.CompilerParams(has_side_effects=_EFFECT),
    )(*srcs, *lands, send_all, recv_sibling, local_all, *handle["fwd"], after)
    return list(out[n:])


BIG = ["mla_w_in", "mla_w_uq", "mla_w_ukv", "mla_w_o", "gdn_w_in", "gdn_w_o", "sc_w_in", "sc_w_o",
       "xa_w_q", "xa_w_kv", "xa_w_o", "mlp_w1", "mlp_w2"]
TINY = [("mla_q_norm", 1), ("mla_kv_norm", 1), ("gdn_conv_w", 2), ("sc_conv_w", 2)]
REPL = ["gdn_a_log", "gdn_dt_bias", "gdn_o_norm", "norm_mix", "norm_mem", "norm_mlp", "mem_norm", "final_norm"]
WEIGHTS = ["mla_w_in", "mla_q_norm", "mla_kv_norm", "mla_w_uq", "mla_w_ukv", "mla_w_o", "gdn_w_in",
           "gdn_conv_w", "gdn_a_log", "gdn_dt_bias", "gdn_o_norm", "gdn_w_o", "sc_w_in", "sc_conv_w",
           "sc_w_o", "norm_mix", "norm_mem", "norm_mlp", "xa_w_q", "xa_w_kv", "xa_w_o", "mlp_w1",
           "mlp_w2", "mem_norm", "final_norm"]
MIXER_WEIGHTS = (["mla_w_in", "mla_w_uq", "mla_w_ukv", "mla_w_o"], ["gdn_w_in", "gdn_w_o"], ["sc_w_in", "sc_w_o"])
MIXER_PARAMS = (["norm_mem", "mla_q_norm", "mla_kv_norm"],
                ["norm_mem", "gdn_conv_w", "gdn_a_log", "gdn_dt_bias", "gdn_o_norm"],
                ["norm_mem", "sc_conv_w"])


def from_shards(a8, axis):
    a = jnp.moveaxis(a8, 0, axis)
    shp = a.shape
    return a.reshape(shp[:axis] + (shp[axis] * shp[axis + 1],) + shp[axis + 2:])


def pack_rows(flat_list, width, row_mult):
    total = sum(a.shape[-1] for a in flat_list)
    rows = -(-total // width)
    rows = -(-rows // row_mult) * row_mult
    pad = rows * width - total
    parts = list(flat_list)
    if pad:
        parts.append(jnp.zeros((pad,), flat_list[0].dtype))
    return jnp.concatenate(parts, axis=-1).reshape(rows, width)


def unpack_rows(packed, shapes):
    lead = packed.shape[:-2]
    flat = packed.reshape(lead + (-1,))
    out, off = [], 0
    for shp in shapes:
        n = math.prod(shp)
        out.append(flat[..., off:off + n].reshape(lead + tuple(shp)))
        off += n
    return out


def _swap_halves(w):
    half = w.shape[-1] // 2
    return jnp.concatenate([w[..., half:], w[..., :half]], axis=-1)


def _pad_last(w, n):
    return jnp.pad(w, [(0, 0)] * (w.ndim - 1) + [(0, n - w.shape[-1])])


def _unblock(w8):
    return jnp.transpose(w8, (1, 0, 2)).reshape(w8.shape[1], -1)


def _stack_rows(w8):
    return w8.reshape(-1, w8.shape[-1])


def rms_op(name, rows, d, out_dtype):
    tm = rows if rows * d * 4 <= BLOCK_BYTES else 512
    return make_tile_op(fn_rms, name, ["row", "par"], [True, True], [("row", d, out_dtype)], rows, min(tm, rows))


def seg_memory(p, mem):
    return rms_op("rms_memory", mem.shape[0], mem.shape[1], BF16)(mem, p["mem_norm"].reshape(1, -1))[0]


def seg_mixer(i, wts, p, x, h, rope_c, rope_s):
    s, d = x.shape
    j, kind = i // N_MIXERS, i % N_MIXERS
    tag = f"l{i}"
    hd = MLA_NOPE
    next_gain = p["norm_mem"][i].reshape(1, d)
    if kind == 0:
        w_in = _stack_rows(wts["mla_w_in"])
        w_cq = w_in[:, :MLA_Q_RANK]
        w_ckv = w_in[:, MLA_Q_RANK:MLA_Q_RANK + MLA_KV_RANK]
        w_kr = w_in[:, MLA_Q_RANK + MLA_KV_RANK:]
        c_qn, c_kvn, k_rope = make_tile_op(
            fn_mla_down, tag + "_mla_down", ["row", "row", "row"] + ["par"] * 6, [True, False, False] + [True] * 6,
            [("row", MLA_Q_RANK, BF16), ("row", MLA_KV_RANK, BF16), ("row", hd, F32)], s, 256)(
            h, rope_c, rope_s, w_cq, w_ckv, _pad_last(w_kr, hd), _pad_last(_swap_halves(w_kr), hd),
            p["mla_q_norm"][j].reshape(1, -1), p["mla_kv_norm"][j].reshape(1, -1))
        w_uq8 = wts["mla_w_uq"]
        w_qn = _unblock(w_uq8[:, :, :MLA_NOPE])
        w_qr = w_uq8[:, :, MLA_NOPE:]
        w_qr_p = _unblock(_pad_last(w_qr, hd))
        w_qr_s = _unblock(_pad_last(_swap_halves(w_qr), hd))
        nq = MLA_HEADS * hd
        q_nope, q_rope, kv = make_tile_op(
            fn_mla_up, tag + "_mla_up", ["row", "row", "row", "row"] + ["par"] * 4, [True, True, False, False] + [True] * 4,
            [("row", nq, BF16), ("row", nq, F32), ("row", 2 * nq, BF16)], s, 256)(
            c_qn, c_kvn, rope_c, rope_s, w_qn, w_qr_p, w_qr_s, _unblock(wts["mla_w_ukv"]))
        n_groups = MLA_QUERY_GROUPS if s % (MLA_QUERY_GROUPS * 256) == 0 else 1
        rows_g = s // n_groups
        o_groups = []
        for grp in range(n_groups):
            r0, r1 = grp * rows_g, (grp + 1) * rows_g
            o_groups.append(make_tile_op(
                fn_mla_attn, f"{tag}_mla_attn{grp}", [("rowh", hd), ("rowh", hd), ("parh", hd, 2), "par"],
                [True] * 4, [(("rowh", hd), nq, BF16)], rows_g, 256, MLA_HEADS, row_base=r0)(
                q_nope[r0:r1], q_rope[r0:r1], kv[:r1], k_rope[:r1])[0])
        o = jnp.concatenate(o_groups, axis=0)
        return make_mm_res_rms(tag + "_mla_o")(x, o, _stack_rows(wts["mla_w_o"]), next_gain)
    if kind == 1:
        ng = GDN_HEADS * GDN_DK
        w_in = _unblock(wts["gdn_w_in"])
        cw = p["gdn_conv_w"][j]
        conv_out = []
        for part, nm in enumerate(("q", "k", "v")):
            cols = slice(part * ng, (part + 1) * ng)
            pre = make_mm(f"{tag}_gdn_in_{nm}", F32)(h, w_in[:, cols])
            conv_out.append(make_conv(f"{tag}_gdn_conv_{nm}")(pre, cw[:, cols]))
        gate = make_mm(tag + "_gdn_in_g", F32)(h, w_in[:, 3 * ng:4 * ng])
        ba = make_mm(tag + "_gdn_in_ba", F32)(h, _pad_last(w_in[:, 4 * ng:], LANES))
        heads_row = ("row", GDN_DK, GDN_HEADS)
        q, k, v = make_tile_op(fn_gdn_prep, tag + "_gdn_prep", [heads_row] * 3, [True] * 3,
                               [(heads_row, ng, F32)] * 3, s, 512)(*conv_out)
        alog = jnp.pad(p["gdn_a_log"][j].reshape(1, -1), ((0, 0), (GDN_HEADS, LANES - 2 * GDN_HEADS)))
        dtb = jnp.pad(p["gdn_dt_bias"][j].reshape(1, -1), ((0, 0), (GDN_HEADS, LANES - 2 * GDN_HEADS)))
        beta_b, g_b = make_tile_op(fn_gdn_gates, tag + "_gdn_gates", ["row", "par", "par"], [True] * 3,
                                   [("row", ng, F32)] * 2, s, 512)(ba, alog, dtb)
        o = make_gdn(tag + "_gdn_core")(q, k, v, g_b, beta_b)
        o = make_tile_op(fn_gdn_out, tag + "_gdn_out", [heads_row, heads_row, "par"],
                         [True] * 3, [(heads_row, ng, BF16)], s, 512)(
            o, gate, p["gdn_o_norm"][j].reshape(1, -1))[0]
        return make_mm_res_rms(tag + "_gdn_o")(x, o, _stack_rows(wts["gdn_w_o"]), next_gain)
    w_in = _unblock(wts["sc_w_in"])
    b_gate = make_mm(tag + "_sc_in_b", F32)(h, w_in[:, :d])
    c_gate = make_mm(tag + "_sc_in_c", F32)(h, w_in[:, d:2 * d])
    u = make_mm(tag + "_sc_in_u", F32)(h, w_in[:, 2 * d:])
    cu = make_tile_op(fn_mul, tag + "_sc_cu", ["row", "row"], [True, True], [("row", d, F32)], s, 512)(
        c_gate, u)[0]
    cv = make_conv(tag + "_sc_conv")(cu, p["sc_conv_w"][j])
    yv = make_tile_op(fn_mul, tag + "_sc_gate", ["row", "row"], [True, True], [("row", d, BF16)], s, 512)(
        b_gate, cv)[0]
    return make_mm_res_rms(tag + "_sc_o")(x, yv, _stack_rows(wts["sc_w_o"]), next_gain)


def seg_xattn(i, wts, p, x, hx, mem_n):
    s, d = x.shape
    tag = f"l{i}"
    kv = make_mm(tag + "_xa_kv", BF16, blocked=True)(mem_n, wts["xa_w_kv"])
    heads = ("row", X_HEAD_DIM, X_HEADS)
    o = make_tile_op(fn_xattn, tag + "_xattn",
                     ["row", ("par", X_HEAD_DIM, X_HEADS), ("par", X_HEAD_DIM, 2 * X_HEADS)], [True] * 3,
                     [(heads, d, BF16)], s, 512)(hx, _stack_rows(wts["xa_w_q"]), kv)[0]
    return make_mm_res_rms(tag + "_xa_o")(x, o, _stack_rows(wts["xa_w_o"]), p["norm_mlp"][i].reshape(1, d))


def seg_mlp(i, wts, p, x, hm):
    d = x.shape[1]
    gain = p["norm_mix"][i + 1].reshape(1, d) if i + 1 < DEPTH else None
    return make_mlp(f"l{i}_mlp", gain is not None)(x, hm, wts["mlp_w1"], _stack_rows(wts["mlp_w2"]), gain)


def segments():
    segs = []
    for i in range(DEPTH):
        j, kind = i // N_MIXERS, i % N_MIXERS
        segs.append((f"l{i}_mixer", [(n, j) for n in MIXER_WEIGHTS[kind]], MIXER_PARAMS[kind], "mixer"))
        segs.append((f"l{i}_xattn", [(n, i) for n in ("xa_w_q", "xa_w_kv", "xa_w_o")], ["norm_mlp"], "xattn"))
        segs.append((f"l{i}_mlp", [(n, i) for n in ("mlp_w1", "mlp_w2")], ["norm_mix"] if i + 1 < DEPTH else [],
                     "mlp"))
    return segs


def run_segment(index, kind, wts, p, x, h, mem_n, rope_c, rope_s):
    layer = index // 3
    if kind == "mixer":
        return seg_mixer(layer, wts, p, x, h, rope_c, rope_s)
    if kind == "xattn":
        return seg_xattn(layer, wts, p, x, h, mem_n)
    return seg_mlp(layer, wts, p, x, h)


def rope_tables(positions):
    inv_freq = ROPE_THETA ** (-jnp.arange(0, MLA_ROPE, 2, dtype=F32) / MLA_ROPE)
    ang = positions.astype(F32)[:, None] * inv_freq
    cos, sin = jnp.cos(ang), jnp.sin(ang)
    zeros = jnp.zeros((positions.shape[0], MLA_NOPE - MLA_ROPE), F32)
    return jnp.concatenate([cos, cos, zeros], axis=-1), jnp.concatenate([-sin, sin, zeros], axis=-1)


def kernel(x, mem, positions, mla_w_in, mla_q_norm, mla_kv_norm, mla_w_uq, mla_w_ukv, mla_w_o, gdn_w_in, gdn_conv_w, gdn_a_log, gdn_dt_bias, gdn_o_norm, gdn_w_o, sc_w_in, sc_conv_w, sc_w_o, norm_mix, norm_mem, norm_mlp, xa_w_q, xa_w_kv, xa_w_o, mlp_w1, mlp_w2, mem_norm, final_norm, loss_target, m_mla_w_in, m_mla_q_norm, m_mla_kv_norm, m_mla_w_uq, m_mla_w_ukv, m_mla_w_o, m_gdn_w_in, m_gdn_conv_w, m_gdn_a_log, m_gdn_dt_bias, m_gdn_o_norm, m_gdn_w_o, m_sc_w_in, m_sc_conv_w, m_sc_w_o, m_norm_mix, m_norm_mem, m_norm_mlp, m_xa_w_q, m_xa_w_kv, m_xa_w_o, m_mlp_w1, m_mlp_w2, m_mem_norm, m_final_norm, v_mla_w_in, v_mla_q_norm, v_mla_kv_norm, v_mla_w_uq, v_mla_w_ukv, v_mla_w_o, v_gdn_w_in, v_gdn_conv_w, v_gdn_a_log, v_gdn_dt_bias, v_gdn_o_norm, v_gdn_w_o, v_sc_w_in, v_sc_conv_w, v_sc_w_o, v_norm_mix, v_norm_mem, v_norm_mlp, v_xa_w_q, v_xa_w_kv, v_xa_w_o, v_mlp_w1, v_mlp_w2, v_mem_norm, v_final_norm):
    args = locals()
    w_loc = {n: args[n] for n in WEIGHTS}
    m_loc = {n: args["m_" + n] for n in WEIGHTS}
    v_loc = {n: args["v_" + n] for n in WEIGHTS}
    me = 4 * lax.axis_index("x") + 2 * lax.axis_index("y") + lax.axis_index("c")
    segs = segments()

    w16 = {n: w_loc[n].astype(BF16) for n in BIG}
    tiny_pack = pack_rows([w_loc[n].reshape(-1) for n, _ in TINY], LANES, 8)
    gather_handles, token = gather2_start(
        [[tiny_pack]] + [[w16[n][layer] for n, layer in units] for _, units, _, _ in segs], "gather_start")

    x_cur = x[0]
    rope_c, rope_s = rope_tables(positions[0])
    tiny_handle, _ = gather2_forward(gather_handles[0], token, "gather_forward_tiny")
    tiny_all = gather2_wait(tiny_handle, token, "gather_wait_tiny")[0]
    gather_handles = gather_handles[1:]
    params = {}
    for (n, ax), a8 in zip(TINY, unpack_rows(tiny_all, [w_loc[n].shape for n, _ in TINY])):
        params[n] = from_shards(a8, ax)
    for n in REPL:
        params[n] = w_loc[n]

    mem_n, vjp_memory = jax.vjp(lambda p_: seg_memory(p_, mem[0]), {"mem_norm": params["mem_norm"]})
    h_cur, vjp_first_norm = jax.vjp(
        lambda p_, x_: rms_op("l0_rms_mix", x_.shape[0], x_.shape[1], BF16)(x_, p_["norm_mix"][0].reshape(1, -1))[0],
        {"norm_mix": params["norm_mix"]}, x_cur)
    vjps = []
    forwarded, _ = gather2_forward(gather_handles[0], token, f"gather_forward_{segs[0][0]}")
    for index, (tag, units, p_names, kind) in enumerate(segs):
        landed = gather2_wait(forwarded, token if index == 0 else x_cur, f"gather_wait_{tag}")
        wts = {n: a for (n, _), a in zip(units, landed)}
        p_seg = {n: params[n] for n in p_names}
        if index + 1 < len(segs):
            forwarded, (p_seg[p_names[0]],) = gather2_forward(
                gather_handles[index + 1], landed[0], f"gather_forward_{segs[index + 1][0]}",
                carry=[p_seg[p_names[0]]])
        outs, vjp_seg = jax.vjp(
            lambda w_, p_, x_, h_, m_, index=index, kind=kind:
            run_segment(index, kind, w_, p_, x_, h_, m_, rope_c, rope_s),
            wts, p_seg, x_cur, h_cur, mem_n)
        x_cur, h_cur = outs[0], (outs[1] if len(outs) > 1 else None)
        vjps.append(vjp_seg)

    loss_vec, g_x, d_final = loss_head(x_cur, params["final_norm"].reshape(1, -1), loss_target[0], "loss_head")

    grads = {n: jnp.zeros_like(params[n]) for n in params}
    grads["final_norm"] = d_final.reshape(-1)
    g_mem_n = jnp.zeros_like(mem_n)
    g_h = None
    scatter_handles = []
    for (tag, units, _, _), vjp_seg in zip(reversed(segs), reversed(vjps)):
        g_wts, g_p, g_x, g_h, g_m = vjp_seg((g_x,) if g_h is None else (g_x, g_h))
        for n, g in g_p.items():
            grads[n] = grads[n] + g
        g_mem_n = g_mem_n + g_m
        handle, _, (g_h,) = exchange_start("scatter", [g_wts[n] for n, _ in units], f"scatter_start_{tag}",
                                           carry=[g_h])
        scatter_handles.append((units, handle))
    grads["mem_norm"] = grads["mem_norm"] + vjp_memory(g_mem_n)[0]["mem_norm"]
    g_first, g_x_norm = vjp_first_norm(g_h)
    grads["norm_mix"] = grads["norm_mix"] + g_first["norm_mix"]
    g_x = g_x + g_x_norm

    small_names = [n for n, _ in TINY] + REPL
    small_g = pack_rows([loss_vec[0, :1]] + [grads[n].astype(F32).reshape(-1) for n in small_names], PACK_W, 8)
    small_handle, _, _ = exchange_start("gather", [small_g], "gather_start_small_grads")

    g_recv = {}
    for units, handle in scatter_handles:
        landed = exchange_wait(handle, g_x, f"scatter_wait_{units[0][0]}_{units[0][1]}")
        g_recv.update(dict(zip(units, landed)))

    res = {}
    for n in BIG:
        outs = None
        for layer in range(w_loc[n].shape[0]):
            outs = adamw(g_recv[n, layer], w_loc[n], m_loc[n], v_loc[n], layer, outs, f"adamw_{n}_{layer}")
        for kind, a in zip(("grad", "delta", "m", "v"), outs):
            res[(kind, n)] = a
    small_recv = exchange_wait(small_handle, res[("grad", BIG[-1])], "gather_wait_small_grads")[0]

    def full_small(d):
        parts = [jnp.zeros((1,), F32)]
        for n, ax in TINY:
            full_shape = params[n].shape
            start = [0] * len(full_shape)
            start[ax] = me * d[n].shape[ax]
            parts.append(lax.dynamic_update_slice(jnp.zeros(full_shape, F32), d[n], start).reshape(-1))
        parts += [d[n].reshape(-1) for n in REPL]
        return pack_rows(parts, PACK_W, 8)

    outs_small = adamw(small_recv, full_small(w_loc)[None], full_small(m_loc)[None], full_small(v_loc)[None],
                       0, None, "adamw_small")
    small_shapes = [(1,)] + [params[n].shape for n, _ in TINY] + [w_loc[n].shape for n in REPL]
    loss = None
    for kind, packed in zip(("grad", "delta", "m", "v"), outs_small):
        parts = unpack_rows(packed[0], small_shapes)
        if kind == "grad":
            loss = parts[0][0]
        for (n, ax), a in zip(TINY, parts[1:1 + len(TINY)]):
            start = [0] * a.ndim
            start[ax] = me * w_loc[n].shape[ax]
            res[(kind, n)] = lax.dynamic_slice(a, start, w_loc[n].shape)
        for n, a in zip(REPL, parts[1 + len(TINY):]):
            res[(kind, n)] = a

    out = [loss, g_x[None]]
    for kind in ("grad", "delta", "m", "v"):
        out += [res[(kind, n)] for n in WEIGHTS]
    return tuple(out)
```

```python
import math

import jax
import jax.numpy as jnp
from jax import lax
from jax.experimental import pallas as pl
from jax.experimental.pallas import tpu as pltpu

F32 = jnp.float32
BF16 = jnp.bfloat16

N_DEV = 8
LANES = 128
EPS = 1e-6
ROPE_THETA = 10000.0
MLA_HEADS, MLA_NOPE, MLA_ROPE, MLA_V = 8, 128, 64, 128
MLA_Q_RANK, MLA_KV_RANK = 384, 256
GDN_HEADS, GDN_DK, GDN_CONV, GDN_CHUNK = 8, 128, 4, 64
X_HEADS, X_HEAD_DIM = 4, 256
DEPTH, N_MIXERS = 4, 3
ADAM_LR, ADAM_B1, ADAM_B2, ADAM_EPS, ADAM_WD, ADAM_STEP = 0.001, 0.9, 0.999, 1e-08, 0.01, 10
MLA_QUERY_GROUPS = 4
NEG_BIG = -1e30
PACK_W = 1024


_NN = (((1,), (0,)), ((), ()))
_NT = (((1,), (1,)), ((), ()))
_TN = (((0,), (0,)), ((), ()))
_NN3 = (((2,), (1,)), ((0,), (0,)))
_NT3 = (((2,), (2,)), ((0,), (0,)))
_TN3 = (((1,), (1,)), ((0,), (0,)))


def _dot(a, b, dims):
    return lax.dot_general(a, b, dims, preferred_element_type=F32)


def _hi_lo(x):
    hi = x.astype(BF16)
    return hi, (x - hi.astype(F32)).astype(BF16)


def _split3(x):
    hi = x.astype(BF16)
    r = x - hi.astype(F32)
    mid = r.astype(BF16)
    return hi, mid, (r - mid.astype(F32)).astype(BF16)


def _dg(a, b, dims, prec):
    if prec == "h":
        return lax.dot_general(a, b, dims, precision=lax.Precision.HIGHEST, preferred_element_type=F32)
    if prec == "m":
        a_hi, a_lo = _hi_lo(a)
        b_hi, b_lo = _hi_lo(b)
        return _dot(a_hi, b_hi, dims) + _dot(a_hi, b_lo, dims) + _dot(a_lo, b_hi, dims)
    return _dot(a.astype(BF16), b.astype(BF16), dims)


def _dg_sel(sel, x, dims, sel_first):
    s16 = sel.astype(BF16)
    parts = [(_dot(s16, piece, dims) if sel_first else _dot(piece, s16, dims)) for piece in _split3(x)]
    return parts[0] + parts[1] + parts[2]


class _Ops:
    def __init__(self, prec, differentiable, batched=False):
        d_nn, d_nt, d_tn = (_NN3, _NT3, _TN3) if batched else (_NN, _NT, _TN)

        def nn(a, b):
            return _dg(a, b, d_nn, prec)

        def nt(a, b):
            return _dg(a, b, d_nt, prec)

        def tn(a, b):
            return _dg(a, b, d_tn, prec)

        if differentiable:
            dnn = jax.custom_vjp(nn)
            dnn.defvjp(lambda a, b: (nn(a, b), (a, b)), lambda r, g: (nt(g, r[1]), tn(r[0], g)))
            dnt = jax.custom_vjp(nt)
            dnt.defvjp(lambda a, b: (nt(a, b), (a, b)), lambda r, g: (nn(g, r[1]), tn(g, r[0])))
            dtn = jax.custom_vjp(tn)
            dtn.defvjp(lambda a, b: (tn(a, b), (a, b)), lambda r, g: (nt(r[1], g), nn(r[0], g)))
            nn, nt, tn = dnn, dnt, dtn
        self.nn, self.nt, self.tn = nn, nt, tn


class _SelOps:
    def __init__(self, differentiable, batched=False):
        d_nn, d_nt, d_tn = (_NN3, _NT3, _TN3) if batched else (_NN, _NT, _TN)

        def sel_nn(sel, x):
            return _dg_sel(sel, x, d_nn, True)

        def sel_nt(sel, x):
            return _dg_sel(sel, x, d_nt, True)

        if differentiable:
            dnn = jax.custom_vjp(sel_nn)
            dnn.defvjp(lambda s, x: (sel_nn(s, x), s),
                       lambda s, g: (jnp.zeros_like(s), _dg_sel(s, g, d_tn, True)))
            dnt = jax.custom_vjp(sel_nt)
            dnt.defvjp(lambda s, x: (sel_nt(s, x), s),
                       lambda s, g: (jnp.zeros_like(s), _dg_sel(s, g, d_tn, False)))
            sel_nn, sel_nt = dnn, dnt
        self.sel_nn, self.sel_nt = sel_nn, sel_nt


class _OpSet:
    def __init__(self, differentiable):
        self.b = _Ops("b", differentiable)
        self.h = _Ops("h", differentiable)
        self.bb = _Ops("b", differentiable, batched=True)
        self.bm = _Ops("m", differentiable, batched=True)
        self.bs = _SelOps(differentiable, batched=True)


_PLAIN = _OpSet(False)
_DIFF = _OpSet(True)


def _params(sem):
    return pltpu.CompilerParams(dimension_semantics=sem)


BLOCK_BYTES = 4 * 1024 * 1024


def _pick(n, cands):
    for c in cands:
        if n % c == 0:
            return c
    return n


def _tile(n, cap):
    if n <= cap:
        return n
    return _pick(n, tuple(c for c in (2048, 1024, 768, 512, 384, 256, 128) if c <= cap))


def matmul(a, b, form, out_dtype, name, res=None, blocked=False, relu_gate=None, rms_gain=None, a_relu2=False):
    if form == "nn":
        m, k = a.shape
        k2, n = (b.shape[1], N_DEV * b.shape[2]) if blocked else b.shape
    elif form == "nt":
        m, k = a.shape
        n, k2 = (b.shape[1], N_DEV * b.shape[2]) if blocked else b.shape
    else:
        (k, m), (k2, n) = a.shape, b.shape
    assert k == k2, (a.shape, b.shape, form)
    tk = k if k <= 2048 else _tile(k, 1024)
    cb = nb = 1
    if blocked:
        cb = (k if form == "nt" else n) // N_DEV
        nb = _pick(N_DEV, tuple(c for c in (8, 4, 2, 1) if c * cb <= 1024))
    if blocked and form == "nt":
        tk = nb * cb
    if blocked and form != "nt":
        tn = nb * cb
    else:
        tn = _tile(n, min(1024, BLOCK_BYTES // (tk * b.dtype.itemsize)))
    out_elems = BLOCK_BYTES // 2 if (out_dtype == BF16 and res is None) else BLOCK_BYTES // 4
    tm = _tile(m, min(BLOCK_BYTES // (tk * a.dtype.itemsize), out_elems // tn))
    nk = k // tk
    dims = {"nn": _NN, "nt": _NT, "tn": _TN}[form]

    a_spec = {"nn": pl.BlockSpec((tm, tk), lambda i, j, kk: (i, kk)),
              "nt": pl.BlockSpec((tm, tk), lambda i, j, kk: (i, kk)),
              "tn": pl.BlockSpec((tk, tm), lambda i, j, kk: (kk, i))}[form]
    if blocked and form == "nn":
        b_spec = pl.BlockSpec((nb, tk, cb), lambda i, j, kk: (j, kk, 0))
    elif blocked and form == "nt":
        b_spec = pl.BlockSpec((nb, tn, cb), lambda i, j, kk: (kk, j, 0))
    else:
        b_spec = {"nn": pl.BlockSpec((tk, tn), lambda i, j, kk: (kk, j)),
                  "nt": pl.BlockSpec((tn, tk), lambda i, j, kk: (j, kk)),
                  "tn": pl.BlockSpec((tk, tn), lambda i, j, kk: (kk, j))}[form]
    c_spec = pl.BlockSpec((tm, tn), lambda i, j, kk: (i, j))
    out_shape = jax.ShapeDtypeStruct((m, n), out_dtype)
    o_spec = c_spec
    blocked_out = blocked and form == "tn"
    if blocked_out:
        out_shape = jax.ShapeDtypeStruct((N_DEV, m, cb), out_dtype)
        o_spec = pl.BlockSpec((nb, tm, cb), lambda i, j, kk: (j, i, 0))
    has_res, has_gate, has_gain = res is not None, relu_gate is not None, rms_gain is not None
    extras = [e for e in (res, relu_gate) if e is not None]
    n_in = 2 + len(extras) + has_gain
    second = has_gain
    assert not (second and (blocked_out or tn != n))

    def body(*refs):
        a_ref, b_ref = refs[0], refs[1]
        r_ref = refs[2] if has_res else None
        gate_ref = refs[2 + has_res] if has_gate else None
        gain_ref = refs[n_in - 1] if has_gain else None
        o_ref = refs[n_in]
        if a_relu2:
            relu = jnp.maximum(a_ref[...].astype(F32), 0.0)
            a_val = (relu * relu).astype(BF16)
        else:
            a_val = a_ref[...].astype(BF16)
        if blocked and form == "nn":
            part = jnp.concatenate([_dot(a_val, b_ref[t].astype(BF16), dims) for t in range(nb)], axis=-1)
        elif blocked and form == "nt":
            part = _dot(a_val[:, :cb], b_ref[0].astype(BF16), dims)
            for t in range(1, nb):
                part = part + _dot(a_val[:, t * cb:(t + 1) * cb], b_ref[t].astype(BF16), dims)
        else:
            part = _dot(a_val, b_ref[...].astype(BF16), dims)

        def finish(acc):
            if has_res:
                acc = acc + r_ref[...].astype(F32)
            if has_gate:
                acc = acc * (2.0 * jnp.maximum(gate_ref[...].astype(F32), 0.0))
            if blocked_out:
                for t in range(nb):
                    o_ref[t] = acc[:, t * cb:(t + 1) * cb].astype(out_dtype)
            else:
                o_ref[...] = acc.astype(out_dtype)
            if has_gain:
                refs[n_in + 1][...] = _rms(acc, gain_ref[...]).astype(BF16)

        if nk == 1:
            finish(part)
        else:
            acc_ref = refs[-1]
            kk = pl.program_id(2)

            @pl.when(kk == 0)
            def _():
                acc_ref[...] = part

            @pl.when(jnp.logical_and(kk > 0, kk < nk - 1))
            def _():
                acc_ref[...] += part

            @pl.when(kk == nk - 1)
            def _():
                finish(acc_ref[...] + part)

    in_specs = [a_spec, b_spec] + [c_spec] * len(extras)
    args = [a, b] + extras
    if has_gain:
        in_specs.append(pl.BlockSpec((1, tn), lambda i, j, kk: (0, j)))
        args.append(rms_gain)
    if second:
        out_shape = [out_shape, jax.ShapeDtypeStruct((m, n), BF16)]
        o_spec = [o_spec, c_spec]
    return pl.pallas_call(
        body, name=name,
        out_shape=out_shape,
        grid=(m // tm, n // tn, nk),
        in_specs=in_specs, out_specs=o_spec,
        scratch_shapes=[pltpu.VMEM((tm, tn), F32)] if nk > 1 else [],
        compiler_params=_params(("parallel", "parallel", "arbitrary")),
    )(*args)


def make_mm(name, out_dtype, with_res=False, blocked=False):
    def bwd_mm(a, w, g):
        da = matmul(g, w, "nt", a.dtype, name + "_da", blocked=blocked)
        dw = matmul(a, g, "tn", w.dtype, name + "_dw", blocked=blocked)
        return da, dw

    if with_res:
        @jax.custom_vjp
        def op(res, a, w):
            return matmul(a, w, "nn", out_dtype, name + "_f", res=res, blocked=blocked)

        def fwd(res, a, w):
            return op(res, a, w), (a, w)

        def bwd(saved, g):
            return (g,) + bwd_mm(*saved, g)
    else:
        @jax.custom_vjp
        def op(a, w):
            return matmul(a, w, "nn", out_dtype, name + "_f", blocked=blocked)

        def fwd(a, w):
            return op(a, w), (a, w)

        def bwd(saved, g):
            return bwd_mm(*saved, g)
    op.defvjp(fwd, bwd)
    return op


def _rms_fan_bwd(x_new, gain, dx, dh, name):
    rows, d = x_new.shape
    outs = [("row", d, F32), ("row", d, BF16)]
    return tile_bwd(fn_fan_rms, name, ["row", "par"], [x_new, gain], [True, True], outs, [dx, dh],
                    rows, min(512, rows), 0)


def make_mm_res_rms(name):
    @jax.custom_vjp
    def op(res, a, w, gain):
        return tuple(matmul(a, w, "nn", F32, name + "_f", res=res, rms_gain=gain))

    def fwd(res, a, w, gain):
        x_new, h = op(res, a, w, gain)
        return (x_new, h), (a, w, x_new, gain)

    def bwd(saved, cts):
        a, w, x_new, gain = saved
        dx, dgain = _rms_fan_bwd(x_new, gain, cts[0], cts[1], name + "_nb")
        da = matmul(dx, w, "nt", a.dtype, name + "_da")
        dw = matmul(a, dx, "tn", w.dtype, name + "_dw")
        return dx, da, dw, dgain

    op.defvjp(fwd, bwd)
    return op


def make_mlp(name, with_norm):
    def run(x, h, w1, w2, gain):
        a = matmul(h, w1, "nn", BF16, name + "_1_f", blocked=True)
        out = matmul(a, w2, "nn", F32, name + "_2_f", res=x, rms_gain=gain if with_norm else None, a_relu2=True)
        return (tuple(out) if with_norm else (out,)), a

    @jax.custom_vjp
    def op(x, h, w1, w2, gain):
        return run(x, h, w1, w2, gain)[0]

    def fwd(x, h, w1, w2, gain):
        out, a = run(x, h, w1, w2, gain)
        return out, (h, w1, w2, gain, a, out[0])

    def bwd(saved, cts):
        h, w1, w2, gain, a, x_new = saved
        if with_norm:
            dx, dgain = _rms_fan_bwd(x_new, gain, cts[0], cts[1], name + "_nb")
        else:
            dx, dgain = cts[0], None
        da = matmul(dx, w2, "nt", BF16, name + "_2_da", relu_gate=a)
        dw2 = matmul(a, dx, "tn", w2.dtype, name + "_2_dw", a_relu2=True)
        dh = matmul(da, w1, "nt", h.dtype, name + "_1_da", blocked=True)
        dw1 = matmul(h, da, "tn", w1.dtype, name + "_1_dw", blocked=True)
        return dx, dh, dw1, dw2, dgain

    op.defvjp(fwd, bwd)
    return op


def _kind(k):
    if isinstance(k, str):
        return k, None, 1
    return k[0], k[1], (k[2] if len(k) > 2 else 1)


def _tile_spec(kind, shape, tm, heads):
    k, d, ns = _kind(kind)
    if k == "row":
        return pl.BlockSpec((tm, shape[1]), (lambda h, i: (i, 0)) if heads else (lambda i: (i, 0)))
    if k == "par":
        return pl.BlockSpec(tuple(shape), (lambda h, i: (0, 0)) if heads else (lambda i: (0, 0)))
    if k == "rowh":
        return pl.BlockSpec((tm, d * ns), lambda h, i: (i, h))
    if k == "parh":
        return pl.BlockSpec((shape[0], d * ns), lambda h, i: (0, h))
    raise ValueError(kind)


def _tile_grid(rows, tm, heads):
    n_rows = rows // tm
    return ((heads, n_rows) if heads else (n_rows,)), (1 if heads else 0)


def _split_vals(kinds, refs):
    vals, counts = [], []
    for kind, r in zip(kinds, refs):
        _, d, ns = _kind(kind)
        v = r[...].astype(F32)
        vals += [v] if ns == 1 else [v[:, p * d:(p + 1) * d] for p in range(ns)]
        counts.append(ns)
    return vals, counts


def tile_fwd(fn, name, kinds, args, outs, rows, tm, heads, row_base=0):
    grid, row_axis = _tile_grid(rows, tm, heads)
    n_in = len(args)
    out_shapes = [jax.ShapeDtypeStruct((rows, w), dt) for (_, w, dt) in outs]

    def body(*refs):
        vals, _ = _split_vals(kinds, refs[:n_in])
        row0 = row_base + pl.program_id(row_axis) * tm
        res = list(fn(_PLAIN, row0, *vals))
        for o_ref, (k, _, _) in zip(refs[n_in:], outs):
            pieces = [res.pop(0) for _ in range(_kind(k)[2])]
            v = pieces[0] if len(pieces) == 1 else jnp.concatenate(pieces, axis=-1)
            o_ref[...] = v.astype(o_ref.dtype)

    return pl.pallas_call(
        body, name=name, out_shape=out_shapes, grid=grid,
        in_specs=[_tile_spec(k, a.shape, tm, heads) for k, a in zip(kinds, args)],
        out_specs=[_tile_spec(k, (rows, w), tm, heads) for (k, w, _) in outs],
        compiler_params=_params(("arbitrary",) * len(grid)),
    )(*args)


def tile_bwd(fn, name, kinds, args, diff, outs, cts, rows, tm, heads, row_base=0):
    grid, row_axis = _tile_grid(rows, tm, heads)
    n_in, n_ct = len(args), len(cts)
    diff_idx = [i for i, d in enumerate(diff) if d]
    g_shapes, g_specs = [], []
    for i in diff_idx:
        k = _kind(kinds[i])[0]
        dt = args[i].dtype if k in ("row", "rowh") else F32
        g_shapes.append(jax.ShapeDtypeStruct(args[i].shape, dt))
        g_specs.append(_tile_spec(kinds[i], args[i].shape, tm, heads))

    def body(*refs):
        in_refs, ct_refs, g_refs = refs[:n_in], refs[n_in:n_in + n_ct], refs[n_in + n_ct:]
        vals, counts = _split_vals(kinds, in_refs)
        first_piece = [sum(counts[:i]) for i in range(n_in)]
        flat_diff = [first_piece[i] + p for i in diff_idx for p in range(counts[i])]
        row_id = pl.program_id(row_axis)
        row0 = row_base + row_id * tm

        def f(*dvals):
            full = list(vals)
            for i, dv in zip(flat_diff, dvals):
                full[i] = dv
            return tuple(fn(_DIFF, row0, *full))

        _, vjp = jax.vjp(f, *[vals[i] for i in flat_diff])
        ct_vals, _ = _split_vals([k for (k, _, _) in outs], ct_refs)
        flat_grads = list(vjp(tuple(ct_vals)))
        for g_ref, i in zip(g_refs, diff_idx):
            pieces = [flat_grads.pop(0) for _ in range(counts[i])]
            g = pieces[0] if len(pieces) == 1 else jnp.concatenate(pieces, axis=-1)
            k = _kind(kinds[i])[0]
            if k in ("row", "rowh"):
                g_ref[...] = g.astype(g_ref.dtype)
            else:
                first = row_id == 0
                if heads and k == "par":
                    first = jnp.logical_and(first, pl.program_id(0) == 0)

                @pl.when(first)
                def _(g_ref=g_ref, g=g):
                    g_ref[...] = g

                @pl.when(jnp.logical_not(first))
                def _(g_ref=g_ref, g=g):
                    g_ref[...] += g

    return pl.pallas_call(
        body, name=name, out_shape=g_shapes, grid=grid,
        in_specs=[_tile_spec(k, a.shape, tm, heads) for k, a in zip(kinds, args)]
        + [_tile_spec(k, (rows, w), tm, heads) for (k, w, _) in outs],
        out_specs=g_specs,
        compiler_params=_params(("arbitrary",) * len(grid)),
    )(*args, *cts)


def make_tile_op(fn, name, kinds, diff, outs, rows, tm, heads=0, row_base=0):
    tm = min(tm, rows)

    @jax.custom_vjp
    def op(*args):
        return tuple(tile_fwd(fn, name + "_f", kinds, args, outs, rows, tm, heads, row_base))

    def fwd(*args):
        return op(*args), args

    def bwd(args, cts):
        grads = tile_bwd(fn, name + "_b", kinds, args, diff, outs, cts, rows, tm, heads, row_base)
        it = iter(grads)
        res = []
        for a, d in zip(args, diff):
            res.append(next(it).astype(a.dtype) if d else None)
        return tuple(res)

    op.defvjp(fwd, bwd)
    return op


def _rms(x, g):
    return x * lax.rsqrt(jnp.mean(x * x, axis=-1, keepdims=True) + EPS) * g


def fn_rms(ops, row0, x, g):
    return (_rms(x, g),)


def fn_fan_rms(ops, row0, x, g):
    return x, _rms(x, g)


def fn_mla_down(ops, row0, h, c, s, w_cq, w_ckv, w_kr, w_krs, g_q, g_kv):
    c_qn = _rms(ops.b.nn(h, w_cq), g_q)
    c_kvn = _rms(ops.b.nn(h, w_ckv), g_kv)
    return c_qn, c_kvn, ops.b.nn(h, w_kr) * c + ops.b.nn(h, w_krs) * s


def fn_mla_up(ops, row0, c_qn, c_kvn, c, s, w_qn, w_qr, w_qrs, w_kv):
    c_all = jnp.concatenate([c] * MLA_HEADS, axis=-1)
    s_all = jnp.concatenate([s] * MLA_HEADS, axis=-1)
    q_rope = ops.b.nn(c_qn, w_qr) * c_all + ops.b.nn(c_qn, w_qrs) * s_all
    return ops.b.nn(c_qn, w_qn), q_rope, ops.b.nn(c_kvn, w_kv)


def _softmax(s):
    m = lax.stop_gradient(jnp.max(s, axis=-1, keepdims=True))
    e = jnp.exp(s - m)
    return e / jnp.sum(e, axis=-1, keepdims=True)


def fn_xattn(ops, row0, hx, *t):
    w_q, k, v = t[:X_HEADS], t[X_HEADS:2 * X_HEADS], t[2 * X_HEADS:]
    outs = []
    for w_h, k_h, v_h in zip(w_q, k, v):
        s = ops.b.nt(ops.b.nn(hx, w_h), k_h) * (X_HEAD_DIM ** -0.5)
        outs.append(ops.b.nn(_softmax(s), v_h))
    return tuple(outs)


def _silu(x):
    return x * jax.nn.sigmoid(x)


def fn_gdn_prep(ops, row0, *t):
    nh = len(t) // 3
    qs, ks, vs = [], [], []
    for qc, kc, vc in zip(t[:nh], t[nh:2 * nh], t[2 * nh:]):
        q, k = _silu(qc), _silu(kc)
        qs.append(q * lax.rsqrt(jnp.sum(q * q, -1, keepdims=True) + EPS) * (GDN_DK ** -0.5))
        ks.append(k * lax.rsqrt(jnp.sum(k * k, -1, keepdims=True) + EPS))
        vs.append(_silu(vc))
    return tuple(qs + ks + vs)


def fn_gdn_gates(ops, row0, ba, alog, dtb):
    width = GDN_HEADS * GDN_DK
    beta = jax.nn.sigmoid(ba)
    z = ba + dtb
    softplus = jnp.maximum(z, 0.0) + jnp.log1p(jnp.exp(-jnp.abs(z)))
    g = -jnp.exp(alog) * softplus
    r = lax.broadcasted_iota(jnp.int32, (LANES, width), 0)
    c = lax.broadcasted_iota(jnp.int32, (LANES, width), 1) // GDN_DK
    e_beta = (r == c).astype(F32)
    e_g = (r == c + GDN_HEADS).astype(F32)
    return ops.h.nn(beta, e_beta), ops.h.nn(g, e_g)


def fn_gdn_out(ops, row0, *t):
    nh = (len(t) - 1) // 2
    g = t[-1]
    return tuple(_rms(o, g) * _silu(gate) for o, gate in zip(t[:nh], t[nh:2 * nh]))


def fn_mla_attn(ops, row0, qn, qr, kn, v, kr):
    s = (ops.b.nt(qn, kn) + ops.b.nt(qr, kr)) * ((MLA_NOPE + MLA_ROPE) ** -0.5)
    rows = row0 + lax.broadcasted_iota(jnp.int32, s.shape, 0)
    cols = lax.broadcasted_iota(jnp.int32, s.shape, 1)
    s = jnp.where(rows >= cols, s, NEG_BIG)
    return (ops.b.nn(_softmax(s), v),)


def _shift_down(x, d, t_idx):
    if d == 0:
        return x
    return jnp.where(t_idx >= d, pltpu.roll(x, d, axis=0), 0.0)


def _shift_up(x, d, t_idx):
    if d == 0:
        return x
    n = x.shape[0]
    return jnp.where(t_idx < n - d, pltpu.roll(x, n - d, axis=0), 0.0)


def conv_fwd(x, w, name):
    s, c = x.shape
    kw = w.shape[0]
    tc = _pick(c, (256, 128))

    def body(x_ref, w_ref, y_ref):
        xv = x_ref[...]
        t_idx = lax.broadcasted_iota(jnp.int32, xv.shape, 0)
        acc = jnp.zeros_like(xv)
        for j in range(kw):
            acc = acc + w_ref[j:j + 1, :] * _shift_down(xv, kw - 1 - j, t_idx)
        y_ref[...] = acc

    return pl.pallas_call(
        body, name=name, out_shape=jax.ShapeDtypeStruct((s, c), F32), grid=(c // tc,),
        in_specs=[pl.BlockSpec((s, tc), lambda i: (0, i)), pl.BlockSpec((kw, tc), lambda i: (0, i))],
        out_specs=pl.BlockSpec((s, tc), lambda i: (0, i)),
        compiler_params=_params(("parallel",)),
    )(x, w)


def conv_bwd(x, w, dy, name):
    s, c = x.shape
    kw = w.shape[0]
    tc = _pick(c, (256, 128))

    def body(x_ref, w_ref, dy_ref, dx_ref, dw_ref):
        xv, dyv = x_ref[...], dy_ref[...]
        t_idx = lax.broadcasted_iota(jnp.int32, xv.shape, 0)
        dx = jnp.zeros_like(xv)
        for j in range(kw):
            d = kw - 1 - j
            dx = dx + w_ref[j:j + 1, :] * _shift_up(dyv, d, t_idx)
            dw_ref[j:j + 1, :] = jnp.sum(dyv * _shift_down(xv, d, t_idx), axis=0, keepdims=True)
        dx_ref[...] = dx

    return pl.pallas_call(
        body, name=name,
        out_shape=[jax.ShapeDtypeStruct((s, c), F32), jax.ShapeDtypeStruct((kw, c), F32)],
        grid=(c // tc,),
        in_specs=[pl.BlockSpec((s, tc), lambda i: (0, i)), pl.BlockSpec((kw, tc), lambda i: (0, i)),
                  pl.BlockSpec((s, tc), lambda i: (0, i))],
        out_specs=[pl.BlockSpec((s, tc), lambda i: (0, i)), pl.BlockSpec((kw, tc), lambda i: (0, i))],
        compiler_params=_params(("parallel",)),
    )(x, w, dy)


def _conv_taps(x, w_ref, t_idx):
    kw = w_ref.shape[0]
    acc = jnp.zeros_like(x)
    for j in range(kw):
        acc = acc + w_ref[j:j + 1, :] * _shift_down(x, kw - 1 - j, t_idx)
    return acc


def gated_conv_fwd(b, c, u, w, name):
    s, ch = c.shape
    kw = w.shape[0]
    tc = _pick(ch, (256, 128))

    def body(b_ref, c_ref, u_ref, w_ref, y_ref):
        x = c_ref[...] * u_ref[...]
        t_idx = lax.broadcasted_iota(jnp.int32, x.shape, 0)
        y_ref[...] = (b_ref[...] * _conv_taps(x, w_ref, t_idx)).astype(y_ref.dtype)

    blk = pl.BlockSpec((s, tc), lambda i: (0, i))
    return pl.pallas_call(
        body, name=name, out_shape=jax.ShapeDtypeStruct((s, ch), BF16), grid=(ch // tc,),
        in_specs=[blk, blk, blk, pl.BlockSpec((kw, tc), lambda i: (0, i))], out_specs=blk,
        compiler_params=_params(("parallel",)),
    )(b, c, u, w)


def gated_conv_bwd(b, c, u, w, dy, name):
    s, ch = c.shape
    kw = w.shape[0]
    tc = _pick(ch, (256, 128))

    def body(b_ref, c_ref, u_ref, w_ref, dy_ref, db_ref, dc_ref, du_ref, dw_ref):
        cv, uv, dyv = c_ref[...], u_ref[...], dy_ref[...].astype(F32)
        x = cv * uv
        t_idx = lax.broadcasted_iota(jnp.int32, x.shape, 0)
        db_ref[...] = dyv * _conv_taps(x, w_ref, t_idx)
        dconv = dyv * b_ref[...]
        dx = jnp.zeros_like(x)
        for j in range(kw):
            d = kw - 1 - j
            dx = dx + w_ref[j:j + 1, :] * _shift_up(dconv, d, t_idx)
            dw_ref[j:j + 1, :] = jnp.sum(dconv * _shift_down(x, d, t_idx), axis=0, keepdims=True)
        dc_ref[...] = dx * uv
        du_ref[...] = dx * cv

    blk = pl.BlockSpec((s, tc), lambda i: (0, i))
    wblk = pl.BlockSpec((kw, tc), lambda i: (0, i))
    return pl.pallas_call(
        body, name=name,
        out_shape=[jax.ShapeDtypeStruct((s, ch), F32)] * 3 + [jax.ShapeDtypeStruct((kw, ch), F32)],
        grid=(ch // tc,), in_specs=[blk, blk, blk, wblk, blk], out_specs=[blk, blk, blk, wblk],
        compiler_params=_params(("parallel",)),
    )(b, c, u, w, dy)


def make_gated_conv(name):
    @jax.custom_vjp
    def op(b, c, u, w):
        return gated_conv_fwd(b, c, u, w, name + "_f")

    def fwd(b, c, u, w):
        return op(b, c, u, w), (b, c, u, w)

    def bwd(saved, dy):
        return tuple(gated_conv_bwd(*saved, dy, name + "_b"))

    op.defvjp(fwd, bwd)
    return op


def make_conv(name):
    @jax.custom_vjp
    def op(x, w):
        return conv_fwd(x, w, name + "_f")

    def fwd(x, w):
        return op(x, w), (x, w)

    def bwd(saved, dy):
        dx, dw = conv_bwd(saved[0], saved[1], dy, name + "_b")
        return dx, dw

    op.defvjp(fwd, bwd)
    return op


def _gdn_consts():
    c, d = GDN_CHUNK, GDN_DK
    i = lax.broadcasted_iota(jnp.int32, (c, c), 0)
    j = lax.broadcasted_iota(jnp.int32, (c, c), 1)
    tri = i >= j
    return dict(
        tri=tri, strict=i > j,
        tri_f=tri.astype(F32),
        eye=(i == j).astype(F32),
        lane0=(lax.broadcasted_iota(jnp.int32, (c, d), 1) == 0).astype(F32),
        last_row=(lax.broadcasted_iota(jnp.int32, (c, d), 0) == c - 1).astype(F32),
    )


def _inverse_given(m_ops):
    @jax.custom_vjp
    def given(mm_, t):
        return t

    def bwd(t, dt):
        return -m_ops.nt(m_ops.tn(t, dt), t), jnp.zeros_like(t)

    given.defvjp(lambda mm_, t: (t, t), bwd)
    return given


def _gdn_chunk(ops, q, k, v, g, beta, state, t_saved=None):
    b, m, sel = ops.bb, ops.bm, ops.bs
    nh, c, d = q.shape[0], GDN_CHUNK, GDN_DK
    k_ = _gdn_consts()

    def per_head(a):
        return jnp.broadcast_to(a, (nh,) + a.shape)

    gc = sel.sel_nn(per_head(k_["tri_f"]), g)
    col = jnp.broadcast_to(jnp.sum(gc * k_["lane0"], axis=2, keepdims=True), (nh, c, c))
    row = sel.sel_nt(per_head(k_["lane0"]), gc)
    decay = jnp.where(k_["tri"], jnp.exp(jnp.where(k_["tri"], col - row, 0.0)), 0.0)
    kb = k * beta
    mm_ = jnp.where(k_["strict"], b.nt(kb, k) * decay, 0.0)
    if t_saved is None:
        p = -mm_
        t = k_["eye"] + p
        for _ in range(int(math.log2(GDN_CHUNK)) - 1):
            p = m.nn(p, p)
            t = t + m.nn(t, p)
    else:
        t = _inverse_given(_PLAIN.bm)(mm_, t_saved)
    egc = jnp.exp(gc)
    u = b.nn(t, v * beta)
    w = b.nn(t, kb * egc)
    attn = b.nt(q, k) * decay
    v_new = u - b.nn(w, state)
    o = b.nn(q * egc, state) + b.nn(attn, v_new)
    g_last = jnp.sum(gc * k_["last_row"], axis=1, keepdims=True)
    new_state = (state * jnp.exp(jnp.broadcast_to(g_last, (nh, d, d)))
                 + b.tn(k * jnp.exp(jnp.broadcast_to(g_last, (nh, c, d)) - gc), v_new))
    return o, new_state, t


GDN_HEAD_GROUP = 8
GDN_TILE_CHUNKS = 4


def _heads_of(ref, rows, n_heads):
    d = GDN_DK
    return jnp.stack([ref[rows, h * d:(h + 1) * d] for h in range(n_heads)])


def _gdn_specs(s, reverse):
    d, hg = GDN_DK, GDN_HEAD_GROUP
    tile = min(GDN_TILE_CHUNKS * GDN_CHUNK, s)
    n_tiles = s // tile
    t_of = (lambda t: n_tiles - 1 - t) if reverse else (lambda t: t)
    seq = pl.BlockSpec((tile, hg * d), lambda grp, t: (t_of(t), grp))
    st = pl.BlockSpec((hg, tile // GDN_CHUNK, d, d), lambda grp, t: (grp, t_of(t), 0, 0))
    inv = pl.BlockSpec((hg, tile // GDN_CHUNK, GDN_CHUNK, GDN_CHUNK), lambda grp, t: (grp, t_of(t), 0, 0))
    return seq, st, inv, tile, n_tiles


def gdn_fwd(q, k, v, g, beta, name):
    s = q.shape[0]
    d, hg = GDN_DK, GDN_HEAD_GROUP
    seq, st, inv, tile, n_tiles = _gdn_specs(s, False)

    def body(q_ref, k_ref, v_ref, g_ref, b_ref, o_ref, st_ref, inv_ref, state_scr):
        @pl.when(pl.program_id(1) == 0)
        def _():
            state_scr[...] = jnp.zeros_like(state_scr)

        def step(ci, carry):
            rows = pl.ds(pl.multiple_of(ci * GDN_CHUNK, GDN_CHUNK), GDN_CHUNK)
            state = state_scr[...]
            for h in range(hg):
                st_ref[h, ci] = state[h]
            o, new_state, t = _gdn_chunk(_PLAIN, *[_heads_of(r, rows, hg) for r in (q_ref, k_ref, v_ref, g_ref, b_ref)],
                                         state)
            for h in range(hg):
                o_ref[rows, h * d:(h + 1) * d] = o[h]
                inv_ref[h, ci] = t[h]
            state_scr[...] = new_state
            return carry

        lax.fori_loop(0, tile // GDN_CHUNK, step, 0)

    return pl.pallas_call(
        body, name=name,
        out_shape=[jax.ShapeDtypeStruct(q.shape, F32),
                   jax.ShapeDtypeStruct((GDN_HEADS, s // GDN_CHUNK, d, d), F32),
                   jax.ShapeDtypeStruct((GDN_HEADS, s // GDN_CHUNK, GDN_CHUNK, GDN_CHUNK), F32)],
        grid=(GDN_HEADS // hg, n_tiles), in_specs=[seq] * 5, out_specs=[seq, st, inv],
        scratch_shapes=[pltpu.VMEM((hg, d, d), F32)],
        compiler_params=_params(("parallel", "arbitrary")),
    )(q, k, v, g, beta)


def gdn_bwd(q, k, v, g, beta, states, inverses, do, name):
    s = q.shape[0]
    d, hg = GDN_DK, GDN_HEAD_GROUP
    seq, st, inv, tile, n_tiles = _gdn_specs(s, True)
    tile_chunks = tile // GDN_CHUNK

    def body(q_ref, k_ref, v_ref, g_ref, b_ref, st_ref, inv_ref, do_ref, dq_ref, dk_ref, dv_ref, dg_ref, db_ref,
             dstate_scr):
        @pl.when(pl.program_id(1) == 0)
        def _():
            dstate_scr[...] = jnp.zeros_like(dstate_scr)

        def step(it, carry):
            ci = tile_chunks - 1 - it
            rows = pl.ds(pl.multiple_of(ci * GDN_CHUNK, GDN_CHUNK), GDN_CHUNK)
            prim = [_heads_of(r, rows, hg) for r in (q_ref, k_ref, v_ref, g_ref, b_ref)]
            prim.append(jnp.stack([st_ref[h, ci] for h in range(hg)]))
            t_saved = jnp.stack([inv_ref[h, ci] for h in range(hg)])
            _, vjp = jax.vjp(lambda *a: _gdn_chunk(_DIFF, *a, t_saved=t_saved)[:2], *prim)
            grads = vjp((_heads_of(do_ref, rows, hg), dstate_scr[...]))
            for g_ref_out, gr in zip((dq_ref, dk_ref, dv_ref, dg_ref, db_ref), grads[:5]):
                for h in range(hg):
                    g_ref_out[rows, h * d:(h + 1) * d] = gr[h]
            dstate_scr[...] = grads[5]
            return carry

        lax.fori_loop(0, tile_chunks, step, 0)

    return pl.pallas_call(
        body, name=name,
        out_shape=[jax.ShapeDtypeStruct(q.shape, F32)] * 5,
        grid=(GDN_HEADS // hg, n_tiles), in_specs=[seq] * 5 + [st, inv, seq], out_specs=[seq] * 5,
        scratch_shapes=[pltpu.VMEM((hg, d, d), F32)],
        compiler_params=_params(("parallel", "arbitrary")),
    )(q, k, v, g, beta, states, inverses, do)


def make_gdn(name):
    @jax.custom_vjp
    def op(q, k, v, g, beta):
        return gdn_fwd(q, k, v, g, beta, name + "_f")[0]

    def fwd(q, k, v, g, beta):
        o, states, inverses = gdn_fwd(q, k, v, g, beta, name + "_f")
        return o, (q, k, v, g, beta, states, inverses)

    def bwd(saved, do):
        return tuple(gdn_bwd(*saved, do, name + "_b"))

    op.defvjp(fwd, bwd)
    return op


def loss_head(x, g, target, name):
    s, d = x.shape
    tm = min(256, s)

    def body(x_ref, g_ref, t_ref, loss_ref, dx_ref, dg_ref):
        tgt = t_ref[...]

        def f(xv, gv):
            err = _rms(xv, gv) - tgt
            per_row = jnp.mean(err * err, axis=-1, keepdims=True)
            return 0.5 * jnp.sum(per_row, axis=0, keepdims=True)

        val, vjp = jax.vjp(f, x_ref[...], g_ref[...])
        dx, dg = vjp(jnp.ones((1, 1), F32))
        dx_ref[...] = dx
        first = pl.program_id(0) == 0

        @pl.when(first)
        def _():
            dg_ref[...] = dg
            loss_ref[...] = jnp.broadcast_to(val, loss_ref.shape)

        @pl.when(jnp.logical_not(first))
        def _():
            dg_ref[...] += dg
            loss_ref[...] += jnp.broadcast_to(val, loss_ref.shape)

    row = pl.BlockSpec((tm, d), lambda i: (i, 0))
    vec = pl.BlockSpec((1, d), lambda i: (0, 0))
    return pl.pallas_call(
        body, name=name,
        out_shape=[jax.ShapeDtypeStruct((1, LANES), F32), jax.ShapeDtypeStruct((s, d), F32),
                   jax.ShapeDtypeStruct((1, d), F32)],
        grid=(s // tm,), in_specs=[row, vec, row],
        out_specs=[pl.BlockSpec((1, LANES), lambda i: (0, 0)), row, vec],
        compiler_params=_params(("arbitrary",)),
    )(x, g, target)


def adamw(g8, w, m, v, layer, prev, name):
    n_layers, rows, width = w.shape
    tr = _pick(rows, (256, 128, 64, 32, 16, 8))

    def body(g_ref, w_ref, m_ref, v_ref, *rest):
        go_ref, d_ref, mo_ref, vo_ref = rest[-4:]
        g = g_ref[0].astype(F32)
        for p in range(1, N_DEV):
            g = g + g_ref[p].astype(F32)
        m_new = ADAM_B1 * m_ref[...] + (1.0 - ADAM_B1) * g
        v_new = ADAM_B2 * v_ref[...] + (1.0 - ADAM_B2) * (g * g)
        m_hat = m_new / (1.0 - ADAM_B1 ** ADAM_STEP)
        v_hat = v_new / (1.0 - ADAM_B2 ** ADAM_STEP)
        go_ref[...] = g
        d_ref[...] = -ADAM_LR * (m_hat / (jnp.sqrt(v_hat) + ADAM_EPS) + ADAM_WD * w_ref[...])
        mo_ref[...] = m_new
        vo_ref[...] = v_new

    blk = pl.BlockSpec((None, tr, width), lambda i: (layer, i, 0))
    carried = list(prev) if prev is not None else []
    return pl.pallas_call(
        body, name=name, out_shape=[jax.ShapeDtypeStruct((n_layers, rows, width), F32)] * 4,
        grid=(rows // tr,),
        in_specs=[pl.BlockSpec((N_DEV, tr, width), lambda i: (0, i, 0)), blk, blk, blk]
        + [pl.BlockSpec(memory_space=pl.ANY)] * len(carried),
        out_specs=[blk] * 4,
        input_output_aliases={4 + j: j for j in range(len(carried))},
        compiler_params=_params(("parallel",)),
    )(g8, w, m, v, *carried)


_HBM = pl.BlockSpec(memory_space=pltpu.HBM)
_SEM = pl.BlockSpec(memory_space=pltpu.SEMAPHORE)
_EFFECT = pltpu.SideEffectType.DATAFLOW_SIDE_EFFECTING


def _exchange_copies(mode, src_refs, land_refs, send_sems, recv_sems, local_sems):
    x, y, c = lax.axis_index("x"), lax.axis_index("y"), lax.axis_index("c")
    me = 4 * x + 2 * y + c
    n = len(src_refs)

    def src(k, p):
        return src_refs[k] if mode == "gather" else src_refs[k].at[p]

    local = [pltpu.make_async_copy(src(k, me), land_refs[k].at[me], local_sems.at[k]) for k in range(n)]
    sends, recvs = [], []
    for k in range(n):
        for r in range(1, N_DEV):
            px = (1 - x) if r & 4 else x
            py = (1 - y) if r & 2 else y
            pc = (1 - c) if r & 1 else c
            p = 4 * px + 2 * py + pc
            sem = k * (N_DEV - 1) + r - 1
            sends.append(pltpu.make_async_remote_copy(
                src_ref=src(k, p), dst_ref=land_refs[k].at[me],
                send_sem=send_sems.at[sem], recv_sem=recv_sems.at[sem],
                device_id=(px, py, pc), device_id_type=pl.DeviceIdType.MESH))
            recvs.append(pltpu.make_async_remote_copy(
                src_ref=src(k, p), dst_ref=land_refs[k].at[p],
                send_sem=send_sems.at[sem], recv_sem=recv_sems.at[sem],
                device_id=(px, py, pc), device_id_type=pl.DeviceIdType.MESH))
    return local, sends, recvs


def exchange_start(mode, arrays, name, carry=()):
    n, nc = len(arrays), len(carry)
    land_shapes = [((N_DEV,) + tuple(a.shape)) if mode == "gather" else tuple(a.shape) for a in arrays]
    lands = [pltpu.with_memory_space_constraint(lax.empty(shp, a.dtype), pltpu.HBM)
             for shp, a in zip(land_shapes, arrays)]
    srcs = [pltpu.with_memory_space_constraint(a, pltpu.HBM) for a in arrays]
    carried = [pltpu.with_memory_space_constraint(a, pltpu.HBM) for a in carry]

    def body(*refs):
        src_refs, land_refs = refs[:n], refs[n:2 * n]
        first_out = 2 * n + nc
        send_sems, recv_sems, local_sems = refs[first_out:first_out + 3]
        token = refs[-1]
        local, sends, _ = _exchange_copies(mode, src_refs, land_refs, send_sems, recv_sems, local_sems)
        for cp in local + sends:
            cp.start()
        token[...] = jnp.zeros_like(token)

    n_sem = n * (N_DEV - 1)
    out = pl.pallas_call(
        body, name=name,
        out_shape=(pltpu.SemaphoreType.DMA((n_sem,)), pltpu.SemaphoreType.DMA((n_sem,)),
                   pltpu.SemaphoreType.DMA((n,)),
                   *[pltpu.HBM(a.shape, a.dtype) for a in arrays],
                   *[pltpu.HBM(shp, a.dtype) for shp, a in zip(land_shapes, arrays)],
                   *[pltpu.HBM(a.shape, a.dtype) for a in carry],
                   jax.ShapeDtypeStruct((8, LANES), F32)),
        in_specs=[_HBM] * (2 * n + nc),
        out_specs=(_SEM, _SEM, _SEM, *[_HBM] * (2 * n + nc), pl.BlockSpec(memory_space=pltpu.VMEM)),
        input_output_aliases={i: 3 + i for i in range(2 * n + nc)},
        compiler_params=pltpu.CompilerParams(has_side_effects=_EFFECT),
    )(*srcs, *lands, *carried)
    handle = dict(mode=mode, sems=out[:3], srcs=out[3:3 + n], lands=out[3 + n:3 + 2 * n])
    return handle, out[-1], list(out[3 + 2 * n:3 + 2 * n + nc])


def exchange_wait(handle, after, name):
    mode, srcs, lands = handle["mode"], list(handle["srcs"]), list(handle["lands"])
    n = len(srcs)

    def body(*refs):
        src_refs, land_refs = refs[:n], refs[n:2 * n]
        send_sems, recv_sems, local_sems = refs[2 * n:2 * n + 3]
        local, sends, recvs = _exchange_copies(mode, src_refs, land_refs, send_sems, recv_sems, local_sems)
        for cp in sends:
            cp.wait_send()
        for cp in recvs:
            cp.wait_recv()
        for cp in local:
            cp.wait()

    out = pl.pallas_call(
        body, name=name,
        out_shape=(*[pltpu.HBM(a.shape, a.dtype) for a in srcs], *[pltpu.HBM(a.shape, a.dtype) for a in lands]),
        in_specs=[_HBM] * (2 * n) + [_SEM] * 3 + [pl.BlockSpec(memory_space=pl.ANY)],
        out_specs=tuple([_HBM] * (2 * n)),
        input_output_aliases={i: i for i in range(2 * n)},
        compiler_params=pltpu.CompilerParams(has_side_effects=_EFFECT),
    )(*srcs, *lands, *handle["sems"], after)
    return list(out[n:])


_ICI_RELATIONS = (2, 4, 6)


def _mesh_place():
    x, y, c = lax.axis_index("x"), lax.axis_index("y"), lax.axis_index("c")

    def peer(r):
        px = (1 - x) if r & 4 else x
        py = (1 - y) if r & 2 else y
        pc = (1 - c) if r & 1 else c
        return (px, py, pc), 4 * px + 2 * py + pc

    return 4 * x + 2 * y + c, peer


def _remote(src, dst, send_sem, recv_sem, device):
    return pltpu.make_async_remote_copy(src_ref=src, dst_ref=dst, send_sem=send_sem, recv_sem=recv_sem,
                                        device_id=device, device_id_type=pl.DeviceIdType.MESH)


def gather2_start(groups, name):
    flat = [a for g in groups for a in g]
    n = len(flat)
    lands = [pltpu.with_memory_space_constraint(lax.empty((N_DEV,) + tuple(a.shape), a.dtype), pltpu.HBM) for a in flat]
    srcs = [pltpu.with_memory_space_constraint(a, pltpu.HBM) for a in flat]
    n_rel = 1 + len(_ICI_RELATIONS)

    def body(*refs):
        src_refs, land_refs = refs[:n], refs[n:2 * n]
        sem_refs = refs[2 * n:2 * n + 4 * len(groups)]
        me, peer = _mesh_place()
        k = 0
        for gi, g in enumerate(groups):
            send_sems, recv_sib, recv_ici, local_sems = sem_refs[4 * gi:4 * gi + 4]
            for j in range(len(g)):
                pltpu.make_async_copy(src_refs[k], land_refs[k].at[me], local_sems.at[j]).start()
                dev, _ = peer(1)
                _remote(src_refs[k], land_refs[k].at[me], send_sems.at[n_rel * j], recv_sib.at[j], dev).start()
                for t, r in enumerate(_ICI_RELATIONS):
                    dev, _ = peer(r)
                    _remote(src_refs[k], land_refs[k].at[me], send_sems.at[n_rel * j + 1 + t],
                            recv_ici.at[len(_ICI_RELATIONS) * j + t], dev).start()
                k += 1
        refs[-1][...] = jnp.zeros_like(refs[-1])

    sem_shapes = []
    for g in groups:
        sem_shapes += [pltpu.SemaphoreType.DMA((n_rel * len(g),)), pltpu.SemaphoreType.DMA((len(g),)),
                       pltpu.SemaphoreType.DMA((len(_ICI_RELATIONS) * len(g),)), pltpu.SemaphoreType.DMA((len(g),))]
    out = pl.pallas_call(
        body, name=name,
        out_shape=(*sem_shapes, *[pltpu.HBM(a.shape, a.dtype) for a in flat],
                   *[pltpu.HBM((N_DEV,) + tuple(a.shape), a.dtype) for a in flat],
                   jax.ShapeDtypeStruct((8, LANES), F32)),
        in_specs=[_HBM] * (2 * n),
        out_specs=(*[_SEM] * len(sem_shapes), *[_HBM] * (2 * n), pl.BlockSpec(memory_space=pltpu.VMEM)),
        input_output_aliases={i: len(sem_shapes) + i for i in range(2 * n)},
        compiler_params=pltpu.CompilerParams(has_side_effects=_EFFECT),
    )(*srcs, *lands)
    handles, k, base = [], 0, len(sem_shapes)
    for gi, g in enumerate(groups):
        handles.append(dict(sems=out[4 * gi:4 * gi + 4], srcs=out[base + k:base + k + len(g)],
                            lands=out[base + n + k:base + n + k + len(g)]))
        k += len(g)
    return handles, out[-1]


def gather2_forward(handle, after, name, carry=()):
    lands, nc = list(handle["lands"]), len(carry)
    n, n_ici = len(lands), len(_ICI_RELATIONS)
    carried = [pltpu.with_memory_space_constraint(a, pltpu.HBM) for a in carry]

    def body(*refs):
        land_refs = refs[:n]
        recv_ici = refs[n + nc]
        fwd_send, fwd_recv = refs[n + nc + 2], refs[n + nc + 3]
        me, peer = _mesh_place()
        sibling, _ = peer(1)
        for j in range(n):
            for t, r in enumerate(_ICI_RELATIONS):
                dev, p = peer(r)
                landed = land_refs[j].at[p]
                _remote(landed, landed, fwd_send.at[n_ici * j + t], recv_ici.at[n_ici * j + t], dev).wait_recv()
                _remote(landed, landed, fwd_send.at[n_ici * j + t], fwd_recv.at[n_ici * j + t], sibling).start()

    out = pl.pallas_call(
        body, name=name,
        out_shape=(pltpu.SemaphoreType.DMA((n_ici * n,)), pltpu.SemaphoreType.DMA((n_ici * n,)),
                   *[pltpu.HBM(a.shape, a.dtype) for a in lands], *[pltpu.HBM(a.shape, a.dtype) for a in carry]),
        in_specs=[_HBM] * (n + nc) + [_SEM, pl.BlockSpec(memory_space=pl.ANY)],
        out_specs=(_SEM, _SEM, *[_HBM] * (n + nc)),
        input_output_aliases={i: 2 + i for i in range(n + nc)},
        compiler_params=pltpu.CompilerParams(has_side_effects=_EFFECT),
    )(*lands, *carried, handle["sems"][2], after)
    new_handle = dict(sems=handle["sems"], srcs=handle["srcs"], lands=out[2:2 + n], fwd=out[:2])
    return new_handle, list(out[2 + n:])


def gather2_wait(handle, after, name):
    srcs, lands = list(handle["srcs"]), list(handle["lands"])
    n, n_ici = len(srcs), len(_ICI_RELATIONS)
    n_rel = 1 + n_ici
    send_all, recv_sibling, _, local_all = handle["sems"]

    def body(*refs):
        src_refs, land_refs = refs[:n], refs[n:2 * n]
        send_sems, recv_sib, local_sems, fwd_send, fwd_recv = refs[2 * n:2 * n + 5]
        me, peer = _mesh_place()
        sibling, sib = peer(1)
        for j in range(n):
            pltpu.make_async_copy(src_refs[j], land_refs[j].at[me], local_sems.at[j]).wait()
            _remote(src_refs[j], land_refs[j].at[sib], send_sems.at[n_rel * j], recv_sib.at[j], sibling).wait()
            for t, r in enumerate(_ICI_RELATIONS):
                dev, p = peer(r)
                _remote(src_refs[j], land_refs[j].at[me], send_sems.at[n_rel * j + 1 + t],
                        recv_sib.at[j], dev).wait_send()
                _, p_sib = peer(r ^ 1)
                _remote(land_refs[j].at[p], land_refs[j].at[p_sib], fwd_send.at[n_ici * j + t],
                        fwd_recv.at[n_ici * j + t], sibling).wait()

    out = pl.pallas_call(
        body, name=name,
        out_shape=(*[pltpu.HBM(a.shape, a.dtype) for a in srcs], *[pltpu.HBM(a.shape, a.dtype) for a in lands]),
        in_specs=[_HBM] * (2 * n) + [_SEM] * 5 + [pl.BlockSpec(memory_space=pl.ANY)],
        out_specs=tuple([_HBM] * (2 * n)),
        input_output_aliases={i: i for i in range(2 * n)},
        compiler_params=pltpu.CompilerParams(has_side_effects=_EFFECT),
    )(*srcs, *lands, send_all, recv_sibling, local_all, *handle["fwd"], after)
    return list(out[n:])


BIG = ["mla_w_in", "mla_w_uq", "mla_w_ukv", "mla_w_o", "gdn_w_in", "gdn_w_o", "sc_w_in", "sc_w_o",
       "xa_w_q", "xa_w_kv", "xa_w_o", "mlp_w1", "mlp_w2"]
TINY = [("mla_q_norm", 1), ("mla_kv_norm", 1), ("gdn_conv_w", 2), ("sc_conv_w", 2)]
REPL = ["gdn_a_log", "gdn_dt_bias", "gdn_o_norm", "norm_mix", "norm_mem", "norm_mlp", "mem_norm", "final_norm"]
WEIGHTS = ["mla_w_in", "mla_q_norm", "mla_kv_norm", "mla_w_uq", "mla_w_ukv", "mla_w_o", "gdn_w_in",
           "gdn_conv_w", "gdn_a_log", "gdn_dt_bias", "gdn_o_norm", "gdn_w_o", "sc_w_in", "sc_conv_w",
           "sc_w_o", "norm_mix", "norm_mem", "norm_mlp", "xa_w_q", "xa_w_kv", "xa_w_o", "mlp_w1",
           "mlp_w2", "mem_norm", "final_norm"]
MIXER_WEIGHTS = (["mla_w_in", "mla_w_uq", "mla_w_ukv", "mla_w_o"], ["gdn_w_in", "gdn_w_o"], ["sc_w_in", "sc_w_o"])
MIXER_PARAMS = (["norm_mem", "mla_q_norm", "mla_kv_norm"],
                ["norm_mem", "gdn_conv_w", "gdn_a_log", "gdn_dt_bias", "gdn_o_norm"],
                ["norm_mem", "sc_conv_w"])


def from_shards(a8, axis):
    a = jnp.moveaxis(a8, 0, axis)
    shp = a.shape
    return a.reshape(shp[:axis] + (shp[axis] * shp[axis + 1],) + shp[axis + 2:])


def pack_rows(flat_list, width, row_mult):
    total = sum(a.shape[-1] for a in flat_list)
    rows = -(-total // width)
    rows = -(-rows // row_mult) * row_mult
    pad = rows * width - total
    parts = list(flat_list)
    if pad:
        parts.append(jnp.zeros((pad,), flat_list[0].dtype))
    return jnp.concatenate(parts, axis=-1).reshape(rows, width)


def unpack_rows(packed, shapes):
    lead = packed.shape[:-2]
    flat = packed.reshape(lead + (-1,))
    out, off = [], 0
    for shp in shapes:
        n = math.prod(shp)
        out.append(flat[..., off:off + n].reshape(lead + tuple(shp)))
        off += n
    return out


def _swap_halves(w):
    half = w.shape[-1] // 2
    return jnp.concatenate([w[..., half:], w[..., :half]], axis=-1)


def _pad_last(w, n):
    return jnp.pad(w, [(0, 0)] * (w.ndim - 1) + [(0, n - w.shape[-1])])


def _unblock(w8):
    return jnp.transpose(w8, (1, 0, 2)).reshape(w8.shape[1], -1)


def _stack_rows(w8):
    return w8.reshape(-1, w8.shape[-1])


def rms_op(name, rows, d, out_dtype):
    tm = rows if rows * d * 4 <= BLOCK_BYTES else 512
    return make_tile_op(fn_rms, name, ["row", "par"], [True, True], [("row", d, out_dtype)], rows, min(tm, rows))


def seg_memory(p, mem):
    return rms_op("rms_memory", mem.shape[0], mem.shape[1], BF16)(mem, p["mem_norm"].reshape(1, -1))[0]


def seg_mixer(i, wts, p, x, h, rope_c, rope_s):
    s, d = x.shape
    j, kind = i // N_MIXERS, i % N_MIXERS
    tag = f"l{i}"
    hd = MLA_NOPE
    next_gain = p["norm_mem"][i].reshape(1, d)
    if kind == 0:
        w_in = _stack_rows(wts["mla_w_in"])
        w_cq = w_in[:, :MLA_Q_RANK]
        w_ckv = w_in[:, MLA_Q_RANK:MLA_Q_RANK + MLA_KV_RANK]
        w_kr = w_in[:, MLA_Q_RANK + MLA_KV_RANK:]
        c_qn, c_kvn, k_rope = make_tile_op(
            fn_mla_down, tag + "_mla_down", ["row", "row", "row"] + ["par"] * 6, [True, False, False] + [True] * 6,
            [("row", MLA_Q_RANK, BF16), ("row", MLA_KV_RANK, BF16), ("row", hd, F32)], s, 256)(
            h, rope_c, rope_s, w_cq, w_ckv, _pad_last(w_kr, hd), _pad_last(_swap_halves(w_kr), hd),
            p["mla_q_norm"][j].reshape(1, -1), p["mla_kv_norm"][j].reshape(1, -1))
        w_uq8 = wts["mla_w_uq"]
        w_qn = _unblock(w_uq8[:, :, :MLA_NOPE])
        w_qr = w_uq8[:, :, MLA_NOPE:]
        w_qr_p = _unblock(_pad_last(w_qr, hd))
        w_qr_s = _unblock(_pad_last(_swap_halves(w_qr), hd))
        nq = MLA_HEADS * hd
        q_nope, q_rope, kv = make_tile_op(
            fn_mla_up, tag + "_mla_up", ["row", "row", "row", "row"] + ["par"] * 4, [True, True, False, False] + [True] * 4,
            [("row", nq, BF16), ("row", nq, F32), ("row", 2 * nq, BF16)], s, 256)(
            c_qn, c_kvn, rope_c, rope_s, w_qn, w_qr_p, w_qr_s, _unblock(wts["mla_w_ukv"]))
        n_groups = MLA_QUERY_GROUPS if s % (MLA_QUERY_GROUPS * 256) == 0 else 1
        rows_g = s // n_groups
        o_groups = []
        for grp in range(n_groups):
            r0, r1 = grp * rows_g, (grp + 1) * rows_g
            o_groups.append(make_tile_op(
                fn_mla_attn, f"{tag}_mla_attn{grp}", [("rowh", hd), ("rowh", hd), ("parh", hd, 2), "par"],
                [True] * 4, [(("rowh", hd), nq, BF16)], rows_g, 256, MLA_HEADS, row_base=r0)(
                q_nope[r0:r1], q_rope[r0:r1], kv[:r1], k_rope[:r1])[0])
        o = jnp.concatenate(o_groups, axis=0)
        return make_mm_res_rms(tag + "_mla_o")(x, o, _stack_rows(wts["mla_w_o"]), next_gain)
    if kind == 1:
        ng = GDN_HEADS * GDN_DK
        w_in = _unblock(wts["gdn_w_in"])
        cw = p["gdn_conv_w"][j]
        conv_out = []
        for part, nm in enumerate(("q", "k", "v")):
            cols = slice(part * ng, (part + 1) * ng)
            pre = make_mm(f"{tag}_gdn_in_{nm}", F32)(h, w_in[:, cols])
            conv_out.append(make_conv(f"{tag}_gdn_conv_{nm}")(pre, cw[:, cols]))
        gate = make_mm(tag + "_gdn_in_g", F32)(h, w_in[:, 3 * ng:4 * ng])
        ba = make_mm(tag + "_gdn_in_ba", F32)(h, _pad_last(w_in[:, 4 * ng:], LANES))
        heads_row = ("row", GDN_DK, GDN_HEADS)
        q, k, v = make_tile_op(fn_gdn_prep, tag + "_gdn_prep", [heads_row] * 3, [True] * 3,
                               [(heads_row, ng, F32)] * 3, s, 512)(*conv_out)
        alog = jnp.pad(p["gdn_a_log"][j].reshape(1, -1), ((0, 0), (GDN_HEADS, LANES - 2 * GDN_HEADS)))
        dtb = jnp.pad(p["gdn_dt_bias"][j].reshape(1, -1), ((0, 0), (GDN_HEADS, LANES - 2 * GDN_HEADS)))
        beta_b, g_b = make_tile_op(fn_gdn_gates, tag + "_gdn_gates", ["row", "par", "par"], [True] * 3,
                                   [("row", ng, F32)] * 2, s, 512)(ba, alog, dtb)
        o = make_gdn(tag + "_gdn_core")(q, k, v, g_b, beta_b)
        o = make_tile_op(fn_gdn_out, tag + "_gdn_out", [heads_row, heads_row, "par"],
                         [True] * 3, [(heads_row, ng, BF16)], s, 512)(
            o, gate, p["gdn_o_norm"][j].reshape(1, -1))[0]
        return make_mm_res_rms(tag + "_gdn_o")(x, o, _stack_rows(wts["gdn_w_o"]), next_gain)
    w_in = _unblock(wts["sc_w_in"])
    b_gate = make_mm(tag + "_sc_in_b", F32)(h, w_in[:, :d])
    c_gate = make_mm(tag + "_sc_in_c", F32)(h, w_in[:, d:2 * d])
    u = make_mm(tag + "_sc_in_u", F32)(h, w_in[:, 2 * d:])
    yv = make_gated_conv(tag + "_sc_conv")(b_gate, c_gate, u, p["sc_conv_w"][j])
    return make_mm_res_rms(tag + "_sc_o")(x, yv, _stack_rows(wts["sc_w_o"]), next_gain)


def seg_xattn(i, wts, p, x, hx, mem_n):
    s, d = x.shape
    tag = f"l{i}"
    kv = make_mm(tag + "_xa_kv", BF16, blocked=True)(mem_n, wts["xa_w_kv"])
    heads = ("row", X_HEAD_DIM, X_HEADS)
    o = make_tile_op(fn_xattn, tag + "_xattn",
                     ["row", ("par", X_HEAD_DIM, X_HEADS), ("par", X_HEAD_DIM, 2 * X_HEADS)], [True] * 3,
                     [(heads, d, BF16)], s, 512)(hx, _stack_rows(wts["xa_w_q"]), kv)[0]
    return make_mm_res_rms(tag + "_xa_o")(x, o, _stack_rows(wts["xa_w_o"]), p["norm_mlp"][i].reshape(1, d))


def seg_mlp(i, wts, p, x, hm):
    d = x.shape[1]
    gain = p["norm_mix"][i + 1].reshape(1, d) if i + 1 < DEPTH else None
    return make_mlp(f"l{i}_mlp", gain is not None)(x, hm, wts["mlp_w1"], _stack_rows(wts["mlp_w2"]), gain)


def segments():
    segs = []
    for i in range(DEPTH):
        j, kind = i // N_MIXERS, i % N_MIXERS
        segs.append((f"l{i}_mixer", [(n, j) for n in MIXER_WEIGHTS[kind]], MIXER_PARAMS[kind], "mixer"))
        segs.append((f"l{i}_xattn", [(n, i) for n in ("xa_w_q", "xa_w_kv", "xa_w_o")], ["norm_mlp"], "xattn"))
        segs.append((f"l{i}_mlp", [(n, i) for n in ("mlp_w1", "mlp_w2")], ["norm_mix"] if i + 1 < DEPTH else [],
                     "mlp"))
    return segs


def run_segment(index, kind, wts, p, x, h, mem_n, rope_c, rope_s):
    layer = index // 3
    if kind == "mixer":
        return seg_mixer(layer, wts, p, x, h, rope_c, rope_s)
    if kind == "xattn":
        return seg_xattn(layer, wts, p, x, h, mem_n)
    return seg_mlp(layer, wts, p, x, h)


def rope_tables(positions):
    inv_freq = ROPE_THETA ** (-jnp.arange(0, MLA_ROPE, 2, dtype=F32) / MLA_ROPE)
    ang = positions.astype(F32)[:, None] * inv_freq
    cos, sin = jnp.cos(ang), jnp.sin(ang)
    zeros = jnp.zeros((positions.shape[0], MLA_NOPE - MLA_ROPE), F32)
    return jnp.concatenate([cos, cos, zeros], axis=-1), jnp.concatenate([-sin, sin, zeros], axis=-1)


def kernel(x, mem, positions, mla_w_in, mla_q_norm, mla_kv_norm, mla_w_uq, mla_w_ukv, mla_w_o, gdn_w_in, gdn_conv_w, gdn_a_log, gdn_dt_bias, gdn_o_norm, gdn_w_o, sc_w_in, sc_conv_w, sc_w_o, norm_mix, norm_mem, norm_mlp, xa_w_q, xa_w_kv, xa_w_o, mlp_w1, mlp_w2, mem_norm, final_norm, loss_target, m_mla_w_in, m_mla_q_norm, m_mla_kv_norm, m_mla_w_uq, m_mla_w_ukv, m_mla_w_o, m_gdn_w_in, m_gdn_conv_w, m_gdn_a_log, m_gdn_dt_bias, m_gdn_o_norm, m_gdn_w_o, m_sc_w_in, m_sc_conv_w, m_sc_w_o, m_norm_mix, m_norm_mem, m_norm_mlp, m_xa_w_q, m_xa_w_kv, m_xa_w_o, m_mlp_w1, m_mlp_w2, m_mem_norm, m_final_norm, v_mla_w_in, v_mla_q_norm, v_mla_kv_norm, v_mla_w_uq, v_mla_w_ukv, v_mla_w_o, v_gdn_w_in, v_gdn_conv_w, v_gdn_a_log, v_gdn_dt_bias, v_gdn_o_norm, v_gdn_w_o, v_sc_w_in, v_sc_conv_w, v_sc_w_o, v_norm_mix, v_norm_mem, v_norm_mlp, v_xa_w_q, v_xa_w_kv, v_xa_w_o, v_mlp_w1, v_mlp_w2, v_mem_norm, v_final_norm):
    args = locals()
    w_loc = {n: args[n] for n in WEIGHTS}
    m_loc = {n: args["m_" + n] for n in WEIGHTS}
    v_loc = {n: args["v_" + n] for n in WEIGHTS}
    me = 4 * lax.axis_index("x") + 2 * lax.axis_index("y") + lax.axis_index("c")
    segs = segments()

    w16 = {n: w_loc[n].astype(BF16) for n in BIG}
    tiny_pack = pack_rows([w_loc[n].reshape(-1) for n, _ in TINY], LANES, 8)
    gather_handles, token = gather2_start(
        [[tiny_pack]] + [[w16[n][layer] for n, layer in units] for _, units, _, _ in segs], "gather_start")

    x_cur = x[0]
    rope_c, rope_s = rope_tables(positions[0])
    tiny_handle, _ = gather2_forward(gather_handles[0], token, "gather_forward_tiny")
    tiny_all = gather2_wait(tiny_handle, token, "gather_wait_tiny")[0]
    gather_handles = gather_handles[1:]
    params = {}
    for (n, ax), a8 in zip(TINY, unpack_rows(tiny_all, [w_loc[n].shape for n, _ in TINY])):
        params[n] = from_shards(a8, ax)
    for n in REPL:
        params[n] = w_loc[n]

    mem_n, vjp_memory = jax.vjp(lambda p_: seg_memory(p_, mem[0]), {"mem_norm": params["mem_norm"]})
    h_cur, vjp_first_norm = jax.vjp(
        lambda p_, x_: rms_op("l0_rms_mix", x_.shape[0], x_.shape[1], BF16)(x_, p_["norm_mix"][0].reshape(1, -1))[0],
        {"norm_mix": params["norm_mix"]}, x_cur)
    vjps = []
    forwarded, _ = gather2_forward(gather_handles[0], token, f"gather_forward_{segs[0][0]}")
    for index, (tag, units, p_names, kind) in enumerate(segs):
        landed = gather2_wait(forwarded, token if index == 0 else x_cur, f"gather_wait_{tag}")
        wts = {n: a for (n, _), a in zip(units, landed)}
        p_seg = {n: params[n] for n in p_names}
        if index + 1 < len(segs):
            forwarded, (p_seg[p_names[0]],) = gather2_forward(
                gather_handles[index + 1], landed[0], f"gather_forward_{segs[index + 1][0]}",
                carry=[p_seg[p_names[0]]])
        outs, vjp_seg = jax.vjp(
            lambda w_, p_, x_, h_, m_, index=index, kind=kind:
            run_segment(index, kind, w_, p_, x_, h_, m_, rope_c, rope_s),
            wts, p_seg, x_cur, h_cur, mem_n)
        x_cur, h_cur = outs[0], (outs[1] if len(outs) > 1 else None)
        vjps.append(vjp_seg)

    loss_vec, g_x, d_final = loss_head(x_cur, params["final_norm"].reshape(1, -1), loss_target[0], "loss_head")

    grads = {n: jnp.zeros_like(params[n]) for n in params}
    grads["final_norm"] = d_final.reshape(-1)
    g_mem_n = jnp.zeros_like(mem_n)
    g_h = None
    scatter_handles = []
    for (tag, units, _, _), vjp_seg in zip(reversed(segs), reversed(vjps)):
        g_wts, g_p, g_x, g_h, g_m = vjp_seg((g_x,) if g_h is None else (g_x, g_h))
        for n, g in g_p.items():
            grads[n] = grads[n] + g
        g_mem_n = g_mem_n + g_m
        handle, _, (g_h,) = exchange_start("scatter", [g_wts[n] for n, _ in units], f"scatter_start_{tag}",
                                           carry=[g_h])
        scatter_handles.append((units, handle))
    grads["mem_norm"] = grads["mem_norm"] + vjp_memory(g_mem_n)[0]["mem_norm"]
    g_first, g_x_norm = vjp_first_norm(g_h)
    grads["norm_mix"] = grads["norm_mix"] + g_first["norm_mix"]
    g_x = g_x + g_x_norm

    small_names = [n for n, _ in TINY] + REPL
    small_g = pack_rows([loss_vec[0, :1]] + [grads[n].astype(F32).reshape(-1) for n in small_names], PACK_W, 8)
    small_handle, _, _ = exchange_start("gather", [small_g], "gather_start_small_grads")

    g_recv = {}
    for units, handle in scatter_handles:
        landed = exchange_wait(handle, g_x, f"scatter_wait_{units[0][0]}_{units[0][1]}")
        g_recv.update(dict(zip(units, landed)))

    res = {}
    for n in BIG:
        outs = None
        for layer in range(w_loc[n].shape[0]):
            outs = adamw(g_recv[n, layer], w_loc[n], m_loc[n], v_loc[n], layer, outs, f"adamw_{n}_{layer}")
        for kind, a in zip(("grad", "delta", "m", "v"), outs):
            res[(kind, n)] = a
    small_recv = exchange_wait(small_handle, res[("grad", BIG[-1])], "gather_wait_small_grads")[0]

    def full_small(d):
        parts = [jnp.zeros((1,), F32)]
        for n, ax in TINY:
            full_shape = params[n].shape
            start = [0] * len(full_shape)
            start[ax] = me * d[n].shape[ax]
            parts.append(lax.dynamic_update_slice(jnp.zeros(full_shape, F32), d[n], start).reshape(-1))
        parts += [d[n].reshape(-1) for n in REPL]
        return pack_rows(parts, PACK_W, 8)

    outs_small = adamw(small_recv, full_small(w_loc)[None], full_small(m_loc)[None], full_small(v_loc)[None],
                       0, None, "adamw_small")
    small_shapes = [(1,)] + [params[n].shape for n, _ in TINY] + [w_loc[n].shape for n in REPL]
    loss = None
    for kind, packed in zip(("grad", "delta", "m", "v"), outs_small):
        parts = unpack_rows(packed[0], small_shapes)
        if kind == "grad":
            loss = parts[0][0]
        for (n, ax), a in zip(TINY, parts[1:1 + len(TINY)]):
            start = [0] * a.ndim
            start[ax] = me * w_loc[n].shape[ax]
            res[(kind, n)] = lax.dynamic_slice(a, start, w_loc[n].shape)
        for n, a in zip(REPL, parts[1 + len(TINY):]):
            res[(kind, n)] = a

    out = [loss, g_x[None]]
    for kind in ("grad", "delta", "m", "v"):
        out += [res[(kind, n)] for n in WEIGHTS]
    return tuple(out)
```

```python
import math

import jax
import jax.numpy as jnp
from jax import lax
from jax.experimental import pallas as pl
from jax.experimental.pallas import tpu as pltpu

F32 = jnp.float32
BF16 = jnp.bfloat16

N_DEV = 8
LANES = 128
EPS = 1e-6
ROPE_THETA = 10000.0
MLA_HEADS, MLA_NOPE, MLA_ROPE, MLA_V = 8, 128, 64, 128
MLA_Q_RANK, MLA_KV_RANK = 384, 256
GDN_HEADS, GDN_DK, GDN_CONV, GDN_CHUNK = 8, 128, 4, 64
X_HEADS, X_HEAD_DIM = 4, 256
DEPTH, N_MIXERS = 4, 3
ADAM_LR, ADAM_B1, ADAM_B2, ADAM_EPS, ADAM_WD, ADAM_STEP = 0.001, 0.9, 0.999, 1e-08, 0.01, 10
MLA_QUERY_GROUPS = 4
NEG_BIG = -1e30
PACK_W = 1024


_NN = (((1,), (0,)), ((), ()))
_NT = (((1,), (1,)), ((), ()))
_TN = (((0,), (0,)), ((), ()))
_NN3 = (((2,), (1,)), ((0,), (0,)))
_NT3 = (((2,), (2,)), ((0,), (0,)))
_TN3 = (((1,), (1,)), ((0,), (0,)))


def _dot(a, b, dims):
    return lax.dot_general(a, b, dims, preferred_element_type=F32)


def _hi_lo(x):
    hi = x.astype(BF16)
    return hi, (x - hi.astype(F32)).astype(BF16)


def _split3(x):
    hi = x.astype(BF16)
    r = x - hi.astype(F32)
    mid = r.astype(BF16)
    return hi, mid, (r - mid.astype(F32)).astype(BF16)


def _dg(a, b, dims, prec):
    if prec == "h":
        return lax.dot_general(a, b, dims, precision=lax.Precision.HIGHEST, preferred_element_type=F32)
    if prec == "m":
        a_hi, a_lo = _hi_lo(a)
        b_hi, b_lo = _hi_lo(b)
        return _dot(a_hi, b_hi, dims) + _dot(a_hi, b_lo, dims) + _dot(a_lo, b_hi, dims)
    return _dot(a.astype(BF16), b.astype(BF16), dims)


def _dg_sel(sel, x, dims, sel_first):
    s16 = sel.astype(BF16)
    parts = [(_dot(s16, piece, dims) if sel_first else _dot(piece, s16, dims)) for piece in _split3(x)]
    return parts[0] + parts[1] + parts[2]


class _Ops:
    def __init__(self, prec, differentiable, batched=False):
        d_nn, d_nt, d_tn = (_NN3, _NT3, _TN3) if batched else (_NN, _NT, _TN)

        def nn(a, b):
            return _dg(a, b, d_nn, prec)

        def nt(a, b):
            return _dg(a, b, d_nt, prec)

        def tn(a, b):
            return _dg(a, b, d_tn, prec)

        if differentiable:
            dnn = jax.custom_vjp(nn)
            dnn.defvjp(lambda a, b: (nn(a, b), (a, b)), lambda r, g: (nt(g, r[1]), tn(r[0], g)))
            dnt = jax.custom_vjp(nt)
            dnt.defvjp(lambda a, b: (nt(a, b), (a, b)), lambda r, g: (nn(g, r[1]), tn(g, r[0])))
            dtn = jax.custom_vjp(tn)
            dtn.defvjp(lambda a, b: (tn(a, b), (a, b)), lambda r, g: (nt(r[1], g), nn(r[0], g)))
            nn, nt, tn = dnn, dnt, dtn
        self.nn, self.nt, self.tn = nn, nt, tn


class _SelOps:
    def __init__(self, differentiable, batched=False):
        d_nn, d_nt, d_tn = (_NN3, _NT3, _TN3) if batched else (_NN, _NT, _TN)

        def sel_nn(sel, x):
            return _dg_sel(sel, x, d_nn, True)

        def sel_nt(sel, x):
            return _dg_sel(sel, x, d_nt, True)

        if differentiable:
            dnn = jax.custom_vjp(sel_nn)
            dnn.defvjp(lambda s, x: (sel_nn(s, x), s),
                       lambda s, g: (jnp.zeros_like(s), _dg_sel(s, g, d_tn, True)))
            dnt = jax.custom_vjp(sel_nt)
            dnt.defvjp(lambda s, x: (sel_nt(s, x), s),
                       lambda s, g: (jnp.zeros_like(s), _dg_sel(s, g, d_tn, False)))
            sel_nn, sel_nt = dnn, dnt
        self.sel_nn, self.sel_nt = sel_nn, sel_nt


class _OpSet:
    def __init__(self, differentiable):
        self.b = _Ops("b", differentiable)
        self.h = _Ops("h", differentiable)
        self.bb = _Ops("b", differentiable, batched=True)
        self.bm = _Ops("m", differentiable, batched=True)
        self.bs = _SelOps(differentiable, batched=True)


_PLAIN = _OpSet(False)
_DIFF = _OpSet(True)


def _params(sem):
    return pltpu.CompilerParams(dimension_semantics=sem)


BLOCK_BYTES = 4 * 1024 * 1024


def _pick(n, cands):
    for c in cands:
        if n % c == 0:
            return c
    return n


def _tile(n, cap):
    if n <= cap:
        return n
    return _pick(n, tuple(c for c in (2048, 1024, 768, 512, 384, 256, 128) if c <= cap))


def matmul(a, b, form, out_dtype, name, res=None, blocked=False, relu_gate=None, rms_gain=None, a_relu2=False):
    if form == "nn":
        m, k = a.shape
        k2, n = (b.shape[1], N_DEV * b.shape[2]) if blocked else b.shape
    elif form == "nt":
        m, k = a.shape
        n, k2 = (b.shape[1], N_DEV * b.shape[2]) if blocked else b.shape
    else:
        (k, m), (k2, n) = a.shape, b.shape
    assert k == k2, (a.shape, b.shape, form)
    tk = k if k <= 2048 else _tile(k, 1024)
    cb = nb = 1
    if blocked:
        cb = (k if form == "nt" else n) // N_DEV
        nb = _pick(N_DEV, tuple(c for c in (8, 4, 2, 1) if c * cb <= 1024))
    if blocked and form == "nt":
        tk = nb * cb
    if blocked and form != "nt":
        tn = nb * cb
    else:
        tn = _tile(n, min(1024, BLOCK_BYTES // (tk * b.dtype.itemsize)))
    out_elems = BLOCK_BYTES // 2 if (out_dtype == BF16 and res is None) else BLOCK_BYTES // 4
    tm = _tile(m, min(BLOCK_BYTES // (tk * a.dtype.itemsize), out_elems // tn))
    nk = k // tk
    dims = {"nn": _NN, "nt": _NT, "tn": _TN}[form]

    a_spec = {"nn": pl.BlockSpec((tm, tk), lambda i, j, kk: (i, kk)),
              "nt": pl.BlockSpec((tm, tk), lambda i, j, kk: (i, kk)),
              "tn": pl.BlockSpec((tk, tm), lambda i, j, kk: (kk, i))}[form]
    if blocked and form == "nn":
        b_spec = pl.BlockSpec((nb, tk, cb), lambda i, j, kk: (j, kk, 0))
    elif blocked and form == "nt":
        b_spec = pl.BlockSpec((nb, tn, cb), lambda i, j, kk: (kk, j, 0))
    else:
        b_spec = {"nn": pl.BlockSpec((tk, tn), lambda i, j, kk: (kk, j)),
                  "nt": pl.BlockSpec((tn, tk), lambda i, j, kk: (j, kk)),
                  "tn": pl.BlockSpec((tk, tn), lambda i, j, kk: (kk, j))}[form]
    c_spec = pl.BlockSpec((tm, tn), lambda i, j, kk: (i, j))
    out_shape = jax.ShapeDtypeStruct((m, n), out_dtype)
    o_spec = c_spec
    blocked_out = blocked and form == "tn"
    if blocked_out:
        out_shape = jax.ShapeDtypeStruct((N_DEV, m, cb), out_dtype)
        o_spec = pl.BlockSpec((nb, tm, cb), lambda i, j, kk: (j, i, 0))
    has_res, has_gate, has_gain = res is not None, relu_gate is not None, rms_gain is not None
    extras = [e for e in (res, relu_gate) if e is not None]
    n_in = 2 + len(extras) + has_gain
    second = has_gain
    assert not (second and (blocked_out or tn != n))

    def body(*refs):
        a_ref, b_ref = refs[0], refs[1]
        r_ref = refs[2] if has_res else None
        gate_ref = refs[2 + has_res] if has_gate else None
        gain_ref = refs[n_in - 1] if has_gain else None
        o_ref = refs[n_in]
        if a_relu2:
            relu = jnp.maximum(a_ref[...].astype(F32), 0.0)
            a_val = (relu * relu).astype(BF16)
        else:
            a_val = a_ref[...].astype(BF16)
        if blocked and form == "nn":
            part = jnp.concatenate([_dot(a_val, b_ref[t].astype(BF16), dims) for t in range(nb)], axis=-1)
        elif blocked and form == "nt":
            part = _dot(a_val[:, :cb], b_ref[0].astype(BF16), dims)
            for t in range(1, nb):
                part = part + _dot(a_val[:, t * cb:(t + 1) * cb], b_ref[t].astype(BF16), dims)
        else:
            part = _dot(a_val, b_ref[...].astype(BF16), dims)

        def finish(acc):
            if has_res:
                acc = acc + r_ref[...].astype(F32)
            if has_gate:
                acc = acc * (2.0 * jnp.maximum(gate_ref[...].astype(F32), 0.0))
            if blocked_out:
                for t in range(nb):
                    o_ref[t] = acc[:, t * cb:(t + 1) * cb].astype(out_dtype)
            else:
                o_ref[...] = acc.astype(out_dtype)
            if has_gain:
                refs[n_in + 1][...] = _rms(acc, gain_ref[...]).astype(BF16)

        if nk == 1:
            finish(part)
        else:
            acc_ref = refs[-1]
            kk = pl.program_id(2)

            @pl.when(kk == 0)
            def _():
                acc_ref[...] = part

            @pl.when(jnp.logical_and(kk > 0, kk < nk - 1))
            def _():
                acc_ref[...] += part

            @pl.when(kk == nk - 1)
            def _():
                finish(acc_ref[...] + part)

    in_specs = [a_spec, b_spec] + [c_spec] * len(extras)
    args = [a, b] + extras
    if has_gain:
        in_specs.append(pl.BlockSpec((1, tn), lambda i, j, kk: (0, j)))
        args.append(rms_gain)
    if second:
        out_shape = [out_shape, jax.ShapeDtypeStruct((m, n), BF16)]
        o_spec = [o_spec, c_spec]
    return pl.pallas_call(
        body, name=name,
        out_shape=out_shape,
        grid=(m // tm, n // tn, nk),
        in_specs=in_specs, out_specs=o_spec,
        scratch_shapes=[pltpu.VMEM((tm, tn), F32)] if nk > 1 else [],
        compiler_params=_params(("parallel", "parallel", "arbitrary")),
    )(*args)


def make_mm(name, out_dtype, with_res=False, blocked=False):
    def bwd_mm(a, w, g):
        da = matmul(g, w, "nt", a.dtype, name + "_da", blocked=blocked)
        dw = matmul(a, g, "tn", w.dtype, name + "_dw", blocked=blocked)
        return da, dw

    if with_res:
        @jax.custom_vjp
        def op(res, a, w):
            return matmul(a, w, "nn", out_dtype, name + "_f", res=res, blocked=blocked)

        def fwd(res, a, w):
            return op(res, a, w), (a, w)

        def bwd(saved, g):
            return (g,) + bwd_mm(*saved, g)
    else:
        @jax.custom_vjp
        def op(a, w):
            return matmul(a, w, "nn", out_dtype, name + "_f", blocked=blocked)

        def fwd(a, w):
            return op(a, w), (a, w)

        def bwd(saved, g):
            return bwd_mm(*saved, g)
    op.defvjp(fwd, bwd)
    return op


def _rms_fan_bwd(x_new, gain, dx, dh, name):
    rows, d = x_new.shape
    outs = [("row", d, F32), ("row", d, BF16)]
    return tile_bwd(fn_fan_rms, name, ["row", "par"], [x_new, gain], [True, True], outs, [dx, dh],
                    rows, min(512, rows), 0)


def make_mm_res_rms(name):
    @jax.custom_vjp
    def op(res, a, w, gain):
        return tuple(matmul(a, w, "nn", F32, name + "_f", res=res, rms_gain=gain))

    def fwd(res, a, w, gain):
        x_new, h = op(res, a, w, gain)
        return (x_new, h), (a, w, x_new, gain)

    def bwd(saved, cts):
        a, w, x_new, gain = saved
        dx, dgain = _rms_fan_bwd(x_new, gain, cts[0], cts[1], name + "_nb")
        da = matmul(dx, w, "nt", a.dtype, name + "_da")
        dw = matmul(a, dx, "tn", w.dtype, name + "_dw")
        return dx, da, dw, dgain

    op.defvjp(fwd, bwd)
    return op


def make_mlp(name, with_norm):
    def run(x, h, w1, w2, gain):
        a = matmul(h, w1, "nn", BF16, name + "_1_f", blocked=True)
        out = matmul(a, w2, "nn", F32, name + "_2_f", res=x, rms_gain=gain if with_norm else None, a_relu2=True)
        return (tuple(out) if with_norm else (out,)), a

    @jax.custom_vjp
    def op(x, h, w1, w2, gain):
        return run(x, h, w1, w2, gain)[0]

    def fwd(x, h, w1, w2, gain):
        out, a = run(x, h, w1, w2, gain)
        return out, (h, w1, w2, gain, a, out[0])

    def bwd(saved, cts):
        h, w1, w2, gain, a, x_new = saved
        if with_norm:
            dx, dgain = _rms_fan_bwd(x_new, gain, cts[0], cts[1], name + "_nb")
        else:
            dx, dgain = cts[0], None
        da = matmul(dx, w2, "nt", BF16, name + "_2_da", relu_gate=a)
        dw2 = matmul(a, dx, "tn", w2.dtype, name + "_2_dw", a_relu2=True)
        dh = matmul(da, w1, "nt", h.dtype, name + "_1_da", blocked=True)
        dw1 = matmul(h, da, "tn", w1.dtype, name + "_1_dw", blocked=True)
        return dx, dh, dw1, dw2, dgain

    op.defvjp(fwd, bwd)
    return op


def _kind(k):
    if isinstance(k, str):
        return k, None, 1
    return k[0], k[1], (k[2] if len(k) > 2 else 1)


def _tile_spec(kind, shape, tm, heads):
    k, d, ns = _kind(kind)
    if k == "row":
        return pl.BlockSpec((tm, shape[1]), (lambda h, i: (i, 0)) if heads else (lambda i: (i, 0)))
    if k == "par":
        return pl.BlockSpec(tuple(shape), (lambda h, i: (0, 0)) if heads else (lambda i: (0, 0)))
    if k == "rowh":
        return pl.BlockSpec((tm, d * ns), lambda h, i: (i, h))
    if k == "parh":
        return pl.BlockSpec((shape[0], d * ns), lambda h, i: (0, h))
    raise ValueError(kind)


def _tile_grid(rows, tm, heads):
    n_rows = rows // tm
    return ((heads, n_rows) if heads else (n_rows,)), (1 if heads else 0)


def _split_vals(kinds, refs):
    vals, counts = [], []
    for kind, r in zip(kinds, refs):
        _, d, ns = _kind(kind)
        v = r[...].astype(F32)
        vals += [v] if ns == 1 else [v[:, p * d:(p + 1) * d] for p in range(ns)]
        counts.append(ns)
    return vals, counts


def tile_fwd(fn, name, kinds, args, outs, rows, tm, heads, row_base=0):
    grid, row_axis = _tile_grid(rows, tm, heads)
    n_in = len(args)
    out_shapes = [jax.ShapeDtypeStruct((rows, w), dt) for (_, w, dt) in outs]

    def body(*refs):
        vals, _ = _split_vals(kinds, refs[:n_in])
        row0 = row_base + pl.program_id(row_axis) * tm
        res = list(fn(_PLAIN, row0, *vals))
        for o_ref, (k, _, _) in zip(refs[n_in:], outs):
            pieces = [res.pop(0) for _ in range(_kind(k)[2])]
            v = pieces[0] if len(pieces) == 1 else jnp.concatenate(pieces, axis=-1)
            o_ref[...] = v.astype(o_ref.dtype)

    return pl.pallas_call(
        body, name=name, out_shape=out_shapes, grid=grid,
        in_specs=[_tile_spec(k, a.shape, tm, heads) for k, a in zip(kinds, args)],
        out_specs=[_tile_spec(k, (rows, w), tm, heads) for (k, w, _) in outs],
        compiler_params=_params(("arbitrary",) * len(grid)),
    )(*args)


def tile_bwd(fn, name, kinds, args, diff, outs, cts, rows, tm, heads, row_base=0):
    grid, row_axis = _tile_grid(rows, tm, heads)
    n_in, n_ct = len(args), len(cts)
    diff_idx = [i for i, d in enumerate(diff) if d]
    g_shapes, g_specs = [], []
    for i in diff_idx:
        k = _kind(kinds[i])[0]
        dt = args[i].dtype if k in ("row", "rowh") else F32
        g_shapes.append(jax.ShapeDtypeStruct(args[i].shape, dt))
        g_specs.append(_tile_spec(kinds[i], args[i].shape, tm, heads))

    def body(*refs):
        in_refs, ct_refs, g_refs = refs[:n_in], refs[n_in:n_in + n_ct], refs[n_in + n_ct:]
        vals, counts = _split_vals(kinds, in_refs)
        first_piece = [sum(counts[:i]) for i in range(n_in)]
        flat_diff = [first_piece[i] + p for i in diff_idx for p in range(counts[i])]
        row_id = pl.program_id(row_axis)
        row0 = row_base + row_id * tm

        def f(*dvals):
            full = list(vals)
            for i, dv in zip(flat_diff, dvals):
                full[i] = dv
            return tuple(fn(_DIFF, row0, *full))

        _, vjp = jax.vjp(f, *[vals[i] for i in flat_diff])
        ct_vals, _ = _split_vals([k for (k, _, _) in outs], ct_refs)
        flat_grads = list(vjp(tuple(ct_vals)))
        for g_ref, i in zip(g_refs, diff_idx):
            pieces = [flat_grads.pop(0) for _ in range(counts[i])]
            g = pieces[0] if len(pieces) == 1 else jnp.concatenate(pieces, axis=-1)
            k = _kind(kinds[i])[0]
            if k in ("row", "rowh"):
                g_ref[...] = g.astype(g_ref.dtype)
            else:
                first = row_id == 0
                if heads and k == "par":
                    first = jnp.logical_and(first, pl.program_id(0) == 0)

                @pl.when(first)
                def _(g_ref=g_ref, g=g):
                    g_ref[...] = g

                @pl.when(jnp.logical_not(first))
                def _(g_ref=g_ref, g=g):
                    g_ref[...] += g

    return pl.pallas_call(
        body, name=name, out_shape=g_shapes, grid=grid,
        in_specs=[_tile_spec(k, a.shape, tm, heads) for k, a in zip(kinds, args)]
        + [_tile_spec(k, (rows, w), tm, heads) for (k, w, _) in outs],
        out_specs=g_specs,
        compiler_params=_params(("arbitrary",) * len(grid)),
    )(*args, *cts)


def make_tile_op(fn, name, kinds, diff, outs, rows, tm, heads=0, row_base=0):
    tm = min(tm, rows)

    @jax.custom_vjp
    def op(*args):
        return tuple(tile_fwd(fn, name + "_f", kinds, args, outs, rows, tm, heads, row_base))

    def fwd(*args):
        return op(*args), args

    def bwd(args, cts):
        grads = tile_bwd(fn, name + "_b", kinds, args, diff, outs, cts, rows, tm, heads, row_base)
        it = iter(grads)
        res = []
        for a, d in zip(args, diff):
            res.append(next(it).astype(a.dtype) if d else None)
        return tuple(res)

    op.defvjp(fwd, bwd)
    return op


def _rms(x, g):
    return x * lax.rsqrt(jnp.mean(x * x, axis=-1, keepdims=True) + EPS) * g


def fn_rms(ops, row0, x, g):
    return (_rms(x, g),)


def fn_fan_rms(ops, row0, x, g):
    return x, _rms(x, g)


def fn_mla_down(ops, row0, h, c, s, w_cq, w_ckv, w_kr, w_krs, g_q, g_kv):
    c_qn = _rms(ops.b.nn(h, w_cq), g_q)
    c_kvn = _rms(ops.b.nn(h, w_ckv), g_kv)
    return c_qn, c_kvn, ops.b.nn(h, w_kr) * c + ops.b.nn(h, w_krs) * s


def fn_mla_up(ops, row0, c_qn, c_kvn, c, s, w_qn, w_qr, w_qrs, w_kv):
    c_all = jnp.concatenate([c] * MLA_HEADS, axis=-1)
    s_all = jnp.concatenate([s] * MLA_HEADS, axis=-1)
    q_rope = ops.b.nn(c_qn, w_qr) * c_all + ops.b.nn(c_qn, w_qrs) * s_all
    return ops.b.nn(c_qn, w_qn), q_rope, ops.b.nn(c_kvn, w_kv)


def _softmax(s):
    m = lax.stop_gradient(jnp.max(s, axis=-1, keepdims=True))
    e = jnp.exp(s - m)
    return e / jnp.sum(e, axis=-1, keepdims=True)


def fn_xattn(ops, row0, hx, *t):
    w_q, k, v = t[:X_HEADS], t[X_HEADS:2 * X_HEADS], t[2 * X_HEADS:]
    outs = []
    for w_h, k_h, v_h in zip(w_q, k, v):
        s = ops.b.nt(ops.b.nn(hx, w_h), k_h) * (X_HEAD_DIM ** -0.5)
        outs.append(ops.b.nn(_softmax(s), v_h))
    return tuple(outs)


def _silu(x):
    return x * jax.nn.sigmoid(x)


def fn_gdn_prep(ops, row0, *t):
    nh = len(t) // 3
    qs, ks, vs = [], [], []
    for qc, kc, vc in zip(t[:nh], t[nh:2 * nh], t[2 * nh:]):
        q, k = _silu(qc), _silu(kc)
        qs.append(q * lax.rsqrt(jnp.sum(q * q, -1, keepdims=True) + EPS) * (GDN_DK ** -0.5))
        ks.append(k * lax.rsqrt(jnp.sum(k * k, -1, keepdims=True) + EPS))
        vs.append(_silu(vc))
    return tuple(qs + ks + vs)


def fn_gdn_gates(ops, row0, ba, alog, dtb):
    width = GDN_HEADS * GDN_DK
    beta = jax.nn.sigmoid(ba)
    z = ba + dtb
    softplus = jnp.maximum(z, 0.0) + jnp.log1p(jnp.exp(-jnp.abs(z)))
    g = -jnp.exp(alog) * softplus
    r = lax.broadcasted_iota(jnp.int32, (LANES, width), 0)
    c = lax.broadcasted_iota(jnp.int32, (LANES, width), 1) // GDN_DK
    e_beta = (r == c).astype(F32)
    e_g = (r == c + GDN_HEADS).astype(F32)
    return ops.h.nn(beta, e_beta), ops.h.nn(g, e_g)


def fn_gdn_out(ops, row0, *t):
    nh = (len(t) - 1) // 2
    g = t[-1]
    return tuple(_rms(o, g) * _silu(gate) for o, gate in zip(t[:nh], t[nh:2 * nh]))


def fn_mla_attn(ops, row0, qn, qr, kn, v, kr):
    s = (ops.b.nt(qn, kn) + ops.b.nt(qr, kr)) * ((MLA_NOPE + MLA_ROPE) ** -0.5)
    rows = row0 + lax.broadcasted_iota(jnp.int32, s.shape, 0)
    cols = lax.broadcasted_iota(jnp.int32, s.shape, 1)
    s = jnp.where(rows >= cols, s, NEG_BIG)
    return (ops.b.nn(_softmax(s), v),)


def _shift_down(x, d, t_idx):
    if d == 0:
        return x
    return jnp.where(t_idx >= d, pltpu.roll(x, d, axis=0), 0.0)


def _shift_up(x, d, t_idx):
    if d == 0:
        return x
    n = x.shape[0]
    return jnp.where(t_idx < n - d, pltpu.roll(x, n - d, axis=0), 0.0)


def conv_fwd(x, w, name):
    s, c = x.shape
    kw = w.shape[0]
    tc = _pick(c, (256, 128))

    def body(x_ref, w_ref, y_ref):
        xv = x_ref[...]
        t_idx = lax.broadcasted_iota(jnp.int32, xv.shape, 0)
        acc = jnp.zeros_like(xv)
        for j in range(kw):
            acc = acc + w_ref[j:j + 1, :] * _shift_down(xv, kw - 1 - j, t_idx)
        y_ref[...] = acc

    return pl.pallas_call(
        body, name=name, out_shape=jax.ShapeDtypeStruct((s, c), F32), grid=(c // tc,),
        in_specs=[pl.BlockSpec((s, tc), lambda i: (0, i)), pl.BlockSpec((kw, tc), lambda i: (0, i))],
        out_specs=pl.BlockSpec((s, tc), lambda i: (0, i)),
        compiler_params=_params(("parallel",)),
    )(x, w)


def conv_bwd(x, w, dy, name):
    s, c = x.shape
    kw = w.shape[0]
    tc = _pick(c, (256, 128))

    def body(x_ref, w_ref, dy_ref, dx_ref, dw_ref):
        xv, dyv = x_ref[...], dy_ref[...]
        t_idx = lax.broadcasted_iota(jnp.int32, xv.shape, 0)
        dx = jnp.zeros_like(xv)
        for j in range(kw):
            d = kw - 1 - j
            dx = dx + w_ref[j:j + 1, :] * _shift_up(dyv, d, t_idx)
            dw_ref[j:j + 1, :] = jnp.sum(dyv * _shift_down(xv, d, t_idx), axis=0, keepdims=True)
        dx_ref[...] = dx

    return pl.pallas_call(
        body, name=name,
        out_shape=[jax.ShapeDtypeStruct((s, c), F32), jax.ShapeDtypeStruct((kw, c), F32)],
        grid=(c // tc,),
        in_specs=[pl.BlockSpec((s, tc), lambda i: (0, i)), pl.BlockSpec((kw, tc), lambda i: (0, i)),
                  pl.BlockSpec((s, tc), lambda i: (0, i))],
        out_specs=[pl.BlockSpec((s, tc), lambda i: (0, i)), pl.BlockSpec((kw, tc), lambda i: (0, i))],
        compiler_params=_params(("parallel",)),
    )(x, w, dy)


def _conv_taps(x, w_ref, t_idx):
    kw = w_ref.shape[0]
    acc = jnp.zeros_like(x)
    for j in range(kw):
        acc = acc + w_ref[j:j + 1, :] * _shift_down(x, kw - 1 - j, t_idx)
    return acc


def gated_conv_fwd(b, c, u, w, name):
    s, ch = c.shape
    kw = w.shape[0]
    tc = _pick(ch, (256, 128))

    def body(b_ref, c_ref, u_ref, w_ref, y_ref):
        x = c_ref[...] * u_ref[...]
        t_idx = lax.broadcasted_iota(jnp.int32, x.shape, 0)
        y_ref[...] = (b_ref[...] * _conv_taps(x, w_ref, t_idx)).astype(y_ref.dtype)

    blk = pl.BlockSpec((s, tc), lambda i: (0, i))
    return pl.pallas_call(
        body, name=name, out_shape=jax.ShapeDtypeStruct((s, ch), BF16), grid=(ch // tc,),
        in_specs=[blk, blk, blk, pl.BlockSpec((kw, tc), lambda i: (0, i))], out_specs=blk,
        compiler_params=_params(("parallel",)),
    )(b, c, u, w)


def gated_conv_bwd(b, c, u, w, dy, name):
    s, ch = c.shape
    kw = w.shape[0]
    tc = _pick(ch, (256, 128))

    def body(b_ref, c_ref, u_ref, w_ref, dy_ref, db_ref, dc_ref, du_ref, dw_ref):
        cv, uv, dyv = c_ref[...], u_ref[...], dy_ref[...].astype(F32)
        x = cv * uv
        t_idx = lax.broadcasted_iota(jnp.int32, x.shape, 0)
        db_ref[...] = dyv * _conv_taps(x, w_ref, t_idx)
        dconv = dyv * b_ref[...]
        dx = jnp.zeros_like(x)
        for j in range(kw):
            d = kw - 1 - j
            dx = dx + w_ref[j:j + 1, :] * _shift_up(dconv, d, t_idx)
            dw_ref[j:j + 1, :] = jnp.sum(dconv * _shift_down(x, d, t_idx), axis=0, keepdims=True)
        dc_ref[...] = dx * uv
        du_ref[...] = dx * cv

    blk = pl.BlockSpec((s, tc), lambda i: (0, i))
    wblk = pl.BlockSpec((kw, tc), lambda i: (0, i))
    return pl.pallas_call(
        body, name=name,
        out_shape=[jax.ShapeDtypeStruct((s, ch), F32)] * 3 + [jax.ShapeDtypeStruct((kw, ch), F32)],
        grid=(ch // tc,), in_specs=[blk, blk, blk, wblk, blk], out_specs=[blk, blk, blk, wblk],
        compiler_params=_params(("parallel",)),
    )(b, c, u, w, dy)


def make_gated_conv(name):
    @jax.custom_vjp
    def op(b, c, u, w):
        return gated_conv_fwd(b, c, u, w, name + "_f")

    def fwd(b, c, u, w):
        return op(b, c, u, w), (b, c, u, w)

    def bwd(saved, dy):
        return tuple(gated_conv_bwd(*saved, dy, name + "_b"))

    op.defvjp(fwd, bwd)
    return op


def make_conv(name):
    @jax.custom_vjp
    def op(x, w):
        return conv_fwd(x, w, name + "_f")

    def fwd(x, w):
        return op(x, w), (x, w)

    def bwd(saved, dy):
        dx, dw = conv_bwd(saved[0], saved[1], dy, name + "_b")
        return dx, dw

    op.defvjp(fwd, bwd)
    return op


def _gdn_consts():
    c, d = GDN_CHUNK, GDN_DK
    i = lax.broadcasted_iota(jnp.int32, (c, c), 0)
    j = lax.broadcasted_iota(jnp.int32, (c, c), 1)
    tri = i >= j
    return dict(
        tri=tri, strict=i > j,
        tri_f=tri.astype(F32),
        eye=(i == j).astype(F32),
        lane0=(lax.broadcasted_iota(jnp.int32, (c, d), 1) == 0).astype(F32),
        last_row=(lax.broadcasted_iota(jnp.int32, (c, d), 0) == c - 1).astype(F32),
    )


def _inverse_given(m_ops):
    @jax.custom_vjp
    def given(mm_, t):
        return t

    def bwd(t, dt):
        return -m_ops.nt(m_ops.tn(t, dt), t), jnp.zeros_like(t)

    given.defvjp(lambda mm_, t: (t, t), bwd)
    return given


def _gdn_chunk(ops, q, k, v, g, beta, state, t_saved=None):
    b, m, sel = ops.bb, ops.bm, ops.bs
    nh, c, d = q.shape[0], GDN_CHUNK, GDN_DK
    k_ = _gdn_consts()

    def per_head(a):
        return jnp.broadcast_to(a, (nh,) + a.shape)

    gc = sel.sel_nn(per_head(k_["tri_f"]), g)
    col = jnp.broadcast_to(jnp.sum(gc * k_["lane0"], axis=2, keepdims=True), (nh, c, c))
    row = sel.sel_nt(per_head(k_["lane0"]), gc)
    decay = jnp.where(k_["tri"], jnp.exp(jnp.where(k_["tri"], col - row, 0.0)), 0.0)
    kb = k * beta
    mm_ = jnp.where(k_["strict"], b.nt(kb, k) * decay, 0.0)
    if t_saved is None:
        p = -mm_
        t = k_["eye"] + p
        for _ in range(int(math.log2(GDN_CHUNK)) - 1):
            p = m.nn(p, p)
            t = t + m.nn(t, p)
    else:
        t = _inverse_given(_PLAIN.bm)(mm_, t_saved)
    egc = jnp.exp(gc)
    u = b.nn(t, v * beta)
    w = b.nn(t, kb * egc)
    attn = b.nt(q, k) * decay
    v_new = u - b.nn(w, state)
    o = b.nn(q * egc, state) + b.nn(attn, v_new)
    g_last = jnp.sum(gc * k_["last_row"], axis=1, keepdims=True)
    new_state = (state * jnp.exp(jnp.broadcast_to(g_last, (nh, d, d)))
                 + b.tn(k * jnp.exp(jnp.broadcast_to(g_last, (nh, c, d)) - gc), v_new))
    return o, new_state, t


GDN_HEAD_GROUP = 8
GDN_TILE_CHUNKS = 4


def _heads_of(ref, rows, n_heads):
    d = GDN_DK
    return jnp.stack([ref[rows, h * d:(h + 1) * d] for h in range(n_heads)])


def _gdn_specs(s, reverse):
    d, hg = GDN_DK, GDN_HEAD_GROUP
    tile = min(GDN_TILE_CHUNKS * GDN_CHUNK, s)
    n_tiles = s // tile
    t_of = (lambda t: n_tiles - 1 - t) if reverse else (lambda t: t)
    seq = pl.BlockSpec((tile, hg * d), lambda grp, t: (t_of(t), grp))
    st = pl.BlockSpec((hg, tile // GDN_CHUNK, d, d), lambda grp, t: (grp, t_of(t), 0, 0))
    inv = pl.BlockSpec((hg, tile // GDN_CHUNK, GDN_CHUNK, GDN_CHUNK), lambda grp, t: (grp, t_of(t), 0, 0))
    return seq, st, inv, tile, n_tiles


def gdn_fwd(q, k, v, g, beta, name):
    s = q.shape[0]
    d, hg = GDN_DK, GDN_HEAD_GROUP
    seq, st, inv, tile, n_tiles = _gdn_specs(s, False)

    def body(q_ref, k_ref, v_ref, g_ref, b_ref, o_ref, st_ref, inv_ref, state_scr):
        @pl.when(pl.program_id(1) == 0)
        def _():
            state_scr[...] = jnp.zeros_like(state_scr)

        def step(ci, carry):
            rows = pl.ds(pl.multiple_of(ci * GDN_CHUNK, GDN_CHUNK), GDN_CHUNK)
            state = state_scr[...]
            for h in range(hg):
                st_ref[h, ci] = state[h]
            o, new_state, t = _gdn_chunk(_PLAIN, *[_heads_of(r, rows, hg) for r in (q_ref, k_ref, v_ref, g_ref, b_ref)],
                                         state)
            for h in range(hg):
                o_ref[rows, h * d:(h + 1) * d] = o[h]
                inv_ref[h, ci] = t[h]
            state_scr[...] = new_state
            return carry

        lax.fori_loop(0, tile // GDN_CHUNK, step, 0)

    return pl.pallas_call(
        body, name=name,
        out_shape=[jax.ShapeDtypeStruct(q.shape, F32),
                   jax.ShapeDtypeStruct((GDN_HEADS, s // GDN_CHUNK, d, d), F32),
                   jax.ShapeDtypeStruct((GDN_HEADS, s // GDN_CHUNK, GDN_CHUNK, GDN_CHUNK), F32)],
        grid=(GDN_HEADS // hg, n_tiles), in_specs=[seq] * 5, out_specs=[seq, st, inv],
        scratch_shapes=[pltpu.VMEM((hg, d, d), F32)],
        compiler_params=_params(("parallel", "arbitrary")),
    )(q, k, v, g, beta)


def gdn_bwd(q, k, v, g, beta, states, inverses, do, name):
    s = q.shape[0]
    d, hg = GDN_DK, GDN_HEAD_GROUP
    seq, st, inv, tile, n_tiles = _gdn_specs(s, True)
    tile_chunks = tile // GDN_CHUNK

    def body(q_ref, k_ref, v_ref, g_ref, b_ref, st_ref, inv_ref, do_ref, dq_ref, dk_ref, dv_ref, dg_ref, db_ref,
             dstate_scr):
        @pl.when(pl.program_id(1) == 0)
        def _():
            dstate_scr[...] = jnp.zeros_like(dstate_scr)

        def step(it, carry):
            ci = tile_chunks - 1 - it
            rows = pl.ds(pl.multiple_of(ci * GDN_CHUNK, GDN_CHUNK), GDN_CHUNK)
            prim = [_heads_of(r, rows, hg) for r in (q_ref, k_ref, v_ref, g_ref, b_ref)]
            prim.append(jnp.stack([st_ref[h, ci] for h in range(hg)]))
            t_saved = jnp.stack([inv_ref[h, ci] for h in range(hg)])
            _, vjp = jax.vjp(lambda *a: _gdn_chunk(_DIFF, *a, t_saved=t_saved)[:2], *prim)
            grads = vjp((_heads_of(do_ref, rows, hg), dstate_scr[...]))
            for g_ref_out, gr in zip((dq_ref, dk_ref, dv_ref, dg_ref, db_ref), grads[:5]):
                for h in range(hg):
                    g_ref_out[rows, h * d:(h + 1) * d] = gr[h]
            dstate_scr[...] = grads[5]
            return carry

        lax.fori_loop(0, tile_chunks, step, 0)

    return pl.pallas_call(
        body, name=name,
        out_shape=[jax.ShapeDtypeStruct(q.shape, F32)] * 5,
        grid=(GDN_HEADS // hg, n_tiles), in_specs=[seq] * 5 + [st, inv, seq], out_specs=[seq] * 5,
        scratch_shapes=[pltpu.VMEM((hg, d, d), F32)],
        compiler_params=_params(("parallel", "arbitrary")),
    )(q, k, v, g, beta, states, inverses, do)


def make_gdn(name):
    @jax.custom_vjp
    def op(q, k, v, g, beta):
        return gdn_fwd(q, k, v, g, beta, name + "_f")[0]

    def fwd(q, k, v, g, beta):
        o, states, inverses = gdn_fwd(q, k, v, g, beta, name + "_f")
        return o, (q, k, v, g, beta, states, inverses)

    def bwd(saved, do):
        return tuple(gdn_bwd(*saved, do, name + "_b"))

    op.defvjp(fwd, bwd)
    return op


def loss_head(x, g, target, name):
    s, d = x.shape
    tm = min(256, s)

    def body(x_ref, g_ref, t_ref, loss_ref, dx_ref, dg_ref):
        tgt = t_ref[...]

        def f(xv, gv):
            err = _rms(xv, gv) - tgt
            per_row = jnp.mean(err * err, axis=-1, keepdims=True)
            return 0.5 * jnp.sum(per_row, axis=0, keepdims=True)

        val, vjp = jax.vjp(f, x_ref[...], g_ref[...])
        dx, dg = vjp(jnp.ones((1, 1), F32))
        dx_ref[...] = dx
        first = pl.program_id(0) == 0

        @pl.when(first)
        def _():
            dg_ref[...] = dg
            loss_ref[...] = jnp.broadcast_to(val, loss_ref.shape)

        @pl.when(jnp.logical_not(first))
        def _():
            dg_ref[...] += dg
            loss_ref[...] += jnp.broadcast_to(val, loss_ref.shape)

    row = pl.BlockSpec((tm, d), lambda i: (i, 0))
    vec = pl.BlockSpec((1, d), lambda i: (0, 0))
    return pl.pallas_call(
        body, name=name,
        out_shape=[jax.ShapeDtypeStruct((1, LANES), F32), jax.ShapeDtypeStruct((s, d), F32),
                   jax.ShapeDtypeStruct((1, d), F32)],
        grid=(s // tm,), in_specs=[row, vec, row],
        out_specs=[pl.BlockSpec((1, LANES), lambda i: (0, 0)), row, vec],
        compiler_params=_params(("arbitrary",)),
    )(x, g, target)


def adamw(g8, w, m, v, layer, prev, name):
    n_layers, rows, width = w.shape
    tr = _pick(rows, (256, 128, 64, 32, 16, 8))

    def body(g_ref, w_ref, m_ref, v_ref, *rest):
        go_ref, d_ref, mo_ref, vo_ref = rest[-4:]
        g = g_ref[0].astype(F32)
        for p in range(1, N_DEV):
            g = g + g_ref[p].astype(F32)
        m_new = ADAM_B1 * m_ref[...] + (1.0 - ADAM_B1) * g
        v_new = ADAM_B2 * v_ref[...] + (1.0 - ADAM_B2) * (g * g)
        m_hat = m_new / (1.0 - ADAM_B1 ** ADAM_STEP)
        v_hat = v_new / (1.0 - ADAM_B2 ** ADAM_STEP)
        go_ref[...] = g
        d_ref[...] = -ADAM_LR * (m_hat / (jnp.sqrt(v_hat) + ADAM_EPS) + ADAM_WD * w_ref[...])
        mo_ref[...] = m_new
        vo_ref[...] = v_new

    blk = pl.BlockSpec((None, tr, width), lambda i: (layer, i, 0))
    carried = list(prev) if prev is not None else []
    return pl.pallas_call(
        body, name=name, out_shape=[jax.ShapeDtypeStruct((n_layers, rows, width), F32)] * 4,
        grid=(rows // tr,),
        in_specs=[pl.BlockSpec((N_DEV, tr, width), lambda i: (0, i, 0)), blk, blk, blk]
        + [pl.BlockSpec(memory_space=pl.ANY)] * len(carried),
        out_specs=[blk] * 4,
        input_output_aliases={4 + j: j for j in range(len(carried))},
        compiler_params=_params(("parallel",)),
    )(g8, w, m, v, *carried)


_HBM = pl.BlockSpec(memory_space=pltpu.HBM)
_SEM = pl.BlockSpec(memory_space=pltpu.SEMAPHORE)
_EFFECT = pltpu.SideEffectType.DATAFLOW_SIDE_EFFECTING


def _exchange_copies(mode, src_refs, land_refs, send_sems, recv_sems, local_sems):
    x, y, c = lax.axis_index("x"), lax.axis_index("y"), lax.axis_index("c")
    me = 4 * x + 2 * y + c
    n = len(src_refs)

    def src(k, p):
        return src_refs[k] if mode == "gather" else src_refs[k].at[p]

    local = [pltpu.make_async_copy(src(k, me), land_refs[k].at[me], local_sems.at[k]) for k in range(n)]
    sends, recvs = [], []
    for k in range(n):
        for r in range(1, N_DEV):
            px = (1 - x) if r & 4 else x
            py = (1 - y) if r & 2 else y
            pc = (1 - c) if r & 1 else c
            p = 4 * px + 2 * py + pc
            sem = k * (N_DEV - 1) + r - 1
            sends.append(pltpu.make_async_remote_copy(
                src_ref=src(k, p), dst_ref=land_refs[k].at[me],
                send_sem=send_sems.at[sem], recv_sem=recv_sems.at[sem],
                device_id=(px, py, pc), device_id_type=pl.DeviceIdType.MESH))
            recvs.append(pltpu.make_async_remote_copy(
                src_ref=src(k, p), dst_ref=land_refs[k].at[p],
                send_sem=send_sems.at[sem], recv_sem=recv_sems.at[sem],
                device_id=(px, py, pc), device_id_type=pl.DeviceIdType.MESH))
    return local, sends, recvs


def exchange_start(mode, arrays, name, carry=()):
    n, nc = len(arrays), len(carry)
    land_shapes = [((N_DEV,) + tuple(a.shape)) if mode == "gather" else tuple(a.shape) for a in arrays]
    lands = [pltpu.with_memory_space_constraint(lax.empty(shp, a.dtype), pltpu.HBM)
             for shp, a in zip(land_shapes, arrays)]
    srcs = [pltpu.with_memory_space_constraint(a, pltpu.HBM) for a in arrays]
    carried = [pltpu.with_memory_space_constraint(a, pltpu.HBM) for a in carry]

    def body(*refs):
        src_refs, land_refs = refs[:n], refs[n:2 * n]
        first_out = 2 * n + nc
        send_sems, recv_sems, local_sems = refs[first_out:first_out + 3]
        token = refs[-1]
        local, sends, _ = _exchange_copies(mode, src_refs, land_refs, send_sems, recv_sems, local_sems)
        for cp in local + sends:
            cp.start()
        token[...] = jnp.zeros_like(token)

    n_sem = n * (N_DEV - 1)
    out = pl.pallas_call(
        body, name=name,
        out_shape=(pltpu.SemaphoreType.DMA((n_sem,)), pltpu.SemaphoreType.DMA((n_sem,)),
                   pltpu.SemaphoreType.DMA((n,)),
                   *[pltpu.HBM(a.shape, a.dtype) for a in arrays],
                   *[pltpu.HBM(shp, a.dtype) for shp, a in zip(land_shapes, arrays)],
                   *[pltpu.HBM(a.shape, a.dtype) for a in carry],
                   jax.ShapeDtypeStruct((8, LANES), F32)),
        in_specs=[_HBM] * (2 * n + nc),
        out_specs=(_SEM, _SEM, _SEM, *[_HBM] * (2 * n + nc), pl.BlockSpec(memory_space=pltpu.VMEM)),
        input_output_aliases={i: 3 + i for i in range(2 * n + nc)},
        compiler_params=pltpu.CompilerParams(has_side_effects=_EFFECT),
    )(*srcs, *lands, *carried)
    handle = dict(mode=mode, sems=out[:3], srcs=out[3:3 + n], lands=out[3 + n:3 + 2 * n])
    return handle, out[-1], list(out[3 + 2 * n:3 + 2 * n + nc])


def exchange_wait(handle, after, name):
    mode, srcs, lands = handle["mode"], list(handle["srcs"]), list(handle["lands"])
    n = len(srcs)

    def body(*refs):
        src_refs, land_refs = refs[:n], refs[n:2 * n]
        send_sems, recv_sems, local_sems = refs[2 * n:2 * n + 3]
        local, sends, recvs = _exchange_copies(mode, src_refs, land_refs, send_sems, recv_sems, local_sems)
        for cp in sends:
            cp.wait_send()
        for cp in recvs:
            cp.wait_recv()
        for cp in local:
            cp.wait()

    out = pl.pallas_call(
        body, name=name,
        out_shape=(*[pltpu.HBM(a.shape, a.dtype) for a in srcs], *[pltpu.HBM(a.shape, a.dtype) for a in lands]),
        in_specs=[_HBM] * (2 * n) + [_SEM] * 3 + [pl.BlockSpec(memory_space=pl.ANY)],
        out_specs=tuple([_HBM] * (2 * n)),
        input_output_aliases={i: i for i in range(2 * n)},
        compiler_params=pltpu.CompilerParams(has_side_effects=_EFFECT),
    )(*srcs, *lands, *handle["sems"], after)
    return list(out[n:])


_ICI_RELATIONS = (2, 4, 6)


def _mesh_place():
    x, y, c = lax.axis_index("x"), lax.axis_index("y"), lax.axis_index("c")

    def peer(r):
        px = (1 - x) if r & 4 else x
        py = (1 - y) if r & 2 else y
        pc = (1 - c) if r & 1 else c
        return (px, py, pc), 4 * px + 2 * py + pc

    return 4 * x + 2 * y + c, peer


def _remote(src, dst, send_sem, recv_sem, device):
    return pltpu.make_async_remote_copy(src_ref=src, dst_ref=dst, send_sem=send_sem, recv_sem=recv_sem,
                                        device_id=device, device_id_type=pl.DeviceIdType.MESH)


def gather2_start(groups, name):
    flat = [a for g in groups for a in g]
    n = len(flat)
    lands = [pltpu.with_memory_space_constraint(lax.empty((N_DEV,) + tuple(a.shape), a.dtype), pltpu.HBM) for a in flat]
    srcs = [pltpu.with_memory_space_constraint(a, pltpu.HBM) for a in flat]
    n_rel = 1 + len(_ICI_RELATIONS)

    def body(*refs):
        src_refs, land_refs = refs[:n], refs[n:2 * n]
        sem_refs = refs[2 * n:2 * n + 4 * len(groups)]
        me, peer = _mesh_place()
        k = 0
        for gi, g in enumerate(groups):
            send_sems, recv_sib, recv_ici, local_sems = sem_refs[4 * gi:4 * gi + 4]
            for j in range(len(g)):
                pltpu.make_async_copy(src_refs[k], land_refs[k].at[me], local_sems.at[j]).start()
                dev, _ = peer(1)
                _remote(src_refs[k], land_refs[k].at[me], send_sems.at[n_rel * j], recv_sib.at[j], dev).start()
                for t, r in enumerate(_ICI_RELATIONS):
                    dev, _ = peer(r)
                    _remote(src_refs[k], land_refs[k].at[me], send_sems.at[n_rel * j + 1 + t],
                            recv_ici.at[len(_ICI_RELATIONS) * j + t], dev).start()
                k += 1
        refs[-1][...] = jnp.zeros_like(refs[-1])

    sem_shapes = []
    for g in groups:
        sem_shapes += [pltpu.SemaphoreType.DMA((n_rel * len(g),)), pltpu.SemaphoreType.DMA((len(g),)),
                       pltpu.SemaphoreType.DMA((len(_ICI_RELATIONS) * len(g),)), pltpu.SemaphoreType.DMA((len(g),))]
    out = pl.pallas_call(
        body, name=name,
        out_shape=(*sem_shapes, *[pltpu.HBM(a.shape, a.dtype) for a in flat],
                   *[pltpu.HBM((N_DEV,) + tuple(a.shape), a.dtype) for a in flat],
                   jax.ShapeDtypeStruct((8, LANES), F32)),
        in_specs=[_HBM] * (2 * n),
        out_specs=(*[_SEM] * len(sem_shapes), *[_HBM] * (2 * n), pl.BlockSpec(memory_space=pltpu.VMEM)),
        input_output_aliases={i: len(sem_shapes) + i for i in range(2 * n)},
        compiler_params=pltpu.CompilerParams(has_side_effects=_EFFECT),
    )(*srcs, *lands)
    handles, k, base = [], 0, len(sem_shapes)
    for gi, g in enumerate(groups):
        handles.append(dict(sems=out[4 * gi:4 * gi + 4], srcs=out[base + k:base + k + len(g)],
                            lands=out[base + n + k:base + n + k + len(g)]))
        k += len(g)
    return handles, out[-1]


def gather2_forward(handle, after, name, carry=()):
    lands, nc = list(handle["lands"]), len(carry)
    n, n_ici = len(lands), len(_ICI_RELATIONS)
    carried = [pltpu.with_memory_space_constraint(a, pltpu.HBM) for a in carry]

    def body(*refs):
        land_refs = refs[:n]
        recv_ici = refs[n + nc]
        fwd_send, fwd_recv = refs[n + nc + 2], refs[n + nc + 3]
        me, peer = _mesh_place()
        sibling, _ = peer(1)
        for j in range(n):
            for t, r in enumerate(_ICI_RELATIONS):
                dev, p = peer(r)
                landed = land_refs[j].at[p]
                _remote(landed, landed, fwd_send.at[n_ici * j + t], recv_ici.at[n_ici * j + t], dev).wait_recv()
                _remote(landed, landed, fwd_send.at[n_ici * j + t], fwd_recv.at[n_ici * j + t], sibling).start()

    out = pl.pallas_call(
        body, name=name,
        out_shape=(pltpu.SemaphoreType.DMA((n_ici * n,)), pltpu.SemaphoreType.DMA((n_ici * n,)),
                   *[pltpu.HBM(a.shape, a.dtype) for a in lands], *[pltpu.HBM(a.shape, a.dtype) for a in carry]),
        in_specs=[_HBM] * (n + nc) + [_SEM, pl.BlockSpec(memory_space=pl.ANY)],
        out_specs=(_SEM, _SEM, *[_HBM] * (n + nc)),
        input_output_aliases={i: 2 + i for i in range(n + nc)},
        compiler_params=pltpu.CompilerParams(has_side_effects=_EFFECT),
    )(*lands, *carried, handle["sems"][2], after)
    new_handle = dict(sems=handle["sems"], srcs=handle["srcs"], lands=out[2:2 + n], fwd=out[:2])
    return new_handle, list(out[2 + n:])


def gather2_wait(handle, after, name):
    srcs, lands = list(handle["srcs"]), list(handle["lands"])
    n, n_ici = len(srcs), len(_ICI_RELATIONS)
    n_rel = 1 + n_ici
    send_all, recv_sibling, _, local_all = handle["sems"]

    def body(*refs):
        src_refs, land_refs = refs[:n], refs[n:2 * n]
        send_sems, recv_sib, local_sems, fwd_send, fwd_recv = refs[2 * n:2 * n + 5]
        me, peer = _mesh_place()
        sibling, sib = peer(1)
        for j in range(n):
            pltpu.make_async_copy(src_refs[j], land_refs[j].at[me], local_sems.at[j]).wait()
            _remote(src_refs[j], land_refs[j].at[sib], send_sems.at[n_rel * j], recv_sib.at[j], sibling).wait()
            for t, r in enumerate(_ICI_RELATIONS):
                dev, p = peer(r)
                _remote(src_refs[j], land_refs[j].at[me], send_sems.at[n_rel * j + 1 + t],
                        recv_sib.at[j], dev).wait_send()
                _, p_sib = peer(r ^ 1)
                _remote(land_refs[j].at[p], land_refs[j].at[p_sib], fwd_send.at[n_ici * j + t],
                        fwd_recv.at[n_ici * j + t], sibling).wait()

    out = pl.pallas_call(
        body, name=name,
        out_shape=(*[pltpu.HBM(a.shape, a.dtype) for a in srcs], *[pltpu.HBM(a.shape, a.dtype) for a in lands]),
        in_specs=[_HBM] * (2 * n) + [_SEM] * 5 + [pl.BlockSpec(memory_space=pl.ANY)],
        out_specs=tuple([_HBM] * (2 * n)),
        input_output_aliases={i: i for i in range(2 * n)},
        compiler_params=pltpu.CompilerParams(has_side_effects=_EFFECT),
    )(*srcs, *lands, send_all, recv_sibling, local_all, *handle["fwd"], after)
    return list(out[n:])


BIG = ["mla_w_in", "mla_w_uq", "mla_w_ukv", "mla_w_o", "gdn_w_in", "gdn_w_o", "sc_w_in", "sc_w_o",
       "xa_w_q", "xa_w_kv", "xa_w_o", "mlp_w1", "mlp_w2"]
TINY = [("mla_q_norm", 1), ("mla_kv_norm", 1), ("gdn_conv_w", 2), ("sc_conv_w", 2)]
REPL = ["gdn_a_log", "gdn_dt_bias", "gdn_o_norm", "norm_mix", "norm_mem", "norm_mlp", "mem_norm", "final_norm"]
WEIGHTS = ["mla_w_in", "mla_q_norm", "mla_kv_norm", "mla_w_uq", "mla_w_ukv", "mla_w_o", "gdn_w_in",
           "gdn_conv_w", "gdn_a_log", "gdn_dt_bias", "gdn_o_norm", "gdn_w_o", "sc_w_in", "sc_conv_w",
           "sc_w_o", "norm_mix", "norm_mem", "norm_mlp", "xa_w_q", "xa_w_kv", "xa_w_o", "mlp_w1",
           "mlp_w2", "mem_norm", "final_norm"]
MIXER_WEIGHTS = (["mla_w_in", "mla_w_uq", "mla_w_ukv", "mla_w_o"], ["gdn_w_in", "gdn_w_o"], ["sc_w_in", "sc_w_o"])
MIXER_PARAMS = (["norm_mem", "mla_q_norm", "mla_kv_norm"],
                ["norm_mem", "gdn_conv_w", "gdn_a_log", "gdn_dt_bias", "gdn_o_norm"],
                ["norm_mem", "sc_conv_w"])


def from_shards(a8, axis):
    a = jnp.moveaxis(a8, 0, axis)
    shp = a.shape
    return a.reshape(shp[:axis] + (shp[axis] * shp[axis + 1],) + shp[axis + 2:])


def pack_rows(flat_list, width, row_mult):
    total = sum(a.shape[-1] for a in flat_list)
    rows = -(-total // width)
    rows = -(-rows // row_mult) * row_mult
    pad = rows * width - total
    parts = list(flat_list)
    if pad:
        parts.append(jnp.zeros((pad,), flat_list[0].dtype))
    return jnp.concatenate(parts, axis=-1).reshape(rows, width)


def unpack_rows(packed, shapes):
    lead = packed.shape[:-2]
    flat = packed.reshape(lead + (-1,))
    out, off = [], 0
    for shp in shapes:
        n = math.prod(shp)
        out.append(flat[..., off:off + n].reshape(lead + tuple(shp)))
        off += n
    return out


def _swap_halves(w):
    half = w.shape[-1] // 2
    return jnp.concatenate([w[..., half:], w[..., :half]], axis=-1)


def _pad_last(w, n):
    return jnp.pad(w, [(0, 0)] * (w.ndim - 1) + [(0, n - w.shape[-1])])


def _unblock(w8):
    return jnp.transpose(w8, (1, 0, 2)).reshape(w8.shape[1], -1)


def _stack_rows(w8):
    return w8.reshape(-1, w8.shape[-1])


def rms_op(name, rows, d, out_dtype):
    tm = rows if rows * d * 4 <= BLOCK_BYTES else 512
    return make_tile_op(fn_rms, name, ["row", "par"], [True, True], [("row", d, out_dtype)], rows, min(tm, rows))


def seg_memory(p, mem):
    return rms_op("rms_memory", mem.shape[0], mem.shape[1], BF16)(mem, p["mem_norm"].reshape(1, -1))[0]


def seg_mixer(i, wts, p, x, h, rope_c, rope_s):
    s, d = x.shape
    j, kind = i // N_MIXERS, i % N_MIXERS
    tag = f"l{i}"
    hd = MLA_NOPE
    next_gain = p["norm_mem"][i].reshape(1, d)
    if kind == 0:
        w_in = _stack_rows(wts["mla_w_in"])
        w_cq = w_in[:, :MLA_Q_RANK]
        w_ckv = w_in[:, MLA_Q_RANK:MLA_Q_RANK + MLA_KV_RANK]
        w_kr = w_in[:, MLA_Q_RANK + MLA_KV_RANK:]
        c_qn, c_kvn, k_rope = make_tile_op(
            fn_mla_down, tag + "_mla_down", ["row", "row", "row"] + ["par"] * 6, [True, False, False] + [True] * 6,
            [("row", MLA_Q_RANK, BF16), ("row", MLA_KV_RANK, BF16), ("row", hd, F32)], s, 256)(
            h, rope_c, rope_s, w_cq, w_ckv, _pad_last(w_kr, hd), _pad_last(_swap_halves(w_kr), hd),
            p["mla_q_norm"][j].reshape(1, -1), p["mla_kv_norm"][j].reshape(1, -1))
        w_uq8 = wts["mla_w_uq"]
        w_qn = _unblock(w_uq8[:, :, :MLA_NOPE])
        w_qr = w_uq8[:, :, MLA_NOPE:]
        w_qr_p = _unblock(_pad_last(w_qr, hd))
        w_qr_s = _unblock(_pad_last(_swap_halves(w_qr), hd))
        nq = MLA_HEADS * hd
        q_nope, q_rope, kv = make_tile_op(
            fn_mla_up, tag + "_mla_up", ["row", "row", "row", "row"] + ["par"] * 4, [True, True, False, False] + [True] * 4,
            [("row", nq, BF16), ("row", nq, F32), ("row", 2 * nq, BF16)], s, 256)(
            c_qn, c_kvn, rope_c, rope_s, w_qn, w_qr_p, w_qr_s, _unblock(wts["mla_w_ukv"]))
        n_groups = MLA_QUERY_GROUPS if s % (MLA_QUERY_GROUPS * 256) == 0 else 1
        rows_g = s // n_groups
        o_groups = []
        for grp in range(n_groups):
            r0, r1 = grp * rows_g, (grp + 1) * rows_g
            o_groups.append(make_tile_op(
                fn_mla_attn, f"{tag}_mla_attn{grp}", [("rowh", hd), ("rowh", hd), ("parh", hd, 2), "par"],
                [True] * 4, [(("rowh", hd), nq, BF16)], rows_g, 256, MLA_HEADS, row_base=r0)(
                q_nope[r0:r1], q_rope[r0:r1], kv[:r1], k_rope[:r1])[0])
        o = jnp.concatenate(o_groups, axis=0)
        return make_mm_res_rms(tag + "_mla_o")(x, o, _stack_rows(wts["mla_w_o"]), next_gain)
    if kind == 1:
        ng = GDN_HEADS * GDN_DK
        w_in = _unblock(wts["gdn_w_in"])
        qkv_pre = make_mm(tag + "_gdn_in_qkv", F32)(h, w_in[:, :3 * ng])
        qkv_conv = make_conv(tag + "_gdn_conv")(qkv_pre, p["gdn_conv_w"][j])
        gate = make_mm(tag + "_gdn_in_g", F32)(h, w_in[:, 3 * ng:4 * ng])
        ba = make_mm(tag + "_gdn_in_ba", F32)(h, _pad_last(w_in[:, 4 * ng:], LANES))
        heads_row = ("row", GDN_DK, GDN_HEADS)
        q, k, v = make_tile_op(fn_gdn_prep, tag + "_gdn_prep", [("row", GDN_DK, 3 * GDN_HEADS)], [True],
                               [(heads_row, ng, F32)] * 3, s, 256)(qkv_conv)
        alog = jnp.pad(p["gdn_a_log"][j].reshape(1, -1), ((0, 0), (GDN_HEADS, LANES - 2 * GDN_HEADS)))
        dtb = jnp.pad(p["gdn_dt_bias"][j].reshape(1, -1), ((0, 0), (GDN_HEADS, LANES - 2 * GDN_HEADS)))
        beta_b, g_b = make_tile_op(fn_gdn_gates, tag + "_gdn_gates", ["row", "par", "par"], [True] * 3,
                                   [("row", ng, F32)] * 2, s, 512)(ba, alog, dtb)
        o = make_gdn(tag + "_gdn_core")(q, k, v, g_b, beta_b)
        o = make_tile_op(fn_gdn_out, tag + "_gdn_out", [heads_row, heads_row, "par"],
                         [True] * 3, [(heads_row, ng, BF16)], s, 512)(
            o, gate, p["gdn_o_norm"][j].reshape(1, -1))[0]
        return make_mm_res_rms(tag + "_gdn_o")(x, o, _stack_rows(wts["gdn_w_o"]), next_gain)
    w_in = _unblock(wts["sc_w_in"])
    b_gate = make_mm(tag + "_sc_in_b", F32)(h, w_in[:, :d])
    c_gate = make_mm(tag + "_sc_in_c", F32)(h, w_in[:, d:2 * d])
    u = make_mm(tag + "_sc_in_u", F32)(h, w_in[:, 2 * d:])
    yv = make_gated_conv(tag + "_sc_conv")(b_gate, c_gate, u, p["sc_conv_w"][j])
    return make_mm_res_rms(tag + "_sc_o")(x, yv, _stack_rows(wts["sc_w_o"]), next_gain)


def seg_xattn(i, wts, p, x, hx, mem_n):
    s, d = x.shape
    tag = f"l{i}"
    kv = make_mm(tag + "_xa_kv", BF16, blocked=True)(mem_n, wts["xa_w_kv"])
    heads = ("row", X_HEAD_DIM, X_HEADS)
    o = make_tile_op(fn_xattn, tag + "_xattn",
                     ["row", ("par", X_HEAD_DIM, X_HEADS), ("par", X_HEAD_DIM, 2 * X_HEADS)], [True] * 3,
                     [(heads, d, BF16)], s, 512)(hx, _stack_rows(wts["xa_w_q"]), kv)[0]
    return make_mm_res_rms(tag + "_xa_o")(x, o, _stack_rows(wts["xa_w_o"]), p["norm_mlp"][i].reshape(1, d))


def seg_mlp(i, wts, p, x, hm):
    d = x.shape[1]
    gain = p["norm_mix"][i + 1].reshape(1, d) if i + 1 < DEPTH else None
    return make_mlp(f"l{i}_mlp", gain is not None)(x, hm, wts["mlp_w1"], _stack_rows(wts["mlp_w2"]), gain)


def segments():
    segs = []
    for i in range(DEPTH):
        j, kind = i // N_MIXERS, i % N_MIXERS
        segs.append((f"l{i}_mixer", [(n, j) for n in MIXER_WEIGHTS[kind]], MIXER_PARAMS[kind], "mixer"))
        segs.append((f"l{i}_xattn", [(n, i) for n in ("xa_w_q", "xa_w_kv", "xa_w_o")], ["norm_mlp"], "xattn"))
        segs.append((f"l{i}_mlp", [(n, i) for n in ("mlp_w1", "mlp_w2")], ["norm_mix"] if i + 1 < DEPTH else [],
                     "mlp"))
    return segs


def run_segment(index, kind, wts, p, x, h, mem_n, rope_c, rope_s):
    layer = index // 3
    if kind == "mixer":
        return seg_mixer(layer, wts, p, x, h, rope_c, rope_s)
    if kind == "xattn":
        return seg_xattn(layer, wts, p, x, h, mem_n)
    return seg_mlp(layer, wts, p, x, h)


def rope_tables(positions):
    inv_freq = ROPE_THETA ** (-jnp.arange(0, MLA_ROPE, 2, dtype=F32) / MLA_ROPE)
    ang = positions.astype(F32)[:, None] * inv_freq
    cos, sin = jnp.cos(ang), jnp.sin(ang)
    zeros = jnp.zeros((positions.shape[0], MLA_NOPE - MLA_ROPE), F32)
    return jnp.concatenate([cos, cos, zeros], axis=-1), jnp.concatenate([-sin, sin, zeros], axis=-1)


def kernel(x, mem, positions, mla_w_in, mla_q_norm, mla_kv_norm, mla_w_uq, mla_w_ukv, mla_w_o, gdn_w_in, gdn_conv_w, gdn_a_log, gdn_dt_bias, gdn_o_norm, gdn_w_o, sc_w_in, sc_conv_w, sc_w_o, norm_mix, norm_mem, norm_mlp, xa_w_q, xa_w_kv, xa_w_o, mlp_w1, mlp_w2, mem_norm, final_norm, loss_target, m_mla_w_in, m_mla_q_norm, m_mla_kv_norm, m_mla_w_uq, m_mla_w_ukv, m_mla_w_o, m_gdn_w_in, m_gdn_conv_w, m_gdn_a_log, m_gdn_dt_bias, m_gdn_o_norm, m_gdn_w_o, m_sc_w_in, m_sc_conv_w, m_sc_w_o, m_norm_mix, m_norm_mem, m_norm_mlp, m_xa_w_q, m_xa_w_kv, m_xa_w_o, m_mlp_w1, m_mlp_w2, m_mem_norm, m_final_norm, v_mla_w_in, v_mla_q_norm, v_mla_kv_norm, v_mla_w_uq, v_mla_w_ukv, v_mla_w_o, v_gdn_w_in, v_gdn_conv_w, v_gdn_a_log, v_gdn_dt_bias, v_gdn_o_norm, v_gdn_w_o, v_sc_w_in, v_sc_conv_w, v_sc_w_o, v_norm_mix, v_norm_mem, v_norm_mlp, v_xa_w_q, v_xa_w_kv, v_xa_w_o, v_mlp_w1, v_mlp_w2, v_mem_norm, v_final_norm):
    args = locals()
    w_loc = {n: args[n] for n in WEIGHTS}
    m_loc = {n: args["m_" + n] for n in WEIGHTS}
    v_loc = {n: args["v_" + n] for n in WEIGHTS}
    me = 4 * lax.axis_index("x") + 2 * lax.axis_index("y") + lax.axis_index("c")
    segs = segments()

    w16 = {n: w_loc[n].astype(BF16) for n in BIG}
    tiny_pack = pack_rows([w_loc[n].reshape(-1) for n, _ in TINY], LANES, 8)
    gather_handles, token = gather2_start(
        [[tiny_pack]] + [[w16[n][layer] for n, layer in units] for _, units, _, _ in segs], "gather_start")

    x_cur = x[0]
    rope_c, rope_s = rope_tables(positions[0])
    tiny_handle, _ = gather2_forward(gather_handles[0], token, "gather_forward_tiny")
    tiny_all = gather2_wait(tiny_handle, token, "gather_wait_tiny")[0]
    gather_handles = gather_handles[1:]
    params = {}
    for (n, ax), a8 in zip(TINY, unpack_rows(tiny_all, [w_loc[n].shape for n, _ in TINY])):
        params[n] = from_shards(a8, ax)
    for n in REPL:
        params[n] = w_loc[n]

    mem_n, vjp_memory = jax.vjp(lambda p_: seg_memory(p_, mem[0]), {"mem_norm": params["mem_norm"]})
    h_cur, vjp_first_norm = jax.vjp(
        lambda p_, x_: rms_op("l0_rms_mix", x_.shape[0], x_.shape[1], BF16)(x_, p_["norm_mix"][0].reshape(1, -1))[0],
        {"norm_mix": params["norm_mix"]}, x_cur)
    vjps = []
    forwarded, _ = gather2_forward(gather_handles[0], token, f"gather_forward_{segs[0][0]}")
    for index, (tag, units, p_names, kind) in enumerate(segs):
        landed = gather2_wait(forwarded, token if index == 0 else x_cur, f"gather_wait_{tag}")
        wts = {n: a for (n, _), a in zip(units, landed)}
        p_seg = {n: params[n] for n in p_names}
        if index + 1 < len(segs):
            forwarded, (p_seg[p_names[0]],) = gather2_forward(
                gather_handles[index + 1], landed[0], f"gather_forward_{segs[index + 1][0]}",
                carry=[p_seg[p_names[0]]])
        outs, vjp_seg = jax.vjp(
            lambda w_, p_, x_, h_, m_, index=index, kind=kind:
            run_segment(index, kind, w_, p_, x_, h_, m_, rope_c, rope_s),
            wts, p_seg, x_cur, h_cur, mem_n)
        x_cur, h_cur = outs[0], (outs[1] if len(outs) > 1 else None)
        vjps.append(vjp_seg)

    loss_vec, g_x, d_final = loss_head(x_cur, params["final_norm"].reshape(1, -1), loss_target[0], "loss_head")

    grads = {n: jnp.zeros_like(params[n]) for n in params}
    grads["final_norm"] = d_final.reshape(-1)
    g_mem_n = jnp.zeros_like(mem_n)
    g_h = None
    scatter_handles = []
    for (tag, units, _, _), vjp_seg in zip(reversed(segs), reversed(vjps)):
        g_wts, g_p, g_x, g_h, g_m = vjp_seg((g_x,) if g_h is None else (g_x, g_h))
        for n, g in g_p.items():
            grads[n] = grads[n] + g
        g_mem_n = g_mem_n + g_m
        handle, _, (g_h,) = exchange_start("scatter", [g_wts[n] for n, _ in units], f"scatter_start_{tag}",
                                           carry=[g_h])
        scatter_handles.append((units, handle))
    grads["mem_norm"] = grads["mem_norm"] + vjp_memory(g_mem_n)[0]["mem_norm"]
    g_first, g_x_norm = vjp_first_norm(g_h)
    grads["norm_mix"] = grads["norm_mix"] + g_first["norm_mix"]
    g_x = g_x + g_x_norm

    small_names = [n for n, _ in TINY] + REPL
    small_g = pack_rows([loss_vec[0, :1]] + [grads[n].astype(F32).reshape(-1) for n in small_names], PACK_W, 8)
    small_handle, _, _ = exchange_start("gather", [small_g], "gather_start_small_grads")

    g_recv = {}
    for units, handle in scatter_handles:
        landed = exchange_wait(handle, g_x, f"scatter_wait_{units[0][0]}_{units[0][1]}")
        g_recv.update(dict(zip(units, landed)))

    res = {}
    for n in BIG:
        outs = None
        for layer in range(w_loc[n].shape[0]):
            outs = adamw(g_recv[n, layer], w_loc[n], m_loc[n], v_loc[n], layer, outs, f"adamw_{n}_{layer}")
        for kind, a in zip(("grad", "delta", "m", "v"), outs):
            res[(kind, n)] = a
    small_recv = exchange_wait(small_handle, res[("grad", BIG[-1])], "gather_wait_small_grads")[0]

    def full_small(d):
        parts = [jnp.zeros((1,), F32)]
        for n, ax in TINY:
            full_shape = params[n].shape
            start = [0] * len(full_shape)
            start[ax] = me * d[n].shape[ax]
            parts.append(lax.dynamic_update_slice(jnp.zeros(full_shape, F32), d[n], start).reshape(-1))
        parts += [d[n].reshape(-1) for n in REPL]
        return pack_rows(parts, PACK_W, 8)

    outs_small = adamw(small_recv, full_small(w_loc)[None], full_small(m_loc)[None], full_small(v_loc)[None],
                       0, None, "adamw_small")
    small_shapes = [(1,)] + [params[n].shape for n, _ in TINY] + [w_loc[n].shape for n in REPL]
    loss = None
    for kind, packed in zip(("grad", "delta", "m", "v"), outs_small):
        parts = unpack_rows(packed[0], small_shapes)
        if kind == "grad":
            loss = parts[0][0]
        for (n, ax), a in zip(TINY, parts[1:1 + len(TINY)]):
            start = [0] * a.ndim
            start[ax] = me * w_loc[n].shape[ax]
            res[(kind, n)] = lax.dynamic_slice(a, start, w_loc[n].shape)
        for n, a in zip(REPL, parts[1 + len(TINY):]):
            res[(kind, n)] = a

    out = [loss, g_x[None]]
    for kind in ("grad", "delta", "m", "v"):
        out += [res[(kind, n)] for n in WEIGHTS]
    return tuple(out)
```

```python
import math

import jax
import jax.numpy as jnp
from jax import lax
from jax.experimental import pallas as pl
from jax.experimental.pallas import tpu as pltpu

F32 = jnp.float32
BF16 = jnp.bfloat16

N_DEV = 8
LANES = 128
EPS = 1e-6
ROPE_THETA = 10000.0
MLA_HEADS, MLA_NOPE, MLA_ROPE, MLA_V = 8, 128, 64, 128
MLA_Q_RANK, MLA_KV_RANK = 384, 256
GDN_HEADS, GDN_DK, GDN_CONV, GDN_CHUNK = 8, 128, 4, 64
X_HEADS, X_HEAD_DIM = 4, 256
DEPTH, N_MIXERS = 4, 3
ADAM_LR, ADAM_B1, ADAM_B2, ADAM_EPS, ADAM_WD, ADAM_STEP = 0.001, 0.9, 0.999, 1e-08, 0.01, 10
MLA_QUERY_GROUPS = 4
NEG_BIG = -1e30
PACK_W = 1024


_NN = (((1,), (0,)), ((), ()))
_NT = (((1,), (1,)), ((), ()))
_TN = (((0,), (0,)), ((), ()))
_NN3 = (((2,), (1,)), ((0,), (0,)))
_NT3 = (((2,), (2,)), ((0,), (0,)))
_TN3 = (((1,), (1,)), ((0,), (0,)))


def _dot(a, b, dims):
    return lax.dot_general(a, b, dims, preferred_element_type=F32)


def _hi_lo(x):
    hi = x.astype(BF16)
    return hi, (x - hi.astype(F32)).astype(BF16)


def _split3(x):
    hi = x.astype(BF16)
    r = x - hi.astype(F32)
    mid = r.astype(BF16)
    return hi, mid, (r - mid.astype(F32)).astype(BF16)


def _dg(a, b, dims, prec):
    if prec == "h":
        return lax.dot_general(a, b, dims, precision=lax.Precision.HIGHEST, preferred_element_type=F32)
    if prec == "m":
        a_hi, a_lo = _hi_lo(a)
        b_hi, b_lo = _hi_lo(b)
        return _dot(a_hi, b_hi, dims) + _dot(a_hi, b_lo, dims) + _dot(a_lo, b_hi, dims)
    return _dot(a.astype(BF16), b.astype(BF16), dims)


def _dg_sel(sel, x, dims, sel_first):
    s16 = sel.astype(BF16)
    parts = [(_dot(s16, piece, dims) if sel_first else _dot(piece, s16, dims)) for piece in _split3(x)]
    return parts[0] + parts[1] + parts[2]


class _Ops:
    def __init__(self, prec, differentiable, batched=False):
        d_nn, d_nt, d_tn = (_NN3, _NT3, _TN3) if batched else (_NN, _NT, _TN)

        def nn(a, b):
            return _dg(a, b, d_nn, prec)

        def nt(a, b):
            return _dg(a, b, d_nt, prec)

        def tn(a, b):
            return _dg(a, b, d_tn, prec)

        if differentiable:
            dnn = jax.custom_vjp(nn)
            dnn.defvjp(lambda a, b: (nn(a, b), (a, b)), lambda r, g: (nt(g, r[1]), tn(r[0], g)))
            dnt = jax.custom_vjp(nt)
            dnt.defvjp(lambda a, b: (nt(a, b), (a, b)), lambda r, g: (nn(g, r[1]), tn(g, r[0])))
            dtn = jax.custom_vjp(tn)
            dtn.defvjp(lambda a, b: (tn(a, b), (a, b)), lambda r, g: (nt(r[1], g), nn(r[0], g)))
            nn, nt, tn = dnn, dnt, dtn
        self.nn, self.nt, self.tn = nn, nt, tn


class _SelOps:
    def __init__(self, differentiable, batched=False):
        d_nn, d_nt, d_tn = (_NN3, _NT3, _TN3) if batched else (_NN, _NT, _TN)

        def sel_nn(sel, x):
            return _dg_sel(sel, x, d_nn, True)

        def sel_nt(sel, x):
            return _dg_sel(sel, x, d_nt, True)

        if differentiable:
            dnn = jax.custom_vjp(sel_nn)
            dnn.defvjp(lambda s, x: (sel_nn(s, x), s),
                       lambda s, g: (jnp.zeros_like(s), _dg_sel(s, g, d_tn, True)))
            dnt = jax.custom_vjp(sel_nt)
            dnt.defvjp(lambda s, x: (sel_nt(s, x), s),
                       lambda s, g: (jnp.zeros_like(s), _dg_sel(s, g, d_tn, False)))
            sel_nn, sel_nt = dnn, dnt
        self.sel_nn, self.sel_nt = sel_nn, sel_nt


class _OpSet:
    def __init__(self, differentiable):
        self.b = _Ops("b", differentiable)
        self.h = _Ops("h", differentiable)
        self.bb = _Ops("b", differentiable, batched=True)
        self.bm = _Ops("m", differentiable, batched=True)
        self.bs = _SelOps(differentiable, batched=True)


_PLAIN = _OpSet(False)
_DIFF = _OpSet(True)


def _params(sem):
    return pltpu.CompilerParams(dimension_semantics=sem)


BLOCK_BYTES = 4 * 1024 * 1024


def _pick(n, cands):
    for c in cands:
        if n % c == 0:
            return c
    return n


def _tile(n, cap):
    if n <= cap:
        return n
    return _pick(n, tuple(c for c in (2048, 1024, 768, 512, 384, 256, 128) if c <= cap))


def matmul(a, b, form, out_dtype, name, res=None, blocked=False, relu_gate=None, rms_gain=None, a_relu2=False):
    if form == "nn":
        m, k = a.shape
        k2, n = (b.shape[1], N_DEV * b.shape[2]) if blocked else b.shape
    elif form == "nt":
        m, k = a.shape
        n, k2 = (b.shape[1], N_DEV * b.shape[2]) if blocked else b.shape
    else:
        (k, m), (k2, n) = a.shape, b.shape
    assert k == k2, (a.shape, b.shape, form)
    tk = k if k <= 2048 else _tile(k, 1024)
    cb = nb = 1
    if blocked:
        cb = (k if form == "nt" else n) // N_DEV
        nb = _pick(N_DEV, tuple(c for c in (8, 4, 2, 1) if c * cb <= 1024))
    if blocked and form == "nt":
        tk = nb * cb
    if blocked and form != "nt":
        tn = nb * cb
    else:
        tn = _tile(n, min(1024, BLOCK_BYTES // (tk * b.dtype.itemsize)))
    out_elems = BLOCK_BYTES // 2 if (out_dtype == BF16 and res is None) else BLOCK_BYTES // 4
    tm = _tile(m, min(BLOCK_BYTES // (tk * a.dtype.itemsize), out_elems // tn))
    nk = k // tk
    dims = {"nn": _NN, "nt": _NT, "tn": _TN}[form]

    a_spec = {"nn": pl.BlockSpec((tm, tk), lambda i, j, kk: (i, kk)),
              "nt": pl.BlockSpec((tm, tk), lambda i, j, kk: (i, kk)),
              "tn": pl.BlockSpec((tk, tm), lambda i, j, kk: (kk, i))}[form]
    if blocked and form == "nn":
        b_spec = pl.BlockSpec((nb, tk, cb), lambda i, j, kk: (j, kk, 0))
    elif blocked and form == "nt":
        b_spec = pl.BlockSpec((nb, tn, cb), lambda i, j, kk: (kk, j, 0))
    else:
        b_spec = {"nn": pl.BlockSpec((tk, tn), lambda i, j, kk: (kk, j)),
                  "nt": pl.BlockSpec((tn, tk), lambda i, j, kk: (j, kk)),
                  "tn": pl.BlockSpec((tk, tn), lambda i, j, kk: (kk, j))}[form]
    c_spec = pl.BlockSpec((tm, tn), lambda i, j, kk: (i, j))
    out_shape = jax.ShapeDtypeStruct((m, n), out_dtype)
    o_spec = c_spec
    blocked_out = blocked and form == "tn"
    if blocked_out:
        out_shape = jax.ShapeDtypeStruct((N_DEV, m, cb), out_dtype)
        o_spec = pl.BlockSpec((nb, tm, cb), lambda i, j, kk: (j, i, 0))
    has_res, has_gate, has_gain = res is not None, relu_gate is not None, rms_gain is not None
    extras = [e for e in (res, relu_gate) if e is not None]
    n_in = 2 + len(extras) + has_gain
    second = has_gain
    assert not (second and (blocked_out or tn != n))

    def body(*refs):
        a_ref, b_ref = refs[0], refs[1]
        r_ref = refs[2] if has_res else None
        gate_ref = refs[2 + has_res] if has_gate else None
        gain_ref = refs[n_in - 1] if has_gain else None
        o_ref = refs[n_in]
        if a_relu2:
            relu = jnp.maximum(a_ref[...].astype(F32), 0.0)
            a_val = (relu * relu).astype(BF16)
        else:
            a_val = a_ref[...].astype(BF16)
        if blocked and form == "nn":
            part = jnp.concatenate([_dot(a_val, b_ref[t].astype(BF16), dims) for t in range(nb)], axis=-1)
        elif blocked and form == "nt":
            part = _dot(a_val[:, :cb], b_ref[0].astype(BF16), dims)
            for t in range(1, nb):
                part = part + _dot(a_val[:, t * cb:(t + 1) * cb], b_ref[t].astype(BF16), dims)
        else:
            part = _dot(a_val, b_ref[...].astype(BF16), dims)

        def finish(acc):
            if has_res:
                acc = acc + r_ref[...].astype(F32)
            if has_gate:
                acc = acc * (2.0 * jnp.maximum(gate_ref[...].astype(F32), 0.0))
            if blocked_out:
                for t in range(nb):
                    o_ref[t] = acc[:, t * cb:(t + 1) * cb].astype(out_dtype)
            else:
                o_ref[...] = acc.astype(out_dtype)
            if has_gain:
                refs[n_in + 1][...] = _rms(acc, gain_ref[...]).astype(BF16)

        if nk == 1:
            finish(part)
        else:
            acc_ref = refs[-1]
            kk = pl.program_id(2)

            @pl.when(kk == 0)
            def _():
                acc_ref[...] = part

            @pl.when(jnp.logical_and(kk > 0, kk < nk - 1))
            def _():
                acc_ref[...] += part

            @pl.when(kk == nk - 1)
            def _():
                finish(acc_ref[...] + part)

    in_specs = [a_spec, b_spec] + [c_spec] * len(extras)
    args = [a, b] + extras
    if has_gain:
        in_specs.append(pl.BlockSpec((1, tn), lambda i, j, kk: (0, j)))
        args.append(rms_gain)
    if second:
        out_shape = [out_shape, jax.ShapeDtypeStruct((m, n), BF16)]
        o_spec = [o_spec, c_spec]
    return pl.pallas_call(
        body, name=name,
        out_shape=out_shape,
        grid=(m // tm, n // tn, nk),
        in_specs=in_specs, out_specs=o_spec,
        scratch_shapes=[pltpu.VMEM((tm, tn), F32)] if nk > 1 else [],
        compiler_params=_params(("parallel", "parallel", "arbitrary")),
    )(*args)


def make_mm(name, out_dtype, with_res=False, blocked=False):
    def bwd_mm(a, w, g):
        da = matmul(g, w, "nt", a.dtype, name + "_da", blocked=blocked)
        dw = matmul(a, g, "tn", w.dtype, name + "_dw", blocked=blocked)
        return da, dw

    if with_res:
        @jax.custom_vjp
        def op(res, a, w):
            return matmul(a, w, "nn", out_dtype, name + "_f", res=res, blocked=blocked)

        def fwd(res, a, w):
            return op(res, a, w), (a, w)

        def bwd(saved, g):
            return (g,) + bwd_mm(*saved, g)
    else:
        @jax.custom_vjp
        def op(a, w):
            return matmul(a, w, "nn", out_dtype, name + "_f", blocked=blocked)

        def fwd(a, w):
            return op(a, w), (a, w)

        def bwd(saved, g):
            return bwd_mm(*saved, g)
    op.defvjp(fwd, bwd)
    return op


def _rms_fan_bwd(x_new, gain, dx, dh, name):
    rows, d = x_new.shape
    outs = [("row", d, F32), ("row", d, BF16)]
    return tile_bwd(fn_fan_rms, name, ["row", "par"], [x_new, gain], [True, True], outs, [dx, dh],
                    rows, min(512, rows), 0)


def make_mm_res_rms(name):
    @jax.custom_vjp
    def op(res, a, w, gain):
        return tuple(matmul(a, w, "nn", F32, name + "_f", res=res, rms_gain=gain))

    def fwd(res, a, w, gain):
        x_new, h = op(res, a, w, gain)
        return (x_new, h), (a, w, x_new, gain)

    def bwd(saved, cts):
        a, w, x_new, gain = saved
        dx, dgain = _rms_fan_bwd(x_new, gain, cts[0], cts[1], name + "_nb")
        da = matmul(dx, w, "nt", a.dtype, name + "_da")
        dw = matmul(a, dx, "tn", w.dtype, name + "_dw")
        return dx, da, dw, dgain

    op.defvjp(fwd, bwd)
    return op


def make_mlp(name, with_norm):
    def run(x, h, w1, w2, gain):
        a = matmul(h, w1, "nn", BF16, name + "_1_f", blocked=True)
        out = matmul(a, w2, "nn", F32, name + "_2_f", res=x, rms_gain=gain if with_norm else None, a_relu2=True)
        return (tuple(out) if with_norm else (out,)), a

    @jax.custom_vjp
    def op(x, h, w1, w2, gain):
        return run(x, h, w1, w2, gain)[0]

    def fwd(x, h, w1, w2, gain):
        out, a = run(x, h, w1, w2, gain)
        return out, (h, w1, w2, gain, a, out[0])

    def bwd(saved, cts):
        h, w1, w2, gain, a, x_new = saved
        if with_norm:
            dx, dgain = _rms_fan_bwd(x_new, gain, cts[0], cts[1], name + "_nb")
        else:
            dx, dgain = cts[0], None
        da = matmul(dx, w2, "nt", BF16, name + "_2_da", relu_gate=a)
        dw2 = matmul(a, dx, "tn", w2.dtype, name + "_2_dw", a_relu2=True)
        dh = matmul(da, w1, "nt", h.dtype, name + "_1_da", blocked=True)
        dw1 = matmul(h, da, "tn", w1.dtype, name + "_1_dw", blocked=True)
        return dx, dh, dw1, dw2, dgain

    op.defvjp(fwd, bwd)
    return op


def _kind(k):
    if isinstance(k, str):
        return k, None, 1
    return k[0], k[1], (k[2] if len(k) > 2 else 1)


def _tile_spec(kind, shape, tm, heads):
    k, d, ns = _kind(kind)
    if k == "row":
        return pl.BlockSpec((tm, shape[1]), (lambda h, i: (i, 0)) if heads else (lambda i: (i, 0)))
    if k == "par":
        return pl.BlockSpec(tuple(shape), (lambda h, i: (0, 0)) if heads else (lambda i: (0, 0)))
    if k == "rowh":
        return pl.BlockSpec((tm, d * ns), lambda h, i: (i, h))
    if k == "parh":
        return pl.BlockSpec((shape[0], d * ns), lambda h, i: (0, h))
    raise ValueError(kind)


def _tile_grid(rows, tm, heads):
    n_rows = rows // tm
    return ((heads, n_rows) if heads else (n_rows,)), (1 if heads else 0)


def _split_vals(kinds, refs):
    vals, counts = [], []
    for kind, r in zip(kinds, refs):
        _, d, ns = _kind(kind)
        v = r[...].astype(F32)
        vals += [v] if ns == 1 else [v[:, p * d:(p + 1) * d] for p in range(ns)]
        counts.append(ns)
    return vals, counts


def tile_fwd(fn, name, kinds, args, outs, rows, tm, heads, row_base=0):
    grid, row_axis = _tile_grid(rows, tm, heads)
    n_in = len(args)
    out_shapes = [jax.ShapeDtypeStruct((rows, w), dt) for (_, w, dt) in outs]

    def body(*refs):
        vals, _ = _split_vals(kinds, refs[:n_in])
        row0 = row_base + pl.program_id(row_axis) * tm
        res = list(fn(_PLAIN, row0, *vals))
        for o_ref, (k, _, _) in zip(refs[n_in:], outs):
            pieces = [res.pop(0) for _ in range(_kind(k)[2])]
            v = pieces[0] if len(pieces) == 1 else jnp.concatenate(pieces, axis=-1)
            o_ref[...] = v.astype(o_ref.dtype)

    return pl.pallas_call(
        body, name=name, out_shape=out_shapes, grid=grid,
        in_specs=[_tile_spec(k, a.shape, tm, heads) for k, a in zip(kinds, args)],
        out_specs=[_tile_spec(k, (rows, w), tm, heads) for (k, w, _) in outs],
        compiler_params=_params(("arbitrary",) * len(grid)),
    )(*args)


def tile_bwd(fn, name, kinds, args, diff, outs, cts, rows, tm, heads, row_base=0):
    grid, row_axis = _tile_grid(rows, tm, heads)
    n_in, n_ct = len(args), len(cts)
    diff_idx = [i for i, d in enumerate(diff) if d]
    g_shapes, g_specs = [], []
    for i in diff_idx:
        k = _kind(kinds[i])[0]
        dt = args[i].dtype if k in ("row", "rowh") else F32
        g_shapes.append(jax.ShapeDtypeStruct(args[i].shape, dt))
        g_specs.append(_tile_spec(kinds[i], args[i].shape, tm, heads))

    def body(*refs):
        in_refs, ct_refs, g_refs = refs[:n_in], refs[n_in:n_in + n_ct], refs[n_in + n_ct:]
        vals, counts = _split_vals(kinds, in_refs)
        first_piece = [sum(counts[:i]) for i in range(n_in)]
        flat_diff = [first_piece[i] + p for i in diff_idx for p in range(counts[i])]
        row_id = pl.program_id(row_axis)
        row0 = row_base + row_id * tm

        def f(*dvals):
            full = list(vals)
            for i, dv in zip(flat_diff, dvals):
                full[i] = dv
            return tuple(fn(_DIFF, row0, *full))

        _, vjp = jax.vjp(f, *[vals[i] for i in flat_diff])
        ct_vals, _ = _split_vals([k for (k, _, _) in outs], ct_refs)
        flat_grads = list(vjp(tuple(ct_vals)))
        for g_ref, i in zip(g_refs, diff_idx):
            pieces = [flat_grads.pop(0) for _ in range(counts[i])]
            g = pieces[0] if len(pieces) == 1 else jnp.concatenate(pieces, axis=-1)
            k = _kind(kinds[i])[0]
            if k in ("row", "rowh"):
                g_ref[...] = g.astype(g_ref.dtype)
            else:
                first = row_id == 0
                if heads and k == "par":
                    first = jnp.logical_and(first, pl.program_id(0) == 0)

                @pl.when(first)
                def _(g_ref=g_ref, g=g):
                    g_ref[...] = g

                @pl.when(jnp.logical_not(first))
                def _(g_ref=g_ref, g=g):
                    g_ref[...] += g

    return pl.pallas_call(
        body, name=name, out_shape=g_shapes, grid=grid,
        in_specs=[_tile_spec(k, a.shape, tm, heads) for k, a in zip(kinds, args)]
        + [_tile_spec(k, (rows, w), tm, heads) for (k, w, _) in outs],
        out_specs=g_specs,
        compiler_params=_params(("arbitrary",) * len(grid)),
    )(*args, *cts)


def make_tile_op(fn, name, kinds, diff, outs, rows, tm, heads=0, row_base=0):
    tm = min(tm, rows)

    @jax.custom_vjp
    def op(*args):
        return tuple(tile_fwd(fn, name + "_f", kinds, args, outs, rows, tm, heads, row_base))

    def fwd(*args):
        return op(*args), args

    def bwd(args, cts):
        grads = tile_bwd(fn, name + "_b", kinds, args, diff, outs, cts, rows, tm, heads, row_base)
        it = iter(grads)
        res = []
        for a, d in zip(args, diff):
            res.append(next(it).astype(a.dtype) if d else None)
        return tuple(res)

    op.defvjp(fwd, bwd)
    return op


def _rms(x, g):
    return x * lax.rsqrt(jnp.mean(x * x, axis=-1, keepdims=True) + EPS) * g


def fn_rms(ops, row0, x, g):
    return (_rms(x, g),)


def fn_fan_rms(ops, row0, x, g):
    return x, _rms(x, g)


def fn_mla_down(ops, row0, h, c, s, w_cq, w_ckv, w_kr, w_krs, g_q, g_kv):
    c_qn = _rms(ops.b.nn(h, w_cq), g_q)
    c_kvn = _rms(ops.b.nn(h, w_ckv), g_kv)
    return c_qn, c_kvn, ops.b.nn(h, w_kr) * c + ops.b.nn(h, w_krs) * s


def fn_mla_up(ops, row0, c_qn, c_kvn, c, s, w_qn, w_qr, w_qrs, w_kv):
    c_all = jnp.concatenate([c] * MLA_HEADS, axis=-1)
    s_all = jnp.concatenate([s] * MLA_HEADS, axis=-1)
    q_rope = ops.b.nn(c_qn, w_qr) * c_all + ops.b.nn(c_qn, w_qrs) * s_all
    return ops.b.nn(c_qn, w_qn), q_rope, ops.b.nn(c_kvn, w_kv)


def _softmax(s):
    m = lax.stop_gradient(jnp.max(s, axis=-1, keepdims=True))
    e = jnp.exp(s - m)
    return e / jnp.sum(e, axis=-1, keepdims=True)


def fn_xattn(ops, row0, hx, *t):
    w_q, k, v = t[:X_HEADS], t[X_HEADS:2 * X_HEADS], t[2 * X_HEADS:]
    outs = []
    for w_h, k_h, v_h in zip(w_q, k, v):
        s = ops.b.nt(ops.b.nn(hx, w_h), k_h) * (X_HEAD_DIM ** -0.5)
        outs.append(ops.b.nn(_softmax(s), v_h))
    return tuple(outs)


def _silu(x):
    return x * jax.nn.sigmoid(x)


def fn_gdn_prep(ops, row0, *t):
    nh = len(t) // 3
    qs, ks, vs = [], [], []
    for qc, kc, vc in zip(t[:nh], t[nh:2 * nh], t[2 * nh:]):
        q, k = _silu(qc), _silu(kc)
        qs.append(q * lax.rsqrt(jnp.sum(q * q, -1, keepdims=True) + EPS) * (GDN_DK ** -0.5))
        ks.append(k * lax.rsqrt(jnp.sum(k * k, -1, keepdims=True) + EPS))
        vs.append(_silu(vc))
    return tuple(qs + ks + vs)


def fn_gdn_gates(ops, row0, ba, alog, dtb):
    width = GDN_HEADS * GDN_DK
    beta = jax.nn.sigmoid(ba)
    z = ba + dtb
    softplus = jnp.maximum(z, 0.0) + jnp.log1p(jnp.exp(-jnp.abs(z)))
    g = -jnp.exp(alog) * softplus
    r = lax.broadcasted_iota(jnp.int32, (LANES, width), 0)
    c = lax.broadcasted_iota(jnp.int32, (LANES, width), 1) // GDN_DK
    e_beta = (r == c).astype(F32)
    e_g = (r == c + GDN_HEADS).astype(F32)
    return ops.h.nn(beta, e_beta), ops.h.nn(g, e_g)


def fn_gdn_out(ops, row0, *t):
    nh = (len(t) - 1) // 2
    g = t[-1]
    return tuple(_rms(o, g) * _silu(gate) for o, gate in zip(t[:nh], t[nh:2 * nh]))


def fn_mla_attn(ops, row0, qn, qr, kn, v, kr):
    s = (ops.b.nt(qn, kn) + ops.b.nt(qr, kr)) * ((MLA_NOPE + MLA_ROPE) ** -0.5)
    rows = row0 + lax.broadcasted_iota(jnp.int32, s.shape, 0)
    cols = lax.broadcasted_iota(jnp.int32, s.shape, 1)
    s = jnp.where(rows >= cols, s, NEG_BIG)
    return (ops.b.nn(_softmax(s), v),)


def _shift_down(x, d, t_idx):
    if d == 0:
        return x
    return jnp.where(t_idx >= d, pltpu.roll(x, d, axis=0), 0.0)


def _shift_up(x, d, t_idx):
    if d == 0:
        return x
    n = x.shape[0]
    return jnp.where(t_idx < n - d, pltpu.roll(x, n - d, axis=0), 0.0)


def conv_fwd(x, w, name):
    s, c = x.shape
    kw = w.shape[0]
    tc = _pick(c, (256, 128))

    def body(x_ref, w_ref, y_ref):
        xv = x_ref[...]
        t_idx = lax.broadcasted_iota(jnp.int32, xv.shape, 0)
        acc = jnp.zeros_like(xv)
        for j in range(kw):
            acc = acc + w_ref[j:j + 1, :] * _shift_down(xv, kw - 1 - j, t_idx)
        y_ref[...] = acc

    return pl.pallas_call(
        body, name=name, out_shape=jax.ShapeDtypeStruct((s, c), F32), grid=(c // tc,),
        in_specs=[pl.BlockSpec((s, tc), lambda i: (0, i)), pl.BlockSpec((kw, tc), lambda i: (0, i))],
        out_specs=pl.BlockSpec((s, tc), lambda i: (0, i)),
        compiler_params=_params(("parallel",)),
    )(x, w)


def conv_bwd(x, w, dy, name):
    s, c = x.shape
    kw = w.shape[0]
    tc = _pick(c, (256, 128))

    def body(x_ref, w_ref, dy_ref, dx_ref, dw_ref):
        xv, dyv = x_ref[...], dy_ref[...]
        t_idx = lax.broadcasted_iota(jnp.int32, xv.shape, 0)
        dx = jnp.zeros_like(xv)
        for j in range(kw):
            d = kw - 1 - j
            dx = dx + w_ref[j:j + 1, :] * _shift_up(dyv, d, t_idx)
            dw_ref[j:j + 1, :] = jnp.sum(dyv * _shift_down(xv, d, t_idx), axis=0, keepdims=True)
        dx_ref[...] = dx

    return pl.pallas_call(
        body, name=name,
        out_shape=[jax.ShapeDtypeStruct((s, c), F32), jax.ShapeDtypeStruct((kw, c), F32)],
        grid=(c // tc,),
        in_specs=[pl.BlockSpec((s, tc), lambda i: (0, i)), pl.BlockSpec((kw, tc), lambda i: (0, i)),
                  pl.BlockSpec((s, tc), lambda i: (0, i))],
        out_specs=[pl.BlockSpec((s, tc), lambda i: (0, i)), pl.BlockSpec((kw, tc), lambda i: (0, i))],
        compiler_params=_params(("parallel",)),
    )(x, w, dy)


def _conv_taps(x, w_ref, t_idx):
    kw = w_ref.shape[0]
    acc = jnp.zeros_like(x)
    for j in range(kw):
        acc = acc + w_ref[j:j + 1, :] * _shift_down(x, kw - 1 - j, t_idx)
    return acc


def gated_conv_fwd(b, c, u, w, name):
    s, ch = c.shape
    kw = w.shape[0]
    tc = _pick(ch, (256, 128))

    def body(b_ref, c_ref, u_ref, w_ref, y_ref):
        x = c_ref[...] * u_ref[...]
        t_idx = lax.broadcasted_iota(jnp.int32, x.shape, 0)
        y_ref[...] = (b_ref[...] * _conv_taps(x, w_ref, t_idx)).astype(y_ref.dtype)

    blk = pl.BlockSpec((s, tc), lambda i: (0, i))
    return pl.pallas_call(
        body, name=name, out_shape=jax.ShapeDtypeStruct((s, ch), BF16), grid=(ch // tc,),
        in_specs=[blk, blk, blk, pl.BlockSpec((kw, tc), lambda i: (0, i))], out_specs=blk,
        compiler_params=_params(("parallel",)),
    )(b, c, u, w)


def gated_conv_bwd(b, c, u, w, dy, name):
    s, ch = c.shape
    kw = w.shape[0]
    tc = _pick(ch, (256, 128))

    def body(b_ref, c_ref, u_ref, w_ref, dy_ref, db_ref, dc_ref, du_ref, dw_ref):
        cv, uv, dyv = c_ref[...], u_ref[...], dy_ref[...].astype(F32)
        x = cv * uv
        t_idx = lax.broadcasted_iota(jnp.int32, x.shape, 0)
        db_ref[...] = dyv * _conv_taps(x, w_ref, t_idx)
        dconv = dyv * b_ref[...]
        dx = jnp.zeros_like(x)
        for j in range(kw):
            d = kw - 1 - j
            dx = dx + w_ref[j:j + 1, :] * _shift_up(dconv, d, t_idx)
            dw_ref[j:j + 1, :] = jnp.sum(dconv * _shift_down(x, d, t_idx), axis=0, keepdims=True)
        dc_ref[...] = dx * uv
        du_ref[...] = dx * cv

    blk = pl.BlockSpec((s, tc), lambda i: (0, i))
    wblk = pl.BlockSpec((kw, tc), lambda i: (0, i))
    return pl.pallas_call(
        body, name=name,
        out_shape=[jax.ShapeDtypeStruct((s, ch), F32)] * 3 + [jax.ShapeDtypeStruct((kw, ch), F32)],
        grid=(ch // tc,), in_specs=[blk, blk, blk, wblk, blk], out_specs=[blk, blk, blk, wblk],
        compiler_params=_params(("parallel",)),
    )(b, c, u, w, dy)


def make_gated_conv(name):
    @jax.custom_vjp
    def op(b, c, u, w):
        return gated_conv_fwd(b, c, u, w, name + "_f")

    def fwd(b, c, u, w):
        return op(b, c, u, w), (b, c, u, w)

    def bwd(saved, dy):
        return tuple(gated_conv_bwd(*saved, dy, name + "_b"))

    op.defvjp(fwd, bwd)
    return op


def make_conv(name):
    @jax.custom_vjp
    def op(x, w):
        return conv_fwd(x, w, name + "_f")

    def fwd(x, w):
        return op(x, w), (x, w)

    def bwd(saved, dy):
        dx, dw = conv_bwd(saved[0], saved[1], dy, name + "_b")
        return dx, dw

    op.defvjp(fwd, bwd)
    return op


def _gdn_consts():
    c, d = GDN_CHUNK, GDN_DK
    i = lax.broadcasted_iota(jnp.int32, (c, c), 0)
    j = lax.broadcasted_iota(jnp.int32, (c, c), 1)
    tri = i >= j
    return dict(
        tri=tri, strict=i > j,
        tri_f=tri.astype(F32),
        eye=(i == j).astype(F32),
        lane0=(lax.broadcasted_iota(jnp.int32, (c, d), 1) == 0).astype(F32),
        last_row=(lax.broadcasted_iota(jnp.int32, (c, d), 0) == c - 1).astype(F32),
    )


def _inverse_given(m_ops):
    @jax.custom_vjp
    def given(mm_, t):
        return t

    def bwd(t, dt):
        return -m_ops.nt(m_ops.tn(t, dt), t), jnp.zeros_like(t)

    given.defvjp(lambda mm_, t: (t, t), bwd)
    return given


def _gdn_chunk(ops, q, k, v, g, beta, state, t_saved=None):
    b, m, sel = ops.bb, ops.bm, ops.bs
    nh, c, d = q.shape[0], GDN_CHUNK, GDN_DK
    k_ = _gdn_consts()

    def per_head(a):
        return jnp.broadcast_to(a, (nh,) + a.shape)

    gc = sel.sel_nn(per_head(k_["tri_f"]), g)
    col = jnp.broadcast_to(jnp.sum(gc * k_["lane0"], axis=2, keepdims=True), (nh, c, c))
    row = sel.sel_nt(per_head(k_["lane0"]), gc)
    decay = jnp.where(k_["tri"], jnp.exp(jnp.where(k_["tri"], col - row, 0.0)), 0.0)
    kb = k * beta
    mm_ = jnp.where(k_["strict"], b.nt(kb, k) * decay, 0.0)
    if t_saved is None:
        p = -mm_
        t = k_["eye"] + p
        for _ in range(int(math.log2(GDN_CHUNK)) - 1):
            p = m.nn(p, p)
            t = t + m.nn(t, p)
    else:
        t = _inverse_given(_PLAIN.bm)(mm_, t_saved)
    egc = jnp.exp(gc)
    u = b.nn(t, v * beta)
    w = b.nn(t, kb * egc)
    attn = b.nt(q, k) * decay
    v_new = u - b.nn(w, state)
    o = b.nn(q * egc, state) + b.nn(attn, v_new)
    g_last = jnp.sum(gc * k_["last_row"], axis=1, keepdims=True)
    new_state = (state * jnp.exp(jnp.broadcast_to(g_last, (nh, d, d)))
                 + b.tn(k * jnp.exp(jnp.broadcast_to(g_last, (nh, c, d)) - gc), v_new))
    return o, new_state, t


GDN_HEAD_GROUP = 8
GDN_TILE_CHUNKS = 4


def _heads_of(ref, rows, n_heads):
    d = GDN_DK
    return jnp.stack([ref[rows, h * d:(h + 1) * d] for h in range(n_heads)])


def _gdn_specs(s, reverse):
    d, hg = GDN_DK, GDN_HEAD_GROUP
    tile = min(GDN_TILE_CHUNKS * GDN_CHUNK, s)
    n_tiles = s // tile
    t_of = (lambda t: n_tiles - 1 - t) if reverse else (lambda t: t)
    seq = pl.BlockSpec((tile, hg * d), lambda grp, t: (t_of(t), grp))
    st = pl.BlockSpec((hg, tile // GDN_CHUNK, d, d), lambda grp, t: (grp, t_of(t), 0, 0))
    inv = pl.BlockSpec((hg, tile // GDN_CHUNK, GDN_CHUNK, GDN_CHUNK), lambda grp, t: (grp, t_of(t), 0, 0))
    return seq, st, inv, tile, n_tiles


def gdn_fwd(q, k, v, g, beta, name):
    s = q.shape[0]
    d, hg = GDN_DK, GDN_HEAD_GROUP
    seq, st, inv, tile, n_tiles = _gdn_specs(s, False)

    def body(q_ref, k_ref, v_ref, g_ref, b_ref, o_ref, st_ref, inv_ref, state_scr):
        @pl.when(pl.program_id(1) == 0)
        def _():
            state_scr[...] = jnp.zeros_like(state_scr)

        def step(ci, carry):
            rows = pl.ds(pl.multiple_of(ci * GDN_CHUNK, GDN_CHUNK), GDN_CHUNK)
            state = state_scr[...]
            for h in range(hg):
                st_ref[h, ci] = state[h]
            o, new_state, t = _gdn_chunk(_PLAIN, *[_heads_of(r, rows, hg) for r in (q_ref, k_ref, v_ref, g_ref, b_ref)],
                                         state)
            for h in range(hg):
                o_ref[rows, h * d:(h + 1) * d] = o[h]
                inv_ref[h, ci] = t[h]
            state_scr[...] = new_state
            return carry

        lax.fori_loop(0, tile // GDN_CHUNK, step, 0)

    return pl.pallas_call(
        body, name=name,
        out_shape=[jax.ShapeDtypeStruct(q.shape, F32),
                   jax.ShapeDtypeStruct((GDN_HEADS, s // GDN_CHUNK, d, d), F32),
                   jax.ShapeDtypeStruct((GDN_HEADS, s // GDN_CHUNK, GDN_CHUNK, GDN_CHUNK), F32)],
        grid=(GDN_HEADS // hg, n_tiles), in_specs=[seq] * 5, out_specs=[seq, st, inv],
        scratch_shapes=[pltpu.VMEM((hg, d, d), F32)],
        compiler_params=_params(("parallel", "arbitrary")),
    )(q, k, v, g, beta)


def gdn_bwd(q, k, v, g, beta, states, inverses, do, name):
    s = q.shape[0]
    d, hg = GDN_DK, GDN_HEAD_GROUP
    seq, st, inv, tile, n_tiles = _gdn_specs(s, True)
    tile_chunks = tile // GDN_CHUNK

    def body(q_ref, k_ref, v_ref, g_ref, b_ref, st_ref, inv_ref, do_ref, dq_ref, dk_ref, dv_ref, dg_ref, db_ref,
             dstate_scr):
        @pl.when(pl.program_id(1) == 0)
        def _():
            dstate_scr[...] = jnp.zeros_like(dstate_scr)

        def step(it, carry):
            ci = tile_chunks - 1 - it
            rows = pl.ds(pl.multiple_of(ci * GDN_CHUNK, GDN_CHUNK), GDN_CHUNK)
            prim = [_heads_of(r, rows, hg) for r in (q_ref, k_ref, v_ref, g_ref, b_ref)]
            prim.append(jnp.stack([st_ref[h, ci] for h in range(hg)]))
            t_saved = jnp.stack([inv_ref[h, ci] for h in range(hg)])
            _, vjp = jax.vjp(lambda *a: _gdn_chunk(_DIFF, *a, t_saved=t_saved)[:2], *prim)
            grads = vjp((_heads_of(do_ref, rows, hg), dstate_scr[...]))
            for g_ref_out, gr in zip((dq_ref, dk_ref, dv_ref, dg_ref, db_ref), grads[:5]):
                for h in range(hg):
                    g_ref_out[rows, h * d:(h + 1) * d] = gr[h]
            dstate_scr[...] = grads[5]
            return carry

        lax.fori_loop(0, tile_chunks, step, 0)

    return pl.pallas_call(
        body, name=name,
        out_shape=[jax.ShapeDtypeStruct(q.shape, F32)] * 5,
        grid=(GDN_HEADS // hg, n_tiles), in_specs=[seq] * 5 + [st, inv, seq], out_specs=[seq] * 5,
        scratch_shapes=[pltpu.VMEM((hg, d, d), F32)],
        compiler_params=_params(("parallel", "arbitrary")),
    )(q, k, v, g, beta, states, inverses, do)


def make_gdn(name):
    @jax.custom_vjp
    def op(q, k, v, g, beta):
        return gdn_fwd(q, k, v, g, beta, name + "_f")[0]

    def fwd(q, k, v, g, beta):
        o, states, inverses = gdn_fwd(q, k, v, g, beta, name + "_f")
        return o, (q, k, v, g, beta, states, inverses)

    def bwd(saved, do):
        return tuple(gdn_bwd(*saved, do, name + "_b"))

    op.defvjp(fwd, bwd)
    return op


def loss_head(x, g, target, name):
    s, d = x.shape
    tm = min(256, s)

    def body(x_ref, g_ref, t_ref, loss_ref, dx_ref, dg_ref):
        tgt = t_ref[...]

        def f(xv, gv):
            err = _rms(xv, gv) - tgt
            per_row = jnp.mean(err * err, axis=-1, keepdims=True)
            return 0.5 * jnp.sum(per_row, axis=0, keepdims=True)

        val, vjp = jax.vjp(f, x_ref[...], g_ref[...])
        dx, dg = vjp(jnp.ones((1, 1), F32))
        dx_ref[...] = dx
        first = pl.program_id(0) == 0

        @pl.when(first)
        def _():
            dg_ref[...] = dg
            loss_ref[...] = jnp.broadcast_to(val, loss_ref.shape)

        @pl.when(jnp.logical_not(first))
        def _():
            dg_ref[...] += dg
            loss_ref[...] += jnp.broadcast_to(val, loss_ref.shape)

    row = pl.BlockSpec((tm, d), lambda i: (i, 0))
    vec = pl.BlockSpec((1, d), lambda i: (0, 0))
    return pl.pallas_call(
        body, name=name,
        out_shape=[jax.ShapeDtypeStruct((1, LANES), F32), jax.ShapeDtypeStruct((s, d), F32),
                   jax.ShapeDtypeStruct((1, d), F32)],
        grid=(s // tm,), in_specs=[row, vec, row],
        out_specs=[pl.BlockSpec((1, LANES), lambda i: (0, 0)), row, vec],
        compiler_params=_params(("arbitrary",)),
    )(x, g, target)


def adamw(g8, w, m, v, layer, prev, name):
    n_layers, rows, width = w.shape
    tr = _pick(rows, (256, 128, 64, 32, 16, 8))

    def body(g_ref, w_ref, m_ref, v_ref, *rest):
        go_ref, d_ref, mo_ref, vo_ref = rest[-4:]
        g = g_ref[0].astype(F32)
        for p in range(1, N_DEV):
            g = g + g_ref[p].astype(F32)
        m_new = ADAM_B1 * m_ref[...] + (1.0 - ADAM_B1) * g
        v_new = ADAM_B2 * v_ref[...] + (1.0 - ADAM_B2) * (g * g)
        m_hat = m_new / (1.0 - ADAM_B1 ** ADAM_STEP)
        v_hat = v_new / (1.0 - ADAM_B2 ** ADAM_STEP)
        go_ref[...] = g
        d_ref[...] = -ADAM_LR * (m_hat / (jnp.sqrt(v_hat) + ADAM_EPS) + ADAM_WD * w_ref[...])
        mo_ref[...] = m_new
        vo_ref[...] = v_new

    blk = pl.BlockSpec((None, tr, width), lambda i: (layer, i, 0))
    carried = list(prev) if prev is not None else []
    return pl.pallas_call(
        body, name=name, out_shape=[jax.ShapeDtypeStruct((n_layers, rows, width), F32)] * 4,
        grid=(rows // tr,),
        in_specs=[pl.BlockSpec((N_DEV, tr, width), lambda i: (0, i, 0)), blk, blk, blk]
        + [pl.BlockSpec(memory_space=pl.ANY)] * len(carried),
        out_specs=[blk] * 4,
        input_output_aliases={4 + j: j for j in range(len(carried))},
        compiler_params=_params(("parallel",)),
    )(g8, w, m, v, *carried)


_HBM = pl.BlockSpec(memory_space=pltpu.HBM)
_SEM = pl.BlockSpec(memory_space=pltpu.SEMAPHORE)
_EFFECT = pltpu.SideEffectType.DATAFLOW_SIDE_EFFECTING


def _exchange_copies(mode, src_refs, land_refs, send_sems, recv_sems, local_sems):
    x, y, c = lax.axis_index("x"), lax.axis_index("y"), lax.axis_index("c")
    me = 4 * x + 2 * y + c
    n = len(src_refs)

    def src(k, p):
        return src_refs[k] if mode == "gather" else src_refs[k].at[p]

    local = [pltpu.make_async_copy(src(k, me), land_refs[k].at[me], local_sems.at[k]) for k in range(n)]
    sends, recvs = [], []
    for k in range(n):
        for r in range(1, N_DEV):
            px = (1 - x) if r & 4 else x
            py = (1 - y) if r & 2 else y
            pc = (1 - c) if r & 1 else c
            p = 4 * px + 2 * py + pc
            sem = k * (N_DEV - 1) + r - 1
            sends.append(pltpu.make_async_remote_copy(
                src_ref=src(k, p), dst_ref=land_refs[k].at[me],
                send_sem=send_sems.at[sem], recv_sem=recv_sems.at[sem],
                device_id=(px, py, pc), device_id_type=pl.DeviceIdType.MESH))
            recvs.append(pltpu.make_async_remote_copy(
                src_ref=src(k, p), dst_ref=land_refs[k].at[p],
                send_sem=send_sems.at[sem], recv_sem=recv_sems.at[sem],
                device_id=(px, py, pc), device_id_type=pl.DeviceIdType.MESH))
    return local, sends, recvs


def exchange_start(mode, arrays, name, carry=()):
    n, nc = len(arrays), len(carry)
    land_shapes = [((N_DEV,) + tuple(a.shape)) if mode == "gather" else tuple(a.shape) for a in arrays]
    lands = [pltpu.with_memory_space_constraint(lax.empty(shp, a.dtype), pltpu.HBM)
             for shp, a in zip(land_shapes, arrays)]
    srcs = [pltpu.with_memory_space_constraint(a, pltpu.HBM) for a in arrays]
    carried = [pltpu.with_memory_space_constraint(a, pltpu.HBM) for a in carry]

    def body(*refs):
        src_refs, land_refs = refs[:n], refs[n:2 * n]
        first_out = 2 * n + nc
        send_sems, recv_sems, local_sems = refs[first_out:first_out + 3]
        token = refs[-1]
        local, sends, _ = _exchange_copies(mode, src_refs, land_refs, send_sems, recv_sems, local_sems)
        for cp in local + sends:
            cp.start()
        token[...] = jnp.zeros_like(token)

    n_sem = n * (N_DEV - 1)
    out = pl.pallas_call(
        body, name=name,
        out_shape=(pltpu.SemaphoreType.DMA((n_sem,)), pltpu.SemaphoreType.DMA((n_sem,)),
                   pltpu.SemaphoreType.DMA((n,)),
                   *[pltpu.HBM(a.shape, a.dtype) for a in arrays],
                   *[pltpu.HBM(shp, a.dtype) for shp, a in zip(land_shapes, arrays)],
                   *[pltpu.HBM(a.shape, a.dtype) for a in carry],
                   jax.ShapeDtypeStruct((8, LANES), F32)),
        in_specs=[_HBM] * (2 * n + nc),
        out_specs=(_SEM, _SEM, _SEM, *[_HBM] * (2 * n + nc), pl.BlockSpec(memory_space=pltpu.VMEM)),
        input_output_aliases={i: 3 + i for i in range(2 * n + nc)},
        compiler_params=pltpu.CompilerParams(has_side_effects=_EFFECT),
    )(*srcs, *lands, *carried)
    handle = dict(mode=mode, sems=out[:3], srcs=out[3:3 + n], lands=out[3 + n:3 + 2 * n])
    return handle, out[-1], list(out[3 + 2 * n:3 + 2 * n + nc])


def exchange_wait(handle, after, name):
    mode, srcs, lands = handle["mode"], list(handle["srcs"]), list(handle["lands"])
    n = len(srcs)

    def body(*refs):
        src_refs, land_refs = refs[:n], refs[n:2 * n]
        send_sems, recv_sems, local_sems = refs[2 * n:2 * n + 3]
        local, sends, recvs = _exchange_copies(mode, src_refs, land_refs, send_sems, recv_sems, local_sems)
        for cp in sends:
            cp.wait_send()
        for cp in recvs:
            cp.wait_recv()
        for cp in local:
            cp.wait()

    out = pl.pallas_call(
        body, name=name,
        out_shape=(*[pltpu.HBM(a.shape, a.dtype) for a in srcs], *[pltpu.HBM(a.shape, a.dtype) for a in lands]),
        in_specs=[_HBM] * (2 * n) + [_SEM] * 3 + [pl.BlockSpec(memory_space=pl.ANY)],
        out_specs=tuple([_HBM] * (2 * n)),
        input_output_aliases={i: i for i in range(2 * n)},
        compiler_params=pltpu.CompilerParams(has_side_effects=_EFFECT),
    )(*srcs, *lands, *handle["sems"], after)
    return list(out[n:])


_ICI_RELATIONS = (2, 4, 6)


def _mesh_place():
    x, y, c = lax.axis_index("x"), lax.axis_index("y"), lax.axis_index("c")

    def peer(r):
        px = (1 - x) if r & 4 else x
        py = (1 - y) if r & 2 else y
        pc = (1 - c) if r & 1 else c
        return (px, py, pc), 4 * px + 2 * py + pc

    return 4 * x + 2 * y + c, peer


def _remote(src, dst, send_sem, recv_sem, device):
    return pltpu.make_async_remote_copy(src_ref=src, dst_ref=dst, send_sem=send_sem, recv_sem=recv_sem,
                                        device_id=device, device_id_type=pl.DeviceIdType.MESH)


def gather2_start(groups, name):
    flat = [a for g in groups for a in g]
    n = len(flat)
    lands = [pltpu.with_memory_space_constraint(lax.empty((N_DEV,) + tuple(a.shape), a.dtype), pltpu.HBM) for a in flat]
    srcs = [pltpu.with_memory_space_constraint(a, pltpu.HBM) for a in flat]
    n_rel = 1 + len(_ICI_RELATIONS)

    def body(*refs):
        src_refs, land_refs = refs[:n], refs[n:2 * n]
        sem_refs = refs[2 * n:2 * n + 4 * len(groups)]
        me, peer = _mesh_place()
        k = 0
        for gi, g in enumerate(groups):
            send_sems, recv_sib, recv_ici, local_sems = sem_refs[4 * gi:4 * gi + 4]
            for j in range(len(g)):
                pltpu.make_async_copy(src_refs[k], land_refs[k].at[me], local_sems.at[j]).start()
                dev, _ = peer(1)
                _remote(src_refs[k], land_refs[k].at[me], send_sems.at[n_rel * j], recv_sib.at[j], dev).start()
                for t, r in enumerate(_ICI_RELATIONS):
                    dev, _ = peer(r)
                    _remote(src_refs[k], land_refs[k].at[me], send_sems.at[n_rel * j + 1 + t],
                            recv_ici.at[len(_ICI_RELATIONS) * j + t], dev).start()
                k += 1
        refs[-1][...] = jnp.zeros_like(refs[-1])

    sem_shapes = []
    for g in groups:
        sem_shapes += [pltpu.SemaphoreType.DMA((n_rel * len(g),)), pltpu.SemaphoreType.DMA((len(g),)),
                       pltpu.SemaphoreType.DMA((len(_ICI_RELATIONS) * len(g),)), pltpu.SemaphoreType.DMA((len(g),))]
    out = pl.pallas_call(
        body, name=name,
        out_shape=(*sem_shapes, *[pltpu.HBM(a.shape, a.dtype) for a in flat],
                   *[pltpu.HBM((N_DEV,) + tuple(a.shape), a.dtype) for a in flat],
                   jax.ShapeDtypeStruct((8, LANES), F32)),
        in_specs=[_HBM] * (2 * n),
        out_specs=(*[_SEM] * len(sem_shapes), *[_HBM] * (2 * n), pl.BlockSpec(memory_space=pltpu.VMEM)),
        input_output_aliases={i: len(sem_shapes) + i for i in range(2 * n)},
        compiler_params=pltpu.CompilerParams(has_side_effects=_EFFECT),
    )(*srcs, *lands)
    handles, k, base = [], 0, len(sem_shapes)
    for gi, g in enumerate(groups):
        handles.append(dict(sems=out[4 * gi:4 * gi + 4], srcs=out[base + k:base + k + len(g)],
                            lands=out[base + n + k:base + n + k + len(g)]))
        k += len(g)
    return handles, out[-1]


def gather2_forward(handle, after, name, carry=()):
    lands, nc = list(handle["lands"]), len(carry)
    n, n_ici = len(lands), len(_ICI_RELATIONS)
    carried = [pltpu.with_memory_space_constraint(a, pltpu.HBM) for a in carry]

    def body(*refs):
        land_refs = refs[:n]
        recv_ici = refs[n + nc]
        fwd_send, fwd_recv = refs[n + nc + 2], refs[n + nc + 3]
        me, peer = _mesh_place()
        sibling, _ = peer(1)
        for j in range(n):
            for t, r in enumerate(_ICI_RELATIONS):
                dev, p = peer(r)
                landed = land_refs[j].at[p]
                _remote(landed, landed, fwd_send.at[n_ici * j + t], recv_ici.at[n_ici * j + t], dev).wait_recv()
                _remote(landed, landed, fwd_send.at[n_ici * j + t], fwd_recv.at[n_ici * j + t], sibling).start()

    out = pl.pallas_call(
        body, name=name,
        out_shape=(pltpu.SemaphoreType.DMA((n_ici * n,)), pltpu.SemaphoreType.DMA((n_ici * n,)),
                   *[pltpu.HBM(a.shape, a.dtype) for a in lands], *[pltpu.HBM(a.shape, a.dtype) for a in carry]),
        in_specs=[_HBM] * (n + nc) + [_SEM, pl.BlockSpec(memory_space=pl.ANY)],
        out_specs=(_SEM, _SEM, *[_HBM] * (n + nc)),
        input_output_aliases={i: 2 + i for i in range(n + nc)},
        compiler_params=pltpu.CompilerParams(has_side_effects=_EFFECT),
    )(*lands, *carried, handle["sems"][2], after)
    new_handle = dict(sems=handle["sems"], srcs=handle["srcs"], lands=out[2:2 + n], fwd=out[:2])
    return new_handle, list(out[2 + n:])


def gather2_wait(handle, after, name):
    srcs, lands = list(handle["srcs"]), list(handle["lands"])
    n, n_ici = len(srcs), len(_ICI_RELATIONS)
    n_rel = 1 + n_ici
    send_all, recv_sibling, _, local_all = handle["sems"]

    def body(*refs):
        src_refs, land_refs = refs[:n], refs[n:2 * n]
        send_sems, recv_sib, local_sems, fwd_send, fwd_recv = refs[2 * n:2 * n + 5]
        me, peer = _mesh_place()
        sibling, sib = peer(1)
        for j in range(n):
            pltpu.make_async_copy(src_refs[j], land_refs[j].at[me], local_sems.at[j]).wait()
            _remote(src_refs[j], land_refs[j].at[sib], send_sems.at[n_rel * j], recv_sib.at[j], sibling).wait()
            for t, r in enumerate(_ICI_RELATIONS):
                dev, p = peer(r)
                _remote(src_refs[j], land_refs[j].at[me], send_sems.at[n_rel * j + 1 + t],
                        recv_sib.at[j], dev).wait_send()
                _, p_sib = peer(r ^ 1)
                _remote(land_refs[j].at[p], land_refs[j].at[p_sib], fwd_send.at[n_ici * j + t],
                        fwd_recv.at[n_ici * j + t], sibling).wait()

    out = pl.pallas_call(
        body, name=name,
        out_shape=(*[pltpu.HBM(a.shape, a.dtype) for a in srcs], *[pltpu.HBM(a.shape, a.dtype) for a in lands]),
        in_specs=[_HBM] * (2 * n) + [_SEM] * 5 + [pl.BlockSpec(memory_space=pl.ANY)],
        out_specs=tuple([_HBM] * (2 * n)),
        input_output_aliases={i: i for i in range(2 * n)},
        compiler_params=pltpu.CompilerParams(has_side_effects=_EFFECT),
    )(*srcs, *lands, send_all, recv_sibling, local_all, *handle["fwd"], after)
    return list(out[n:])


BIG = ["mla_w_in", "mla_w_uq", "mla_w_ukv", "mla_w_o", "gdn_w_in", "gdn_w_o", "sc_w_in", "sc_w_o",
       "xa_w_q", "xa_w_kv", "xa_w_o", "mlp_w1", "mlp_w2"]
TINY = [("mla_q_norm", 1), ("mla_kv_norm", 1), ("gdn_conv_w", 2), ("sc_conv_w", 2)]
REPL = ["gdn_a_log", "gdn_dt_bias", "gdn_o_norm", "norm_mix", "norm_mem", "norm_mlp", "mem_norm", "final_norm"]
WEIGHTS = ["mla_w_in", "mla_q_norm", "mla_kv_norm", "mla_w_uq", "mla_w_ukv", "mla_w_o", "gdn_w_in",
           "gdn_conv_w", "gdn_a_log", "gdn_dt_bias", "gdn_o_norm", "gdn_w_o", "sc_w_in", "sc_conv_w",
           "sc_w_o", "norm_mix", "norm_mem", "norm_mlp", "xa_w_q", "xa_w_kv", "xa_w_o", "mlp_w1",
           "mlp_w2", "mem_norm", "final_norm"]
MIXER_WEIGHTS = (["mla_w_in", "mla_w_uq", "mla_w_ukv", "mla_w_o"], ["gdn_w_in", "gdn_w_o"], ["sc_w_in", "sc_w_o"])
MIXER_PARAMS = (["norm_mem", "mla_q_norm", "mla_kv_norm"],
                ["norm_mem", "gdn_conv_w", "gdn_a_log", "gdn_dt_bias", "gdn_o_norm"],
                ["norm_mem", "sc_conv_w"])


def from_shards(a8, axis):
    a = jnp.moveaxis(a8, 0, axis)
    shp = a.shape
    return a.reshape(shp[:axis] + (shp[axis] * shp[axis + 1],) + shp[axis + 2:])


def pack_rows(flat_list, width, row_mult):
    total = sum(a.shape[-1] for a in flat_list)
    rows = -(-total // width)
    rows = -(-rows // row_mult) * row_mult
    pad = rows * width - total
    parts = list(flat_list)
    if pad:
        parts.append(jnp.zeros((pad,), flat_list[0].dtype))
    return jnp.concatenate(parts, axis=-1).reshape(rows, width)


def unpack_rows(packed, shapes):
    lead = packed.shape[:-2]
    flat = packed.reshape(lead + (-1,))
    out, off = [], 0
    for shp in shapes:
        n = math.prod(shp)
        out.append(flat[..., off:off + n].reshape(lead + tuple(shp)))
        off += n
    return out


def _swap_halves(w):
    half = w.shape[-1] // 2
    return jnp.concatenate([w[..., half:], w[..., :half]], axis=-1)


def _pad_last(w, n):
    return jnp.pad(w, [(0, 0)] * (w.ndim - 1) + [(0, n - w.shape[-1])])


def _unblock(w8):
    return jnp.transpose(w8, (1, 0, 2)).reshape(w8.shape[1], -1)


def _stack_rows(w8):
    return w8.reshape(-1, w8.shape[-1])


def rms_op(name, rows, d, out_dtype):
    tm = rows if rows * d * 4 <= BLOCK_BYTES else 512
    return make_tile_op(fn_rms, name, ["row", "par"], [True, True], [("row", d, out_dtype)], rows, min(tm, rows))


def seg_memory(p, mem):
    return rms_op("rms_memory", mem.shape[0], mem.shape[1], BF16)(mem, p["mem_norm"].reshape(1, -1))[0]


def seg_mixer(i, wts, p, x, h, rope_c, rope_s):
    s, d = x.shape
    j, kind = i // N_MIXERS, i % N_MIXERS
    tag = f"l{i}"
    hd = MLA_NOPE
    next_gain = p["norm_mem"][i].reshape(1, d)
    if kind == 0:
        w_in = _stack_rows(wts["mla_w_in"])
        w_cq = w_in[:, :MLA_Q_RANK]
        w_ckv = w_in[:, MLA_Q_RANK:MLA_Q_RANK + MLA_KV_RANK]
        w_kr = w_in[:, MLA_Q_RANK + MLA_KV_RANK:]
        c_qn, c_kvn, k_rope = make_tile_op(
            fn_mla_down, tag + "_mla_down", ["row", "row", "row"] + ["par"] * 6, [True, False, False] + [True] * 6,
            [("row", MLA_Q_RANK, BF16), ("row", MLA_KV_RANK, BF16), ("row", hd, F32)], s, 512)(
            h, rope_c, rope_s, w_cq, w_ckv, _pad_last(w_kr, hd), _pad_last(_swap_halves(w_kr), hd),
            p["mla_q_norm"][j].reshape(1, -1), p["mla_kv_norm"][j].reshape(1, -1))
        w_uq8 = wts["mla_w_uq"]
        w_qn = _unblock(w_uq8[:, :, :MLA_NOPE])
        w_qr = w_uq8[:, :, MLA_NOPE:]
        w_qr_p = _unblock(_pad_last(w_qr, hd))
        w_qr_s = _unblock(_pad_last(_swap_halves(w_qr), hd))
        nq = MLA_HEADS * hd
        q_nope, q_rope, kv = make_tile_op(
            fn_mla_up, tag + "_mla_up", ["row", "row", "row", "row"] + ["par"] * 4, [True, True, False, False] + [True] * 4,
            [("row", nq, BF16), ("row", nq, F32), ("row", 2 * nq, BF16)], s, 512)(
            c_qn, c_kvn, rope_c, rope_s, w_qn, w_qr_p, w_qr_s, _unblock(wts["mla_w_ukv"]))
        n_groups = MLA_QUERY_GROUPS if s % (MLA_QUERY_GROUPS * 256) == 0 else 1
        rows_g = s // n_groups
        o_groups = []
        for grp in range(n_groups):
            r0, r1 = grp * rows_g, (grp + 1) * rows_g
            o_groups.append(make_tile_op(
                fn_mla_attn, f"{tag}_mla_attn{grp}", [("rowh", hd), ("rowh", hd), ("parh", hd, 2), "par"],
                [True] * 4, [(("rowh", hd), nq, BF16)], rows_g, 256, MLA_HEADS, row_base=r0)(
                q_nope[r0:r1], q_rope[r0:r1], kv[:r1], k_rope[:r1])[0])
        o = jnp.concatenate(o_groups, axis=0)
        return make_mm_res_rms(tag + "_mla_o")(x, o, _stack_rows(wts["mla_w_o"]), next_gain)
    if kind == 1:
        ng = GDN_HEADS * GDN_DK
        w_in = _unblock(wts["gdn_w_in"])
        qkv_pre = make_mm(tag + "_gdn_in_qkv", F32)(h, w_in[:, :3 * ng])
        qkv_conv = make_conv(tag + "_gdn_conv")(qkv_pre, p["gdn_conv_w"][j])
        gate = make_mm(tag + "_gdn_in_g", F32)(h, w_in[:, 3 * ng:4 * ng])
        ba = make_mm(tag + "_gdn_in_ba", F32)(h, _pad_last(w_in[:, 4 * ng:], LANES))
        heads_row = ("row", GDN_DK, GDN_HEADS)
        q, k, v = make_tile_op(fn_gdn_prep, tag + "_gdn_prep", [("row", GDN_DK, 3 * GDN_HEADS)], [True],
                               [(heads_row, ng, F32)] * 3, s, 256)(qkv_conv)
        alog = jnp.pad(p["gdn_a_log"][j].reshape(1, -1), ((0, 0), (GDN_HEADS, LANES - 2 * GDN_HEADS)))
        dtb = jnp.pad(p["gdn_dt_bias"][j].reshape(1, -1), ((0, 0), (GDN_HEADS, LANES - 2 * GDN_HEADS)))
        beta_b, g_b = make_tile_op(fn_gdn_gates, tag + "_gdn_gates", ["row", "par", "par"], [True] * 3,
                                   [("row", ng, F32)] * 2, s, 512)(ba, alog, dtb)
        o = make_gdn(tag + "_gdn_core")(q, k, v, g_b, beta_b)
        o = make_tile_op(fn_gdn_out, tag + "_gdn_out", [heads_row, heads_row, "par"],
                         [True] * 3, [(heads_row, ng, BF16)], s, 512)(
            o, gate, p["gdn_o_norm"][j].reshape(1, -1))[0]
        return make_mm_res_rms(tag + "_gdn_o")(x, o, _stack_rows(wts["gdn_w_o"]), next_gain)
    w_in = _unblock(wts["sc_w_in"])
    b_gate = make_mm(tag + "_sc_in_b", F32)(h, w_in[:, :d])
    c_gate = make_mm(tag + "_sc_in_c", F32)(h, w_in[:, d:2 * d])
    u = make_mm(tag + "_sc_in_u", F32)(h, w_in[:, 2 * d:])
    yv = make_gated_conv(tag + "_sc_conv")(b_gate, c_gate, u, p["sc_conv_w"][j])
    return make_mm_res_rms(tag + "_sc_o")(x, yv, _stack_rows(wts["sc_w_o"]), next_gain)


def seg_xattn(i, wts, p, x, hx, mem_n):
    s, d = x.shape
    tag = f"l{i}"
    kv = make_mm(tag + "_xa_kv", BF16, blocked=True)(mem_n, wts["xa_w_kv"])
    heads = ("row", X_HEAD_DIM, X_HEADS)
    o = make_tile_op(fn_xattn, tag + "_xattn",
                     ["row", ("par", X_HEAD_DIM, X_HEADS), ("par", X_HEAD_DIM, 2 * X_HEADS)], [True] * 3,
                     [(heads, d, BF16)], s, 512)(hx, _stack_rows(wts["xa_w_q"]), kv)[0]
    return make_mm_res_rms(tag + "_xa_o")(x, o, _stack_rows(wts["xa_w_o"]), p["norm_mlp"][i].reshape(1, d))


def seg_mlp(i, wts, p, x, hm):
    d = x.shape[1]
    gain = p["norm_mix"][i + 1].reshape(1, d) if i + 1 < DEPTH else None
    return make_mlp(f"l{i}_mlp", gain is not None)(x, hm, wts["mlp_w1"], _stack_rows(wts["mlp_w2"]), gain)


def segments():
    segs = []
    for i in range(DEPTH):
        j, kind = i // N_MIXERS, i % N_MIXERS
        segs.append((f"l{i}_mixer", [(n, j) for n in MIXER_WEIGHTS[kind]], MIXER_PARAMS[kind], "mixer"))
        segs.append((f"l{i}_xattn", [(n, i) for n in ("xa_w_q", "xa_w_kv", "xa_w_o")], ["norm_mlp"], "xattn"))
        segs.append((f"l{i}_mlp", [(n, i) for n in ("mlp_w1", "mlp_w2")], ["norm_mix"] if i + 1 < DEPTH else [],
                     "mlp"))
    return segs


def run_segment(index, kind, wts, p, x, h, mem_n, rope_c, rope_s):
    layer = index // 3
    if kind == "mixer":
        return seg_mixer(layer, wts, p, x, h, rope_c, rope_s)
    if kind == "xattn":
        return seg_xattn(layer, wts, p, x, h, mem_n)
    return seg_mlp(layer, wts, p, x, h)


def rope_tables(positions):
    inv_freq = ROPE_THETA ** (-jnp.arange(0, MLA_ROPE, 2, dtype=F32) / MLA_ROPE)
    ang = positions.astype(F32)[:, None] * inv_freq
    cos, sin = jnp.cos(ang), jnp.sin(ang)
    zeros = jnp.zeros((positions.shape[0], MLA_NOPE - MLA_ROPE), F32)
    return jnp.concatenate([cos, cos, zeros], axis=-1), jnp.concatenate([-sin, sin, zeros], axis=-1)


def kernel(x, mem, positions, mla_w_in, mla_q_norm, mla_kv_norm, mla_w_uq, mla_w_ukv, mla_w_o, gdn_w_in, gdn_conv_w, gdn_a_log, gdn_dt_bias, gdn_o_norm, gdn_w_o, sc_w_in, sc_conv_w, sc_w_o, norm_mix, norm_mem, norm_mlp, xa_w_q, xa_w_kv, xa_w_o, mlp_w1, mlp_w2, mem_norm, final_norm, loss_target, m_mla_w_in, m_mla_q_norm, m_mla_kv_norm, m_mla_w_uq, m_mla_w_ukv, m_mla_w_o, m_gdn_w_in, m_gdn_conv_w, m_gdn_a_log, m_gdn_dt_bias, m_gdn_o_norm, m_gdn_w_o, m_sc_w_in, m_sc_conv_w, m_sc_w_o, m_norm_mix, m_norm_mem, m_norm_mlp, m_xa_w_q, m_xa_w_kv, m_xa_w_o, m_mlp_w1, m_mlp_w2, m_mem_norm, m_final_norm, v_mla_w_in, v_mla_q_norm, v_mla_kv_norm, v_mla_w_uq, v_mla_w_ukv, v_mla_w_o, v_gdn_w_in, v_gdn_conv_w, v_gdn_a_log, v_gdn_dt_bias, v_gdn_o_norm, v_gdn_w_o, v_sc_w_in, v_sc_conv_w, v_sc_w_o, v_norm_mix, v_norm_mem, v_norm_mlp, v_xa_w_q, v_xa_w_kv, v_xa_w_o, v_mlp_w1, v_mlp_w2, v_mem_norm, v_final_norm):
    args = locals()
    w_loc = {n: args[n] for n in WEIGHTS}
    m_loc = {n: args["m_" + n] for n in WEIGHTS}
    v_loc = {n: args["v_" + n] for n in WEIGHTS}
    me = 4 * lax.axis_index("x") + 2 * lax.axis_index("y") + lax.axis_index("c")
    segs = segments()

    w16 = {n: w_loc[n].astype(BF16) for n in BIG}
    tiny_pack = pack_rows([w_loc[n].reshape(-1) for n, _ in TINY], LANES, 8)
    gather_handles, token = gather2_start(
        [[tiny_pack]] + [[w16[n][layer] for n, layer in units] for _, units, _, _ in segs], "gather_start")

    x_cur = x[0]
    rope_c, rope_s = rope_tables(positions[0])
    tiny_handle, _ = gather2_forward(gather_handles[0], token, "gather_forward_tiny")
    tiny_all = gather2_wait(tiny_handle, token, "gather_wait_tiny")[0]
    gather_handles = gather_handles[1:]
    params = {}
    for (n, ax), a8 in zip(TINY, unpack_rows(tiny_all, [w_loc[n].shape for n, _ in TINY])):
        params[n] = from_shards(a8, ax)
    for n in REPL:
        params[n] = w_loc[n]

    mem_n, vjp_memory = jax.vjp(lambda p_: seg_memory(p_, mem[0]), {"mem_norm": params["mem_norm"]})
    h_cur, vjp_first_norm = jax.vjp(
        lambda p_, x_: rms_op("l0_rms_mix", x_.shape[0], x_.shape[1], BF16)(x_, p_["norm_mix"][0].reshape(1, -1))[0],
        {"norm_mix": params["norm_mix"]}, x_cur)
    vjps = []
    forwarded, _ = gather2_forward(gather_handles[0], token, f"gather_forward_{segs[0][0]}")
    for index, (tag, units, p_names, kind) in enumerate(segs):
        landed = gather2_wait(forwarded, token if index == 0 else x_cur, f"gather_wait_{tag}")
        wts = {n: a for (n, _), a in zip(units, landed)}
        p_seg = {n: params[n] for n in p_names}
        if index + 1 < len(segs):
            forwarded, (p_seg[p_names[0]],) = gather2_forward(
                gather_handles[index + 1], landed[0], f"gather_forward_{segs[index + 1][0]}",
                carry=[p_seg[p_names[0]]])
        outs, vjp_seg = jax.vjp(
            lambda w_, p_, x_, h_, m_, index=index, kind=kind:
            run_segment(index, kind, w_, p_, x_, h_, m_, rope_c, rope_s),
            wts, p_seg, x_cur, h_cur, mem_n)
        x_cur, h_cur = outs[0], (outs[1] if len(outs) > 1 else None)
        vjps.append(vjp_seg)

    loss_vec, g_x, d_final = loss_head(x_cur, params["final_norm"].reshape(1, -1), loss_target[0], "loss_head")

    grads = {n: jnp.zeros_like(params[n]) for n in params}
    grads["final_norm"] = d_final.reshape(-1)
    g_mem_n = jnp.zeros_like(mem_n)
    g_h = None
    scatter_handles = []
    for (tag, units, _, _), vjp_seg in zip(reversed(segs), reversed(vjps)):
        g_wts, g_p, g_x, g_h, g_m = vjp_seg((g_x,) if g_h is None else (g_x, g_h))
        for n, g in g_p.items():
            grads[n] = grads[n] + g
        g_mem_n = g_mem_n + g_m
        handle, _, (g_h,) = exchange_start("scatter", [g_wts[n] for n, _ in units], f"scatter_start_{tag}",
                                           carry=[g_h])
        scatter_handles.append((units, handle))
    grads["mem_norm"] = grads["mem_norm"] + vjp_memory(g_mem_n)[0]["mem_norm"]
    g_first, g_x_norm = vjp_first_norm(g_h)
    grads["norm_mix"] = grads["norm_mix"] + g_first["norm_mix"]
    g_x = g_x + g_x_norm

    small_names = [n for n, _ in TINY] + REPL
    small_g = pack_rows([loss_vec[0, :1]] + [grads[n].astype(F32).reshape(-1) for n in small_names], PACK_W, 8)
    small_handle, _, _ = exchange_start("gather", [small_g], "gather_start_small_grads")

    g_recv = {}
    for units, handle in scatter_handles:
        landed = exchange_wait(handle, g_x, f"scatter_wait_{units[0][0]}_{units[0][1]}")
        g_recv.update(dict(zip(units, landed)))

    res = {}
    for n in BIG:
        outs = None
        for layer in range(w_loc[n].shape[0]):
            outs = adamw(g_recv[n, layer], w_loc[n], m_loc[n], v_loc[n], layer, outs, f"adamw_{n}_{layer}")
        for kind, a in zip(("grad", "delta", "m", "v"), outs):
            res[(kind, n)] = a
    small_recv = exchange_wait(small_handle, res[("grad", BIG[-1])], "gather_wait_small_grads")[0]

    def full_small(d):
        parts = [jnp.zeros((1,), F32)]
        for n, ax in TINY:
            full_shape = params[n].shape
            start = [0] * len(full_shape)
            start[ax] = me * d[n].shape[ax]
            parts.append(lax.dynamic_update_slice(jnp.zeros(full_shape, F32), d[n], start).reshape(-1))
        parts += [d[n].reshape(-1) for n in REPL]
        return pack_rows(parts, PACK_W, 8)

    outs_small = adamw(small_recv, full_small(w_loc)[None], full_small(m_loc)[None], full_small(v_loc)[None],
                       0, None, "adamw_small")
    small_shapes = [(1,)] + [params[n].shape for n, _ in TINY] + [w_loc[n].shape for n in REPL]
    loss = None
    for kind, packed in zip(("grad", "delta", "m", "v"), outs_small):
        parts = unpack_rows(packed[0], small_shapes)
        if kind == "grad":
            loss = parts[0][0]
        for (n, ax), a in zip(TINY, parts[1:1 + len(TINY)]):
            start = [0] * a.ndim
            start[ax] = me * w_loc[n].shape[ax]
            res[(kind, n)] = lax.dynamic_slice(a, start, w_loc[n].shape)
        for n, a in zip(REPL, parts[1 + len(TINY):]):
            res[(kind, n)] = a

    out = [loss, g_x[None]]
    for kind in ("grad", "delta", "m", "v"):
        out += [res[(kind, n)] for n in WEIGHTS]
    return tuple(out)
```

```python
import math

import jax
import jax.numpy as jnp
from jax import lax
from jax.experimental import pallas as pl
from jax.experimental.pallas import tpu as pltpu

F32 = jnp.float32
BF16 = jnp.bfloat16

N_DEV = 8
LANES = 128
EPS = 1e-6
ROPE_THETA = 10000.0
MLA_HEADS, MLA_NOPE, MLA_ROPE, MLA_V = 8, 128, 64, 128
MLA_Q_RANK, MLA_KV_RANK = 384, 256
GDN_HEADS, GDN_DK, GDN_CONV, GDN_CHUNK = 8, 128, 4, 64
X_HEADS, X_HEAD_DIM = 4, 256
DEPTH, N_MIXERS = 4, 3
ADAM_LR, ADAM_B1, ADAM_B2, ADAM_EPS, ADAM_WD, ADAM_STEP = 0.001, 0.9, 0.999, 1e-08, 0.01, 10
MLA_QUERY_GROUPS = 4
NEG_BIG = -1e30
PACK_W = 1024


_NN = (((1,), (0,)), ((), ()))
_NT = (((1,), (1,)), ((), ()))
_TN = (((0,), (0,)), ((), ()))
_NN3 = (((2,), (1,)), ((0,), (0,)))
_NT3 = (((2,), (2,)), ((0,), (0,)))
_TN3 = (((1,), (1,)), ((0,), (0,)))


def _dot(a, b, dims):
    return lax.dot_general(a, b, dims, preferred_element_type=F32)


def _hi_lo(x):
    hi = x.astype(BF16)
    return hi, (x - hi.astype(F32)).astype(BF16)


def _split3(x):
    hi = x.astype(BF16)
    r = x - hi.astype(F32)
    mid = r.astype(BF16)
    return hi, mid, (r - mid.astype(F32)).astype(BF16)


def _dg(a, b, dims, prec):
    if prec == "h":
        return lax.dot_general(a, b, dims, precision=lax.Precision.HIGHEST, preferred_element_type=F32)
    if prec == "m":
        a_hi, a_lo = _hi_lo(a)
        b_hi, b_lo = _hi_lo(b)
        return _dot(a_hi, b_hi, dims) + _dot(a_hi, b_lo, dims) + _dot(a_lo, b_hi, dims)
    return _dot(a.astype(BF16), b.astype(BF16), dims)


def _dg_sel(sel, x, dims, sel_first):
    s16 = sel.astype(BF16)
    parts = [(_dot(s16, piece, dims) if sel_first else _dot(piece, s16, dims)) for piece in _split3(x)]
    return parts[0] + parts[1] + parts[2]


class _Ops:
    def __init__(self, prec, differentiable, batched=False):
        d_nn, d_nt, d_tn = (_NN3, _NT3, _TN3) if batched else (_NN, _NT, _TN)

        def nn(a, b):
            return _dg(a, b, d_nn, prec)

        def nt(a, b):
            return _dg(a, b, d_nt, prec)

        def tn(a, b):
            return _dg(a, b, d_tn, prec)

        if differentiable:
            dnn = jax.custom_vjp(nn)
            dnn.defvjp(lambda a, b: (nn(a, b), (a, b)), lambda r, g: (nt(g, r[1]), tn(r[0], g)))
            dnt = jax.custom_vjp(nt)
            dnt.defvjp(lambda a, b: (nt(a, b), (a, b)), lambda r, g: (nn(g, r[1]), tn(g, r[0])))
            dtn = jax.custom_vjp(tn)
            dtn.defvjp(lambda a, b: (tn(a, b), (a, b)), lambda r, g: (nt(r[1], g), nn(r[0], g)))
            nn, nt, tn = dnn, dnt, dtn
        self.nn, self.nt, self.tn = nn, nt, tn


class _SelOps:
    def __init__(self, differentiable, batched=False):
        d_nn, d_nt, d_tn = (_NN3, _NT3, _TN3) if batched else (_NN, _NT, _TN)

        def sel_nn(sel, x):
            return _dg_sel(sel, x, d_nn, True)

        def sel_nt(sel, x):
            return _dg_sel(sel, x, d_nt, True)

        if differentiable:
            dnn = jax.custom_vjp(sel_nn)
            dnn.defvjp(lambda s, x: (sel_nn(s, x), s),
                       lambda s, g: (jnp.zeros_like(s), _dg_sel(s, g, d_tn, True)))
            dnt = jax.custom_vjp(sel_nt)
            dnt.defvjp(lambda s, x: (sel_nt(s, x), s),
                       lambda s, g: (jnp.zeros_like(s), _dg_sel(s, g, d_tn, False)))
            sel_nn, sel_nt = dnn, dnt
        self.sel_nn, self.sel_nt = sel_nn, sel_nt


class _OpSet:
    def __init__(self, differentiable):
        self.b = _Ops("b", differentiable)
        self.h = _Ops("h", differentiable)
        self.bb = _Ops("b", differentiable, batched=True)
        self.bm = _Ops("m", differentiable, batched=True)
        self.bs = _SelOps(differentiable, batched=True)


_PLAIN = _OpSet(False)
_DIFF = _OpSet(True)


def _params(sem):
    return pltpu.CompilerParams(dimension_semantics=sem)


BLOCK_BYTES = 4 * 1024 * 1024


def _pick(n, cands):
    for c in cands:
        if n % c == 0:
            return c
    return n


def _tile(n, cap):
    if n <= cap:
        return n
    return _pick(n, tuple(c for c in (2048, 1024, 768, 512, 384, 256, 128) if c <= cap))


def matmul(a, b, form, out_dtype, name, res=None, blocked=False, relu_gate=None, rms_gain=None, a_relu2=False):
    if form == "nn":
        m, k = a.shape
        k2, n = (b.shape[1], N_DEV * b.shape[2]) if blocked else b.shape
    elif form == "nt":
        m, k = a.shape
        n, k2 = (b.shape[1], N_DEV * b.shape[2]) if blocked else b.shape
    else:
        (k, m), (k2, n) = a.shape, b.shape
    assert k == k2, (a.shape, b.shape, form)
    tk = k if k <= 2048 else _tile(k, 1024)
    cb = nb = 1
    if blocked:
        cb = (k if form == "nt" else n) // N_DEV
        nb = _pick(N_DEV, tuple(c for c in (8, 4, 2, 1) if c * cb <= 1024))
    if blocked and form == "nt":
        tk = nb * cb
    if blocked and form != "nt":
        tn = nb * cb
    else:
        tn = _tile(n, min(1024, BLOCK_BYTES // (tk * b.dtype.itemsize)))
    out_elems = BLOCK_BYTES // 2 if (out_dtype == BF16 and res is None) else BLOCK_BYTES // 4
    tm = _tile(m, min(BLOCK_BYTES // (tk * a.dtype.itemsize), out_elems // tn))
    nk = k // tk
    dims = {"nn": _NN, "nt": _NT, "tn": _TN}[form]

    a_spec = {"nn": pl.BlockSpec((tm, tk), lambda i, j, kk: (i, kk)),
              "nt": pl.BlockSpec((tm, tk), lambda i, j, kk: (i, kk)),
              "tn": pl.BlockSpec((tk, tm), lambda i, j, kk: (kk, i))}[form]
    if blocked and form == "nn":
        b_spec = pl.BlockSpec((nb, tk, cb), lambda i, j, kk: (j, kk, 0))
    elif blocked and form == "nt":
        b_spec = pl.BlockSpec((nb, tn, cb), lambda i, j, kk: (kk, j, 0))
    else:
        b_spec = {"nn": pl.BlockSpec((tk, tn), lambda i, j, kk: (kk, j)),
                  "nt": pl.BlockSpec((tn, tk), lambda i, j, kk: (j, kk)),
                  "tn": pl.BlockSpec((tk, tn), lambda i, j, kk: (kk, j))}[form]
    c_spec = pl.BlockSpec((tm, tn), lambda i, j, kk: (i, j))
    out_shape = jax.ShapeDtypeStruct((m, n), out_dtype)
    o_spec = c_spec
    blocked_out = blocked and form == "tn"
    if blocked_out:
        out_shape = jax.ShapeDtypeStruct((N_DEV, m, cb), out_dtype)
        o_spec = pl.BlockSpec((nb, tm, cb), lambda i, j, kk: (j, i, 0))
    has_res, has_gate, has_gain = res is not None, relu_gate is not None, rms_gain is not None
    extras = [e for e in (res, relu_gate) if e is not None]
    n_in = 2 + len(extras) + has_gain
    second = has_gain
    assert not (second and (blocked_out or tn != n))

    def body(*refs):
        a_ref, b_ref = refs[0], refs[1]
        r_ref = refs[2] if has_res else None
        gate_ref = refs[2 + has_res] if has_gate else None
        gain_ref = refs[n_in - 1] if has_gain else None
        o_ref = refs[n_in]
        if a_relu2:
            relu = jnp.maximum(a_ref[...].astype(F32), 0.0)
            a_val = (relu * relu).astype(BF16)
        else:
            a_val = a_ref[...].astype(BF16)
        if blocked and form == "nn":
            part = jnp.concatenate([_dot(a_val, b_ref[t].astype(BF16), dims) for t in range(nb)], axis=-1)
        elif blocked and form == "nt":
            part = _dot(a_val[:, :cb], b_ref[0].astype(BF16), dims)
            for t in range(1, nb):
                part = part + _dot(a_val[:, t * cb:(t + 1) * cb], b_ref[t].astype(BF16), dims)
        else:
            part = _dot(a_val, b_ref[...].astype(BF16), dims)

        def finish(acc):
            if has_res:
                acc = acc + r_ref[...].astype(F32)
            if has_gate:
                acc = acc * (2.0 * jnp.maximum(gate_ref[...].astype(F32), 0.0))
            if blocked_out:
                for t in range(nb):
                    o_ref[t] = acc[:, t * cb:(t + 1) * cb].astype(out_dtype)
            else:
                o_ref[...] = acc.astype(out_dtype)
            if has_gain:
                refs[n_in + 1][...] = _rms(acc, gain_ref[...]).astype(BF16)

        if nk == 1:
            finish(part)
        else:
            acc_ref = refs[-1]
            kk = pl.program_id(2)

            @pl.when(kk == 0)
            def _():
                acc_ref[...] = part

            @pl.when(jnp.logical_and(kk > 0, kk < nk - 1))
            def _():
                acc_ref[...] += part

            @pl.when(kk == nk - 1)
            def _():
                finish(acc_ref[...] + part)

    in_specs = [a_spec, b_spec] + [c_spec] * len(extras)
    args = [a, b] + extras
    if has_gain:
        in_specs.append(pl.BlockSpec((1, tn), lambda i, j, kk: (0, j)))
        args.append(rms_gain)
    if second:
        out_shape = [out_shape, jax.ShapeDtypeStruct((m, n), BF16)]
        o_spec = [o_spec, c_spec]
    return pl.pallas_call(
        body, name=name,
        out_shape=out_shape,
        grid=(m // tm, n // tn, nk),
        in_specs=in_specs, out_specs=o_spec,
        scratch_shapes=[pltpu.VMEM((tm, tn), F32)] if nk > 1 else [],
        compiler_params=_params(("parallel", "parallel", "arbitrary")),
    )(*args)


def make_mm(name, out_dtype, with_res=False, blocked=False):
    def bwd_mm(a, w, g):
        da = matmul(g, w, "nt", a.dtype, name + "_da", blocked=blocked)
        dw = matmul(a, g, "tn", w.dtype, name + "_dw", blocked=blocked)
        return da, dw

    if with_res:
        @jax.custom_vjp
        def op(res, a, w):
            return matmul(a, w, "nn", out_dtype, name + "_f", res=res, blocked=blocked)

        def fwd(res, a, w):
            return op(res, a, w), (a, w)

        def bwd(saved, g):
            return (g,) + bwd_mm(*saved, g)
    else:
        @jax.custom_vjp
        def op(a, w):
            return matmul(a, w, "nn", out_dtype, name + "_f", blocked=blocked)

        def fwd(a, w):
            return op(a, w), (a, w)

        def bwd(saved, g):
            return bwd_mm(*saved, g)
    op.defvjp(fwd, bwd)
    return op


def _rms_fan_bwd(x_new, gain, dx, dh, name):
    rows, d = x_new.shape
    outs = [("row", d, F32), ("row", d, BF16)]
    return tile_bwd(fn_fan_rms, name, ["row", "par"], [x_new, gain], [True, True], outs, [dx, dh],
                    rows, min(512, rows), 0)


def make_mm_res_rms(name):
    @jax.custom_vjp
    def op(res, a, w, gain):
        return tuple(matmul(a, w, "nn", F32, name + "_f", res=res, rms_gain=gain))

    def fwd(res, a, w, gain):
        x_new, h = op(res, a, w, gain)
        return (x_new, h), (a, w, x_new, gain)

    def bwd(saved, cts):
        a, w, x_new, gain = saved
        dx, dgain = _rms_fan_bwd(x_new, gain, cts[0], cts[1], name + "_nb")
        da = matmul(dx, w, "nt", a.dtype, name + "_da")
        dw = matmul(a, dx, "tn", w.dtype, name + "_dw")
        return dx, da, dw, dgain

    op.defvjp(fwd, bwd)
    return op


def make_mlp(name, with_norm):
    def run(x, h, w1, w2, gain):
        a = matmul(h, w1, "nn", BF16, name + "_1_f", blocked=True)
        out = matmul(a, w2, "nn", F32, name + "_2_f", res=x, rms_gain=gain if with_norm else None, a_relu2=True)
        return (tuple(out) if with_norm else (out,)), a

    @jax.custom_vjp
    def op(x, h, w1, w2, gain):
        return run(x, h, w1, w2, gain)[0]

    def fwd(x, h, w1, w2, gain):
        out, a = run(x, h, w1, w2, gain)
        return out, (h, w1, w2, gain, a, out[0])

    def bwd(saved, cts):
        h, w1, w2, gain, a, x_new = saved
        if with_norm:
            dx, dgain = _rms_fan_bwd(x_new, gain, cts[0], cts[1], name + "_nb")
        else:
            dx, dgain = cts[0], None
        da = matmul(dx, w2, "nt", BF16, name + "_2_da", relu_gate=a)
        dw2 = matmul(a, dx, "tn", w2.dtype, name + "_2_dw", a_relu2=True)
        dh = matmul(da, w1, "nt", h.dtype, name + "_1_da", blocked=True)
        dw1 = matmul(h, da, "tn", w1.dtype, name + "_1_dw", blocked=True)
        return dx, dh, dw1, dw2, dgain

    op.defvjp(fwd, bwd)
    return op


def _kind(k):
    if isinstance(k, str):
        return k, None, 1
    return k[0], k[1], (k[2] if len(k) > 2 else 1)


def _tile_spec(kind, shape, tm, heads):
    k, d, ns = _kind(kind)
    if k == "row":
        return pl.BlockSpec((tm, shape[1]), (lambda h, i: (i, 0)) if heads else (lambda i: (i, 0)))
    if k == "par":
        return pl.BlockSpec(tuple(shape), (lambda h, i: (0, 0)) if heads else (lambda i: (0, 0)))
    if k == "rowh":
        return pl.BlockSpec((tm, d * ns), lambda h, i: (i, h))
    if k == "parh":
        return pl.BlockSpec((shape[0], d * ns), lambda h, i: (0, h))
    raise ValueError(kind)


def _tile_grid(rows, tm, heads):
    n_rows = rows // tm
    return ((heads, n_rows) if heads else (n_rows,)), (1 if heads else 0)


def _split_vals(kinds, refs):
    vals, counts = [], []
    for kind, r in zip(kinds, refs):
        _, d, ns = _kind(kind)
        v = r[...].astype(F32)
        vals += [v] if ns == 1 else [v[:, p * d:(p + 1) * d] for p in range(ns)]
        counts.append(ns)
    return vals, counts


def tile_fwd(fn, name, kinds, args, outs, rows, tm, heads, row_base=0):
    grid, row_axis = _tile_grid(rows, tm, heads)
    n_in = len(args)
    out_shapes = [jax.ShapeDtypeStruct((rows, w), dt) for (_, w, dt) in outs]

    def body(*refs):
        vals, _ = _split_vals(kinds, refs[:n_in])
        row0 = row_base + pl.program_id(row_axis) * tm
        res = list(fn(_PLAIN, row0, *vals))
        for o_ref, (k, _, _) in zip(refs[n_in:], outs):
            pieces = [res.pop(0) for _ in range(_kind(k)[2])]
            v = pieces[0] if len(pieces) == 1 else jnp.concatenate(pieces, axis=-1)
            o_ref[...] = v.astype(o_ref.dtype)

    return pl.pallas_call(
        body, name=name, out_shape=out_shapes, grid=grid,
        in_specs=[_tile_spec(k, a.shape, tm, heads) for k, a in zip(kinds, args)],
        out_specs=[_tile_spec(k, (rows, w), tm, heads) for (k, w, _) in outs],
        compiler_params=_params(("arbitrary",) * len(grid)),
    )(*args)


def tile_bwd(fn, name, kinds, args, diff, outs, cts, rows, tm, heads, row_base=0):
    grid, row_axis = _tile_grid(rows, tm, heads)
    n_in, n_ct = len(args), len(cts)
    diff_idx = [i for i, d in enumerate(diff) if d]
    g_shapes, g_specs = [], []
    for i in diff_idx:
        k = _kind(kinds[i])[0]
        dt = args[i].dtype if k in ("row", "rowh") else F32
        g_shapes.append(jax.ShapeDtypeStruct(args[i].shape, dt))
        g_specs.append(_tile_spec(kinds[i], args[i].shape, tm, heads))

    def body(*refs):
        in_refs, ct_refs, g_refs = refs[:n_in], refs[n_in:n_in + n_ct], refs[n_in + n_ct:]
        vals, counts = _split_vals(kinds, in_refs)
        first_piece = [sum(counts[:i]) for i in range(n_in)]
        flat_diff = [first_piece[i] + p for i in diff_idx for p in range(counts[i])]
        row_id = pl.program_id(row_axis)
        row0 = row_base + row_id * tm

        def f(*dvals):
            full = list(vals)
            for i, dv in zip(flat_diff, dvals):
                full[i] = dv
            return tuple(fn(_DIFF, row0, *full))

        _, vjp = jax.vjp(f, *[vals[i] for i in flat_diff])
        ct_vals, _ = _split_vals([k for (k, _, _) in outs], ct_refs)
        flat_grads = list(vjp(tuple(ct_vals)))
        for g_ref, i in zip(g_refs, diff_idx):
            pieces = [flat_grads.pop(0) for _ in range(counts[i])]
            g = pieces[0] if len(pieces) == 1 else jnp.concatenate(pieces, axis=-1)
            k = _kind(kinds[i])[0]
            if k in ("row", "rowh"):
                g_ref[...] = g.astype(g_ref.dtype)
            else:
                first = row_id == 0
                if heads and k == "par":
                    first = jnp.logical_and(first, pl.program_id(0) == 0)

                @pl.when(first)
                def _(g_ref=g_ref, g=g):
                    g_ref[...] = g

                @pl.when(jnp.logical_not(first))
                def _(g_ref=g_ref, g=g):
                    g_ref[...] += g

    return pl.pallas_call(
        body, name=name, out_shape=g_shapes, grid=grid,
        in_specs=[_tile_spec(k, a.shape, tm, heads) for k, a in zip(kinds, args)]
        + [_tile_spec(k, (rows, w), tm, heads) for (k, w, _) in outs],
        out_specs=g_specs,
        compiler_params=_params(("arbitrary",) * len(grid)),
    )(*args, *cts)


def make_tile_op(fn, name, kinds, diff, outs, rows, tm, heads=0, row_base=0):
    tm = min(tm, rows)

    @jax.custom_vjp
    def op(*args):
        return tuple(tile_fwd(fn, name + "_f", kinds, args, outs, rows, tm, heads, row_base))

    def fwd(*args):
        return op(*args), args

    def bwd(args, cts):
        grads = tile_bwd(fn, name + "_b", kinds, args, diff, outs, cts, rows, tm, heads, row_base)
        it = iter(grads)
        res = []
        for a, d in zip(args, diff):
            res.append(next(it).astype(a.dtype) if d else None)
        return tuple(res)

    op.defvjp(fwd, bwd)
    return op


def _rms(x, g):
    return x * lax.rsqrt(jnp.mean(x * x, axis=-1, keepdims=True) + EPS) * g


def fn_rms(ops, row0, x, g):
    return (_rms(x, g),)


def fn_fan_rms(ops, row0, x, g):
    return x, _rms(x, g)


def fn_mla_down(ops, row0, h, c, s, w_cq, w_ckv, w_kr, w_krs, g_q, g_kv):
    c_qn = _rms(ops.b.nn(h, w_cq), g_q)
    c_kvn = _rms(ops.b.nn(h, w_ckv), g_kv)
    return c_qn, c_kvn, ops.b.nn(h, w_kr) * c + ops.b.nn(h, w_krs) * s


def fn_mla_up(ops, row0, c_qn, c_kvn, c, s, w_qn, w_qr, w_qrs, w_kv):
    c_all = jnp.concatenate([c] * MLA_HEADS, axis=-1)
    s_all = jnp.concatenate([s] * MLA_HEADS, axis=-1)
    q_rope = ops.b.nn(c_qn, w_qr) * c_all + ops.b.nn(c_qn, w_qrs) * s_all
    return ops.b.nn(c_qn, w_qn), q_rope, ops.b.nn(c_kvn, w_kv)


def _softmax(s):
    m = lax.stop_gradient(jnp.max(s, axis=-1, keepdims=True))
    e = jnp.exp(s - m)
    return e / jnp.sum(e, axis=-1, keepdims=True)


def fn_xattn(ops, row0, hx, *t):
    w_q, k, v = t[:X_HEADS], t[X_HEADS:2 * X_HEADS], t[2 * X_HEADS:]
    outs = []
    for w_h, k_h, v_h in zip(w_q, k, v):
        s = ops.b.nt(ops.b.nn(hx, w_h), k_h) * (X_HEAD_DIM ** -0.5)
        outs.append(ops.b.nn(_softmax(s), v_h))
    return tuple(outs)


def _silu(x):
    return x * jax.nn.sigmoid(x)


def fn_gdn_prep(ops, row0, *t):
    nh = len(t) // 3
    qs, ks, vs = [], [], []
    for qc, kc, vc in zip(t[:nh], t[nh:2 * nh], t[2 * nh:]):
        q, k = _silu(qc), _silu(kc)
        qs.append(q * lax.rsqrt(jnp.sum(q * q, -1, keepdims=True) + EPS) * (GDN_DK ** -0.5))
        ks.append(k * lax.rsqrt(jnp.sum(k * k, -1, keepdims=True) + EPS))
        vs.append(_silu(vc))
    return tuple(qs + ks + vs)


def fn_gdn_gates(ops, row0, ba, alog, dtb):
    width = GDN_HEADS * GDN_DK
    beta = jax.nn.sigmoid(ba)
    z = ba + dtb
    softplus = jnp.maximum(z, 0.0) + jnp.log1p(jnp.exp(-jnp.abs(z)))
    g = -jnp.exp(alog) * softplus
    r = lax.broadcasted_iota(jnp.int32, (LANES, width), 0)
    c = lax.broadcasted_iota(jnp.int32, (LANES, width), 1) // GDN_DK
    e_beta = (r == c).astype(F32)
    e_g = (r == c + GDN_HEADS).astype(F32)
    return ops.h.nn(beta, e_beta), ops.h.nn(g, e_g)


def fn_gdn_out(ops, row0, *t):
    nh = (len(t) - 1) // 2
    g = t[-1]
    return tuple(_rms(o, g) * _silu(gate) for o, gate in zip(t[:nh], t[nh:2 * nh]))


def fn_mla_attn(ops, row0, qn, qr, kn, v, kr):
    s = (ops.b.nt(qn, kn) + ops.b.nt(qr, kr)) * ((MLA_NOPE + MLA_ROPE) ** -0.5)
    rows = row0 + lax.broadcasted_iota(jnp.int32, s.shape, 0)
    cols = lax.broadcasted_iota(jnp.int32, s.shape, 1)
    s = jnp.where(rows >= cols, s, NEG_BIG)
    return (ops.b.nn(_softmax(s), v),)


def _shift_down(x, d, t_idx):
    if d == 0:
        return x
    return jnp.where(t_idx >= d, pltpu.roll(x, d, axis=0), 0.0)


def _shift_up(x, d, t_idx):
    if d == 0:
        return x
    n = x.shape[0]
    return jnp.where(t_idx < n - d, pltpu.roll(x, n - d, axis=0), 0.0)


def conv_fwd(x, w, name):
    s, c = x.shape
    kw = w.shape[0]
    tc = _pick(c, (256, 128))

    def body(x_ref, w_ref, y_ref):
        xv = x_ref[...]
        t_idx = lax.broadcasted_iota(jnp.int32, xv.shape, 0)
        acc = jnp.zeros_like(xv)
        for j in range(kw):
            acc = acc + w_ref[j:j + 1, :] * _shift_down(xv, kw - 1 - j, t_idx)
        y_ref[...] = acc

    return pl.pallas_call(
        body, name=name, out_shape=jax.ShapeDtypeStruct((s, c), F32), grid=(c // tc,),
        in_specs=[pl.BlockSpec((s, tc), lambda i: (0, i)), pl.BlockSpec((kw, tc), lambda i: (0, i))],
        out_specs=pl.BlockSpec((s, tc), lambda i: (0, i)),
        compiler_params=_params(("parallel",)),
    )(x, w)


def conv_bwd(x, w, dy, name):
    s, c = x.shape
    kw = w.shape[0]
    tc = _pick(c, (256, 128))

    def body(x_ref, w_ref, dy_ref, dx_ref, dw_ref):
        xv, dyv = x_ref[...], dy_ref[...]
        t_idx = lax.broadcasted_iota(jnp.int32, xv.shape, 0)
        dx = jnp.zeros_like(xv)
        for j in range(kw):
            d = kw - 1 - j
            dx = dx + w_ref[j:j + 1, :] * _shift_up(dyv, d, t_idx)
            dw_ref[j:j + 1, :] = jnp.sum(dyv * _shift_down(xv, d, t_idx), axis=0, keepdims=True)
        dx_ref[...] = dx

    return pl.pallas_call(
        body, name=name,
        out_shape=[jax.ShapeDtypeStruct((s, c), F32), jax.ShapeDtypeStruct((kw, c), F32)],
        grid=(c // tc,),
        in_specs=[pl.BlockSpec((s, tc), lambda i: (0, i)), pl.BlockSpec((kw, tc), lambda i: (0, i)),
                  pl.BlockSpec((s, tc), lambda i: (0, i))],
        out_specs=[pl.BlockSpec((s, tc), lambda i: (0, i)), pl.BlockSpec((kw, tc), lambda i: (0, i))],
        compiler_params=_params(("parallel",)),
    )(x, w, dy)


def _conv_taps(x, w_ref, t_idx):
    kw = w_ref.shape[0]
    acc = jnp.zeros_like(x)
    for j in range(kw):
        acc = acc + w_ref[j:j + 1, :] * _shift_down(x, kw - 1 - j, t_idx)
    return acc


def gated_conv_fwd(b, c, u, w, name):
    s, ch = c.shape
    kw = w.shape[0]
    tc = _pick(ch, (256, 128))

    def body(b_ref, c_ref, u_ref, w_ref, y_ref):
        x = c_ref[...] * u_ref[...]
        t_idx = lax.broadcasted_iota(jnp.int32, x.shape, 0)
        y_ref[...] = (b_ref[...] * _conv_taps(x, w_ref, t_idx)).astype(y_ref.dtype)

    blk = pl.BlockSpec((s, tc), lambda i: (0, i))
    return pl.pallas_call(
        body, name=name, out_shape=jax.ShapeDtypeStruct((s, ch), BF16), grid=(ch // tc,),
        in_specs=[blk, blk, blk, pl.BlockSpec((kw, tc), lambda i: (0, i))], out_specs=blk,
        compiler_params=_params(("parallel",)),
    )(b, c, u, w)


def gated_conv_bwd(b, c, u, w, dy, name):
    s, ch = c.shape
    kw = w.shape[0]
    tc = _pick(ch, (256, 128))

    def body(b_ref, c_ref, u_ref, w_ref, dy_ref, db_ref, dc_ref, du_ref, dw_ref):
        cv, uv, dyv = c_ref[...], u_ref[...], dy_ref[...].astype(F32)
        x = cv * uv
        t_idx = lax.broadcasted_iota(jnp.int32, x.shape, 0)
        db_ref[...] = dyv * _conv_taps(x, w_ref, t_idx)
        dconv = dyv * b_ref[...]
        dx = jnp.zeros_like(x)
        for j in range(kw):
            d = kw - 1 - j
            dx = dx + w_ref[j:j + 1, :] * _shift_up(dconv, d, t_idx)
            dw_ref[j:j + 1, :] = jnp.sum(dconv * _shift_down(x, d, t_idx), axis=0, keepdims=True)
        dc_ref[...] = dx * uv
        du_ref[...] = dx * cv

    blk = pl.BlockSpec((s, tc), lambda i: (0, i))
    wblk = pl.BlockSpec((kw, tc), lambda i: (0, i))
    return pl.pallas_call(
        body, name=name,
        out_shape=[jax.ShapeDtypeStruct((s, ch), F32)] * 3 + [jax.ShapeDtypeStruct((kw, ch), F32)],
        grid=(ch // tc,), in_specs=[blk, blk, blk, wblk, blk], out_specs=[blk, blk, blk, wblk],
        compiler_params=_params(("parallel",)),
    )(b, c, u, w, dy)


def make_gated_conv(name):
    @jax.custom_vjp
    def op(b, c, u, w):
        return gated_conv_fwd(b, c, u, w, name + "_f")

    def fwd(b, c, u, w):
        return op(b, c, u, w), (b, c, u, w)

    def bwd(saved, dy):
        return tuple(gated_conv_bwd(*saved, dy, name + "_b"))

    op.defvjp(fwd, bwd)
    return op


def make_conv(name):
    @jax.custom_vjp
    def op(x, w):
        return conv_fwd(x, w, name + "_f")

    def fwd(x, w):
        return op(x, w), (x, w)

    def bwd(saved, dy):
        dx, dw = conv_bwd(saved[0], saved[1], dy, name + "_b")
        return dx, dw

    op.defvjp(fwd, bwd)
    return op


def _gdn_consts():
    c, d = GDN_CHUNK, GDN_DK
    i = lax.broadcasted_iota(jnp.int32, (c, c), 0)
    j = lax.broadcasted_iota(jnp.int32, (c, c), 1)
    tri = i >= j
    return dict(
        tri=tri, strict=i > j,
        tri_f=tri.astype(F32),
        eye=(i == j).astype(F32),
        lane0=(lax.broadcasted_iota(jnp.int32, (c, d), 1) == 0).astype(F32),
        last_row=(lax.broadcasted_iota(jnp.int32, (c, d), 0) == c - 1).astype(F32),
    )


def _inverse_given(m_ops):
    @jax.custom_vjp
    def given(mm_, t):
        return t

    def bwd(t, dt):
        return -m_ops.nt(m_ops.tn(t, dt), t), jnp.zeros_like(t)

    given.defvjp(lambda mm_, t: (t, t), bwd)
    return given


def _gdn_chunk(ops, q, k, v, g, beta, state, t_saved=None):
    b, m, sel = ops.bb, ops.bm, ops.bs
    nh, c, d = q.shape[0], GDN_CHUNK, GDN_DK
    k_ = _gdn_consts()

    def per_head(a):
        return jnp.broadcast_to(a, (nh,) + a.shape)

    gc = sel.sel_nn(per_head(k_["tri_f"]), g)
    col = jnp.broadcast_to(jnp.sum(gc * k_["lane0"], axis=2, keepdims=True), (nh, c, c))
    row = sel.sel_nt(per_head(k_["lane0"]), gc)
    decay = jnp.where(k_["tri"], jnp.exp(jnp.where(k_["tri"], col - row, 0.0)), 0.0)
    kb = k * beta
    mm_ = jnp.where(k_["strict"], b.nt(kb, k) * decay, 0.0)
    if t_saved is None:
        p = -mm_
        t = k_["eye"] + p
        for _ in range(int(math.log2(GDN_CHUNK)) - 1):
            p = m.nn(p, p)
            t = t + m.nn(t, p)
    else:
        t = _inverse_given(_PLAIN.bm)(mm_, t_saved)
    egc = jnp.exp(gc)
    u = b.nn(t, v * beta)
    w = b.nn(t, kb * egc)
    attn = b.nt(q, k) * decay
    v_new = u - b.nn(w, state)
    o = b.nn(q * egc, state) + b.nn(attn, v_new)
    g_last = jnp.sum(gc * k_["last_row"], axis=1, keepdims=True)
    new_state = (state * jnp.exp(jnp.broadcast_to(g_last, (nh, d, d)))
                 + b.tn(k * jnp.exp(jnp.broadcast_to(g_last, (nh, c, d)) - gc), v_new))
    return o, new_state, t


GDN_HEAD_GROUP = 8
GDN_TILE_CHUNKS = 4


def _heads_of(ref, rows, n_heads):
    d = GDN_DK
    return jnp.stack([ref[rows, h * d:(h + 1) * d] for h in range(n_heads)])


def _gdn_specs(s, reverse):
    d, hg = GDN_DK, GDN_HEAD_GROUP
    tile = min(GDN_TILE_CHUNKS * GDN_CHUNK, s)
    n_tiles = s // tile
    t_of = (lambda t: n_tiles - 1 - t) if reverse else (lambda t: t)
    seq = pl.BlockSpec((tile, hg * d), lambda grp, t: (t_of(t), grp))
    st = pl.BlockSpec((hg, tile // GDN_CHUNK, d, d), lambda grp, t: (grp, t_of(t), 0, 0))
    inv = pl.BlockSpec((hg, tile // GDN_CHUNK, GDN_CHUNK, GDN_CHUNK), lambda grp, t: (grp, t_of(t), 0, 0))
    return seq, st, inv, tile, n_tiles


def gdn_fwd(q, k, v, g, beta, name):
    s = q.shape[0]
    d, hg = GDN_DK, GDN_HEAD_GROUP
    seq, st, inv, tile, n_tiles = _gdn_specs(s, False)

    def body(q_ref, k_ref, v_ref, g_ref, b_ref, o_ref, st_ref, inv_ref, state_scr):
        @pl.when(pl.program_id(1) == 0)
        def _():
            state_scr[...] = jnp.zeros_like(state_scr)

        def step(ci, carry):
            rows = pl.ds(pl.multiple_of(ci * GDN_CHUNK, GDN_CHUNK), GDN_CHUNK)
            state = state_scr[...]
            for h in range(hg):
                st_ref[h, ci] = state[h]
            o, new_state, t = _gdn_chunk(_PLAIN, *[_heads_of(r, rows, hg) for r in (q_ref, k_ref, v_ref, g_ref, b_ref)],
                                         state)
            for h in range(hg):
                o_ref[rows, h * d:(h + 1) * d] = o[h]
                inv_ref[h, ci] = t[h]
            state_scr[...] = new_state
            return carry

        lax.fori_loop(0, tile // GDN_CHUNK, step, 0)

    return pl.pallas_call(
        body, name=name,
        out_shape=[jax.ShapeDtypeStruct(q.shape, F32),
                   jax.ShapeDtypeStruct((GDN_HEADS, s // GDN_CHUNK, d, d), F32),
                   jax.ShapeDtypeStruct((GDN_HEADS, s // GDN_CHUNK, GDN_CHUNK, GDN_CHUNK), F32)],
        grid=(GDN_HEADS // hg, n_tiles), in_specs=[seq] * 5, out_specs=[seq, st, inv],
        scratch_shapes=[pltpu.VMEM((hg, d, d), F32)],
        compiler_params=_params(("parallel", "arbitrary")),
    )(q, k, v, g, beta)


def gdn_bwd(q, k, v, g, beta, states, inverses, do, name):
    s = q.shape[0]
    d, hg = GDN_DK, GDN_HEAD_GROUP
    seq, st, inv, tile, n_tiles = _gdn_specs(s, True)
    tile_chunks = tile // GDN_CHUNK

    def body(q_ref, k_ref, v_ref, g_ref, b_ref, st_ref, inv_ref, do_ref, dq_ref, dk_ref, dv_ref, dg_ref, db_ref,
             dstate_scr):
        @pl.when(pl.program_id(1) == 0)
        def _():
            dstate_scr[...] = jnp.zeros_like(dstate_scr)

        def step(it, carry):
            ci = tile_chunks - 1 - it
            rows = pl.ds(pl.multiple_of(ci * GDN_CHUNK, GDN_CHUNK), GDN_CHUNK)
            prim = [_heads_of(r, rows, hg) for r in (q_ref, k_ref, v_ref, g_ref, b_ref)]
            prim.append(jnp.stack([st_ref[h, ci] for h in range(hg)]))
            t_saved = jnp.stack([inv_ref[h, ci] for h in range(hg)])
            _, vjp = jax.vjp(lambda *a: _gdn_chunk(_DIFF, *a, t_saved=t_saved)[:2], *prim)
            grads = vjp((_heads_of(do_ref, rows, hg), dstate_scr[...]))
            for g_ref_out, gr in zip((dq_ref, dk_ref, dv_ref, dg_ref, db_ref), grads[:5]):
                for h in range(hg):
                    g_ref_out[rows, h * d:(h + 1) * d] = gr[h]
            dstate_scr[...] = grads[5]
            return carry

        lax.fori_loop(0, tile_chunks, step, 0)

    return pl.pallas_call(
        body, name=name,
        out_shape=[jax.ShapeDtypeStruct(q.shape, F32)] * 5,
        grid=(GDN_HEADS // hg, n_tiles), in_specs=[seq] * 5 + [st, inv, seq], out_specs=[seq] * 5,
        scratch_shapes=[pltpu.VMEM((hg, d, d), F32)],
        compiler_params=_params(("parallel", "arbitrary")),
    )(q, k, v, g, beta, states, inverses, do)


def make_gdn(name):
    @jax.custom_vjp
    def op(q, k, v, g, beta):
        return gdn_fwd(q, k, v, g, beta, name + "_f")[0]

    def fwd(q, k, v, g, beta):
        o, states, inverses = gdn_fwd(q, k, v, g, beta, name + "_f")
        return o, (q, k, v, g, beta, states, inverses)

    def bwd(saved, do):
        return tuple(gdn_bwd(*saved, do, name + "_b"))

    op.defvjp(fwd, bwd)
    return op


def loss_head(x, g, target, name):
    s, d = x.shape
    tm = min(256, s)

    def body(x_ref, g_ref, t_ref, loss_ref, dx_ref, dg_ref):
        tgt = t_ref[...]

        def f(xv, gv):
            err = _rms(xv, gv) - tgt
            per_row = jnp.mean(err * err, axis=-1, keepdims=True)
            return 0.5 * jnp.sum(per_row, axis=0, keepdims=True)

        val, vjp = jax.vjp(f, x_ref[...], g_ref[...])
        dx, dg = vjp(jnp.ones((1, 1), F32))
        dx_ref[...] = dx
        first = pl.program_id(0) == 0

        @pl.when(first)
        def _():
            dg_ref[...] = dg
            loss_ref[...] = jnp.broadcast_to(val, loss_ref.shape)

        @pl.when(jnp.logical_not(first))
        def _():
            dg_ref[...] += dg
            loss_ref[...] += jnp.broadcast_to(val, loss_ref.shape)

    row = pl.BlockSpec((tm, d), lambda i: (i, 0))
    vec = pl.BlockSpec((1, d), lambda i: (0, 0))
    return pl.pallas_call(
        body, name=name,
        out_shape=[jax.ShapeDtypeStruct((1, LANES), F32), jax.ShapeDtypeStruct((s, d), F32),
                   jax.ShapeDtypeStruct((1, d), F32)],
        grid=(s // tm,), in_specs=[row, vec, row],
        out_specs=[pl.BlockSpec((1, LANES), lambda i: (0, 0)), row, vec],
        compiler_params=_params(("arbitrary",)),
    )(x, g, target)


def adamw(g8, w, m, v, layer, prev, name):
    n_layers, rows, width = w.shape
    tr = _pick(rows, (256, 128, 64, 32, 16, 8))

    def body(g_ref, w_ref, m_ref, v_ref, *rest):
        go_ref, d_ref, mo_ref, vo_ref = rest[-4:]
        g = g_ref[0].astype(F32)
        for p in range(1, N_DEV):
            g = g + g_ref[p].astype(F32)
        m_new = ADAM_B1 * m_ref[...] + (1.0 - ADAM_B1) * g
        v_new = ADAM_B2 * v_ref[...] + (1.0 - ADAM_B2) * (g * g)
        m_hat = m_new / (1.0 - ADAM_B1 ** ADAM_STEP)
        v_hat = v_new / (1.0 - ADAM_B2 ** ADAM_STEP)
        go_ref[...] = g
        d_ref[...] = -ADAM_LR * (m_hat / (jnp.sqrt(v_hat) + ADAM_EPS) + ADAM_WD * w_ref[...])
        mo_ref[...] = m_new
        vo_ref[...] = v_new

    blk = pl.BlockSpec((None, tr, width), lambda i: (layer, i, 0))
    carried = list(prev) if prev is not None else []
    return pl.pallas_call(
        body, name=name, out_shape=[jax.ShapeDtypeStruct((n_layers, rows, width), F32)] * 4,
        grid=(rows // tr,),
        in_specs=[pl.BlockSpec((N_DEV, tr, width), lambda i: (0, i, 0)), blk, blk, blk]
        + [pl.BlockSpec(memory_space=pl.ANY)] * len(carried),
        out_specs=[blk] * 4,
        input_output_aliases={4 + j: j for j in range(len(carried))},
        compiler_params=_params(("parallel",)),
    )(g8, w, m, v, *carried)


_HBM = pl.BlockSpec(memory_space=pltpu.HBM)
_SEM = pl.BlockSpec(memory_space=pltpu.SEMAPHORE)
_EFFECT = pltpu.SideEffectType.DATAFLOW_SIDE_EFFECTING


def _exchange_copies(mode, src_refs, land_refs, send_sems, recv_sems, local_sems):
    x, y, c = lax.axis_index("x"), lax.axis_index("y"), lax.axis_index("c")
    me = 4 * x + 2 * y + c
    n = len(src_refs)

    def src(k, p):
        return src_refs[k] if mode == "gather" else src_refs[k].at[p]

    local = [pltpu.make_async_copy(src(k, me), land_refs[k].at[me], local_sems.at[k]) for k in range(n)]
    sends, recvs = [], []
    for k in range(n):
        for r in range(1, N_DEV):
            px = (1 - x) if r & 4 else x
            py = (1 - y) if r & 2 else y
            pc = (1 - c) if r & 1 else c
            p = 4 * px + 2 * py + pc
            sem = k * (N_DEV - 1) + r - 1
            sends.append(pltpu.make_async_remote_copy(
                src_ref=src(k, p), dst_ref=land_refs[k].at[me],
                send_sem=send_sems.at[sem], recv_sem=recv_sems.at[sem],
                device_id=(px, py, pc), device_id_type=pl.DeviceIdType.MESH))
            recvs.append(pltpu.make_async_remote_copy(
                src_ref=src(k, p), dst_ref=land_refs[k].at[p],
                send_sem=send_sems.at[sem], recv_sem=recv_sems.at[sem],
                device_id=(px, py, pc), device_id_type=pl.DeviceIdType.MESH))
    return local, sends, recvs


def exchange_start(mode, arrays, name, carry=()):
    n, nc = len(arrays), len(carry)
    land_shapes = [((N_DEV,) + tuple(a.shape)) if mode == "gather" else tuple(a.shape) for a in arrays]
    lands = [pltpu.with_memory_space_constraint(lax.empty(shp, a.dtype), pltpu.HBM)
             for shp, a in zip(land_shapes, arrays)]
    srcs = [pltpu.with_memory_space_constraint(a, pltpu.HBM) for a in arrays]
    carried = [pltpu.with_memory_space_constraint(a, pltpu.HBM) for a in carry]

    def body(*refs):
        src_refs, land_refs = refs[:n], refs[n:2 * n]
        first_out = 2 * n + nc
        send_sems, recv_sems, local_sems = refs[first_out:first_out + 3]
        token = refs[-1]
        local, sends, _ = _exchange_copies(mode, src_refs, land_refs, send_sems, recv_sems, local_sems)
        for cp in local + sends:
            cp.start()
        token[...] = jnp.zeros_like(token)

    n_sem = n * (N_DEV - 1)
    out = pl.pallas_call(
        body, name=name,
        out_shape=(pltpu.SemaphoreType.DMA((n_sem,)), pltpu.SemaphoreType.DMA((n_sem,)),
                   pltpu.SemaphoreType.DMA((n,)),
                   *[pltpu.HBM(a.shape, a.dtype) for a in arrays],
                   *[pltpu.HBM(shp, a.dtype) for shp, a in zip(land_shapes, arrays)],
                   *[pltpu.HBM(a.shape, a.dtype) for a in carry],
                   jax.ShapeDtypeStruct((8, LANES), F32)),
        in_specs=[_HBM] * (2 * n + nc),
        out_specs=(_SEM, _SEM, _SEM, *[_HBM] * (2 * n + nc), pl.BlockSpec(memory_space=pltpu.VMEM)),
        input_output_aliases={i: 3 + i for i in range(2 * n + nc)},
        compiler_params=pltpu.CompilerParams(has_side_effects=_EFFECT),
    )(*srcs, *lands, *carried)
    handle = dict(mode=mode, sems=out[:3], srcs=out[3:3 + n], lands=out[3 + n:3 + 2 * n])
    return handle, out[-1], list(out[3 + 2 * n:3 + 2 * n + nc])


def exchange_wait(handle, after, name):
    mode, srcs, lands = handle["mode"], list(handle["srcs"]), list(handle["lands"])
    n = len(srcs)

    def body(*refs):
        src_refs, land_refs = refs[:n], refs[n:2 * n]
        send_sems, recv_sems, local_sems = refs[2 * n:2 * n + 3]
        local, sends, recvs = _exchange_copies(mode, src_refs, land_refs, send_sems, recv_sems, local_sems)
        for cp in sends:
            cp.wait_send()
        for cp in recvs:
            cp.wait_recv()
        for cp in local:
            cp.wait()

    out = pl.pallas_call(
        body, name=name,
        out_shape=(*[pltpu.HBM(a.shape, a.dtype) for a in srcs], *[pltpu.HBM(a.shape, a.dtype) for a in lands]),
        in_specs=[_HBM] * (2 * n) + [_SEM] * 3 + [pl.BlockSpec(memory_space=pl.ANY)],
        out_specs=tuple([_HBM] * (2 * n)),
        input_output_aliases={i: i for i in range(2 * n)},
        compiler_params=pltpu.CompilerParams(has_side_effects=_EFFECT),
    )(*srcs, *lands, *handle["sems"], after)
    return list(out[n:])


_ICI_RELATIONS = (2, 4, 6)


def _mesh_place():
    x, y, c = lax.axis_index("x"), lax.axis_index("y"), lax.axis_index("c")

    def peer(r):
        px = (1 - x) if r & 4 else x
        py = (1 - y) if r & 2 else y
        pc = (1 - c) if r & 1 else c
        return (px, py, pc), 4 * px + 2 * py + pc

    return 4 * x + 2 * y + c, peer


def _remote(src, dst, send_sem, recv_sem, device):
    return pltpu.make_async_remote_copy(src_ref=src, dst_ref=dst, send_sem=send_sem, recv_sem=recv_sem,
                                        device_id=device, device_id_type=pl.DeviceIdType.MESH)


def gather2_start(groups, name):
    flat = [a for g in groups for a in g]
    n = len(flat)
    lands = [pltpu.with_memory_space_constraint(lax.empty((N_DEV,) + tuple(a.shape), a.dtype), pltpu.HBM) for a in flat]
    srcs = [pltpu.with_memory_space_constraint(a, pltpu.HBM) for a in flat]
    n_rel = 1 + len(_ICI_RELATIONS)

    def body(*refs):
        src_refs, land_refs = refs[:n], refs[n:2 * n]
        sem_refs = refs[2 * n:2 * n + 4 * len(groups)]
        me, peer = _mesh_place()
        k = 0
        for gi, g in enumerate(groups):
            send_sems, recv_sib, recv_ici, local_sems = sem_refs[4 * gi:4 * gi + 4]
            for j in range(len(g)):
                pltpu.make_async_copy(src_refs[k], land_refs[k].at[me], local_sems.at[j]).start()
                dev, _ = peer(1)
                _remote(src_refs[k], land_refs[k].at[me], send_sems.at[n_rel * j], recv_sib.at[j], dev).start()
                for t, r in enumerate(_ICI_RELATIONS):
                    dev, _ = peer(r)
                    _remote(src_refs[k], land_refs[k].at[me], send_sems.at[n_rel * j + 1 + t],
                            recv_ici.at[len(_ICI_RELATIONS) * j + t], dev).start()
                k += 1
        refs[-1][...] = jnp.zeros_like(refs[-1])

    sem_shapes = []
    for g in groups:
        sem_shapes += [pltpu.SemaphoreType.DMA((n_rel * len(g),)), pltpu.SemaphoreType.DMA((len(g),)),
                       pltpu.SemaphoreType.DMA((len(_ICI_RELATIONS) * len(g),)), pltpu.SemaphoreType.DMA((len(g),))]
    out = pl.pallas_call(
        body, name=name,
        out_shape=(*sem_shapes, *[pltpu.HBM(a.shape, a.dtype) for a in flat],
                   *[pltpu.HBM((N_DEV,) + tuple(a.shape), a.dtype) for a in flat],
                   jax.ShapeDtypeStruct((8, LANES), F32)),
        in_specs=[_HBM] * (2 * n),
        out_specs=(*[_SEM] * len(sem_shapes), *[_HBM] * (2 * n), pl.BlockSpec(memory_space=pltpu.VMEM)),
        input_output_aliases={i: len(sem_shapes) + i for i in range(2 * n)},
        compiler_params=pltpu.CompilerParams(has_side_effects=_EFFECT),
    )(*srcs, *lands)
    handles, k, base = [], 0, len(sem_shapes)
    for gi, g in enumerate(groups):
        handles.append(dict(sems=out[4 * gi:4 * gi + 4], srcs=out[base + k:base + k + len(g)],
                            lands=out[base + n + k:base + n + k + len(g)]))
        k += len(g)
    return handles, out[-1]


def gather2_forward(handle, after, name, carry=()):
    lands, nc = list(handle["lands"]), len(carry)
    n, n_ici = len(lands), len(_ICI_RELATIONS)
    carried = [pltpu.with_memory_space_constraint(a, pltpu.HBM) for a in carry]

    def body(*refs):
        land_refs = refs[:n]
        recv_ici = refs[n + nc]
        fwd_send, fwd_recv = refs[n + nc + 2], refs[n + nc + 3]
        me, peer = _mesh_place()
        sibling, _ = peer(1)
        for j in range(n):
            for t, r in enumerate(_ICI_RELATIONS):
                dev, p = peer(r)
                landed = land_refs[j].at[p]
                _remote(landed, landed, fwd_send.at[n_ici * j + t], recv_ici.at[n_ici * j + t], dev).wait_recv()
                _remote(landed, landed, fwd_send.at[n_ici * j + t], fwd_recv.at[n_ici * j + t], sibling).start()

    out = pl.pallas_call(
        body, name=name,
        out_shape=(pltpu.SemaphoreType.DMA((n_ici * n,)), pltpu.SemaphoreType.DMA((n_ici * n,)),
                   *[pltpu.HBM(a.shape, a.dtype) for a in lands], *[pltpu.HBM(a.shape, a.dtype) for a in carry]),
        in_specs=[_HBM] * (n + nc) + [_SEM, pl.BlockSpec(memory_space=pl.ANY)],
        out_specs=(_SEM, _SEM, *[_HBM] * (n + nc)),
        input_output_aliases={i: 2 + i for i in range(n + nc)},
        compiler_params=pltpu.CompilerParams(has_side_effects=_EFFECT),
    )(*lands, *carried, handle["sems"][2], after)
    new_handle = dict(sems=handle["sems"], srcs=handle["srcs"], lands=out[2:2 + n], fwd=out[:2])
    return new_handle, list(out[2 + n:])


def gather2_wait(handle, after, name):
    srcs, lands = list(handle["srcs"]), list(handle["lands"])
    n, n_ici = len(srcs), len(_ICI_RELATIONS)
    n_rel = 1 + n_ici
    send_all, recv_sibling, _, local_all = handle["sems"]

    def body(*refs):
        src_refs, land_refs = refs[:n], refs[n:2 * n]
        send_sems, recv_sib, local_sems, fwd_send, fwd_recv = refs[2 * n:2 * n + 5]
        me, peer = _mesh_place()
        sibling, sib = peer(1)
        for j in range(n):
            pltpu.make_async_copy(src_refs[j], land_refs[j].at[me], local_sems.at[j]).wait()
            _remote(src_refs[j], land_refs[j].at[sib], send_sems.at[n_rel * j], recv_sib.at[j], sibling).wait()
            for t, r in enumerate(_ICI_RELATIONS):
                dev, p = peer(r)
                _remote(src_refs[j], land_refs[j].at[me], send_sems.at[n_rel * j + 1 + t],
                        recv_sib.at[j], dev).wait_send()
                _, p_sib = peer(r ^ 1)
                _remote(land_refs[j].at[p], land_refs[j].at[p_sib], fwd_send.at[n_ici * j + t],
                        fwd_recv.at[n_ici * j + t], sibling).wait()

    out = pl.pallas_call(
        body, name=name,
        out_shape=(*[pltpu.HBM(a.shape, a.dtype) for a in srcs], *[pltpu.HBM(a.shape, a.dtype) for a in lands]),
        in_specs=[_HBM] * (2 * n) + [_SEM] * 5 + [pl.BlockSpec(memory_space=pl.ANY)],
        out_specs=tuple([_HBM] * (2 * n)),
        input_output_aliases={i: i for i in range(2 * n)},
        compiler_params=pltpu.CompilerParams(has_side_effects=_EFFECT),
    )(*srcs, *lands, send_all, recv_sibling, local_all, *handle["fwd"], after)
    return list(out[n:])


BIG = ["mla_w_in", "mla_w_uq", "mla_w_ukv", "mla_w_o", "gdn_w_in", "gdn_w_o", "sc_w_in", "sc_w_o",
       "xa_w_q", "xa_w_kv", "xa_w_o", "mlp_w1", "mlp_w2"]
TINY = [("mla_q_norm", 1), ("mla_kv_norm", 1), ("gdn_conv_w", 2), ("sc_conv_w", 2)]
REPL = ["gdn_a_log", "gdn_dt_bias", "gdn_o_norm", "norm_mix", "norm_mem", "norm_mlp", "mem_norm", "final_norm"]
WEIGHTS = ["mla_w_in", "mla_q_norm", "mla_kv_norm", "mla_w_uq", "mla_w_ukv", "mla_w_o", "gdn_w_in",
           "gdn_conv_w", "gdn_a_log", "gdn_dt_bias", "gdn_o_norm", "gdn_w_o", "sc_w_in", "sc_conv_w",
           "sc_w_o", "norm_mix", "norm_mem", "norm_mlp", "xa_w_q", "xa_w_kv", "xa_w_o", "mlp_w1",
           "mlp_w2", "mem_norm", "final_norm"]
MIXER_WEIGHTS = (["mla_w_in", "mla_w_uq", "mla_w_ukv", "mla_w_o"], ["gdn_w_in", "gdn_w_o"], ["sc_w_in", "sc_w_o"])
MIXER_PARAMS = (["norm_mem", "mla_q_norm", "mla_kv_norm"],
                ["norm_mem", "gdn_conv_w", "gdn_a_log", "gdn_dt_bias", "gdn_o_norm"],
                ["norm_mem", "sc_conv_w"])


def from_shards(a8, axis):
    a = jnp.moveaxis(a8, 0, axis)
    shp = a.shape
    return a.reshape(shp[:axis] + (shp[axis] * shp[axis + 1],) + shp[axis + 2:])


def pack_rows(flat_list, width, row_mult):
    total = sum(a.shape[-1] for a in flat_list)
    rows = -(-total // width)
    rows = -(-rows // row_mult) * row_mult
    pad = rows * width - total
    parts = list(flat_list)
    if pad:
        parts.append(jnp.zeros((pad,), flat_list[0].dtype))
    return jnp.concatenate(parts, axis=-1).reshape(rows, width)


def unpack_rows(packed, shapes):
    lead = packed.shape[:-2]
    flat = packed.reshape(lead + (-1,))
    out, off = [], 0
    for shp in shapes:
        n = math.prod(shp)
        out.append(flat[..., off:off + n].reshape(lead + tuple(shp)))
        off += n
    return out


def _swap_halves(w):
    half = w.shape[-1] // 2
    return jnp.concatenate([w[..., half:], w[..., :half]], axis=-1)


def _pad_last(w, n):
    return jnp.pad(w, [(0, 0)] * (w.ndim - 1) + [(0, n - w.shape[-1])])


def _unblock(w8):
    return jnp.transpose(w8, (1, 0, 2)).reshape(w8.shape[1], -1)


def _stack_rows(w8):
    return w8.reshape(-1, w8.shape[-1])


def rms_op(name, rows, d, out_dtype):
    tm = rows if rows * d * 4 <= BLOCK_BYTES else 512
    return make_tile_op(fn_rms, name, ["row", "par"], [True, True], [("row", d, out_dtype)], rows, min(tm, rows))


def seg_memory(p, mem):
    return rms_op("rms_memory", mem.shape[0], mem.shape[1], BF16)(mem, p["mem_norm"].reshape(1, -1))[0]


def seg_mixer(i, wts, p, x, h, rope_c, rope_s):
    s, d = x.shape
    j, kind = i // N_MIXERS, i % N_MIXERS
    tag = f"l{i}"
    hd = MLA_NOPE
    next_gain = p["norm_mem"][i].reshape(1, d)
    if kind == 0:
        w_in = _stack_rows(wts["mla_w_in"])
        w_cq = w_in[:, :MLA_Q_RANK]
        w_ckv = w_in[:, MLA_Q_RANK:MLA_Q_RANK + MLA_KV_RANK]
        w_kr = w_in[:, MLA_Q_RANK + MLA_KV_RANK:]
        c_qn, c_kvn, k_rope = make_tile_op(
            fn_mla_down, tag + "_mla_down", ["row", "row", "row"] + ["par"] * 6, [True, False, False] + [True] * 6,
            [("row", MLA_Q_RANK, BF16), ("row", MLA_KV_RANK, BF16), ("row", hd, F32)], s, 512)(
            h, rope_c, rope_s, w_cq, w_ckv, _pad_last(w_kr, hd), _pad_last(_swap_halves(w_kr), hd),
            p["mla_q_norm"][j].reshape(1, -1), p["mla_kv_norm"][j].reshape(1, -1))
        w_uq8 = wts["mla_w_uq"]
        w_qn = _unblock(w_uq8[:, :, :MLA_NOPE])
        w_qr = w_uq8[:, :, MLA_NOPE:]
        w_qr_p = _unblock(_pad_last(w_qr, hd))
        w_qr_s = _unblock(_pad_last(_swap_halves(w_qr), hd))
        nq = MLA_HEADS * hd
        q_nope, q_rope, kv = make_tile_op(
            fn_mla_up, tag + "_mla_up", ["row", "row", "row", "row"] + ["par"] * 4, [True, True, False, False] + [True] * 4,
            [("row", nq, BF16), ("row", nq, F32), ("row", 2 * nq, BF16)], s, 512)(
            c_qn, c_kvn, rope_c, rope_s, w_qn, w_qr_p, w_qr_s, _unblock(wts["mla_w_ukv"]))
        n_groups = MLA_QUERY_GROUPS if s % (MLA_QUERY_GROUPS * 256) == 0 else 1
        rows_g = s // n_groups
        o_groups = []
        for grp in range(n_groups):
            r0, r1 = grp * rows_g, (grp + 1) * rows_g
            o_groups.append(make_tile_op(
                fn_mla_attn, f"{tag}_mla_attn{grp}", [("rowh", hd), ("rowh", hd), ("parh", hd, 2), "par"],
                [True] * 4, [(("rowh", hd), nq, BF16)], rows_g, 256, MLA_HEADS, row_base=r0)(
                q_nope[r0:r1], q_rope[r0:r1], kv[:r1], k_rope[:r1])[0])
        o = jnp.concatenate(o_groups, axis=0)
        return make_mm_res_rms(tag + "_mla_o")(x, o, _stack_rows(wts["mla_w_o"]), next_gain)
    if kind == 1:
        ng = GDN_HEADS * GDN_DK
        w_in = _unblock(wts["gdn_w_in"])
        qkv_pre = make_mm(tag + "_gdn_in_qkv", F32)(h, w_in[:, :3 * ng])
        qkv_conv = make_conv(tag + "_gdn_conv")(qkv_pre, p["gdn_conv_w"][j])
        gate = make_mm(tag + "_gdn_in_g", F32)(h, w_in[:, 3 * ng:4 * ng])
        ba = make_mm(tag + "_gdn_in_ba", F32)(h, _pad_last(w_in[:, 4 * ng:], LANES))
        heads_row = ("row", GDN_DK, GDN_HEADS)
        q, k, v = make_tile_op(fn_gdn_prep, tag + "_gdn_prep", [("row", GDN_DK, 3 * GDN_HEADS)], [True],
                               [(heads_row, ng, F32)] * 3, s, 256)(qkv_conv)
        alog = jnp.pad(p["gdn_a_log"][j].reshape(1, -1), ((0, 0), (GDN_HEADS, LANES - 2 * GDN_HEADS)))
        dtb = jnp.pad(p["gdn_dt_bias"][j].reshape(1, -1), ((0, 0), (GDN_HEADS, LANES - 2 * GDN_HEADS)))
        beta_b, g_b = make_tile_op(fn_gdn_gates, tag + "_gdn_gates", ["row", "par", "par"], [True] * 3,
                                   [("row", ng, F32)] * 2, s, 512)(ba, alog, dtb)
        o = make_gdn(tag + "_gdn_core")(q, k, v, g_b, beta_b)
        o = make_tile_op(fn_gdn_out, tag + "_gdn_out", [heads_row, heads_row, "par"],
                         [True] * 3, [(heads_row, ng, BF16)], s, 512)(
            o, gate, p["gdn_o_norm"][j].reshape(1, -1))[0]
        return make_mm_res_rms(tag + "_gdn_o")(x, o, _stack_rows(wts["gdn_w_o"]), next_gain)
    w_in = _unblock(wts["sc_w_in"])
    b_gate = make_mm(tag + "_sc_in_b", F32)(h, w_in[:, :d])
    c_gate = make_mm(tag + "_sc_in_c", F32)(h, w_in[:, d:2 * d])
    u = make_mm(tag + "_sc_in_u", F32)(h, w_in[:, 2 * d:])
    yv = make_gated_conv(tag + "_sc_conv")(b_gate, c_gate, u, p["sc_conv_w"][j])
    return make_mm_res_rms(tag + "_sc_o")(x, yv, _stack_rows(wts["sc_w_o"]), next_gain)


def seg_xattn(i, wts, p, x, hx, mem_n):
    s, d = x.shape
    tag = f"l{i}"
    kv = make_mm(tag + "_xa_kv", BF16, blocked=True)(mem_n, wts["xa_w_kv"])
    heads = ("row", X_HEAD_DIM, X_HEADS)
    o = make_tile_op(fn_xattn, tag + "_xattn",
                     ["row", ("par", X_HEAD_DIM, X_HEADS), ("par", X_HEAD_DIM, 2 * X_HEADS)], [True] * 3,
                     [(heads, d, BF16)], s, 1024)(hx, _stack_rows(wts["xa_w_q"]), kv)[0]
    return make_mm_res_rms(tag + "_xa_o")(x, o, _stack_rows(wts["xa_w_o"]), p["norm_mlp"][i].reshape(1, d))


def seg_mlp(i, wts, p, x, hm):
    d = x.shape[1]
    gain = p["norm_mix"][i + 1].reshape(1, d) if i + 1 < DEPTH else None
    return make_mlp(f"l{i}_mlp", gain is not None)(x, hm, wts["mlp_w1"], _stack_rows(wts["mlp_w2"]), gain)


def segments():
    segs = []
    for i in range(DEPTH):
        j, kind = i // N_MIXERS, i % N_MIXERS
        segs.append((f"l{i}_mixer", [(n, j) for n in MIXER_WEIGHTS[kind]], MIXER_PARAMS[kind], "mixer"))
        segs.append((f"l{i}_xattn", [(n, i) for n in ("xa_w_q", "xa_w_kv", "xa_w_o")], ["norm_mlp"], "xattn"))
        segs.append((f"l{i}_mlp", [(n, i) for n in ("mlp_w1", "mlp_w2")], ["norm_mix"] if i + 1 < DEPTH else [],
                     "mlp"))
    return segs


def run_segment(index, kind, wts, p, x, h, mem_n, rope_c, rope_s):
    layer = index // 3
    if kind == "mixer":
        return seg_mixer(layer, wts, p, x, h, rope_c, rope_s)
    if kind == "xattn":
        return seg_xattn(layer, wts, p, x, h, mem_n)
    return seg_mlp(layer, wts, p, x, h)


def rope_tables(positions):
    inv_freq = ROPE_THETA ** (-jnp.arange(0, MLA_ROPE, 2, dtype=F32) / MLA_ROPE)
    ang = positions.astype(F32)[:, None] * inv_freq
    cos, sin = jnp.cos(ang), jnp.sin(ang)
    zeros = jnp.zeros((positions.shape[0], MLA_NOPE - MLA_ROPE), F32)
    return jnp.concatenate([cos, cos, zeros], axis=-1), jnp.concatenate([-sin, sin, zeros], axis=-1)


def kernel(x, mem, positions, mla_w_in, mla_q_norm, mla_kv_norm, mla_w_uq, mla_w_ukv, mla_w_o, gdn_w_in, gdn_conv_w, gdn_a_log, gdn_dt_bias, gdn_o_norm, gdn_w_o, sc_w_in, sc_conv_w, sc_w_o, norm_mix, norm_mem, norm_mlp, xa_w_q, xa_w_kv, xa_w_o, mlp_w1, mlp_w2, mem_norm, final_norm, loss_target, m_mla_w_in, m_mla_q_norm, m_mla_kv_norm, m_mla_w_uq, m_mla_w_ukv, m_mla_w_o, m_gdn_w_in, m_gdn_conv_w, m_gdn_a_log, m_gdn_dt_bias, m_gdn_o_norm, m_gdn_w_o, m_sc_w_in, m_sc_conv_w, m_sc_w_o, m_norm_mix, m_norm_mem, m_norm_mlp, m_xa_w_q, m_xa_w_kv, m_xa_w_o, m_mlp_w1, m_mlp_w2, m_mem_norm, m_final_norm, v_mla_w_in, v_mla_q_norm, v_mla_kv_norm, v_mla_w_uq, v_mla_w_ukv, v_mla_w_o, v_gdn_w_in, v_gdn_conv_w, v_gdn_a_log, v_gdn_dt_bias, v_gdn_o_norm, v_gdn_w_o, v_sc_w_in, v_sc_conv_w, v_sc_w_o, v_norm_mix, v_norm_mem, v_norm_mlp, v_xa_w_q, v_xa_w_kv, v_xa_w_o, v_mlp_w1, v_mlp_w2, v_mem_norm, v_final_norm):
    args = locals()
    w_loc = {n: args[n] for n in WEIGHTS}
    m_loc = {n: args["m_" + n] for n in WEIGHTS}
    v_loc = {n: args["v_" + n] for n in WEIGHTS}
    me = 4 * lax.axis_index("x") + 2 * lax.axis_index("y") + lax.axis_index("c")
    segs = segments()

    w16 = {n: w_loc[n].astype(BF16) for n in BIG}
    tiny_pack = pack_rows([w_loc[n].reshape(-1) for n, _ in TINY], LANES, 8)
    gather_handles, token = gather2_start(
        [[tiny_pack]] + [[w16[n][layer] for n, layer in units] for _, units, _, _ in segs], "gather_start")

    x_cur = x[0]
    rope_c, rope_s = rope_tables(positions[0])
    tiny_handle, _ = gather2_forward(gather_handles[0], token, "gather_forward_tiny")
    tiny_all = gather2_wait(tiny_handle, token, "gather_wait_tiny")[0]
    gather_handles = gather_handles[1:]
    params = {}
    for (n, ax), a8 in zip(TINY, unpack_rows(tiny_all, [w_loc[n].shape for n, _ in TINY])):
        params[n] = from_shards(a8, ax)
    for n in REPL:
        params[n] = w_loc[n]

    mem_n, vjp_memory = jax.vjp(lambda p_: seg_memory(p_, mem[0]), {"mem_norm": params["mem_norm"]})
    h_cur, vjp_first_norm = jax.vjp(
        lambda p_, x_: rms_op("l0_rms_mix", x_.shape[0], x_.shape[1], BF16)(x_, p_["norm_mix"][0].reshape(1, -1))[0],
        {"norm_mix": params["norm_mix"]}, x_cur)
    vjps = []
    forwarded, _ = gather2_forward(gather_handles[0], token, f"gather_forward_{segs[0][0]}")
    for index, (tag, units, p_names, kind) in enumerate(segs):
        landed = gather2_wait(forwarded, token if index == 0 else x_cur, f"gather_wait_{tag}")
        wts = {n: a for (n, _), a in zip(units, landed)}
        p_seg = {n: params[n] for n in p_names}
        if index + 1 < len(segs):
            forwarded, (p_seg[p_names[0]],) = gather2_forward(
                gather_handles[index + 1], landed[0], f"gather_forward_{segs[index + 1][0]}",
                carry=[p_seg[p_names[0]]])
        outs, vjp_seg = jax.vjp(
            lambda w_, p_, x_, h_, m_, index=index, kind=kind:
            run_segment(index, kind, w_, p_, x_, h_, m_, rope_c, rope_s),
            wts, p_seg, x_cur, h_cur, mem_n)
        x_cur, h_cur = outs[0], (outs[1] if len(outs) > 1 else None)
        vjps.append(vjp_seg)

    loss_vec, g_x, d_final = loss_head(x_cur, params["final_norm"].reshape(1, -1), loss_target[0], "loss_head")

    grads = {n: jnp.zeros_like(params[n]) for n in params}
    grads["final_norm"] = d_final.reshape(-1)
    g_mem_n = jnp.zeros_like(mem_n)
    g_h = None
    scatter_handles = []
    for (tag, units, _, _), vjp_seg in zip(reversed(segs), reversed(vjps)):
        g_wts, g_p, g_x, g_h, g_m = vjp_seg((g_x,) if g_h is None else (g_x, g_h))
        for n, g in g_p.items():
            grads[n] = grads[n] + g
        g_mem_n = g_mem_n + g_m
        handle, _, (g_h,) = exchange_start("scatter", [g_wts[n] for n, _ in units], f"scatter_start_{tag}",
                                           carry=[g_h])
        scatter_handles.append((units, handle))
    grads["mem_norm"] = grads["mem_norm"] + vjp_memory(g_mem_n)[0]["mem_norm"]
    g_first, g_x_norm = vjp_first_norm(g_h)
    grads["norm_mix"] = grads["norm_mix"] + g_first["norm_mix"]
    g_x = g_x + g_x_norm

    small_names = [n for n, _ in TINY] + REPL
    small_g = pack_rows([loss_vec[0, :1]] + [grads[n].astype(F32).reshape(-1) for n in small_names], PACK_W, 8)
    small_handle, _, _ = exchange_start("gather", [small_g], "gather_start_small_grads")

    g_recv = {}
    for units, handle in scatter_handles:
        landed = exchange_wait(handle, g_x, f"scatter_wait_{units[0][0]}_{units[0][1]}")
        g_recv.update(dict(zip(units, landed)))

    res = {}
    for n in BIG:
        outs = None
        for layer in range(w_loc[n].shape[0]):
            outs = adamw(g_recv[n, layer], w_loc[n], m_loc[n], v_loc[n], layer, outs, f"adamw_{n}_{layer}")
        for kind, a in zip(("grad", "delta", "m", "v"), outs):
            res[(kind, n)] = a
    small_recv = exchange_wait(small_handle, res[("grad", BIG[-1])], "gather_wait_small_grads")[0]

    def full_small(d):
        parts = [jnp.zeros((1,), F32)]
        for n, ax in TINY:
            full_shape = params[n].shape
            start = [0] * len(full_shape)
            start[ax] = me * d[n].shape[ax]
            parts.append(lax.dynamic_update_slice(jnp.zeros(full_shape, F32), d[n], start).reshape(-1))
        parts += [d[n].reshape(-1) for n in REPL]
        return pack_rows(parts, PACK_W, 8)

    outs_small = adamw(small_recv, full_small(w_loc)[None], full_small(m_loc)[None], full_small(v_loc)[None],
                       0, None, "adamw_small")
    small_shapes = [(1,)] + [params[n].shape for n, _ in TINY] + [w_loc[n].shape for n in REPL]
    loss = None
    for kind, packed in zip(("grad", "delta", "m", "v"), outs_small):
        parts = unpack_rows(packed[0], small_shapes)
        if kind == "grad":
            loss = parts[0][0]
        for (n, ax), a in zip(TINY, parts[1:1 + len(TINY)]):
            start = [0] * a.ndim
            start[ax] = me * w_loc[n].shape[ax]
            res[(kind, n)] = lax.dynamic_slice(a, start, w_loc[n].shape)
        for n, a in zip(REPL, parts[1 + len(TINY):]):
            res[(kind, n)] = a

    out = [loss, g_x[None]]
    for kind in ("grad", "delta", "m", "v"):
        out += [res[(kind, n)] for n in WEIGHTS]
    return tuple(out)
```

```python
import math

import jax
import jax.numpy as jnp
from jax import lax
from jax.experimental import pallas as pl
from jax.experimental.pallas import tpu as pltpu

F32 = jnp.float32
BF16 = jnp.bfloat16

N_DEV = 8
LANES = 128
EPS = 1e-6
ROPE_THETA = 10000.0
MLA_HEADS, MLA_NOPE, MLA_ROPE, MLA_V = 8, 128, 64, 128
MLA_Q_RANK, MLA_KV_RANK = 384, 256
GDN_HEADS, GDN_DK, GDN_CONV, GDN_CHUNK = 8, 128, 4, 64
X_HEADS, X_HEAD_DIM = 4, 256
DEPTH, N_MIXERS = 4, 3
ADAM_LR, ADAM_B1, ADAM_B2, ADAM_EPS, ADAM_WD, ADAM_STEP = 0.001, 0.9, 0.999, 1e-08, 0.01, 10
MLA_QUERY_GROUPS = 4
NEG_BIG = -1e30
PACK_W = 1024


_NN = (((1,), (0,)), ((), ()))
_NT = (((1,), (1,)), ((), ()))
_TN = (((0,), (0,)), ((), ()))
_NN3 = (((2,), (1,)), ((0,), (0,)))
_NT3 = (((2,), (2,)), ((0,), (0,)))
_TN3 = (((1,), (1,)), ((0,), (0,)))


def _dot(a, b, dims):
    return lax.dot_general(a, b, dims, preferred_element_type=F32)


def _hi_lo(x):
    hi = x.astype(BF16)
    return hi, (x - hi.astype(F32)).astype(BF16)


def _split3(x):
    hi = x.astype(BF16)
    r = x - hi.astype(F32)
    mid = r.astype(BF16)
    return hi, mid, (r - mid.astype(F32)).astype(BF16)


def _dg(a, b, dims, prec):
    if prec == "h":
        return lax.dot_general(a, b, dims, precision=lax.Precision.HIGHEST, preferred_element_type=F32)
    if prec == "m":
        a_hi, a_lo = _hi_lo(a)
        b_hi, b_lo = _hi_lo(b)
        return _dot(a_hi, b_hi, dims) + _dot(a_hi, b_lo, dims) + _dot(a_lo, b_hi, dims)
    return _dot(a.astype(BF16), b.astype(BF16), dims)


def _dg_sel(sel, x, dims, sel_first):
    s16 = sel.astype(BF16)
    parts = [(_dot(s16, piece, dims) if sel_first else _dot(piece, s16, dims)) for piece in _split3(x)]
    return parts[0] + parts[1] + parts[2]


class _Ops:
    def __init__(self, prec, differentiable, batched=False):
        d_nn, d_nt, d_tn = (_NN3, _NT3, _TN3) if batched else (_NN, _NT, _TN)

        def nn(a, b):
            return _dg(a, b, d_nn, prec)

        def nt(a, b):
            return _dg(a, b, d_nt, prec)

        def tn(a, b):
            return _dg(a, b, d_tn, prec)

        if differentiable:
            dnn = jax.custom_vjp(nn)
            dnn.defvjp(lambda a, b: (nn(a, b), (a, b)), lambda r, g: (nt(g, r[1]), tn(r[0], g)))
            dnt = jax.custom_vjp(nt)
            dnt.defvjp(lambda a, b: (nt(a, b), (a, b)), lambda r, g: (nn(g, r[1]), tn(g, r[0])))
            dtn = jax.custom_vjp(tn)
            dtn.defvjp(lambda a, b: (tn(a, b), (a, b)), lambda r, g: (nt(r[1], g), nn(r[0], g)))
            nn, nt, tn = dnn, dnt, dtn
        self.nn, self.nt, self.tn = nn, nt, tn


class _SelOps:
    def __init__(self, differentiable, batched=False):
        d_nn, d_nt, d_tn = (_NN3, _NT3, _TN3) if batched else (_NN, _NT, _TN)

        def sel_nn(sel, x):
            return _dg_sel(sel, x, d_nn, True)

        def sel_nt(sel, x):
            return _dg_sel(sel, x, d_nt, True)

        if differentiable:
            dnn = jax.custom_vjp(sel_nn)
            dnn.defvjp(lambda s, x: (sel_nn(s, x), s),
                       lambda s, g: (jnp.zeros_like(s), _dg_sel(s, g, d_tn, True)))
            dnt = jax.custom_vjp(sel_nt)
            dnt.defvjp(lambda s, x: (sel_nt(s, x), s),
                       lambda s, g: (jnp.zeros_like(s), _dg_sel(s, g, d_tn, False)))
            sel_nn, sel_nt = dnn, dnt
        self.sel_nn, self.sel_nt = sel_nn, sel_nt


class _OpSet:
    def __init__(self, differentiable):
        self.b = _Ops("b", differentiable)
        self.h = _Ops("h", differentiable)
        self.bb = _Ops("b", differentiable, batched=True)
        self.bm = _Ops("m", differentiable, batched=True)
        self.bs = _SelOps(differentiable, batched=True)


_PLAIN = _OpSet(False)
_DIFF = _OpSet(True)


def _params(sem):
    return pltpu.CompilerParams(dimension_semantics=sem)


BLOCK_BYTES = 4 * 1024 * 1024


def _pick(n, cands):
    for c in cands:
        if n % c == 0:
            return c
    return n


def _tile(n, cap):
    if n <= cap:
        return n
    return _pick(n, tuple(c for c in (2048, 1024, 768, 512, 384, 256, 128) if c <= cap))


def matmul(a, b, form, out_dtype, name, res=None, blocked=False, relu_gate=None, rms_gain=None, a_relu2=False):
    if form == "nn":
        m, k = a.shape
        k2, n = (b.shape[1], N_DEV * b.shape[2]) if blocked else b.shape
    elif form == "nt":
        m, k = a.shape
        n, k2 = (b.shape[1], N_DEV * b.shape[2]) if blocked else b.shape
    else:
        (k, m), (k2, n) = a.shape, b.shape
    assert k == k2, (a.shape, b.shape, form)
    tk = k if k <= 2048 else _tile(k, 1024)
    cb = nb = 1
    if blocked:
        cb = (k if form == "nt" else n) // N_DEV
        nb = _pick(N_DEV, tuple(c for c in (8, 4, 2, 1) if c * cb <= 1024))
    if blocked and form == "nt":
        tk = nb * cb
    if blocked and form != "nt":
        tn = nb * cb
    else:
        tn = _tile(n, min(1024, BLOCK_BYTES // (tk * b.dtype.itemsize)))
    out_elems = BLOCK_BYTES // 2 if (out_dtype == BF16 and res is None) else BLOCK_BYTES // 4
    tm = _tile(m, min(BLOCK_BYTES // (tk * a.dtype.itemsize), out_elems // tn))
    nk = k // tk
    dims = {"nn": _NN, "nt": _NT, "tn": _TN}[form]

    a_spec = {"nn": pl.BlockSpec((tm, tk), lambda i, j, kk: (i, kk)),
              "nt": pl.BlockSpec((tm, tk), lambda i, j, kk: (i, kk)),
              "tn": pl.BlockSpec((tk, tm), lambda i, j, kk: (kk, i))}[form]
    if blocked and form == "nn":
        b_spec = pl.BlockSpec((nb, tk, cb), lambda i, j, kk: (j, kk, 0))
    elif blocked and form == "nt":
        b_spec = pl.BlockSpec((nb, tn, cb), lambda i, j, kk: (kk, j, 0))
    else:
        b_spec = {"nn": pl.BlockSpec((tk, tn), lambda i, j, kk: (kk, j)),
                  "nt": pl.BlockSpec((tn, tk), lambda i, j, kk: (j, kk)),
                  "tn": pl.BlockSpec((tk, tn), lambda i, j, kk: (kk, j))}[form]
    c_spec = pl.BlockSpec((tm, tn), lambda i, j, kk: (i, j))
    out_shape = jax.ShapeDtypeStruct((m, n), out_dtype)
    o_spec = c_spec
    blocked_out = blocked and form == "tn"
    if blocked_out:
        out_shape = jax.ShapeDtypeStruct((N_DEV, m, cb), out_dtype)
        o_spec = pl.BlockSpec((nb, tm, cb), lambda i, j, kk: (j, i, 0))
    has_res, has_gate, has_gain = res is not None, relu_gate is not None, rms_gain is not None
    extras = [e for e in (res, relu_gate) if e is not None]
    n_in = 2 + len(extras) + has_gain
    second = has_gain
    assert not (second and (blocked_out or tn != n))

    def body(*refs):
        a_ref, b_ref = refs[0], refs[1]
        r_ref = refs[2] if has_res else None
        gate_ref = refs[2 + has_res] if has_gate else None
        gain_ref = refs[n_in - 1] if has_gain else None
        o_ref = refs[n_in]
        if a_relu2:
            relu = jnp.maximum(a_ref[...].astype(F32), 0.0)
            a_val = (relu * relu).astype(BF16)
        else:
            a_val = a_ref[...].astype(BF16)
        if blocked and form == "nn":
            part = jnp.concatenate([_dot(a_val, b_ref[t].astype(BF16), dims) for t in range(nb)], axis=-1)
        elif blocked and form == "nt":
            part = _dot(a_val[:, :cb], b_ref[0].astype(BF16), dims)
            for t in range(1, nb):
                part = part + _dot(a_val[:, t * cb:(t + 1) * cb], b_ref[t].astype(BF16), dims)
        else:
            part = _dot(a_val, b_ref[...].astype(BF16), dims)

        def finish(acc):
            if has_res:
                acc = acc + r_ref[...].astype(F32)
            if has_gate:
                acc = acc * (2.0 * jnp.maximum(gate_ref[...].astype(F32), 0.0))
            if blocked_out:
                for t in range(nb):
                    o_ref[t] = acc[:, t * cb:(t + 1) * cb].astype(out_dtype)
            else:
                o_ref[...] = acc.astype(out_dtype)
            if has_gain:
                refs[n_in + 1][...] = _rms(acc, gain_ref[...]).astype(BF16)

        if nk == 1:
            finish(part)
        else:
            acc_ref = refs[-1]
            kk = pl.program_id(2)

            @pl.when(kk == 0)
            def _():
                acc_ref[...] = part

            @pl.when(jnp.logical_and(kk > 0, kk < nk - 1))
            def _():
                acc_ref[...] += part

            @pl.when(kk == nk - 1)
            def _():
                finish(acc_ref[...] + part)

    in_specs = [a_spec, b_spec] + [c_spec] * len(extras)
    args = [a, b] + extras
    if has_gain:
        in_specs.append(pl.BlockSpec((1, tn), lambda i, j, kk: (0, j)))
        args.append(rms_gain)
    if second:
        out_shape = [out_shape, jax.ShapeDtypeStruct((m, n), BF16)]
        o_spec = [o_spec, c_spec]
    return pl.pallas_call(
        body, name=name,
        out_shape=out_shape,
        grid=(m // tm, n // tn, nk),
        in_specs=in_specs, out_specs=o_spec,
        scratch_shapes=[pltpu.VMEM((tm, tn), F32)] if nk > 1 else [],
        compiler_params=_params(("parallel", "parallel", "arbitrary")),
    )(*args)


def make_mm(name, out_dtype, with_res=False, blocked=False):
    def bwd_mm(a, w, g):
        da = matmul(g, w, "nt", a.dtype, name + "_da", blocked=blocked)
        dw = matmul(a, g, "tn", w.dtype, name + "_dw", blocked=blocked)
        return da, dw

    if with_res:
        @jax.custom_vjp
        def op(res, a, w):
            return matmul(a, w, "nn", out_dtype, name + "_f", res=res, blocked=blocked)

        def fwd(res, a, w):
            return op(res, a, w), (a, w)

        def bwd(saved, g):
            return (g,) + bwd_mm(*saved, g)
    else:
        @jax.custom_vjp
        def op(a, w):
            return matmul(a, w, "nn", out_dtype, name + "_f", blocked=blocked)

        def fwd(a, w):
            return op(a, w), (a, w)

        def bwd(saved, g):
            return bwd_mm(*saved, g)
    op.defvjp(fwd, bwd)
    return op


def _rms_fan_bwd(x_new, gain, dx, dh, name):
    rows, d = x_new.shape
    outs = [("row", d, F32), ("row", d, BF16)]
    return tile_bwd(fn_fan_rms, name, ["row", "par"], [x_new, gain], [True, True], outs, [dx, dh],
                    rows, min(1024, rows), 0)


def make_mm_res_rms(name):
    @jax.custom_vjp
    def op(res, a, w, gain):
        return tuple(matmul(a, w, "nn", F32, name + "_f", res=res, rms_gain=gain))

    def fwd(res, a, w, gain):
        x_new, h = op(res, a, w, gain)
        return (x_new, h), (a, w, x_new, gain)

    def bwd(saved, cts):
        a, w, x_new, gain = saved
        dx, dgain = _rms_fan_bwd(x_new, gain, cts[0], cts[1], name + "_nb")
        da = matmul(dx, w, "nt", a.dtype, name + "_da")
        dw = matmul(a, dx, "tn", w.dtype, name + "_dw")
        return dx, da, dw, dgain

    op.defvjp(fwd, bwd)
    return op


def make_mlp(name, with_norm):
    def run(x, h, w1, w2, gain):
        a = matmul(h, w1, "nn", BF16, name + "_1_f", blocked=True)
        out = matmul(a, w2, "nn", F32, name + "_2_f", res=x, rms_gain=gain if with_norm else None, a_relu2=True)
        return (tuple(out) if with_norm else (out,)), a

    @jax.custom_vjp
    def op(x, h, w1, w2, gain):
        return run(x, h, w1, w2, gain)[0]

    def fwd(x, h, w1, w2, gain):
        out, a = run(x, h, w1, w2, gain)
        return out, (h, w1, w2, gain, a, out[0])

    def bwd(saved, cts):
        h, w1, w2, gain, a, x_new = saved
        if with_norm:
            dx, dgain = _rms_fan_bwd(x_new, gain, cts[0], cts[1], name + "_nb")
        else:
            dx, dgain = cts[0], None
        da = matmul(dx, w2, "nt", BF16, name + "_2_da", relu_gate=a)
        dw2 = matmul(a, dx, "tn", w2.dtype, name + "_2_dw", a_relu2=True)
        dh = matmul(da, w1, "nt", h.dtype, name + "_1_da", blocked=True)
        dw1 = matmul(h, da, "tn", w1.dtype, name + "_1_dw", blocked=True)
        return dx, dh, dw1, dw2, dgain

    op.defvjp(fwd, bwd)
    return op


def _kind(k):
    if isinstance(k, str):
        return k, None, 1
    return k[0], k[1], (k[2] if len(k) > 2 else 1)


def _tile_spec(kind, shape, tm, heads):
    k, d, ns = _kind(kind)
    if k == "row":
        return pl.BlockSpec((tm, shape[1]), (lambda h, i: (i, 0)) if heads else (lambda i: (i, 0)))
    if k == "par":
        return pl.BlockSpec(tuple(shape), (lambda h, i: (0, 0)) if heads else (lambda i: (0, 0)))
    if k == "rowh":
        return pl.BlockSpec((tm, d * ns), lambda h, i: (i, h))
    if k == "parh":
        return pl.BlockSpec((shape[0], d * ns), lambda h, i: (0, h))
    raise ValueError(kind)


def _tile_grid(rows, tm, heads):
    n_rows = rows // tm
    return ((heads, n_rows) if heads else (n_rows,)), (1 if heads else 0)


def _split_vals(kinds, refs):
    vals, counts = [], []
    for kind, r in zip(kinds, refs):
        _, d, ns = _kind(kind)
        v = r[...].astype(F32)
        vals += [v] if ns == 1 else [v[:, p * d:(p + 1) * d] for p in range(ns)]
        counts.append(ns)
    return vals, counts


def tile_fwd(fn, name, kinds, args, outs, rows, tm, heads, row_base=0):
    grid, row_axis = _tile_grid(rows, tm, heads)
    n_in = len(args)
    out_shapes = [jax.ShapeDtypeStruct((rows, w), dt) for (_, w, dt) in outs]

    def body(*refs):
        vals, _ = _split_vals(kinds, refs[:n_in])
        row0 = row_base + pl.program_id(row_axis) * tm
        res = list(fn(_PLAIN, row0, *vals))
        for o_ref, (k, _, _) in zip(refs[n_in:], outs):
            pieces = [res.pop(0) for _ in range(_kind(k)[2])]
            v = pieces[0] if len(pieces) == 1 else jnp.concatenate(pieces, axis=-1)
            o_ref[...] = v.astype(o_ref.dtype)

    return pl.pallas_call(
        body, name=name, out_shape=out_shapes, grid=grid,
        in_specs=[_tile_spec(k, a.shape, tm, heads) for k, a in zip(kinds, args)],
        out_specs=[_tile_spec(k, (rows, w), tm, heads) for (k, w, _) in outs],
        compiler_params=_params(("arbitrary",) * len(grid)),
    )(*args)


def tile_bwd(fn, name, kinds, args, diff, outs, cts, rows, tm, heads, row_base=0):
    grid, row_axis = _tile_grid(rows, tm, heads)
    n_in, n_ct = len(args), len(cts)
    diff_idx = [i for i, d in enumerate(diff) if d]
    g_shapes, g_specs = [], []
    for i in diff_idx:
        k = _kind(kinds[i])[0]
        dt = args[i].dtype if k in ("row", "rowh") else F32
        g_shapes.append(jax.ShapeDtypeStruct(args[i].shape, dt))
        g_specs.append(_tile_spec(kinds[i], args[i].shape, tm, heads))

    def body(*refs):
        in_refs, ct_refs, g_refs = refs[:n_in], refs[n_in:n_in + n_ct], refs[n_in + n_ct:]
        vals, counts = _split_vals(kinds, in_refs)
        first_piece = [sum(counts[:i]) for i in range(n_in)]
        flat_diff = [first_piece[i] + p for i in diff_idx for p in range(counts[i])]
        row_id = pl.program_id(row_axis)
        row0 = row_base + row_id * tm

        def f(*dvals):
            full = list(vals)
            for i, dv in zip(flat_diff, dvals):
                full[i] = dv
            return tuple(fn(_DIFF, row0, *full))

        _, vjp = jax.vjp(f, *[vals[i] for i in flat_diff])
        ct_vals, _ = _split_vals([k for (k, _, _) in outs], ct_refs)
        flat_grads = list(vjp(tuple(ct_vals)))
        for g_ref, i in zip(g_refs, diff_idx):
            pieces = [flat_grads.pop(0) for _ in range(counts[i])]
            g = pieces[0] if len(pieces) == 1 else jnp.concatenate(pieces, axis=-1)
            k = _kind(kinds[i])[0]
            if k in ("row", "rowh"):
                g_ref[...] = g.astype(g_ref.dtype)
            else:
                first = row_id == 0
                if heads and k == "par":
                    first = jnp.logical_and(first, pl.program_id(0) == 0)

                @pl.when(first)
                def _(g_ref=g_ref, g=g):
                    g_ref[...] = g

                @pl.when(jnp.logical_not(first))
                def _(g_ref=g_ref, g=g):
                    g_ref[...] += g

    return pl.pallas_call(
        body, name=name, out_shape=g_shapes, grid=grid,
        in_specs=[_tile_spec(k, a.shape, tm, heads) for k, a in zip(kinds, args)]
        + [_tile_spec(k, (rows, w), tm, heads) for (k, w, _) in outs],
        out_specs=g_specs,
        compiler_params=_params(("arbitrary",) * len(grid)),
    )(*args, *cts)


def make_tile_op(fn, name, kinds, diff, outs, rows, tm, heads=0, row_base=0):
    tm = min(tm, rows)

    @jax.custom_vjp
    def op(*args):
        return tuple(tile_fwd(fn, name + "_f", kinds, args, outs, rows, tm, heads, row_base))

    def fwd(*args):
        return op(*args), args

    def bwd(args, cts):
        grads = tile_bwd(fn, name + "_b", kinds, args, diff, outs, cts, rows, tm, heads, row_base)
        it = iter(grads)
        res = []
        for a, d in zip(args, diff):
            res.append(next(it).astype(a.dtype) if d else None)
        return tuple(res)

    op.defvjp(fwd, bwd)
    return op


def _rms(x, g):
    return x * lax.rsqrt(jnp.mean(x * x, axis=-1, keepdims=True) + EPS) * g


def fn_rms(ops, row0, x, g):
    return (_rms(x, g),)


def fn_fan_rms(ops, row0, x, g):
    return x, _rms(x, g)


def fn_mla_down(ops, row0, h, c, s, w_cq, w_ckv, w_kr, w_krs, g_q, g_kv):
    c_qn = _rms(ops.b.nn(h, w_cq), g_q)
    c_kvn = _rms(ops.b.nn(h, w_ckv), g_kv)
    return c_qn, c_kvn, ops.b.nn(h, w_kr) * c + ops.b.nn(h, w_krs) * s


def fn_mla_up(ops, row0, c_qn, c_kvn, c, s, w_qn, w_qr, w_qrs, w_kv):
    c_all = jnp.concatenate([c] * MLA_HEADS, axis=-1)
    s_all = jnp.concatenate([s] * MLA_HEADS, axis=-1)
    q_rope = ops.b.nn(c_qn, w_qr) * c_all + ops.b.nn(c_qn, w_qrs) * s_all
    return ops.b.nn(c_qn, w_qn), q_rope, ops.b.nn(c_kvn, w_kv)


def _softmax(s):
    m = lax.stop_gradient(jnp.max(s, axis=-1, keepdims=True))
    e = jnp.exp(s - m)
    return e / jnp.sum(e, axis=-1, keepdims=True)


def fn_xattn(ops, row0, hx, *t):
    w_q, k, v = t[:X_HEADS], t[X_HEADS:2 * X_HEADS], t[2 * X_HEADS:]
    outs = []
    for w_h, k_h, v_h in zip(w_q, k, v):
        s = ops.b.nt(ops.b.nn(hx, w_h), k_h) * (X_HEAD_DIM ** -0.5)
        outs.append(ops.b.nn(_softmax(s), v_h))
    return tuple(outs)


def _silu(x):
    return x * jax.nn.sigmoid(x)


def fn_gdn_prep(ops, row0, *t):
    nh = len(t) // 3
    qs, ks, vs = [], [], []
    for qc, kc, vc in zip(t[:nh], t[nh:2 * nh], t[2 * nh:]):
        q, k = _silu(qc), _silu(kc)
        qs.append(q * lax.rsqrt(jnp.sum(q * q, -1, keepdims=True) + EPS) * (GDN_DK ** -0.5))
        ks.append(k * lax.rsqrt(jnp.sum(k * k, -1, keepdims=True) + EPS))
        vs.append(_silu(vc))
    return tuple(qs + ks + vs)


def fn_gdn_gates(ops, row0, ba, alog, dtb):
    width = GDN_HEADS * GDN_DK
    beta = jax.nn.sigmoid(ba)
    z = ba + dtb
    softplus = jnp.maximum(z, 0.0) + jnp.log1p(jnp.exp(-jnp.abs(z)))
    g = -jnp.exp(alog) * softplus
    r = lax.broadcasted_iota(jnp.int32, (LANES, width), 0)
    c = lax.broadcasted_iota(jnp.int32, (LANES, width), 1) // GDN_DK
    e_beta = (r == c).astype(F32)
    e_g = (r == c + GDN_HEADS).astype(F32)
    return ops.h.nn(beta, e_beta), ops.h.nn(g, e_g)


def fn_gdn_out(ops, row0, *t):
    nh = (len(t) - 1) // 2
    g = t[-1]
    return tuple(_rms(o, g) * _silu(gate) for o, gate in zip(t[:nh], t[nh:2 * nh]))


def fn_mla_attn(ops, row0, qn, qr, kn, v, kr):
    s = (ops.b.nt(qn, kn) + ops.b.nt(qr, kr)) * ((MLA_NOPE + MLA_ROPE) ** -0.5)
    rows = row0 + lax.broadcasted_iota(jnp.int32, s.shape, 0)
    cols = lax.broadcasted_iota(jnp.int32, s.shape, 1)
    s = jnp.where(rows >= cols, s, NEG_BIG)
    return (ops.b.nn(_softmax(s), v),)


def _shift_down(x, d, t_idx):
    if d == 0:
        return x
    return jnp.where(t_idx >= d, pltpu.roll(x, d, axis=0), 0.0)


def _shift_up(x, d, t_idx):
    if d == 0:
        return x
    n = x.shape[0]
    return jnp.where(t_idx < n - d, pltpu.roll(x, n - d, axis=0), 0.0)


def conv_fwd(x, w, name):
    s, c = x.shape
    kw = w.shape[0]
    tc = _pick(c, (256, 128))

    def body(x_ref, w_ref, y_ref):
        xv = x_ref[...]
        t_idx = lax.broadcasted_iota(jnp.int32, xv.shape, 0)
        acc = jnp.zeros_like(xv)
        for j in range(kw):
            acc = acc + w_ref[j:j + 1, :] * _shift_down(xv, kw - 1 - j, t_idx)
        y_ref[...] = acc

    return pl.pallas_call(
        body, name=name, out_shape=jax.ShapeDtypeStruct((s, c), F32), grid=(c // tc,),
        in_specs=[pl.BlockSpec((s, tc), lambda i: (0, i)), pl.BlockSpec((kw, tc), lambda i: (0, i))],
        out_specs=pl.BlockSpec((s, tc), lambda i: (0, i)),
        compiler_params=_params(("parallel",)),
    )(x, w)


def conv_bwd(x, w, dy, name):
    s, c = x.shape
    kw = w.shape[0]
    tc = _pick(c, (256, 128))

    def body(x_ref, w_ref, dy_ref, dx_ref, dw_ref):
        xv, dyv = x_ref[...], dy_ref[...]
        t_idx = lax.broadcasted_iota(jnp.int32, xv.shape, 0)
        dx = jnp.zeros_like(xv)
        for j in range(kw):
            d = kw - 1 - j
            dx = dx + w_ref[j:j + 1, :] * _shift_up(dyv, d, t_idx)
            dw_ref[j:j + 1, :] = jnp.sum(dyv * _shift_down(xv, d, t_idx), axis=0, keepdims=True)
        dx_ref[...] = dx

    return pl.pallas_call(
        body, name=name,
        out_shape=[jax.ShapeDtypeStruct((s, c), F32), jax.ShapeDtypeStruct((kw, c), F32)],
        grid=(c // tc,),
        in_specs=[pl.BlockSpec((s, tc), lambda i: (0, i)), pl.BlockSpec((kw, tc), lambda i: (0, i)),
                  pl.BlockSpec((s, tc), lambda i: (0, i))],
        out_specs=[pl.BlockSpec((s, tc), lambda i: (0, i)), pl.BlockSpec((kw, tc), lambda i: (0, i))],
        compiler_params=_params(("parallel",)),
    )(x, w, dy)


def _conv_taps(x, w_ref, t_idx):
    kw = w_ref.shape[0]
    acc = jnp.zeros_like(x)
    for j in range(kw):
        acc = acc + w_ref[j:j + 1, :] * _shift_down(x, kw - 1 - j, t_idx)
    return acc


def gated_conv_fwd(b, c, u, w, name):
    s, ch = c.shape
    kw = w.shape[0]
    tc = _pick(ch, (256, 128))

    def body(b_ref, c_ref, u_ref, w_ref, y_ref):
        x = c_ref[...] * u_ref[...]
        t_idx = lax.broadcasted_iota(jnp.int32, x.shape, 0)
        y_ref[...] = (b_ref[...] * _conv_taps(x, w_ref, t_idx)).astype(y_ref.dtype)

    blk = pl.BlockSpec((s, tc), lambda i: (0, i))
    return pl.pallas_call(
        body, name=name, out_shape=jax.ShapeDtypeStruct((s, ch), BF16), grid=(ch // tc,),
        in_specs=[blk, blk, blk, pl.BlockSpec((kw, tc), lambda i: (0, i))], out_specs=blk,
        compiler_params=_params(("parallel",)),
    )(b, c, u, w)


def gated_conv_bwd(b, c, u, w, dy, name):
    s, ch = c.shape
    kw = w.shape[0]
    tc = _pick(ch, (256, 128))

    def body(b_ref, c_ref, u_ref, w_ref, dy_ref, db_ref, dc_ref, du_ref, dw_ref):
        cv, uv, dyv = c_ref[...], u_ref[...], dy_ref[...].astype(F32)
        x = cv * uv
        t_idx = lax.broadcasted_iota(jnp.int32, x.shape, 0)
        db_ref[...] = dyv * _conv_taps(x, w_ref, t_idx)
        dconv = dyv * b_ref[...]
        dx = jnp.zeros_like(x)
        for j in range(kw):
            d = kw - 1 - j
            dx = dx + w_ref[j:j + 1, :] * _shift_up(dconv, d, t_idx)
            dw_ref[j:j + 1, :] = jnp.sum(dconv * _shift_down(x, d, t_idx), axis=0, keepdims=True)
        dc_ref[...] = dx * uv
        du_ref[...] = dx * cv

    blk = pl.BlockSpec((s, tc), lambda i: (0, i))
    wblk = pl.BlockSpec((kw, tc), lambda i: (0, i))
    return pl.pallas_call(
        body, name=name,
        out_shape=[jax.ShapeDtypeStruct((s, ch), F32)] * 3 + [jax.ShapeDtypeStruct((kw, ch), F32)],
        grid=(ch // tc,), in_specs=[blk, blk, blk, wblk, blk], out_specs=[blk, blk, blk, wblk],
        compiler_params=_params(("parallel",)),
    )(b, c, u, w, dy)


def make_gated_conv(name):
    @jax.custom_vjp
    def op(b, c, u, w):
        return gated_conv_fwd(b, c, u, w, name + "_f")

    def fwd(b, c, u, w):
        return op(b, c, u, w), (b, c, u, w)

    def bwd(saved, dy):
        return tuple(gated_conv_bwd(*saved, dy, name + "_b"))

    op.defvjp(fwd, bwd)
    return op


def make_conv(name):
    @jax.custom_vjp
    def op(x, w):
        return conv_fwd(x, w, name + "_f")

    def fwd(x, w):
        return op(x, w), (x, w)

    def bwd(saved, dy):
        dx, dw = conv_bwd(saved[0], saved[1], dy, name + "_b")
        return dx, dw

    op.defvjp(fwd, bwd)
    return op


def _gdn_consts():
    c, d = GDN_CHUNK, GDN_DK
    i = lax.broadcasted_iota(jnp.int32, (c, c), 0)
    j = lax.broadcasted_iota(jnp.int32, (c, c), 1)
    tri = i >= j
    return dict(
        tri=tri, strict=i > j,
        tri_f=tri.astype(F32),
        eye=(i == j).astype(F32),
        lane0=(lax.broadcasted_iota(jnp.int32, (c, d), 1) == 0).astype(F32),
        last_row=(lax.broadcasted_iota(jnp.int32, (c, d), 0) == c - 1).astype(F32),
    )


def _inverse_given(m_ops):
    @jax.custom_vjp
    def given(mm_, t):
        return t

    def bwd(t, dt):
        return -m_ops.nt(m_ops.tn(t, dt), t), jnp.zeros_like(t)

    given.defvjp(lambda mm_, t: (t, t), bwd)
    return given


def _gdn_chunk(ops, q, k, v, g, beta, state, t_saved=None):
    b, m, sel = ops.bb, ops.bm, ops.bs
    nh, c, d = q.shape[0], GDN_CHUNK, GDN_DK
    k_ = _gdn_consts()

    def per_head(a):
        return jnp.broadcast_to(a, (nh,) + a.shape)

    gc = sel.sel_nn(per_head(k_["tri_f"]), g)
    col = jnp.broadcast_to(jnp.sum(gc * k_["lane0"], axis=2, keepdims=True), (nh, c, c))
    row = sel.sel_nt(per_head(k_["lane0"]), gc)
    decay = jnp.where(k_["tri"], jnp.exp(jnp.where(k_["tri"], col - row, 0.0)), 0.0)
    kb = k * beta
    mm_ = jnp.where(k_["strict"], b.nt(kb, k) * decay, 0.0)
    if t_saved is None:
        p = -mm_
        t = k_["eye"] + p
        for _ in range(int(math.log2(GDN_CHUNK)) - 1):
            p = m.nn(p, p)
            t = t + m.nn(t, p)
    else:
        t = _inverse_given(_PLAIN.bm)(mm_, t_saved)
    egc = jnp.exp(gc)
    u = b.nn(t, v * beta)
    w = b.nn(t, kb * egc)
    attn = b.nt(q, k) * decay
    v_new = u - b.nn(w, state)
    o = b.nn(q * egc, state) + b.nn(attn, v_new)
    g_last = jnp.sum(gc * k_["last_row"], axis=1, keepdims=True)
    new_state = (state * jnp.exp(jnp.broadcast_to(g_last, (nh, d, d)))
                 + b.tn(k * jnp.exp(jnp.broadcast_to(g_last, (nh, c, d)) - gc), v_new))
    return o, new_state, t


GDN_HEAD_GROUP = 8
GDN_TILE_CHUNKS = 4


def _heads_of(ref, rows, n_heads):
    d = GDN_DK
    return jnp.stack([ref[rows, h * d:(h + 1) * d] for h in range(n_heads)])


def _gdn_specs(s, reverse):
    d, hg = GDN_DK, GDN_HEAD_GROUP
    tile = min(GDN_TILE_CHUNKS * GDN_CHUNK, s)
    n_tiles = s // tile
    t_of = (lambda t: n_tiles - 1 - t) if reverse else (lambda t: t)
    seq = pl.BlockSpec((tile, hg * d), lambda grp, t: (t_of(t), grp))
    st = pl.BlockSpec((hg, tile // GDN_CHUNK, d, d), lambda grp, t: (grp, t_of(t), 0, 0))
    inv = pl.BlockSpec((hg, tile // GDN_CHUNK, GDN_CHUNK, GDN_CHUNK), lambda grp, t: (grp, t_of(t), 0, 0))
    return seq, st, inv, tile, n_tiles


def gdn_fwd(q, k, v, g, beta, name):
    s = q.shape[0]
    d, hg = GDN_DK, GDN_HEAD_GROUP
    seq, st, inv, tile, n_tiles = _gdn_specs(s, False)

    def body(q_ref, k_ref, v_ref, g_ref, b_ref, o_ref, st_ref, inv_ref, state_scr):
        @pl.when(pl.program_id(1) == 0)
        def _():
            state_scr[...] = jnp.zeros_like(state_scr)

        def step(ci, carry):
            rows = pl.ds(pl.multiple_of(ci * GDN_CHUNK, GDN_CHUNK), GDN_CHUNK)
            state = state_scr[...]
            for h in range(hg):
                st_ref[h, ci] = state[h]
            o, new_state, t = _gdn_chunk(_PLAIN, *[_heads_of(r, rows, hg) for r in (q_ref, k_ref, v_ref, g_ref, b_ref)],
                                         state)
            for h in range(hg):
                o_ref[rows, h * d:(h + 1) * d] = o[h]
                inv_ref[h, ci] = t[h]
            state_scr[...] = new_state
            return carry

        lax.fori_loop(0, tile // GDN_CHUNK, step, 0)

    return pl.pallas_call(
        body, name=name,
        out_shape=[jax.ShapeDtypeStruct(q.shape, F32),
                   jax.ShapeDtypeStruct((GDN_HEADS, s // GDN_CHUNK, d, d), F32),
                   jax.ShapeDtypeStruct((GDN_HEADS, s // GDN_CHUNK, GDN_CHUNK, GDN_CHUNK), F32)],
        grid=(GDN_HEADS // hg, n_tiles), in_specs=[seq] * 5, out_specs=[seq, st, inv],
        scratch_shapes=[pltpu.VMEM((hg, d, d), F32)],
        compiler_params=_params(("parallel", "arbitrary")),
    )(q, k, v, g, beta)


def gdn_bwd(q, k, v, g, beta, states, inverses, do, name):
    s = q.shape[0]
    d, hg = GDN_DK, GDN_HEAD_GROUP
    seq, st, inv, tile, n_tiles = _gdn_specs(s, True)
    tile_chunks = tile // GDN_CHUNK

    def body(q_ref, k_ref, v_ref, g_ref, b_ref, st_ref, inv_ref, do_ref, dq_ref, dk_ref, dv_ref, dg_ref, db_ref,
             dstate_scr):
        @pl.when(pl.program_id(1) == 0)
        def _():
            dstate_scr[...] = jnp.zeros_like(dstate_scr)

        def step(it, carry):
            ci = tile_chunks - 1 - it
            rows = pl.ds(pl.multiple_of(ci * GDN_CHUNK, GDN_CHUNK), GDN_CHUNK)
            prim = [_heads_of(r, rows, hg) for r in (q_ref, k_ref, v_ref, g_ref, b_ref)]
            prim.append(jnp.stack([st_ref[h, ci] for h in range(hg)]))
            t_saved = jnp.stack([inv_ref[h, ci] for h in range(hg)])
            _, vjp = jax.vjp(lambda *a: _gdn_chunk(_DIFF, *a, t_saved=t_saved)[:2], *prim)
            grads = vjp((_heads_of(do_ref, rows, hg), dstate_scr[...]))
            for g_ref_out, gr in zip((dq_ref, dk_ref, dv_ref, dg_ref, db_ref), grads[:5]):
                for h in range(hg):
                    g_ref_out[rows, h * d:(h + 1) * d] = gr[h]
            dstate_scr[...] = grads[5]
            return carry

        lax.fori_loop(0, tile_chunks, step, 0)

    return pl.pallas_call(
        body, name=name,
        out_shape=[jax.ShapeDtypeStruct(q.shape, F32)] * 5,
        grid=(GDN_HEADS // hg, n_tiles), in_specs=[seq] * 5 + [st, inv, seq], out_specs=[seq] * 5,
        scratch_shapes=[pltpu.VMEM((hg, d, d), F32)],
        compiler_params=_params(("parallel", "arbitrary")),
    )(q, k, v, g, beta, states, inverses, do)


def make_gdn(name):
    @jax.custom_vjp
    def op(q, k, v, g, beta):
        return gdn_fwd(q, k, v, g, beta, name + "_f")[0]

    def fwd(q, k, v, g, beta):
        o, states, inverses = gdn_fwd(q, k, v, g, beta, name + "_f")
        return o, (q, k, v, g, beta, states, inverses)

    def bwd(saved, do):
        return tuple(gdn_bwd(*saved, do, name + "_b"))

    op.defvjp(fwd, bwd)
    return op


def loss_head(x, g, target, name):
    s, d = x.shape
    tm = min(256, s)

    def body(x_ref, g_ref, t_ref, loss_ref, dx_ref, dg_ref):
        tgt = t_ref[...]

        def f(xv, gv):
            err = _rms(xv, gv) - tgt
            per_row = jnp.mean(err * err, axis=-1, keepdims=True)
            return 0.5 * jnp.sum(per_row, axis=0, keepdims=True)

        val, vjp = jax.vjp(f, x_ref[...], g_ref[...])
        dx, dg = vjp(jnp.ones((1, 1), F32))
        dx_ref[...] = dx
        first = pl.program_id(0) == 0

        @pl.when(first)
        def _():
            dg_ref[...] = dg
            loss_ref[...] = jnp.broadcast_to(val, loss_ref.shape)

        @pl.when(jnp.logical_not(first))
        def _():
            dg_ref[...] += dg
            loss_ref[...] += jnp.broadcast_to(val, loss_ref.shape)

    row = pl.BlockSpec((tm, d), lambda i: (i, 0))
    vec = pl.BlockSpec((1, d), lambda i: (0, 0))
    return pl.pallas_call(
        body, name=name,
        out_shape=[jax.ShapeDtypeStruct((1, LANES), F32), jax.ShapeDtypeStruct((s, d), F32),
                   jax.ShapeDtypeStruct((1, d), F32)],
        grid=(s // tm,), in_specs=[row, vec, row],
        out_specs=[pl.BlockSpec((1, LANES), lambda i: (0, 0)), row, vec],
        compiler_params=_params(("arbitrary",)),
    )(x, g, target)


def adamw(g8, w, m, v, layer, prev, name):
    n_layers, rows, width = w.shape
    tr = _pick(rows, (256, 128, 64, 32, 16, 8))

    def body(g_ref, w_ref, m_ref, v_ref, *rest):
        go_ref, d_ref, mo_ref, vo_ref = rest[-4:]
        g = g_ref[0].astype(F32)
        for p in range(1, N_DEV):
            g = g + g_ref[p].astype(F32)
        m_new = ADAM_B1 * m_ref[...] + (1.0 - ADAM_B1) * g
        v_new = ADAM_B2 * v_ref[...] + (1.0 - ADAM_B2) * (g * g)
        m_hat = m_new / (1.0 - ADAM_B1 ** ADAM_STEP)
        v_hat = v_new / (1.0 - ADAM_B2 ** ADAM_STEP)
        go_ref[...] = g
        d_ref[...] = -ADAM_LR * (m_hat / (jnp.sqrt(v_hat) + ADAM_EPS) + ADAM_WD * w_ref[...])
        mo_ref[...] = m_new
        vo_ref[...] = v_new

    blk = pl.BlockSpec((None, tr, width), lambda i: (layer, i, 0))
    carried = list(prev) if prev is not None else []
    return pl.pallas_call(
        body, name=name, out_shape=[jax.ShapeDtypeStruct((n_layers, rows, width), F32)] * 4,
        grid=(rows // tr,),
        in_specs=[pl.BlockSpec((N_DEV, tr, width), lambda i: (0, i, 0)), blk, blk, blk]
        + [pl.BlockSpec(memory_space=pl.ANY)] * len(carried),
        out_specs=[blk] * 4,
        input_output_aliases={4 + j: j for j in range(len(carried))},
        compiler_params=_params(("parallel",)),
    )(g8, w, m, v, *carried)


_HBM = pl.BlockSpec(memory_space=pltpu.HBM)
_SEM = pl.BlockSpec(memory_space=pltpu.SEMAPHORE)
_EFFECT = pltpu.SideEffectType.DATAFLOW_SIDE_EFFECTING


def _exchange_copies(mode, src_refs, land_refs, send_sems, recv_sems, local_sems):
    x, y, c = lax.axis_index("x"), lax.axis_index("y"), lax.axis_index("c")
    me = 4 * x + 2 * y + c
    n = len(src_refs)

    def src(k, p):
        return src_refs[k] if mode == "gather" else src_refs[k].at[p]

    local = [pltpu.make_async_copy(src(k, me), land_refs[k].at[me], local_sems.at[k]) for k in range(n)]
    sends, recvs = [], []
    for k in range(n):
        for r in range(1, N_DEV):
            px = (1 - x) if r & 4 else x
            py = (1 - y) if r & 2 else y
            pc = (1 - c) if r & 1 else c
            p = 4 * px + 2 * py + pc
            sem = k * (N_DEV - 1) + r - 1
            sends.append(pltpu.make_async_remote_copy(
                src_ref=src(k, p), dst_ref=land_refs[k].at[me],
                send_sem=send_sems.at[sem], recv_sem=recv_sems.at[sem],
                device_id=(px, py, pc), device_id_type=pl.DeviceIdType.MESH))
            recvs.append(pltpu.make_async_remote_copy(
                src_ref=src(k, p), dst_ref=land_refs[k].at[p],
                send_sem=send_sems.at[sem], recv_sem=recv_sems.at[sem],
                device_id=(px, py, pc), device_id_type=pl.DeviceIdType.MESH))
    return local, sends, recvs


def exchange_start(mode, arrays, name, carry=()):
    n, nc = len(arrays), len(carry)
    land_shapes = [((N_DEV,) + tuple(a.shape)) if mode == "gather" else tuple(a.shape) for a in arrays]
    lands = [pltpu.with_memory_space_constraint(lax.empty(shp, a.dtype), pltpu.HBM)
             for shp, a in zip(land_shapes, arrays)]
    srcs = [pltpu.with_memory_space_constraint(a, pltpu.HBM) for a in arrays]
    carried = [pltpu.with_memory_space_constraint(a, pltpu.HBM) for a in carry]

    def body(*refs):
        src_refs, land_refs = refs[:n], refs[n:2 * n]
        first_out = 2 * n + nc
        send_sems, recv_sems, local_sems = refs[first_out:first_out + 3]
        token = refs[-1]
        local, sends, _ = _exchange_copies(mode, src_refs, land_refs, send_sems, recv_sems, local_sems)
        for cp in local + sends:
            cp.start()
        token[...] = jnp.zeros_like(token)

    n_sem = n * (N_DEV - 1)
    out = pl.pallas_call(
        body, name=name,
        out_shape=(pltpu.SemaphoreType.DMA((n_sem,)), pltpu.SemaphoreType.DMA((n_sem,)),
                   pltpu.SemaphoreType.DMA((n,)),
                   *[pltpu.HBM(a.shape, a.dtype) for a in arrays],
                   *[pltpu.HBM(shp, a.dtype) for shp, a in zip(land_shapes, arrays)],
                   *[pltpu.HBM(a.shape, a.dtype) for a in carry],
                   jax.ShapeDtypeStruct((8, LANES), F32)),
        in_specs=[_HBM] * (2 * n + nc),
        out_specs=(_SEM, _SEM, _SEM, *[_HBM] * (2 * n + nc), pl.BlockSpec(memory_space=pltpu.VMEM)),
        input_output_aliases={i: 3 + i for i in range(2 * n + nc)},
        compiler_params=pltpu.CompilerParams(has_side_effects=_EFFECT),
    )(*srcs, *lands, *carried)
    handle = dict(mode=mode, sems=out[:3], srcs=out[3:3 + n], lands=out[3 + n:3 + 2 * n])
    return handle, out[-1], list(out[3 + 2 * n:3 + 2 * n + nc])


def exchange_wait(handle, after, name):
    mode, srcs, lands = handle["mode"], list(handle["srcs"]), list(handle["lands"])
    n = len(srcs)

    def body(*refs):
        src_refs, land_refs = refs[:n], refs[n:2 * n]
        send_sems, recv_sems, local_sems = refs[2 * n:2 * n + 3]
        local, sends, recvs = _exchange_copies(mode, src_refs, land_refs, send_sems, recv_sems, local_sems)
        for cp in sends:
            cp.wait_send()
        for cp in recvs:
            cp.wait_recv()
        for cp in local:
            cp.wait()

    out = pl.pallas_call(
        body, name=name,
        out_shape=(*[pltpu.HBM(a.shape, a.dtype) for a in srcs], *[pltpu.HBM(a.shape, a.dtype) for a in lands]),
        in_specs=[_HBM] * (2 * n) + [_SEM] * 3 + [pl.BlockSpec(memory_space=pl.ANY)],
        out_specs=tuple([_HBM] * (2 * n)),
        input_output_aliases={i: i for i in range(2 * n)},
        compiler_params=pltpu.CompilerParams(has_side_effects=_EFFECT),
    )(*srcs, *lands, *handle["sems"], after)
    return list(out[n:])


_ICI_RELATIONS = (2, 4, 6)


def _mesh_place():
    x, y, c = lax.axis_index("x"), lax.axis_index("y"), lax.axis_index("c")

    def peer(r):
        px = (1 - x) if r & 4 else x
        py = (1 - y) if r & 2 else y
        pc = (1 - c) if r & 1 else c
        return (px, py, pc), 4 * px + 2 * py + pc

    return 4 * x + 2 * y + c, peer


def _remote(src, dst, send_sem, recv_sem, device):
    return pltpu.make_async_remote_copy(src_ref=src, dst_ref=dst, send_sem=send_sem, recv_sem=recv_sem,
                                        device_id=device, device_id_type=pl.DeviceIdType.MESH)


def gather2_start(groups, name):
    flat = [a for g in groups for a in g]
    n = len(flat)
    lands = [pltpu.with_memory_space_constraint(lax.empty((N_DEV,) + tuple(a.shape), a.dtype), pltpu.HBM) for a in flat]
    srcs = [pltpu.with_memory_space_constraint(a, pltpu.HBM) for a in flat]
    n_rel = 1 + len(_ICI_RELATIONS)

    def body(*refs):
        src_refs, land_refs = refs[:n], refs[n:2 * n]
        sem_refs = refs[2 * n:2 * n + 4 * len(groups)]
        me, peer = _mesh_place()
        k = 0
        for gi, g in enumerate(groups):
            send_sems, recv_sib, recv_ici, local_sems = sem_refs[4 * gi:4 * gi + 4]
            for j in range(len(g)):
                pltpu.make_async_copy(src_refs[k], land_refs[k].at[me], local_sems.at[j]).start()
                dev, _ = peer(1)
                _remote(src_refs[k], land_refs[k].at[me], send_sems.at[n_rel * j], recv_sib.at[j], dev).start()
                for t, r in enumerate(_ICI_RELATIONS):
                    dev, _ = peer(r)
                    _remote(src_refs[k], land_refs[k].at[me], send_sems.at[n_rel * j + 1 + t],
                            recv_ici.at[len(_ICI_RELATIONS) * j + t], dev).start()
                k += 1
        refs[-1][...] = jnp.zeros_like(refs[-1])

    sem_shapes = []
    for g in groups:
        sem_shapes += [pltpu.SemaphoreType.DMA((n_rel * len(g),)), pltpu.SemaphoreType.DMA((len(g),)),
                       pltpu.SemaphoreType.DMA((len(_ICI_RELATIONS) * len(g),)), pltpu.SemaphoreType.DMA((len(g),))]
    out = pl.pallas_call(
        body, name=name,
        out_shape=(*sem_shapes, *[pltpu.HBM(a.shape, a.dtype) for a in flat],
                   *[pltpu.HBM((N_DEV,) + tuple(a.shape), a.dtype) for a in flat],
                   jax.ShapeDtypeStruct((8, LANES), F32)),
        in_specs=[_HBM] * (2 * n),
        out_specs=(*[_SEM] * len(sem_shapes), *[_HBM] * (2 * n), pl.BlockSpec(memory_space=pltpu.VMEM)),
        input_output_aliases={i: len(sem_shapes) + i for i in range(2 * n)},
        compiler_params=pltpu.CompilerParams(has_side_effects=_EFFECT),
    )(*srcs, *lands)
    handles, k, base = [], 0, len(sem_shapes)
    for gi, g in enumerate(groups):
        handles.append(dict(sems=out[4 * gi:4 * gi + 4], srcs=out[base + k:base + k + len(g)],
                            lands=out[base + n + k:base + n + k + len(g)]))
        k += len(g)
    return handles, out[-1]


def gather2_forward(handle, after, name, carry=()):
    lands, nc = list(handle["lands"]), len(carry)
    n, n_ici = len(lands), len(_ICI_RELATIONS)
    carried = [pltpu.with_memory_space_constraint(a, pltpu.HBM) for a in carry]

    def body(*refs):
        land_refs = refs[:n]
        recv_ici = refs[n + nc]
        fwd_send, fwd_recv = refs[n + nc + 2], refs[n + nc + 3]
        me, peer = _mesh_place()
        sibling, _ = peer(1)
        for j in range(n):
            for t, r in enumerate(_ICI_RELATIONS):
                dev, p = peer(r)
                landed = land_refs[j].at[p]
                _remote(landed, landed, fwd_send.at[n_ici * j + t], recv_ici.at[n_ici * j + t], dev).wait_recv()
                _remote(landed, landed, fwd_send.at[n_ici * j + t], fwd_recv.at[n_ici * j + t], sibling).start()

    out = pl.pallas_call(
        body, name=name,
        out_shape=(pltpu.SemaphoreType.DMA((n_ici * n,)), pltpu.SemaphoreType.DMA((n_ici * n,)),
                   *[pltpu.HBM(a.shape, a.dtype) for a in lands], *[pltpu.HBM(a.shape, a.dtype) for a in carry]),
        in_specs=[_HBM] * (n + nc) + [_SEM, pl.BlockSpec(memory_space=pl.ANY)],
        out_specs=(_SEM, _SEM, *[_HBM] * (n + nc)),
        input_output_aliases={i: 2 + i for i in range(n + nc)},
        compiler_params=pltpu.CompilerParams(has_side_effects=_EFFECT),
    )(*lands, *carried, handle["sems"][2], after)
    new_handle = dict(sems=handle["sems"], srcs=handle["srcs"], lands=out[2:2 + n], fwd=out[:2])
    return new_handle, list(out[2 + n:])


def gather2_wait(handle, after, name):
    srcs, lands = list(handle["srcs"]), list(handle["lands"])
    n, n_ici = len(srcs), len(_ICI_RELATIONS)
    n_rel = 1 + n_ici
    send_all, recv_sibling, _, local_all = handle["sems"]

    def body(*refs):
        src_refs, land_refs = refs[:n], refs[n:2 * n]
        send_sems, recv_sib, local_sems, fwd_send, fwd_recv = refs[2 * n:2 * n + 5]
        me, peer = _mesh_place()
        sibling, sib = peer(1)
        for j in range(n):
            pltpu.make_async_copy(src_refs[j], land_refs[j].at[me], local_sems.at[j]).wait()
            _remote(src_refs[j], land_refs[j].at[sib], send_sems.at[n_rel * j], recv_sib.at[j], sibling).wait()
            for t, r in enumerate(_ICI_RELATIONS):
                dev, p = peer(r)
                _remote(src_refs[j], land_refs[j].at[me], send_sems.at[n_rel * j + 1 + t],
                        recv_sib.at[j], dev).wait_send()
                _, p_sib = peer(r ^ 1)
                _remote(land_refs[j].at[p], land_refs[j].at[p_sib], fwd_send.at[n_ici * j + t],
                        fwd_recv.at[n_ici * j + t], sibling).wait()

    out = pl.pallas_call(
        body, name=name,
        out_shape=(*[pltpu.HBM(a.shape, a.dtype) for a in srcs], *[pltpu.HBM(a.shape, a.dtype) for a in lands]),
        in_specs=[_HBM] * (2 * n) + [_SEM] * 5 + [pl.BlockSpec(memory_space=pl.ANY)],
        out_specs=tuple([_HBM] * (2 * n)),
        input_output_aliases={i: i for i in range(2 * n)},
        compiler_params=pltpu.CompilerParams(has_side_effects=_EFFECT),
    )(*srcs, *lands, send_all, recv_sibling, local_all, *handle["fwd"], after)
    return list(out[n:])


BIG = ["mla_w_in", "mla_w_uq", "mla_w_ukv", "mla_w_o", "gdn_w_in", "gdn_w_o", "sc_w_in", "sc_w_o",
       "xa_w_q", "xa_w_kv", "xa_w_o", "mlp_w1", "mlp_w2"]
TINY = [("mla_q_norm", 1), ("mla_kv_norm", 1), ("gdn_conv_w", 2), ("sc_conv_w", 2)]
REPL = ["gdn_a_log", "gdn_dt_bias", "gdn_o_norm", "norm_mix", "norm_mem", "norm_mlp", "mem_norm", "final_norm"]
WEIGHTS = ["mla_w_in", "mla_q_norm", "mla_kv_norm", "mla_w_uq", "mla_w_ukv", "mla_w_o", "gdn_w_in",
           "gdn_conv_w", "gdn_a_log", "gdn_dt_bias", "gdn_o_norm", "gdn_w_o", "sc_w_in", "sc_conv_w",
           "sc_w_o", "norm_mix", "norm_mem", "norm_mlp", "xa_w_q", "xa_w_kv", "xa_w_o", "mlp_w1",
           "mlp_w2", "mem_norm", "final_norm"]
MIXER_WEIGHTS = (["mla_w_in", "mla_w_uq", "mla_w_ukv", "mla_w_o"], ["gdn_w_in", "gdn_w_o"], ["sc_w_in", "sc_w_o"])
MIXER_PARAMS = (["norm_mem", "mla_q_norm", "mla_kv_norm"],
                ["norm_mem", "gdn_conv_w", "gdn_a_log", "gdn_dt_bias", "gdn_o_norm"],
                ["norm_mem", "sc_conv_w"])


def from_shards(a8, axis):
    a = jnp.moveaxis(a8, 0, axis)
    shp = a.shape
    return a.reshape(shp[:axis] + (shp[axis] * shp[axis + 1],) + shp[axis + 2:])


def pack_rows(flat_list, width, row_mult):
    total = sum(a.shape[-1] for a in flat_list)
    rows = -(-total // width)
    rows = -(-rows // row_mult) * row_mult
    pad = rows * width - total
    parts = list(flat_list)
    if pad:
        parts.append(jnp.zeros((pad,), flat_list[0].dtype))
    return jnp.concatenate(parts, axis=-1).reshape(rows, width)


def unpack_rows(packed, shapes):
    lead = packed.shape[:-2]
    flat = packed.reshape(lead + (-1,))
    out, off = [], 0
    for shp in shapes:
        n = math.prod(shp)
        out.append(flat[..., off:off + n].reshape(lead + tuple(shp)))
        off += n
    return out


def _swap_halves(w):
    half = w.shape[-1] // 2
    return jnp.concatenate([w[..., half:], w[..., :half]], axis=-1)


def _pad_last(w, n):
    return jnp.pad(w, [(0, 0)] * (w.ndim - 1) + [(0, n - w.shape[-1])])


def _unblock(w8):
    return jnp.transpose(w8, (1, 0, 2)).reshape(w8.shape[1], -1)


def _stack_rows(w8):
    return w8.reshape(-1, w8.shape[-1])


def rms_op(name, rows, d, out_dtype):
    tm = rows if rows * d * 4 <= BLOCK_BYTES else 512
    return make_tile_op(fn_rms, name, ["row", "par"], [True, True], [("row", d, out_dtype)], rows, min(tm, rows))


def seg_memory(p, mem):
    return rms_op("rms_memory", mem.shape[0], mem.shape[1], BF16)(mem, p["mem_norm"].reshape(1, -1))[0]


def seg_mixer(i, wts, p, x, h, rope_c, rope_s):
    s, d = x.shape
    j, kind = i // N_MIXERS, i % N_MIXERS
    tag = f"l{i}"
    hd = MLA_NOPE
    next_gain = p["norm_mem"][i].reshape(1, d)
    if kind == 0:
        w_in = _stack_rows(wts["mla_w_in"])
        w_cq = w_in[:, :MLA_Q_RANK]
        w_ckv = w_in[:, MLA_Q_RANK:MLA_Q_RANK + MLA_KV_RANK]
        w_kr = w_in[:, MLA_Q_RANK + MLA_KV_RANK:]
        c_qn, c_kvn, k_rope = make_tile_op(
            fn_mla_down, tag + "_mla_down", ["row", "row", "row"] + ["par"] * 6, [True, False, False] + [True] * 6,
            [("row", MLA_Q_RANK, BF16), ("row", MLA_KV_RANK, BF16), ("row", hd, F32)], s, 512)(
            h, rope_c, rope_s, w_cq, w_ckv, _pad_last(w_kr, hd), _pad_last(_swap_halves(w_kr), hd),
            p["mla_q_norm"][j].reshape(1, -1), p["mla_kv_norm"][j].reshape(1, -1))
        w_uq8 = wts["mla_w_uq"]
        w_qn = _unblock(w_uq8[:, :, :MLA_NOPE])
        w_qr = w_uq8[:, :, MLA_NOPE:]
        w_qr_p = _unblock(_pad_last(w_qr, hd))
        w_qr_s = _unblock(_pad_last(_swap_halves(w_qr), hd))
        nq = MLA_HEADS * hd
        q_nope, q_rope, kv = make_tile_op(
            fn_mla_up, tag + "_mla_up", ["row", "row", "row", "row"] + ["par"] * 4, [True, True, False, False] + [True] * 4,
            [("row", nq, BF16), ("row", nq, F32), ("row", 2 * nq, BF16)], s, 512)(
            c_qn, c_kvn, rope_c, rope_s, w_qn, w_qr_p, w_qr_s, _unblock(wts["mla_w_ukv"]))
        n_groups = MLA_QUERY_GROUPS if s % (MLA_QUERY_GROUPS * 256) == 0 else 1
        rows_g = s // n_groups
        o_groups = []
        for grp in range(n_groups):
            r0, r1 = grp * rows_g, (grp + 1) * rows_g
            o_groups.append(make_tile_op(
                fn_mla_attn, f"{tag}_mla_attn{grp}", [("rowh", hd), ("rowh", hd), ("parh", hd, 2), "par"],
                [True] * 4, [(("rowh", hd), nq, BF16)], rows_g, 256, MLA_HEADS, row_base=r0)(
                q_nope[r0:r1], q_rope[r0:r1], kv[:r1], k_rope[:r1])[0])
        o = jnp.concatenate(o_groups, axis=0)
        return make_mm_res_rms(tag + "_mla_o")(x, o, _stack_rows(wts["mla_w_o"]), next_gain)
    if kind == 1:
        ng = GDN_HEADS * GDN_DK
        w_in = _unblock(wts["gdn_w_in"])
        qkv_pre = make_mm(tag + "_gdn_in_qkv", F32)(h, w_in[:, :3 * ng])
        qkv_conv = make_conv(tag + "_gdn_conv")(qkv_pre, p["gdn_conv_w"][j])
        gate = make_mm(tag + "_gdn_in_g", F32)(h, w_in[:, 3 * ng:4 * ng])
        ba = make_mm(tag + "_gdn_in_ba", F32)(h, _pad_last(w_in[:, 4 * ng:], LANES))
        heads_row = ("row", GDN_DK, GDN_HEADS)
        q, k, v = make_tile_op(fn_gdn_prep, tag + "_gdn_prep", [("row", GDN_DK, 3 * GDN_HEADS)], [True],
                               [(heads_row, ng, F32)] * 3, s, 256)(qkv_conv)
        alog = jnp.pad(p["gdn_a_log"][j].reshape(1, -1), ((0, 0), (GDN_HEADS, LANES - 2 * GDN_HEADS)))
        dtb = jnp.pad(p["gdn_dt_bias"][j].reshape(1, -1), ((0, 0), (GDN_HEADS, LANES - 2 * GDN_HEADS)))
        beta_b, g_b = make_tile_op(fn_gdn_gates, tag + "_gdn_gates", ["row", "par", "par"], [True] * 3,
                                   [("row", ng, F32)] * 2, s, 512)(ba, alog, dtb)
        o = make_gdn(tag + "_gdn_core")(q, k, v, g_b, beta_b)
        o = make_tile_op(fn_gdn_out, tag + "_gdn_out", [heads_row, heads_row, "par"],
                         [True] * 3, [(heads_row, ng, BF16)], s, 512)(
            o, gate, p["gdn_o_norm"][j].reshape(1, -1))[0]
        return make_mm_res_rms(tag + "_gdn_o")(x, o, _stack_rows(wts["gdn_w_o"]), next_gain)
    w_in = _unblock(wts["sc_w_in"])
    b_gate = make_mm(tag + "_sc_in_b", F32)(h, w_in[:, :d])
    c_gate = make_mm(tag + "_sc_in_c", F32)(h, w_in[:, d:2 * d])
    u = make_mm(tag + "_sc_in_u", F32)(h, w_in[:, 2 * d:])
    yv = make_gated_conv(tag + "_sc_conv")(b_gate, c_gate, u, p["sc_conv_w"][j])
    return make_mm_res_rms(tag + "_sc_o")(x, yv, _stack_rows(wts["sc_w_o"]), next_gain)


def seg_xattn(i, wts, p, x, hx, mem_n):
    s, d = x.shape
    tag = f"l{i}"
    kv = make_mm(tag + "_xa_kv", BF16, blocked=True)(mem_n, wts["xa_w_kv"])
    heads = ("row", X_HEAD_DIM, X_HEADS)
    o = make_tile_op(fn_xattn, tag + "_xattn",
                     ["row", ("par", X_HEAD_DIM, X_HEADS), ("par", X_HEAD_DIM, 2 * X_HEADS)], [True] * 3,
                     [(heads, d, BF16)], s, 1024)(hx, _stack_rows(wts["xa_w_q"]), kv)[0]
    return make_mm_res_rms(tag + "_xa_o")(x, o, _stack_rows(wts["xa_w_o"]), p["norm_mlp"][i].reshape(1, d))


def seg_mlp(i, wts, p, x, hm):
    d = x.shape[1]
    gain = p["norm_mix"][i + 1].reshape(1, d) if i + 1 < DEPTH else None
    return make_mlp(f"l{i}_mlp", gain is not None)(x, hm, wts["mlp_w1"], _stack_rows(wts["mlp_w2"]), gain)


def segments():
    segs = []
    for i in range(DEPTH):
        j, kind = i // N_MIXERS, i % N_MIXERS
        segs.append((f"l{i}_mixer", [(n, j) for n in MIXER_WEIGHTS[kind]], MIXER_PARAMS[kind], "mixer"))
        segs.append((f"l{i}_xattn", [(n, i) for n in ("xa_w_q", "xa_w_kv", "xa_w_o")], ["norm_mlp"], "xattn"))
        segs.append((f"l{i}_mlp", [(n, i) for n in ("mlp_w1", "mlp_w2")], ["norm_mix"] if i + 1 < DEPTH else [],
                     "mlp"))
    return segs


def run_segment(index, kind, wts, p, x, h, mem_n, rope_c, rope_s):
    layer = index // 3
    if kind == "mixer":
        return seg_mixer(layer, wts, p, x, h, rope_c, rope_s)
    if kind == "xattn":
        return seg_xattn(layer, wts, p, x, h, mem_n)
    return seg_mlp(layer, wts, p, x, h)


def rope_tables(positions):
    inv_freq = ROPE_THETA ** (-jnp.arange(0, MLA_ROPE, 2, dtype=F32) / MLA_ROPE)
    ang = positions.astype(F32)[:, None] * inv_freq
    cos, sin = jnp.cos(ang), jnp.sin(ang)
    zeros = jnp.zeros((positions.shape[0], MLA_NOPE - MLA_ROPE), F32)
    return jnp.concatenate([cos, cos, zeros], axis=-1), jnp.concatenate([-sin, sin, zeros], axis=-1)


def kernel(x, mem, positions, mla_w_in, mla_q_norm, mla_kv_norm, mla_w_uq, mla_w_ukv, mla_w_o, gdn_w_in, gdn_conv_w, gdn_a_log, gdn_dt_bias, gdn_o_norm, gdn_w_o, sc_w_in, sc_conv_w, sc_w_o, norm_mix, norm_mem, norm_mlp, xa_w_q, xa_w_kv, xa_w_o, mlp_w1, mlp_w2, mem_norm, final_norm, loss_target, m_mla_w_in, m_mla_q_norm, m_mla_kv_norm, m_mla_w_uq, m_mla_w_ukv, m_mla_w_o, m_gdn_w_in, m_gdn_conv_w, m_gdn_a_log, m_gdn_dt_bias, m_gdn_o_norm, m_gdn_w_o, m_sc_w_in, m_sc_conv_w, m_sc_w_o, m_norm_mix, m_norm_mem, m_norm_mlp, m_xa_w_q, m_xa_w_kv, m_xa_w_o, m_mlp_w1, m_mlp_w2, m_mem_norm, m_final_norm, v_mla_w_in, v_mla_q_norm, v_mla_kv_norm, v_mla_w_uq, v_mla_w_ukv, v_mla_w_o, v_gdn_w_in, v_gdn_conv_w, v_gdn_a_log, v_gdn_dt_bias, v_gdn_o_norm, v_gdn_w_o, v_sc_w_in, v_sc_conv_w, v_sc_w_o, v_norm_mix, v_norm_mem, v_norm_mlp, v_xa_w_q, v_xa_w_kv, v_xa_w_o, v_mlp_w1, v_mlp_w2, v_mem_norm, v_final_norm):
    args = locals()
    w_loc = {n: args[n] for n in WEIGHTS}
    m_loc = {n: args["m_" + n] for n in WEIGHTS}
    v_loc = {n: args["v_" + n] for n in WEIGHTS}
    me = 4 * lax.axis_index("x") + 2 * lax.axis_index("y") + lax.axis_index("c")
    segs = segments()

    w16 = {n: w_loc[n].astype(BF16) for n in BIG}
    tiny_pack = pack_rows([w_loc[n].reshape(-1) for n, _ in TINY], LANES, 8)
    gather_handles, token = gather2_start(
        [[tiny_pack]] + [[w16[n][layer] for n, layer in units] for _, units, _, _ in segs], "gather_start")

    x_cur = x[0]
    rope_c, rope_s = rope_tables(positions[0])
    tiny_handle, _ = gather2_forward(gather_handles[0], token, "gather_forward_tiny")
    tiny_all = gather2_wait(tiny_handle, token, "gather_wait_tiny")[0]
    gather_handles = gather_handles[1:]
    params = {}
    for (n, ax), a8 in zip(TINY, unpack_rows(tiny_all, [w_loc[n].shape for n, _ in TINY])):
        params[n] = from_shards(a8, ax)
    for n in REPL:
        params[n] = w_loc[n]

    mem_n, vjp_memory = jax.vjp(lambda p_: seg_memory(p_, mem[0]), {"mem_norm": params["mem_norm"]})
    h_cur, vjp_first_norm = jax.vjp(
        lambda p_, x_: rms_op("l0_rms_mix", x_.shape[0], x_.shape[1], BF16)(x_, p_["norm_mix"][0].reshape(1, -1))[0],
        {"norm_mix": params["norm_mix"]}, x_cur)
    vjps = []
    forwarded, _ = gather2_forward(gather_handles[0], token, f"gather_forward_{segs[0][0]}")
    for index, (tag, units, p_names, kind) in enumerate(segs):
        landed = gather2_wait(forwarded, token if index == 0 else x_cur, f"gather_wait_{tag}")
        wts = {n: a for (n, _), a in zip(units, landed)}
        p_seg = {n: params[n] for n in p_names}
        if index + 1 < len(segs):
            forwarded, (p_seg[p_names[0]],) = gather2_forward(
                gather_handles[index + 1], landed[0], f"gather_forward_{segs[index + 1][0]}",
                carry=[p_seg[p_names[0]]])
        outs, vjp_seg = jax.vjp(
            lambda w_, p_, x_, h_, m_, index=index, kind=kind:
            run_segment(index, kind, w_, p_, x_, h_, m_, rope_c, rope_s),
            wts, p_seg, x_cur, h_cur, mem_n)
        x_cur, h_cur = outs[0], (outs[1] if len(outs) > 1 else None)
        vjps.append(vjp_seg)

    loss_vec, g_x, d_final = loss_head(x_cur, params["final_norm"].reshape(1, -1), loss_target[0], "loss_head")

    grads = {n: jnp.zeros_like(params[n]) for n in params}
    grads["final_norm"] = d_final.reshape(-1)
    g_mem_n = jnp.zeros_like(mem_n)
    g_h = None
    scatter_handles = []
    for (tag, units, _, _), vjp_seg in zip(reversed(segs), reversed(vjps)):
        g_wts, g_p, g_x, g_h, g_m = vjp_seg((g_x,) if g_h is None else (g_x, g_h))
        for n, g in g_p.items():
            grads[n] = grads[n] + g
        g_mem_n = g_mem_n + g_m
        handle, _, (g_h,) = exchange_start("scatter", [g_wts[n] for n, _ in units], f"scatter_start_{tag}",
                                           carry=[g_h])
        scatter_handles.append((units, handle))
    grads["mem_norm"] = grads["mem_norm"] + vjp_memory(g_mem_n)[0]["mem_norm"]
    g_first, g_x_norm = vjp_first_norm(g_h)
    grads["norm_mix"] = grads["norm_mix"] + g_first["norm_mix"]
    g_x = g_x + g_x_norm

    small_names = [n for n, _ in TINY] + REPL
    small_g = pack_rows([loss_vec[0, :1]] + [grads[n].astype(F32).reshape(-1) for n in small_names], PACK_W, 8)
    small_handle, _, _ = exchange_start("gather", [small_g], "gather_start_small_grads")

    g_recv = {}
    for units, handle in scatter_handles:
        landed = exchange_wait(handle, g_x, f"scatter_wait_{units[0][0]}_{units[0][1]}")
        g_recv.update(dict(zip(units, landed)))

    res = {}
    for n in BIG:
        outs = None
        for layer in range(w_loc[n].shape[0]):
            outs = adamw(g_recv[n, layer], w_loc[n], m_loc[n], v_loc[n], layer, outs, f"adamw_{n}_{layer}")
        for kind, a in zip(("grad", "delta", "m", "v"), outs):
            res[(kind, n)] = a
    small_recv = exchange_wait(small_handle, res[("grad", BIG[-1])], "gather_wait_small_grads")[0]

    def full_small(d):
        parts = [jnp.zeros((1,), F32)]
        for n, ax in TINY:
            full_shape = params[n].shape
            start = [0] * len(full_shape)
            start[ax] = me * d[n].shape[ax]
            parts.append(lax.dynamic_update_slice(jnp.zeros(full_shape, F32), d[n], start).reshape(-1))
        parts += [d[n].reshape(-1) for n in REPL]
        return pack_rows(parts, PACK_W, 8)

    outs_small = adamw(small_recv, full_small(w_loc)[None], full_small(m_loc)[None], full_small(v_loc)[None],
                       0, None, "adamw_small")
    small_shapes = [(1,)] + [params[n].shape for n, _ in TINY] + [w_loc[n].shape for n in REPL]
    loss = None
    for kind, packed in zip(("grad", "delta", "m", "v"), outs_small):
        parts = unpack_rows(packed[0], small_shapes)
        if kind == "grad":
            loss = parts[0][0]
        for (n, ax), a in zip(TINY, parts[1:1 + len(TINY)]):
            start = [0] * a.ndim
            start[ax] = me * w_loc[n].shape[ax]
            res[(kind, n)] = lax.dynamic_slice(a, start, w_loc[n].shape)
        for n, a in zip(REPL, parts[1 + len(TINY):]):
            res[(kind, n)] = a

    out = [loss, g_x[None]]
    for kind in ("grad", "delta", "m", "v"):
        out += [res[(kind, n)] for n in WEIGHTS]
    return tuple(out)
```
